```python
import jax, jax.numpy as jnp
from jax import lax
import numpy as np

D_MODEL = 1024
BATCH = 8
SEQ = 4096
DEPTH = 4

A_WIDTH = D_MODEL
A_GROUPS = 8
A_GROUP_DIM = A_WIDTH // A_GROUPS
A_CHUNK = 128
B_HEADS = 8
B_DK = 128
B_DV = 128
B_CONV = 4
B_CHUNK = 64
FFN_HIDDEN = -(-8 * D_MODEL // (3 * 256)) * 256
IN_UV = 2 * A_WIDTH
QK_WIDTH = B_HEADS * B_DK
V_WIDTH = B_HEADS * B_DV
IN_QKV = 2 * QK_WIDTH + V_WIDTH
IN_GATES = 2 * D_MODEL
IN_TOTAL = IN_UV + IN_QKV + V_WIDTH + 2 * B_HEADS + IN_GATES
NORM_EPS = 1e-6

kernel_name = "hybrid_gmlp_gdn_adaln_block"


def rms_norm(x, w):
    xf = x.astype(jnp.float32)
    y = xf * lax.rsqrt(jnp.mean(xf * xf, axis=-1, keepdims=True) + NORM_EPS)
    return (y * w.astype(jnp.float32)).astype(x.dtype)


def l2_norm(x):
    return x * lax.rsqrt(jnp.sum(x * x, axis=-1, keepdims=True) + NORM_EPS)


def causal_dwconv(x, w):
    K, C = w.shape
    return lax.conv_general_dilated(
        x, w[:, None, :], window_strides=(1,), padding=[(K - 1, 0)],
        dimension_numbers=("NWC", "WIO", "NWC"), feature_group_count=C)


def chunked_spatial_gating(uv, w_s, b_s, g_v):
    Bn, T, _ = uv.shape
    u, v = jnp.split(uv, 2, axis=-1)
    v = v.reshape(Bn, T // A_CHUNK, A_CHUNK, A_GROUPS, A_GROUP_DIM)
    v = rms_norm(v, g_v.reshape(A_GROUPS, A_GROUP_DIM))
    causal = jnp.tril(jnp.ones((A_CHUNK, A_CHUNK), dtype=bool))
    w = jnp.where(causal, w_s, 0)
    s = jnp.einsum("gts,bnsgd->bntgd", w, v) + jnp.swapaxes(b_s, 0, 1)[None, None, :, :, None]
    return u * s.reshape(Bn, T, A_WIDTH)


def gated_delta_rule(q, k, v, g, beta):
    Bn, T, H, DK = q.shape
    DV = v.shape[-1]
    C = B_CHUNK
    N = T // C
    q = q * DK ** -0.5

    def chunk4(t):
        return t.reshape(Bn, N, C, H, t.shape[-1]).transpose(0, 3, 1, 2, 4)

    def chunk3(t):
        return t.reshape(Bn, N, C, H).transpose(0, 3, 1, 2)

    qc, kc, vc = chunk4(q), chunk4(k), chunk4(v)
    gc, bc = chunk3(g), chunk3(beta)
    gam = jnp.cumsum(gc, axis=-1)
    causal = jnp.tril(jnp.ones((C, C), dtype=bool))
    strict = jnp.tril(jnp.ones((C, C), dtype=bool), k=-1)
    decay_mat = jnp.exp(jnp.where(causal, gam[..., :, None] - gam[..., None, :], -jnp.inf))
    k_beta = kc * bc[..., None]
    kk = jnp.einsum("bhnid,bhnjd->bhnij", k_beta, kc)
    a_mat = jnp.where(strict, kk * decay_mat, 0.0)
    eye = jnp.broadcast_to(jnp.eye(C, dtype=a_mat.dtype), a_mat.shape)
    t_mat = lax.linalg.triangular_solve(a_mat, eye, left_side=True, lower=True, unit_diagonal=True)
    u_val = t_mat @ (vc * bc[..., None])
    w_key = t_mat @ (k_beta * jnp.exp(gam)[..., None])
    qk = jnp.einsum("bhnid,bhnjd->bhnij", qc, kc) * decay_mat
    q_dec = qc * jnp.exp(gam)[..., None]
    k_dec = kc * jnp.exp(gam[..., -1:] - gam)[..., None]
    g_last = jnp.exp(gam[..., -1])

    def step(S, xs):
        qd, kd, uv_, wk, a, gl = xs
        v_new = uv_ - wk @ S
        o = qd @ S + a @ v_new
        S = S * gl[..., None, None] + jnp.swapaxes(kd, -1, -2) @ v_new
        return S, o

    xs = tuple(jnp.moveaxis(t, 2, 0) for t in (q_dec, k_dec, u_val, w_key, qk, g_last))
    S0 = jnp.zeros((Bn, H, DK, DV), dtype=q.dtype)
    _, o = lax.scan(step, S0, xs)
    return o.transpose(1, 0, 3, 2, 4).reshape(Bn, T, H, DV)


def gated_deltanet(qkv, z, b_raw, a_raw, conv_w, a_log, dt_bias, g_o):
    Bn, T, _ = qkv.shape
    dt = qkv.dtype
    qkv = jax.nn.silu(causal_dwconv(qkv, conv_w)).astype(jnp.float32)
    q, k, v = jnp.split(qkv, [QK_WIDTH, 2 * QK_WIDTH], axis=-1)
    q = l2_norm(q.reshape(Bn, T, B_HEADS, B_DK))
    k = l2_norm(k.reshape(Bn, T, B_HEADS, B_DK))
    v = v.reshape(Bn, T, B_HEADS, B_DV)
    beta = jax.nn.sigmoid(b_raw.astype(jnp.float32))
    g = -jnp.exp(a_log.astype(jnp.float32)) * jax.nn.softplus(
        a_raw.astype(jnp.float32) + dt_bias.astype(jnp.float32))
    o = gated_delta_rule(q, k, v, g, beta)
    o = rms_norm(o, g_o) * jax.nn.silu(z.reshape(Bn, T, B_HEADS, B_DV).astype(jnp.float32))
    return o.reshape(Bn, T, V_WIDTH).astype(dt)


def _fwd_setup_inputs(seed: int = 0) -> dict:
    key = jax.random.key(seed)
    ks = jax.random.split(key, 24)
    nrm = jax.random.normal
    L, D, F = DEPTH, D_MODEL, FFN_HIDDEN
    dt_min, dt_max = 1e-3, 1e-1
    dt0 = jnp.exp(jax.random.uniform(ks[10], (L, B_HEADS)) * (np.log(dt_max) - np.log(dt_min)) + np.log(dt_min))
    return {
        "x": nrm(ks[0], (BATCH, SEQ, D), jnp.float32),
        "c": nrm(ks[1], (BATCH, D), jnp.float32),
        "ada_w": nrm(ks[2], (L, D, 6 * D), jnp.float32) * (0.5 * D ** -0.5),
        "ada_b": nrm(ks[3], (L, 6 * D), jnp.float32) * 0.02,
        "norm1_g": 1.0 + 0.02 * nrm(ks[4], (L, D), jnp.float32),
        "w_in": nrm(ks[5], (L, D, IN_TOTAL), jnp.float32) * D ** -0.5,
        "conv_w": nrm(ks[6], (L, B_CONV, IN_QKV), jnp.float32) * B_CONV ** -0.5,
        "spatial_w": nrm(ks[7], (L, A_GROUPS, A_CHUNK, A_CHUNK), jnp.float32) * A_CHUNK ** -0.5,
        "spatial_b": 1.0 + 0.02 * nrm(ks[8], (L, A_GROUPS, A_CHUNK), jnp.float32),
        "v_norm_g": 1.0 + 0.02 * nrm(ks[9], (L, A_WIDTH), jnp.float32),
        "a_log": jnp.log(jax.random.uniform(ks[11], (L, B_HEADS), jnp.float32, 1.0, 16.0)),
        "dt_bias": (dt0 + jnp.log(-jnp.expm1(-dt0))).astype(jnp.float32),
        "o_norm_g": 1.0 + 0.02 * nrm(ks[12], (L, B_DV), jnp.float32),
        "w_branch_a": nrm(ks[13], (L, A_WIDTH, D), jnp.float32) * A_WIDTH ** -0.5,
        "w_branch_b": nrm(ks[14], (L, V_WIDTH, D), jnp.float32) * V_WIDTH ** -0.5,
        "w_out": nrm(ks[15], (L, D, D), jnp.float32) * D ** -0.5,
        "norm2_g": 1.0 + 0.02 * nrm(ks[16], (L, D), jnp.float32),
        "w_ffn_in": nrm(ks[17], (L, D, 2 * F), jnp.float32) * D ** -0.5,
        "w_ffn_out": nrm(ks[18], (L, F, D), jnp.float32) * F ** -0.5,
        "final_g": 1.0 + 0.02 * nrm(ks[19], (D,), jnp.float32),
    }


def _fwd_reference(x, c, ada_w, ada_b, norm1_g, w_in, conv_w, spatial_w, spatial_b, v_norm_g,
              a_log, dt_bias, o_norm_g, w_branch_a, w_branch_b, w_out, norm2_g,
              w_ffn_in, w_ffn_out, final_g):
    cond = jax.nn.silu(c)
    offs = [IN_UV, IN_UV + IN_QKV, IN_UV + IN_QKV + V_WIDTH,
            IN_UV + IN_QKV + V_WIDTH + B_HEADS, IN_UV + IN_QKV + V_WIDTH + 2 * B_HEADS]
    for i in range(DEPTH):
        mod = cond @ ada_w[i] + ada_b[i]
        sh1, sc1, gt1, sh2, sc2, gt2 = [m[:, None, :] for m in jnp.split(mod, 6, axis=-1)]
        h = rms_norm(x, norm1_g[i]) * (1 + sc1) + sh1
        proj = h @ w_in[i]
        uv, qkv, z, b_raw, a_raw, gates = jnp.split(proj, offs, axis=-1)
        y_a = chunked_spatial_gating(jax.nn.gelu(uv), spatial_w[i], spatial_b[i], v_norm_g[i])
        y_b = gated_deltanet(qkv, z, b_raw, a_raw, conv_w[i], a_log[i], dt_bias[i], o_norm_g[i])
        g_a, g_b = jnp.split(jax.nn.sigmoid(gates), 2, axis=-1)
        merged = g_a * (y_a @ w_branch_a[i]) + g_b * (y_b @ w_branch_b[i])
        x = x + gt1 * (merged @ w_out[i])
        h = rms_norm(x, norm2_g[i]) * (1 + sc2) + sh2
        gate, up = jnp.split(h @ w_ffn_in[i], 2, axis=-1)
        x = x + gt2 * ((jax.nn.silu(gate) * up) @ w_ffn_out[i])
    return rms_norm(x, final_g)


import jax as _jax
import jax.numpy as _jnp

TWIN_FORMAT = 'train_step'
FWD_PARAMS = ['x', 'c', 'ada_w', 'ada_b', 'norm1_g', 'w_in', 'conv_w', 'spatial_w', 'spatial_b', 'v_norm_g', 'a_log', 'dt_bias', 'o_norm_g', 'w_branch_a', 'w_branch_b', 'w_out', 'norm2_g', 'w_ffn_in', 'w_ffn_out', 'final_g']
TWIN_WEIGHTS = ['ada_w', 'ada_b', 'norm1_g', 'w_in', 'conv_w', 'spatial_w', 'spatial_b', 'v_norm_g', 'a_log', 'dt_bias', 'o_norm_g', 'w_branch_a', 'w_branch_b', 'w_out', 'norm2_g', 'w_ffn_in', 'w_ffn_out', 'final_g']
TWIN_DIFF_INPUT = 'x'
TWIN_INPUTS = ['x', 'c', 'ada_w', 'ada_b', 'norm1_g', 'w_in', 'conv_w', 'spatial_w', 'spatial_b', 'v_norm_g', 'a_log', 'dt_bias', 'o_norm_g', 'w_branch_a', 'w_branch_b', 'w_out', 'norm2_g', 'w_ffn_in', 'w_ffn_out', 'final_g', 'loss_target', 'm_ada_w', 'm_ada_b', 'm_norm1_g', 'm_w_in', 'm_conv_w', 'm_spatial_w', 'm_spatial_b', 'm_v_norm_g', 'm_a_log', 'm_dt_bias', 'm_o_norm_g', 'm_w_branch_a', 'm_w_branch_b', 'm_w_out', 'm_norm2_g', 'm_w_ffn_in', 'm_w_ffn_out', 'm_final_g', 'v_ada_w', 'v_ada_b', 'v_norm1_g', 'v_w_in', 'v_conv_w', 'v_spatial_w', 'v_spatial_b', 'v_v_norm_g', 'v_a_log', 'v_dt_bias', 'v_o_norm_g', 'v_w_branch_a', 'v_w_branch_b', 'v_w_out', 'v_norm2_g', 'v_w_ffn_in', 'v_w_ffn_out', 'v_final_g']
TWIN_OUTPUTS = ['loss', 'grad_x', 'grad_ada_w', 'grad_ada_b', 'grad_norm1_g', 'grad_w_in', 'grad_conv_w', 'grad_spatial_w', 'grad_spatial_b', 'grad_v_norm_g', 'grad_a_log', 'grad_dt_bias', 'grad_o_norm_g', 'grad_w_branch_a', 'grad_w_branch_b', 'grad_w_out', 'grad_norm2_g', 'grad_w_ffn_in', 'grad_w_ffn_out', 'grad_final_g', 'delta_ada_w', 'delta_ada_b', 'delta_norm1_g', 'delta_w_in', 'delta_conv_w', 'delta_spatial_w', 'delta_spatial_b', 'delta_v_norm_g', 'delta_a_log', 'delta_dt_bias', 'delta_o_norm_g', 'delta_w_branch_a', 'delta_w_branch_b', 'delta_w_out', 'delta_norm2_g', 'delta_w_ffn_in', 'delta_w_ffn_out', 'delta_final_g', 'new_m_ada_w', 'new_m_ada_b', 'new_m_norm1_g', 'new_m_w_in', 'new_m_conv_w', 'new_m_spatial_w', 'new_m_spatial_b', 'new_m_v_norm_g', 'new_m_a_log', 'new_m_dt_bias', 'new_m_o_norm_g', 'new_m_w_branch_a', 'new_m_w_branch_b', 'new_m_w_out', 'new_m_norm2_g', 'new_m_w_ffn_in', 'new_m_w_ffn_out', 'new_m_final_g', 'new_v_ada_w', 'new_v_ada_b', 'new_v_norm1_g', 'new_v_w_in', 'new_v_conv_w', 'new_v_spatial_w', 'new_v_spatial_b', 'new_v_v_norm_g', 'new_v_a_log', 'new_v_dt_bias', 'new_v_o_norm_g', 'new_v_w_branch_a', 'new_v_w_branch_b', 'new_v_w_out', 'new_v_norm2_g', 'new_v_w_ffn_in', 'new_v_w_ffn_out', 'new_v_final_g']
TWIN_LEAF_KINDS = {'loss': 'loss', 'grad_x': 'grad_x', 'grad_ada_w': 'grad_w', 'grad_ada_b': 'grad_w', 'grad_norm1_g': 'grad_w', 'grad_w_in': 'grad_w', 'grad_conv_w': 'grad_w', 'grad_spatial_w': 'grad_w', 'grad_spatial_b': 'grad_w', 'grad_v_norm_g': 'grad_w', 'grad_a_log': 'grad_w', 'grad_dt_bias': 'grad_w', 'grad_o_norm_g': 'grad_w', 'grad_w_branch_a': 'grad_w', 'grad_w_branch_b': 'grad_w', 'grad_w_out': 'grad_w', 'grad_norm2_g': 'grad_w', 'grad_w_ffn_in': 'grad_w', 'grad_w_ffn_out': 'grad_w', 'grad_final_g': 'grad_w', 'delta_ada_w': 'delta_w', 'delta_ada_b': 'delta_w', 'delta_norm1_g': 'delta_w', 'delta_w_in': 'delta_w', 'delta_conv_w': 'delta_w', 'delta_spatial_w': 'delta_w', 'delta_spatial_b': 'delta_w', 'delta_v_norm_g': 'delta_w', 'delta_a_log': 'delta_w', 'delta_dt_bias': 'delta_w', 'delta_o_norm_g': 'delta_w', 'delta_w_branch_a': 'delta_w', 'delta_w_branch_b': 'delta_w', 'delta_w_out': 'delta_w', 'delta_norm2_g': 'delta_w', 'delta_w_ffn_in': 'delta_w', 'delta_w_ffn_out': 'delta_w', 'delta_final_g': 'delta_w', 'new_m_ada_w': 'new_m', 'new_m_ada_b': 'new_m', 'new_m_norm1_g': 'new_m', 'new_m_w_in': 'new_m', 'new_m_conv_w': 'new_m', 'new_m_spatial_w': 'new_m', 'new_m_spatial_b': 'new_m', 'new_m_v_norm_g': 'new_m', 'new_m_a_log': 'new_m', 'new_m_dt_bias': 'new_m', 'new_m_o_norm_g': 'new_m', 'new_m_w_branch_a': 'new_m', 'new_m_w_branch_b': 'new_m', 'new_m_w_out': 'new_m', 'new_m_norm2_g': 'new_m', 'new_m_w_ffn_in': 'new_m', 'new_m_w_ffn_out': 'new_m', 'new_m_final_g': 'new_m', 'new_v_ada_w': 'new_v', 'new_v_ada_b': 'new_v', 'new_v_norm1_g': 'new_v', 'new_v_w_in': 'new_v', 'new_v_conv_w': 'new_v', 'new_v_spatial_w': 'new_v', 'new_v_spatial_b': 'new_v', 'new_v_v_norm_g': 'new_v', 'new_v_a_log': 'new_v', 'new_v_dt_bias': 'new_v', 'new_v_o_norm_g': 'new_v', 'new_v_w_branch_a': 'new_v', 'new_v_w_branch_b': 'new_v', 'new_v_w_out': 'new_v', 'new_v_norm2_g': 'new_v', 'new_v_w_ffn_in': 'new_v', 'new_v_w_ffn_out': 'new_v', 'new_v_final_g': 'new_v'}


def _forward(args):
    return _fwd_reference(*[args[k] for k in FWD_PARAMS])


def _output_shape():
    out = _jax.eval_shape(lambda: _forward(_fwd_setup_inputs(0)))
    return out.shape, out.dtype

N_MICROBATCH = 1
ADAM_LR = 0.001
ADAM_B1 = 0.9
ADAM_B2 = 0.999
ADAM_EPS = 1e-08
ADAM_WD = 0.01
ADAM_STEP = 10
PER_EXAMPLE_BATCH_AXIS = {'x': 0, 'c': 0, 'loss_target': 0}
SHARED_INPUTS = []
_WEIGHT_DTYPES = {'ada_w': _jnp.float32, 'ada_b': _jnp.float32, 'norm1_g': _jnp.float32, 'w_in': _jnp.float32, 'conv_w': _jnp.float32, 'spatial_w': _jnp.float32, 'spatial_b': _jnp.float32, 'v_norm_g': _jnp.float32, 'a_log': _jnp.float32, 'dt_bias': _jnp.float32, 'o_norm_g': _jnp.float32, 'w_branch_a': _jnp.float32, 'w_branch_b': _jnp.float32, 'w_out': _jnp.float32, 'norm2_g': _jnp.float32, 'w_ffn_in': _jnp.float32, 'w_ffn_out': _jnp.float32, 'final_g': _jnp.float32}
MOMENT_SCALE = {'ada_w': 4.863293e-02, 'ada_b': 7.997666e-02, 'norm1_g': 4.573332e-02, 'w_in': 1.682278e-02, 'conv_w': 1.478124e-02, 'spatial_w': 1.531083e-02, 'spatial_b': 2.201396e-02, 'v_norm_g': 1.541689e-02, 'a_log': 6.480009e-02, 'dt_bias': 6.307986e-02, 'o_norm_g': 5.521028e-02, 'w_branch_a': 2.655376e-02, 'w_branch_b': 1.952793e-02, 'w_out': 3.308769e-02, 'norm2_g': 5.324386e-02, 'w_ffn_in': 2.326063e-02, 'w_ffn_out': 3.790217e-02, 'final_g': 3.199211e+01}


def _to_microbatches(a, axis):
    t = _jnp.moveaxis(a, axis, 0)
    t = t.reshape((N_MICROBATCH, t.shape[0] // N_MICROBATCH) + t.shape[1:])
    return _jnp.moveaxis(t, 1, axis + 1)


def setup_inputs(seed: int = 0) -> dict:
    inp = _fwd_setup_inputs(seed)
    key = _jax.random.fold_in(_jax.random.key(seed), 7919)
    shape, _ = _output_shape()
    out = dict(inp)
    out["loss_target"] = _jax.random.normal(_jax.random.fold_in(key, 0), shape, _jnp.float32)
    for i, name in enumerate(TWIN_WEIGHTS):
        w = inp[name].astype(_jnp.float32)
        if MOMENT_SCALE is None:
            s = _jnp.sqrt(_jnp.mean(_jnp.square(w)) + 1e-30)
        else:
            s = MOMENT_SCALE[name]
        km, kv = _jax.random.split(_jax.random.fold_in(key, i + 1))
        out[name] = w
        out["m_" + name] = s * _jax.random.normal(km, w.shape, _jnp.float32)
        out["v_" + name] = (s * s) * _jax.random.uniform(kv, w.shape, _jnp.float32, 0.5, 1.5)
    if N_MICROBATCH > 1:
        for name, axis in PER_EXAMPLE_BATCH_AXIS.items():
            out[name] = _to_microbatches(out[name], axis)
    return {'x': out['x'], 'c': out['c'], 'ada_w': out['ada_w'], 'ada_b': out['ada_b'], 'norm1_g': out['norm1_g'], 'w_in': out['w_in'], 'conv_w': out['conv_w'], 'spatial_w': out['spatial_w'], 'spatial_b': out['spatial_b'], 'v_norm_g': out['v_norm_g'], 'a_log': out['a_log'], 'dt_bias': out['dt_bias'], 'o_norm_g': out['o_norm_g'], 'w_branch_a': out['w_branch_a'], 'w_branch_b': out['w_branch_b'], 'w_out': out['w_out'], 'norm2_g': out['norm2_g'], 'w_ffn_in': out['w_ffn_in'], 'w_ffn_out': out['w_ffn_out'], 'final_g': out['final_g'], 'loss_target': out['loss_target'], 'm_ada_w': out['m_ada_w'], 'm_ada_b': out['m_ada_b'], 'm_norm1_g': out['m_norm1_g'], 'm_w_in': out['m_w_in'], 'm_conv_w': out['m_conv_w'], 'm_spatial_w': out['m_spatial_w'], 'm_spatial_b': out['m_spatial_b'], 'm_v_norm_g': out['m_v_norm_g'], 'm_a_log': out['m_a_log'], 'm_dt_bias': out['m_dt_bias'], 'm_o_norm_g': out['m_o_norm_g'], 'm_w_branch_a': out['m_w_branch_a'], 'm_w_branch_b': out['m_w_branch_b'], 'm_w_out': out['m_w_out'], 'm_norm2_g': out['m_norm2_g'], 'm_w_ffn_in': out['m_w_ffn_in'], 'm_w_ffn_out': out['m_w_ffn_out'], 'm_final_g': out['m_final_g'], 'v_ada_w': out['v_ada_w'], 'v_ada_b': out['v_ada_b'], 'v_norm1_g': out['v_norm1_g'], 'v_w_in': out['v_w_in'], 'v_conv_w': out['v_conv_w'], 'v_spatial_w': out['v_spatial_w'], 'v_spatial_b': out['v_spatial_b'], 'v_v_norm_g': out['v_v_norm_g'], 'v_a_log': out['v_a_log'], 'v_dt_bias': out['v_dt_bias'], 'v_o_norm_g': out['v_o_norm_g'], 'v_w_branch_a': out['v_w_branch_a'], 'v_w_branch_b': out['v_w_branch_b'], 'v_w_out': out['v_w_out'], 'v_norm2_g': out['v_norm2_g'], 'v_w_ffn_in': out['v_w_ffn_in'], 'v_w_ffn_out': out['v_w_ffn_out'], 'v_final_g': out['v_final_g']}


def _loss(weights, diff, rest, loss_target):
    with _jax.named_scope("forward"):
        args = {**rest, TWIN_DIFF_INPUT: diff, **{k: w.astype(_WEIGHT_DTYPES[k]) for k, w in weights.items()}}
        y = _forward(args)
    with _jax.named_scope("loss_head"):
        err = _jnp.square(y.astype(_jnp.float32) - loss_target)
        return 0.5 * _jnp.sum(_jnp.mean(err, axis=-1)) if err.ndim else 0.5 * err


def _adamw(w, g, m, v):
    m = ADAM_B1 * m + (1.0 - ADAM_B1) * g
    v = ADAM_B2 * v + (1.0 - ADAM_B2) * _jnp.square(g)
    m_hat = m / (1.0 - ADAM_B1 ** ADAM_STEP)
    v_hat = v / (1.0 - ADAM_B2 ** ADAM_STEP)
    delta = -ADAM_LR * (m_hat / (_jnp.sqrt(v_hat) + ADAM_EPS) + ADAM_WD * w)
    return delta, m, v


def reference(x, c, ada_w, ada_b, norm1_g, w_in, conv_w, spatial_w, spatial_b, v_norm_g, a_log, dt_bias, o_norm_g, w_branch_a, w_branch_b, w_out, norm2_g, w_ffn_in, w_ffn_out, final_g, loss_target, m_ada_w, m_ada_b, m_norm1_g, m_w_in, m_conv_w, m_spatial_w, m_spatial_b, m_v_norm_g, m_a_log, m_dt_bias, m_o_norm_g, m_w_branch_a, m_w_branch_b, m_w_out, m_norm2_g, m_w_ffn_in, m_w_ffn_out, m_final_g, v_ada_w, v_ada_b, v_norm1_g, v_w_in, v_conv_w, v_spatial_w, v_spatial_b, v_v_norm_g, v_a_log, v_dt_bias, v_o_norm_g, v_w_branch_a, v_w_branch_b, v_w_out, v_norm2_g, v_w_ffn_in, v_w_ffn_out, v_final_g):
    given = dict(x=x, c=c, ada_w=ada_w, ada_b=ada_b, norm1_g=norm1_g, w_in=w_in, conv_w=conv_w, spatial_w=spatial_w, spatial_b=spatial_b, v_norm_g=v_norm_g, a_log=a_log, dt_bias=dt_bias, o_norm_g=o_norm_g, w_branch_a=w_branch_a, w_branch_b=w_branch_b, w_out=w_out, norm2_g=norm2_g, w_ffn_in=w_ffn_in, w_ffn_out=w_ffn_out, final_g=final_g, loss_target=loss_target, m_ada_w=m_ada_w, m_ada_b=m_ada_b, m_norm1_g=m_norm1_g, m_w_in=m_w_in, m_conv_w=m_conv_w, m_spatial_w=m_spatial_w, m_spatial_b=m_spatial_b, m_v_norm_g=m_v_norm_g, m_a_log=m_a_log, m_dt_bias=m_dt_bias, m_o_norm_g=m_o_norm_g, m_w_branch_a=m_w_branch_a, m_w_branch_b=m_w_branch_b, m_w_out=m_w_out, m_norm2_g=m_norm2_g, m_w_ffn_in=m_w_ffn_in, m_w_ffn_out=m_w_ffn_out, m_final_g=m_final_g, v_ada_w=v_ada_w, v_ada_b=v_ada_b, v_norm1_g=v_norm1_g, v_w_in=v_w_in, v_conv_w=v_conv_w, v_spatial_w=v_spatial_w, v_spatial_b=v_spatial_b, v_v_norm_g=v_v_norm_g, v_a_log=v_a_log, v_dt_bias=v_dt_bias, v_o_norm_g=v_o_norm_g, v_w_branch_a=v_w_branch_a, v_w_branch_b=v_w_branch_b, v_w_out=v_w_out, v_norm2_g=v_norm2_g, v_w_ffn_in=v_w_ffn_in, v_w_ffn_out=v_w_ffn_out, v_final_g=v_final_g)
    weights = {n: given[n] for n in TWIN_WEIGHTS}
    shared = {n: given[n] for n in SHARED_INPUTS}
    per_example = {n: given[n] for n in ['x', 'c']}
    grad_fn = _jax.value_and_grad(_loss, argnums=(0, 1))

    def one_microbatch(ex, loss_target):
        ex = dict(ex)
        diff = ex.pop(TWIN_DIFF_INPUT)
        return grad_fn(weights, diff, {**shared, **ex}, loss_target)

    if N_MICROBATCH == 1:
        loss, (grad_w, grad_x) = one_microbatch(per_example, given["loss_target"])
    else:
        def body(carry, xs):
            loss_sum, grad_sum = carry
            l_k, (gw_k, gx_k) = one_microbatch(xs[0], xs[1])
            with _jax.named_scope("update"):
                return (loss_sum + l_k, _jax.tree.map(_jnp.add, grad_sum, gw_k)), gx_k

        init = (_jnp.zeros((), _jnp.float32), _jax.tree.map(_jnp.zeros_like, weights))
        (loss, grad_w), grad_x = _jax.lax.scan(body, init, (per_example, given["loss_target"]))
    with _jax.named_scope("update"):
        delta_w, new_m, new_v = {}, {}, {}
        for n in TWIN_WEIGHTS:
            delta_w[n], new_m[n], new_v[n] = _adamw(weights[n], grad_w[n], given["m_" + n], given["v_" + n])
    return (loss, grad_x, *[grad_w[n] for n in TWIN_WEIGHTS], *[delta_w[n] for n in TWIN_WEIGHTS],
            *[new_m[n] for n in TWIN_WEIGHTS], *[new_v[n] for n in TWIN_WEIGHTS])
```

```python
import functools

import jax
import jax.numpy as jnp
from jax import lax
from jax.experimental import pallas as pl
from jax.experimental.pallas import tpu as pltpu

F32 = jnp.float32
BF16 = jnp.bfloat16
MXU_DTYPE = BF16
EPS = 1e-6
LANES = 128
SUBLANES = 8
GDN_CHUNK = 128
A_CHUNK = 128
GROUPS = 8
HEADS = 8
HEAD_DIM = 128
CONV_K = 4
N_DEV = 8
VMEM_LIMIT = 48 * 1024 * 1024
MESH = pl.DeviceIdType.MESH

ADAM_LR = 0.001
ADAM_B1 = 0.9
ADAM_B2 = 0.999
ADAM_EPS = 1e-08
ADAM_WD = 0.01
ADAM_STEP = 10

_NN = (((1,), (0,)), ((), ()))
_NT = (((1,), (1,)), ((), ()))
_TN = (((0,), (0,)), ((), ()))


def _mm(a, b, dims=_NN):
    return lax.dot_general(a.astype(MXU_DTYPE), b.astype(MXU_DTYPE), dims, preferred_element_type=F32)


def _mm_hi(a, b):
    return lax.dot_general(a, b, _NN, precision=lax.Precision.HIGHEST, preferred_element_type=F32)


def _tile(n, cands):
    for c in cands:
        if n % c == 0:
            return c
    return n


def _params(sem=None):
    return pltpu.CompilerParams(dimension_semantics=sem, vmem_limit_bytes=VMEM_LIMIT)


def _sigmoid(x):
    return 1.0 / (1.0 + jnp.exp(-x))


def _silu(x):
    return x * _sigmoid(x)


def _dsilu(x):
    s = _sigmoid(x)
    return s * (1.0 + x * (1.0 - s))


_GELU_C = 0.7978845608028654
_GELU_A = 0.044715


def _gelu(x):
    return 0.5 * x * (1.0 + jnp.tanh(_GELU_C * (x + _GELU_A * x * x * x)))


def _dgelu(x):
    t = jnp.tanh(_GELU_C * (x + _GELU_A * x * x * x))
    return 0.5 * (1.0 + t) + 0.5 * x * (1.0 - t * t) * _GELU_C * (1.0 + 3.0 * _GELU_A * x * x)


def _softplus(x):
    return jnp.maximum(x, 0.0) + jnp.log(1.0 + jnp.exp(-jnp.abs(x)))


_MM_TILES = (1024, 1408, 512, 256, 128)


def _matmul(a, b, mode, name, out_dtype=F32, acc=None):
    if mode == "nn":
        (m, k), n = a.shape, b.shape[1]
    elif mode == "nt":
        (m, k), n = a.shape, b.shape[0]
    else:
        (k, m), n = a.shape, b.shape[1]
    tm, tn, tk = _tile(m, _MM_TILES), _tile(n, _MM_TILES), _tile(k, _MM_TILES)
    nk = k // tk
    dims = {"nn": _NN, "nt": _NT, "tn": _TN}[mode]
    has_acc = acc is not None

    def body(*refs):
        if has_acc:
            a_ref, b_ref, c_ref, o_ref, acc_ref = refs
        else:
            a_ref, b_ref, o_ref, acc_ref = refs
        kk = pl.program_id(2)

        @pl.when(kk == 0)
        def _():
            acc_ref[...] = c_ref[...] if has_acc else jnp.zeros_like(acc_ref)

        acc_ref[...] += _mm(a_ref[...], b_ref[...], dims)

        @pl.when(kk == nk - 1)
        def _():
            o_ref[...] = acc_ref[...].astype(o_ref.dtype)

    a_spec = (pl.BlockSpec((tk, tm), lambda i, j, l: (l, i)) if mode == "tn"
              else pl.BlockSpec((tm, tk), lambda i, j, l: (i, l)))
    b_spec = (pl.BlockSpec((tn, tk), lambda i, j, l: (j, l)) if mode == "nt"
              else pl.BlockSpec((tk, tn), lambda i, j, l: (l, j)))
    o_spec = pl.BlockSpec((tm, tn), lambda i, j, l: (i, j))
    in_specs = [a_spec, b_spec] + ([o_spec] if has_acc else [])
    args = (a, b) + ((acc,) if has_acc else ())
    return pl.pallas_call(
        body, name=name, grid=(m // tm, n // tn, nk), in_specs=in_specs, out_specs=o_spec,
        out_shape=jax.ShapeDtypeStruct((m, n), out_dtype), scratch_shapes=[pltpu.VMEM((tm, tn), F32)],
        input_output_aliases=({2: 0} if has_acc else {}),
        compiler_params=_params(("parallel", "parallel", "arbitrary")))(*args)


_ROW_TILES = (512, 256, 128)


def _resid_norm(x, delta, gt, g, sc, sh, name):
    t, d = x.shape
    tt = _tile(t, _ROW_TILES)
    has = delta is not None

    def body(*refs):
        if has:
            x_ref, d_ref, gt_ref, g_ref, sc_ref, sh_ref, xo_ref, h_ref = refs
            xv = x_ref[...] + gt_ref[...] * d_ref[...]
            xo_ref[...] = xv
        else:
            x_ref, g_ref, sc_ref, sh_ref, h_ref = refs
            xv = x_ref[...]
        r = lax.rsqrt(jnp.mean(xv * xv, axis=-1, keepdims=True) + EPS)
        y = xv * r * g_ref[...]
        h_ref[...] = (y * (1.0 + sc_ref[...]) + sh_ref[...]).astype(h_ref.dtype)

    row = pl.BlockSpec((tt, d), lambda i: (i, 0))
    vec = pl.BlockSpec((1, d), lambda i: (0, 0))
    if has:
        return pl.pallas_call(
            body, name=name, grid=(t // tt,), in_specs=[row, row, vec, vec, vec, vec], out_specs=[row, row],
            out_shape=[jax.ShapeDtypeStruct((t, d), F32), jax.ShapeDtypeStruct((t, d), MXU_DTYPE)],
            compiler_params=_params(("parallel",)))(x, delta, gt, g, sc, sh)
    h = pl.pallas_call(
        body, name=name + "_first", grid=(t // tt,), in_specs=[row, vec, vec, vec], out_specs=row,
        out_shape=jax.ShapeDtypeStruct((t, d), MXU_DTYPE), compiler_params=_params(("parallel",)))(x, g, sc, sh)
    return x, h


def _final_loss(x, delta, gt, g, target, name):
    t, d = x.shape
    tt = _tile(t, _ROW_TILES)

    def body(x_ref, d_ref, gt_ref, g_ref, tg_ref, dx_ref, dg_ref, loss_ref):
        @pl.when(pl.program_id(0) == 0)
        def _():
            dg_ref[...] = jnp.zeros_like(dg_ref)
            loss_ref[...] = jnp.zeros_like(loss_ref)

        xv = x_ref[...] + gt_ref[...] * d_ref[...]
        r = lax.rsqrt(jnp.mean(xv * xv, axis=-1, keepdims=True) + EPS)
        xh = xv * r
        diff = xh * g_ref[...] - tg_ref[...]
        loss_ref[...] += jnp.sum(diff * diff) * (0.5 / d)
        dy = diff * (1.0 / d)
        dg_ref[...] += jnp.sum(dy * xh, axis=0, keepdims=True)
        dxh = dy * g_ref[...]
        dx_ref[...] = r * (dxh - xh * jnp.mean(dxh * xh, axis=-1, keepdims=True))

    row = pl.BlockSpec((tt, d), lambda i: (i, 0))
    vec = pl.BlockSpec((1, d), lambda i: (0, 0))
    tile = pl.BlockSpec((SUBLANES, LANES), lambda i: (0, 0))
    return pl.pallas_call(
        body, name=name, grid=(t // tt,), in_specs=[row, row, vec, vec, row], out_specs=[row, vec, tile],
        out_shape=[jax.ShapeDtypeStruct((t, d), F32), jax.ShapeDtypeStruct((1, d), F32),
                   jax.ShapeDtypeStruct((SUBLANES, LANES), F32)],
        compiler_params=_params(("arbitrary",)))(x, delta, gt, g, target)


def _norm_bwd(x, dh, dres, g, sc, name):
    t, d = x.shape
    tt = _tile(t, _ROW_TILES)

    def body(x_ref, dh_ref, dr_ref, g_ref, sc_ref, dx_ref, dsh_ref, dsc_ref, dg_ref):
        @pl.when(pl.program_id(0) == 0)
        def _():
            dsh_ref[...] = jnp.zeros_like(dsh_ref)
            dsc_ref[...] = jnp.zeros_like(dsc_ref)
            dg_ref[...] = jnp.zeros_like(dg_ref)

        xv, dh = x_ref[...], dh_ref[...]
        r = lax.rsqrt(jnp.mean(xv * xv, axis=-1, keepdims=True) + EPS)
        xh = xv * r
        gv, sc1 = g_ref[...], 1.0 + sc_ref[...]
        dsh_ref[...] += jnp.sum(dh, axis=0, keepdims=True)
        dsc_ref[...] += jnp.sum(dh * xh, axis=0, keepdims=True) * gv
        dg_ref[...] += jnp.sum(dh * xh, axis=0, keepdims=True) * sc1
        dxh = dh * (gv * sc1)
        dx_ref[...] = dr_ref[...] + r * (dxh - xh * jnp.mean(dxh * xh, axis=-1, keepdims=True))

    row = pl.BlockSpec((tt, d), lambda i: (i, 0))
    vec = pl.BlockSpec((1, d), lambda i: (0, 0))
    vshape = jax.ShapeDtypeStruct((1, d), F32)
    return pl.pallas_call(
        body, name=name, grid=(t // tt,), in_specs=[row, row, row, vec, vec], out_specs=[row, vec, vec, vec],
        out_shape=[jax.ShapeDtypeStruct((t, d), F32), vshape, vshape, vshape],
        compiler_params=_params(("arbitrary",)))(x, dh, dres, g, sc)


def _gate_bwd(dxo, branch, gt, name):
    t, d = dxo.shape
    tt = _tile(t, _ROW_TILES)

    def body(dx_ref, br_ref, gt_ref, db_ref, dgt_ref):
        @pl.when(pl.program_id(0) == 0)
        def _():
            dgt_ref[...] = jnp.zeros_like(dgt_ref)

        dx = dx_ref[...]
        db_ref[...] = (dx * gt_ref[...]).astype(db_ref.dtype)
        dgt_ref[...] += jnp.sum(dx * br_ref[...], axis=0, keepdims=True)

    row = pl.BlockSpec((tt, d), lambda i: (i, 0))
    vec = pl.BlockSpec((1, d), lambda i: (0, 0))
    return pl.pallas_call(
        body, name=name, grid=(t // tt,), in_specs=[row, row, vec], out_specs=[row, vec],
        out_shape=[jax.ShapeDtypeStruct((t, d), MXU_DTYPE), jax.ShapeDtypeStruct((1, d), F32)],
        compiler_params=_params(("arbitrary",)))(dxo, branch, gt)


def _swiglu_fwd(gu, name):
    t, f2 = gu.shape
    f = f2 // 2
    tt = _tile(t, (256, 128))

    def body(g_ref, u_ref, o_ref):
        o_ref[...] = (_silu(g_ref[...]) * u_ref[...]).astype(o_ref.dtype)

    return pl.pallas_call(
        body, name=name, grid=(t // tt,),
        in_specs=[pl.BlockSpec((tt, f), lambda i: (i, 0)), pl.BlockSpec((tt, f), lambda i: (i, 1))],
        out_specs=pl.BlockSpec((tt, f), lambda i: (i, 0)), out_shape=jax.ShapeDtypeStruct((t, f), MXU_DTYPE),
        compiler_params=_params(("parallel",)))(gu, gu)


def _swiglu_bwd(gu, da, name):
    t, f2 = gu.shape
    f = f2 // 2
    tt = _tile(t, (256, 128))

    def body(g_ref, u_ref, da_ref, o_ref):
        gate, da = g_ref[...], da_ref[...]
        o_ref[:, :f] = (da * u_ref[...] * _dsilu(gate)).astype(o_ref.dtype)
        o_ref[:, f:] = (da * _silu(gate)).astype(o_ref.dtype)

    return pl.pallas_call(
        body, name=name, grid=(t // tt,),
        in_specs=[pl.BlockSpec((tt, f), lambda i: (i, 0)), pl.BlockSpec((tt, f), lambda i: (i, 1)),
                  pl.BlockSpec((tt, f), lambda i: (i, 0))],
        out_specs=pl.BlockSpec((tt, f2), lambda i: (i, 0)), out_shape=jax.ShapeDtypeStruct((t, f2), MXU_DTYPE),
        compiler_params=_params(("parallel",)))(gu, gu, da)


def _merge_fwd(pa, pb, gates, name):
    t, d = pa.shape
    tt = _tile(t, _ROW_TILES)

    def body(pa_ref, pb_ref, ga_ref, gb_ref, o_ref):
        o_ref[...] = (_sigmoid(ga_ref[...]) * pa_ref[...] + _sigmoid(gb_ref[...]) * pb_ref[...]).astype(o_ref.dtype)

    row = pl.BlockSpec((tt, d), lambda i: (i, 0))
    row1 = pl.BlockSpec((tt, d), lambda i: (i, 1))
    return pl.pallas_call(
        body, name=name, grid=(t // tt,), in_specs=[row, row, row, row1], out_specs=row,
        out_shape=jax.ShapeDtypeStruct((t, d), MXU_DTYPE), compiler_params=_params(("parallel",)))(pa, pb, gates, gates)


def _merge_bwd(dm, pa, pb, gates, name):
    t, d = pa.shape
    tt = _tile(t, _ROW_TILES)

    def body(dm_ref, pa_ref, pb_ref, ga_ref, gb_ref, dpa_ref, dpb_ref, dg_ref):
        dm = dm_ref[...]
        sa, sb = _sigmoid(ga_ref[...]), _sigmoid(gb_ref[...])
        dpa_ref[...] = (dm * sa).astype(dpa_ref.dtype)
        dpb_ref[...] = (dm * sb).astype(dpb_ref.dtype)
        dg_ref[:, :d] = (dm * pa_ref[...] * sa * (1.0 - sa)).astype(dg_ref.dtype)
        dg_ref[:, d:] = (dm * pb_ref[...] * sb * (1.0 - sb)).astype(dg_ref.dtype)

    row = pl.BlockSpec((tt, d), lambda i: (i, 0))
    row1 = pl.BlockSpec((tt, d), lambda i: (i, 1))
    wide = pl.BlockSpec((tt, 2 * d), lambda i: (i, 0))
    return pl.pallas_call(
        body, name=name, grid=(t // tt,), in_specs=[row, row, row, row, row1], out_specs=[row, row, wide],
        out_shape=[jax.ShapeDtypeStruct((t, d), MXU_DTYPE), jax.ShapeDtypeStruct((t, d), MXU_DTYPE),
                   jax.ShapeDtypeStruct((t, 2 * d), MXU_DTYPE)],
        compiler_params=_params(("parallel",)))(dm, pa, pb, gates, gates)


def _tri_masks(n):
    ri = lax.broadcasted_iota(jnp.int32, (n, n), 0)
    ci = lax.broadcasted_iota(jnp.int32, (n, n), 1)
    return ri >= ci, ri > ci, ri == ci


def _mixer_a_fwd(uv, w_s, b_col, g_v, name):
    t, w2 = uv.shape
    w = w2 // 2
    c = A_CHUNK

    def body(u_ref, v_ref, w_ref, b_ref, gv_ref, y_ref):
        tril, _, _ = _tri_masks(c)
        ug, vg = _gelu(u_ref[...]), _gelu(v_ref[...])
        for g in range(GROUPS):
            sl = slice(g * c, (g + 1) * c)
            vt = vg[:, sl]
            r = lax.rsqrt(jnp.mean(vt * vt, axis=-1, keepdims=True) + EPS)
            vn = vt * r * gv_ref[:, sl]
            s = _mm(jnp.where(tril, w_ref[g], 0.0), vn) + b_ref[g]
            y_ref[:, sl] = (ug[:, sl] * s).astype(y_ref.dtype)

    return pl.pallas_call(
        body, name=name, grid=(t // c,),
        in_specs=[pl.BlockSpec((c, w), lambda i: (i, 0)), pl.BlockSpec((c, w), lambda i: (i, 1)),
                  pl.BlockSpec((GROUPS, c, c), lambda i: (0, 0, 0)), pl.BlockSpec((GROUPS, c, 1), lambda i: (0, 0, 0)),
                  pl.BlockSpec((1, w), lambda i: (0, 0))],
        out_specs=pl.BlockSpec((c, w), lambda i: (i, 0)), out_shape=jax.ShapeDtypeStruct((t, w), MXU_DTYPE),
        compiler_params=_params(("parallel",)))(uv, uv, w_s, b_col, g_v)


def _mixer_a_bwd(uv, dy, w_s, w_st, b_col, g_v, name):
    t, w2 = uv.shape
    w = w2 // 2
    c = A_CHUNK

    def body(u_ref, v_ref, dy_ref, w_ref, wt_ref, b_ref, gv_ref, duv_ref, dw_ref, db_ref, dgv_ref):
        @pl.when(pl.program_id(0) == 0)
        def _():
            dw_ref[...] = jnp.zeros_like(dw_ref)
            db_ref[...] = jnp.zeros_like(db_ref)
            dgv_ref[...] = jnp.zeros_like(dgv_ref)

        tril, _, _ = _tri_masks(c)
        triu = lax.broadcasted_iota(jnp.int32, (c, c), 0) <= lax.broadcasted_iota(jnp.int32, (c, c), 1)
        up, vp = u_ref[...], v_ref[...]
        ug, vg = _gelu(up), _gelu(vp)
        for g in range(GROUPS):
            sl = slice(g * c, (g + 1) * c)
            vt = vg[:, sl]
            r = lax.rsqrt(jnp.mean(vt * vt, axis=-1, keepdims=True) + EPS)
            vh = vt * r
            gv = gv_ref[:, sl]
            vn = vh * gv
            s = _mm(jnp.where(tril, w_ref[g], 0.0), vn) + b_ref[g]
            dy = dy_ref[:, sl]
            ds = dy * ug[:, sl]
            dw_ref[g] += jnp.where(tril, _mm(ds, vn, _NT), 0.0)
            db_ref[g] += jnp.sum(ds, axis=1, keepdims=True)
            dvn = _mm(jnp.where(triu, wt_ref[g], 0.0), ds)
            dgv_ref[:, sl] += jnp.sum(dvn * vh, axis=0, keepdims=True)
            dvh = dvn * gv
            dvt = r * (dvh - vh * jnp.mean(dvh * vh, axis=-1, keepdims=True))
            duv_ref[:, sl] = (dy * s * _dgelu(up[:, sl])).astype(duv_ref.dtype)
            duv_ref[:, w + g * c:w + (g + 1) * c] = (dvt * _dgelu(vp[:, sl])).astype(duv_ref.dtype)

    full3 = lambda shape: pl.BlockSpec(shape, lambda i: (0, 0, 0))
    return pl.pallas_call(
        body, name=name, grid=(t // c,),
        in_specs=[pl.BlockSpec((c, w), lambda i: (i, 0)), pl.BlockSpec((c, w), lambda i: (i, 1)),
                  pl.BlockSpec((c, w), lambda i: (i, 0)), full3((GROUPS, c, c)), full3((GROUPS, c, c)),
                  full3((GROUPS, c, 1)), pl.BlockSpec((1, w), lambda i: (0, 0))],
        out_specs=[pl.BlockSpec((c, w2), lambda i: (i, 0)), full3((GROUPS, c, c)), full3((GROUPS, c, 1)),
                   pl.BlockSpec((1, w), lambda i: (0, 0))],
        out_shape=[jax.ShapeDtypeStruct((t, w2), MXU_DTYPE), jax.ShapeDtypeStruct((GROUPS, c, c), F32),
                   jax.ShapeDtypeStruct((GROUPS, c, 1), F32), jax.ShapeDtypeStruct((1, w), F32)],
        compiler_params=_params(("arbitrary",)))(uv, uv, dy, w_s, w_st, b_col, g_v)


_Q_SCALE = HEAD_DIM ** -0.5


def _conv_taps(ext, w_ref):
    shifted = [ext[SUBLANES:]] + [pltpu.roll(ext, s, 0)[SUBLANES:] for s in range(1, CONV_K)]
    acc = shifted[0] * w_ref[pl.ds(CONV_K - 1, 1), :]
    for s in range(1, CONV_K):
        acc = acc + shifted[s] * w_ref[pl.ds(CONV_K - 1 - s, 1), :]
    return acc, shifted


def _conv_fwd(qkv, w, name):
    t, cw = qkv.shape
    tt = _tile(t, (256, 128))
    hb = tt // SUBLANES

    def body(x_ref, p_ref, w_ref, o_ref):
        prev = jnp.where(pl.program_id(0) > 0, p_ref[...], 0.0)
        acc, _ = _conv_taps(jnp.concatenate([prev, x_ref[...]], axis=0), w_ref)
        y = _silu(acc)
        for which in range(3):
            for h in range(HEADS):
                lo = (which * HEADS + h) * HEAD_DIM
                seg = y[:, lo:lo + HEAD_DIM]
                if which < 2:
                    seg = seg * lax.rsqrt(jnp.sum(seg * seg, axis=-1, keepdims=True) + EPS)
                if which == 0:
                    seg = seg * _Q_SCALE
                o_ref[which, h] = seg

    return pl.pallas_call(
        body, name=name, grid=(t // tt,),
        in_specs=[pl.BlockSpec((tt, cw), lambda i: (i, 0)),
                  pl.BlockSpec((SUBLANES, cw), lambda i: (jnp.maximum(i * hb - 1, 0), 0)),
                  pl.BlockSpec((CONV_K, cw), lambda i: (0, 0))],
        out_specs=pl.BlockSpec((3, HEADS, tt, HEAD_DIM), lambda i: (0, 0, i, 0)),
        out_shape=jax.ShapeDtypeStruct((3, HEADS, t, HEAD_DIM), F32),
        compiler_params=_params(("parallel",)))(qkv, qkv, w)


def _conv_bwd_pre(qkv, dq, dk, dv, w, name):
    t, cw = qkv.shape
    tt = _tile(t, (256, 128))
    hb = tt // SUBLANES

    def body(x_ref, p_ref, dq_ref, dk_ref, dv_ref, w_ref, da_ref, dw_ref):
        @pl.when(pl.program_id(0) == 0)
        def _():
            dw_ref[...] = jnp.zeros_like(dw_ref)

        prev = jnp.where(pl.program_id(0) > 0, p_ref[...], 0.0)
        acc, shifted = _conv_taps(jnp.concatenate([prev, x_ref[...]], axis=0), w_ref)
        y = _silu(acc)
        d_refs = (dq_ref, dk_ref, dv_ref)
        for which in range(3):
            for h in range(HEADS):
                lo = (which * HEADS + h) * HEAD_DIM
                sl = slice(lo, lo + HEAD_DIM)
                dn = d_refs[which][h]
                if which < 2:
                    seg = y[:, sl]
                    rho = lax.rsqrt(jnp.sum(seg * seg, axis=-1, keepdims=True) + EPS)
                    nrm = seg * rho
                    if which == 0:
                        dn = dn * _Q_SCALE
                    dn = rho * (dn - nrm * jnp.sum(dn * nrm, axis=-1, keepdims=True))
                dacc = dn * _dsilu(acc[:, sl])
                da_ref[:, sl] = dacc
                for s in range(CONV_K):
                    dw_ref[pl.ds(CONV_K - 1 - s, 1), sl] += jnp.sum(dacc * shifted[s][:, sl], axis=0, keepdims=True)

    head = pl.BlockSpec((HEADS, tt, HEAD_DIM), lambda i: (0, i, 0))
    return pl.pallas_call(
        body, name=name, grid=(t // tt,),
        in_specs=[pl.BlockSpec((tt, cw), lambda i: (i, 0)),
                  pl.BlockSpec((SUBLANES, cw), lambda i: (jnp.maximum(i * hb - 1, 0), 0)),
                  head, head, head, pl.BlockSpec((CONV_K, cw), lambda i: (0, 0))],
        out_specs=[pl.BlockSpec((tt, cw), lambda i: (i, 0)), pl.BlockSpec((CONV_K, cw), lambda i: (0, 0))],
        out_shape=[jax.ShapeDtypeStruct((t, cw), F32), jax.ShapeDtypeStruct((CONV_K, cw), F32)],
        compiler_params=_params(("arbitrary",)))(qkv, qkv, dq, dk, dv, w)


def _conv_bwd_in(dacc, w, name):
    t, cw = dacc.shape
    tt = _tile(t, (256, 128))
    hb = tt // SUBLANES
    nt = t // tt
    rows = tt + SUBLANES

    def body(d_ref, n_ref, w_ref, o_ref):
        cur = d_ref[...]
        nxt = jnp.where(pl.program_id(0) < nt - 1, n_ref[...], 0.0)
        ext = jnp.concatenate([cur, nxt], axis=0)
        acc = cur * w_ref[pl.ds(CONV_K - 1, 1), :]
        for s in range(1, CONV_K):
            acc = acc + pltpu.roll(ext, rows - s, 0)[:tt] * w_ref[pl.ds(CONV_K - 1 - s, 1), :]
        o_ref[...] = acc.astype(o_ref.dtype)

    return pl.pallas_call(
        body, name=name, grid=(nt,),
        in_specs=[pl.BlockSpec((tt, cw), lambda i: (i, 0)),
                  pl.BlockSpec((SUBLANES, cw), lambda i: (jnp.minimum((i + 1) * hb, t // SUBLANES - 1), 0)),
                  pl.BlockSpec((CONV_K, cw), lambda i: (0, 0))],
        out_specs=pl.BlockSpec((tt, cw), lambda i: (i, 0)), out_shape=jax.ShapeDtypeStruct((t, cw), MXU_DTYPE),
        compiler_params=_params(("parallel",)))(dacc, dacc, w)


def _inv_unit_lower(a, eye):
    x = -a
    p = jnp.where(eye, 1.0, 0.0) + x
    xp = _mm_hi(x, x)
    steps = GDN_CHUNK.bit_length() - 2
    for s in range(steps):
        p = p + _mm_hi(xp, p)
        if s < steps - 1:
            xp = _mm_hi(xp, xp)
    return p


def _rowsum(x):
    return jnp.sum(x, axis=1, keepdims=True)


def _colsum(x):
    return jnp.sum(x, axis=0, keepdims=True)


class _Pre:
    pass


def _gdn_pre(q, k, v, araw, braw, alog, dtb):
    c = GDN_CHUNK
    p = _Pre()
    p.tril, p.strict, p.eye = _tri_masks(c)
    p.to_col = lambda row: _rowsum(jnp.where(p.eye, row, 0.0))
    p.to_row = lambda col: _colsum(jnp.where(p.eye, col, 0.0))
    p.a_neg = -jnp.exp(jnp.full((1, c), alog, F32))
    p.xg = araw + dtb
    p.g_row = p.a_neg * _softplus(p.xg)
    p.beta_row = _sigmoid(braw)
    p.beta = p.to_col(p.beta_row)
    gam = _rowsum(jnp.where(p.tril, p.g_row, 0.0))
    gam_last = _rowsum(p.g_row)
    p.dm = jnp.where(p.tril, jnp.exp(jnp.where(p.tril, gam - p.to_row(gam), 0.0)), 0.0)
    p.e, p.ek, p.el = jnp.exp(gam), jnp.exp(gam_last - gam), jnp.exp(gam_last)
    p.kb = k * p.beta
    p.kk = _mm(p.kb, k, _NT)
    p.t = _inv_unit_lower(jnp.where(p.strict, p.kk * p.dm, 0.0), p.eye)
    p.vb, p.kbe = v * p.beta, p.kb * p.e
    p.u, p.w = _mm(p.t, p.vb), _mm(p.t, p.kbe)
    p.qk0 = _mm(q, k, _NT)
    p.qk = p.qk0 * p.dm
    p.qd, p.kd = q * p.e, k * p.ek
    return p


def _gdn_specs(n, reverse):
    c, dk = GDN_CHUNK, HEAD_DIM
    ix = (lambda i: n - 1 - i) if reverse else (lambda i: i)
    smem = pl.BlockSpec(memory_space=pltpu.SMEM)
    qkv = [pl.BlockSpec((None, None, c, dk), functools.partial(lambda w, h, i: (w, h, ix(i), 0), w)) for w in range(3)]
    row = pl.BlockSpec((None, None, 1, c), lambda h, i: (h, ix(i), 0, 0))
    tok = pl.BlockSpec((None, c, dk), lambda h, i: (h, ix(i), 0))
    state = pl.BlockSpec((None, None, dk, dk), lambda h, i: (h, ix(i), 0, 0))
    return smem, qkv, row, tok, state


def _gdn_fwd(qkv_h, araw, braw, alog, dtb, name):
    _, hh, t, dk = qkv_h.shape
    n = t // GDN_CHUNK
    smem, qkv, row, tok, state = _gdn_specs(n, False)

    def body(alog_ref, dt_ref, q_ref, k_ref, v_ref, a_ref, b_ref, o_ref, so_ref, s_ref):
        h = pl.program_id(0)

        @pl.when(pl.program_id(1) == 0)
        def _():
            s_ref[...] = jnp.zeros_like(s_ref)

        p = _gdn_pre(q_ref[...], k_ref[...], v_ref[...], a_ref[...], b_ref[...], alog_ref[h], dt_ref[h])
        s = s_ref[...]
        vn = p.u - _mm(p.w, s)
        o_ref[...] = _mm(p.qd, s) + _mm(p.qk, vn)
        so_ref[...] = s
        s_ref[...] = s * p.el + _mm(p.kd, vn, _TN)

    return pl.pallas_call(
        body, name=name, grid=(hh, n), in_specs=[smem, smem] + qkv + [row, row], out_specs=[tok, state],
        out_shape=[jax.ShapeDtypeStruct((hh, t, dk), F32), jax.ShapeDtypeStruct((hh, n, dk, dk), F32)],
        scratch_shapes=[pltpu.VMEM((dk, dk), F32)],
        compiler_params=_params(("parallel", "arbitrary")))(alog, dtb, qkv_h, qkv_h, qkv_h, araw, braw)


def _gdn_bwd(qkv_h, araw, braw, alog, dtb, states, do, name):
    _, hh, t, dk = qkv_h.shape
    c = GDN_CHUNK
    n = t // c
    smem, qkv, row, tok, state = _gdn_specs(n, True)
    acc = pl.BlockSpec((None, 1, LANES), lambda h, i: (h, 0, 0))

    def body(alog_ref, dt_ref, q_ref, k_ref, v_ref, a_ref, b_ref, s_ref, do_ref,
             dq_ref, dk_ref, dv_ref, da_ref, db_ref, dal_ref, ddt_ref, ds_ref):
        h = pl.program_id(0)

        @pl.when(pl.program_id(1) == 0)
        def _():
            ds_ref[...] = jnp.zeros_like(ds_ref)
            dal_ref[...] = jnp.zeros_like(dal_ref)
            ddt_ref[...] = jnp.zeros_like(ddt_ref)

        q, k, v = q_ref[...], k_ref[...], v_ref[...]
        p = _gdn_pre(q, k, v, a_ref[...], b_ref[...], alog_ref[h], dt_ref[h])
        s, do, dsp = s_ref[...], do_ref[...], ds_ref[...]
        vn = p.u - _mm(p.w, s)
        dqd = _mm(do, s, _NT)
        dqk = _mm(do, vn, _NT)
        dvn = _mm(p.qk, do, _TN) + _mm(p.kd, dsp)
        dkd = _mm(vn, dsp, _NT)
        d_el = _colsum(_rowsum(s * dsp))
        ds_ref[...] = dsp * p.el + _mm(p.qd, do, _TN) - _mm(p.w, dvn, _TN)
        dw = -_mm(dvn, s, _NT)
        d_t = _mm(dvn, p.vb, _NT) + _mm(dw, p.kbe, _NT)
        dvb, dkbe = _mm(p.t, dvn, _TN), _mm(p.t, dw, _TN)
        d_a = jnp.where(p.strict, -_mm(p.t, _mm(d_t, p.t, _NT), _TN), 0.0)
        dkk = d_a * p.dm
        dqk0 = dqk * p.dm
        ddm = d_a * p.kk + dqk * p.qk0
        dkb = _mm(dkk, k) + dkbe * p.e
        dq_ref[...] = _mm(dqk0, k) + dqd * p.e
        dk_ref[...] = _mm(dkk, p.kb, _TN) + _mm(dqk0, q, _TN) + dkd * p.ek + dkb * p.beta
        dv_ref[...] = dvb * p.beta
        dbeta = _rowsum(dkb * k) + _rowsum(dvb * v)
        d_e = _rowsum(dqd * q) + _rowsum(dkbe * p.kb)
        d_ek = _rowsum(dkd * k)
        m = ddm * p.dm
        dgam = d_e * p.e - d_ek * p.ek + _rowsum(m) - p.to_col(_colsum(m))
        dgam_last = _colsum(d_ek * p.ek) + d_el * p.el
        dg_row = _colsum(jnp.where(p.tril, dgam, 0.0)) + dgam_last
        da_row = dg_row * p.a_neg * _sigmoid(p.xg)
        da_ref[...] = da_row
        db_ref[...] = p.to_row(dbeta) * p.beta_row * (1.0 - p.beta_row)
        dal_ref[...] += _rowsum(dg_row * p.g_row)
        ddt_ref[...] += _rowsum(da_row)

    tok_shape = jax.ShapeDtypeStruct((hh, t, dk), F32)
    row_shape = jax.ShapeDtypeStruct((hh, n, 1, c), F32)
    acc_shape = jax.ShapeDtypeStruct((hh, 1, LANES), F32)
    return pl.pallas_call(
        body, name=name, grid=(hh, n), in_specs=[smem, smem] + qkv + [row, row, state, tok],
        out_specs=[tok, tok, tok, row, row, acc, acc],
        out_shape=[tok_shape, tok_shape, tok_shape, row_shape, row_shape, acc_shape, acc_shape],
        scratch_shapes=[pltpu.VMEM((dk, dk), F32)],
        compiler_params=_params(("parallel", "arbitrary")))(alog, dtb, qkv_h, qkv_h, qkv_h, araw, braw, states, do)


def _gdn_post_fwd(o, z, g_o, name):
    hh, t, dv = o.shape
    tt = _tile(t, _ROW_TILES)

    def body(o_ref, z_ref, g_ref, y_ref):
        for h in range(hh):
            sl = slice(h * dv, (h + 1) * dv)
            ov = o_ref[h]
            r = lax.rsqrt(jnp.mean(ov * ov, axis=-1, keepdims=True) + EPS)
            y_ref[:, sl] = (ov * r * g_ref[...] * _silu(z_ref[:, sl])).astype(y_ref.dtype)

    return pl.pallas_call(
        body, name=name, grid=(t // tt,),
        in_specs=[pl.BlockSpec((hh, tt, dv), lambda i: (0, i, 0)), pl.BlockSpec((tt, hh * dv), lambda i: (i, 0)),
                  pl.BlockSpec((1, dv), lambda i: (0, 0))],
        out_specs=pl.BlockSpec((tt, hh * dv), lambda i: (i, 0)),
        out_shape=jax.ShapeDtypeStruct((t, hh * dv), MXU_DTYPE), compiler_params=_params(("parallel",)))(o, z, g_o)


def _gdn_post_bwd(o, z, dy, g_o, name):
    hh, t, dv = o.shape
    tt = _tile(t, _ROW_TILES)

    def body(o_ref, z_ref, dy_ref, g_ref, do_ref, dz_ref, dg_ref):
        @pl.when(pl.program_id(0) == 0)
        def _():
            dg_ref[...] = jnp.zeros_like(dg_ref)

        gv = g_ref[...]
        for h in range(hh):
            sl = slice(h * dv, (h + 1) * dv)
            ov, zz, dy = o_ref[h], z_ref[:, sl], dy_ref[:, sl]
            r = lax.rsqrt(jnp.mean(ov * ov, axis=-1, keepdims=True) + EPS)
            oh = ov * r
            dz_ref[:, sl] = (dy * oh * gv * _dsilu(zz)).astype(dz_ref.dtype)
            don = dy * _silu(zz)
            dg_ref[...] += _colsum(don * oh)
            doh = don * gv
            do_ref[h] = r * (doh - oh * jnp.mean(doh * oh, axis=-1, keepdims=True))

    return pl.pallas_call(
        body, name=name, grid=(t // tt,),
        in_specs=[pl.BlockSpec((hh, tt, dv), lambda i: (0, i, 0)), pl.BlockSpec((tt, hh * dv), lambda i: (i, 0)),
                  pl.BlockSpec((tt, hh * dv), lambda i: (i, 0)), pl.BlockSpec((1, dv), lambda i: (0, 0))],
        out_specs=[pl.BlockSpec((hh, tt, dv), lambda i: (0, i, 0)), pl.BlockSpec((tt, hh * dv), lambda i: (i, 0)),
                   pl.BlockSpec((1, dv), lambda i: (0, 0))],
        out_shape=[jax.ShapeDtypeStruct((hh, t, dv), F32), jax.ShapeDtypeStruct((t, hh * dv), MXU_DTYPE),
                   jax.ShapeDtypeStruct((1, dv), F32)],
        compiler_params=_params(("arbitrary",)))(o, z, dy, g_o)


def _adamw(g, w, m, v):
    m = ADAM_B1 * m + (1.0 - ADAM_B1) * g
    v = ADAM_B2 * v + (1.0 - ADAM_B2) * (g * g)
    m_hat = m / (1.0 - ADAM_B1 ** ADAM_STEP)
    v_hat = v / (1.0 - ADAM_B2 ** ADAM_STEP)
    return -ADAM_LR * (m_hat / (jnp.sqrt(v_hat) + ADAM_EPS) + ADAM_WD * w), m, v


def _ada_fwd(c_all, ada_w, name):
    nl, d, cols = ada_w.shape
    b = c_all.shape[0]

    def body(c_ref, w_ref, o_ref):
        o_ref[...] = _mm_hi(_silu(c_ref[...]), w_ref[...])

    return pl.pallas_call(
        body, name=name, grid=(nl,),
        in_specs=[pl.BlockSpec((b, d), lambda i: (0, 0)), pl.BlockSpec((None, d, cols), lambda i: (i, 0, 0))],
        out_specs=pl.BlockSpec((None, b, cols), lambda i: (i, 0, 0)),
        out_shape=jax.ShapeDtypeStruct((nl, b, cols), F32), compiler_params=_params(("parallel",)))(c_all, ada_w)


def _ada_bwd(c_col, dm, w, m, v, name):
    nl, d, cols = w.shape
    b = c_col.shape[0]
    tr = _tile(d, (256, 128))

    def body(c_ref, dm_ref, w_ref, m_ref, v_ref, g_ref, dl_ref, mo_ref, vo_ref):
        g = _silu(c_ref[0]) * dm_ref[pl.ds(0, 1), :]
        for j in range(1, b):
            g = g + _silu(c_ref[j]) * dm_ref[pl.ds(j, 1), :]
        g_ref[...] = g
        dl_ref[...], mo_ref[...], vo_ref[...] = _adamw(g, w_ref[...], m_ref[...], v_ref[...])

    blk = pl.BlockSpec((None, tr, cols), lambda l, i: (l, i, 0))
    shape = jax.ShapeDtypeStruct((nl, d, cols), F32)
    return pl.pallas_call(
        body, name=name, grid=(nl, d // tr),
        in_specs=[pl.BlockSpec((b, tr, 1), lambda l, i: (0, i, 0)), pl.BlockSpec((None, b, cols), lambda l, i: (l, 0, 0)),
                  blk, blk, blk],
        out_specs=[blk, blk, blk, blk], out_shape=[shape] * 4,
        compiler_params=_params(("parallel", "parallel")))(c_col, dm, w, m, v)


def _sum_adam(parts, w, m, v, name):
    nl, npart, r, cdim = parts.shape
    tr = _tile(r, (256, 128))

    def body(p_ref, w_ref, m_ref, v_ref, g_ref, dl_ref, mo_ref, vo_ref):
        g = p_ref[0]
        for j in range(1, npart):
            g = g + p_ref[j]
        g_ref[...] = g
        dl_ref[...], mo_ref[...], vo_ref[...] = _adamw(g, w_ref[...], m_ref[...], v_ref[...])

    blk = pl.BlockSpec((None, tr, cdim), lambda l, i: (l, i, 0))
    shape = jax.ShapeDtypeStruct((nl, r, cdim), F32)
    return pl.pallas_call(
        body, name=name, grid=(nl, r // tr),
        in_specs=[pl.BlockSpec((None, npart, tr, cdim), lambda l, i: (l, 0, i, 0)), blk, blk, blk],
        out_specs=[blk, blk, blk, blk], out_shape=[shape] * 4,
        compiler_params=_params(("parallel", "parallel")))(parts, w, m, v)


def _pair_sum(x, tmp, core, name):
    _, r, cdim = x.shape
    tr = _tile(r, (256, 128))

    def body(core_ref, x_ref, t_ref, o_ref):
        o_ref[...] = x_ref[...] + t_ref[...]

    grid_spec = pltpu.PrefetchScalarGridSpec(
        num_scalar_prefetch=1, grid=(N_DEV // 2, r // tr),
        in_specs=[pl.BlockSpec((None, tr, cdim), lambda ch, i, core_ref: (2 * ch + core_ref[0], i, 0)),
                  pl.BlockSpec((None, tr, cdim), lambda ch, i, core_ref: (ch, i, 0))],
        out_specs=pl.BlockSpec((None, tr, cdim), lambda ch, i, core_ref: (ch, i, 0)))
    return pl.pallas_call(
        body, name=name, grid_spec=grid_spec, out_shape=jax.ShapeDtypeStruct((N_DEV // 2, r, cdim), F32),
        compiler_params=_params(("parallel", "parallel")))(core, x, tmp)


_ANY = pl.BlockSpec(memory_space=pl.ANY)
_CHIP_FLIPS = ((1, 0), (0, 1), (1, 1))


def _coords():
    return lax.axis_index("x"), lax.axis_index("y"), lax.axis_index("c")


def _flip(v, f):
    return 1 - v if f else v


def _a2a_direct(xs, name):
    n, ncp = len(xs), N_DEV - 1

    def body(*refs):
        ins, outs = refs[:n], refs[n:2 * n]
        send, recv, loc = refs[2 * n:]
        x, y, c = _coords()
        me = 4 * x + 2 * y + c
        local = [pltpu.make_async_copy(ins[i].at[me], outs[i].at[me], loc.at[i]) for i in range(n)]
        for cp in local:
            cp.start()
        remote = []
        for i in range(n):
            for k in range(1, N_DEV):
                px, py, pc = _flip(x, k & 4), _flip(y, k & 2), _flip(c, k & 1)
                cp = pltpu.make_async_remote_copy(
                    src_ref=ins[i].at[4 * px + 2 * py + pc], dst_ref=outs[i].at[me],
                    send_sem=send.at[i * ncp + k - 1], recv_sem=recv.at[i * ncp + k - 1],
                    device_id=(px, py, pc), device_id_type=MESH)
                cp.start()
                remote.append(cp)
        for cp in remote:
            cp.wait()
        for cp in local:
            cp.wait()

    return pl.pallas_call(
        body, name=name, in_specs=[_ANY] * n, out_specs=[_ANY] * n,
        out_shape=[jax.ShapeDtypeStruct(a.shape, a.dtype) for a in xs],
        scratch_shapes=[pltpu.SemaphoreType.DMA((n * ncp,)), pltpu.SemaphoreType.DMA((n * ncp,)),
                        pltpu.SemaphoreType.DMA((n,))])(*xs)


def _ag_ici(blocks, name):
    n, ncp = len(blocks), len(_CHIP_FLIPS)

    def body(*refs):
        ins, outs = refs[:n], refs[n:2 * n]
        send, recv, loc = refs[2 * n:]
        x, y, c = _coords()
        me = 4 * x + 2 * y + c
        local = [pltpu.make_async_copy(ins[i], outs[i].at[me], loc.at[i]) for i in range(n)]
        for cp in local:
            cp.start()
        remote = []
        for i in range(n):
            for j, (fx, fy) in enumerate(_CHIP_FLIPS):
                cp = pltpu.make_async_remote_copy(
                    src_ref=ins[i], dst_ref=outs[i].at[me], send_sem=send.at[i * ncp + j],
                    recv_sem=recv.at[i * ncp + j], device_id=(_flip(x, fx), _flip(y, fy), c), device_id_type=MESH)
                cp.start()
                remote.append(cp)
        for cp in remote:
            cp.wait()
        for cp in local:
            cp.wait()

    return pl.pallas_call(
        body, name=name, in_specs=[_ANY] * n, out_specs=[_ANY] * n,
        out_shape=[jax.ShapeDtypeStruct((N_DEV,) + a.shape, a.dtype) for a in blocks],
        scratch_shapes=[pltpu.SemaphoreType.DMA((n * ncp,)), pltpu.SemaphoreType.DMA((n * ncp,)),
                        pltpu.SemaphoreType.DMA((n,))])(*blocks)


def _ag_d2d(ys, name):
    n, ncp = len(ys), N_DEV // 2

    def body(*refs):
        ins, outs = refs[:n], refs[n:2 * n]
        send, recv = refs[2 * n:]
        x, y, c = _coords()
        remote = []
        for i in range(n):
            for ch in range(ncp):
                slot = 2 * ch + c
                cp = pltpu.make_async_remote_copy(
                    src_ref=ins[i].at[slot], dst_ref=outs[i].at[slot], send_sem=send.at[i * ncp + ch],
                    recv_sem=recv.at[i * ncp + ch], device_id=(x, y, 1 - c), device_id_type=MESH)
                cp.start()
                remote.append(cp)
        for cp in remote:
            cp.wait()

    return pl.pallas_call(
        body, name=name, in_specs=[_ANY] * n, out_specs=[_ANY] * n,
        out_shape=[jax.ShapeDtypeStruct(a.shape, a.dtype) for a in ys],
        input_output_aliases={i: i for i in range(n)},
        scratch_shapes=[pltpu.SemaphoreType.DMA((n * ncp,)), pltpu.SemaphoreType.DMA((n * ncp,))])(*ys)


def _rs_d2d(xs, name):
    n, ncp = len(xs), N_DEV // 2

    def body(*refs):
        ins, outs = refs[:n], refs[n:2 * n]
        send, recv = refs[2 * n:]
        x, y, c = _coords()
        remote = []
        for i in range(n):
            for ch in range(ncp):
                cp = pltpu.make_async_remote_copy(
                    src_ref=ins[i].at[2 * ch + 1 - c], dst_ref=outs[i].at[ch], send_sem=send.at[i * ncp + ch],
                    recv_sem=recv.at[i * ncp + ch], device_id=(x, y, 1 - c), device_id_type=MESH)
                cp.start()
                remote.append(cp)
        for cp in remote:
            cp.wait()

    return pl.pallas_call(
        body, name=name, in_specs=[_ANY] * n, out_specs=[_ANY] * n,
        out_shape=[jax.ShapeDtypeStruct((ncp,) + a.shape[1:], a.dtype) for a in xs],
        scratch_shapes=[pltpu.SemaphoreType.DMA((n * ncp,)), pltpu.SemaphoreType.DMA((n * ncp,))])(*xs)


def _rs_ici(ss, accs, layer, name):
    n, ncp = len(ss), len(_CHIP_FLIPS)

    def body(*refs):
        ins, outs = refs[:n], refs[2 * n:3 * n]
        send, recv, loc = refs[3 * n:]
        x, y, c = _coords()
        chip = 2 * x + y
        local = [pltpu.make_async_copy(ins[i].at[chip], outs[i].at[layer, chip], loc.at[i]) for i in range(n)]
        for cp in local:
            cp.start()
        remote = []
        for i in range(n):
            for j, (fx, fy) in enumerate(_CHIP_FLIPS):
                px, py = _flip(x, fx), _flip(y, fy)
                cp = pltpu.make_async_remote_copy(
                    src_ref=ins[i].at[2 * px + py], dst_ref=outs[i].at[layer, chip], send_sem=send.at[i * ncp + j],
                    recv_sem=recv.at[i * ncp + j], device_id=(px, py, c), device_id_type=MESH)
                cp.start()
                remote.append(cp)
        for cp in remote:
            cp.wait()
        for cp in local:
            cp.wait()

    return pl.pallas_call(
        body, name=name, in_specs=[_ANY] * (2 * n), out_specs=[_ANY] * n,
        out_shape=[jax.ShapeDtypeStruct(a.shape, a.dtype) for a in accs],
        input_output_aliases={n + i: i for i in range(n)},
        scratch_shapes=[pltpu.SemaphoreType.DMA((n * ncp,)), pltpu.SemaphoreType.DMA((n * ncp,)),
                        pltpu.SemaphoreType.DMA((n,))])(*ss, *accs)


_PACK_ROWS = 256


def _pack(arrs):
    flat = jnp.concatenate([a.reshape(-1) for a in arrs])
    quantum = _PACK_ROWS * LANES
    total = -(-flat.shape[0] // quantum) * quantum
    return jnp.pad(flat, (0, total - flat.shape[0])).reshape(-1, LANES)


def _unpack(packed, like):
    flat, out, pos = packed.reshape(-1), [], 0
    for a in like:
        out.append(flat[pos:pos + a.size].reshape(a.shape))
        pos += a.size
    return out


def kernel(x, c, ada_w, ada_b, norm1_g, w_in, conv_w, spatial_w, spatial_b, v_norm_g, a_log, dt_bias, o_norm_g, w_branch_a, w_branch_b, w_out, norm2_g, w_ffn_in, w_ffn_out, final_g, loss_target, m_ada_w, m_ada_b, m_norm1_g, m_w_in, m_conv_w, m_spatial_w, m_spatial_b, m_v_norm_g, m_a_log, m_dt_bias, m_o_norm_g, m_w_branch_a, m_w_branch_b, m_w_out, m_norm2_g, m_w_ffn_in, m_w_ffn_out, m_final_g, v_ada_w, v_ada_b, v_norm1_g, v_w_in, v_conv_w, v_spatial_w, v_spatial_b, v_v_norm_g, v_a_log, v_dt_bias, v_o_norm_g, v_w_branch_a, v_w_branch_b, v_w_out, v_norm2_g, v_w_ffn_in, v_w_ffn_out, v_final_g):
    nl, d = ada_w.shape[0], x.shape[2]
    t = x.shape[1]
    nchunk = t // GDN_CHUNK
    xi, yi, ci = _coords()
    me = 4 * xi + 2 * yi + ci
    core = jnp.reshape(ci, (1,)).astype(jnp.int32)
    x0, target = x[0], loss_target[0]
    wcols = 3 * HEADS * HEAD_DIM
    o_uv, o_qkv, o_z, o_ba = 2 * d, 2 * d + wcols, 2 * d + wcols + d, 2 * d + wcols + d + 2 * HEADS

    c_all, cw_all = _a2a_direct([jnp.broadcast_to(c[None], (N_DEV,) + c.shape),
                                 jnp.broadcast_to(conv_w[None], (N_DEV,) + conv_w.shape)], "gather_small")
    c_all = c_all[:, 0]
    conv_full = cw_all.transpose(1, 2, 0, 3).reshape(nl, CONV_K, wcols)
    modp = _ada_fwd(c_all, ada_w, "ada_fwd")
    (modx,) = _a2a_direct([modp.transpose(1, 0, 2)], "mod_exchange")
    mod = (modx.transpose(1, 0, 2).reshape(nl, 6 * d) + ada_b).reshape(nl, 6, 1, d)

    big = (w_in, w_branch_a, w_branch_b, w_out, w_ffn_in, w_ffn_out)
    gathered = _ag_d2d(_ag_ici([w.astype(MXU_DTYPE) for w in big], "ag_ici"), "ag_d2d")
    col_full = lambda g: g.transpose(1, 2, 0, 3).reshape(nl, g.shape[2], -1)
    row_full = lambda g: g.transpose(1, 0, 2, 3).reshape(nl, -1, g.shape[3])
    w_in_f, w_a_f, w_b_f, w_o_f = col_full(gathered[0]), row_full(gathered[1]), row_full(gathered[2]), row_full(gathered[3])
    w_fi_f, w_fo_f = col_full(gathered[4]), row_full(gathered[5])

    def in_segments(i):
        w = w_in_f[i]
        ba = jnp.pad(w[:, o_z:o_ba], ((0, 0), (0, LANES - 2 * HEADS)))
        return w[:, :o_uv], w[:, o_uv:o_qkv], w[:, o_qkv:o_z], ba, w[:, o_ba:]

    def rows_of(ba, lo):
        return ba[:, lo:lo + HEADS].T.reshape(HEADS, nchunk, 1, GDN_CHUNK)

    saved = []
    x_cur, delta, gt_prev = x0, None, None
    for i in range(nl):
        sh1, sc1, gt1, sh2, sc2, gt2 = (mod[i, k] for k in range(6))
        s = dict(gt1=gt1, gt2=gt2, sc1=sc1, sc2=sc2)
        s["seg"] = in_segments(i)
        s["x_in"], s["h"] = _resid_norm(x_cur, delta, gt_prev, norm1_g[i][None], sc1, sh1, "norm1_fwd")
        s["uv"], s["qkv"], s["z"], s["ba"], s["gates"] = (
            _matmul(s["h"], w, "nn", "proj_fwd_%d" % k) for k, w in enumerate(s["seg"]))
        s["b_col"] = spatial_b[i][:, :, None]
        s["ya"] = _mixer_a_fwd(s["uv"], spatial_w[i], s["b_col"], v_norm_g[i][None], "mixer_a_fwd")
        s["qkv_h"] = _conv_fwd(s["qkv"], conv_full[i], "conv_fwd")
        s["braw"], s["araw"] = rows_of(s["ba"], 0), rows_of(s["ba"], HEADS)
        s["o"], s["states"] = _gdn_fwd(s["qkv_h"], s["araw"], s["braw"], a_log[i], dt_bias[i], "gdn_fwd")
        s["yb"] = _gdn_post_fwd(s["o"], s["z"], o_norm_g[i][None], "gdn_post_fwd")
        s["pa"] = _matmul(s["ya"], w_a_f[i], "nn", "branch_a_fwd")
        s["pb"] = _matmul(s["yb"], w_b_f[i], "nn", "branch_b_fwd")
        s["merged"] = _merge_fwd(s["pa"], s["pb"], s["gates"], "merge_fwd")
        s["mo"] = _matmul(s["merged"], w_o_f[i], "nn", "out_fwd")
        s["x1"], s["h2"] = _resid_norm(s["x_in"], s["mo"], gt1, norm2_g[i][None], sc2, sh2, "norm2_fwd")
        s["gu"] = _matmul(s["h2"], w_fi_f[i], "nn", "ffn_in_fwd")
        s["a"] = _swiglu_fwd(s["gu"], "swiglu_fwd")
        s["fo"] = _matmul(s["a"], w_fo_f[i], "nn", "ffn_out_fwd")
        saved.append(s)
        x_cur, delta, gt_prev = s["x1"], s["fo"], gt2
    dx, d_final_g, loss_tile = _final_loss(x_cur, delta, gt_prev, final_g[None], target, "final_loss")
    loss = lax.psum(loss_tile[0, 0], ("x", "y", "c"))

    big_shapes = [(d, w_in.shape[2]), w_branch_a.shape[1:], w_branch_b.shape[1:], w_out.shape[1:],
                  (d, w_ffn_in.shape[2]), w_ffn_out.shape[1:]]
    accs = [jnp.zeros((nl, N_DEV // 2) + tuple(sh), F32) for sh in big_shapes]
    col_blocks = lambda g: g.reshape(g.shape[0], N_DEV, -1).transpose(1, 0, 2)
    row_blocks = lambda g: g.reshape(N_DEV, -1, g.shape[1])
    dmod, small = [None] * nl, [None] * nl
    d_conv = [None] * nl
    for i in reversed(range(nl)):
        s = saved[i]
        dfo, dgt2 = _gate_bwd(dx, s["fo"], s["gt2"], "gate2_bwd")
        g_fo = _matmul(s["a"], dfo, "tn", "ffn_out_dw")
        da = _matmul(dfo, w_fo_f[i], "nt", "ffn_out_dx")
        dgu = _swiglu_bwd(s["gu"], da, "swiglu_bwd")
        g_fi = _matmul(s["h2"], dgu, "tn", "ffn_in_dw")
        dh2 = _matmul(dgu, w_fi_f[i], "nt", "ffn_in_dx")
        dx1, dsh2, dsc2, dg2 = _norm_bwd(s["x1"], dh2, dx, norm2_g[i][None], s["sc2"], "norm2_bwd")
        dmo, dgt1 = _gate_bwd(dx1, s["mo"], s["gt1"], "gate1_bwd")
        g_o = _matmul(s["merged"], dmo, "tn", "out_dw")
        dmerged = _matmul(dmo, w_o_f[i], "nt", "out_dx")
        dpa, dpb, dgates = _merge_bwd(dmerged, s["pa"], s["pb"], s["gates"], "merge_bwd")
        g_a = _matmul(s["ya"], dpa, "tn", "branch_a_dw")
        dya = _matmul(dpa, w_a_f[i], "nt", "branch_a_dx")
        g_b = _matmul(s["yb"], dpb, "tn", "branch_b_dw")
        dyb = _matmul(dpb, w_b_f[i], "nt", "branch_b_dx")
        duv, d_ws, d_bs, d_gv = _mixer_a_bwd(s["uv"], dya, spatial_w[i], jnp.swapaxes(spatial_w[i], 1, 2),
                                             s["b_col"], v_norm_g[i][None], "mixer_a_bwd")
        do, dz, d_go = _gdn_post_bwd(s["o"], s["z"], dyb, o_norm_g[i][None], "gdn_post_bwd")
        dq, dk, dv, d_ar, d_br, d_al, d_dt = _gdn_bwd(s["qkv_h"], s["araw"], s["braw"], a_log[i], dt_bias[i],
                                                      s["states"], do, "gdn_bwd")
        dacc, d_conv[i] = _conv_bwd_pre(s["qkv"], dq, dk, dv, conv_full[i], "conv_bwd_pre")
        dqkv = _conv_bwd_in(dacc, conv_full[i], "conv_bwd_in")
        cols = lambda r: r.reshape(HEADS, t).T
        dba = jnp.pad(jnp.concatenate([cols(d_br), cols(d_ar)], axis=1),
                      ((0, 0), (0, LANES - 2 * HEADS))).astype(MXU_DTYPE)
        dsegs = (duv, dqkv, dz, dba, dgates)
        g_segs = [_matmul(s["h"], dsg, "tn", "proj_dw_%d" % k) for k, dsg in enumerate(dsegs)]
        dh = None
        for k, dsg in enumerate(dsegs):
            dh = _matmul(dsg, s["seg"][k], "nt", "proj_dx_%d" % k, acc=dh)
        dx, dsh1, dsc1, dg1 = _norm_bwd(s["x_in"], dh, dx1, norm1_g[i][None], s["sc1"], "norm1_bwd")
        dmod[i] = jnp.concatenate([dsh1, dsc1, dgt1, dsh2, dsc2, dgt2], axis=1)[0]
        small[i] = (dg1[0], d_ws, d_bs[:, :, 0], d_gv[0], d_al[:, 0, 0], d_dt[:, 0, 0], d_go[0], dg2[0])
        g_in = jnp.concatenate([g_segs[0], g_segs[1], g_segs[2], g_segs[3][:, :2 * HEADS], g_segs[4]], axis=1)
        parts = [col_blocks(g_in), row_blocks(g_a), row_blocks(g_b), row_blocks(g_o), col_blocks(g_fi), row_blocks(g_fo)]
        other = _rs_d2d(parts, "rs_d2d")
        sums = [_pair_sum(p, o, core, "rs_pair_sum_%d" % k) for k, (p, o) in enumerate(zip(parts, other))]
        accs = _rs_ici(sums, accs, i, "rs_ici_%d" % i)

    dmod = jnp.stack(dmod)
    sm = [jnp.stack([small[i][k] for i in range(nl)]) for k in range(8)]
    rep_w = (ada_b, norm1_g, spatial_w, spatial_b, v_norm_g, a_log, dt_bias, o_norm_g, norm2_g, final_g)
    rep_m = (m_ada_b, m_norm1_g, m_spatial_w, m_spatial_b, m_v_norm_g, m_a_log, m_dt_bias, m_o_norm_g, m_norm2_g, m_final_g)
    rep_v = (v_ada_b, v_norm1_g, v_spatial_w, v_spatial_b, v_v_norm_g, v_a_log, v_dt_bias, v_o_norm_g, v_norm2_g, v_final_g)
    rep_g = (dmod, sm[0], sm[1], sm[2], sm[3], sm[4], sm[5], sm[6], sm[7], d_final_g[0])
    packed = _pack(rep_g)
    d_conv_blocks = jnp.stack(d_conv).reshape(nl, CONV_K, N_DEV, -1).transpose(2, 0, 1, 3).reshape(N_DEV, -1, LANES)
    dmod_blocks = dmod.reshape(nl, N_DEV, -1).transpose(1, 0, 2)
    rep_all, conv_all, dmod_all = _a2a_direct(
        [jnp.broadcast_to(packed[None], (N_DEV,) + packed.shape), d_conv_blocks, dmod_blocks], "small_grads")
    rep_out = _sum_adam(rep_all[None], _pack(rep_w)[None], _pack(rep_m)[None], _pack(rep_v)[None], "adam_small")
    rep_out = [_unpack(o[0], rep_w) for o in rep_out]
    conv_out = _sum_adam(conv_all[None], conv_w.reshape(1, -1, LANES), m_conv_w.reshape(1, -1, LANES),
                         v_conv_w.reshape(1, -1, LANES), "adam_conv")
    conv_out = [o.reshape(conv_w.shape) for o in conv_out]
    ada_out = _ada_bwd(c_all[:, :, None], dmod_all.transpose(1, 0, 2), ada_w, m_ada_w, v_ada_w, "ada_bwd_adam")
    big_m = (m_w_in, m_w_branch_a, m_w_branch_b, m_w_out, m_w_ffn_in, m_w_ffn_out)
    big_v = (v_w_in, v_w_branch_a, v_w_branch_b, v_w_out, v_w_ffn_in, v_w_ffn_out)
    big_out = [_sum_adam(accs[k], big[k], big_m[k], big_v[k], "adam_big_%d" % k) for k in range(6)]

    def ordered(kind):
        rep = rep_out[kind]
        return (ada_out[kind], rep[0], rep[1], big_out[0][kind], conv_out[kind], rep[2], rep[3], rep[4], rep[5],
                rep[6], rep[7], big_out[1][kind], big_out[2][kind], big_out[3][kind], rep[8], big_out[4][kind],
                big_out[5][kind], rep[9])

    return (loss, dx[None]) + ordered(0) + ordered(1) + ordered(2) + ordered(3)
```

```python
import functools

import jax
import jax.numpy as jnp
from jax import lax
from jax.experimental import pallas as pl
from jax.experimental.pallas import tpu as pltpu

F32 = jnp.float32
BF16 = jnp.bfloat16
MXU_DTYPE = BF16
EPS = 1e-6
LANES = 128
SUBLANES = 8
GDN_CHUNK = 128
A_CHUNK = 128
GROUPS = 8
HEADS = 8
HEAD_DIM = 128
CONV_K = 4
N_DEV = 8
VMEM_LIMIT = 48 * 1024 * 1024
MESH = pl.DeviceIdType.MESH

ADAM_LR = 0.001
ADAM_B1 = 0.9
ADAM_B2 = 0.999
ADAM_EPS = 1e-08
ADAM_WD = 0.01
ADAM_STEP = 10

_NN = (((1,), (0,)), ((), ()))
_NT = (((1,), (1,)), ((), ()))
_TN = (((0,), (0,)), ((), ()))


def _mm(a, b, dims=_NN):
    return lax.dot_general(a.astype(MXU_DTYPE), b.astype(MXU_DTYPE), dims, preferred_element_type=F32)


def _mm_hi(a, b):
    return lax.dot_general(a, b, _NN, precision=lax.Precision.HIGHEST, preferred_element_type=F32)


def _tile(n, cands):
    for c in cands:
        if n % c == 0:
            return c
    return n


def _params(sem=None):
    return pltpu.CompilerParams(dimension_semantics=sem, vmem_limit_bytes=VMEM_LIMIT)


def _sigmoid(x):
    return 1.0 / (1.0 + jnp.exp(-x))


def _silu(x):
    return x * _sigmoid(x)


def _dsilu(x):
    s = _sigmoid(x)
    return s * (1.0 + x * (1.0 - s))


_GELU_C = 0.7978845608028654
_GELU_A = 0.044715


def _gelu(x):
    return 0.5 * x * (1.0 + jnp.tanh(_GELU_C * (x + _GELU_A * x * x * x)))


def _dgelu(x):
    t = jnp.tanh(_GELU_C * (x + _GELU_A * x * x * x))
    return 0.5 * (1.0 + t) + 0.5 * x * (1.0 - t * t) * _GELU_C * (1.0 + 3.0 * _GELU_A * x * x)


def _softplus(x):
    return jnp.maximum(x, 0.0) + jnp.log(1.0 + jnp.exp(-jnp.abs(x)))


_MM_TILES = (1024, 1408, 512, 256, 128)


def _matmul(a, b, mode, name, out_dtype=F32, acc=None):
    if mode == "nn":
        (m, k), n = a.shape, b.shape[1]
    elif mode == "nt":
        (m, k), n = a.shape, b.shape[0]
    else:
        (k, m), n = a.shape, b.shape[1]
    tm, tn, tk = _tile(m, _MM_TILES), _tile(n, _MM_TILES), _tile(k, _MM_TILES)
    nk = k // tk
    dims = {"nn": _NN, "nt": _NT, "tn": _TN}[mode]
    has_acc = acc is not None

    def body(*refs):
        if has_acc:
            a_ref, b_ref, c_ref, o_ref, acc_ref = refs
        else:
            a_ref, b_ref, o_ref, acc_ref = refs
        kk = pl.program_id(2)

        @pl.when(kk == 0)
        def _():
            acc_ref[...] = c_ref[...] if has_acc else jnp.zeros_like(acc_ref)

        acc_ref[...] += _mm(a_ref[...], b_ref[...], dims)

        @pl.when(kk == nk - 1)
        def _():
            o_ref[...] = acc_ref[...].astype(o_ref.dtype)

    a_spec = (pl.BlockSpec((tk, tm), lambda i, j, l: (l, i)) if mode == "tn"
              else pl.BlockSpec((tm, tk), lambda i, j, l: (i, l)))
    b_spec = (pl.BlockSpec((tn, tk), lambda i, j, l: (j, l)) if mode == "nt"
              else pl.BlockSpec((tk, tn), lambda i, j, l: (l, j)))
    o_spec = pl.BlockSpec((tm, tn), lambda i, j, l: (i, j))
    in_specs = [a_spec, b_spec] + ([o_spec] if has_acc else [])
    args = (a, b) + ((acc,) if has_acc else ())
    return pl.pallas_call(
        body, name=name, grid=(m // tm, n // tn, nk), in_specs=in_specs, out_specs=o_spec,
        out_shape=jax.ShapeDtypeStruct((m, n), out_dtype), scratch_shapes=[pltpu.VMEM((tm, tn), F32)],
        input_output_aliases=({2: 0} if has_acc else {}),
        compiler_params=_params(("parallel", "parallel", "arbitrary")))(*args)


_ROW_TILES = (512, 256, 128)


def _resid_norm(x, delta, gt, g, sc, sh, name):
    t, d = x.shape
    tt = _tile(t, _ROW_TILES)
    has = delta is not None

    def body(*refs):
        if has:
            x_ref, d_ref, gt_ref, g_ref, sc_ref, sh_ref, xo_ref, h_ref = refs
            xv = x_ref[...] + gt_ref[...] * d_ref[...]
            xo_ref[...] = xv
        else:
            x_ref, g_ref, sc_ref, sh_ref, h_ref = refs
            xv = x_ref[...]
        r = lax.rsqrt(jnp.mean(xv * xv, axis=-1, keepdims=True) + EPS)
        y = xv * r * g_ref[...]
        h_ref[...] = (y * (1.0 + sc_ref[...]) + sh_ref[...]).astype(h_ref.dtype)

    row = pl.BlockSpec((tt, d), lambda i: (i, 0))
    vec = pl.BlockSpec((1, d), lambda i: (0, 0))
    if has:
        return pl.pallas_call(
            body, name=name, grid=(t // tt,), in_specs=[row, row, vec, vec, vec, vec], out_specs=[row, row],
            out_shape=[jax.ShapeDtypeStruct((t, d), F32), jax.ShapeDtypeStruct((t, d), MXU_DTYPE)],
            compiler_params=_params(("parallel",)))(x, delta, gt, g, sc, sh)
    h = pl.pallas_call(
        body, name=name + "_first", grid=(t // tt,), in_specs=[row, vec, vec, vec], out_specs=row,
        out_shape=jax.ShapeDtypeStruct((t, d), MXU_DTYPE), compiler_params=_params(("parallel",)))(x, g, sc, sh)
    return x, h


def _final_loss(x, delta, gt, g, target, name):
    t, d = x.shape
    tt = _tile(t, _ROW_TILES)

    def body(x_ref, d_ref, gt_ref, g_ref, tg_ref, dx_ref, dg_ref, loss_ref):
        @pl.when(pl.program_id(0) == 0)
        def _():
            dg_ref[...] = jnp.zeros_like(dg_ref)
            loss_ref[...] = jnp.zeros_like(loss_ref)

        xv = x_ref[...] + gt_ref[...] * d_ref[...]
        r = lax.rsqrt(jnp.mean(xv * xv, axis=-1, keepdims=True) + EPS)
        xh = xv * r
        diff = xh * g_ref[...] - tg_ref[...]
        loss_ref[...] += jnp.sum(diff * diff) * (0.5 / d)
        dy = diff * (1.0 / d)
        dg_ref[...] += jnp.sum(dy * xh, axis=0, keepdims=True)
        dxh = dy * g_ref[...]
        dx_ref[...] = r * (dxh - xh * jnp.mean(dxh * xh, axis=-1, keepdims=True))

    row = pl.BlockSpec((tt, d), lambda i: (i, 0))
    vec = pl.BlockSpec((1, d), lambda i: (0, 0))
    tile = pl.BlockSpec((SUBLANES, LANES), lambda i: (0, 0))
    return pl.pallas_call(
        body, name=name, grid=(t // tt,), in_specs=[row, row, vec, vec, row], out_specs=[row, vec, tile],
        out_shape=[jax.ShapeDtypeStruct((t, d), F32), jax.ShapeDtypeStruct((1, d), F32),
                   jax.ShapeDtypeStruct((SUBLANES, LANES), F32)],
        compiler_params=_params(("arbitrary",)))(x, delta, gt, g, target)


def _norm_bwd(x, dh, dres, g, sc, name):
    t, d = x.shape
    tt = _tile(t, _ROW_TILES)

    def body(x_ref, dh_ref, dr_ref, g_ref, sc_ref, dx_ref, dsh_ref, dsc_ref, dg_ref):
        @pl.when(pl.program_id(0) == 0)
        def _():
            dsh_ref[...] = jnp.zeros_like(dsh_ref)
            dsc_ref[...] = jnp.zeros_like(dsc_ref)
            dg_ref[...] = jnp.zeros_like(dg_ref)

        xv, dh = x_ref[...], dh_ref[...]
        r = lax.rsqrt(jnp.mean(xv * xv, axis=-1, keepdims=True) + EPS)
        xh = xv * r
        gv, sc1 = g_ref[...], 1.0 + sc_ref[...]
        dsh_ref[...] += jnp.sum(dh, axis=0, keepdims=True)
        dsc_ref[...] += jnp.sum(dh * xh, axis=0, keepdims=True) * gv
        dg_ref[...] += jnp.sum(dh * xh, axis=0, keepdims=True) * sc1
        dxh = dh * (gv * sc1)
        dx_ref[...] = dr_ref[...] + r * (dxh - xh * jnp.mean(dxh * xh, axis=-1, keepdims=True))

    row = pl.BlockSpec((tt, d), lambda i: (i, 0))
    vec = pl.BlockSpec((1, d), lambda i: (0, 0))
    vshape = jax.ShapeDtypeStruct((1, d), F32)
    return pl.pallas_call(
        body, name=name, grid=(t // tt,), in_specs=[row, row, row, vec, vec], out_specs=[row, vec, vec, vec],
        out_shape=[jax.ShapeDtypeStruct((t, d), F32), vshape, vshape, vshape],
        compiler_params=_params(("arbitrary",)))(x, dh, dres, g, sc)


def _gate_bwd(dxo, branch, gt, name):
    t, d = dxo.shape
    tt = _tile(t, _ROW_TILES)

    def body(dx_ref, br_ref, gt_ref, db_ref, dgt_ref):
        @pl.when(pl.program_id(0) == 0)
        def _():
            dgt_ref[...] = jnp.zeros_like(dgt_ref)

        dx = dx_ref[...]
        db_ref[...] = (dx * gt_ref[...]).astype(db_ref.dtype)
        dgt_ref[...] += jnp.sum(dx * br_ref[...], axis=0, keepdims=True)

    row = pl.BlockSpec((tt, d), lambda i: (i, 0))
    vec = pl.BlockSpec((1, d), lambda i: (0, 0))
    return pl.pallas_call(
        body, name=name, grid=(t // tt,), in_specs=[row, row, vec], out_specs=[row, vec],
        out_shape=[jax.ShapeDtypeStruct((t, d), MXU_DTYPE), jax.ShapeDtypeStruct((1, d), F32)],
        compiler_params=_params(("arbitrary",)))(dxo, branch, gt)


def _swiglu_fwd(gu, name):
    t, f2 = gu.shape
    f = f2 // 2
    tt = _tile(t, (256, 128))

    def body(g_ref, u_ref, o_ref):
        o_ref[...] = (_silu(g_ref[...]) * u_ref[...]).astype(o_ref.dtype)

    return pl.pallas_call(
        body, name=name, grid=(t // tt,),
        in_specs=[pl.BlockSpec((tt, f), lambda i: (i, 0)), pl.BlockSpec((tt, f), lambda i: (i, 1))],
        out_specs=pl.BlockSpec((tt, f), lambda i: (i, 0)), out_shape=jax.ShapeDtypeStruct((t, f), MXU_DTYPE),
        compiler_params=_params(("parallel",)))(gu, gu)


def _swiglu_bwd(gu, da, name):
    t, f2 = gu.shape
    f = f2 // 2
    tt = _tile(t, (256, 128))

    def body(g_ref, u_ref, da_ref, o_ref):
        gate, da = g_ref[...], da_ref[...]
        o_ref[:, :f] = (da * u_ref[...] * _dsilu(gate)).astype(o_ref.dtype)
        o_ref[:, f:] = (da * _silu(gate)).astype(o_ref.dtype)

    return pl.pallas_call(
        body, name=name, grid=(t // tt,),
        in_specs=[pl.BlockSpec((tt, f), lambda i: (i, 0)), pl.BlockSpec((tt, f), lambda i: (i, 1)),
                  pl.BlockSpec((tt, f), lambda i: (i, 0))],
        out_specs=pl.BlockSpec((tt, f2), lambda i: (i, 0)), out_shape=jax.ShapeDtypeStruct((t, f2), MXU_DTYPE),
        compiler_params=_params(("parallel",)))(gu, gu, da)


def _merge_fwd(pa, pb, gates, name):
    t, d = pa.shape
    tt = _tile(t, _ROW_TILES)

    def body(pa_ref, pb_ref, ga_ref, gb_ref, o_ref):
        o_ref[...] = (_sigmoid(ga_ref[...]) * pa_ref[...] + _sigmoid(gb_ref[...]) * pb_ref[...]).astype(o_ref.dtype)

    row = pl.BlockSpec((tt, d), lambda i: (i, 0))
    row1 = pl.BlockSpec((tt, d), lambda i: (i, 1))
    return pl.pallas_call(
        body, name=name, grid=(t // tt,), in_specs=[row, row, row, row1], out_specs=row,
        out_shape=jax.ShapeDtypeStruct((t, d), MXU_DTYPE), compiler_params=_params(("parallel",)))(pa, pb, gates, gates)


def _merge_bwd(dm, pa, pb, gates, name):
    t, d = pa.shape
    tt = _tile(t, _ROW_TILES)

    def body(dm_ref, pa_ref, pb_ref, ga_ref, gb_ref, dpa_ref, dpb_ref, dg_ref):
        dm = dm_ref[...]
        sa, sb = _sigmoid(ga_ref[...]), _sigmoid(gb_ref[...])
        dpa_ref[...] = (dm * sa).astype(dpa_ref.dtype)
        dpb_ref[...] = (dm * sb).astype(dpb_ref.dtype)
        dg_ref[:, :d] = (dm * pa_ref[...] * sa * (1.0 - sa)).astype(dg_ref.dtype)
        dg_ref[:, d:] = (dm * pb_ref[...] * sb * (1.0 - sb)).astype(dg_ref.dtype)

    row = pl.BlockSpec((tt, d), lambda i: (i, 0))
    row1 = pl.BlockSpec((tt, d), lambda i: (i, 1))
    wide = pl.BlockSpec((tt, 2 * d), lambda i: (i, 0))
    return pl.pallas_call(
        body, name=name, grid=(t // tt,), in_specs=[row, row, row, row, row1], out_specs=[row, row, wide],
        out_shape=[jax.ShapeDtypeStruct((t, d), MXU_DTYPE), jax.ShapeDtypeStruct((t, d), MXU_DTYPE),
                   jax.ShapeDtypeStruct((t, 2 * d), MXU_DTYPE)],
        compiler_params=_params(("parallel",)))(dm, pa, pb, gates, gates)


def _tri_masks(n):
    ri = lax.broadcasted_iota(jnp.int32, (n, n), 0)
    ci = lax.broadcasted_iota(jnp.int32, (n, n), 1)
    return ri >= ci, ri > ci, ri == ci


def _mixer_a_fwd(uv, w_s, b_col, g_v, name):
    t, w2 = uv.shape
    w = w2 // 2
    c = A_CHUNK

    def body(u_ref, v_ref, w_ref, b_ref, gv_ref, y_ref):
        tril, _, _ = _tri_masks(c)
        ug, vg = _gelu(u_ref[...]), _gelu(v_ref[...])
        for g in range(GROUPS):
            sl = slice(g * c, (g + 1) * c)
            vt = vg[:, sl]
            r = lax.rsqrt(jnp.mean(vt * vt, axis=-1, keepdims=True) + EPS)
            vn = vt * r * gv_ref[:, sl]
            s = _mm(jnp.where(tril, w_ref[g], 0.0), vn) + b_ref[g]
            y_ref[:, sl] = (ug[:, sl] * s).astype(y_ref.dtype)

    return pl.pallas_call(
        body, name=name, grid=(t // c,),
        in_specs=[pl.BlockSpec((c, w), lambda i: (i, 0)), pl.BlockSpec((c, w), lambda i: (i, 1)),
                  pl.BlockSpec((GROUPS, c, c), lambda i: (0, 0, 0)), pl.BlockSpec((GROUPS, c, 1), lambda i: (0, 0, 0)),
                  pl.BlockSpec((1, w), lambda i: (0, 0))],
        out_specs=pl.BlockSpec((c, w), lambda i: (i, 0)), out_shape=jax.ShapeDtypeStruct((t, w), MXU_DTYPE),
        compiler_params=_params(("parallel",)))(uv, uv, w_s, b_col, g_v)


def _mixer_a_bwd(uv, dy, w_s, w_st, b_col, g_v, name):
    t, w2 = uv.shape
    w = w2 // 2
    c = A_CHUNK

    def body(u_ref, v_ref, dy_ref, w_ref, wt_ref, b_ref, gv_ref, duv_ref, dw_ref, db_ref, dgv_ref):
        @pl.when(pl.program_id(0) == 0)
        def _():
            dw_ref[...] = jnp.zeros_like(dw_ref)
            db_ref[...] = jnp.zeros_like(db_ref)
            dgv_ref[...] = jnp.zeros_like(dgv_ref)

        tril, _, _ = _tri_masks(c)
        triu = lax.broadcasted_iota(jnp.int32, (c, c), 0) <= lax.broadcasted_iota(jnp.int32, (c, c), 1)
        up, vp = u_ref[...], v_ref[...]
        ug, vg = _gelu(up), _gelu(vp)
        for g in range(GROUPS):
            sl = slice(g * c, (g + 1) * c)
            vt = vg[:, sl]
            r = lax.rsqrt(jnp.mean(vt * vt, axis=-1, keepdims=True) + EPS)
            vh = vt * r
            gv = gv_ref[:, sl]
            vn = vh * gv
            s = _mm(jnp.where(tril, w_ref[g], 0.0), vn) + b_ref[g]
            dy = dy_ref[:, sl]
            ds = dy * ug[:, sl]
            dw_ref[g] += jnp.where(tril, _mm(ds, vn, _NT), 0.0)
            db_ref[g] += jnp.sum(ds, axis=1, keepdims=True)
            dvn = _mm(jnp.where(triu, wt_ref[g], 0.0), ds)
            dgv_ref[:, sl] += jnp.sum(dvn * vh, axis=0, keepdims=True)
            dvh = dvn * gv
            dvt = r * (dvh - vh * jnp.mean(dvh * vh, axis=-1, keepdims=True))
            duv_ref[:, sl] = (dy * s * _dgelu(up[:, sl])).astype(duv_ref.dtype)
            duv_ref[:, w + g * c:w + (g + 1) * c] = (dvt * _dgelu(vp[:, sl])).astype(duv_ref.dtype)

    full3 = lambda shape: pl.BlockSpec(shape, lambda i: (0, 0, 0))
    return pl.pallas_call(
        body, name=name, grid=(t // c,),
        in_specs=[pl.BlockSpec((c, w), lambda i: (i, 0)), pl.BlockSpec((c, w), lambda i: (i, 1)),
                  pl.BlockSpec((c, w), lambda i: (i, 0)), full3((GROUPS, c, c)), full3((GROUPS, c, c)),
                  full3((GROUPS, c, 1)), pl.BlockSpec((1, w), lambda i: (0, 0))],
        out_specs=[pl.BlockSpec((c, w2), lambda i: (i, 0)), full3((GROUPS, c, c)), full3((GROUPS, c, 1)),
                   pl.BlockSpec((1, w), lambda i: (0, 0))],
        out_shape=[jax.ShapeDtypeStruct((t, w2), MXU_DTYPE), jax.ShapeDtypeStruct((GROUPS, c, c), F32),
                   jax.ShapeDtypeStruct((GROUPS, c, 1), F32), jax.ShapeDtypeStruct((1, w), F32)],
        compiler_params=_params(("arbitrary",)))(uv, uv, dy, w_s, w_st, b_col, g_v)


_Q_SCALE = HEAD_DIM ** -0.5


def _conv_taps(ext, w_ref):
    shifted = [ext[SUBLANES:]] + [pltpu.roll(ext, s, 0)[SUBLANES:] for s in range(1, CONV_K)]
    acc = shifted[0] * w_ref[pl.ds(CONV_K - 1, 1), :]
    for s in range(1, CONV_K):
        acc = acc + shifted[s] * w_ref[pl.ds(CONV_K - 1 - s, 1), :]
    return acc, shifted


def _conv_fwd(qkv, w, name):
    t, cw = qkv.shape
    tt = _tile(t, (256, 128))
    hb = tt // SUBLANES

    def body(x_ref, p_ref, w_ref, o_ref):
        prev = jnp.where(pl.program_id(0) > 0, p_ref[...], 0.0)
        acc, _ = _conv_taps(jnp.concatenate([prev, x_ref[...]], axis=0), w_ref)
        y = _silu(acc)
        for which in range(3):
            for h in range(HEADS):
                lo = (which * HEADS + h) * HEAD_DIM
                seg = y[:, lo:lo + HEAD_DIM]
                if which < 2:
                    seg = seg * lax.rsqrt(jnp.sum(seg * seg, axis=-1, keepdims=True) + EPS)
                if which == 0:
                    seg = seg * _Q_SCALE
                o_ref[which, h] = seg

    return pl.pallas_call(
        body, name=name, grid=(t // tt,),
        in_specs=[pl.BlockSpec((tt, cw), lambda i: (i, 0)),
                  pl.BlockSpec((SUBLANES, cw), lambda i: (jnp.maximum(i * hb - 1, 0), 0)),
                  pl.BlockSpec((CONV_K, cw), lambda i: (0, 0))],
        out_specs=pl.BlockSpec((3, HEADS, tt, HEAD_DIM), lambda i: (0, 0, i, 0)),
        out_shape=jax.ShapeDtypeStruct((3, HEADS, t, HEAD_DIM), F32),
        compiler_params=_params(("parallel",)))(qkv, qkv, w)


def _conv_bwd_pre(qkv, dq, dk, dv, w, name):
    t, cw = qkv.shape
    tt = _tile(t, (256, 128))
    hb = tt // SUBLANES

    def body(x_ref, p_ref, dq_ref, dk_ref, dv_ref, w_ref, da_ref, dw_ref):
        @pl.when(pl.program_id(0) == 0)
        def _():
            dw_ref[...] = jnp.zeros_like(dw_ref)

        prev = jnp.where(pl.program_id(0) > 0, p_ref[...], 0.0)
        acc, shifted = _conv_taps(jnp.concatenate([prev, x_ref[...]], axis=0), w_ref)
        y = _silu(acc)
        d_refs = (dq_ref, dk_ref, dv_ref)
        for which in range(3):
            for h in range(HEADS):
                lo = (which * HEADS + h) * HEAD_DIM
                sl = slice(lo, lo + HEAD_DIM)
                dn = d_refs[which][h]
                if which < 2:
                    seg = y[:, sl]
                    rho = lax.rsqrt(jnp.sum(seg * seg, axis=-1, keepdims=True) + EPS)
                    nrm = seg * rho
                    if which == 0:
                        dn = dn * _Q_SCALE
                    dn = rho * (dn - nrm * jnp.sum(dn * nrm, axis=-1, keepdims=True))
                dacc = dn * _dsilu(acc[:, sl])
                da_ref[:, sl] = dacc
                for s in range(CONV_K):
                    dw_ref[pl.ds(CONV_K - 1 - s, 1), sl] += jnp.sum(dacc * shifted[s][:, sl], axis=0, keepdims=True)

    head = pl.BlockSpec((HEADS, tt, HEAD_DIM), lambda i: (0, i, 0))
    return pl.pallas_call(
        body, name=name, grid=(t // tt,),
        in_specs=[pl.BlockSpec((tt, cw), lambda i: (i, 0)),
                  pl.BlockSpec((SUBLANES, cw), lambda i: (jnp.maximum(i * hb - 1, 0), 0)),
                  head, head, head, pl.BlockSpec((CONV_K, cw), lambda i: (0, 0))],
        out_specs=[pl.BlockSpec((tt, cw), lambda i: (i, 0)), pl.BlockSpec((CONV_K, cw), lambda i: (0, 0))],
        out_shape=[jax.ShapeDtypeStruct((t, cw), F32), jax.ShapeDtypeStruct((CONV_K, cw), F32)],
        compiler_params=_params(("arbitrary",)))(qkv, qkv, dq, dk, dv, w)


def _conv_bwd_in(dacc, w, name):
    t, cw = dacc.shape
    tt = _tile(t, (256, 128))
    hb = tt // SUBLANES
    nt = t // tt
    rows = tt + SUBLANES

    def body(d_ref, n_ref, w_ref, o_ref):
        cur = d_ref[...]
        nxt = jnp.where(pl.program_id(0) < nt - 1, n_ref[...], 0.0)
        ext = jnp.concatenate([cur, nxt], axis=0)
        acc = cur * w_ref[pl.ds(CONV_K - 1, 1), :]
        for s in range(1, CONV_K):
            acc = acc + pltpu.roll(ext, rows - s, 0)[:tt] * w_ref[pl.ds(CONV_K - 1 - s, 1), :]
        o_ref[...] = acc.astype(o_ref.dtype)

    return pl.pallas_call(
        body, name=name, grid=(nt,),
        in_specs=[pl.BlockSpec((tt, cw), lambda i: (i, 0)),
                  pl.BlockSpec((SUBLANES, cw), lambda i: (jnp.minimum((i + 1) * hb, t // SUBLANES - 1), 0)),
                  pl.BlockSpec((CONV_K, cw), lambda i: (0, 0))],
        out_specs=pl.BlockSpec((tt, cw), lambda i: (i, 0)), out_shape=jax.ShapeDtypeStruct((t, cw), MXU_DTYPE),
        compiler_params=_params(("parallel",)))(dacc, dacc, w)


_INV_BASE_SHIFT = 3


def _inv_unit_lower(a, eye):
    c = GDN_CHUNK
    ri = lax.broadcasted_iota(jnp.int32, (c, c), 0)
    ci = lax.broadcasted_iota(jnp.int32, (c, c), 1)
    same = lambda sh: (ri >> sh) == (ci >> sh)
    x = jnp.where(same(_INV_BASE_SHIFT), -a, 0.0)
    p = jnp.where(eye, 1.0, 0.0) + x
    xs = _split(x)
    x2 = _mm3(xs, xs)
    x2s, ps = _split(x2), _split(p)
    r = _mm3(x2s, tuple(jnp.concatenate([u, v], axis=-1) for u, v in zip(x2s, ps)))
    x4, p = r[..., :c], p + r[..., c:]
    p = p + _mm3(_split(x4), _split(p))
    for sh in range(_INV_BASE_SHIFT, c.bit_length() - 1):
        off = jnp.where(same(sh + 1) & jnp.logical_not(same(sh)), a, 0.0)
        ps = _split(p)
        p = p - _mm3(ps, _split(_mm3(_split(off), ps)))
    return p


def _split(a):
    hi = a.astype(BF16)
    return hi, (a - hi.astype(F32)).astype(BF16)


def _dot_heads(u, v, dims):
    if u.ndim == 3:
        return jnp.stack([_dot_heads(u[j], v[j], dims) for j in range(u.shape[0])])
    return lax.dot_general(u, v, dims, preferred_element_type=F32)


def _mm3(a, b):
    return _dot_heads(a[0], b[0], _NN) + (_dot_heads(a[0], b[1], _NN) + _dot_heads(a[1], b[0], _NN))


def _hmm(a, b, dims=_NN):
    return _dot_heads(a.astype(MXU_DTYPE), b.astype(MXU_DTYPE), dims)


def _rowsum(x):
    return jnp.sum(x, axis=-1, keepdims=True)


def _colsum(x):
    return jnp.sum(x, axis=-2, keepdims=True)


class _Pre:
    pass


def _gdn_pre(q, k, v, araw, braw, alog, dtb):
    c = GDN_CHUNK
    p = _Pre()
    p.tril, p.strict, p.eye = _tri_masks(c)
    p.to_col = lambda row: _rowsum(jnp.where(p.eye, row, 0.0))
    p.to_row = lambda col: _colsum(jnp.where(p.eye, col, 0.0))
    p.a_neg = -jnp.exp(alog + jnp.zeros((1, c), F32))
    p.xg = araw + dtb
    p.g_row = p.a_neg * _softplus(p.xg)
    p.beta_row = _sigmoid(braw)
    p.beta = p.to_col(p.beta_row)
    gam = _rowsum(jnp.where(p.tril, p.g_row, 0.0))
    gam_last = _rowsum(p.g_row)
    p.dm = jnp.where(p.tril, jnp.exp(jnp.where(p.tril, gam - p.to_row(gam), 0.0)), 0.0)
    p.e, p.ek, p.el = jnp.exp(gam), jnp.exp(gam_last - gam), jnp.exp(gam_last)
    p.kb = k * p.beta
    p.kk = _hmm(p.kb, k, _NT)
    p.t = _inv_unit_lower(jnp.where(p.strict, p.kk * p.dm, 0.0), p.eye)
    p.vb, p.kbe = v * p.beta, p.kb * p.e
    p.u, p.w = _hmm(p.t, p.vb), _hmm(p.t, p.kbe)
    p.qk0 = _hmm(q, k, _NT)
    p.qk = p.qk0 * p.dm
    p.qd, p.kd = q * p.e, k * p.ek
    return p


GDN_HEADS_PER_STEP = 8


def _head_scalars(ref, hb):
    h0 = pl.program_id(0) * hb
    return jnp.stack([jnp.full((1, 1), ref[h0 + j], F32) for j in range(hb)])


def _gdn_specs(n, reverse):
    c, dk, hb = GDN_CHUNK, HEAD_DIM, GDN_HEADS_PER_STEP
    ix = (lambda i: n - 1 - i) if reverse else (lambda i: i)
    smem = pl.BlockSpec(memory_space=pltpu.SMEM)
    qkv = [pl.BlockSpec((None, hb, c, dk), functools.partial(lambda w, h, i: (w, h, ix(i), 0), w)) for w in range(3)]
    row = pl.BlockSpec((hb, None, 1, c), lambda h, i: (h, ix(i), 0, 0))
    tok = pl.BlockSpec((hb, c, dk), lambda h, i: (h, ix(i), 0))
    state = pl.BlockSpec((hb, None, dk, dk), lambda h, i: (h, ix(i), 0, 0))
    return smem, qkv, row, tok, state


def _gdn_fwd(qkv_h, araw, braw, alog, dtb, name):
    _, hh, t, dk = qkv_h.shape
    n, hb = t // GDN_CHUNK, GDN_HEADS_PER_STEP
    smem, qkv, row, tok, state = _gdn_specs(n, False)

    def body(alog_ref, dt_ref, q_ref, k_ref, v_ref, a_ref, b_ref, o_ref, so_ref, s_ref):
        @pl.when(pl.program_id(1) == 0)
        def _():
            s_ref[...] = jnp.zeros_like(s_ref)

        p = _gdn_pre(q_ref[...], k_ref[...], v_ref[...], a_ref[...], b_ref[...],
                     _head_scalars(alog_ref, hb), _head_scalars(dt_ref, hb))
        s = s_ref[...]
        vn = p.u - _hmm(p.w, s)
        o_ref[...] = _hmm(p.qd, s) + _hmm(p.qk, vn)
        so_ref[...] = s
        s_ref[...] = s * p.el + _hmm(p.kd, vn, _TN)

    return pl.pallas_call(
        body, name=name, grid=(hh // hb, n), in_specs=[smem, smem] + qkv + [row, row], out_specs=[tok, state],
        out_shape=[jax.ShapeDtypeStruct((hh, t, dk), F32), jax.ShapeDtypeStruct((hh, n, dk, dk), F32)],
        scratch_shapes=[pltpu.VMEM((hb, dk, dk), F32)],
        compiler_params=_params(("parallel", "arbitrary")))(alog, dtb, qkv_h, qkv_h, qkv_h, araw, braw)


def _gdn_bwd(qkv_h, araw, braw, alog, dtb, states, do, name):
    _, hh, t, dk = qkv_h.shape
    c, hb = GDN_CHUNK, GDN_HEADS_PER_STEP
    n = t // c
    smem, qkv, row, tok, state = _gdn_specs(n, True)
    acc = pl.BlockSpec((hb, 1, LANES), lambda h, i: (h, 0, 0))

    def body(alog_ref, dt_ref, q_ref, k_ref, v_ref, a_ref, b_ref, s_ref, do_ref,
             dq_ref, dk_ref, dv_ref, da_ref, db_ref, dal_ref, ddt_ref, ds_ref):
        @pl.when(pl.program_id(1) == 0)
        def _():
            ds_ref[...] = jnp.zeros_like(ds_ref)
            dal_ref[...] = jnp.zeros_like(dal_ref)
            ddt_ref[...] = jnp.zeros_like(ddt_ref)

        q, k, v = q_ref[...], k_ref[...], v_ref[...]
        p = _gdn_pre(q, k, v, a_ref[...], b_ref[...], _head_scalars(alog_ref, hb), _head_scalars(dt_ref, hb))
        s, do, dsp = s_ref[...], do_ref[...], ds_ref[...]
        vn = p.u - _hmm(p.w, s)
        dqd = _hmm(do, s, _NT)
        dqk = _hmm(do, vn, _NT)
        dvn = _hmm(p.qk, do, _TN) + _hmm(p.kd, dsp)
        dkd = _hmm(vn, dsp, _NT)
        d_el = _colsum(_rowsum(s * dsp))
        ds_ref[...] = dsp * p.el + _hmm(p.qd, do, _TN) - _hmm(p.w, dvn, _TN)
        dw = -_hmm(dvn, s, _NT)
        d_t = _hmm(dvn, p.vb, _NT) + _hmm(dw, p.kbe, _NT)
        dvb, dkbe = _hmm(p.t, dvn, _TN), _hmm(p.t, dw, _TN)
        d_a = jnp.where(p.strict, -_hmm(p.t, _hmm(d_t, p.t, _NT), _TN), 0.0)
        dkk = d_a * p.dm
        dqk0 = dqk * p.dm
        ddm = d_a * p.kk + dqk * p.qk0
        dkb = _hmm(dkk, k) + dkbe * p.e
        dq_ref[...] = _hmm(dqk0, k) + dqd * p.e
        dk_ref[...] = _hmm(dkk, p.kb, _TN) + _hmm(dqk0, q, _TN) + dkd * p.ek + dkb * p.beta
        dv_ref[...] = dvb * p.beta
        dbeta = _rowsum(dkb * k) + _rowsum(dvb * v)
        d_e = _rowsum(dqd * q) + _rowsum(dkbe * p.kb)
        d_ek = _rowsum(dkd * k)
        m = ddm * p.dm
        dgam = d_e * p.e - d_ek * p.ek + _rowsum(m) - p.to_col(_colsum(m))
        dgam_last = _colsum(d_ek * p.ek) + d_el * p.el
        dg_row = _colsum(jnp.where(p.tril, dgam, 0.0)) + dgam_last
        da_row = dg_row * p.a_neg * _sigmoid(p.xg)
        da_ref[...] = da_row
        db_ref[...] = p.to_row(dbeta) * p.beta_row * (1.0 - p.beta_row)
        dal_ref[...] += _rowsum(dg_row * p.g_row)
        ddt_ref[...] += _rowsum(da_row)

    tok_shape = jax.ShapeDtypeStruct((hh, t, dk), F32)
    row_shape = jax.ShapeDtypeStruct((hh, n, 1, c), F32)
    acc_shape = jax.ShapeDtypeStruct((hh, 1, LANES), F32)
    return pl.pallas_call(
        body, name=name, grid=(hh // hb, n), in_specs=[smem, smem] + qkv + [row, row, state, tok],
        out_specs=[tok, tok, tok, row, row, acc, acc],
        out_shape=[tok_shape, tok_shape, tok_shape, row_shape, row_shape, acc_shape, acc_shape],
        scratch_shapes=[pltpu.VMEM((hb, dk, dk), F32)],
        compiler_params=_params(("parallel", "arbitrary")))(alog, dtb, qkv_h, qkv_h, qkv_h, araw, braw, states, do)


def _gdn_post_fwd(o, z, g_o, name):
    hh, t, dv = o.shape
    tt = _tile(t, _ROW_TILES)

    def body(o_ref, z_ref, g_ref, y_ref):
        for h in range(hh):
            sl = slice(h * dv, (h + 1) * dv)
            ov = o_ref[h]
            r = lax.rsqrt(jnp.mean(ov * ov, axis=-1, keepdims=True) + EPS)
            y_ref[:, sl] = (ov * r * g_ref[...] * _silu(z_ref[:, sl])).astype(y_ref.dtype)

    return pl.pallas_call(
        body, name=name, grid=(t // tt,),
        in_specs=[pl.BlockSpec((hh, tt, dv), lambda i: (0, i, 0)), pl.BlockSpec((tt, hh * dv), lambda i: (i, 0)),
                  pl.BlockSpec((1, dv), lambda i: (0, 0))],
        out_specs=pl.BlockSpec((tt, hh * dv), lambda i: (i, 0)),
        out_shape=jax.ShapeDtypeStruct((t, hh * dv), MXU_DTYPE), compiler_params=_params(("parallel",)))(o, z, g_o)


def _gdn_post_bwd(o, z, dy, g_o, name):
    hh, t, dv = o.shape
    tt = _tile(t, _ROW_TILES)

    def body(o_ref, z_ref, dy_ref, g_ref, do_ref, dz_ref, dg_ref):
        @pl.when(pl.program_id(0) == 0)
        def _():
            dg_ref[...] = jnp.zeros_like(dg_ref)

        gv = g_ref[...]
        for h in range(hh):
            sl = slice(h * dv, (h + 1) * dv)
            ov, zz, dy = o_ref[h], z_ref[:, sl], dy_ref[:, sl]
            r = lax.rsqrt(jnp.mean(ov * ov, axis=-1, keepdims=True) + EPS)
            oh = ov * r
            dz_ref[:, sl] = (dy * oh * gv * _dsilu(zz)).astype(dz_ref.dtype)
            don = dy * _silu(zz)
            dg_ref[...] += _colsum(don * oh)
            doh = don * gv
            do_ref[h] = r * (doh - oh * jnp.mean(doh * oh, axis=-1, keepdims=True))

    return pl.pallas_call(
        body, name=name, grid=(t // tt,),
        in_specs=[pl.BlockSpec((hh, tt, dv), lambda i: (0, i, 0)), pl.BlockSpec((tt, hh * dv), lambda i: (i, 0)),
                  pl.BlockSpec((tt, hh * dv), lambda i: (i, 0)), pl.BlockSpec((1, dv), lambda i: (0, 0))],
        out_specs=[pl.BlockSpec((hh, tt, dv), lambda i: (0, i, 0)), pl.BlockSpec((tt, hh * dv), lambda i: (i, 0)),
                   pl.BlockSpec((1, dv), lambda i: (0, 0))],
        out_shape=[jax.ShapeDtypeStruct((hh, t, dv), F32), jax.ShapeDtypeStruct((t, hh * dv), MXU_DTYPE),
                   jax.ShapeDtypeStruct((1, dv), F32)],
        compiler_params=_params(("arbitrary",)))(o, z, dy, g_o)


def _adamw(g, w, m, v):
    m = ADAM_B1 * m + (1.0 - ADAM_B1) * g
    v = ADAM_B2 * v + (1.0 - ADAM_B2) * (g * g)
    m_hat = m / (1.0 - ADAM_B1 ** ADAM_STEP)
    v_hat = v / (1.0 - ADAM_B2 ** ADAM_STEP)
    return -ADAM_LR * (m_hat / (jnp.sqrt(v_hat) + ADAM_EPS) + ADAM_WD * w), m, v


def _ada_fwd(c_all, ada_w, name):
    nl, d, cols = ada_w.shape
    b = c_all.shape[0]

    def body(c_ref, w_ref, o_ref):
        o_ref[...] = _mm_hi(_silu(c_ref[...]), w_ref[...])

    return pl.pallas_call(
        body, name=name, grid=(nl,),
        in_specs=[pl.BlockSpec((b, d), lambda i: (0, 0)), pl.BlockSpec((None, d, cols), lambda i: (i, 0, 0))],
        out_specs=pl.BlockSpec((None, b, cols), lambda i: (i, 0, 0)),
        out_shape=jax.ShapeDtypeStruct((nl, b, cols), F32), compiler_params=_params(("parallel",)))(c_all, ada_w)


def _ada_bwd(c_col, dm, w, m, v, name):
    nl, d, cols = w.shape
    b = c_col.shape[0]
    tr = _tile(d, (256, 128))

    def body(c_ref, dm_ref, w_ref, m_ref, v_ref, g_ref, dl_ref, mo_ref, vo_ref):
        g = _silu(c_ref[0]) * dm_ref[pl.ds(0, 1), :]
        for j in range(1, b):
            g = g + _silu(c_ref[j]) * dm_ref[pl.ds(j, 1), :]
        g_ref[...] = g
        dl_ref[...], mo_ref[...], vo_ref[...] = _adamw(g, w_ref[...], m_ref[...], v_ref[...])

    blk = pl.BlockSpec((None, tr, cols), lambda l, i: (l, i, 0))
    shape = jax.ShapeDtypeStruct((nl, d, cols), F32)
    return pl.pallas_call(
        body, name=name, grid=(nl, d // tr),
        in_specs=[pl.BlockSpec((b, tr, 1), lambda l, i: (0, i, 0)), pl.BlockSpec((None, b, cols), lambda l, i: (l, 0, 0)),
                  blk, blk, blk],
        out_specs=[blk, blk, blk, blk], out_shape=[shape] * 4,
        compiler_params=_params(("parallel", "parallel")))(c_col, dm, w, m, v)


def _sum_adam(parts, w, m, v, name):
    nl, npart, r, cdim = parts.shape
    tr = _tile(r, (256, 128))

    def body(p_ref, w_ref, m_ref, v_ref, g_ref, dl_ref, mo_ref, vo_ref):
        g = p_ref[0]
        for j in range(1, npart):
            g = g + p_ref[j]
        g_ref[...] = g
        dl_ref[...], mo_ref[...], vo_ref[...] = _adamw(g, w_ref[...], m_ref[...], v_ref[...])

    blk = pl.BlockSpec((None, tr, cdim), lambda l, i: (l, i, 0))
    shape = jax.ShapeDtypeStruct((nl, r, cdim), F32)
    return pl.pallas_call(
        body, name=name, grid=(nl, r // tr),
        in_specs=[pl.BlockSpec((None, npart, tr, cdim), lambda l, i: (l, 0, i, 0)), blk, blk, blk],
        out_specs=[blk, blk, blk, blk], out_shape=[shape] * 4,
        compiler_params=_params(("parallel", "parallel")))(parts, w, m, v)


def _pair_sum(x, tmp, core, name):
    _, r, cdim = x.shape
    tr = _tile(r, (256, 128))

    def body(core_ref, x_ref, t_ref, o_ref):
        o_ref[...] = x_ref[...] + t_ref[...]

    grid_spec = pltpu.PrefetchScalarGridSpec(
        num_scalar_prefetch=1, grid=(N_DEV // 2, r // tr),
        in_specs=[pl.BlockSpec((None, tr, cdim), lambda ch, i, core_ref: (2 * ch + core_ref[0], i, 0)),
                  pl.BlockSpec((None, tr, cdim), lambda ch, i, core_ref: (ch, i, 0))],
        out_specs=pl.BlockSpec((None, tr, cdim), lambda ch, i, core_ref: (ch, i, 0)))
    return pl.pallas_call(
        body, name=name, grid_spec=grid_spec, out_shape=jax.ShapeDtypeStruct((N_DEV // 2, r, cdim), F32),
        compiler_params=_params(("parallel", "parallel")))(core, x, tmp)


_ANY = pl.BlockSpec(memory_space=pl.ANY)
_CHIP_FLIPS = ((1, 0), (0, 1), (1, 1))


def _coords():
    return lax.axis_index("x"), lax.axis_index("y"), lax.axis_index("c")


def _flip(v, f):
    return 1 - v if f else v


def _a2a_direct(xs, name):
    n, ncp = len(xs), N_DEV - 1

    def body(*refs):
        ins, outs = refs[:n], refs[n:2 * n]
        send, recv, loc = refs[2 * n:]
        x, y, c = _coords()
        me = 4 * x + 2 * y + c
        local = [pltpu.make_async_copy(ins[i].at[me], outs[i].at[me], loc.at[i]) for i in range(n)]
        for cp in local:
            cp.start()
        remote = []
        for i in range(n):
            for k in range(1, N_DEV):
                px, py, pc = _flip(x, k & 4), _flip(y, k & 2), _flip(c, k & 1)
                cp = pltpu.make_async_remote_copy(
                    src_ref=ins[i].at[4 * px + 2 * py + pc], dst_ref=outs[i].at[me],
                    send_sem=send.at[i * ncp + k - 1], recv_sem=recv.at[i * ncp + k - 1],
                    device_id=(px, py, pc), device_id_type=MESH)
                cp.start()
                remote.append(cp)
        for cp in remote:
            cp.wait()
        for cp in local:
            cp.wait()

    return pl.pallas_call(
        body, name=name, in_specs=[_ANY] * n, out_specs=[_ANY] * n,
        out_shape=[jax.ShapeDtypeStruct(a.shape, a.dtype) for a in xs],
        scratch_shapes=[pltpu.SemaphoreType.DMA((n * ncp,)), pltpu.SemaphoreType.DMA((n * ncp,)),
                        pltpu.SemaphoreType.DMA((n,))])(*xs)


def _ag_ici(blocks, name):
    n, ncp = len(blocks), len(_CHIP_FLIPS)

    def body(*refs):
        ins, outs = refs[:n], refs[n:2 * n]
        send, recv, loc = refs[2 * n:]
        x, y, c = _coords()
        me = 4 * x + 2 * y + c
        local = [pltpu.make_async_copy(ins[i], outs[i].at[me], loc.at[i]) for i in range(n)]
        for cp in local:
            cp.start()
        remote = []
        for i in range(n):
            for j, (fx, fy) in enumerate(_CHIP_FLIPS):
                cp = pltpu.make_async_remote_copy(
                    src_ref=ins[i], dst_ref=outs[i].at[me], send_sem=send.at[i * ncp + j],
                    recv_sem=recv.at[i * ncp + j], device_id=(_flip(x, fx), _flip(y, fy), c), device_id_type=MESH)
                cp.start()
                remote.append(cp)
        for cp in remote:
            cp.wait()
        for cp in local:
            cp.wait()

    return pl.pallas_call(
        body, name=name, in_specs=[_ANY] * n, out_specs=[_ANY] * n,
        out_shape=[jax.ShapeDtypeStruct((N_DEV,) + a.shape, a.dtype) for a in blocks],
        scratch_shapes=[pltpu.SemaphoreType.DMA((n * ncp,)), pltpu.SemaphoreType.DMA((n * ncp,)),
                        pltpu.SemaphoreType.DMA((n,))])(*blocks)


def _ag_d2d(ys, name):
    n, ncp = len(ys), N_DEV // 2

    def body(*refs):
        ins, outs = refs[:n], refs[n:2 * n]
        send, recv = refs[2 * n:]
        x, y, c = _coords()
        remote = []
        for i in range(n):
            for ch in range(ncp):
                slot = 2 * ch + c
                cp = pltpu.make_async_remote_copy(
                    src_ref=ins[i].at[slot], dst_ref=outs[i].at[slot], send_sem=send.at[i * ncp + ch],
                    recv_sem=recv.at[i * ncp + ch], device_id=(x, y, 1 - c), device_id_type=MESH)
                cp.start()
                remote.append(cp)
        for cp in remote:
            cp.wait()

    return pl.pallas_call(
        body, name=name, in_specs=[_ANY] * n, out_specs=[_ANY] * n,
        out_shape=[jax.ShapeDtypeStruct(a.shape, a.dtype) for a in ys],
        input_output_aliases={i: i for i in range(n)},
        scratch_shapes=[pltpu.SemaphoreType.DMA((n * ncp,)), pltpu.SemaphoreType.DMA((n * ncp,))])(*ys)


def _rs_d2d(xs, name):
    n, ncp = len(xs), N_DEV // 2

    def body(*refs):
        ins, outs = refs[:n], refs[n:2 * n]
        send, recv = refs[2 * n:]
        x, y, c = _coords()
        remote = []
        for i in range(n):
            for ch in range(ncp):
                cp = pltpu.make_async_remote_copy(
                    src_ref=ins[i].at[2 * ch + 1 - c], dst_ref=outs[i].at[ch], send_sem=send.at[i * ncp + ch],
                    recv_sem=recv.at[i * ncp + ch], device_id=(x, y, 1 - c), device_id_type=MESH)
                cp.start()
                remote.append(cp)
        for cp in remote:
            cp.wait()

    return pl.pallas_call(
        body, name=name, in_specs=[_ANY] * n, out_specs=[_ANY] * n,
        out_shape=[jax.ShapeDtypeStruct((ncp,) + a.shape[1:], a.dtype) for a in xs],
        scratch_shapes=[pltpu.SemaphoreType.DMA((n * ncp,)), pltpu.SemaphoreType.DMA((n * ncp,))])(*xs)


def _rs_ici(ss, accs, layer, name):
    n, ncp = len(ss), len(_CHIP_FLIPS)

    def body(*refs):
        ins, outs = refs[:n], refs[2 * n:3 * n]
        send, recv, loc = refs[3 * n:]
        x, y, c = _coords()
        chip = 2 * x + y
        local = [pltpu.make_async_copy(ins[i].at[chip], outs[i].at[layer, chip], loc.at[i]) for i in range(n)]
        for cp in local:
            cp.start()
        remote = []
        for i in range(n):
            for j, (fx, fy) in enumerate(_CHIP_FLIPS):
                px, py = _flip(x, fx), _flip(y, fy)
                cp = pltpu.make_async_remote_copy(
                    src_ref=ins[i].at[2 * px + py], dst_ref=outs[i].at[layer, chip], send_sem=send.at[i * ncp + j],
                    recv_sem=recv.at[i * ncp + j], device_id=(px, py, c), device_id_type=MESH)
                cp.start()
                remote.append(cp)
        for cp in remote:
            cp.wait()
        for cp in local:
            cp.wait()

    return pl.pallas_call(
        body, name=name, in_specs=[_ANY] * (2 * n), out_specs=[_ANY] * n,
        out_shape=[jax.ShapeDtypeStruct(a.shape, a.dtype) for a in accs],
        input_output_aliases={n + i: i for i in range(n)},
        scratch_shapes=[pltpu.SemaphoreType.DMA((n * ncp,)), pltpu.SemaphoreType.DMA((n * ncp,)),
                        pltpu.SemaphoreType.DMA((n,))])(*ss, *accs)


_PACK_ROWS = 256


def _pack(arrs):
    flat = jnp.concatenate([a.reshape(-1) for a in arrs])
    quantum = _PACK_ROWS * LANES
    total = -(-flat.shape[0] // quantum) * quantum
    return jnp.pad(flat, (0, total - flat.shape[0])).reshape(-1, LANES)


def _unpack(packed, like):
    flat, out, pos = packed.reshape(-1), [], 0
    for a in like:
        out.append(flat[pos:pos + a.size].reshape(a.shape))
        pos += a.size
    return out


def kernel(x, c, ada_w, ada_b, norm1_g, w_in, conv_w, spatial_w, spatial_b, v_norm_g, a_log, dt_bias, o_norm_g, w_branch_a, w_branch_b, w_out, norm2_g, w_ffn_in, w_ffn_out, final_g, loss_target, m_ada_w, m_ada_b, m_norm1_g, m_w_in, m_conv_w, m_spatial_w, m_spatial_b, m_v_norm_g, m_a_log, m_dt_bias, m_o_norm_g, m_w_branch_a, m_w_branch_b, m_w_out, m_norm2_g, m_w_ffn_in, m_w_ffn_out, m_final_g, v_ada_w, v_ada_b, v_norm1_g, v_w_in, v_conv_w, v_spatial_w, v_spatial_b, v_v_norm_g, v_a_log, v_dt_bias, v_o_norm_g, v_w_branch_a, v_w_branch_b, v_w_out, v_norm2_g, v_w_ffn_in, v_w_ffn_out, v_final_g):
    nl, d = ada_w.shape[0], x.shape[2]
    t = x.shape[1]
    nchunk = t // GDN_CHUNK
    xi, yi, ci = _coords()
    me = 4 * xi + 2 * yi + ci
    core = jnp.reshape(ci, (1,)).astype(jnp.int32)
    x0, target = x[0], loss_target[0]
    wcols = 3 * HEADS * HEAD_DIM
    o_uv, o_qkv, o_z, o_ba = 2 * d, 2 * d + wcols, 2 * d + wcols + d, 2 * d + wcols + d + 2 * HEADS

    c_all, cw_all = _a2a_direct([jnp.broadcast_to(c[None], (N_DEV,) + c.shape),
                                 jnp.broadcast_to(conv_w[None], (N_DEV,) + conv_w.shape)], "gather_small")
    c_all = c_all[:, 0]
    conv_full = cw_all.transpose(1, 2, 0, 3).reshape(nl, CONV_K, wcols)
    modp = _ada_fwd(c_all, ada_w, "ada_fwd")
    (modx,) = _a2a_direct([modp.transpose(1, 0, 2)], "mod_exchange")
    mod = (modx.transpose(1, 0, 2).reshape(nl, 6 * d) + ada_b).reshape(nl, 6, 1, d)

    big = (w_in, w_branch_a, w_branch_b, w_out, w_ffn_in, w_ffn_out)
    gathered = _ag_d2d(_ag_ici([w.astype(MXU_DTYPE) for w in big], "ag_ici"), "ag_d2d")
    col_full = lambda g: g.transpose(1, 2, 0, 3).reshape(nl, g.shape[2], -1)
    row_full = lambda g: g.transpose(1, 0, 2, 3).reshape(nl, -1, g.shape[3])
    w_in_f, w_a_f, w_b_f, w_o_f = col_full(gathered[0]), row_full(gathered[1]), row_full(gathered[2]), row_full(gathered[3])
    w_fi_f, w_fo_f = col_full(gathered[4]), row_full(gathered[5])

    def in_segments(i):
        w = w_in_f[i]
        ba = jnp.pad(w[:, o_z:o_ba], ((0, 0), (0, LANES - 2 * HEADS)))
        return w[:, :o_uv], w[:, o_uv:o_qkv], w[:, o_qkv:o_z], ba, w[:, o_ba:]

    def rows_of(ba, lo):
        return ba[:, lo:lo + HEADS].T.reshape(HEADS, nchunk, 1, GDN_CHUNK)

    saved = []
    x_cur, delta, gt_prev = x0, None, None
    for i in range(nl):
        sh1, sc1, gt1, sh2, sc2, gt2 = (mod[i, k] for k in range(6))
        s = dict(gt1=gt1, gt2=gt2, sc1=sc1, sc2=sc2)
        s["seg"] = in_segments(i)
        s["x_in"], s["h"] = _resid_norm(x_cur, delta, gt_prev, norm1_g[i][None], sc1, sh1, "norm1_fwd")
        s["uv"], s["qkv"], s["z"], s["ba"], s["gates"] = (
            _matmul(s["h"], w, "nn", "proj_fwd_%d" % k) for k, w in enumerate(s["seg"]))
        s["b_col"] = spatial_b[i][:, :, None]
        s["ya"] = _mixer_a_fwd(s["uv"], spatial_w[i], s["b_col"], v_norm_g[i][None], "mixer_a_fwd")
        s["qkv_h"] = _conv_fwd(s["qkv"], conv_full[i], "conv_fwd")
        s["braw"], s["araw"] = rows_of(s["ba"], 0), rows_of(s["ba"], HEADS)
        s["o"], s["states"] = _gdn_fwd(s["qkv_h"], s["araw"], s["braw"], a_log[i], dt_bias[i], "gdn_fwd")
        s["yb"] = _gdn_post_fwd(s["o"], s["z"], o_norm_g[i][None], "gdn_post_fwd")
        s["pa"] = _matmul(s["ya"], w_a_f[i], "nn", "branch_a_fwd")
        s["pb"] = _matmul(s["yb"], w_b_f[i], "nn", "branch_b_fwd")
        s["merged"] = _merge_fwd(s["pa"], s["pb"], s["gates"], "merge_fwd")
        s["mo"] = _matmul(s["merged"], w_o_f[i], "nn", "out_fwd")
        s["x1"], s["h2"] = _resid_norm(s["x_in"], s["mo"], gt1, norm2_g[i][None], sc2, sh2, "norm2_fwd")
        s["gu"] = _matmul(s["h2"], w_fi_f[i], "nn", "ffn_in_fwd")
        s["a"] = _swiglu_fwd(s["gu"], "swiglu_fwd")
        s["fo"] = _matmul(s["a"], w_fo_f[i], "nn", "ffn_out_fwd")
        saved.append(s)
        x_cur, delta, gt_prev = s["x1"], s["fo"], gt2
    dx, d_final_g, loss_tile = _final_loss(x_cur, delta, gt_prev, final_g[None], target, "final_loss")
    loss = lax.psum(loss_tile[0, 0], ("x", "y", "c"))

    big_shapes = [(d, w_in.shape[2]), w_branch_a.shape[1:], w_branch_b.shape[1:], w_out.shape[1:],
                  (d, w_ffn_in.shape[2]), w_ffn_out.shape[1:]]
    accs = [jnp.zeros((nl, N_DEV // 2) + tuple(sh), F32) for sh in big_shapes]
    col_blocks = lambda g: g.reshape(g.shape[0], N_DEV, -1).transpose(1, 0, 2)
    row_blocks = lambda g: g.reshape(N_DEV, -1, g.shape[1])
    dmod, small = [None] * nl, [None] * nl
    d_conv = [None] * nl
    for i in reversed(range(nl)):
        s = saved[i]
        dfo, dgt2 = _gate_bwd(dx, s["fo"], s["gt2"], "gate2_bwd")
        g_fo = _matmul(s["a"], dfo, "tn", "ffn_out_dw")
        da = _matmul(dfo, w_fo_f[i], "nt", "ffn_out_dx")
        dgu = _swiglu_bwd(s["gu"], da, "swiglu_bwd")
        g_fi = _matmul(s["h2"], dgu, "tn", "ffn_in_dw")
        dh2 = _matmul(dgu, w_fi_f[i], "nt", "ffn_in_dx")
        dx1, dsh2, dsc2, dg2 = _norm_bwd(s["x1"], dh2, dx, norm2_g[i][None], s["sc2"], "norm2_bwd")
        dmo, dgt1 = _gate_bwd(dx1, s["mo"], s["gt1"], "gate1_bwd")
        g_o = _matmul(s["merged"], dmo, "tn", "out_dw")
        dmerged = _matmul(dmo, w_o_f[i], "nt", "out_dx")
        dpa, dpb, dgates = _merge_bwd(dmerged, s["pa"], s["pb"], s["gates"], "merge_bwd")
        g_a = _matmul(s["ya"], dpa, "tn", "branch_a_dw")
        dya = _matmul(dpa, w_a_f[i], "nt", "branch_a_dx")
        g_b = _matmul(s["yb"], dpb, "tn", "branch_b_dw")
        dyb = _matmul(dpb, w_b_f[i], "nt", "branch_b_dx")
        duv, d_ws, d_bs, d_gv = _mixer_a_bwd(s["uv"], dya, spatial_w[i], jnp.swapaxes(spatial_w[i], 1, 2),
                                             s["b_col"], v_norm_g[i][None], "mixer_a_bwd")
        do, dz, d_go = _gdn_post_bwd(s["o"], s["z"], dyb, o_norm_g[i][None], "gdn_post_bwd")
        dq, dk, dv, d_ar, d_br, d_al, d_dt = _gdn_bwd(s["qkv_h"], s["araw"], s["braw"], a_log[i], dt_bias[i],
                                                      s["states"], do, "gdn_bwd")
        dacc, d_conv[i] = _conv_bwd_pre(s["qkv"], dq, dk, dv, conv_full[i], "conv_bwd_pre")
        dqkv = _conv_bwd_in(dacc, conv_full[i], "conv_bwd_in")
        cols = lambda r: r.reshape(HEADS, t).T
        dba = jnp.pad(jnp.concatenate([cols(d_br), cols(d_ar)], axis=1),
                      ((0, 0), (0, LANES - 2 * HEADS))).astype(MXU_DTYPE)
        dsegs = (duv, dqkv, dz, dba, dgates)
        g_segs = [_matmul(s["h"], dsg, "tn", "proj_dw_%d" % k) for k, dsg in enumerate(dsegs)]
        dh = None
        for k, dsg in enumerate(dsegs):
            dh = _matmul(dsg, s["seg"][k], "nt", "proj_dx_%d" % k, acc=dh)
        dx, dsh1, dsc1, dg1 = _norm_bwd(s["x_in"], dh, dx1, norm1_g[i][None], s["sc1"], "norm1_bwd")
        dmod[i] = jnp.concatenate([dsh1, dsc1, dgt1, dsh2, dsc2, dgt2], axis=1)[0]
        small[i] = (dg1[0], d_ws, d_bs[:, :, 0], d_gv[0], d_al[:, 0, 0], d_dt[:, 0, 0], d_go[0], dg2[0])
        g_in = jnp.concatenate([g_segs[0], g_segs[1], g_segs[2], g_segs[3][:, :2 * HEADS], g_segs[4]], axis=1)
        parts = [col_blocks(g_in), row_blocks(g_a), row_blocks(g_b), row_blocks(g_o), col_blocks(g_fi), row_blocks(g_fo)]
        other = _rs_d2d(parts, "rs_d2d")
        sums = [_pair_sum(p, o, core, "rs_pair_sum_%d" % k) for k, (p, o) in enumerate(zip(parts, other))]
        accs = _rs_ici(sums, accs, i, "rs_ici_%d" % i)

    dmod = jnp.stack(dmod)
    sm = [jnp.stack([small[i][k] for i in range(nl)]) for k in range(8)]
    rep_w = (ada_b, norm1_g, spatial_w, spatial_b, v_norm_g, a_log, dt_bias, o_norm_g, norm2_g, final_g)
    rep_m = (m_ada_b, m_norm1_g, m_spatial_w, m_spatial_b, m_v_norm_g, m_a_log, m_dt_bias, m_o_norm_g, m_norm2_g, m_final_g)
    rep_v = (v_ada_b, v_norm1_g, v_spatial_w, v_spatial_b, v_v_norm_g, v_a_log, v_dt_bias, v_o_norm_g, v_norm2_g, v_final_g)
    rep_g = (dmod, sm[0], sm[1], sm[2], sm[3], sm[4], sm[5], sm[6], sm[7], d_final_g[0])
    packed = _pack(rep_g)
    d_conv_blocks = jnp.stack(d_conv).reshape(nl, CONV_K, N_DEV, -1).transpose(2, 0, 1, 3).reshape(N_DEV, -1, LANES)
    dmod_blocks = dmod.reshape(nl, N_DEV, -1).transpose(1, 0, 2)
    rep_all, conv_all, dmod_all = _a2a_direct(
        [jnp.broadcast_to(packed[None], (N_DEV,) + packed.shape), d_conv_blocks, dmod_blocks], "small_grads")
    rep_out = _sum_adam(rep_all[None], _pack(rep_w)[None], _pack(rep_m)[None], _pack(rep_v)[None], "adam_small")
    rep_out = [_unpack(o[0], rep_w) for o in rep_out]
    conv_out = _sum_adam(conv_all[None], conv_w.reshape(1, -1, LANES), m_conv_w.reshape(1, -1, LANES),
                         v_conv_w.reshape(1, -1, LANES), "adam_conv")
    conv_out = [o.reshape(conv_w.shape) for o in conv_out]
    ada_out = _ada_bwd(c_all[:, :, None], dmod_all.transpose(1, 0, 2), ada_w, m_ada_w, v_ada_w, "ada_bwd_adam")
    big_m = (m_w_in, m_w_branch_a, m_w_branch_b, m_w_out, m_w_ffn_in, m_w_ffn_out)
    big_v = (v_w_in, v_w_branch_a, v_w_branch_b, v_w_out, v_w_ffn_in, v_w_ffn_out)
    big_out = [_sum_adam(accs[k], big[k], big_m[k], big_v[k], "adam_big_%d" % k) for k in range(6)]

    def ordered(kind):
        rep = rep_out[kind]
        return (ada_out[kind], rep[0], rep[1], big_out[0][kind], conv_out[kind], rep[2], rep[3], rep[4], rep[5],
                rep[6], rep[7], big_out[1][kind], big_out[2][kind], big_out[3][kind], rep[8], big_out[4][kind],
                big_out[5][kind], rep[9])

    return (loss, dx[None]) + ordered(0) + ordered(1) + ordered(2) + ordered(3)
```

```python
import functools

import jax
import jax.numpy as jnp
from jax import lax
from jax.experimental import pallas as pl
from jax.experimental.pallas import tpu as pltpu

F32 = jnp.float32
BF16 = jnp.bfloat16
MXU_DTYPE = BF16
WIRE_DTYPE = BF16
EPS = 1e-6
LANES = 128
SUBLANES = 8
GDN_CHUNK = 128
A_CHUNK = 128
GROUPS = 8
HEADS = 8
HEAD_DIM = 128
CONV_K = 4
N_DEV = 8
VMEM_LIMIT = 48 * 1024 * 1024
MESH = pl.DeviceIdType.MESH

ADAM_LR = 0.001
ADAM_B1 = 0.9
ADAM_B2 = 0.999
ADAM_EPS = 1e-08
ADAM_WD = 0.01
ADAM_STEP = 10

_NN = (((1,), (0,)), ((), ()))
_NT = (((1,), (1,)), ((), ()))
_TN = (((0,), (0,)), ((), ()))


def _mm(a, b, dims=_NN):
    return lax.dot_general(a.astype(MXU_DTYPE), b.astype(MXU_DTYPE), dims, preferred_element_type=F32)


def _mm_hi(a, b):
    return lax.dot_general(a, b, _NN, precision=lax.Precision.HIGHEST, preferred_element_type=F32)


def _tile(n, cands):
    for c in cands:
        if n % c == 0:
            return c
    return n


def _params(sem=None):
    return pltpu.CompilerParams(dimension_semantics=sem, vmem_limit_bytes=VMEM_LIMIT)


def _sigmoid(x):
    return 1.0 / (1.0 + jnp.exp(-x))


def _silu(x):
    return x * _sigmoid(x)


def _dsilu(x):
    s = _sigmoid(x)
    return s * (1.0 + x * (1.0 - s))


_GELU_C = 0.7978845608028654
_GELU_A = 0.044715


def _gelu(x):
    return 0.5 * x * (1.0 + jnp.tanh(_GELU_C * (x + _GELU_A * x * x * x)))


def _dgelu(x):
    t = jnp.tanh(_GELU_C * (x + _GELU_A * x * x * x))
    return 0.5 * (1.0 + t) + 0.5 * x * (1.0 - t * t) * _GELU_C * (1.0 + 3.0 * _GELU_A * x * x)


def _softplus(x):
    return jnp.maximum(x, 0.0) + jnp.log(1.0 + jnp.exp(-jnp.abs(x)))


_MM_TILES = (1024, 1408, 512, 256, 128)


def _matmul(a, b, mode, name, out_dtype=F32, acc=None):
    if mode == "nn":
        (m, k), n = a.shape, b.shape[1]
    elif mode == "nt":
        (m, k), n = a.shape, b.shape[0]
    else:
        (k, m), n = a.shape, b.shape[1]
    tm, tn, tk = _tile(m, _MM_TILES), _tile(n, _MM_TILES), _tile(k, _MM_TILES)
    nk = k // tk
    dims = {"nn": _NN, "nt": _NT, "tn": _TN}[mode]
    has_acc = acc is not None

    def body(*refs):
        if has_acc:
            a_ref, b_ref, c_ref, o_ref, acc_ref = refs
        else:
            a_ref, b_ref, o_ref, acc_ref = refs
        kk = pl.program_id(2)

        @pl.when(kk == 0)
        def _():
            acc_ref[...] = c_ref[...] if has_acc else jnp.zeros_like(acc_ref)

        acc_ref[...] += _mm(a_ref[...], b_ref[...], dims)

        @pl.when(kk == nk - 1)
        def _():
            o_ref[...] = acc_ref[...].astype(o_ref.dtype)

    a_spec = (pl.BlockSpec((tk, tm), lambda i, j, l: (l, i)) if mode == "tn"
              else pl.BlockSpec((tm, tk), lambda i, j, l: (i, l)))
    b_spec = (pl.BlockSpec((tn, tk), lambda i, j, l: (j, l)) if mode == "nt"
              else pl.BlockSpec((tk, tn), lambda i, j, l: (l, j)))
    o_spec = pl.BlockSpec((tm, tn), lambda i, j, l: (i, j))
    in_specs = [a_spec, b_spec] + ([o_spec] if has_acc else [])
    args = (a, b) + ((acc,) if has_acc else ())
    return pl.pallas_call(
        body, name=name, grid=(m // tm, n // tn, nk), in_specs=in_specs, out_specs=o_spec,
        out_shape=jax.ShapeDtypeStruct((m, n), out_dtype), scratch_shapes=[pltpu.VMEM((tm, tn), F32)],
        input_output_aliases=({2: 0} if has_acc else {}),
        compiler_params=_params(("parallel", "parallel", "arbitrary")))(*args)


_ROW_TILES = (512, 256, 128)


def _resid_norm(x, delta, gt, g, sc, sh, name):
    t, d = x.shape
    tt = _tile(t, _ROW_TILES)
    has = delta is not None

    def body(*refs):
        if has:
            x_ref, d_ref, gt_ref, g_ref, sc_ref, sh_ref, xo_ref, h_ref = refs
            xv = x_ref[...] + gt_ref[...] * d_ref[...]
            xo_ref[...] = xv
        else:
            x_ref, g_ref, sc_ref, sh_ref, h_ref = refs
            xv = x_ref[...]
        r = lax.rsqrt(jnp.mean(xv * xv, axis=-1, keepdims=True) + EPS)
        y = xv * r * g_ref[...]
        h_ref[...] = (y * (1.0 + sc_ref[...]) + sh_ref[...]).astype(h_ref.dtype)

    row = pl.BlockSpec((tt, d), lambda i: (i, 0))
    vec = pl.BlockSpec((1, d), lambda i: (0, 0))
    if has:
        return pl.pallas_call(
            body, name=name, grid=(t // tt,), in_specs=[row, row, vec, vec, vec, vec], out_specs=[row, row],
            out_shape=[jax.ShapeDtypeStruct((t, d), F32), jax.ShapeDtypeStruct((t, d), MXU_DTYPE)],
            compiler_params=_params(("parallel",)))(x, delta, gt, g, sc, sh)
    h = pl.pallas_call(
        body, name=name + "_first", grid=(t // tt,), in_specs=[row, vec, vec, vec], out_specs=row,
        out_shape=jax.ShapeDtypeStruct((t, d), MXU_DTYPE), compiler_params=_params(("parallel",)))(x, g, sc, sh)
    return x, h


def _final_loss(x, delta, gt, g, target, name):
    t, d = x.shape
    tt = _tile(t, _ROW_TILES)

    def body(x_ref, d_ref, gt_ref, g_ref, tg_ref, dx_ref, dg_ref, loss_ref):
        @pl.when(pl.program_id(0) == 0)
        def _():
            dg_ref[...] = jnp.zeros_like(dg_ref)
            loss_ref[...] = jnp.zeros_like(loss_ref)

        xv = x_ref[...] + gt_ref[...] * d_ref[...]
        r = lax.rsqrt(jnp.mean(xv * xv, axis=-1, keepdims=True) + EPS)
        xh = xv * r
        diff = xh * g_ref[...] - tg_ref[...]
        loss_ref[...] += jnp.sum(diff * diff) * (0.5 / d)
        dy = diff * (1.0 / d)
        dg_ref[...] += jnp.sum(dy * xh, axis=0, keepdims=True)
        dxh = dy * g_ref[...]
        dx_ref[...] = r * (dxh - xh * jnp.mean(dxh * xh, axis=-1, keepdims=True))

    row = pl.BlockSpec((tt, d), lambda i: (i, 0))
    vec = pl.BlockSpec((1, d), lambda i: (0, 0))
    tile = pl.BlockSpec((SUBLANES, LANES), lambda i: (0, 0))
    return pl.pallas_call(
        body, name=name, grid=(t // tt,), in_specs=[row, row, vec, vec, row], out_specs=[row, vec, tile],
        out_shape=[jax.ShapeDtypeStruct((t, d), F32), jax.ShapeDtypeStruct((1, d), F32),
                   jax.ShapeDtypeStruct((SUBLANES, LANES), F32)],
        compiler_params=_params(("arbitrary",)))(x, delta, gt, g, target)


def _norm_bwd(x, dh, dres, g, sc, name):
    t, d = x.shape
    tt = _tile(t, _ROW_TILES)

    def body(x_ref, dh_ref, dr_ref, g_ref, sc_ref, dx_ref, dsh_ref, dsc_ref, dg_ref):
        @pl.when(pl.program_id(0) == 0)
        def _():
            dsh_ref[...] = jnp.zeros_like(dsh_ref)
            dsc_ref[...] = jnp.zeros_like(dsc_ref)
            dg_ref[...] = jnp.zeros_like(dg_ref)

        xv, dh = x_ref[...], dh_ref[...]
        r = lax.rsqrt(jnp.mean(xv * xv, axis=-1, keepdims=True) + EPS)
        xh = xv * r
        gv, sc1 = g_ref[...], 1.0 + sc_ref[...]
        dsh_ref[...] += jnp.sum(dh, axis=0, keepdims=True)
        dsc_ref[...] += jnp.sum(dh * xh, axis=0, keepdims=True) * gv
        dg_ref[...] += jnp.sum(dh * xh, axis=0, keepdims=True) * sc1
        dxh = dh * (gv * sc1)
        dx_ref[...] = dr_ref[...] + r * (dxh - xh * jnp.mean(dxh * xh, axis=-1, keepdims=True))

    row = pl.BlockSpec((tt, d), lambda i: (i, 0))
    vec = pl.BlockSpec((1, d), lambda i: (0, 0))
    vshape = jax.ShapeDtypeStruct((1, d), F32)
    return pl.pallas_call(
        body, name=name, grid=(t // tt,), in_specs=[row, row, row, vec, vec], out_specs=[row, vec, vec, vec],
        out_shape=[jax.ShapeDtypeStruct((t, d), F32), vshape, vshape, vshape],
        compiler_params=_params(("arbitrary",)))(x, dh, dres, g, sc)


def _gate_bwd(dxo, branch, gt, name):
    t, d = dxo.shape
    tt = _tile(t, _ROW_TILES)

    def body(dx_ref, br_ref, gt_ref, db_ref, dgt_ref):
        @pl.when(pl.program_id(0) == 0)
        def _():
            dgt_ref[...] = jnp.zeros_like(dgt_ref)

        dx = dx_ref[...]
        db_ref[...] = (dx * gt_ref[...]).astype(db_ref.dtype)
        dgt_ref[...] += jnp.sum(dx * br_ref[...], axis=0, keepdims=True)

    row = pl.BlockSpec((tt, d), lambda i: (i, 0))
    vec = pl.BlockSpec((1, d), lambda i: (0, 0))
    return pl.pallas_call(
        body, name=name, grid=(t // tt,), in_specs=[row, row, vec], out_specs=[row, vec],
        out_shape=[jax.ShapeDtypeStruct((t, d), MXU_DTYPE), jax.ShapeDtypeStruct((1, d), F32)],
        compiler_params=_params(("arbitrary",)))(dxo, branch, gt)


def _swiglu_fwd(gu, name):
    t, f2 = gu.shape
    f = f2 // 2
    tt = _tile(t, (256, 128))

    def body(g_ref, u_ref, o_ref):
        o_ref[...] = (_silu(g_ref[...]) * u_ref[...]).astype(o_ref.dtype)

    return pl.pallas_call(
        body, name=name, grid=(t // tt,),
        in_specs=[pl.BlockSpec((tt, f), lambda i: (i, 0)), pl.BlockSpec((tt, f), lambda i: (i, 1))],
        out_specs=pl.BlockSpec((tt, f), lambda i: (i, 0)), out_shape=jax.ShapeDtypeStruct((t, f), MXU_DTYPE),
        compiler_params=_params(("parallel",)))(gu, gu)


def _swiglu_bwd(gu, da, name):
    t, f2 = gu.shape
    f = f2 // 2
    tt = _tile(t, (256, 128))

    def body(g_ref, u_ref, da_ref, o_ref):
        gate, da = g_ref[...], da_ref[...]
        o_ref[:, :f] = (da * u_ref[...] * _dsilu(gate)).astype(o_ref.dtype)
        o_ref[:, f:] = (da * _silu(gate)).astype(o_ref.dtype)

    return pl.pallas_call(
        body, name=name, grid=(t // tt,),
        in_specs=[pl.BlockSpec((tt, f), lambda i: (i, 0)), pl.BlockSpec((tt, f), lambda i: (i, 1)),
                  pl.BlockSpec((tt, f), lambda i: (i, 0))],
        out_specs=pl.BlockSpec((tt, f2), lambda i: (i, 0)), out_shape=jax.ShapeDtypeStruct((t, f2), MXU_DTYPE),
        compiler_params=_params(("parallel",)))(gu, gu, da)


def _merge_fwd(pa, pb, gates, name):
    t, d = pa.shape
    tt = _tile(t, _ROW_TILES)

    def body(pa_ref, pb_ref, ga_ref, gb_ref, o_ref):
        o_ref[...] = (_sigmoid(ga_ref[...]) * pa_ref[...] + _sigmoid(gb_ref[...]) * pb_ref[...]).astype(o_ref.dtype)

    row = pl.BlockSpec((tt, d), lambda i: (i, 0))
    row1 = pl.BlockSpec((tt, d), lambda i: (i, 1))
    return pl.pallas_call(
        body, name=name, grid=(t // tt,), in_specs=[row, row, row, row1], out_specs=row,
        out_shape=jax.ShapeDtypeStruct((t, d), MXU_DTYPE), compiler_params=_params(("parallel",)))(pa, pb, gates, gates)


def _merge_bwd(dm, pa, pb, gates, name):
    t, d = pa.shape
    tt = _tile(t, _ROW_TILES)

    def body(dm_ref, pa_ref, pb_ref, ga_ref, gb_ref, dpa_ref, dpb_ref, dg_ref):
        dm = dm_ref[...]
        sa, sb = _sigmoid(ga_ref[...]), _sigmoid(gb_ref[...])
        dpa_ref[...] = (dm * sa).astype(dpa_ref.dtype)
        dpb_ref[...] = (dm * sb).astype(dpb_ref.dtype)
        dg_ref[:, :d] = (dm * pa_ref[...] * sa * (1.0 - sa)).astype(dg_ref.dtype)
        dg_ref[:, d:] = (dm * pb_ref[...] * sb * (1.0 - sb)).astype(dg_ref.dtype)

    row = pl.BlockSpec((tt, d), lambda i: (i, 0))
    row1 = pl.BlockSpec((tt, d), lambda i: (i, 1))
    wide = pl.BlockSpec((tt, 2 * d), lambda i: (i, 0))
    return pl.pallas_call(
        body, name=name, grid=(t // tt,), in_specs=[row, row, row, row, row1], out_specs=[row, row, wide],
        out_shape=[jax.ShapeDtypeStruct((t, d), MXU_DTYPE), jax.ShapeDtypeStruct((t, d), MXU_DTYPE),
                   jax.ShapeDtypeStruct((t, 2 * d), MXU_DTYPE)],
        compiler_params=_params(("parallel",)))(dm, pa, pb, gates, gates)


def _tri_masks(n):
    ri = lax.broadcasted_iota(jnp.int32, (n, n), 0)
    ci = lax.broadcasted_iota(jnp.int32, (n, n), 1)
    return ri >= ci, ri > ci, ri == ci


def _mixer_a_fwd(uv, w_s, b_col, g_v, name):
    t, w2 = uv.shape
    w = w2 // 2
    c = A_CHUNK

    def body(u_ref, v_ref, w_ref, b_ref, gv_ref, y_ref):
        tril, _, _ = _tri_masks(c)
        ug, vg = _gelu(u_ref[...]), _gelu(v_ref[...])
        for g in range(GROUPS):
            sl = slice(g * c, (g + 1) * c)
            vt = vg[:, sl]
            r = lax.rsqrt(jnp.mean(vt * vt, axis=-1, keepdims=True) + EPS)
            vn = vt * r * gv_ref[:, sl]
            s = _mm(jnp.where(tril, w_ref[g], 0.0), vn) + b_ref[g]
            y_ref[:, sl] = (ug[:, sl] * s).astype(y_ref.dtype)

    return pl.pallas_call(
        body, name=name, grid=(t // c,),
        in_specs=[pl.BlockSpec((c, w), lambda i: (i, 0)), pl.BlockSpec((c, w), lambda i: (i, 1)),
                  pl.BlockSpec((GROUPS, c, c), lambda i: (0, 0, 0)), pl.BlockSpec((GROUPS, c, 1), lambda i: (0, 0, 0)),
                  pl.BlockSpec((1, w), lambda i: (0, 0))],
        out_specs=pl.BlockSpec((c, w), lambda i: (i, 0)), out_shape=jax.ShapeDtypeStruct((t, w), MXU_DTYPE),
        compiler_params=_params(("parallel",)))(uv, uv, w_s, b_col, g_v)


def _mixer_a_bwd(uv, dy, w_s, w_st, b_col, g_v, name):
    t, w2 = uv.shape
    w = w2 // 2
    c = A_CHUNK

    def body(u_ref, v_ref, dy_ref, w_ref, wt_ref, b_ref, gv_ref, duv_ref, dw_ref, db_ref, dgv_ref):
        @pl.when(pl.program_id(0) == 0)
        def _():
            dw_ref[...] = jnp.zeros_like(dw_ref)
            db_ref[...] = jnp.zeros_like(db_ref)
            dgv_ref[...] = jnp.zeros_like(dgv_ref)

        tril, _, _ = _tri_masks(c)
        triu = lax.broadcasted_iota(jnp.int32, (c, c), 0) <= lax.broadcasted_iota(jnp.int32, (c, c), 1)
        up, vp = u_ref[...], v_ref[...]
        ug, vg = _gelu(up), _gelu(vp)
        for g in range(GROUPS):
            sl = slice(g * c, (g + 1) * c)
            vt = vg[:, sl]
            r = lax.rsqrt(jnp.mean(vt * vt, axis=-1, keepdims=True) + EPS)
            vh = vt * r
            gv = gv_ref[:, sl]
            vn = vh * gv
            s = _mm(jnp.where(tril, w_ref[g], 0.0), vn) + b_ref[g]
            dy = dy_ref[:, sl]
            ds = dy * ug[:, sl]
            dw_ref[g] += jnp.where(tril, _mm(ds, vn, _NT), 0.0)
            db_ref[g] += jnp.sum(ds, axis=1, keepdims=True)
            dvn = _mm(jnp.where(triu, wt_ref[g], 0.0), ds)
            dgv_ref[:, sl] += jnp.sum(dvn * vh, axis=0, keepdims=True)
            dvh = dvn * gv
            dvt = r * (dvh - vh * jnp.mean(dvh * vh, axis=-1, keepdims=True))
            duv_ref[:, sl] = (dy * s * _dgelu(up[:, sl])).astype(duv_ref.dtype)
            duv_ref[:, w + g * c:w + (g + 1) * c] = (dvt * _dgelu(vp[:, sl])).astype(duv_ref.dtype)

    full3 = lambda shape: pl.BlockSpec(shape, lambda i: (0, 0, 0))
    return pl.pallas_call(
        body, name=name, grid=(t // c,),
        in_specs=[pl.BlockSpec((c, w), lambda i: (i, 0)), pl.BlockSpec((c, w), lambda i: (i, 1)),
                  pl.BlockSpec((c, w), lambda i: (i, 0)), full3((GROUPS, c, c)), full3((GROUPS, c, c)),
                  full3((GROUPS, c, 1)), pl.BlockSpec((1, w), lambda i: (0, 0))],
        out_specs=[pl.BlockSpec((c, w2), lambda i: (i, 0)), full3((GROUPS, c, c)), full3((GROUPS, c, 1)),
                   pl.BlockSpec((1, w), lambda i: (0, 0))],
        out_shape=[jax.ShapeDtypeStruct((t, w2), MXU_DTYPE), jax.ShapeDtypeStruct((GROUPS, c, c), F32),
                   jax.ShapeDtypeStruct((GROUPS, c, 1), F32), jax.ShapeDtypeStruct((1, w), F32)],
        compiler_params=_params(("arbitrary",)))(uv, uv, dy, w_s, w_st, b_col, g_v)


_Q_SCALE = HEAD_DIM ** -0.5


def _conv_taps(ext, w_ref):
    shifted = [ext[SUBLANES:]] + [pltpu.roll(ext, s, 0)[SUBLANES:] for s in range(1, CONV_K)]
    acc = shifted[0] * w_ref[pl.ds(CONV_K - 1, 1), :]
    for s in range(1, CONV_K):
        acc = acc + shifted[s] * w_ref[pl.ds(CONV_K - 1 - s, 1), :]
    return acc, shifted


def _conv_fwd(qkv, w, name):
    t, cw = qkv.shape
    tt = _tile(t, (256, 128))
    hb = tt // SUBLANES

    def body(x_ref, p_ref, w_ref, o_ref):
        prev = jnp.where(pl.program_id(0) > 0, p_ref[...], 0.0)
        acc, _ = _conv_taps(jnp.concatenate([prev, x_ref[...]], axis=0), w_ref)
        y = _silu(acc)
        for which in range(3):
            for h in range(HEADS):
                lo = (which * HEADS + h) * HEAD_DIM
                seg = y[:, lo:lo + HEAD_DIM]
                if which < 2:
                    seg = seg * lax.rsqrt(jnp.sum(seg * seg, axis=-1, keepdims=True) + EPS)
                if which == 0:
                    seg = seg * _Q_SCALE
                o_ref[which, h] = seg

    return pl.pallas_call(
        body, name=name, grid=(t // tt,),
        in_specs=[pl.BlockSpec((tt, cw), lambda i: (i, 0)),
                  pl.BlockSpec((SUBLANES, cw), lambda i: (jnp.maximum(i * hb - 1, 0), 0)),
                  pl.BlockSpec((CONV_K, cw), lambda i: (0, 0))],
        out_specs=pl.BlockSpec((3, HEADS, tt, HEAD_DIM), lambda i: (0, 0, i, 0)),
        out_shape=jax.ShapeDtypeStruct((3, HEADS, t, HEAD_DIM), F32),
        compiler_params=_params(("parallel",)))(qkv, qkv, w)


def _conv_bwd_pre(qkv, dq, dk, dv, w, name):
    t, cw = qkv.shape
    tt = _tile(t, (256, 128))
    hb = tt // SUBLANES

    def body(x_ref, p_ref, dq_ref, dk_ref, dv_ref, w_ref, da_ref, dw_ref):
        @pl.when(pl.program_id(0) == 0)
        def _():
            dw_ref[...] = jnp.zeros_like(dw_ref)

        prev = jnp.where(pl.program_id(0) > 0, p_ref[...], 0.0)
        acc, shifted = _conv_taps(jnp.concatenate([prev, x_ref[...]], axis=0), w_ref)
        y = _silu(acc)
        d_refs = (dq_ref, dk_ref, dv_ref)
        for which in range(3):
            for h in range(HEADS):
                lo = (which * HEADS + h) * HEAD_DIM
                sl = slice(lo, lo + HEAD_DIM)
                dn = d_refs[which][h]
                if which < 2:
                    seg = y[:, sl]
                    rho = lax.rsqrt(jnp.sum(seg * seg, axis=-1, keepdims=True) + EPS)
                    nrm = seg * rho
                    if which == 0:
                        dn = dn * _Q_SCALE
                    dn = rho * (dn - nrm * jnp.sum(dn * nrm, axis=-1, keepdims=True))
                dacc = dn * _dsilu(acc[:, sl])
                da_ref[:, sl] = dacc
                for s in range(CONV_K):
                    dw_ref[pl.ds(CONV_K - 1 - s, 1), sl] += jnp.sum(dacc * shifted[s][:, sl], axis=0, keepdims=True)

    head = pl.BlockSpec((HEADS, tt, HEAD_DIM), lambda i: (0, i, 0))
    return pl.pallas_call(
        body, name=name, grid=(t // tt,),
        in_specs=[pl.BlockSpec((tt, cw), lambda i: (i, 0)),
                  pl.BlockSpec((SUBLANES, cw), lambda i: (jnp.maximum(i * hb - 1, 0), 0)),
                  head, head, head, pl.BlockSpec((CONV_K, cw), lambda i: (0, 0))],
        out_specs=[pl.BlockSpec((tt, cw), lambda i: (i, 0)), pl.BlockSpec((CONV_K, cw), lambda i: (0, 0))],
        out_shape=[jax.ShapeDtypeStruct((t, cw), F32), jax.ShapeDtypeStruct((CONV_K, cw), F32)],
        compiler_params=_params(("arbitrary",)))(qkv, qkv, dq, dk, dv, w)


def _conv_bwd_in(dacc, w, name):
    t, cw = dacc.shape
    tt = _tile(t, (256, 128))
    hb = tt // SUBLANES
    nt = t // tt
    rows = tt + SUBLANES

    def body(d_ref, n_ref, w_ref, o_ref):
        cur = d_ref[...]
        nxt = jnp.where(pl.program_id(0) < nt - 1, n_ref[...], 0.0)
        ext = jnp.concatenate([cur, nxt], axis=0)
        acc = cur * w_ref[pl.ds(CONV_K - 1, 1), :]
        for s in range(1, CONV_K):
            acc = acc + pltpu.roll(ext, rows - s, 0)[:tt] * w_ref[pl.ds(CONV_K - 1 - s, 1), :]
        o_ref[...] = acc.astype(o_ref.dtype)

    return pl.pallas_call(
        body, name=name, grid=(nt,),
        in_specs=[pl.BlockSpec((tt, cw), lambda i: (i, 0)),
                  pl.BlockSpec((SUBLANES, cw), lambda i: (jnp.minimum((i + 1) * hb, t // SUBLANES - 1), 0)),
                  pl.BlockSpec((CONV_K, cw), lambda i: (0, 0))],
        out_specs=pl.BlockSpec((tt, cw), lambda i: (i, 0)), out_shape=jax.ShapeDtypeStruct((t, cw), MXU_DTYPE),
        compiler_params=_params(("parallel",)))(dacc, dacc, w)


_INV_BASE_SHIFT = 3


def _inv_unit_lower(a, eye):
    c = GDN_CHUNK
    ri = lax.broadcasted_iota(jnp.int32, (c, c), 0)
    ci = lax.broadcasted_iota(jnp.int32, (c, c), 1)
    same = lambda sh: (ri >> sh) == (ci >> sh)
    x = jnp.where(same(_INV_BASE_SHIFT), -a, 0.0)
    p = jnp.where(eye, 1.0, 0.0) + x
    xs = _split(x)
    x2 = _mm3(xs, xs)
    x2s, ps = _split(x2), _split(p)
    r = _mm3(x2s, tuple(jnp.concatenate([u, v], axis=-1) for u, v in zip(x2s, ps)))
    x4, p = r[..., :c], p + r[..., c:]
    p = p + _mm3(_split(x4), _split(p))
    for sh in range(_INV_BASE_SHIFT, c.bit_length() - 1):
        off = jnp.where(same(sh + 1) & jnp.logical_not(same(sh)), a, 0.0)
        ps = _split(p)
        p = p - _mm3(ps, _split(_mm3(_split(off), ps)))
    return p


def _split(a):
    hi = a.astype(BF16)
    return hi, (a - hi.astype(F32)).astype(BF16)


def _dot_heads(u, v, dims):
    if u.ndim == 3:
        return jnp.stack([_dot_heads(u[j], v[j], dims) for j in range(u.shape[0])])
    return lax.dot_general(u, v, dims, preferred_element_type=F32)


def _mm3(a, b):
    return _dot_heads(a[0], b[0], _NN) + (_dot_heads(a[0], b[1], _NN) + _dot_heads(a[1], b[0], _NN))


def _hmm(a, b, dims=_NN):
    return _dot_heads(a.astype(MXU_DTYPE), b.astype(MXU_DTYPE), dims)


def _rowsum(x):
    return jnp.sum(x, axis=-1, keepdims=True)


def _colsum(x):
    return jnp.sum(x, axis=-2, keepdims=True)


class _Pre:
    pass


def _gdn_pre(q, k, v, araw, braw, alog, dtb):
    c = GDN_CHUNK
    p = _Pre()
    p.tril, p.strict, p.eye = _tri_masks(c)
    p.to_col = lambda row: _rowsum(jnp.where(p.eye, row, 0.0))
    p.to_row = lambda col: _colsum(jnp.where(p.eye, col, 0.0))
    p.a_neg = -jnp.exp(alog + jnp.zeros((1, c), F32))
    p.xg = araw + dtb
    p.g_row = p.a_neg * _softplus(p.xg)
    p.beta_row = _sigmoid(braw)
    p.beta = p.to_col(p.beta_row)
    gam = _rowsum(jnp.where(p.tril, p.g_row, 0.0))
    gam_last = _rowsum(p.g_row)
    p.dm = jnp.where(p.tril, jnp.exp(jnp.where(p.tril, gam - p.to_row(gam), 0.0)), 0.0)
    p.e, p.ek, p.el = jnp.exp(gam), jnp.exp(gam_last - gam), jnp.exp(gam_last)
    p.kb = k * p.beta
    p.kk = _hmm(p.kb, k, _NT)
    p.t = _inv_unit_lower(jnp.where(p.strict, p.kk * p.dm, 0.0), p.eye)
    p.vb, p.kbe = v * p.beta, p.kb * p.e
    p.u, p.w = _hmm(p.t, p.vb), _hmm(p.t, p.kbe)
    p.qk0 = _hmm(q, k, _NT)
    p.qk = p.qk0 * p.dm
    p.qd, p.kd = q * p.e, k * p.ek
    return p


GDN_HEADS_PER_STEP = 8


def _head_scalars(ref, hb):
    h0 = pl.program_id(0) * hb
    return jnp.stack([jnp.full((1, 1), ref[h0 + j], F32) for j in range(hb)])


def _gdn_specs(n, reverse):
    c, dk, hb = GDN_CHUNK, HEAD_DIM, GDN_HEADS_PER_STEP
    ix = (lambda i: n - 1 - i) if reverse else (lambda i: i)
    smem = pl.BlockSpec(memory_space=pltpu.SMEM)
    qkv = [pl.BlockSpec((None, hb, c, dk), functools.partial(lambda w, h, i: (w, h, ix(i), 0), w)) for w in range(3)]
    row = pl.BlockSpec((hb, None, 1, c), lambda h, i: (h, ix(i), 0, 0))
    tok = pl.BlockSpec((hb, c, dk), lambda h, i: (h, ix(i), 0))
    state = pl.BlockSpec((hb, None, dk, dk), lambda h, i: (h, ix(i), 0, 0))
    return smem, qkv, row, tok, state


class _NoSide:
    operands, out_shape, scratch, aliases, n_in, n_out = [], [], [], {}, 0, 0


def _side_hooks(side, refs, n_main_in, n_main_out, n_main_scratch, grid):
    a = n_main_in + side.n_in
    b = a + n_main_out + side.n_out
    ins, outs, sems = refs[n_main_in:a], refs[a + n_main_out:b], refs[b + n_main_scratch:]
    main = refs[:n_main_in] + refs[a:a + n_main_out] + refs[b:b + n_main_scratch]
    ids = [pl.program_id(k) for k in range(len(grid))]

    def start():
        if side.n_in:
            pl.when(functools.reduce(jnp.logical_and, [i == 0 for i in ids]))(lambda: side.start(ins, outs, sems))

    def finish():
        if side.n_in:
            last = functools.reduce(jnp.logical_and, [i == g - 1 for i, g in zip(ids, grid)])
            pl.when(last)(lambda: side.finish(ins, outs, sems))

    return main, start, finish


def _carrier_call(body, name, grid, in_specs, out_specs, out_shape, scratch, side, args):
    aliases = {len(in_specs) + k: len(out_specs) + v for k, v in side.aliases.items()}
    return pl.pallas_call(
        body, name=name, grid=grid, in_specs=list(in_specs) + [_ANY] * side.n_in,
        out_specs=list(out_specs) + [_ANY] * side.n_out, out_shape=list(out_shape) + list(side.out_shape),
        scratch_shapes=list(scratch) + list(side.scratch), input_output_aliases=aliases,
        compiler_params=_params(("arbitrary",) * len(grid)))(*args, *side.operands)


def _gdn_fwd(qkv_h, araw, braw, alog, dtb, name, side=_NoSide):
    _, hh, t, dk = qkv_h.shape
    n, hb = t // GDN_CHUNK, GDN_HEADS_PER_STEP
    smem, qkv, row, tok, state = _gdn_specs(n, False)
    grid = (hh // hb, n)

    def body(*refs):
        main, side_start, side_finish = _side_hooks(side, refs, 7, 2, 1, grid)
        alog_ref, dt_ref, q_ref, k_ref, v_ref, a_ref, b_ref, o_ref, so_ref, s_ref = main
        side_start()

        @pl.when(pl.program_id(1) == 0)
        def _():
            s_ref[...] = jnp.zeros_like(s_ref)

        p = _gdn_pre(q_ref[...], k_ref[...], v_ref[...], a_ref[...], b_ref[...],
                     _head_scalars(alog_ref, hb), _head_scalars(dt_ref, hb))
        s = s_ref[...]
        vn = p.u - _hmm(p.w, s)
        o_ref[...] = _hmm(p.qd, s) + _hmm(p.qk, vn)
        so_ref[...] = s
        s_ref[...] = s * p.el + _hmm(p.kd, vn, _TN)
        side_finish()

    return _carrier_call(
        body, name, grid, [smem, smem] + qkv + [row, row], [tok, state],
        [jax.ShapeDtypeStruct((hh, t, dk), F32), jax.ShapeDtypeStruct((hh, n, dk, dk), F32)],
        [pltpu.VMEM((hb, dk, dk), F32)], side, (alog, dtb, qkv_h, qkv_h, qkv_h, araw, braw))


def _gdn_bwd(qkv_h, araw, braw, alog, dtb, states, do, name, side=_NoSide):
    _, hh, t, dk = qkv_h.shape
    c, hb = GDN_CHUNK, GDN_HEADS_PER_STEP
    n = t // c
    smem, qkv, row, tok, state = _gdn_specs(n, True)
    acc = pl.BlockSpec((hb, 1, LANES), lambda h, i: (h, 0, 0))
    grid = (hh // hb, n)

    def body(*refs):
        main, side_start, side_finish = _side_hooks(side, refs, 9, 7, 1, grid)
        (alog_ref, dt_ref, q_ref, k_ref, v_ref, a_ref, b_ref, s_ref, do_ref,
         dq_ref, dk_ref, dv_ref, da_ref, db_ref, dal_ref, ddt_ref, ds_ref) = main
        side_start()

        @pl.when(pl.program_id(1) == 0)
        def _():
            ds_ref[...] = jnp.zeros_like(ds_ref)
            dal_ref[...] = jnp.zeros_like(dal_ref)
            ddt_ref[...] = jnp.zeros_like(ddt_ref)

        q, k, v = q_ref[...], k_ref[...], v_ref[...]
        p = _gdn_pre(q, k, v, a_ref[...], b_ref[...], _head_scalars(alog_ref, hb), _head_scalars(dt_ref, hb))
        s, do, dsp = s_ref[...], do_ref[...], ds_ref[...]
        vn = p.u - _hmm(p.w, s)
        dqd = _hmm(do, s, _NT)
        dqk = _hmm(do, vn, _NT)
        dvn = _hmm(p.qk, do, _TN) + _hmm(p.kd, dsp)
        dkd = _hmm(vn, dsp, _NT)
        d_el = _colsum(_rowsum(s * dsp))
        ds_ref[...] = dsp * p.el + _hmm(p.qd, do, _TN) - _hmm(p.w, dvn, _TN)
        dw = -_hmm(dvn, s, _NT)
        d_t = _hmm(dvn, p.vb, _NT) + _hmm(dw, p.kbe, _NT)
        dvb, dkbe = _hmm(p.t, dvn, _TN), _hmm(p.t, dw, _TN)
        d_a = jnp.where(p.strict, -_hmm(p.t, _hmm(d_t, p.t, _NT), _TN), 0.0)
        dkk = d_a * p.dm
        dqk0 = dqk * p.dm
        ddm = d_a * p.kk + dqk * p.qk0
        dkb = _hmm(dkk, k) + dkbe * p.e
        dq_ref[...] = _hmm(dqk0, k) + dqd * p.e
        dk_ref[...] = _hmm(dkk, p.kb, _TN) + _hmm(dqk0, q, _TN) + dkd * p.ek + dkb * p.beta
        dv_ref[...] = dvb * p.beta
        dbeta = _rowsum(dkb * k) + _rowsum(dvb * v)
        d_e = _rowsum(dqd * q) + _rowsum(dkbe * p.kb)
        d_ek = _rowsum(dkd * k)
        m = ddm * p.dm
        dgam = d_e * p.e - d_ek * p.ek + _rowsum(m) - p.to_col(_colsum(m))
        dgam_last = _colsum(d_ek * p.ek) + d_el * p.el
        dg_row = _colsum(jnp.where(p.tril, dgam, 0.0)) + dgam_last
        da_row = dg_row * p.a_neg * _sigmoid(p.xg)
        da_ref[...] = da_row
        db_ref[...] = p.to_row(dbeta) * p.beta_row * (1.0 - p.beta_row)
        dal_ref[...] += _rowsum(dg_row * p.g_row)
        ddt_ref[...] += _rowsum(da_row)
        side_finish()

    tok_shape = jax.ShapeDtypeStruct((hh, t, dk), F32)
    row_shape = jax.ShapeDtypeStruct((hh, n, 1, c), F32)
    acc_shape = jax.ShapeDtypeStruct((hh, 1, LANES), F32)
    return _carrier_call(
        body, name, grid, [smem, smem] + qkv + [row, row, state, tok], [tok, tok, tok, row, row, acc, acc],
        [tok_shape, tok_shape, tok_shape, row_shape, row_shape, acc_shape, acc_shape],
        [pltpu.VMEM((hb, dk, dk), F32)], side, (alog, dtb, qkv_h, qkv_h, qkv_h, araw, braw, states, do))


def _gdn_post_fwd(o, z, g_o, name):
    hh, t, dv = o.shape
    tt = _tile(t, _ROW_TILES)

    def body(o_ref, z_ref, g_ref, y_ref):
        for h in range(hh):
            sl = slice(h * dv, (h + 1) * dv)
            ov = o_ref[h]
            r = lax.rsqrt(jnp.mean(ov * ov, axis=-1, keepdims=True) + EPS)
            y_ref[:, sl] = (ov * r * g_ref[...] * _silu(z_ref[:, sl])).astype(y_ref.dtype)

    return pl.pallas_call(
        body, name=name, grid=(t // tt,),
        in_specs=[pl.BlockSpec((hh, tt, dv), lambda i: (0, i, 0)), pl.BlockSpec((tt, hh * dv), lambda i: (i, 0)),
                  pl.BlockSpec((1, dv), lambda i: (0, 0))],
        out_specs=pl.BlockSpec((tt, hh * dv), lambda i: (i, 0)),
        out_shape=jax.ShapeDtypeStruct((t, hh * dv), MXU_DTYPE), compiler_params=_params(("parallel",)))(o, z, g_o)


def _gdn_post_bwd(o, z, dy, g_o, name):
    hh, t, dv = o.shape
    tt = _tile(t, _ROW_TILES)

    def body(o_ref, z_ref, dy_ref, g_ref, do_ref, dz_ref, dg_ref):
        @pl.when(pl.program_id(0) == 0)
        def _():
            dg_ref[...] = jnp.zeros_like(dg_ref)

        gv = g_ref[...]
        for h in range(hh):
            sl = slice(h * dv, (h + 1) * dv)
            ov, zz, dy = o_ref[h], z_ref[:, sl], dy_ref[:, sl]
            r = lax.rsqrt(jnp.mean(ov * ov, axis=-1, keepdims=True) + EPS)
            oh = ov * r
            dz_ref[:, sl] = (dy * oh * gv * _dsilu(zz)).astype(dz_ref.dtype)
            don = dy * _silu(zz)
            dg_ref[...] += _colsum(don * oh)
            doh = don * gv
            do_ref[h] = r * (doh - oh * jnp.mean(doh * oh, axis=-1, keepdims=True))

    return pl.pallas_call(
        body, name=name, grid=(t // tt,),
        in_specs=[pl.BlockSpec((hh, tt, dv), lambda i: (0, i, 0)), pl.BlockSpec((tt, hh * dv), lambda i: (i, 0)),
                  pl.BlockSpec((tt, hh * dv), lambda i: (i, 0)), pl.BlockSpec((1, dv), lambda i: (0, 0))],
        out_specs=[pl.BlockSpec((hh, tt, dv), lambda i: (0, i, 0)), pl.BlockSpec((tt, hh * dv), lambda i: (i, 0)),
                   pl.BlockSpec((1, dv), lambda i: (0, 0))],
        out_shape=[jax.ShapeDtypeStruct((hh, t, dv), F32), jax.ShapeDtypeStruct((t, hh * dv), MXU_DTYPE),
                   jax.ShapeDtypeStruct((1, dv), F32)],
        compiler_params=_params(("arbitrary",)))(o, z, dy, g_o)


def _adamw(g, w, m, v):
    m = ADAM_B1 * m + (1.0 - ADAM_B1) * g
    v = ADAM_B2 * v + (1.0 - ADAM_B2) * (g * g)
    m_hat = m / (1.0 - ADAM_B1 ** ADAM_STEP)
    v_hat = v / (1.0 - ADAM_B2 ** ADAM_STEP)
    return -ADAM_LR * (m_hat / (jnp.sqrt(v_hat) + ADAM_EPS) + ADAM_WD * w), m, v


def _ada_fwd(c_all, ada_w, name):
    nl, d, cols = ada_w.shape
    b = c_all.shape[0]

    def body(c_ref, w_ref, o_ref):
        o_ref[...] = _mm_hi(_silu(c_ref[...]), w_ref[...])

    return pl.pallas_call(
        body, name=name, grid=(nl,),
        in_specs=[pl.BlockSpec((b, d), lambda i: (0, 0)), pl.BlockSpec((None, d, cols), lambda i: (i, 0, 0))],
        out_specs=pl.BlockSpec((None, b, cols), lambda i: (i, 0, 0)),
        out_shape=jax.ShapeDtypeStruct((nl, b, cols), F32), compiler_params=_params(("parallel",)))(c_all, ada_w)


def _ada_bwd(c_col, dm, w, m, v, name):
    nl, d, cols = w.shape
    b = c_col.shape[0]
    tr = _tile(d, (256, 128))

    def body(c_ref, dm_ref, w_ref, m_ref, v_ref, g_ref, dl_ref, mo_ref, vo_ref):
        g = _silu(c_ref[0]) * dm_ref[pl.ds(0, 1), :]
        for j in range(1, b):
            g = g + _silu(c_ref[j]) * dm_ref[pl.ds(j, 1), :]
        g_ref[...] = g
        dl_ref[...], mo_ref[...], vo_ref[...] = _adamw(g, w_ref[...], m_ref[...], v_ref[...])

    blk = pl.BlockSpec((None, tr, cols), lambda l, i: (l, i, 0))
    shape = jax.ShapeDtypeStruct((nl, d, cols), F32)
    return pl.pallas_call(
        body, name=name, grid=(nl, d // tr),
        in_specs=[pl.BlockSpec((b, tr, 1), lambda l, i: (0, i, 0)), pl.BlockSpec((None, b, cols), lambda l, i: (l, 0, 0)),
                  blk, blk, blk],
        out_specs=[blk, blk, blk, blk], out_shape=[shape] * 4,
        compiler_params=_params(("parallel", "parallel")))(c_col, dm, w, m, v)


def _sum_adam(parts, w, m, v, name):
    nl, npart, r, cdim = parts.shape
    tr = _tile(r, (256, 128))

    def body(p_ref, w_ref, m_ref, v_ref, g_ref, dl_ref, mo_ref, vo_ref):
        g = p_ref[0].astype(F32)
        for j in range(1, npart):
            g = g + p_ref[j].astype(F32)
        g_ref[...] = g
        dl_ref[...], mo_ref[...], vo_ref[...] = _adamw(g, w_ref[...], m_ref[...], v_ref[...])

    blk = pl.BlockSpec((None, tr, cdim), lambda l, i: (l, i, 0))
    shape = jax.ShapeDtypeStruct((nl, r, cdim), F32)
    return pl.pallas_call(
        body, name=name, grid=(nl, r // tr),
        in_specs=[pl.BlockSpec((None, npart, tr, cdim), lambda l, i: (l, 0, i, 0)), blk, blk, blk],
        out_specs=[blk, blk, blk, blk], out_shape=[shape] * 4,
        compiler_params=_params(("parallel", "parallel")))(parts, w, m, v)


def _pair_sum(x, tmp, core, name):
    _, r, cdim = x.shape
    tr = _tile(r, (256, 128))

    def body(core_ref, x_ref, t_ref, o_ref):
        o_ref[...] = (x_ref[...] + t_ref[...]).astype(o_ref.dtype)

    grid_spec = pltpu.PrefetchScalarGridSpec(
        num_scalar_prefetch=1, grid=(N_DEV // 2, r // tr),
        in_specs=[pl.BlockSpec((None, tr, cdim), lambda ch, i, core_ref: (2 * ch + core_ref[0], i, 0)),
                  pl.BlockSpec((None, tr, cdim), lambda ch, i, core_ref: (ch, i, 0))],
        out_specs=pl.BlockSpec((None, tr, cdim), lambda ch, i, core_ref: (ch, i, 0)))
    return pl.pallas_call(
        body, name=name, grid_spec=grid_spec, out_shape=jax.ShapeDtypeStruct((N_DEV // 2, r, cdim), WIRE_DTYPE),
        compiler_params=_params(("parallel", "parallel")))(core, x, tmp)


_ANY = pl.BlockSpec(memory_space=pl.ANY)
_CHIP_FLIPS = ((1, 0), (0, 1), (1, 1))


def _coords():
    return lax.axis_index("x"), lax.axis_index("y"), lax.axis_index("c")


def _flip(v, f):
    return 1 - v if f else v


def _a2a_direct(xs, name):
    n, ncp = len(xs), N_DEV - 1

    def body(*refs):
        ins, outs = refs[:n], refs[n:2 * n]
        send, recv, loc = refs[2 * n:]
        x, y, c = _coords()
        me = 4 * x + 2 * y + c
        local = [pltpu.make_async_copy(ins[i].at[me], outs[i].at[me], loc.at[i]) for i in range(n)]
        for cp in local:
            cp.start()
        remote = []
        for i in range(n):
            for k in range(1, N_DEV):
                px, py, pc = _flip(x, k & 4), _flip(y, k & 2), _flip(c, k & 1)
                cp = pltpu.make_async_remote_copy(
                    src_ref=ins[i].at[4 * px + 2 * py + pc], dst_ref=outs[i].at[me],
                    send_sem=send.at[i * ncp + k - 1], recv_sem=recv.at[i * ncp + k - 1],
                    device_id=(px, py, pc), device_id_type=MESH)
                cp.start()
                remote.append(cp)
        for cp in remote:
            cp.wait()
        for cp in local:
            cp.wait()

    return pl.pallas_call(
        body, name=name, in_specs=[_ANY] * n, out_specs=[_ANY] * n,
        out_shape=[jax.ShapeDtypeStruct(a.shape, a.dtype) for a in xs],
        scratch_shapes=[pltpu.SemaphoreType.DMA((n * ncp,)), pltpu.SemaphoreType.DMA((n * ncp,)),
                        pltpu.SemaphoreType.DMA((n,))])(*xs)


class _AllGatherSide:
    def __init__(self, blocks):
        self.operands = list(blocks)
        n = self.n = len(self.operands)
        self.n_in = self.n_out = n
        self.out_shape = [jax.ShapeDtypeStruct((N_DEV,) + a.shape, a.dtype) for a in self.operands]
        self.aliases = {}
        nici, nd2d = len(_CHIP_FLIPS), N_DEV // 2
        self.scratch = [pltpu.SemaphoreType.DMA((n * nici,)), pltpu.SemaphoreType.DMA((n * nici,)),
                        pltpu.SemaphoreType.DMA((n * nd2d,)), pltpu.SemaphoreType.DMA((n * nd2d,)),
                        pltpu.SemaphoreType.DMA((n,))]

    def _first(self, ins, outs, sems):
        send, recv, _, _, loc = sems
        x, y, c = _coords()
        me = 4 * x + 2 * y + c
        nici = len(_CHIP_FLIPS)
        local = [pltpu.make_async_copy(ins[i], outs[i].at[me], loc.at[i]) for i in range(self.n)]
        remote = [pltpu.make_async_remote_copy(
            src_ref=ins[i], dst_ref=outs[i].at[me], send_sem=send.at[i * nici + j], recv_sem=recv.at[i * nici + j],
            device_id=(_flip(x, fx), _flip(y, fy), c), device_id_type=MESH)
            for i in range(self.n) for j, (fx, fy) in enumerate(_CHIP_FLIPS)]
        return local + remote

    def _second(self, outs, sems):
        _, _, send, recv, _ = sems
        x, y, c = _coords()
        nd2d = N_DEV // 2
        return [pltpu.make_async_remote_copy(
            src_ref=outs[i].at[2 * ch + c], dst_ref=outs[i].at[2 * ch + c], send_sem=send.at[i * nd2d + ch],
            recv_sem=recv.at[i * nd2d + ch], device_id=(x, y, 1 - c), device_id_type=MESH)
            for i in range(self.n) for ch in range(nd2d)]

    def start(self, ins, outs, sems):
        for cp in self._first(ins, outs, sems):
            cp.start()

    def finish(self, ins, outs, sems):
        for cp in self._first(ins, outs, sems):
            cp.wait()
        second = self._second(outs, sems)
        for cp in second:
            cp.start()
        for cp in second:
            cp.wait()


class _ReduceScatterIciSide:
    def __init__(self, sums, accs, layer):
        self.operands = list(sums) + list(accs)
        n = self.n = len(sums)
        self.layer = layer
        self.n_in, self.n_out = 2 * n, n
        self.out_shape = [jax.ShapeDtypeStruct(a.shape, a.dtype) for a in accs]
        self.aliases = {n + i: i for i in range(n)}
        nici = len(_CHIP_FLIPS)
        self.scratch = [pltpu.SemaphoreType.DMA((n * nici,)), pltpu.SemaphoreType.DMA((n * nici,)),
                        pltpu.SemaphoreType.DMA((n,))]

    def _copies(self, ins, outs, sems):
        send, recv, loc = sems
        x, y, c = _coords()
        chip = 2 * x + y
        nici = len(_CHIP_FLIPS)
        local = [pltpu.make_async_copy(ins[i].at[chip], outs[i].at[self.layer, chip], loc.at[i])
                 for i in range(self.n)]
        remote = [pltpu.make_async_remote_copy(
            src_ref=ins[i].at[2 * _flip(x, fx) + _flip(y, fy)], dst_ref=outs[i].at[self.layer, chip],
            send_sem=send.at[i * nici + j], recv_sem=recv.at[i * nici + j],
            device_id=(_flip(x, fx), _flip(y, fy), c), device_id_type=MESH)
            for i in range(self.n) for j, (fx, fy) in enumerate(_CHIP_FLIPS)]
        return local + remote

    def start(self, ins, outs, sems):
        for cp in self._copies(ins, outs, sems):
            cp.start()

    def finish(self, ins, outs, sems):
        for cp in self._copies(ins, outs, sems):
            cp.wait()


def _run_side(side, name):
    def body(*refs):
        ins, outs = refs[:side.n_in], refs[side.n_in:side.n_in + side.n_out]
        sems = refs[side.n_in + side.n_out:]
        side.start(ins, outs, sems)
        side.finish(ins, outs, sems)

    return pl.pallas_call(
        body, name=name, in_specs=[_ANY] * side.n_in, out_specs=[_ANY] * side.n_out, out_shape=side.out_shape,
        input_output_aliases=side.aliases, scratch_shapes=side.scratch)(*side.operands)


def _rs_d2d(xs, name):
    n, ncp = len(xs), N_DEV // 2

    def body(*refs):
        ins, outs = refs[:n], refs[n:2 * n]
        send, recv = refs[2 * n:]
        x, y, c = _coords()
        remote = []
        for i in range(n):
            for ch in range(ncp):
                cp = pltpu.make_async_remote_copy(
                    src_ref=ins[i].at[2 * ch + 1 - c], dst_ref=outs[i].at[ch], send_sem=send.at[i * ncp + ch],
                    recv_sem=recv.at[i * ncp + ch], device_id=(x, y, 1 - c), device_id_type=MESH)
                cp.start()
                remote.append(cp)
        for cp in remote:
            cp.wait()

    return pl.pallas_call(
        body, name=name, in_specs=[_ANY] * n, out_specs=[_ANY] * n,
        out_shape=[jax.ShapeDtypeStruct((ncp,) + a.shape[1:], a.dtype) for a in xs],
        scratch_shapes=[pltpu.SemaphoreType.DMA((n * ncp,)), pltpu.SemaphoreType.DMA((n * ncp,))])(*xs)


_PACK_ROWS = 256


def _pack(arrs):
    flat = jnp.concatenate([a.reshape(-1) for a in arrs])
    quantum = _PACK_ROWS * LANES
    total = -(-flat.shape[0] // quantum) * quantum
    return jnp.pad(flat, (0, total - flat.shape[0])).reshape(-1, LANES)


def _unpack(packed, like):
    flat, out, pos = packed.reshape(-1), [], 0
    for a in like:
        out.append(flat[pos:pos + a.size].reshape(a.shape))
        pos += a.size
    return out


def kernel(x, c, ada_w, ada_b, norm1_g, w_in, conv_w, spatial_w, spatial_b, v_norm_g, a_log, dt_bias, o_norm_g, w_branch_a, w_branch_b, w_out, norm2_g, w_ffn_in, w_ffn_out, final_g, loss_target, m_ada_w, m_ada_b, m_norm1_g, m_w_in, m_conv_w, m_spatial_w, m_spatial_b, m_v_norm_g, m_a_log, m_dt_bias, m_o_norm_g, m_w_branch_a, m_w_branch_b, m_w_out, m_norm2_g, m_w_ffn_in, m_w_ffn_out, m_final_g, v_ada_w, v_ada_b, v_norm1_g, v_w_in, v_conv_w, v_spatial_w, v_spatial_b, v_v_norm_g, v_a_log, v_dt_bias, v_o_norm_g, v_w_branch_a, v_w_branch_b, v_w_out, v_norm2_g, v_w_ffn_in, v_w_ffn_out, v_final_g):
    nl, d = ada_w.shape[0], x.shape[2]
    t = x.shape[1]
    nchunk = t // GDN_CHUNK
    xi, yi, ci = _coords()
    me = 4 * xi + 2 * yi + ci
    core = jnp.reshape(ci, (1,)).astype(jnp.int32)
    x0, target = x[0], loss_target[0]
    wcols = 3 * HEADS * HEAD_DIM
    o_uv, o_qkv, o_z, o_ba = 2 * d, 2 * d + wcols, 2 * d + wcols + d, 2 * d + wcols + d + 2 * HEADS

    c_all, cw_all = _a2a_direct([jnp.broadcast_to(c[None], (N_DEV,) + c.shape),
                                 jnp.broadcast_to(conv_w[None], (N_DEV,) + conv_w.shape)], "gather_small")
    c_all = c_all[:, 0]
    conv_full = cw_all.transpose(1, 2, 0, 3).reshape(nl, CONV_K, wcols)
    modp = _ada_fwd(c_all, ada_w, "ada_fwd")
    (modx,) = _a2a_direct([modp.transpose(1, 0, 2)], "mod_exchange")
    mod = (modx.transpose(1, 0, 2).reshape(nl, 6 * d) + ada_b).reshape(nl, 6, 1, d)

    big = (w_in, w_branch_a, w_branch_b, w_out, w_ffn_in, w_ffn_out)
    big_wire = [w.astype(WIRE_DTYPE) for w in big]
    layer_blocks = lambda i: [w[i] for w in big_wire]
    col_full = lambda g: g.transpose(1, 0, 2).reshape(g.shape[1], -1)
    row_full = lambda g: g.reshape(-1, g.shape[2])

    def full_weights(g):
        return col_full(g[0]), row_full(g[1]), row_full(g[2]), row_full(g[3]), col_full(g[4]), row_full(g[5])

    weights = [full_weights(_run_side(_AllGatherSide(layer_blocks(0)), "ag_first"))] + [None] * (nl - 1)

    def in_segments(i):
        w = weights[i][0]
        ba = jnp.pad(w[:, o_z:o_ba], ((0, 0), (0, LANES - 2 * HEADS)))
        return w[:, :o_uv], w[:, o_uv:o_qkv], w[:, o_qkv:o_z], ba, w[:, o_ba:]

    def rows_of(ba, lo):
        return ba[:, lo:lo + HEADS].T.reshape(HEADS, nchunk, 1, GDN_CHUNK)

    saved = []
    x_cur, delta, gt_prev = x0, None, None
    for i in range(nl):
        sh1, sc1, gt1, sh2, sc2, gt2 = (mod[i, k] for k in range(6))
        s = dict(gt1=gt1, gt2=gt2, sc1=sc1, sc2=sc2)
        s["seg"] = in_segments(i)
        s["x_in"], s["h"] = _resid_norm(x_cur, delta, gt_prev, norm1_g[i][None], sc1, sh1, "norm1_fwd")
        s["uv"], s["qkv"], s["z"], s["ba"], s["gates"] = (
            _matmul(s["h"], w, "nn", "proj_fwd_%d" % k) for k, w in enumerate(s["seg"]))
        s["b_col"] = spatial_b[i][:, :, None]
        s["ya"] = _mixer_a_fwd(s["uv"], spatial_w[i], s["b_col"], v_norm_g[i][None], "mixer_a_fwd")
        s["qkv_h"] = _conv_fwd(s["qkv"], conv_full[i], "conv_fwd")
        s["braw"], s["araw"] = rows_of(s["ba"], 0), rows_of(s["ba"], HEADS)
        gather_next = _AllGatherSide(layer_blocks(i + 1)) if i + 1 < nl else _NoSide
        s["o"], s["states"], *gathered = _gdn_fwd(s["qkv_h"], s["araw"], s["braw"], a_log[i], dt_bias[i],
                                                  "gdn_fwd", gather_next)
        if gathered:
            weights[i + 1] = full_weights(gathered)
        _, w_a, w_b, w_o, w_fi, w_fo = weights[i]
        s["yb"] = _gdn_post_fwd(s["o"], s["z"], o_norm_g[i][None], "gdn_post_fwd")
        s["pa"] = _matmul(s["ya"], w_a, "nn", "branch_a_fwd")
        s["pb"] = _matmul(s["yb"], w_b, "nn", "branch_b_fwd")
        s["merged"] = _merge_fwd(s["pa"], s["pb"], s["gates"], "merge_fwd")
        s["mo"] = _matmul(s["merged"], w_o, "nn", "out_fwd")
        s["x1"], s["h2"] = _resid_norm(s["x_in"], s["mo"], gt1, norm2_g[i][None], sc2, sh2, "norm2_fwd")
        s["gu"] = _matmul(s["h2"], w_fi, "nn", "ffn_in_fwd")
        s["a"] = _swiglu_fwd(s["gu"], "swiglu_fwd")
        s["fo"] = _matmul(s["a"], w_fo, "nn", "ffn_out_fwd")
        saved.append(s)
        x_cur, delta, gt_prev = s["x1"], s["fo"], gt2
    dx, d_final_g, loss_tile = _final_loss(x_cur, delta, gt_prev, final_g[None], target, "final_loss")
    loss = lax.psum(loss_tile[0, 0], ("x", "y", "c"))

    big_shapes = [(d, w_in.shape[2]), w_branch_a.shape[1:], w_branch_b.shape[1:], w_out.shape[1:],
                  (d, w_ffn_in.shape[2]), w_ffn_out.shape[1:]]
    accs = [jnp.zeros((nl, N_DEV // 2) + tuple(sh), WIRE_DTYPE) for sh in big_shapes]
    col_blocks = lambda g: g.reshape(g.shape[0], N_DEV, -1).transpose(1, 0, 2)
    row_blocks = lambda g: g.reshape(N_DEV, -1, g.shape[1])
    dmod, small = [None] * nl, [None] * nl
    d_conv = [None] * nl
    scatter = _NoSide
    for i in reversed(range(nl)):
        s = saved[i]
        _, w_a, w_b, w_o, w_fi, w_fo = weights[i]
        dfo, dgt2 = _gate_bwd(dx, s["fo"], s["gt2"], "gate2_bwd")
        g_fo = _matmul(s["a"], dfo, "tn", "ffn_out_dw")
        da = _matmul(dfo, w_fo, "nt", "ffn_out_dx")
        dgu = _swiglu_bwd(s["gu"], da, "swiglu_bwd")
        g_fi = _matmul(s["h2"], dgu, "tn", "ffn_in_dw")
        dh2 = _matmul(dgu, w_fi, "nt", "ffn_in_dx")
        dx1, dsh2, dsc2, dg2 = _norm_bwd(s["x1"], dh2, dx, norm2_g[i][None], s["sc2"], "norm2_bwd")
        dmo, dgt1 = _gate_bwd(dx1, s["mo"], s["gt1"], "gate1_bwd")
        g_o = _matmul(s["merged"], dmo, "tn", "out_dw")
        dmerged = _matmul(dmo, w_o, "nt", "out_dx")
        dpa, dpb, dgates = _merge_bwd(dmerged, s["pa"], s["pb"], s["gates"], "merge_bwd")
        g_a = _matmul(s["ya"], dpa, "tn", "branch_a_dw")
        dya = _matmul(dpa, w_a, "nt", "branch_a_dx")
        g_b = _matmul(s["yb"], dpb, "tn", "branch_b_dw")
        dyb = _matmul(dpb, w_b, "nt", "branch_b_dx")
        duv, d_ws, d_bs, d_gv = _mixer_a_bwd(s["uv"], dya, spatial_w[i], jnp.swapaxes(spatial_w[i], 1, 2),
                                             s["b_col"], v_norm_g[i][None], "mixer_a_bwd")
        do, dz, d_go = _gdn_post_bwd(s["o"], s["z"], dyb, o_norm_g[i][None], "gdn_post_bwd")
        dq, dk, dv, d_ar, d_br, d_al, d_dt, *scattered = _gdn_bwd(
            s["qkv_h"], s["araw"], s["braw"], a_log[i], dt_bias[i], s["states"], do, "gdn_bwd", scatter)
        if scattered:
            accs = scattered
        dacc, d_conv[i] = _conv_bwd_pre(s["qkv"], dq, dk, dv, conv_full[i], "conv_bwd_pre")
        dqkv = _conv_bwd_in(dacc, conv_full[i], "conv_bwd_in")
        cols = lambda r: r.reshape(HEADS, t).T
        dba = jnp.pad(jnp.concatenate([cols(d_br), cols(d_ar)], axis=1),
                      ((0, 0), (0, LANES - 2 * HEADS))).astype(MXU_DTYPE)
        dsegs = (duv, dqkv, dz, dba, dgates)
        g_segs = [_matmul(s["h"], dsg, "tn", "proj_dw_%d" % k) for k, dsg in enumerate(dsegs)]
        dh = None
        for k, dsg in enumerate(dsegs):
            dh = _matmul(dsg, s["seg"][k], "nt", "proj_dx_%d" % k, acc=dh)
        dx, dsh1, dsc1, dg1 = _norm_bwd(s["x_in"], dh, dx1, norm1_g[i][None], s["sc1"], "norm1_bwd")
        dmod[i] = jnp.concatenate([dsh1, dsc1, dgt1, dsh2, dsc2, dgt2], axis=1)[0]
        small[i] = (dg1[0], d_ws, d_bs[:, :, 0], d_gv[0], d_al[:, 0, 0], d_dt[:, 0, 0], d_go[0], dg2[0])
        g_in = jnp.concatenate([g_segs[0], g_segs[1], g_segs[2], g_segs[3][:, :2 * HEADS], g_segs[4]], axis=1)
        parts = [col_blocks(g_in), row_blocks(g_a), row_blocks(g_b), row_blocks(g_o), col_blocks(g_fi), row_blocks(g_fo)]
        other = _rs_d2d(parts, "rs_d2d")
        sums = [_pair_sum(p, o, core, "rs_pair_sum_%d" % k) for k, (p, o) in enumerate(zip(parts, other))]
        scatter = _ReduceScatterIciSide(sums, accs, i)
    accs = _run_side(scatter, "rs_ici_last")

    dmod = jnp.stack(dmod)
    sm = [jnp.stack([small[i][k] for i in range(nl)]) for k in range(8)]
    rep_w = (ada_b, norm1_g, spatial_w, spatial_b, v_norm_g, a_log, dt_bias, o_norm_g, norm2_g, final_g)
    rep_m = (m_ada_b, m_norm1_g, m_spatial_w, m_spatial_b, m_v_norm_g, m_a_log, m_dt_bias, m_o_norm_g, m_norm2_g, m_final_g)
    rep_v = (v_ada_b, v_norm1_g, v_spatial_w, v_spatial_b, v_v_norm_g, v_a_log, v_dt_bias, v_o_norm_g, v_norm2_g, v_final_g)
    rep_g = (dmod, sm[0], sm[1], sm[2], sm[3], sm[4], sm[5], sm[6], sm[7], d_final_g[0])
    packed = _pack(rep_g)
    d_conv_blocks = jnp.stack(d_conv).reshape(nl, CONV_K, N_DEV, -1).transpose(2, 0, 1, 3).reshape(N_DEV, -1, LANES)
    dmod_blocks = dmod.reshape(nl, N_DEV, -1).transpose(1, 0, 2)
    rep_all, conv_all, dmod_all = _a2a_direct(
        [jnp.broadcast_to(packed[None], (N_DEV,) + packed.shape), d_conv_blocks, dmod_blocks], "small_grads")
    rep_out = _sum_adam(rep_all[None], _pack(rep_w)[None], _pack(rep_m)[None], _pack(rep_v)[None], "adam_small")
    rep_out = [_unpack(o[0], rep_w) for o in rep_out]
    conv_out = _sum_adam(conv_all[None], conv_w.reshape(1, -1, LANES), m_conv_w.reshape(1, -1, LANES),
                         v_conv_w.reshape(1, -1, LANES), "adam_conv")
    conv_out = [o.reshape(conv_w.shape) for o in conv_out]
    ada_out = _ada_bwd(c_all[:, :, None], dmod_all.transpose(1, 0, 2), ada_w, m_ada_w, v_ada_w, "ada_bwd_adam")
    big_m = (m_w_in, m_w_branch_a, m_w_branch_b, m_w_out, m_w_ffn_in, m_w_ffn_out)
    big_v = (v_w_in, v_w_branch_a, v_w_branch_b, v_w_out, v_w_ffn_in, v_w_ffn_out)
    big_out = [_sum_adam(accs[k], big[k], big_m[k], big_v[k], "adam_big_%d" % k) for k in range(6)]

    def ordered(kind):
        rep = rep_out[kind]
        return (ada_out[kind], rep[0], rep[1], big_out[0][kind], conv_out[kind], rep[2], rep[3], rep[4], rep[5],
                rep[6], rep[7], big_out[1][kind], big_out[2][kind], big_out[3][kind], rep[8], big_out[4][kind],
                big_out[5][kind], rep[9])

    return (loss, dx[None]) + ordered(0) + ordered(1) + ordered(2) + ordered(3)
```

```python
import functools

import jax
import jax.numpy as jnp
from jax import lax
from jax.experimental import pallas as pl
from jax.experimental.pallas import tpu as pltpu

F32 = jnp.float32
BF16 = jnp.bfloat16
MXU_DTYPE = BF16
WIRE_DTYPE = BF16
EPS = 1e-6
LANES = 128
SUBLANES = 8
GDN_CHUNK = 128
A_CHUNK = 128
GROUPS = 8
HEADS = 8
HEAD_DIM = 128
CONV_K = 4
N_DEV = 8
VMEM_LIMIT = 48 * 1024 * 1024
MESH = pl.DeviceIdType.MESH

ADAM_LR = 0.001
ADAM_B1 = 0.9
ADAM_B2 = 0.999
ADAM_EPS = 1e-08
ADAM_WD = 0.01
ADAM_STEP = 10

_NN = (((1,), (0,)), ((), ()))
_NT = (((1,), (1,)), ((), ()))
_TN = (((0,), (0,)), ((), ()))


def _mm(a, b, dims=_NN):
    return lax.dot_general(a.astype(MXU_DTYPE), b.astype(MXU_DTYPE), dims, preferred_element_type=F32)


def _mm_hi(a, b):
    return lax.dot_general(a, b, _NN, precision=lax.Precision.HIGHEST, preferred_element_type=F32)


def _tile(n, cands):
    for c in cands:
        if n % c == 0:
            return c
    return n


def _params(sem=None):
    return pltpu.CompilerParams(dimension_semantics=sem, vmem_limit_bytes=VMEM_LIMIT)


def _sigmoid(x):
    return 1.0 / (1.0 + jnp.exp(-x))


def _silu(x):
    return x * _sigmoid(x)


def _dsilu(x):
    s = _sigmoid(x)
    return s * (1.0 + x * (1.0 - s))


_GELU_C = 0.7978845608028654
_GELU_A = 0.044715


def _gelu(x):
    return 0.5 * x * (1.0 + jnp.tanh(_GELU_C * (x + _GELU_A * x * x * x)))


def _dgelu(x):
    t = jnp.tanh(_GELU_C * (x + _GELU_A * x * x * x))
    return 0.5 * (1.0 + t) + 0.5 * x * (1.0 - t * t) * _GELU_C * (1.0 + 3.0 * _GELU_A * x * x)


def _softplus(x):
    return jnp.maximum(x, 0.0) + jnp.log(1.0 + jnp.exp(-jnp.abs(x)))


_MM_TILES = (1024, 1408, 1664, 512, 256, 128)


class _NoSide:
    operands, out_shape, scratch, aliases, n_in, n_out = [], [], [], {}, 0, 0


def _side_hooks(side, refs, n_main_in, n_main_out, n_main_scratch, grid):
    a = n_main_in + side.n_in
    b = a + n_main_out + side.n_out
    ins, outs, sems = refs[n_main_in:a], refs[a + n_main_out:b], refs[b + n_main_scratch:]
    main = refs[:n_main_in] + refs[a:a + n_main_out] + refs[b:b + n_main_scratch]
    ids = [pl.program_id(k) for k in range(len(grid))]

    def start():
        if side.n_in:
            pl.when(functools.reduce(jnp.logical_and, [i == 0 for i in ids]))(lambda: side.start(ins, outs, sems))

    def finish():
        if side.n_in:
            last = functools.reduce(jnp.logical_and, [i == g - 1 for i, g in zip(ids, grid)])
            pl.when(last)(lambda: side.finish(ins, outs, sems))

    return main, start, finish


def _carrier_call(body, name, grid, in_specs, out_specs, out_shape, scratch, side, args):
    aliases = {len(in_specs) + k: len(out_specs) + v for k, v in side.aliases.items()}
    return pl.pallas_call(
        body, name=name, grid=grid, in_specs=list(in_specs) + [_ANY] * side.n_in,
        out_specs=list(out_specs) + [_ANY] * side.n_out, out_shape=list(out_shape) + list(side.out_shape),
        scratch_shapes=list(scratch) + list(side.scratch), input_output_aliases=aliases,
        compiler_params=_params(("arbitrary",) * len(grid)))(*args, *side.operands)


def _matmul(a, b, mode, name, out_dtype=F32, side=_NoSide):
    if mode == "nn":
        (m, k), n = a.shape, b.shape[1]
    elif mode == "nt":
        (m, k), n = a.shape, b.shape[0]
    else:
        (k, m), n = a.shape, b.shape[1]
    tm, tn, tk = _tile(m, _MM_TILES), _tile(n, _MM_TILES), _tile(k, _MM_TILES)
    nk = k // tk
    grid = (m // tm, n // tn, nk)
    dims = {"nn": _NN, "nt": _NT, "tn": _TN}[mode]

    def body(*refs):
        (a_ref, b_ref, o_ref, acc_ref), side_start, side_finish = _side_hooks(side, refs, 2, 1, 1, grid)
        kk = pl.program_id(2)
        side_start()

        @pl.when(kk == 0)
        def _():
            acc_ref[...] = jnp.zeros_like(acc_ref)

        acc_ref[...] += _mm(a_ref[...], b_ref[...], dims)

        @pl.when(kk == nk - 1)
        def _():
            o_ref[...] = acc_ref[...].astype(o_ref.dtype)

        side_finish()

    a_spec = (pl.BlockSpec((tk, tm), lambda i, j, l: (l, i)) if mode == "tn"
              else pl.BlockSpec((tm, tk), lambda i, j, l: (i, l)))
    b_spec = (pl.BlockSpec((tn, tk), lambda i, j, l: (j, l)) if mode == "nt"
              else pl.BlockSpec((tk, tn), lambda i, j, l: (l, j)))
    o_spec = pl.BlockSpec((tm, tn), lambda i, j, l: (i, j))
    out = _carrier_call(body, name, grid, [a_spec, b_spec], [o_spec], [jax.ShapeDtypeStruct((m, n), out_dtype)],
                        [pltpu.VMEM((tm, tn), F32)], side, (a, b))
    return out if side.n_in else out[0]


_ROW_TILES = (512, 256, 128)


def _resid_norm(x, delta, gt, g, sc, sh, name):
    t, d = x.shape
    tt = _tile(t, _ROW_TILES)
    has = delta is not None

    def body(*refs):
        if has:
            x_ref, d_ref, gt_ref, g_ref, sc_ref, sh_ref, xo_ref, h_ref = refs
            xv = x_ref[...] + gt_ref[...] * d_ref[...]
            xo_ref[...] = xv
        else:
            x_ref, g_ref, sc_ref, sh_ref, h_ref = refs
            xv = x_ref[...]
        r = lax.rsqrt(jnp.mean(xv * xv, axis=-1, keepdims=True) + EPS)
        y = xv * r * g_ref[...]
        h_ref[...] = (y * (1.0 + sc_ref[...]) + sh_ref[...]).astype(h_ref.dtype)

    row = pl.BlockSpec((tt, d), lambda i: (i, 0))
    vec = pl.BlockSpec((1, d), lambda i: (0, 0))
    if has:
        return pl.pallas_call(
            body, name=name, grid=(t // tt,), in_specs=[row, row, vec, vec, vec, vec], out_specs=[row, row],
            out_shape=[jax.ShapeDtypeStruct((t, d), F32), jax.ShapeDtypeStruct((t, d), MXU_DTYPE)],
            compiler_params=_params(("parallel",)))(x, delta, gt, g, sc, sh)
    h = pl.pallas_call(
        body, name=name + "_first", grid=(t // tt,), in_specs=[row, vec, vec, vec], out_specs=row,
        out_shape=jax.ShapeDtypeStruct((t, d), MXU_DTYPE), compiler_params=_params(("parallel",)))(x, g, sc, sh)
    return x, h


def _final_loss(x, delta, gt, g, target, name):
    t, d = x.shape
    tt = _tile(t, _ROW_TILES)

    def body(x_ref, d_ref, gt_ref, g_ref, tg_ref, dx_ref, dg_ref, loss_ref):
        @pl.when(pl.program_id(0) == 0)
        def _():
            dg_ref[...] = jnp.zeros_like(dg_ref)
            loss_ref[...] = jnp.zeros_like(loss_ref)

        xv = x_ref[...] + gt_ref[...] * d_ref[...]
        r = lax.rsqrt(jnp.mean(xv * xv, axis=-1, keepdims=True) + EPS)
        xh = xv * r
        diff = xh * g_ref[...] - tg_ref[...]
        loss_ref[...] += jnp.sum(diff * diff) * (0.5 / d)
        dy = diff * (1.0 / d)
        dg_ref[...] += jnp.sum(dy * xh, axis=0, keepdims=True)
        dxh = dy * g_ref[...]
        dx_ref[...] = r * (dxh - xh * jnp.mean(dxh * xh, axis=-1, keepdims=True))

    row = pl.BlockSpec((tt, d), lambda i: (i, 0))
    vec = pl.BlockSpec((1, d), lambda i: (0, 0))
    tile = pl.BlockSpec((SUBLANES, LANES), lambda i: (0, 0))
    return pl.pallas_call(
        body, name=name, grid=(t // tt,), in_specs=[row, row, vec, vec, row], out_specs=[row, vec, tile],
        out_shape=[jax.ShapeDtypeStruct((t, d), F32), jax.ShapeDtypeStruct((1, d), F32),
                   jax.ShapeDtypeStruct((SUBLANES, LANES), F32)],
        compiler_params=_params(("arbitrary",)))(x, delta, gt, g, target)


def _norm_bwd(x, dh, dres, g, sc, name):
    t, d = x.shape
    tt = _tile(t, _ROW_TILES)

    def body(x_ref, dh_ref, dr_ref, g_ref, sc_ref, dx_ref, dsh_ref, dsc_ref, dg_ref):
        @pl.when(pl.program_id(0) == 0)
        def _():
            dsh_ref[...] = jnp.zeros_like(dsh_ref)
            dsc_ref[...] = jnp.zeros_like(dsc_ref)
            dg_ref[...] = jnp.zeros_like(dg_ref)

        xv, dh = x_ref[...], dh_ref[...]
        r = lax.rsqrt(jnp.mean(xv * xv, axis=-1, keepdims=True) + EPS)
        xh = xv * r
        gv, sc1 = g_ref[...], 1.0 + sc_ref[...]
        dsh_ref[...] += jnp.sum(dh, axis=0, keepdims=True)
        dsc_ref[...] += jnp.sum(dh * xh, axis=0, keepdims=True) * gv
        dg_ref[...] += jnp.sum(dh * xh, axis=0, keepdims=True) * sc1
        dxh = dh * (gv * sc1)
        dx_ref[...] = dr_ref[...] + r * (dxh - xh * jnp.mean(dxh * xh, axis=-1, keepdims=True))

    row = pl.BlockSpec((tt, d), lambda i: (i, 0))
    vec = pl.BlockSpec((1, d), lambda i: (0, 0))
    vshape = jax.ShapeDtypeStruct((1, d), F32)
    return pl.pallas_call(
        body, name=name, grid=(t // tt,), in_specs=[row, row, row, vec, vec], out_specs=[row, vec, vec, vec],
        out_shape=[jax.ShapeDtypeStruct((t, d), F32), vshape, vshape, vshape],
        compiler_params=_params(("arbitrary",)))(x, dh, dres, g, sc)


def _gate_bwd(dxo, branch, gt, name):
    t, d = dxo.shape
    tt = _tile(t, _ROW_TILES)

    def body(dx_ref, br_ref, gt_ref, db_ref, dgt_ref):
        @pl.when(pl.program_id(0) == 0)
        def _():
            dgt_ref[...] = jnp.zeros_like(dgt_ref)

        dx = dx_ref[...]
        db_ref[...] = (dx * gt_ref[...]).astype(db_ref.dtype)
        dgt_ref[...] += jnp.sum(dx * br_ref[...], axis=0, keepdims=True)

    row = pl.BlockSpec((tt, d), lambda i: (i, 0))
    vec = pl.BlockSpec((1, d), lambda i: (0, 0))
    return pl.pallas_call(
        body, name=name, grid=(t // tt,), in_specs=[row, row, vec], out_specs=[row, vec],
        out_shape=[jax.ShapeDtypeStruct((t, d), MXU_DTYPE), jax.ShapeDtypeStruct((1, d), F32)],
        compiler_params=_params(("arbitrary",)))(dxo, branch, gt)


def _swiglu_fwd(gu, name):
    t, f2 = gu.shape
    f = f2 // 2
    tt = _tile(t, (256, 128))

    def body(g_ref, u_ref, o_ref):
        o_ref[...] = (_silu(g_ref[...]) * u_ref[...]).astype(o_ref.dtype)

    return pl.pallas_call(
        body, name=name, grid=(t // tt,),
        in_specs=[pl.BlockSpec((tt, f), lambda i: (i, 0)), pl.BlockSpec((tt, f), lambda i: (i, 1))],
        out_specs=pl.BlockSpec((tt, f), lambda i: (i, 0)), out_shape=jax.ShapeDtypeStruct((t, f), MXU_DTYPE),
        compiler_params=_params(("parallel",)))(gu, gu)


def _swiglu_bwd(gu, da, name):
    t, f2 = gu.shape
    f = f2 // 2
    tt = _tile(t, (256, 128))

    def body(g_ref, u_ref, da_ref, o_ref):
        gate, da = g_ref[...], da_ref[...]
        o_ref[:, :f] = (da * u_ref[...] * _dsilu(gate)).astype(o_ref.dtype)
        o_ref[:, f:] = (da * _silu(gate)).astype(o_ref.dtype)

    return pl.pallas_call(
        body, name=name, grid=(t // tt,),
        in_specs=[pl.BlockSpec((tt, f), lambda i: (i, 0)), pl.BlockSpec((tt, f), lambda i: (i, 1)),
                  pl.BlockSpec((tt, f), lambda i: (i, 0))],
        out_specs=pl.BlockSpec((tt, f2), lambda i: (i, 0)), out_shape=jax.ShapeDtypeStruct((t, f2), MXU_DTYPE),
        compiler_params=_params(("parallel",)))(gu, gu, da)


class _ProjLayout:
    def __init__(self, d):
        wc = 3 * HEADS * HEAD_DIM
        self.d, self.wc = d, wc
        self.qkv, self.z, self.uv, self.gates, self.ba = 0, wc, wc + d, wc + 3 * d, wc + 5 * d
        self.width = self.ba + LANES
        assert self.z % d == 0 and self.uv % (2 * d) == 0 and self.gates % (2 * d) == 0 and self.ba % LANES == 0

    def pieces(self, shard):
        d, wc, out, lo = self.d, self.wc, [], 0
        for length, dst in ((2 * d, self.uv), (wc, self.qkv), (d, self.z), (2 * HEADS, self.ba), (2 * d, self.gates)):
            pos = lo
            while pos < lo + length:
                j = pos // shard
                n = min(lo + length, (j + 1) * shard) - pos
                out.append((j, pos - j * shard, n, dst + pos - lo))
                pos += n
            lo += length
        return out


def _merge_fwd(pa, pb, proj, gcol, name):
    t, d = pa.shape
    tt = _tile(t, _ROW_TILES)

    def body(pa_ref, pb_ref, ga_ref, gb_ref, o_ref):
        o_ref[...] = (_sigmoid(ga_ref[...]) * pa_ref[...] + _sigmoid(gb_ref[...]) * pb_ref[...]).astype(o_ref.dtype)

    row = pl.BlockSpec((tt, d), lambda i: (i, 0))
    gate = lambda k: pl.BlockSpec((tt, d), lambda i: (i, gcol // d + k))
    return pl.pallas_call(
        body, name=name, grid=(t // tt,), in_specs=[row, row, gate(0), gate(1)], out_specs=row,
        out_shape=jax.ShapeDtypeStruct((t, d), MXU_DTYPE), compiler_params=_params(("parallel",)))(pa, pb, proj, proj)


def _merge_bwd(dm, pa, pb, proj, gcol, dproj, name):
    t, d = pa.shape
    tt = _tile(t, _ROW_TILES)

    def body(dm_ref, pa_ref, pb_ref, ga_ref, gb_ref, _, dpa_ref, dpb_ref, dg_ref):
        dm = dm_ref[...]
        sa, sb = _sigmoid(ga_ref[...]), _sigmoid(gb_ref[...])
        dpa_ref[...] = (dm * sa).astype(dpa_ref.dtype)
        dpb_ref[...] = (dm * sb).astype(dpb_ref.dtype)
        dg_ref[:, :d] = (dm * pa_ref[...] * sa * (1.0 - sa)).astype(dg_ref.dtype)
        dg_ref[:, d:] = (dm * pb_ref[...] * sb * (1.0 - sb)).astype(dg_ref.dtype)

    row = pl.BlockSpec((tt, d), lambda i: (i, 0))
    gate = lambda k: pl.BlockSpec((tt, d), lambda i: (i, gcol // d + k))
    wide = pl.BlockSpec((tt, 2 * d), lambda i: (i, gcol // (2 * d)))
    return pl.pallas_call(
        body, name=name, grid=(t // tt,), in_specs=[row, row, row, gate(0), gate(1), _ANY], out_specs=[row, row, wide],
        out_shape=[jax.ShapeDtypeStruct((t, d), MXU_DTYPE), jax.ShapeDtypeStruct((t, d), MXU_DTYPE),
                   jax.ShapeDtypeStruct(dproj.shape, dproj.dtype)],
        input_output_aliases={5: 2}, compiler_params=_params(("parallel",)))(dm, pa, pb, proj, proj, dproj)


def _tri_masks(n):
    ri = lax.broadcasted_iota(jnp.int32, (n, n), 0)
    ci = lax.broadcasted_iota(jnp.int32, (n, n), 1)
    return ri >= ci, ri > ci, ri == ci


def _mixer_a_fwd(proj, ucol, w_s, b_col, g_v, name):
    t, w = proj.shape[0], g_v.shape[1]
    c = A_CHUNK

    def body(u_ref, v_ref, w_ref, b_ref, gv_ref, y_ref):
        tril, _, _ = _tri_masks(c)
        ug, vg = _gelu(u_ref[...]), _gelu(v_ref[...])
        for g in range(GROUPS):
            sl = slice(g * c, (g + 1) * c)
            vt = vg[:, sl]
            r = lax.rsqrt(jnp.mean(vt * vt, axis=-1, keepdims=True) + EPS)
            vn = vt * r * gv_ref[:, sl]
            s = _mm(jnp.where(tril, w_ref[g], 0.0), vn) + b_ref[g]
            y_ref[:, sl] = (ug[:, sl] * s).astype(y_ref.dtype)

    return pl.pallas_call(
        body, name=name, grid=(t // c,),
        in_specs=[pl.BlockSpec((c, w), lambda i: (i, ucol // w)), pl.BlockSpec((c, w), lambda i: (i, ucol // w + 1)),
                  pl.BlockSpec((GROUPS, c, c), lambda i: (0, 0, 0)), pl.BlockSpec((GROUPS, c, 1), lambda i: (0, 0, 0)),
                  pl.BlockSpec((1, w), lambda i: (0, 0))],
        out_specs=pl.BlockSpec((c, w), lambda i: (i, 0)), out_shape=jax.ShapeDtypeStruct((t, w), MXU_DTYPE),
        compiler_params=_params(("parallel",)))(proj, proj, w_s, b_col, g_v)


def _mixer_a_bwd(proj, ucol, dy, w_s, w_st, b_col, g_v, dproj, name):
    t, w = proj.shape[0], g_v.shape[1]
    w2 = 2 * w
    c = A_CHUNK

    def body(u_ref, v_ref, dy_ref, w_ref, wt_ref, b_ref, gv_ref, _, duv_ref, dw_ref, db_ref, dgv_ref):
        @pl.when(pl.program_id(0) == 0)
        def _():
            dw_ref[...] = jnp.zeros_like(dw_ref)
            db_ref[...] = jnp.zeros_like(db_ref)
            dgv_ref[...] = jnp.zeros_like(dgv_ref)

        tril, _, _ = _tri_masks(c)
        triu = lax.broadcasted_iota(jnp.int32, (c, c), 0) <= lax.broadcasted_iota(jnp.int32, (c, c), 1)
        up, vp = u_ref[...], v_ref[...]
        ug, vg = _gelu(up), _gelu(vp)
        for g in range(GROUPS):
            sl = slice(g * c, (g + 1) * c)
            vt = vg[:, sl]
            r = lax.rsqrt(jnp.mean(vt * vt, axis=-1, keepdims=True) + EPS)
            vh = vt * r
            gv = gv_ref[:, sl]
            vn = vh * gv
            s = _mm(jnp.where(tril, w_ref[g], 0.0), vn) + b_ref[g]
            dy = dy_ref[:, sl]
            ds = dy * ug[:, sl]
            dw_ref[g] += jnp.where(tril, _mm(ds, vn, _NT), 0.0)
            db_ref[g] += jnp.sum(ds, axis=1, keepdims=True)
            dvn = _mm(jnp.where(triu, wt_ref[g], 0.0), ds)
            dgv_ref[:, sl] += jnp.sum(dvn * vh, axis=0, keepdims=True)
            dvh = dvn * gv
            dvt = r * (dvh - vh * jnp.mean(dvh * vh, axis=-1, keepdims=True))
            duv_ref[:, sl] = (dy * s * _dgelu(up[:, sl])).astype(duv_ref.dtype)
            duv_ref[:, w + g * c:w + (g + 1) * c] = (dvt * _dgelu(vp[:, sl])).astype(duv_ref.dtype)

    full3 = lambda shape: pl.BlockSpec(shape, lambda i: (0, 0, 0))
    return pl.pallas_call(
        body, name=name, grid=(t // c,),
        in_specs=[pl.BlockSpec((c, w), lambda i: (i, ucol // w)), pl.BlockSpec((c, w), lambda i: (i, ucol // w + 1)),
                  pl.BlockSpec((c, w), lambda i: (i, 0)), full3((GROUPS, c, c)), full3((GROUPS, c, c)),
                  full3((GROUPS, c, 1)), pl.BlockSpec((1, w), lambda i: (0, 0)), _ANY],
        out_specs=[pl.BlockSpec((c, w2), lambda i: (i, ucol // w2)), full3((GROUPS, c, c)), full3((GROUPS, c, 1)),
                   pl.BlockSpec((1, w), lambda i: (0, 0))],
        out_shape=[jax.ShapeDtypeStruct(dproj.shape, dproj.dtype), jax.ShapeDtypeStruct((GROUPS, c, c), F32),
                   jax.ShapeDtypeStruct((GROUPS, c, 1), F32), jax.ShapeDtypeStruct((1, w), F32)],
        input_output_aliases={7: 0},
        compiler_params=_params(("arbitrary",)))(proj, proj, dy, w_s, w_st, b_col, g_v, dproj)


_Q_SCALE = HEAD_DIM ** -0.5


def _conv_taps(ext, w_ref):
    shifted = [ext[SUBLANES:]] + [pltpu.roll(ext, s, 0)[SUBLANES:] for s in range(1, CONV_K)]
    acc = shifted[0] * w_ref[pl.ds(CONV_K - 1, 1), :]
    for s in range(1, CONV_K):
        acc = acc + shifted[s] * w_ref[pl.ds(CONV_K - 1 - s, 1), :]
    return acc, shifted


def _conv_fwd(qkv, w, name):
    t, cw = qkv.shape[0], w.shape[1]
    tt = _tile(t, (256, 128))
    hb = tt // SUBLANES

    def body(x_ref, p_ref, w_ref, o_ref):
        prev = jnp.where(pl.program_id(0) > 0, p_ref[...], 0.0)
        acc, _ = _conv_taps(jnp.concatenate([prev, x_ref[...]], axis=0), w_ref)
        y = _silu(acc)
        for which in range(3):
            for h in range(HEADS):
                lo = (which * HEADS + h) * HEAD_DIM
                seg = y[:, lo:lo + HEAD_DIM]
                if which < 2:
                    seg = seg * lax.rsqrt(jnp.sum(seg * seg, axis=-1, keepdims=True) + EPS)
                if which == 0:
                    seg = seg * _Q_SCALE
                o_ref[which, h] = seg

    return pl.pallas_call(
        body, name=name, grid=(t // tt,),
        in_specs=[pl.BlockSpec((tt, cw), lambda i: (i, 0)),
                  pl.BlockSpec((SUBLANES, cw), lambda i: (jnp.maximum(i * hb - 1, 0), 0)),
                  pl.BlockSpec((CONV_K, cw), lambda i: (0, 0))],
        out_specs=pl.BlockSpec((3, HEADS, tt, HEAD_DIM), lambda i: (0, 0, i, 0)),
        out_shape=jax.ShapeDtypeStruct((3, HEADS, t, HEAD_DIM), F32),
        compiler_params=_params(("parallel",)))(qkv, qkv, w)


def _conv_bwd_pre(qkv, dq, dk, dv, w, name):
    t, cw = qkv.shape[0], w.shape[1]
    tt = _tile(t, (256, 128))
    hb = tt // SUBLANES

    def body(x_ref, p_ref, dq_ref, dk_ref, dv_ref, w_ref, da_ref, dw_ref):
        @pl.when(pl.program_id(0) == 0)
        def _():
            dw_ref[...] = jnp.zeros_like(dw_ref)

        prev = jnp.where(pl.program_id(0) > 0, p_ref[...], 0.0)
        acc, shifted = _conv_taps(jnp.concatenate([prev, x_ref[...]], axis=0), w_ref)
        y = _silu(acc)
        d_refs = (dq_ref, dk_ref, dv_ref)
        for which in range(3):
            for h in range(HEADS):
                lo = (which * HEADS + h) * HEAD_DIM
                sl = slice(lo, lo + HEAD_DIM)
                dn = d_refs[which][h]
                if which < 2:
                    seg = y[:, sl]
                    rho = lax.rsqrt(jnp.sum(seg * seg, axis=-1, keepdims=True) + EPS)
                    nrm = seg * rho
                    if which == 0:
                        dn = dn * _Q_SCALE
                    dn = rho * (dn - nrm * jnp.sum(dn * nrm, axis=-1, keepdims=True))
                dacc = dn * _dsilu(acc[:, sl])
                da_ref[:, sl] = dacc
                for s in range(CONV_K):
                    dw_ref[pl.ds(CONV_K - 1 - s, 1), sl] += jnp.sum(dacc * shifted[s][:, sl], axis=0, keepdims=True)

    head = pl.BlockSpec((HEADS, tt, HEAD_DIM), lambda i: (0, i, 0))
    return pl.pallas_call(
        body, name=name, grid=(t // tt,),
        in_specs=[pl.BlockSpec((tt, cw), lambda i: (i, 0)),
                  pl.BlockSpec((SUBLANES, cw), lambda i: (jnp.maximum(i * hb - 1, 0), 0)),
                  head, head, head, pl.BlockSpec((CONV_K, cw), lambda i: (0, 0))],
        out_specs=[pl.BlockSpec((tt, cw), lambda i: (i, 0)), pl.BlockSpec((CONV_K, cw), lambda i: (0, 0))],
        out_shape=[jax.ShapeDtypeStruct((t, cw), F32), jax.ShapeDtypeStruct((CONV_K, cw), F32)],
        compiler_params=_params(("arbitrary",)))(qkv, qkv, dq, dk, dv, w)


def _conv_bwd_in(dacc, w, dproj, name):
    t, cw = dacc.shape
    tt = _tile(t, (256, 128))
    hb = tt // SUBLANES
    nt = t // tt
    rows = tt + SUBLANES

    def body(d_ref, n_ref, w_ref, _, o_ref):
        cur = d_ref[...]
        nxt = jnp.where(pl.program_id(0) < nt - 1, n_ref[...], 0.0)
        ext = jnp.concatenate([cur, nxt], axis=0)
        acc = cur * w_ref[pl.ds(CONV_K - 1, 1), :]
        for s in range(1, CONV_K):
            acc = acc + pltpu.roll(ext, rows - s, 0)[:tt] * w_ref[pl.ds(CONV_K - 1 - s, 1), :]
        o_ref[...] = acc.astype(o_ref.dtype)

    return pl.pallas_call(
        body, name=name, grid=(nt,),
        in_specs=[pl.BlockSpec((tt, cw), lambda i: (i, 0)),
                  pl.BlockSpec((SUBLANES, cw), lambda i: (jnp.minimum((i + 1) * hb, t // SUBLANES - 1), 0)),
                  pl.BlockSpec((CONV_K, cw), lambda i: (0, 0)), _ANY],
        out_specs=pl.BlockSpec((tt, cw), lambda i: (i, 0)), out_shape=jax.ShapeDtypeStruct(dproj.shape, dproj.dtype),
        input_output_aliases={3: 0}, compiler_params=_params(("parallel",)))(dacc, dacc, w, dproj)


_INV_BASE_SHIFT = 3


def _inv_unit_lower(a, eye):
    c = GDN_CHUNK
    ri = lax.broadcasted_iota(jnp.int32, (c, c), 0)
    ci = lax.broadcasted_iota(jnp.int32, (c, c), 1)
    same = lambda sh: (ri >> sh) == (ci >> sh)
    x = jnp.where(same(_INV_BASE_SHIFT), -a, 0.0)
    p = jnp.where(eye, 1.0, 0.0) + x
    xs = _split(x)
    x2 = _mm3(xs, xs)
    x2s, ps = _split(x2), _split(p)
    r = _mm3(x2s, tuple(jnp.concatenate([u, v], axis=-1) for u, v in zip(x2s, ps)))
    x4, p = r[..., :c], p + r[..., c:]
    p = p + _mm3(_split(x4), _split(p))
    for sh in range(_INV_BASE_SHIFT, c.bit_length() - 1):
        off = jnp.where(same(sh + 1) & jnp.logical_not(same(sh)), a, 0.0)
        ps = _split(p)
        p = p - _mm3(ps, _split(_mm3(_split(off), ps)))
    return p


def _split(a):
    hi = a.astype(BF16)
    return hi, (a - hi.astype(F32)).astype(BF16)


def _dot_heads(u, v, dims):
    if u.ndim == 3:
        return jnp.stack([_dot_heads(u[j], v[j], dims) for j in range(u.shape[0])])
    return lax.dot_general(u, v, dims, preferred_element_type=F32)


def _mm3(a, b):
    return _dot_heads(a[0], b[0], _NN) + (_dot_heads(a[0], b[1], _NN) + _dot_heads(a[1], b[0], _NN))


def _hmm(a, b, dims=_NN):
    return _dot_heads(a.astype(MXU_DTYPE), b.astype(MXU_DTYPE), dims)


def _rowsum(x):
    return jnp.sum(x, axis=-1, keepdims=True)


def _colsum(x):
    return jnp.sum(x, axis=-2, keepdims=True)


class _Pre:
    pass


def _gdn_pre(q, k, v, araw, braw, alog, dtb):
    c = GDN_CHUNK
    p = _Pre()
    p.tril, p.strict, p.eye = _tri_masks(c)
    p.to_col = lambda row: _rowsum(jnp.where(p.eye, row, 0.0))
    p.to_row = lambda col: _colsum(jnp.where(p.eye, col, 0.0))
    p.a_neg = -jnp.exp(alog + jnp.zeros((1, c), F32))
    p.xg = araw + dtb
    p.g_row = p.a_neg * _softplus(p.xg)
    p.beta_row = _sigmoid(braw)
    p.beta = p.to_col(p.beta_row)
    gam = _rowsum(jnp.where(p.tril, p.g_row, 0.0))
    gam_last = _rowsum(p.g_row)
    p.dm = jnp.where(p.tril, jnp.exp(jnp.where(p.tril, gam - p.to_row(gam), 0.0)), 0.0)
    p.e, p.ek, p.el = jnp.exp(gam), jnp.exp(gam_last - gam), jnp.exp(gam_last)
    p.kb = k * p.beta
    p.kk = _hmm(p.kb, k, _NT)
    p.t = _inv_unit_lower(jnp.where(p.strict, p.kk * p.dm, 0.0), p.eye)
    p.vb, p.kbe = v * p.beta, p.kb * p.e
    p.u, p.w = _hmm(p.t, p.vb), _hmm(p.t, p.kbe)
    p.qk0 = _hmm(q, k, _NT)
    p.qk = p.qk0 * p.dm
    p.qd, p.kd = q * p.e, k * p.ek
    return p


GDN_HEADS_PER_STEP = 8


def _head_scalars(ref, hb):
    h0 = pl.program_id(0) * hb
    return jnp.stack([jnp.full((1, 1), ref[h0 + j], F32) for j in range(hb)])


def _gdn_specs(n, reverse):
    c, dk, hb = GDN_CHUNK, HEAD_DIM, GDN_HEADS_PER_STEP
    ix = (lambda i: n - 1 - i) if reverse else (lambda i: i)
    smem = pl.BlockSpec(memory_space=pltpu.SMEM)
    qkv = [pl.BlockSpec((None, hb, c, dk), functools.partial(lambda w, h, i: (w, h, ix(i), 0), w)) for w in range(3)]
    row = pl.BlockSpec((hb, None, 1, c), lambda h, i: (h, ix(i), 0, 0))
    tok = pl.BlockSpec((hb, c, dk), lambda h, i: (h, ix(i), 0))
    state = pl.BlockSpec((hb, None, dk, dk), lambda h, i: (h, ix(i), 0, 0))
    return smem, qkv, row, tok, state


def _gdn_fwd(qkv_h, araw, braw, alog, dtb, name, side=_NoSide):
    _, hh, t, dk = qkv_h.shape
    n, hb = t // GDN_CHUNK, GDN_HEADS_PER_STEP
    smem, qkv, row, tok, state = _gdn_specs(n, False)
    grid = (hh // hb, n)

    def body(*refs):
        main, side_start, side_finish = _side_hooks(side, refs, 7, 2, 1, grid)
        alog_ref, dt_ref, q_ref, k_ref, v_ref, a_ref, b_ref, o_ref, so_ref, s_ref = main
        side_start()

        @pl.when(pl.program_id(1) == 0)
        def _():
            s_ref[...] = jnp.zeros_like(s_ref)

        p = _gdn_pre(q_ref[...], k_ref[...], v_ref[...], a_ref[...], b_ref[...],
                     _head_scalars(alog_ref, hb), _head_scalars(dt_ref, hb))
        s = s_ref[...]
        vn = p.u - _hmm(p.w, s)
        o_ref[...] = _hmm(p.qd, s) + _hmm(p.qk, vn)
        so_ref[...] = s
        s_ref[...] = s * p.el + _hmm(p.kd, vn, _TN)
        side_finish()

    return _carrier_call(
        body, name, grid, [smem, smem] + qkv + [row, row], [tok, state],
        [jax.ShapeDtypeStruct((hh, t, dk), F32), jax.ShapeDtypeStruct((hh, n, dk, dk), F32)],
        [pltpu.VMEM((hb, dk, dk), F32)], side, (alog, dtb, qkv_h, qkv_h, qkv_h, araw, braw))


def _gdn_bwd(qkv_h, araw, braw, alog, dtb, states, do, name, side=_NoSide):
    _, hh, t, dk = qkv_h.shape
    c, hb = GDN_CHUNK, GDN_HEADS_PER_STEP
    n = t // c
    smem, qkv, row, tok, state = _gdn_specs(n, True)
    acc = pl.BlockSpec((hb, 1, LANES), lambda h, i: (h, 0, 0))
    grid = (hh // hb, n)

    def body(*refs):
        main, side_start, side_finish = _side_hooks(side, refs, 9, 7, 1, grid)
        (alog_ref, dt_ref, q_ref, k_ref, v_ref, a_ref, b_ref, s_ref, do_ref,
         dq_ref, dk_ref, dv_ref, da_ref, db_ref, dal_ref, ddt_ref, ds_ref) = main
        side_start()

        @pl.when(pl.program_id(1) == 0)
        def _():
            ds_ref[...] = jnp.zeros_like(ds_ref)
            dal_ref[...] = jnp.zeros_like(dal_ref)
            ddt_ref[...] = jnp.zeros_like(ddt_ref)

        q, k, v = q_ref[...], k_ref[...], v_ref[...]
        p = _gdn_pre(q, k, v, a_ref[...], b_ref[...], _head_scalars(alog_ref, hb), _head_scalars(dt_ref, hb))
        s, do, dsp = s_ref[...], do_ref[...], ds_ref[...]
        vn = p.u - _hmm(p.w, s)
        dqd = _hmm(do, s, _NT)
        dqk = _hmm(do, vn, _NT)
        dvn = _hmm(p.qk, do, _TN) + _hmm(p.kd, dsp)
        dkd = _hmm(vn, dsp, _NT)
        d_el = _colsum(_rowsum(s * dsp))
        ds_ref[...] = dsp * p.el + _hmm(p.qd, do, _TN) - _hmm(p.w, dvn, _TN)
        dw = -_hmm(dvn, s, _NT)
        d_t = _hmm(dvn, p.vb, _NT) + _hmm(dw, p.kbe, _NT)
        dvb, dkbe = _hmm(p.t, dvn, _TN), _hmm(p.t, dw, _TN)
        d_a = jnp.where(p.strict, -_hmm(p.t, _hmm(d_t, p.t, _NT), _TN), 0.0)
        dkk = d_a * p.dm
        dqk0 = dqk * p.dm
        ddm = d_a * p.kk + dqk * p.qk0
        dkb = _hmm(dkk, k) + dkbe * p.e
        dq_ref[...] = _hmm(dqk0, k) + dqd * p.e
        dk_ref[...] = _hmm(dkk, p.kb, _TN) + _hmm(dqk0, q, _TN) + dkd * p.ek + dkb * p.beta
        dv_ref[...] = dvb * p.beta
        dbeta = _rowsum(dkb * k) + _rowsum(dvb * v)
        d_e = _rowsum(dqd * q) + _rowsum(dkbe * p.kb)
        d_ek = _rowsum(dkd * k)
        m = ddm * p.dm
        dgam = d_e * p.e - d_ek * p.ek + _rowsum(m) - p.to_col(_colsum(m))
        dgam_last = _colsum(d_ek * p.ek) + d_el * p.el
        dg_row = _colsum(jnp.where(p.tril, dgam, 0.0)) + dgam_last
        da_row = dg_row * p.a_neg * _sigmoid(p.xg)
        da_ref[...] = da_row
        db_ref[...] = p.to_row(dbeta) * p.beta_row * (1.0 - p.beta_row)
        dal_ref[...] += _rowsum(dg_row * p.g_row)
        ddt_ref[...] += _rowsum(da_row)
        side_finish()

    tok_shape = jax.ShapeDtypeStruct((hh, t, dk), F32)
    row_shape = jax.ShapeDtypeStruct((hh, n, 1, c), F32)
    acc_shape = jax.ShapeDtypeStruct((hh, 1, LANES), F32)
    return _carrier_call(
        body, name, grid, [smem, smem] + qkv + [row, row, state, tok], [tok, tok, tok, row, row, acc, acc],
        [tok_shape, tok_shape, tok_shape, row_shape, row_shape, acc_shape, acc_shape],
        [pltpu.VMEM((hb, dk, dk), F32)], side, (alog, dtb, qkv_h, qkv_h, qkv_h, araw, braw, states, do))


def _gdn_post_fwd(o, proj, zcol, g_o, name):
    hh, t, dv = o.shape
    tt = _tile(t, _ROW_TILES)
    zblk = zcol // (hh * dv)

    def body(o_ref, z_ref, g_ref, y_ref):
        for h in range(hh):
            sl = slice(h * dv, (h + 1) * dv)
            ov = o_ref[h]
            r = lax.rsqrt(jnp.mean(ov * ov, axis=-1, keepdims=True) + EPS)
            y_ref[:, sl] = (ov * r * g_ref[...] * _silu(z_ref[:, sl])).astype(y_ref.dtype)

    return pl.pallas_call(
        body, name=name, grid=(t // tt,),
        in_specs=[pl.BlockSpec((hh, tt, dv), lambda i: (0, i, 0)), pl.BlockSpec((tt, hh * dv), lambda i: (i, zblk)),
                  pl.BlockSpec((1, dv), lambda i: (0, 0))],
        out_specs=pl.BlockSpec((tt, hh * dv), lambda i: (i, 0)),
        out_shape=jax.ShapeDtypeStruct((t, hh * dv), MXU_DTYPE), compiler_params=_params(("parallel",)))(o, proj, g_o)


def _gdn_post_bwd(o, proj, zcol, dy, g_o, dproj, name):
    hh, t, dv = o.shape
    tt = _tile(t, _ROW_TILES)
    zblk = zcol // (hh * dv)

    def body(o_ref, z_ref, dy_ref, g_ref, _, do_ref, dz_ref, dg_ref):
        @pl.when(pl.program_id(0) == 0)
        def _():
            dg_ref[...] = jnp.zeros_like(dg_ref)

        gv = g_ref[...]
        for h in range(hh):
            sl = slice(h * dv, (h + 1) * dv)
            ov, zz, dy = o_ref[h], z_ref[:, sl], dy_ref[:, sl]
            r = lax.rsqrt(jnp.mean(ov * ov, axis=-1, keepdims=True) + EPS)
            oh = ov * r
            dz_ref[:, sl] = (dy * oh * gv * _dsilu(zz)).astype(dz_ref.dtype)
            don = dy * _silu(zz)
            dg_ref[...] += _colsum(don * oh)
            doh = don * gv
            do_ref[h] = r * (doh - oh * jnp.mean(doh * oh, axis=-1, keepdims=True))

    return pl.pallas_call(
        body, name=name, grid=(t // tt,),
        in_specs=[pl.BlockSpec((hh, tt, dv), lambda i: (0, i, 0)), pl.BlockSpec((tt, hh * dv), lambda i: (i, zblk)),
                  pl.BlockSpec((tt, hh * dv), lambda i: (i, 0)), pl.BlockSpec((1, dv), lambda i: (0, 0)), _ANY],
        out_specs=[pl.BlockSpec((hh, tt, dv), lambda i: (0, i, 0)), pl.BlockSpec((tt, hh * dv), lambda i: (i, zblk)),
                   pl.BlockSpec((1, dv), lambda i: (0, 0))],
        out_shape=[jax.ShapeDtypeStruct((hh, t, dv), F32), jax.ShapeDtypeStruct(dproj.shape, dproj.dtype),
                   jax.ShapeDtypeStruct((1, dv), F32)],
        input_output_aliases={4: 1}, compiler_params=_params(("arbitrary",)))(o, proj, dy, g_o, dproj)


def _write_cols(dst, src, col, name):
    t, w = src.shape
    tt = _tile(t, _ROW_TILES)

    def body(s_ref, _, o_ref):
        o_ref[...] = s_ref[...].astype(o_ref.dtype)

    return pl.pallas_call(
        body, name=name, grid=(t // tt,), in_specs=[pl.BlockSpec((tt, w), lambda i: (i, 0)), _ANY],
        out_specs=pl.BlockSpec((tt, w), lambda i: (i, col // w)), out_shape=jax.ShapeDtypeStruct(dst.shape, dst.dtype),
        input_output_aliases={1: 0}, compiler_params=_params(("parallel",)))(src, dst)


def _adamw(g, w, m, v):
    m = ADAM_B1 * m + (1.0 - ADAM_B1) * g
    v = ADAM_B2 * v + (1.0 - ADAM_B2) * (g * g)
    m_hat = m / (1.0 - ADAM_B1 ** ADAM_STEP)
    v_hat = v / (1.0 - ADAM_B2 ** ADAM_STEP)
    return -ADAM_LR * (m_hat / (jnp.sqrt(v_hat) + ADAM_EPS) + ADAM_WD * w), m, v


def _ada_fwd(c_all, ada_w, name):
    nl, d, cols = ada_w.shape
    b = c_all.shape[0]

    def body(c_ref, w_ref, o_ref):
        o_ref[...] = _mm_hi(_silu(c_ref[...]), w_ref[...])

    return pl.pallas_call(
        body, name=name, grid=(nl,),
        in_specs=[pl.BlockSpec((b, d), lambda i: (0, 0)), pl.BlockSpec((None, d, cols), lambda i: (i, 0, 0))],
        out_specs=pl.BlockSpec((None, b, cols), lambda i: (i, 0, 0)),
        out_shape=jax.ShapeDtypeStruct((nl, b, cols), F32), compiler_params=_params(("parallel",)))(c_all, ada_w)


def _ada_bwd(c_col, dm, w, m, v, name):
    nl, d, cols = w.shape
    b = c_col.shape[0]
    tr = _tile(d, (256, 128))

    def body(c_ref, dm_ref, w_ref, m_ref, v_ref, g_ref, dl_ref, mo_ref, vo_ref):
        g = _silu(c_ref[0]) * dm_ref[pl.ds(0, 1), :]
        for j in range(1, b):
            g = g + _silu(c_ref[j]) * dm_ref[pl.ds(j, 1), :]
        g_ref[...] = g
        dl_ref[...], mo_ref[...], vo_ref[...] = _adamw(g, w_ref[...], m_ref[...], v_ref[...])

    blk = pl.BlockSpec((None, tr, cols), lambda l, i: (l, i, 0))
    shape = jax.ShapeDtypeStruct((nl, d, cols), F32)
    return pl.pallas_call(
        body, name=name, grid=(nl, d // tr),
        in_specs=[pl.BlockSpec((b, tr, 1), lambda l, i: (0, i, 0)), pl.BlockSpec((None, b, cols), lambda l, i: (l, 0, 0)),
                  blk, blk, blk],
        out_specs=[blk, blk, blk, blk], out_shape=[shape] * 4,
        compiler_params=_params(("parallel", "parallel")))(c_col, dm, w, m, v)


def _sum_adam(parts, w, m, v, name):
    nl, npart, r, cdim = parts.shape
    tr = _tile(r, (256, 128))

    def body(p_ref, w_ref, m_ref, v_ref, g_ref, dl_ref, mo_ref, vo_ref):
        g = p_ref[0].astype(F32)
        for j in range(1, npart):
            g = g + p_ref[j].astype(F32)
        g_ref[...] = g
        dl_ref[...], mo_ref[...], vo_ref[...] = _adamw(g, w_ref[...], m_ref[...], v_ref[...])

    blk = pl.BlockSpec((None, tr, cdim), lambda l, i: (l, i, 0))
    shape = jax.ShapeDtypeStruct((nl, r, cdim), F32)
    return pl.pallas_call(
        body, name=name, grid=(nl, r // tr),
        in_specs=[pl.BlockSpec((None, npart, tr, cdim), lambda l, i: (l, 0, i, 0)), blk, blk, blk],
        out_specs=[blk, blk, blk, blk], out_shape=[shape] * 4,
        compiler_params=_params(("parallel", "parallel")))(parts, w, m, v)


def _cols_from_blocks(g, plan, width, name):
    _, r, cdim = g.shape
    tr = _tile(r, (256, 128))
    covered = sorted((dst, dst + n) for _, _, n, dst in plan)
    holes, pos = [], 0
    for a, b in covered:
        if a > pos:
            holes.append((pos, a))
        pos = max(pos, b)
    if pos < width:
        holes.append((pos, width))

    def body(g_ref, o_ref):
        for a, b in holes:
            o_ref[:, a:b] = jnp.zeros((tr, b - a), o_ref.dtype)
        for j, src, n, dst in plan:
            o_ref[:, dst:dst + n] = g_ref[j, :, src:src + n]

    return pl.pallas_call(
        body, name=name, grid=(r // tr,), in_specs=[pl.BlockSpec((N_DEV, tr, cdim), lambda i: (0, i, 0))],
        out_specs=pl.BlockSpec((tr, width), lambda i: (i, 0)), out_shape=jax.ShapeDtypeStruct((r, width), g.dtype),
        compiler_params=_params(("parallel",)))(g)


def _blocks_from_cols(w, plan, cdim, name):
    r, width = w.shape
    tr = _tile(r, (256, 128))

    def body(w_ref, o_ref):
        for j, src, n, dst in plan:
            o_ref[j, :, src:src + n] = w_ref[:, dst:dst + n]

    return pl.pallas_call(
        body, name=name, grid=(r // tr,), in_specs=[pl.BlockSpec((tr, width), lambda i: (i, 0))],
        out_specs=pl.BlockSpec((N_DEV, tr, cdim), lambda i: (0, i, 0)),
        out_shape=jax.ShapeDtypeStruct((N_DEV, r, cdim), w.dtype), compiler_params=_params(("parallel",)))(w)


def _pair_sum(x, tmp, core, name):
    _, r, cdim = x.shape
    tr = _tile(r, (256, 128))

    def body(core_ref, x_ref, t_ref, o_ref):
        o_ref[...] = (x_ref[...] + t_ref[...]).astype(o_ref.dtype)

    grid_spec = pltpu.PrefetchScalarGridSpec(
        num_scalar_prefetch=1, grid=(N_DEV // 2, r // tr),
        in_specs=[pl.BlockSpec((None, tr, cdim), lambda ch, i, core_ref: (2 * ch + core_ref[0], i, 0)),
                  pl.BlockSpec((None, tr, cdim), lambda ch, i, core_ref: (ch, i, 0))],
        out_specs=pl.BlockSpec((None, tr, cdim), lambda ch, i, core_ref: (ch, i, 0)))
    return pl.pallas_call(
        body, name=name, grid_spec=grid_spec, out_shape=jax.ShapeDtypeStruct((N_DEV // 2, r, cdim), WIRE_DTYPE),
        compiler_params=_params(("parallel", "parallel")))(core, x, tmp)


_ANY = pl.BlockSpec(memory_space=pl.ANY)
_CHIP_FLIPS = ((1, 0), (0, 1), (1, 1))


def _coords():
    return lax.axis_index("x"), lax.axis_index("y"), lax.axis_index("c")


def _flip(v, f):
    return 1 - v if f else v


def _a2a_direct(xs, name):
    n, ncp = len(xs), N_DEV - 1

    def body(*refs):
        ins, outs = refs[:n], refs[n:2 * n]
        send, recv, loc = refs[2 * n:]
        x, y, c = _coords()
        me = 4 * x + 2 * y + c
        local = [pltpu.make_async_copy(ins[i].at[me], outs[i].at[me], loc.at[i]) for i in range(n)]
        for cp in local:
            cp.start()
        remote = []
        for i in range(n):
            for k in range(1, N_DEV):
                px, py, pc = _flip(x, k & 4), _flip(y, k & 2), _flip(c, k & 1)
                cp = pltpu.make_async_remote_copy(
                    src_ref=ins[i].at[4 * px + 2 * py + pc], dst_ref=outs[i].at[me],
                    send_sem=send.at[i * ncp + k - 1], recv_sem=recv.at[i * ncp + k - 1],
                    device_id=(px, py, pc), device_id_type=MESH)
                cp.start()
                remote.append(cp)
        for cp in remote:
            cp.wait()
        for cp in local:
            cp.wait()

    return pl.pallas_call(
        body, name=name, in_specs=[_ANY] * n, out_specs=[_ANY] * n,
        out_shape=[jax.ShapeDtypeStruct(a.shape, a.dtype) for a in xs],
        scratch_shapes=[pltpu.SemaphoreType.DMA((n * ncp,)), pltpu.SemaphoreType.DMA((n * ncp,)),
                        pltpu.SemaphoreType.DMA((n,))])(*xs)


class _AllGatherSide:
    def __init__(self, blocks):
        self.operands = list(blocks)
        n = self.n = len(self.operands)
        self.n_in = self.n_out = n
        self.out_shape = [jax.ShapeDtypeStruct((N_DEV,) + a.shape, a.dtype) for a in self.operands]
        self.aliases = {}
        nici, nd2d = len(_CHIP_FLIPS), N_DEV // 2
        self.scratch = [pltpu.SemaphoreType.DMA((n * nici,)), pltpu.SemaphoreType.DMA((n * nici,)),
                        pltpu.SemaphoreType.DMA((n * nd2d,)), pltpu.SemaphoreType.DMA((n * nd2d,)),
                        pltpu.SemaphoreType.DMA((n,))]

    def _first(self, ins, outs, sems):
        send, recv, _, _, loc = sems
        x, y, c = _coords()
        me = 4 * x + 2 * y + c
        nici = len(_CHIP_FLIPS)
        local = [pltpu.make_async_copy(ins[i], outs[i].at[me], loc.at[i]) for i in range(self.n)]
        remote = [pltpu.make_async_remote_copy(
            src_ref=ins[i], dst_ref=outs[i].at[me], send_sem=send.at[i * nici + j], recv_sem=recv.at[i * nici + j],
            device_id=(_flip(x, fx), _flip(y, fy), c), device_id_type=MESH)
            for i in range(self.n) for j, (fx, fy) in enumerate(_CHIP_FLIPS)]
        return local + remote

    def _second(self, outs, sems):
        _, _, send, recv, _ = sems
        x, y, c = _coords()
        nd2d = N_DEV // 2
        return [pltpu.make_async_remote_copy(
            src_ref=outs[i].at[2 * ch + c], dst_ref=outs[i].at[2 * ch + c], send_sem=send.at[i * nd2d + ch],
            recv_sem=recv.at[i * nd2d + ch], device_id=(x, y, 1 - c), device_id_type=MESH)
            for i in range(self.n) for ch in range(nd2d)]

    def start(self, ins, outs, sems):
        for cp in self._first(ins, outs, sems):
            cp.start()

    def finish(self, ins, outs, sems):
        for cp in self._first(ins, outs, sems):
            cp.wait()
        second = self._second(outs, sems)
        for cp in second:
            cp.start()
        for cp in second:
            cp.wait()


class _ReduceScatterIciSide:
    def __init__(self, sums, accs, layer):
        self.operands = list(sums) + list(accs)
        n = self.n = len(sums)
        self.layer = layer
        self.n_in, self.n_out = 2 * n, n
        self.out_shape = [jax.ShapeDtypeStruct(a.shape, a.dtype) for a in accs]
        self.aliases = {n + i: i for i in range(n)}
        nici = len(_CHIP_FLIPS)
        self.scratch = [pltpu.SemaphoreType.DMA((n * nici,)), pltpu.SemaphoreType.DMA((n * nici,)),
                        pltpu.SemaphoreType.DMA((n,))]

    def _copies(self, ins, outs, sems):
        send, recv, loc = sems
        x, y, c = _coords()
        chip = 2 * x + y
        nici = len(_CHIP_FLIPS)
        local = [pltpu.make_async_copy(ins[i].at[chip], outs[i].at[self.layer, chip], loc.at[i])
                 for i in range(self.n)]
        remote = [pltpu.make_async_remote_copy(
            src_ref=ins[i].at[2 * _flip(x, fx) + _flip(y, fy)], dst_ref=outs[i].at[self.layer, chip],
            send_sem=send.at[i * nici + j], recv_sem=recv.at[i * nici + j],
            device_id=(_flip(x, fx), _flip(y, fy), c), device_id_type=MESH)
            for i in range(self.n) for j, (fx, fy) in enumerate(_CHIP_FLIPS)]
        return local + remote

    def start(self, ins, outs, sems):
        for cp in self._copies(ins, outs, sems):
            cp.start()

    def finish(self, ins, outs, sems):
        for cp in self._copies(ins, outs, sems):
            cp.wait()


def _run_side(side, name):
    def body(*refs):
        ins, outs = refs[:side.n_in], refs[side.n_in:side.n_in + side.n_out]
        sems = refs[side.n_in + side.n_out:]
        side.start(ins, outs, sems)
        side.finish(ins, outs, sems)

    return pl.pallas_call(
        body, name=name, in_specs=[_ANY] * side.n_in, out_specs=[_ANY] * side.n_out, out_shape=side.out_shape,
        input_output_aliases=side.aliases, scratch_shapes=side.scratch)(*side.operands)


class _ReduceScatterD2dSide:
    def __init__(self, parts):
        self.operands = list(parts)
        n = self.n = len(self.operands)
        self.n_in = self.n_out = n
        nd2d = N_DEV // 2
        self.out_shape = [jax.ShapeDtypeStruct((nd2d,) + a.shape[1:], a.dtype) for a in self.operands]
        self.aliases = {}
        self.scratch = [pltpu.SemaphoreType.DMA((n * nd2d,)), pltpu.SemaphoreType.DMA((n * nd2d,))]

    def _copies(self, ins, outs, sems):
        send, recv = sems
        x, y, c = _coords()
        nd2d = N_DEV // 2
        return [pltpu.make_async_remote_copy(
            src_ref=ins[i].at[2 * ch + 1 - c], dst_ref=outs[i].at[ch], send_sem=send.at[i * nd2d + ch],
            recv_sem=recv.at[i * nd2d + ch], device_id=(x, y, 1 - c), device_id_type=MESH)
            for i in range(self.n) for ch in range(nd2d)]

    def start(self, ins, outs, sems):
        for cp in self._copies(ins, outs, sems):
            cp.start()

    def finish(self, ins, outs, sems):
        for cp in self._copies(ins, outs, sems):
            cp.wait()


_PACK_ROWS = 256


def _pack(arrs):
    flat = jnp.concatenate([a.reshape(-1) for a in arrs])
    quantum = _PACK_ROWS * LANES
    total = -(-flat.shape[0] // quantum) * quantum
    return jnp.pad(flat, (0, total - flat.shape[0])).reshape(-1, LANES)


def _unpack(packed, like):
    flat, out, pos = packed.reshape(-1), [], 0
    for a in like:
        out.append(flat[pos:pos + a.size].reshape(a.shape))
        pos += a.size
    return out


def kernel(x, c, ada_w, ada_b, norm1_g, w_in, conv_w, spatial_w, spatial_b, v_norm_g, a_log, dt_bias, o_norm_g, w_branch_a, w_branch_b, w_out, norm2_g, w_ffn_in, w_ffn_out, final_g, loss_target, m_ada_w, m_ada_b, m_norm1_g, m_w_in, m_conv_w, m_spatial_w, m_spatial_b, m_v_norm_g, m_a_log, m_dt_bias, m_o_norm_g, m_w_branch_a, m_w_branch_b, m_w_out, m_norm2_g, m_w_ffn_in, m_w_ffn_out, m_final_g, v_ada_w, v_ada_b, v_norm1_g, v_w_in, v_conv_w, v_spatial_w, v_spatial_b, v_v_norm_g, v_a_log, v_dt_bias, v_o_norm_g, v_w_branch_a, v_w_branch_b, v_w_out, v_norm2_g, v_w_ffn_in, v_w_ffn_out, v_final_g):
    nl, d = ada_w.shape[0], x.shape[2]
    t = x.shape[1]
    nchunk = t // GDN_CHUNK
    xi, yi, ci = _coords()
    me = 4 * xi + 2 * yi + ci
    core = jnp.reshape(ci, (1,)).astype(jnp.int32)
    x0, target = x[0], loss_target[0]
    wcols = 3 * HEADS * HEAD_DIM
    lay = _ProjLayout(d)
    in_pieces = lay.pieces(w_in.shape[2])
    fi_shard = w_ffn_in.shape[2]
    fi_pieces = [(j, 0, fi_shard, fi_shard * j) for j in range(N_DEV)]

    c_all, cw_all = _a2a_direct([jnp.broadcast_to(c[None], (N_DEV,) + c.shape),
                                 jnp.broadcast_to(conv_w[None], (N_DEV,) + conv_w.shape)], "gather_small")
    c_all = c_all[:, 0]
    conv_full = cw_all.transpose(1, 2, 0, 3).reshape(nl, CONV_K, wcols)
    modp = _ada_fwd(c_all, ada_w, "ada_fwd")
    (modx,) = _a2a_direct([modp.transpose(1, 0, 2)], "mod_exchange")
    mod = (modx.transpose(1, 0, 2).reshape(nl, 6 * d) + ada_b).reshape(nl, 6, 1, d)

    big = (w_in, w_branch_a, w_branch_b, w_out, w_ffn_in, w_ffn_out)
    big_wire = [w.astype(WIRE_DTYPE) for w in big]
    gather_in = lambda i: _AllGatherSide([big_wire[0][i]])
    gather_rest = lambda i: _AllGatherSide([w[i] for w in big_wire[1:]])
    row_full = lambda g: g.reshape(-1, g.shape[2])
    padded_in = lambda g: _cols_from_blocks(g, in_pieces, lay.width, "w_in_cols")
    w_pads = [padded_in(_run_side(gather_in(0), "ag_first")[0])] + [None] * (nl - 1)
    weights = [None] * nl

    def rows_of(proj, lo):
        return proj[:, lo:lo + HEADS].T.reshape(HEADS, nchunk, 1, GDN_CHUNK)

    saved = []
    x_cur, delta, gt_prev = x0, None, None
    for i in range(nl):
        sh1, sc1, gt1, sh2, sc2, gt2 = (mod[i, k] for k in range(6))
        s = dict(gt1=gt1, gt2=gt2, sc1=sc1, sc2=sc2)
        s["x_in"], s["h"] = _resid_norm(x_cur, delta, gt_prev, norm1_g[i][None], sc1, sh1, "norm1_fwd")
        s["proj"], g_a, g_b, g_o, g_fi, g_fo = _matmul(s["h"], w_pads[i], "nn", "proj_fwd", side=gather_rest(i))
        weights[i] = (row_full(g_a), row_full(g_b), row_full(g_o),
                      _cols_from_blocks(g_fi, fi_pieces, N_DEV * fi_shard, "w_ffn_in_cols"), row_full(g_fo))
        w_a, w_b, w_o, w_fi, w_fo = weights[i]
        s["b_col"] = spatial_b[i][:, :, None]
        s["ya"] = _mixer_a_fwd(s["proj"], lay.uv, spatial_w[i], s["b_col"], v_norm_g[i][None], "mixer_a_fwd")
        s["qkv_h"] = _conv_fwd(s["proj"], conv_full[i], "conv_fwd")
        s["braw"], s["araw"] = rows_of(s["proj"], lay.ba), rows_of(s["proj"], lay.ba + HEADS)
        s["o"], s["states"], *gathered = _gdn_fwd(s["qkv_h"], s["araw"], s["braw"], a_log[i], dt_bias[i],
                                                  "gdn_fwd", gather_in(i + 1) if i + 1 < nl else _NoSide)
        if gathered:
            w_pads[i + 1] = padded_in(gathered[0])
        s["yb"] = _gdn_post_fwd(s["o"], s["proj"], lay.z, o_norm_g[i][None], "gdn_post_fwd")
        s["pa"] = _matmul(s["ya"], w_a, "nn", "branch_a_fwd")
        s["pb"] = _matmul(s["yb"], w_b, "nn", "branch_b_fwd")
        s["merged"] = _merge_fwd(s["pa"], s["pb"], s["proj"], lay.gates, "merge_fwd")
        s["mo"] = _matmul(s["merged"], w_o, "nn", "out_fwd")
        s["x1"], s["h2"] = _resid_norm(s["x_in"], s["mo"], gt1, norm2_g[i][None], sc2, sh2, "norm2_fwd")
        s["gu"] = _matmul(s["h2"], w_fi, "nn", "ffn_in_fwd")
        s["a"] = _swiglu_fwd(s["gu"], "swiglu_fwd")
        s["fo"] = _matmul(s["a"], w_fo, "nn", "ffn_out_fwd")
        saved.append(s)
        x_cur, delta, gt_prev = s["x1"], s["fo"], gt2
    dx, d_final_g, loss_tile = _final_loss(x_cur, delta, gt_prev, final_g[None], target, "final_loss")
    loss = lax.psum(loss_tile[0, 0], ("x", "y", "c"))

    big_shapes = [(d, w_in.shape[2]), w_branch_a.shape[1:], w_branch_b.shape[1:], w_out.shape[1:],
                  (d, w_ffn_in.shape[2]), w_ffn_out.shape[1:]]
    accs = [jnp.zeros((nl, N_DEV // 2) + tuple(sh), WIRE_DTYPE) for sh in big_shapes]
    row_blocks = lambda g: g.reshape(N_DEV, -1, g.shape[1])
    dmod, small = [None] * nl, [None] * nl
    d_conv = [None] * nl
    parts, scatter = None, _NoSide
    for i in reversed(range(nl)):
        s = saved[i]
        w_a, w_b, w_o, w_fi, w_fo = weights[i]
        dfo, dgt2 = _gate_bwd(dx, s["fo"], s["gt2"], "gate2_bwd")
        g_fo = _matmul(s["a"], dfo, "tn", "ffn_out_dw")
        da = _matmul(dfo, w_fo, "nt", "ffn_out_dx")
        dgu = _swiglu_bwd(s["gu"], da, "swiglu_bwd")
        if parts is None:
            g_fi = _matmul(s["h2"], dgu, "tn", "ffn_in_dw")
        else:
            g_fi, *other = _matmul(s["h2"], dgu, "tn", "ffn_in_dw", side=_ReduceScatterD2dSide(parts))
            sums = [_pair_sum(p, o, core, "rs_pair_sum_%d" % k) for k, (p, o) in enumerate(zip(parts, other))]
            scatter = _ReduceScatterIciSide(sums, accs, i + 1)
        dh2 = _matmul(dgu, w_fi, "nt", "ffn_in_dx")
        dx1, dsh2, dsc2, dg2 = _norm_bwd(s["x1"], dh2, dx, norm2_g[i][None], s["sc2"], "norm2_bwd")
        dmo, dgt1 = _gate_bwd(dx1, s["mo"], s["gt1"], "gate1_bwd")
        g_o = _matmul(s["merged"], dmo, "tn", "out_dw")
        dmerged = _matmul(dmo, w_o, "nt", "out_dx")
        dproj = jnp.zeros((t, lay.width), MXU_DTYPE)
        dpa, dpb, dproj = _merge_bwd(dmerged, s["pa"], s["pb"], s["proj"], lay.gates, dproj, "merge_bwd")
        g_a = _matmul(s["ya"], dpa, "tn", "branch_a_dw")
        dya = _matmul(dpa, w_a, "nt", "branch_a_dx")
        g_b = _matmul(s["yb"], dpb, "tn", "branch_b_dw")
        dyb = _matmul(dpb, w_b, "nt", "branch_b_dx")
        dproj, d_ws, d_bs, d_gv = _mixer_a_bwd(s["proj"], lay.uv, dya, spatial_w[i], jnp.swapaxes(spatial_w[i], 1, 2),
                                               s["b_col"], v_norm_g[i][None], dproj, "mixer_a_bwd")
        do, dproj, d_go = _gdn_post_bwd(s["o"], s["proj"], lay.z, dyb, o_norm_g[i][None], dproj, "gdn_post_bwd")
        dq, dk, dv, d_ar, d_br, d_al, d_dt, *scattered = _gdn_bwd(
            s["qkv_h"], s["araw"], s["braw"], a_log[i], dt_bias[i], s["states"], do, "gdn_bwd", scatter)
        if scattered:
            accs = scattered
        dacc, d_conv[i] = _conv_bwd_pre(s["proj"], dq, dk, dv, conv_full[i], "conv_bwd_pre")
        dproj = _conv_bwd_in(dacc, conv_full[i], dproj, "conv_bwd_in")
        cols = lambda r: r.reshape(HEADS, t).T
        dba = jnp.pad(jnp.concatenate([cols(d_br), cols(d_ar)], axis=1), ((0, 0), (0, LANES - 2 * HEADS)))
        dproj = _write_cols(dproj, dba, lay.ba, "dproj_ba")
        g_pad = _matmul(s["h"], dproj, "tn", "proj_dw")
        dh = _matmul(dproj, w_pads[i], "nt", "proj_dx")
        dx, dsh1, dsc1, dg1 = _norm_bwd(s["x_in"], dh, dx1, norm1_g[i][None], s["sc1"], "norm1_bwd")
        dmod[i] = jnp.concatenate([dsh1, dsc1, dgt1, dsh2, dsc2, dgt2], axis=1)[0]
        small[i] = (dg1[0], d_ws, d_bs[:, :, 0], d_gv[0], d_al[:, 0, 0], d_dt[:, 0, 0], d_go[0], dg2[0])
        parts = [_blocks_from_cols(g_pad, in_pieces, w_in.shape[2], "w_in_blocks"), row_blocks(g_a), row_blocks(g_b),
                 row_blocks(g_o), _blocks_from_cols(g_fi, fi_pieces, fi_shard, "w_ffn_in_blocks"), row_blocks(g_fo)]
    other = _run_side(_ReduceScatterD2dSide(parts), "rs_d2d_last")
    sums = [_pair_sum(p, o, core, "rs_pair_sum_%d" % k) for k, (p, o) in enumerate(zip(parts, other))]
    accs = _run_side(_ReduceScatterIciSide(sums, accs, 0), "rs_ici_last")

    dmod = jnp.stack(dmod)
    sm = [jnp.stack([small[i][k] for i in range(nl)]) for k in range(8)]
    rep_w = (ada_b, norm1_g, spatial_w, spatial_b, v_norm_g, a_log, dt_bias, o_norm_g, norm2_g, final_g)
    rep_m = (m_ada_b, m_norm1_g, m_spatial_w, m_spatial_b, m_v_norm_g, m_a_log, m_dt_bias, m_o_norm_g, m_norm2_g, m_final_g)
    rep_v = (v_ada_b, v_norm1_g, v_spatial_w, v_spatial_b, v_v_norm_g, v_a_log, v_dt_bias, v_o_norm_g, v_norm2_g, v_final_g)
    rep_g = (dmod, sm[0], sm[1], sm[2], sm[3], sm[4], sm[5], sm[6], sm[7], d_final_g[0])
    packed = _pack(rep_g)
    d_conv_blocks = jnp.stack(d_conv).reshape(nl, CONV_K, N_DEV, -1).transpose(2, 0, 1, 3).reshape(N_DEV, -1, LANES)
    dmod_blocks = dmod.reshape(nl, N_DEV, -1).transpose(1, 0, 2)
    rep_all, conv_all, dmod_all = _a2a_direct(
        [jnp.broadcast_to(packed[None], (N_DEV,) + packed.shape), d_conv_blocks, dmod_blocks], "small_grads")
    rep_out = _sum_adam(rep_all[None], _pack(rep_w)[None], _pack(rep_m)[None], _pack(rep_v)[None], "adam_small")
    rep_out = [_unpack(o[0], rep_w) for o in rep_out]
    conv_out = _sum_adam(conv_all[None], conv_w.reshape(1, -1, LANES), m_conv_w.reshape(1, -1, LANES),
                         v_conv_w.reshape(1, -1, LANES), "adam_conv")
    conv_out = [o.reshape(conv_w.shape) for o in conv_out]
    ada_out = _ada_bwd(c_all[:, :, None], dmod_all.transpose(1, 0, 2), ada_w, m_ada_w, v_ada_w, "ada_bwd_adam")
    big_m = (m_w_in, m_w_branch_a, m_w_branch_b, m_w_out, m_w_ffn_in, m_w_ffn_out)
    big_v = (v_w_in, v_w_branch_a, v_w_branch_b, v_w_out, v_w_ffn_in, v_w_ffn_out)
    big_out = [_sum_adam(accs[k], big[k], big_m[k], big_v[k], "adam_big_%d" % k) for k in range(6)]

    def ordered(kind):
        rep = rep_out[kind]
        return (ada_out[kind], rep[0], rep[1], big_out[0][kind], conv_out[kind], rep[2], rep[3], rep[4], rep[5],
                rep[6], rep[7], big_out[1][kind], big_out[2][kind], big_out[3][kind], rep[8], big_out[4][kind],
                big_out[5][kind], rep[9])

    return (loss, dx[None]) + ordered(0) + ordered(1) + ordered(2) + ordered(3)
```

```python
import functools

import jax
import jax.numpy as jnp
from jax import lax
from jax.experimental import pallas as pl
from jax.experimental.pallas import tpu as pltpu

F32 = jnp.float32
BF16 = jnp.bfloat16
MXU_DTYPE = BF16
WIRE_DTYPE = BF16
EPS = 1e-6
LANES = 128
SUBLANES = 8
GDN_CHUNK = 128
A_CHUNK = 128
GROUPS = 8
HEADS = 8
HEAD_DIM = 128
CONV_K = 4
N_DEV = 8
VMEM_LIMIT = 48 * 1024 * 1024
MESH = pl.DeviceIdType.MESH

ADAM_LR = 0.001
ADAM_B1 = 0.9
ADAM_B2 = 0.999
ADAM_EPS = 1e-08
ADAM_WD = 0.01
ADAM_STEP = 10

_NN = (((1,), (0,)), ((), ()))
_NT = (((1,), (1,)), ((), ()))
_TN = (((0,), (0,)), ((), ()))


def _mm(a, b, dims=_NN):
    return lax.dot_general(a.astype(MXU_DTYPE), b.astype(MXU_DTYPE), dims, preferred_element_type=F32)


def _mm_hi(a, b):
    return lax.dot_general(a, b, _NN, precision=lax.Precision.HIGHEST, preferred_element_type=F32)


def _tile(n, cands):
    for c in cands:
        if n % c == 0:
            return c
    return n


def _params(sem=None):
    return pltpu.CompilerParams(dimension_semantics=sem, vmem_limit_bytes=VMEM_LIMIT)


def _sigmoid(x):
    return 1.0 / (1.0 + jnp.exp(-x))


def _silu(x):
    return x * _sigmoid(x)


def _dsilu(x):
    s = _sigmoid(x)
    return s * (1.0 + x * (1.0 - s))


_GELU_C = 0.7978845608028654
_GELU_A = 0.044715


def _gelu(x):
    return 0.5 * x * (1.0 + jnp.tanh(_GELU_C * (x + _GELU_A * x * x * x)))


def _dgelu(x):
    t = jnp.tanh(_GELU_C * (x + _GELU_A * x * x * x))
    return 0.5 * (1.0 + t) + 0.5 * x * (1.0 - t * t) * _GELU_C * (1.0 + 3.0 * _GELU_A * x * x)


def _softplus(x):
    return jnp.maximum(x, 0.0) + jnp.log(1.0 + jnp.exp(-jnp.abs(x)))


_MM_TILES = (1024, 1408, 1664, 512, 256, 128)


class _NoSide:
    operands, out_shape, scratch, aliases, n_in, n_out = [], [], [], {}, 0, 0


def _side_hooks(side, refs, n_main_in, n_main_out, n_main_scratch, grid):
    a = n_main_in + side.n_in
    b = a + n_main_out + side.n_out
    ins, outs, sems = refs[n_main_in:a], refs[a + n_main_out:b], refs[b + n_main_scratch:]
    main = refs[:n_main_in] + refs[a:a + n_main_out] + refs[b:b + n_main_scratch]
    ids = [pl.program_id(k) for k in range(len(grid))]

    def start():
        if side.n_in:
            pl.when(functools.reduce(jnp.logical_and, [i == 0 for i in ids]))(lambda: side.start(ins, outs, sems))

    def finish():
        if side.n_in:
            last = functools.reduce(jnp.logical_and, [i == g - 1 for i, g in zip(ids, grid)])
            pl.when(last)(lambda: side.finish(ins, outs, sems))

    return main, start, finish


def _carrier_call(body, name, grid, in_specs, out_specs, out_shape, scratch, side, args):
    aliases = {len(in_specs) + k: len(out_specs) + v for k, v in side.aliases.items()}
    return pl.pallas_call(
        body, name=name, grid=grid, in_specs=list(in_specs) + [_ANY] * side.n_in,
        out_specs=list(out_specs) + [_ANY] * side.n_out, out_shape=list(out_shape) + list(side.out_shape),
        scratch_shapes=list(scratch) + list(side.scratch), input_output_aliases=aliases,
        compiler_params=_params(("arbitrary",) * len(grid)))(*args, *side.operands)


def _matmul(a, b, mode, name, out_dtype=F32, side=_NoSide):
    if mode == "nn":
        (m, k), n = a.shape, b.shape[1]
    elif mode == "nt":
        (m, k), n = a.shape, b.shape[0]
    else:
        (k, m), n = a.shape, b.shape[1]
    tm, tn, tk = _tile(m, _MM_TILES), _tile(n, _MM_TILES), _tile(k, _MM_TILES)
    nk = k // tk
    grid = (m // tm, n // tn, nk)
    dims = {"nn": _NN, "nt": _NT, "tn": _TN}[mode]

    def body(*refs):
        (a_ref, b_ref, o_ref, acc_ref), side_start, side_finish = _side_hooks(side, refs, 2, 1, 1, grid)
        kk = pl.program_id(2)
        side_start()

        @pl.when(kk == 0)
        def _():
            acc_ref[...] = jnp.zeros_like(acc_ref)

        acc_ref[...] += _mm(a_ref[...], b_ref[...], dims)

        @pl.when(kk == nk - 1)
        def _():
            o_ref[...] = acc_ref[...].astype(o_ref.dtype)

        side_finish()

    a_spec = (pl.BlockSpec((tk, tm), lambda i, j, l: (l, i)) if mode == "tn"
              else pl.BlockSpec((tm, tk), lambda i, j, l: (i, l)))
    b_spec = (pl.BlockSpec((tn, tk), lambda i, j, l: (j, l)) if mode == "nt"
              else pl.BlockSpec((tk, tn), lambda i, j, l: (l, j)))
    o_spec = pl.BlockSpec((tm, tn), lambda i, j, l: (i, j))
    out = _carrier_call(body, name, grid, [a_spec, b_spec], [o_spec], [jax.ShapeDtypeStruct((m, n), out_dtype)],
                        [pltpu.VMEM((tm, tn), F32)], side, (a, b))
    return out if side.n_in else out[0]


_ROW_TILES = (512, 256, 128)


def _resid_norm(x, delta, gt, g, sc, sh, name):
    t, d = x.shape
    tt = _tile(t, _ROW_TILES)
    has = delta is not None

    def body(*refs):
        if has:
            x_ref, d_ref, gt_ref, g_ref, sc_ref, sh_ref, xo_ref, h_ref = refs
            xv = x_ref[...] + gt_ref[...] * d_ref[...]
            xo_ref[...] = xv
        else:
            x_ref, g_ref, sc_ref, sh_ref, h_ref = refs
            xv = x_ref[...]
        r = lax.rsqrt(jnp.mean(xv * xv, axis=-1, keepdims=True) + EPS)
        y = xv * r * g_ref[...]
        h_ref[...] = (y * (1.0 + sc_ref[...]) + sh_ref[...]).astype(h_ref.dtype)

    row = pl.BlockSpec((tt, d), lambda i: (i, 0))
    vec = pl.BlockSpec((1, d), lambda i: (0, 0))
    if has:
        return pl.pallas_call(
            body, name=name, grid=(t // tt,), in_specs=[row, row, vec, vec, vec, vec], out_specs=[row, row],
            out_shape=[jax.ShapeDtypeStruct((t, d), F32), jax.ShapeDtypeStruct((t, d), MXU_DTYPE)],
            compiler_params=_params(("parallel",)))(x, delta, gt, g, sc, sh)
    h = pl.pallas_call(
        body, name=name + "_first", grid=(t // tt,), in_specs=[row, vec, vec, vec], out_specs=row,
        out_shape=jax.ShapeDtypeStruct((t, d), MXU_DTYPE), compiler_params=_params(("parallel",)))(x, g, sc, sh)
    return x, h


def _final_loss(x, delta, gt, g, target, name):
    t, d = x.shape
    tt = _tile(t, _ROW_TILES)

    def body(x_ref, d_ref, gt_ref, g_ref, tg_ref, dx_ref, dg_ref, loss_ref):
        @pl.when(pl.program_id(0) == 0)
        def _():
            dg_ref[...] = jnp.zeros_like(dg_ref)
            loss_ref[...] = jnp.zeros_like(loss_ref)

        xv = x_ref[...] + gt_ref[...] * d_ref[...]
        r = lax.rsqrt(jnp.mean(xv * xv, axis=-1, keepdims=True) + EPS)
        xh = xv * r
        diff = xh * g_ref[...] - tg_ref[...]
        loss_ref[...] += jnp.sum(diff * diff) * (0.5 / d)
        dy = diff * (1.0 / d)
        dg_ref[...] += jnp.sum(dy * xh, axis=0, keepdims=True)
        dxh = dy * g_ref[...]
        dx_ref[...] = r * (dxh - xh * jnp.mean(dxh * xh, axis=-1, keepdims=True))

    row = pl.BlockSpec((tt, d), lambda i: (i, 0))
    vec = pl.BlockSpec((1, d), lambda i: (0, 0))
    tile = pl.BlockSpec((SUBLANES, LANES), lambda i: (0, 0))
    return pl.pallas_call(
        body, name=name, grid=(t // tt,), in_specs=[row, row, vec, vec, row], out_specs=[row, vec, tile],
        out_shape=[jax.ShapeDtypeStruct((t, d), F32), jax.ShapeDtypeStruct((1, d), F32),
                   jax.ShapeDtypeStruct((SUBLANES, LANES), F32)],
        compiler_params=_params(("arbitrary",)))(x, delta, gt, g, target)


def _norm_bwd(x, dh, dres, g, sc, name):
    t, d = x.shape
    tt = _tile(t, _ROW_TILES)

    def body(x_ref, dh_ref, dr_ref, g_ref, sc_ref, dx_ref, dsh_ref, dsc_ref, dg_ref):
        @pl.when(pl.program_id(0) == 0)
        def _():
            dsh_ref[...] = jnp.zeros_like(dsh_ref)
            dsc_ref[...] = jnp.zeros_like(dsc_ref)
            dg_ref[...] = jnp.zeros_like(dg_ref)

        xv, dh = x_ref[...], dh_ref[...]
        r = lax.rsqrt(jnp.mean(xv * xv, axis=-1, keepdims=True) + EPS)
        xh = xv * r
        gv, sc1 = g_ref[...], 1.0 + sc_ref[...]
        dsh_ref[...] += jnp.sum(dh, axis=0, keepdims=True)
        dsc_ref[...] += jnp.sum(dh * xh, axis=0, keepdims=True) * gv
        dg_ref[...] += jnp.sum(dh * xh, axis=0, keepdims=True) * sc1
        dxh = dh * (gv * sc1)
        dx_ref[...] = dr_ref[...] + r * (dxh - xh * jnp.mean(dxh * xh, axis=-1, keepdims=True))

    row = pl.BlockSpec((tt, d), lambda i: (i, 0))
    vec = pl.BlockSpec((1, d), lambda i: (0, 0))
    vshape = jax.ShapeDtypeStruct((1, d), F32)
    return pl.pallas_call(
        body, name=name, grid=(t // tt,), in_specs=[row, row, row, vec, vec], out_specs=[row, vec, vec, vec],
        out_shape=[jax.ShapeDtypeStruct((t, d), F32), vshape, vshape, vshape],
        compiler_params=_params(("arbitrary",)))(x, dh, dres, g, sc)


def _gate_bwd(dxo, branch, gt, name):
    t, d = dxo.shape
    tt = _tile(t, _ROW_TILES)

    def body(dx_ref, br_ref, gt_ref, db_ref, dgt_ref):
        @pl.when(pl.program_id(0) == 0)
        def _():
            dgt_ref[...] = jnp.zeros_like(dgt_ref)

        dx = dx_ref[...]
        db_ref[...] = (dx * gt_ref[...]).astype(db_ref.dtype)
        dgt_ref[...] += jnp.sum(dx * br_ref[...], axis=0, keepdims=True)

    row = pl.BlockSpec((tt, d), lambda i: (i, 0))
    vec = pl.BlockSpec((1, d), lambda i: (0, 0))
    return pl.pallas_call(
        body, name=name, grid=(t // tt,), in_specs=[row, row, vec], out_specs=[row, vec],
        out_shape=[jax.ShapeDtypeStruct((t, d), MXU_DTYPE), jax.ShapeDtypeStruct((1, d), F32)],
        compiler_params=_params(("arbitrary",)))(dxo, branch, gt)


def _swiglu_fwd(gu, name):
    t, f2 = gu.shape
    f = f2 // 2
    tt = _tile(t, (256, 128))

    def body(g_ref, u_ref, o_ref):
        o_ref[...] = (_silu(g_ref[...]) * u_ref[...]).astype(o_ref.dtype)

    return pl.pallas_call(
        body, name=name, grid=(t // tt,),
        in_specs=[pl.BlockSpec((tt, f), lambda i: (i, 0)), pl.BlockSpec((tt, f), lambda i: (i, 1))],
        out_specs=pl.BlockSpec((tt, f), lambda i: (i, 0)), out_shape=jax.ShapeDtypeStruct((t, f), MXU_DTYPE),
        compiler_params=_params(("parallel",)))(gu, gu)


def _swiglu_bwd(gu, da, name):
    t, f2 = gu.shape
    f = f2 // 2
    tt = _tile(t, (256, 128))

    def body(g_ref, u_ref, da_ref, o_ref):
        gate, da = g_ref[...], da_ref[...]
        o_ref[:, :f] = (da * u_ref[...] * _dsilu(gate)).astype(o_ref.dtype)
        o_ref[:, f:] = (da * _silu(gate)).astype(o_ref.dtype)

    return pl.pallas_call(
        body, name=name, grid=(t // tt,),
        in_specs=[pl.BlockSpec((tt, f), lambda i: (i, 0)), pl.BlockSpec((tt, f), lambda i: (i, 1)),
                  pl.BlockSpec((tt, f), lambda i: (i, 0))],
        out_specs=pl.BlockSpec((tt, f2), lambda i: (i, 0)), out_shape=jax.ShapeDtypeStruct((t, f2), MXU_DTYPE),
        compiler_params=_params(("parallel",)))(gu, gu, da)


class _ProjLayout:
    def __init__(self, d):
        wc = 3 * HEADS * HEAD_DIM
        self.d, self.wc = d, wc
        self.qkv, self.z, self.uv, self.gates, self.ba = 0, wc, wc + d, wc + 3 * d, wc + 5 * d
        self.width = self.ba + LANES
        assert self.z % d == 0 and self.uv % (2 * d) == 0 and self.gates % (2 * d) == 0 and self.ba % LANES == 0

    def pieces(self, shard):
        d, wc, out, lo = self.d, self.wc, [], 0
        for length, dst in ((2 * d, self.uv), (wc, self.qkv), (d, self.z), (2 * HEADS, self.ba), (2 * d, self.gates)):
            pos = lo
            while pos < lo + length:
                j = pos // shard
                n = min(lo + length, (j + 1) * shard) - pos
                out.append((j, pos - j * shard, n, dst + pos - lo))
                pos += n
            lo += length
        return out


def _merge_fwd(pa, pb, proj, gcol, name):
    t, d = pa.shape
    tt = _tile(t, _ROW_TILES)

    def body(pa_ref, pb_ref, ga_ref, gb_ref, o_ref):
        o_ref[...] = (_sigmoid(ga_ref[...]) * pa_ref[...] + _sigmoid(gb_ref[...]) * pb_ref[...]).astype(o_ref.dtype)

    row = pl.BlockSpec((tt, d), lambda i: (i, 0))
    gate = lambda k: pl.BlockSpec((tt, d), lambda i: (i, gcol // d + k))
    return pl.pallas_call(
        body, name=name, grid=(t // tt,), in_specs=[row, row, gate(0), gate(1)], out_specs=row,
        out_shape=jax.ShapeDtypeStruct((t, d), MXU_DTYPE), compiler_params=_params(("parallel",)))(pa, pb, proj, proj)


def _merge_bwd(dm, pa, pb, proj, gcol, dproj, name):
    t, d = pa.shape
    tt = _tile(t, _ROW_TILES)

    def body(dm_ref, pa_ref, pb_ref, ga_ref, gb_ref, _, dpa_ref, dpb_ref, dg_ref):
        dm = dm_ref[...]
        sa, sb = _sigmoid(ga_ref[...]), _sigmoid(gb_ref[...])
        dpa_ref[...] = (dm * sa).astype(dpa_ref.dtype)
        dpb_ref[...] = (dm * sb).astype(dpb_ref.dtype)
        dg_ref[:, :d] = (dm * pa_ref[...] * sa * (1.0 - sa)).astype(dg_ref.dtype)
        dg_ref[:, d:] = (dm * pb_ref[...] * sb * (1.0 - sb)).astype(dg_ref.dtype)

    row = pl.BlockSpec((tt, d), lambda i: (i, 0))
    gate = lambda k: pl.BlockSpec((tt, d), lambda i: (i, gcol // d + k))
    wide = pl.BlockSpec((tt, 2 * d), lambda i: (i, gcol // (2 * d)))
    return pl.pallas_call(
        body, name=name, grid=(t // tt,), in_specs=[row, row, row, gate(0), gate(1), _ANY], out_specs=[row, row, wide],
        out_shape=[jax.ShapeDtypeStruct((t, d), MXU_DTYPE), jax.ShapeDtypeStruct((t, d), MXU_DTYPE),
                   jax.ShapeDtypeStruct(dproj.shape, dproj.dtype)],
        input_output_aliases={5: 2}, compiler_params=_params(("parallel",)))(dm, pa, pb, proj, proj, dproj)


def _tri_masks(n):
    ri = lax.broadcasted_iota(jnp.int32, (n, n), 0)
    ci = lax.broadcasted_iota(jnp.int32, (n, n), 1)
    return ri >= ci, ri > ci, ri == ci


def _mixer_a_fwd(proj, ucol, w_s, b_col, g_v, name):
    t, w = proj.shape[0], g_v.shape[1]
    c = A_CHUNK

    def body(u_ref, v_ref, w_ref, b_ref, gv_ref, y_ref):
        tril, _, _ = _tri_masks(c)
        ug, vg = _gelu(u_ref[...]), _gelu(v_ref[...])
        for g in range(GROUPS):
            sl = slice(g * c, (g + 1) * c)
            vt = vg[:, sl]
            r = lax.rsqrt(jnp.mean(vt * vt, axis=-1, keepdims=True) + EPS)
            vn = vt * r * gv_ref[:, sl]
            s = _mm(jnp.where(tril, w_ref[g], 0.0), vn) + b_ref[g]
            y_ref[:, sl] = (ug[:, sl] * s).astype(y_ref.dtype)

    return pl.pallas_call(
        body, name=name, grid=(t // c,),
        in_specs=[pl.BlockSpec((c, w), lambda i: (i, ucol // w)), pl.BlockSpec((c, w), lambda i: (i, ucol // w + 1)),
                  pl.BlockSpec((GROUPS, c, c), lambda i: (0, 0, 0)), pl.BlockSpec((GROUPS, c, 1), lambda i: (0, 0, 0)),
                  pl.BlockSpec((1, w), lambda i: (0, 0))],
        out_specs=pl.BlockSpec((c, w), lambda i: (i, 0)), out_shape=jax.ShapeDtypeStruct((t, w), MXU_DTYPE),
        compiler_params=_params(("parallel",)))(proj, proj, w_s, b_col, g_v)


def _mixer_a_bwd(proj, ucol, dy, w_s, w_st, b_col, g_v, dproj, name):
    t, w = proj.shape[0], g_v.shape[1]
    w2 = 2 * w
    c = A_CHUNK

    def body(u_ref, v_ref, dy_ref, w_ref, wt_ref, b_ref, gv_ref, _, duv_ref, dw_ref, db_ref, dgv_ref):
        @pl.when(pl.program_id(0) == 0)
        def _():
            dw_ref[...] = jnp.zeros_like(dw_ref)
            db_ref[...] = jnp.zeros_like(db_ref)
            dgv_ref[...] = jnp.zeros_like(dgv_ref)

        tril, _, _ = _tri_masks(c)
        triu = lax.broadcasted_iota(jnp.int32, (c, c), 0) <= lax.broadcasted_iota(jnp.int32, (c, c), 1)
        up, vp = u_ref[...], v_ref[...]
        ug, vg = _gelu(up), _gelu(vp)
        for g in range(GROUPS):
            sl = slice(g * c, (g + 1) * c)
            vt = vg[:, sl]
            r = lax.rsqrt(jnp.mean(vt * vt, axis=-1, keepdims=True) + EPS)
            vh = vt * r
            gv = gv_ref[:, sl]
            vn = vh * gv
            s = _mm(jnp.where(tril, w_ref[g], 0.0), vn) + b_ref[g]
            dy = dy_ref[:, sl]
            ds = dy * ug[:, sl]
            dw_ref[g] += jnp.where(tril, _mm(ds, vn, _NT), 0.0)
            db_ref[g] += jnp.sum(ds, axis=1, keepdims=True)
            dvn = _mm(jnp.where(triu, wt_ref[g], 0.0), ds)
            dgv_ref[:, sl] += jnp.sum(dvn * vh, axis=0, keepdims=True)
            dvh = dvn * gv
            dvt = r * (dvh - vh * jnp.mean(dvh * vh, axis=-1, keepdims=True))
            duv_ref[:, sl] = (dy * s * _dgelu(up[:, sl])).astype(duv_ref.dtype)
            duv_ref[:, w + g * c:w + (g + 1) * c] = (dvt * _dgelu(vp[:, sl])).astype(duv_ref.dtype)

    full3 = lambda shape: pl.BlockSpec(shape, lambda i: (0, 0, 0))
    return pl.pallas_call(
        body, name=name, grid=(t // c,),
        in_specs=[pl.BlockSpec((c, w), lambda i: (i, ucol // w)), pl.BlockSpec((c, w), lambda i: (i, ucol // w + 1)),
                  pl.BlockSpec((c, w), lambda i: (i, 0)), full3((GROUPS, c, c)), full3((GROUPS, c, c)),
                  full3((GROUPS, c, 1)), pl.BlockSpec((1, w), lambda i: (0, 0)), _ANY],
        out_specs=[pl.BlockSpec((c, w2), lambda i: (i, ucol // w2)), full3((GROUPS, c, c)), full3((GROUPS, c, 1)),
                   pl.BlockSpec((1, w), lambda i: (0, 0))],
        out_shape=[jax.ShapeDtypeStruct(dproj.shape, dproj.dtype), jax.ShapeDtypeStruct((GROUPS, c, c), F32),
                   jax.ShapeDtypeStruct((GROUPS, c, 1), F32), jax.ShapeDtypeStruct((1, w), F32)],
        input_output_aliases={7: 0},
        compiler_params=_params(("arbitrary",)))(proj, proj, dy, w_s, w_st, b_col, g_v, dproj)


_Q_SCALE = HEAD_DIM ** -0.5


def _conv_taps(ext, w_ref):
    shifted = [ext[SUBLANES:]] + [pltpu.roll(ext, s, 0)[SUBLANES:] for s in range(1, CONV_K)]
    acc = shifted[0] * w_ref[pl.ds(CONV_K - 1, 1), :]
    for s in range(1, CONV_K):
        acc = acc + shifted[s] * w_ref[pl.ds(CONV_K - 1 - s, 1), :]
    return acc, shifted


def _conv_fwd(qkv, w, name):
    t, cw = qkv.shape[0], w.shape[1]
    tt = _tile(t, (256, 128))
    hb = tt // SUBLANES

    def body(x_ref, p_ref, w_ref, o_ref):
        prev = jnp.where(pl.program_id(0) > 0, p_ref[...], 0.0)
        acc, _ = _conv_taps(jnp.concatenate([prev, x_ref[...]], axis=0), w_ref)
        y = _silu(acc)
        for which in range(3):
            for h in range(HEADS):
                lo = (which * HEADS + h) * HEAD_DIM
                seg = y[:, lo:lo + HEAD_DIM]
                if which < 2:
                    seg = seg * lax.rsqrt(jnp.sum(seg * seg, axis=-1, keepdims=True) + EPS)
                if which == 0:
                    seg = seg * _Q_SCALE
                o_ref[which, h] = seg

    return pl.pallas_call(
        body, name=name, grid=(t // tt,),
        in_specs=[pl.BlockSpec((tt, cw), lambda i: (i, 0)),
                  pl.BlockSpec((SUBLANES, cw), lambda i: (jnp.maximum(i * hb - 1, 0), 0)),
                  pl.BlockSpec((CONV_K, cw), lambda i: (0, 0))],
        out_specs=pl.BlockSpec((3, HEADS, tt, HEAD_DIM), lambda i: (0, 0, i, 0)),
        out_shape=jax.ShapeDtypeStruct((3, HEADS, t, HEAD_DIM), F32),
        compiler_params=_params(("parallel",)))(qkv, qkv, w)


def _conv_bwd_pre(qkv, dq, dk, dv, w, name):
    t, cw = qkv.shape[0], w.shape[1]
    tt = _tile(t, (256, 128))
    hb = tt // SUBLANES

    def body(x_ref, p_ref, dq_ref, dk_ref, dv_ref, w_ref, da_ref, dw_ref):
        @pl.when(pl.program_id(0) == 0)
        def _():
            dw_ref[...] = jnp.zeros_like(dw_ref)

        prev = jnp.where(pl.program_id(0) > 0, p_ref[...], 0.0)
        acc, shifted = _conv_taps(jnp.concatenate([prev, x_ref[...]], axis=0), w_ref)
        y = _silu(acc)
        d_refs = (dq_ref, dk_ref, dv_ref)
        for which in range(3):
            for h in range(HEADS):
                lo = (which * HEADS + h) * HEAD_DIM
                sl = slice(lo, lo + HEAD_DIM)
                dn = d_refs[which][h]
                if which < 2:
                    seg = y[:, sl]
                    rho = lax.rsqrt(jnp.sum(seg * seg, axis=-1, keepdims=True) + EPS)
                    nrm = seg * rho
                    if which == 0:
                        dn = dn * _Q_SCALE
                    dn = rho * (dn - nrm * jnp.sum(dn * nrm, axis=-1, keepdims=True))
                dacc = dn * _dsilu(acc[:, sl])
                da_ref[:, sl] = dacc
                for s in range(CONV_K):
                    dw_ref[pl.ds(CONV_K - 1 - s, 1), sl] += jnp.sum(dacc * shifted[s][:, sl], axis=0, keepdims=True)

    head = pl.BlockSpec((HEADS, tt, HEAD_DIM), lambda i: (0, i, 0))
    return pl.pallas_call(
        body, name=name, grid=(t // tt,),
        in_specs=[pl.BlockSpec((tt, cw), lambda i: (i, 0)),
                  pl.BlockSpec((SUBLANES, cw), lambda i: (jnp.maximum(i * hb - 1, 0), 0)),
                  head, head, head, pl.BlockSpec((CONV_K, cw), lambda i: (0, 0))],
        out_specs=[pl.BlockSpec((tt, cw), lambda i: (i, 0)), pl.BlockSpec((CONV_K, cw), lambda i: (0, 0))],
        out_shape=[jax.ShapeDtypeStruct((t, cw), F32), jax.ShapeDtypeStruct((CONV_K, cw), F32)],
        compiler_params=_params(("arbitrary",)))(qkv, qkv, dq, dk, dv, w)


def _conv_bwd_in(dacc, w, dproj, name):
    t, cw = dacc.shape
    tt = _tile(t, (256, 128))
    hb = tt // SUBLANES
    nt = t // tt
    rows = tt + SUBLANES

    def body(d_ref, n_ref, w_ref, _, o_ref):
        cur = d_ref[...]
        nxt = jnp.where(pl.program_id(0) < nt - 1, n_ref[...], 0.0)
        ext = jnp.concatenate([cur, nxt], axis=0)
        acc = cur * w_ref[pl.ds(CONV_K - 1, 1), :]
        for s in range(1, CONV_K):
            acc = acc + pltpu.roll(ext, rows - s, 0)[:tt] * w_ref[pl.ds(CONV_K - 1 - s, 1), :]
        o_ref[...] = acc.astype(o_ref.dtype)

    return pl.pallas_call(
        body, name=name, grid=(nt,),
        in_specs=[pl.BlockSpec((tt, cw), lambda i: (i, 0)),
                  pl.BlockSpec((SUBLANES, cw), lambda i: (jnp.minimum((i + 1) * hb, t // SUBLANES - 1), 0)),
                  pl.BlockSpec((CONV_K, cw), lambda i: (0, 0)), _ANY],
        out_specs=pl.BlockSpec((tt, cw), lambda i: (i, 0)), out_shape=jax.ShapeDtypeStruct(dproj.shape, dproj.dtype),
        input_output_aliases={3: 0}, compiler_params=_params(("parallel",)))(dacc, dacc, w, dproj)


_INV_BASE_SHIFT = 3


def _inv_unit_lower(a, eye):
    c = GDN_CHUNK
    ri = lax.broadcasted_iota(jnp.int32, (c, c), 0)
    ci = lax.broadcasted_iota(jnp.int32, (c, c), 1)
    same = lambda sh: (ri >> sh) == (ci >> sh)
    x = jnp.where(same(_INV_BASE_SHIFT), -a, 0.0)
    p = jnp.where(eye, 1.0, 0.0) + x
    xs = _split(x)
    x2 = _mm3(xs, xs)
    x2s, ps = _split(x2), _split(p)
    r = _mm3(x2s, tuple(jnp.concatenate([u, v], axis=-1) for u, v in zip(x2s, ps)))
    x4, p = r[..., :c], p + r[..., c:]
    p = p + _mm3(_split(x4), _split(p))
    for sh in range(_INV_BASE_SHIFT, c.bit_length() - 1):
        off = jnp.where(same(sh + 1) & jnp.logical_not(same(sh)), a, 0.0)
        ps = _split(p)
        p = p - _mm3(ps, _split(_mm3(_split(off), ps)))
    return p


def _split(a):
    hi = a.astype(BF16)
    return hi, (a - hi.astype(F32)).astype(BF16)


def _dot_heads(u, v, dims):
    if u.ndim == 3:
        return jnp.stack([_dot_heads(u[j], v[j], dims) for j in range(u.shape[0])])
    return lax.dot_general(u, v, dims, preferred_element_type=F32)


def _mm3(a, b):
    n = b[0].shape[-1]
    wide = _dot_heads(a[0], jnp.concatenate(b, axis=-1), _NN)
    return wide[..., :n] + (wide[..., n:] + _dot_heads(a[1], b[0], _NN))


def _hmm(a, b, dims=_NN):
    return _dot_heads(a.astype(MXU_DTYPE), b.astype(MXU_DTYPE), dims)


def _rowsum(x):
    return jnp.sum(x, axis=-1, keepdims=True)


def _colsum(x):
    return jnp.sum(x, axis=-2, keepdims=True)


class _Pre:
    pass


def _gdn_pre(q, k, v, araw, braw, alog, dtb, t_mat=None):
    c = GDN_CHUNK
    p = _Pre()
    p.tril, p.strict, p.eye = _tri_masks(c)
    p.to_col = lambda row: _rowsum(jnp.where(p.eye, row, 0.0))
    p.to_row = lambda col: _colsum(jnp.where(p.eye, col, 0.0))
    p.a_neg = -jnp.exp(alog + jnp.zeros((1, c), F32))
    p.xg = araw + dtb
    p.g_row = p.a_neg * _softplus(p.xg)
    p.beta_row = _sigmoid(braw)
    p.beta = p.to_col(p.beta_row)
    gam = _rowsum(jnp.where(p.tril, p.g_row, 0.0))
    gam_last = _rowsum(p.g_row)
    p.dm = jnp.where(p.tril, jnp.exp(jnp.where(p.tril, gam - p.to_row(gam), 0.0)), 0.0)
    p.e, p.ek, p.el = jnp.exp(gam), jnp.exp(gam_last - gam), jnp.exp(gam_last)
    p.kb = k * p.beta
    p.kk = _hmm(p.kb, k, _NT)
    p.t = _inv_unit_lower(jnp.where(p.strict, p.kk * p.dm, 0.0), p.eye) if t_mat is None else t_mat
    p.vb, p.kbe = v * p.beta, p.kb * p.e
    uw = _hmm(p.t, jnp.concatenate([p.vb, p.kbe], axis=-1))
    p.u, p.w = uw[..., :v.shape[-1]], uw[..., v.shape[-1]:]
    p.qk0 = _hmm(q, k, _NT)
    p.qk = p.qk0 * p.dm
    p.qd, p.kd = q * p.e, k * p.ek
    return p


GDN_HEADS_PER_STEP = 8


def _head_scalars(ref, hb):
    h0 = pl.program_id(0) * hb
    return jnp.stack([jnp.full((1, 1), ref[h0 + j], F32) for j in range(hb)])


def _gdn_specs(n, reverse):
    c, dk, hb = GDN_CHUNK, HEAD_DIM, GDN_HEADS_PER_STEP
    ix = (lambda i: n - 1 - i) if reverse else (lambda i: i)
    smem = pl.BlockSpec(memory_space=pltpu.SMEM)
    qkv = [pl.BlockSpec((None, hb, c, dk), functools.partial(lambda w, h, i: (w, h, ix(i), 0), w)) for w in range(3)]
    row = pl.BlockSpec((hb, None, 1, c), lambda h, i: (h, ix(i), 0, 0))
    tok = pl.BlockSpec((hb, c, dk), lambda h, i: (h, ix(i), 0))
    state = pl.BlockSpec((hb, None, dk, dk), lambda h, i: (h, ix(i), 0, 0))
    return smem, qkv, row, tok, state


def _gdn_fwd(qkv_h, araw, braw, alog, dtb, name, side=_NoSide):
    _, hh, t, dk = qkv_h.shape
    n, hb = t // GDN_CHUNK, GDN_HEADS_PER_STEP
    smem, qkv, row, tok, state = _gdn_specs(n, False)
    grid = (hh // hb, n)

    def body(*refs):
        main, side_start, side_finish = _side_hooks(side, refs, 7, 3, 1, grid)
        alog_ref, dt_ref, q_ref, k_ref, v_ref, a_ref, b_ref, o_ref, so_ref, to_ref, s_ref = main
        side_start()

        @pl.when(pl.program_id(1) == 0)
        def _():
            s_ref[...] = jnp.zeros_like(s_ref)

        p = _gdn_pre(q_ref[...], k_ref[...], v_ref[...], a_ref[...], b_ref[...],
                     _head_scalars(alog_ref, hb), _head_scalars(dt_ref, hb))
        s = s_ref[...]
        vn = p.u - _hmm(p.w, s)
        o_ref[...] = _hmm(p.qd, s) + _hmm(p.qk, vn)
        so_ref[...] = s
        to_ref[...] = p.t
        s_ref[...] = s * p.el + _hmm(p.kd, vn, _TN)
        side_finish()

    mats = jax.ShapeDtypeStruct((hh, n, dk, dk), F32)
    return _carrier_call(
        body, name, grid, [smem, smem] + qkv + [row, row], [tok, state, state],
        [jax.ShapeDtypeStruct((hh, t, dk), F32), mats, mats],
        [pltpu.VMEM((hb, dk, dk), F32)], side, (alog, dtb, qkv_h, qkv_h, qkv_h, araw, braw))


def _gdn_bwd(qkv_h, araw, braw, alog, dtb, states, t_mats, do, name, side=_NoSide):
    _, hh, t, dk = qkv_h.shape
    c, hb = GDN_CHUNK, GDN_HEADS_PER_STEP
    n = t // c
    smem, qkv, row, tok, state = _gdn_specs(n, True)
    acc = pl.BlockSpec((hb, 1, LANES), lambda h, i: (h, 0, 0))
    grid = (hh // hb, n)

    def body(*refs):
        main, side_start, side_finish = _side_hooks(side, refs, 10, 7, 1, grid)
        (alog_ref, dt_ref, q_ref, k_ref, v_ref, a_ref, b_ref, s_ref, t_ref, do_ref,
         dq_ref, dk_ref, dv_ref, da_ref, db_ref, dal_ref, ddt_ref, ds_ref) = main
        side_start()

        @pl.when(pl.program_id(1) == 0)
        def _():
            ds_ref[...] = jnp.zeros_like(ds_ref)
            dal_ref[...] = jnp.zeros_like(dal_ref)
            ddt_ref[...] = jnp.zeros_like(ddt_ref)

        q, k, v = q_ref[...], k_ref[...], v_ref[...]
        p = _gdn_pre(q, k, v, a_ref[...], b_ref[...], _head_scalars(alog_ref, hb), _head_scalars(dt_ref, hb),
                     t_ref[...])
        s, do, dsp = s_ref[...], do_ref[...], ds_ref[...]
        vn = p.u - _hmm(p.w, s)
        dqd = _hmm(do, s, _NT)
        dqk = _hmm(do, vn, _NT)
        dvn = _hmm(p.qk, do, _TN) + _hmm(p.kd, dsp)
        dkd = _hmm(vn, dsp, _NT)
        d_el = _colsum(_rowsum(s * dsp))
        ds_ref[...] = dsp * p.el + _hmm(p.qd, do, _TN) - _hmm(p.w, dvn, _TN)
        dw = -_hmm(dvn, s, _NT)
        d_t = _hmm(dvn, p.vb, _NT) + _hmm(dw, p.kbe, _NT)
        dvb, dkbe = _hmm(p.t, dvn, _TN), _hmm(p.t, dw, _TN)
        d_a = jnp.where(p.strict, -_hmm(p.t, _hmm(d_t, p.t, _NT), _TN), 0.0)
        dkk = d_a * p.dm
        dqk0 = dqk * p.dm
        ddm = d_a * p.kk + dqk * p.qk0
        dkb = _hmm(dkk, k) + dkbe * p.e
        dq_ref[...] = _hmm(dqk0, k) + dqd * p.e
        dk_ref[...] = _hmm(dkk, p.kb, _TN) + _hmm(dqk0, q, _TN) + dkd * p.ek + dkb * p.beta
        dv_ref[...] = dvb * p.beta
        dbeta = _rowsum(dkb * k) + _rowsum(dvb * v)
        d_e = _rowsum(dqd * q) + _rowsum(dkbe * p.kb)
        d_ek = _rowsum(dkd * k)
        m = ddm * p.dm
        dgam = d_e * p.e - d_ek * p.ek + _rowsum(m) - p.to_col(_colsum(m))
        dgam_last = _colsum(d_ek * p.ek) + d_el * p.el
        dg_row = _colsum(jnp.where(p.tril, dgam, 0.0)) + dgam_last
        da_row = dg_row * p.a_neg * _sigmoid(p.xg)
        da_ref[...] = da_row
        db_ref[...] = p.to_row(dbeta) * p.beta_row * (1.0 - p.beta_row)
        dal_ref[...] += _rowsum(dg_row * p.g_row)
        ddt_ref[...] += _rowsum(da_row)
        side_finish()

    tok_shape = jax.ShapeDtypeStruct((hh, t, dk), F32)
    row_shape = jax.ShapeDtypeStruct((hh, n, 1, c), F32)
    acc_shape = jax.ShapeDtypeStruct((hh, 1, LANES), F32)
    return _carrier_call(
        body, name, grid, [smem, smem] + qkv + [row, row, state, state, tok], [tok, tok, tok, row, row, acc, acc],
        [tok_shape, tok_shape, tok_shape, row_shape, row_shape, acc_shape, acc_shape],
        [pltpu.VMEM((hb, dk, dk), F32)], side, (alog, dtb, qkv_h, qkv_h, qkv_h, araw, braw, states, t_mats, do))


def _gdn_post_fwd(o, proj, zcol, g_o, name):
    hh, t, dv = o.shape
    tt = _tile(t, _ROW_TILES)
    zblk = zcol // (hh * dv)

    def body(o_ref, z_ref, g_ref, y_ref):
        for h in range(hh):
            sl = slice(h * dv, (h + 1) * dv)
            ov = o_ref[h]
            r = lax.rsqrt(jnp.mean(ov * ov, axis=-1, keepdims=True) + EPS)
            y_ref[:, sl] = (ov * r * g_ref[...] * _silu(z_ref[:, sl])).astype(y_ref.dtype)

    return pl.pallas_call(
        body, name=name, grid=(t // tt,),
        in_specs=[pl.BlockSpec((hh, tt, dv), lambda i: (0, i, 0)), pl.BlockSpec((tt, hh * dv), lambda i: (i, zblk)),
                  pl.BlockSpec((1, dv), lambda i: (0, 0))],
        out_specs=pl.BlockSpec((tt, hh * dv), lambda i: (i, 0)),
        out_shape=jax.ShapeDtypeStruct((t, hh * dv), MXU_DTYPE), compiler_params=_params(("parallel",)))(o, proj, g_o)


def _gdn_post_bwd(o, proj, zcol, dy, g_o, dproj, name):
    hh, t, dv = o.shape
    tt = _tile(t, _ROW_TILES)
    zblk = zcol // (hh * dv)

    def body(o_ref, z_ref, dy_ref, g_ref, _, do_ref, dz_ref, dg_ref):
        @pl.when(pl.program_id(0) == 0)
        def _():
            dg_ref[...] = jnp.zeros_like(dg_ref)

        gv = g_ref[...]
        for h in range(hh):
            sl = slice(h * dv, (h + 1) * dv)
            ov, zz, dy = o_ref[h], z_ref[:, sl], dy_ref[:, sl]
            r = lax.rsqrt(jnp.mean(ov * ov, axis=-1, keepdims=True) + EPS)
            oh = ov * r
            dz_ref[:, sl] = (dy * oh * gv * _dsilu(zz)).astype(dz_ref.dtype)
            don = dy * _silu(zz)
            dg_ref[...] += _colsum(don * oh)
            doh = don * gv
            do_ref[h] = r * (doh - oh * jnp.mean(doh * oh, axis=-1, keepdims=True))

    return pl.pallas_call(
        body, name=name, grid=(t // tt,),
        in_specs=[pl.BlockSpec((hh, tt, dv), lambda i: (0, i, 0)), pl.BlockSpec((tt, hh * dv), lambda i: (i, zblk)),
                  pl.BlockSpec((tt, hh * dv), lambda i: (i, 0)), pl.BlockSpec((1, dv), lambda i: (0, 0)), _ANY],
        out_specs=[pl.BlockSpec((hh, tt, dv), lambda i: (0, i, 0)), pl.BlockSpec((tt, hh * dv), lambda i: (i, zblk)),
                   pl.BlockSpec((1, dv), lambda i: (0, 0))],
        out_shape=[jax.ShapeDtypeStruct((hh, t, dv), F32), jax.ShapeDtypeStruct(dproj.shape, dproj.dtype),
                   jax.ShapeDtypeStruct((1, dv), F32)],
        input_output_aliases={4: 1}, compiler_params=_params(("arbitrary",)))(o, proj, dy, g_o, dproj)


def _write_cols(dst, src, col, name):
    t, w = src.shape
    tt = _tile(t, _ROW_TILES)

    def body(s_ref, _, o_ref):
        o_ref[...] = s_ref[...].astype(o_ref.dtype)

    return pl.pallas_call(
        body, name=name, grid=(t // tt,), in_specs=[pl.BlockSpec((tt, w), lambda i: (i, 0)), _ANY],
        out_specs=pl.BlockSpec((tt, w), lambda i: (i, col // w)), out_shape=jax.ShapeDtypeStruct(dst.shape, dst.dtype),
        input_output_aliases={1: 0}, compiler_params=_params(("parallel",)))(src, dst)


def _adamw(g, w, m, v):
    m = ADAM_B1 * m + (1.0 - ADAM_B1) * g
    v = ADAM_B2 * v + (1.0 - ADAM_B2) * (g * g)
    m_hat = m / (1.0 - ADAM_B1 ** ADAM_STEP)
    v_hat = v / (1.0 - ADAM_B2 ** ADAM_STEP)
    return -ADAM_LR * (m_hat / (jnp.sqrt(v_hat) + ADAM_EPS) + ADAM_WD * w), m, v


def _ada_fwd(c_all, ada_w, name):
    nl, d, cols = ada_w.shape
    b = c_all.shape[0]

    def body(c_ref, w_ref, o_ref):
        o_ref[...] = _mm_hi(_silu(c_ref[...]), w_ref[...])

    return pl.pallas_call(
        body, name=name, grid=(nl,),
        in_specs=[pl.BlockSpec((b, d), lambda i: (0, 0)), pl.BlockSpec((None, d, cols), lambda i: (i, 0, 0))],
        out_specs=pl.BlockSpec((None, b, cols), lambda i: (i, 0, 0)),
        out_shape=jax.ShapeDtypeStruct((nl, b, cols), F32), compiler_params=_params(("parallel",)))(c_all, ada_w)


def _ada_bwd(c_col, dm, w, m, v, name):
    nl, d, cols = w.shape
    b = c_col.shape[0]
    tr = _tile(d, (256, 128))

    def body(c_ref, dm_ref, w_ref, m_ref, v_ref, g_ref, dl_ref, mo_ref, vo_ref):
        g = _silu(c_ref[0]) * dm_ref[pl.ds(0, 1), :]
        for j in range(1, b):
            g = g + _silu(c_ref[j]) * dm_ref[pl.ds(j, 1), :]
        g_ref[...] = g
        dl_ref[...], mo_ref[...], vo_ref[...] = _adamw(g, w_ref[...], m_ref[...], v_ref[...])

    blk = pl.BlockSpec((None, tr, cols), lambda l, i: (l, i, 0))
    shape = jax.ShapeDtypeStruct((nl, d, cols), F32)
    return pl.pallas_call(
        body, name=name, grid=(nl, d // tr),
        in_specs=[pl.BlockSpec((b, tr, 1), lambda l, i: (0, i, 0)), pl.BlockSpec((None, b, cols), lambda l, i: (l, 0, 0)),
                  blk, blk, blk],
        out_specs=[blk, blk, blk, blk], out_shape=[shape] * 4,
        compiler_params=_params(("parallel", "parallel")))(c_col, dm, w, m, v)


def _sum_adam(parts, w, m, v, name):
    nl, npart, r, cdim = parts.shape
    tr = _tile(r, (256, 128))

    def body(p_ref, w_ref, m_ref, v_ref, g_ref, dl_ref, mo_ref, vo_ref):
        g = p_ref[0].astype(F32)
        for j in range(1, npart):
            g = g + p_ref[j].astype(F32)
        g_ref[...] = g
        dl_ref[...], mo_ref[...], vo_ref[...] = _adamw(g, w_ref[...], m_ref[...], v_ref[...])

    blk = pl.BlockSpec((None, tr, cdim), lambda l, i: (l, i, 0))
    shape = jax.ShapeDtypeStruct((nl, r, cdim), F32)
    return pl.pallas_call(
        body, name=name, grid=(nl, r // tr),
        in_specs=[pl.BlockSpec((None, npart, tr, cdim), lambda l, i: (l, 0, i, 0)), blk, blk, blk],
        out_specs=[blk, blk, blk, blk], out_shape=[shape] * 4,
        compiler_params=_params(("parallel", "parallel")))(parts, w, m, v)


def _cols_from_blocks(g, plan, width, name):
    _, r, cdim = g.shape
    tr = _tile(r, (256, 128))
    covered = sorted((dst, dst + n) for _, _, n, dst in plan)
    holes, pos = [], 0
    for a, b in covered:
        if a > pos:
            holes.append((pos, a))
        pos = max(pos, b)
    if pos < width:
        holes.append((pos, width))

    def body(g_ref, o_ref):
        for a, b in holes:
            o_ref[:, a:b] = jnp.zeros((tr, b - a), o_ref.dtype)
        for j, src, n, dst in plan:
            o_ref[:, dst:dst + n] = g_ref[j, :, src:src + n]

    return pl.pallas_call(
        body, name=name, grid=(r // tr,), in_specs=[pl.BlockSpec((N_DEV, tr, cdim), lambda i: (0, i, 0))],
        out_specs=pl.BlockSpec((tr, width), lambda i: (i, 0)), out_shape=jax.ShapeDtypeStruct((r, width), g.dtype),
        compiler_params=_params(("parallel",)))(g)


def _blocks_from_cols(w, plan, cdim, name):
    r, width = w.shape
    tr = _tile(r, (256, 128))

    def body(w_ref, o_ref):
        for j, src, n, dst in plan:
            o_ref[j, :, src:src + n] = w_ref[:, dst:dst + n]

    return pl.pallas_call(
        body, name=name, grid=(r // tr,), in_specs=[pl.BlockSpec((tr, width), lambda i: (i, 0))],
        out_specs=pl.BlockSpec((N_DEV, tr, cdim), lambda i: (0, i, 0)),
        out_shape=jax.ShapeDtypeStruct((N_DEV, r, cdim), w.dtype), compiler_params=_params(("parallel",)))(w)


def _pair_sum(x, tmp, core, name):
    _, r, cdim = x.shape
    tr = _tile(r, (256, 128))

    def body(core_ref, x_ref, t_ref, o_ref):
        o_ref[...] = (x_ref[...] + t_ref[...]).astype(o_ref.dtype)

    grid_spec = pltpu.PrefetchScalarGridSpec(
        num_scalar_prefetch=1, grid=(N_DEV // 2, r // tr),
        in_specs=[pl.BlockSpec((None, tr, cdim), lambda ch, i, core_ref: (2 * ch + core_ref[0], i, 0)),
                  pl.BlockSpec((None, tr, cdim), lambda ch, i, core_ref: (ch, i, 0))],
        out_specs=pl.BlockSpec((None, tr, cdim), lambda ch, i, core_ref: (ch, i, 0)))
    return pl.pallas_call(
        body, name=name, grid_spec=grid_spec, out_shape=jax.ShapeDtypeStruct((N_DEV // 2, r, cdim), WIRE_DTYPE),
        compiler_params=_params(("parallel", "parallel")))(core, x, tmp)


_ANY = pl.BlockSpec(memory_space=pl.ANY)
_CHIP_FLIPS = ((1, 0), (0, 1), (1, 1))


def _coords():
    return lax.axis_index("x"), lax.axis_index("y"), lax.axis_index("c")


def _flip(v, f):
    return 1 - v if f else v


def _a2a_direct(xs, name):
    n, ncp = len(xs), N_DEV - 1

    def body(*refs):
        ins, outs = refs[:n], refs[n:2 * n]
        send, recv, loc = refs[2 * n:]
        x, y, c = _coords()
        me = 4 * x + 2 * y + c
        local = [pltpu.make_async_copy(ins[i].at[me], outs[i].at[me], loc.at[i]) for i in range(n)]
        for cp in local:
            cp.start()
        remote = []
        for i in range(n):
            for k in range(1, N_DEV):
                px, py, pc = _flip(x, k & 4), _flip(y, k & 2), _flip(c, k & 1)
                cp = pltpu.make_async_remote_copy(
                    src_ref=ins[i].at[4 * px + 2 * py + pc], dst_ref=outs[i].at[me],
                    send_sem=send.at[i * ncp + k - 1], recv_sem=recv.at[i * ncp + k - 1],
                    device_id=(px, py, pc), device_id_type=MESH)
                cp.start()
                remote.append(cp)
        for cp in remote:
            cp.wait()
        for cp in local:
            cp.wait()

    return pl.pallas_call(
        body, name=name, in_specs=[_ANY] * n, out_specs=[_ANY] * n,
        out_shape=[jax.ShapeDtypeStruct(a.shape, a.dtype) for a in xs],
        scratch_shapes=[pltpu.SemaphoreType.DMA((n * ncp,)), pltpu.SemaphoreType.DMA((n * ncp,)),
                        pltpu.SemaphoreType.DMA((n,))])(*xs)


class _AllGatherSide:
    def __init__(self, blocks):
        self.operands = list(blocks)
        n = self.n = len(self.operands)
        self.n_in = self.n_out = n
        self.out_shape = [jax.ShapeDtypeStruct((N_DEV,) + a.shape, a.dtype) for a in self.operands]
        self.aliases = {}
        nici, nd2d = len(_CHIP_FLIPS), N_DEV // 2
        self.scratch = [pltpu.SemaphoreType.DMA((n * nici,)), pltpu.SemaphoreType.DMA((n * nici,)),
                        pltpu.SemaphoreType.DMA((n * nd2d,)), pltpu.SemaphoreType.DMA((n * nd2d,)),
                        pltpu.SemaphoreType.DMA((n,))]

    def _first(self, ins, outs, sems):
        send, recv, _, _, loc = sems
        x, y, c = _coords()
        me = 4 * x + 2 * y + c
        nici = len(_CHIP_FLIPS)
        local = [pltpu.make_async_copy(ins[i], outs[i].at[me], loc.at[i]) for i in range(self.n)]
        remote = [pltpu.make_async_remote_copy(
            src_ref=ins[i], dst_ref=outs[i].at[me], send_sem=send.at[i * nici + j], recv_sem=recv.at[i * nici + j],
            device_id=(_flip(x, fx), _flip(y, fy), c), device_id_type=MESH)
            for i in range(self.n) for j, (fx, fy) in enumerate(_CHIP_FLIPS)]
        return local + remote

    def _second(self, outs, sems):
        _, _, send, recv, _ = sems
        x, y, c = _coords()
        nd2d = N_DEV // 2
        return [pltpu.make_async_remote_copy(
            src_ref=outs[i].at[2 * ch + c], dst_ref=outs[i].at[2 * ch + c], send_sem=send.at[i * nd2d + ch],
            recv_sem=recv.at[i * nd2d + ch], device_id=(x, y, 1 - c), device_id_type=MESH)
            for i in range(self.n) for ch in range(nd2d)]

    def start(self, ins, outs, sems):
        for cp in self._first(ins, outs, sems):
            cp.start()

    def finish(self, ins, outs, sems):
        for cp in self._first(ins, outs, sems):
            cp.wait()
        second = self._second(outs, sems)
        for cp in second:
            cp.start()
        for cp in second:
            cp.wait()


class _ReduceScatterIciSide:
    def __init__(self, sums, accs, layer):
        self.operands = list(sums) + list(accs)
        n = self.n = len(sums)
        self.layer = layer
        self.n_in, self.n_out = 2 * n, n
        self.out_shape = [jax.ShapeDtypeStruct(a.shape, a.dtype) for a in accs]
        self.aliases = {n + i: i for i in range(n)}
        nici = len(_CHIP_FLIPS)
        self.scratch = [pltpu.SemaphoreType.DMA((n * nici,)), pltpu.SemaphoreType.DMA((n * nici,)),
                        pltpu.SemaphoreType.DMA((n,))]

    def _copies(self, ins, outs, sems):
        send, recv, loc = sems
        x, y, c = _coords()
        chip = 2 * x + y
        nici = len(_CHIP_FLIPS)
        local = [pltpu.make_async_copy(ins[i].at[chip], outs[i].at[self.layer, chip], loc.at[i])
                 for i in range(self.n)]
        remote = [pltpu.make_async_remote_copy(
            src_ref=ins[i].at[2 * _flip(x, fx) + _flip(y, fy)], dst_ref=outs[i].at[self.layer, chip],
            send_sem=send.at[i * nici + j], recv_sem=recv.at[i * nici + j],
            device_id=(_flip(x, fx), _flip(y, fy), c), device_id_type=MESH)
            for i in range(self.n) for j, (fx, fy) in enumerate(_CHIP_FLIPS)]
        return local + remote

    def start(self, ins, outs, sems):
        for cp in self._copies(ins, outs, sems):
            cp.start()

    def finish(self, ins, outs, sems):
        for cp in self._copies(ins, outs, sems):
            cp.wait()


def _run_side(side, name):
    def body(*refs):
        ins, outs = refs[:side.n_in], refs[side.n_in:side.n_in + side.n_out]
        sems = refs[side.n_in + side.n_out:]
        side.start(ins, outs, sems)
        side.finish(ins, outs, sems)

    return pl.pallas_call(
        body, name=name, in_specs=[_ANY] * side.n_in, out_specs=[_ANY] * side.n_out, out_shape=side.out_shape,
        input_output_aliases=side.aliases, scratch_shapes=side.scratch)(*side.operands)


class _ReduceScatterD2dSide:
    def __init__(self, parts):
        self.operands = list(parts)
        n = self.n = len(self.operands)
        self.n_in = self.n_out = n
        nd2d = N_DEV // 2
        self.out_shape = [jax.ShapeDtypeStruct((nd2d,) + a.shape[1:], a.dtype) for a in self.operands]
        self.aliases = {}
        self.scratch = [pltpu.SemaphoreType.DMA((n * nd2d,)), pltpu.SemaphoreType.DMA((n * nd2d,))]

    def _copies(self, ins, outs, sems):
        send, recv = sems
        x, y, c = _coords()
        nd2d = N_DEV // 2
        return [pltpu.make_async_remote_copy(
            src_ref=ins[i].at[2 * ch + 1 - c], dst_ref=outs[i].at[ch], send_sem=send.at[i * nd2d + ch],
            recv_sem=recv.at[i * nd2d + ch], device_id=(x, y, 1 - c), device_id_type=MESH)
            for i in range(self.n) for ch in range(nd2d)]

    def start(self, ins, outs, sems):
        for cp in self._copies(ins, outs, sems):
            cp.start()

    def finish(self, ins, outs, sems):
        for cp in self._copies(ins, outs, sems):
            cp.wait()


_PACK_ROWS = 256


def _pack(arrs):
    flat = jnp.concatenate([a.reshape(-1) for a in arrs])
    quantum = _PACK_ROWS * LANES
    total = -(-flat.shape[0] // quantum) * quantum
    return jnp.pad(flat, (0, total - flat.shape[0])).reshape(-1, LANES)


def _unpack(packed, like):
    flat, out, pos = packed.reshape(-1), [], 0
    for a in like:
        out.append(flat[pos:pos + a.size].reshape(a.shape))
        pos += a.size
    return out


def kernel(x, c, ada_w, ada_b, norm1_g, w_in, conv_w, spatial_w, spatial_b, v_norm_g, a_log, dt_bias, o_norm_g, w_branch_a, w_branch_b, w_out, norm2_g, w_ffn_in, w_ffn_out, final_g, loss_target, m_ada_w, m_ada_b, m_norm1_g, m_w_in, m_conv_w, m_spatial_w, m_spatial_b, m_v_norm_g, m_a_log, m_dt_bias, m_o_norm_g, m_w_branch_a, m_w_branch_b, m_w_out, m_norm2_g, m_w_ffn_in, m_w_ffn_out, m_final_g, v_ada_w, v_ada_b, v_norm1_g, v_w_in, v_conv_w, v_spatial_w, v_spatial_b, v_v_norm_g, v_a_log, v_dt_bias, v_o_norm_g, v_w_branch_a, v_w_branch_b, v_w_out, v_norm2_g, v_w_ffn_in, v_w_ffn_out, v_final_g):
    nl, d = ada_w.shape[0], x.shape[2]
    t = x.shape[1]
    nchunk = t // GDN_CHUNK
    xi, yi, ci = _coords()
    me = 4 * xi + 2 * yi + ci
    core = jnp.reshape(ci, (1,)).astype(jnp.int32)
    x0, target = x[0], loss_target[0]
    wcols = 3 * HEADS * HEAD_DIM
    lay = _ProjLayout(d)
    in_pieces = lay.pieces(w_in.shape[2])
    fi_shard = w_ffn_in.shape[2]
    fi_pieces = [(j, 0, fi_shard, fi_shard * j) for j in range(N_DEV)]

    c_all, cw_all = _a2a_direct([jnp.broadcast_to(c[None], (N_DEV,) + c.shape),
                                 jnp.broadcast_to(conv_w[None], (N_DEV,) + conv_w.shape)], "gather_small")
    c_all = c_all[:, 0]
    conv_full = cw_all.transpose(1, 2, 0, 3).reshape(nl, CONV_K, wcols)
    modp = _ada_fwd(c_all, ada_w, "ada_fwd")
    (modx,) = _a2a_direct([modp.transpose(1, 0, 2)], "mod_exchange")
    mod = (modx.transpose(1, 0, 2).reshape(nl, 6 * d) + ada_b).reshape(nl, 6, 1, d)

    big = (w_in, w_branch_a, w_branch_b, w_out, w_ffn_in, w_ffn_out)
    big_wire = [w.astype(WIRE_DTYPE) for w in big]
    gather_in = lambda i: _AllGatherSide([big_wire[0][i]])
    gather_rest = lambda i: _AllGatherSide([w[i] for w in big_wire[1:]])
    row_full = lambda g: g.reshape(-1, g.shape[2])
    padded_in = lambda g: _cols_from_blocks(g, in_pieces, lay.width, "w_in_cols")
    w_pads = [padded_in(_run_side(gather_in(0), "ag_first")[0])] + [None] * (nl - 1)
    weights = [None] * nl

    def rows_of(proj, lo):
        return proj[:, lo:lo + HEADS].T.reshape(HEADS, nchunk, 1, GDN_CHUNK)

    saved = []
    x_cur, delta, gt_prev = x0, None, None
    for i in range(nl):
        sh1, sc1, gt1, sh2, sc2, gt2 = (mod[i, k] for k in range(6))
        s = dict(gt1=gt1, gt2=gt2, sc1=sc1, sc2=sc2)
        s["x_in"], s["h"] = _resid_norm(x_cur, delta, gt_prev, norm1_g[i][None], sc1, sh1, "norm1_fwd")
        s["proj"], g_a, g_b, g_o, g_fi, g_fo = _matmul(s["h"], w_pads[i], "nn", "proj_fwd", side=gather_rest(i))
        weights[i] = (row_full(g_a), row_full(g_b), row_full(g_o),
                      _cols_from_blocks(g_fi, fi_pieces, N_DEV * fi_shard, "w_ffn_in_cols"), row_full(g_fo))
        w_a, w_b, w_o, w_fi, w_fo = weights[i]
        s["b_col"] = spatial_b[i][:, :, None]
        s["ya"] = _mixer_a_fwd(s["proj"], lay.uv, spatial_w[i], s["b_col"], v_norm_g[i][None], "mixer_a_fwd")
        s["qkv_h"] = _conv_fwd(s["proj"], conv_full[i], "conv_fwd")
        s["braw"], s["araw"] = rows_of(s["proj"], lay.ba), rows_of(s["proj"], lay.ba + HEADS)
        s["o"], s["states"], s["t_mats"], *gathered = _gdn_fwd(
            s["qkv_h"], s["araw"], s["braw"], a_log[i], dt_bias[i], "gdn_fwd",
            gather_in(i + 1) if i + 1 < nl else _NoSide)
        if gathered:
            w_pads[i + 1] = padded_in(gathered[0])
        s["yb"] = _gdn_post_fwd(s["o"], s["proj"], lay.z, o_norm_g[i][None], "gdn_post_fwd")
        s["pa"] = _matmul(s["ya"], w_a, "nn", "branch_a_fwd")
        s["pb"] = _matmul(s["yb"], w_b, "nn", "branch_b_fwd")
        s["merged"] = _merge_fwd(s["pa"], s["pb"], s["proj"], lay.gates, "merge_fwd")
        s["mo"] = _matmul(s["merged"], w_o, "nn", "out_fwd")
        s["x1"], s["h2"] = _resid_norm(s["x_in"], s["mo"], gt1, norm2_g[i][None], sc2, sh2, "norm2_fwd")
        s["gu"] = _matmul(s["h2"], w_fi, "nn", "ffn_in_fwd")
        s["a"] = _swiglu_fwd(s["gu"], "swiglu_fwd")
        s["fo"] = _matmul(s["a"], w_fo, "nn", "ffn_out_fwd")
        saved.append(s)
        x_cur, delta, gt_prev = s["x1"], s["fo"], gt2
    dx, d_final_g, loss_tile = _final_loss(x_cur, delta, gt_prev, final_g[None], target, "final_loss")
    loss = lax.psum(loss_tile[0, 0], ("x", "y", "c"))

    big_shapes = [(d, w_in.shape[2]), w_branch_a.shape[1:], w_branch_b.shape[1:], w_out.shape[1:],
                  (d, w_ffn_in.shape[2]), w_ffn_out.shape[1:]]
    accs = [lax.empty((nl, N_DEV // 2) + tuple(sh), WIRE_DTYPE) for sh in big_shapes]
    row_blocks = lambda g: g.reshape(N_DEV, -1, g.shape[1])
    dmod, small = [None] * nl, [None] * nl
    d_conv = [None] * nl
    parts, scatter = None, _NoSide
    for i in reversed(range(nl)):
        s = saved[i]
        w_a, w_b, w_o, w_fi, w_fo = weights[i]
        dfo, dgt2 = _gate_bwd(dx, s["fo"], s["gt2"], "gate2_bwd")
        g_fo = _matmul(s["a"], dfo, "tn", "ffn_out_dw")
        da = _matmul(dfo, w_fo, "nt", "ffn_out_dx")
        dgu = _swiglu_bwd(s["gu"], da, "swiglu_bwd")
        if parts is None:
            g_fi = _matmul(s["h2"], dgu, "tn", "ffn_in_dw")
        else:
            g_fi, *other = _matmul(s["h2"], dgu, "tn", "ffn_in_dw", side=_ReduceScatterD2dSide(parts))
            sums = [_pair_sum(p, o, core, "rs_pair_sum_%d" % k) for k, (p, o) in enumerate(zip(parts, other))]
            scatter = _ReduceScatterIciSide(sums, accs, i + 1)
        dh2 = _matmul(dgu, w_fi, "nt", "ffn_in_dx")
        dx1, dsh2, dsc2, dg2 = _norm_bwd(s["x1"], dh2, dx, norm2_g[i][None], s["sc2"], "norm2_bwd")
        dmo, dgt1 = _gate_bwd(dx1, s["mo"], s["gt1"], "gate1_bwd")
        g_o = _matmul(s["merged"], dmo, "tn", "out_dw")
        dmerged = _matmul(dmo, w_o, "nt", "out_dx")
        dproj = lax.empty((t, lay.width), MXU_DTYPE)
        dpa, dpb, dproj = _merge_bwd(dmerged, s["pa"], s["pb"], s["proj"], lay.gates, dproj, "merge_bwd")
        g_a = _matmul(s["ya"], dpa, "tn", "branch_a_dw")
        dya = _matmul(dpa, w_a, "nt", "branch_a_dx")
        g_b = _matmul(s["yb"], dpb, "tn", "branch_b_dw")
        dyb = _matmul(dpb, w_b, "nt", "branch_b_dx")
        dproj, d_ws, d_bs, d_gv = _mixer_a_bwd(s["proj"], lay.uv, dya, spatial_w[i], jnp.swapaxes(spatial_w[i], 1, 2),
                                               s["b_col"], v_norm_g[i][None], dproj, "mixer_a_bwd")
        do, dproj, d_go = _gdn_post_bwd(s["o"], s["proj"], lay.z, dyb, o_norm_g[i][None], dproj, "gdn_post_bwd")
        dq, dk, dv, d_ar, d_br, d_al, d_dt, *scattered = _gdn_bwd(
            s["qkv_h"], s["araw"], s["braw"], a_log[i], dt_bias[i], s["states"], s["t_mats"], do, "gdn_bwd", scatter)
        if scattered:
            accs = scattered
        dacc, d_conv[i] = _conv_bwd_pre(s["proj"], dq, dk, dv, conv_full[i], "conv_bwd_pre")
        dproj = _conv_bwd_in(dacc, conv_full[i], dproj, "conv_bwd_in")
        cols = lambda r: r.reshape(HEADS, t).T
        dba = jnp.pad(jnp.concatenate([cols(d_br), cols(d_ar)], axis=1), ((0, 0), (0, LANES - 2 * HEADS)))
        dproj = _write_cols(dproj, dba, lay.ba, "dproj_ba")
        g_pad = _matmul(s["h"], dproj, "tn", "proj_dw")
        dh = _matmul(dproj, w_pads[i], "nt", "proj_dx")
        dx, dsh1, dsc1, dg1 = _norm_bwd(s["x_in"], dh, dx1, norm1_g[i][None], s["sc1"], "norm1_bwd")
        dmod[i] = jnp.concatenate([dsh1, dsc1, dgt1, dsh2, dsc2, dgt2], axis=1)[0]
        small[i] = (dg1[0], d_ws, d_bs[:, :, 0], d_gv[0], d_al[:, 0, 0], d_dt[:, 0, 0], d_go[0], dg2[0])
        parts = [_blocks_from_cols(g_pad, in_pieces, w_in.shape[2], "w_in_blocks"), row_blocks(g_a), row_blocks(g_b),
                 row_blocks(g_o), _blocks_from_cols(g_fi, fi_pieces, fi_shard, "w_ffn_in_blocks"), row_blocks(g_fo)]
    other = _run_side(_ReduceScatterD2dSide(parts), "rs_d2d_last")
    sums = [_pair_sum(p, o, core, "rs_pair_sum_%d" % k) for k, (p, o) in enumerate(zip(parts, other))]
    accs = _run_side(_ReduceScatterIciSide(sums, accs, 0), "rs_ici_last")

    dmod = jnp.stack(dmod)
    sm = [jnp.stack([small[i][k] for i in range(nl)]) for k in range(8)]
    rep_w = (ada_b, norm1_g, spatial_w, spatial_b, v_norm_g, a_log, dt_bias, o_norm_g, norm2_g, final_g)
    rep_m = (m_ada_b, m_norm1_g, m_spatial_w, m_spatial_b, m_v_norm_g, m_a_log, m_dt_bias, m_o_norm_g, m_norm2_g, m_final_g)
    rep_v = (v_ada_b, v_norm1_g, v_spatial_w, v_spatial_b, v_v_norm_g, v_a_log, v_dt_bias, v_o_norm_g, v_norm2_g, v_final_g)
    rep_g = (dmod, sm[0], sm[1], sm[2], sm[3], sm[4], sm[5], sm[6], sm[7], d_final_g[0])
    packed = _pack(rep_g)
    d_conv_blocks = jnp.stack(d_conv).reshape(nl, CONV_K, N_DEV, -1).transpose(2, 0, 1, 3).reshape(N_DEV, -1, LANES)
    dmod_blocks = dmod.reshape(nl, N_DEV, -1).transpose(1, 0, 2)
    rep_all, conv_all, dmod_all = _a2a_direct(
        [jnp.broadcast_to(packed[None], (N_DEV,) + packed.shape), d_conv_blocks, dmod_blocks], "small_grads")
    rep_out = _sum_adam(rep_all[None], _pack(rep_w)[None], _pack(rep_m)[None], _pack(rep_v)[None], "adam_small")
    rep_out = [_unpack(o[0], rep_w) for o in rep_out]
    conv_out = _sum_adam(conv_all[None], conv_w.reshape(1, -1, LANES), m_conv_w.reshape(1, -1, LANES),
                         v_conv_w.reshape(1, -1, LANES), "adam_conv")
    conv_out = [o.reshape(conv_w.shape) for o in conv_out]
    ada_out = _ada_bwd(c_all[:, :, None], dmod_all.transpose(1, 0, 2), ada_w, m_ada_w, v_ada_w, "ada_bwd_adam")
    big_m = (m_w_in, m_w_branch_a, m_w_branch_b, m_w_out, m_w_ffn_in, m_w_ffn_out)
    big_v = (v_w_in, v_w_branch_a, v_w_branch_b, v_w_out, v_w_ffn_in, v_w_ffn_out)
    big_out = [_sum_adam(accs[k], big[k], big_m[k], big_v[k], "adam_big_%d" % k) for k in range(6)]

    def ordered(kind):
        rep = rep_out[kind]
        return (ada_out[kind], rep[0], rep[1], big_out[0][kind], conv_out[kind], rep[2], rep[3], rep[4], rep[5],
                rep[6], rep[7], big_out[1][kind], big_out[2][kind], big_out[3][kind], rep[8], big_out[4][kind],
                big_out[5][kind], rep[9])

    return (loss, dx[None]) + ordered(0) + ordered(1) + ordered(2) + ordered(3)
```

```python
import functools

import jax
import jax.numpy as jnp
from jax import lax
from jax.experimental import pallas as pl
from jax.experimental.pallas import tpu as pltpu

F32 = jnp.float32
BF16 = jnp.bfloat16
MXU_DTYPE = BF16
WIRE_DTYPE = BF16
EPS = 1e-6
LANES = 128
SUBLANES = 8
GDN_CHUNK = 128
A_CHUNK = 128
GROUPS = 8
HEADS = 8
HEAD_DIM = 128
CONV_K = 4
N_DEV = 8
VMEM_LIMIT = 48 * 1024 * 1024
MESH = pl.DeviceIdType.MESH

ADAM_LR = 0.001
ADAM_B1 = 0.9
ADAM_B2 = 0.999
ADAM_EPS = 1e-08
ADAM_WD = 0.01
ADAM_STEP = 10

_NN = (((1,), (0,)), ((), ()))
_NT = (((1,), (1,)), ((), ()))
_TN = (((0,), (0,)), ((), ()))


def _mm(a, b, dims=_NN):
    return lax.dot_general(a.astype(MXU_DTYPE), b.astype(MXU_DTYPE), dims, preferred_element_type=F32)


def _mm_hi(a, b):
    return lax.dot_general(a, b, _NN, precision=lax.Precision.HIGHEST, preferred_element_type=F32)


def _tile(n, cands):
    for c in cands:
        if n % c == 0:
            return c
    return n


def _params(sem=None):
    return pltpu.CompilerParams(dimension_semantics=sem, vmem_limit_bytes=VMEM_LIMIT)


def _sigmoid(x):
    return 1.0 / (1.0 + jnp.exp(-x))


def _silu(x):
    return x * _sigmoid(x)


def _dsilu(x):
    s = _sigmoid(x)
    return s * (1.0 + x * (1.0 - s))


_GELU_C = 0.7978845608028654
_GELU_A = 0.044715


def _gelu(x):
    return 0.5 * x * (1.0 + jnp.tanh(_GELU_C * (x + _GELU_A * x * x * x)))


def _dgelu(x):
    t = jnp.tanh(_GELU_C * (x + _GELU_A * x * x * x))
    return 0.5 * (1.0 + t) + 0.5 * x * (1.0 - t * t) * _GELU_C * (1.0 + 3.0 * _GELU_A * x * x)


def _softplus(x):
    return jnp.maximum(x, 0.0) + jnp.log(1.0 + jnp.exp(-jnp.abs(x)))


_MM_TILES = (1024, 1408, 1664, 512, 256, 128)


class _NoSide:
    operands, out_shape, scratch, aliases, n_in, n_out = [], [], [], {}, 0, 0


def _side_hooks(side, refs, n_main_in, n_main_out, n_main_scratch, grid):
    a = n_main_in + side.n_in
    b = a + n_main_out + side.n_out
    ins, outs, sems = refs[n_main_in:a], refs[a + n_main_out:b], refs[b + n_main_scratch:]
    main = refs[:n_main_in] + refs[a:a + n_main_out] + refs[b:b + n_main_scratch]
    ids = [pl.program_id(k) for k in range(len(grid))]

    def start():
        if side.n_in:
            pl.when(functools.reduce(jnp.logical_and, [i == 0 for i in ids]))(lambda: side.start(ins, outs, sems))

    def finish():
        if side.n_in:
            last = functools.reduce(jnp.logical_and, [i == g - 1 for i, g in zip(ids, grid)])
            pl.when(last)(lambda: side.finish(ins, outs, sems))

    return main, start, finish


def _carrier_call(body, name, grid, in_specs, out_specs, out_shape, scratch, side, args):
    aliases = {len(in_specs) + k: len(out_specs) + v for k, v in side.aliases.items()}
    return pl.pallas_call(
        body, name=name, grid=grid, in_specs=list(in_specs) + [_ANY] * side.n_in,
        out_specs=list(out_specs) + [_ANY] * side.n_out, out_shape=list(out_shape) + list(side.out_shape),
        scratch_shapes=list(scratch) + list(side.scratch), input_output_aliases=aliases,
        compiler_params=_params(("arbitrary",) * len(grid)))(*args, *side.operands)


def _matmul(a, b, mode, name, out_dtype=F32, side=_NoSide):
    if mode == "nn":
        (m, k), n = a.shape, b.shape[1]
    elif mode == "nt":
        (m, k), n = a.shape, b.shape[0]
    else:
        (k, m), n = a.shape, b.shape[1]
    tm, tn, tk = _tile(m, _MM_TILES), _tile(n, _MM_TILES), _tile(k, _MM_TILES)
    nk = k // tk
    grid = (m // tm, n // tn, nk)
    dims = {"nn": _NN, "nt": _NT, "tn": _TN}[mode]

    def body(*refs):
        (a_ref, b_ref, o_ref, acc_ref), side_start, side_finish = _side_hooks(side, refs, 2, 1, 1, grid)
        kk = pl.program_id(2)
        side_start()

        @pl.when(kk == 0)
        def _():
            acc_ref[...] = jnp.zeros_like(acc_ref)

        acc_ref[...] += _mm(a_ref[...], b_ref[...], dims)

        @pl.when(kk == nk - 1)
        def _():
            o_ref[...] = acc_ref[...].astype(o_ref.dtype)

        side_finish()

    a_spec = (pl.BlockSpec((tk, tm), lambda i, j, l: (l, i)) if mode == "tn"
              else pl.BlockSpec((tm, tk), lambda i, j, l: (i, l)))
    b_spec = (pl.BlockSpec((tn, tk), lambda i, j, l: (j, l)) if mode == "nt"
              else pl.BlockSpec((tk, tn), lambda i, j, l: (l, j)))
    o_spec = pl.BlockSpec((tm, tn), lambda i, j, l: (i, j))
    out = _carrier_call(body, name, grid, [a_spec, b_spec], [o_spec], [jax.ShapeDtypeStruct((m, n), out_dtype)],
                        [pltpu.VMEM((tm, tn), F32)], side, (a, b))
    return out if side.n_in else out[0]


_ROW_TILES = (512, 256, 128)


def _resid_norm(x, delta, gt, g, sc, sh, name):
    t, d = x.shape
    tt = _tile(t, _ROW_TILES)
    has = delta is not None

    def body(*refs):
        if has:
            x_ref, d_ref, gt_ref, g_ref, sc_ref, sh_ref, xo_ref, h_ref = refs
            xv = x_ref[...] + gt_ref[...] * d_ref[...]
            xo_ref[...] = xv
        else:
            x_ref, g_ref, sc_ref, sh_ref, h_ref = refs
            xv = x_ref[...]
        r = lax.rsqrt(jnp.mean(xv * xv, axis=-1, keepdims=True) + EPS)
        y = xv * r * g_ref[...]
        h_ref[...] = (y * (1.0 + sc_ref[...]) + sh_ref[...]).astype(h_ref.dtype)

    row = pl.BlockSpec((tt, d), lambda i: (i, 0))
    vec = pl.BlockSpec((1, d), lambda i: (0, 0))
    if has:
        return pl.pallas_call(
            body, name=name, grid=(t // tt,), in_specs=[row, row, vec, vec, vec, vec], out_specs=[row, row],
            out_shape=[jax.ShapeDtypeStruct((t, d), F32), jax.ShapeDtypeStruct((t, d), MXU_DTYPE)],
            compiler_params=_params(("parallel",)))(x, delta, gt, g, sc, sh)
    h = pl.pallas_call(
        body, name=name + "_first", grid=(t // tt,), in_specs=[row, vec, vec, vec], out_specs=row,
        out_shape=jax.ShapeDtypeStruct((t, d), MXU_DTYPE), compiler_params=_params(("parallel",)))(x, g, sc, sh)
    return x, h


def _final_loss(x, delta, gt, g, target, name):
    t, d = x.shape
    tt = _tile(t, _ROW_TILES)

    def body(x_ref, d_ref, gt_ref, g_ref, tg_ref, dx_ref, dg_ref, loss_ref):
        @pl.when(pl.program_id(0) == 0)
        def _():
            dg_ref[...] = jnp.zeros_like(dg_ref)
            loss_ref[...] = jnp.zeros_like(loss_ref)

        xv = x_ref[...] + gt_ref[...] * d_ref[...]
        r = lax.rsqrt(jnp.mean(xv * xv, axis=-1, keepdims=True) + EPS)
        xh = xv * r
        diff = xh * g_ref[...] - tg_ref[...]
        loss_ref[...] += jnp.sum(diff * diff) * (0.5 / d)
        dy = diff * (1.0 / d)
        dg_ref[...] += jnp.sum(dy * xh, axis=0, keepdims=True)
        dxh = dy * g_ref[...]
        dx_ref[...] = r * (dxh - xh * jnp.mean(dxh * xh, axis=-1, keepdims=True))

    row = pl.BlockSpec((tt, d), lambda i: (i, 0))
    vec = pl.BlockSpec((1, d), lambda i: (0, 0))
    tile = pl.BlockSpec((SUBLANES, LANES), lambda i: (0, 0))
    return pl.pallas_call(
        body, name=name, grid=(t // tt,), in_specs=[row, row, vec, vec, row], out_specs=[row, vec, tile],
        out_shape=[jax.ShapeDtypeStruct((t, d), F32), jax.ShapeDtypeStruct((1, d), F32),
                   jax.ShapeDtypeStruct((SUBLANES, LANES), F32)],
        compiler_params=_params(("arbitrary",)))(x, delta, gt, g, target)


def _norm_bwd(x, dh, dres, g, sc, name):
    t, d = x.shape
    tt = _tile(t, _ROW_TILES)

    def body(x_ref, dh_ref, dr_ref, g_ref, sc_ref, dx_ref, dsh_ref, dsc_ref, dg_ref):
        @pl.when(pl.program_id(0) == 0)
        def _():
            dsh_ref[...] = jnp.zeros_like(dsh_ref)
            dsc_ref[...] = jnp.zeros_like(dsc_ref)
            dg_ref[...] = jnp.zeros_like(dg_ref)

        xv, dh = x_ref[...], dh_ref[...]
        r = lax.rsqrt(jnp.mean(xv * xv, axis=-1, keepdims=True) + EPS)
        xh = xv * r
        gv, sc1 = g_ref[...], 1.0 + sc_ref[...]
        dsh_ref[...] += jnp.sum(dh, axis=0, keepdims=True)
        dsc_ref[...] += jnp.sum(dh * xh, axis=0, keepdims=True) * gv
        dg_ref[...] += jnp.sum(dh * xh, axis=0, keepdims=True) * sc1
        dxh = dh * (gv * sc1)
        dx_ref[...] = dr_ref[...] + r * (dxh - xh * jnp.mean(dxh * xh, axis=-1, keepdims=True))

    row = pl.BlockSpec((tt, d), lambda i: (i, 0))
    vec = pl.BlockSpec((1, d), lambda i: (0, 0))
    vshape = jax.ShapeDtypeStruct((1, d), F32)
    return pl.pallas_call(
        body, name=name, grid=(t // tt,), in_specs=[row, row, row, vec, vec], out_specs=[row, vec, vec, vec],
        out_shape=[jax.ShapeDtypeStruct((t, d), F32), vshape, vshape, vshape],
        compiler_params=_params(("arbitrary",)))(x, dh, dres, g, sc)


def _gate_bwd(dxo, branch, gt, name):
    t, d = dxo.shape
    tt = _tile(t, _ROW_TILES)

    def body(dx_ref, br_ref, gt_ref, db_ref, dgt_ref):
        @pl.when(pl.program_id(0) == 0)
        def _():
            dgt_ref[...] = jnp.zeros_like(dgt_ref)

        dx = dx_ref[...]
        db_ref[...] = (dx * gt_ref[...]).astype(db_ref.dtype)
        dgt_ref[...] += jnp.sum(dx * br_ref[...], axis=0, keepdims=True)

    row = pl.BlockSpec((tt, d), lambda i: (i, 0))
    vec = pl.BlockSpec((1, d), lambda i: (0, 0))
    return pl.pallas_call(
        body, name=name, grid=(t // tt,), in_specs=[row, row, vec], out_specs=[row, vec],
        out_shape=[jax.ShapeDtypeStruct((t, d), MXU_DTYPE), jax.ShapeDtypeStruct((1, d), F32)],
        compiler_params=_params(("arbitrary",)))(dxo, branch, gt)


def _swiglu_fwd(gu, name):
    t, f2 = gu.shape
    f = f2 // 2
    tt = _tile(t, (256, 128))

    def body(g_ref, u_ref, o_ref):
        o_ref[...] = (_silu(g_ref[...]) * u_ref[...]).astype(o_ref.dtype)

    return pl.pallas_call(
        body, name=name, grid=(t // tt,),
        in_specs=[pl.BlockSpec((tt, f), lambda i: (i, 0)), pl.BlockSpec((tt, f), lambda i: (i, 1))],
        out_specs=pl.BlockSpec((tt, f), lambda i: (i, 0)), out_shape=jax.ShapeDtypeStruct((t, f), MXU_DTYPE),
        compiler_params=_params(("parallel",)))(gu, gu)


def _swiglu_bwd(gu, da, name):
    t, f2 = gu.shape
    f = f2 // 2
    tt = _tile(t, (256, 128))

    def body(g_ref, u_ref, da_ref, o_ref):
        gate, da = g_ref[...], da_ref[...]
        o_ref[:, :f] = (da * u_ref[...] * _dsilu(gate)).astype(o_ref.dtype)
        o_ref[:, f:] = (da * _silu(gate)).astype(o_ref.dtype)

    return pl.pallas_call(
        body, name=name, grid=(t // tt,),
        in_specs=[pl.BlockSpec((tt, f), lambda i: (i, 0)), pl.BlockSpec((tt, f), lambda i: (i, 1)),
                  pl.BlockSpec((tt, f), lambda i: (i, 0))],
        out_specs=pl.BlockSpec((tt, f2), lambda i: (i, 0)), out_shape=jax.ShapeDtypeStruct((t, f2), MXU_DTYPE),
        compiler_params=_params(("parallel",)))(gu, gu, da)


class _ProjLayout:
    def __init__(self, d):
        wc = 3 * HEADS * HEAD_DIM
        self.d, self.wc = d, wc
        self.qkv, self.z, self.uv, self.gates, self.ba = 0, wc, wc + d, wc + 3 * d, wc + 5 * d
        self.width = self.ba + LANES
        assert self.z % d == 0 and self.uv % (2 * d) == 0 and self.gates % (2 * d) == 0 and self.ba % LANES == 0

    def pieces(self, shard):
        d, wc, out, lo = self.d, self.wc, [], 0
        for length, dst in ((2 * d, self.uv), (wc, self.qkv), (d, self.z), (2 * HEADS, self.ba), (2 * d, self.gates)):
            pos = lo
            while pos < lo + length:
                j = pos // shard
                n = min(lo + length, (j + 1) * shard) - pos
                out.append((j, pos - j * shard, n, dst + pos - lo))
                pos += n
            lo += length
        return out


def _merge_fwd(pa, pb, proj, gcol, name):
    t, d = pa.shape
    tt = _tile(t, _ROW_TILES)

    def body(pa_ref, pb_ref, ga_ref, gb_ref, o_ref):
        o_ref[...] = (_sigmoid(ga_ref[...]) * pa_ref[...] + _sigmoid(gb_ref[...]) * pb_ref[...]).astype(o_ref.dtype)

    row = pl.BlockSpec((tt, d), lambda i: (i, 0))
    gate = lambda k: pl.BlockSpec((tt, d), lambda i: (i, gcol // d + k))
    return pl.pallas_call(
        body, name=name, grid=(t // tt,), in_specs=[row, row, gate(0), gate(1)], out_specs=row,
        out_shape=jax.ShapeDtypeStruct((t, d), MXU_DTYPE), compiler_params=_params(("parallel",)))(pa, pb, proj, proj)


def _merge_bwd(dm, pa, pb, proj, gcol, dproj, name):
    t, d = pa.shape
    tt = _tile(t, _ROW_TILES)

    def body(dm_ref, pa_ref, pb_ref, ga_ref, gb_ref, _, dpa_ref, dpb_ref, dg_ref):
        dm = dm_ref[...]
        sa, sb = _sigmoid(ga_ref[...]), _sigmoid(gb_ref[...])
        dpa_ref[...] = (dm * sa).astype(dpa_ref.dtype)
        dpb_ref[...] = (dm * sb).astype(dpb_ref.dtype)
        dg_ref[:, :d] = (dm * pa_ref[...] * sa * (1.0 - sa)).astype(dg_ref.dtype)
        dg_ref[:, d:] = (dm * pb_ref[...] * sb * (1.0 - sb)).astype(dg_ref.dtype)

    row = pl.BlockSpec((tt, d), lambda i: (i, 0))
    gate = lambda k: pl.BlockSpec((tt, d), lambda i: (i, gcol // d + k))
    wide = pl.BlockSpec((tt, 2 * d), lambda i: (i, gcol // (2 * d)))
    return pl.pallas_call(
        body, name=name, grid=(t // tt,), in_specs=[row, row, row, gate(0), gate(1), _ANY], out_specs=[row, row, wide],
        out_shape=[jax.ShapeDtypeStruct((t, d), MXU_DTYPE), jax.ShapeDtypeStruct((t, d), MXU_DTYPE),
                   jax.ShapeDtypeStruct(dproj.shape, dproj.dtype)],
        input_output_aliases={5: 2}, compiler_params=_params(("parallel",)))(dm, pa, pb, proj, proj, dproj)


def _tri_masks(n):
    ri = lax.broadcasted_iota(jnp.int32, (n, n), 0)
    ci = lax.broadcasted_iota(jnp.int32, (n, n), 1)
    return ri >= ci, ri > ci, ri == ci


def _mixer_a_fwd(proj, ucol, w_s, b_col, g_v, name):
    t, w = proj.shape[0], g_v.shape[1]
    c = A_CHUNK

    def body(u_ref, v_ref, w_ref, b_ref, gv_ref, y_ref):
        tril, _, _ = _tri_masks(c)
        ug, vg = _gelu(u_ref[...]), _gelu(v_ref[...])
        for g in range(GROUPS):
            sl = slice(g * c, (g + 1) * c)
            vt = vg[:, sl]
            r = lax.rsqrt(jnp.mean(vt * vt, axis=-1, keepdims=True) + EPS)
            vn = vt * r * gv_ref[:, sl]
            s = _mm(jnp.where(tril, w_ref[g], 0.0), vn) + b_ref[g]
            y_ref[:, sl] = (ug[:, sl] * s).astype(y_ref.dtype)

    return pl.pallas_call(
        body, name=name, grid=(t // c,),
        in_specs=[pl.BlockSpec((c, w), lambda i: (i, ucol // w)), pl.BlockSpec((c, w), lambda i: (i, ucol // w + 1)),
                  pl.BlockSpec((GROUPS, c, c), lambda i: (0, 0, 0)), pl.BlockSpec((GROUPS, c, 1), lambda i: (0, 0, 0)),
                  pl.BlockSpec((1, w), lambda i: (0, 0))],
        out_specs=pl.BlockSpec((c, w), lambda i: (i, 0)), out_shape=jax.ShapeDtypeStruct((t, w), MXU_DTYPE),
        compiler_params=_params(("parallel",)))(proj, proj, w_s, b_col, g_v)


def _mixer_a_bwd(proj, ucol, dy, w_s, w_st, b_col, g_v, dproj, name):
    t, w = proj.shape[0], g_v.shape[1]
    w2 = 2 * w
    c = A_CHUNK

    def body(u_ref, v_ref, dy_ref, w_ref, wt_ref, b_ref, gv_ref, _, duv_ref, dw_ref, db_ref, dgv_ref):
        @pl.when(pl.program_id(0) == 0)
        def _():
            dw_ref[...] = jnp.zeros_like(dw_ref)
            db_ref[...] = jnp.zeros_like(db_ref)
            dgv_ref[...] = jnp.zeros_like(dgv_ref)

        tril, _, _ = _tri_masks(c)
        triu = lax.broadcasted_iota(jnp.int32, (c, c), 0) <= lax.broadcasted_iota(jnp.int32, (c, c), 1)
        up, vp = u_ref[...], v_ref[...]
        ug, vg = _gelu(up), _gelu(vp)
        for g in range(GROUPS):
            sl = slice(g * c, (g + 1) * c)
            vt = vg[:, sl]
            r = lax.rsqrt(jnp.mean(vt * vt, axis=-1, keepdims=True) + EPS)
            vh = vt * r
            gv = gv_ref[:, sl]
            vn = vh * gv
            s = _mm(jnp.where(tril, w_ref[g], 0.0), vn) + b_ref[g]
            dy = dy_ref[:, sl]
            ds = dy * ug[:, sl]
            dw_ref[g] += jnp.where(tril, _mm(ds, vn, _NT), 0.0)
            db_ref[g] += jnp.sum(ds, axis=1, keepdims=True)
            dvn = _mm(jnp.where(triu, wt_ref[g], 0.0), ds)
            dgv_ref[:, sl] += jnp.sum(dvn * vh, axis=0, keepdims=True)
            dvh = dvn * gv
            dvt = r * (dvh - vh * jnp.mean(dvh * vh, axis=-1, keepdims=True))
            duv_ref[:, sl] = (dy * s * _dgelu(up[:, sl])).astype(duv_ref.dtype)
            duv_ref[:, w + g * c:w + (g + 1) * c] = (dvt * _dgelu(vp[:, sl])).astype(duv_ref.dtype)

    full3 = lambda shape: pl.BlockSpec(shape, lambda i: (0, 0, 0))
    return pl.pallas_call(
        body, name=name, grid=(t // c,),
        in_specs=[pl.BlockSpec((c, w), lambda i: (i, ucol // w)), pl.BlockSpec((c, w), lambda i: (i, ucol // w + 1)),
                  pl.BlockSpec((c, w), lambda i: (i, 0)), full3((GROUPS, c, c)), full3((GROUPS, c, c)),
                  full3((GROUPS, c, 1)), pl.BlockSpec((1, w), lambda i: (0, 0)), _ANY],
        out_specs=[pl.BlockSpec((c, w2), lambda i: (i, ucol // w2)), full3((GROUPS, c, c)), full3((GROUPS, c, 1)),
                   pl.BlockSpec((1, w), lambda i: (0, 0))],
        out_shape=[jax.ShapeDtypeStruct(dproj.shape, dproj.dtype), jax.ShapeDtypeStruct((GROUPS, c, c), F32),
                   jax.ShapeDtypeStruct((GROUPS, c, 1), F32), jax.ShapeDtypeStruct((1, w), F32)],
        input_output_aliases={7: 0},
        compiler_params=_params(("arbitrary",)))(proj, proj, dy, w_s, w_st, b_col, g_v, dproj)


_Q_SCALE = HEAD_DIM ** -0.5


def _conv_taps(ext, w_ref):
    shifted = [ext[SUBLANES:]] + [pltpu.roll(ext, s, 0)[SUBLANES:] for s in range(1, CONV_K)]
    acc = shifted[0] * w_ref[pl.ds(CONV_K - 1, 1), :]
    for s in range(1, CONV_K):
        acc = acc + shifted[s] * w_ref[pl.ds(CONV_K - 1 - s, 1), :]
    return acc, shifted


def _conv_fwd(qkv, w, name):
    t, cw = qkv.shape[0], w.shape[1]
    tt = _tile(t, (256, 128))
    hb = tt // SUBLANES

    def body(x_ref, p_ref, w_ref, o_ref):
        prev = jnp.where(pl.program_id(0) > 0, p_ref[...], 0.0)
        acc, _ = _conv_taps(jnp.concatenate([prev, x_ref[...]], axis=0), w_ref)
        y = _silu(acc)
        for which in range(3):
            for h in range(HEADS):
                lo = (which * HEADS + h) * HEAD_DIM
                seg = y[:, lo:lo + HEAD_DIM]
                if which < 2:
                    seg = seg * lax.rsqrt(jnp.sum(seg * seg, axis=-1, keepdims=True) + EPS)
                if which == 0:
                    seg = seg * _Q_SCALE
                o_ref[which, h] = seg

    return pl.pallas_call(
        body, name=name, grid=(t // tt,),
        in_specs=[pl.BlockSpec((tt, cw), lambda i: (i, 0)),
                  pl.BlockSpec((SUBLANES, cw), lambda i: (jnp.maximum(i * hb - 1, 0), 0)),
                  pl.BlockSpec((CONV_K, cw), lambda i: (0, 0))],
        out_specs=pl.BlockSpec((3, HEADS, tt, HEAD_DIM), lambda i: (0, 0, i, 0)),
        out_shape=jax.ShapeDtypeStruct((3, HEADS, t, HEAD_DIM), F32),
        compiler_params=_params(("parallel",)))(qkv, qkv, w)


def _conv_bwd_pre(qkv, dq, dk, dv, w, name):
    t, cw = qkv.shape[0], w.shape[1]
    tt = _tile(t, (256, 128))
    hb = tt // SUBLANES

    def body(x_ref, p_ref, dq_ref, dk_ref, dv_ref, w_ref, da_ref, dw_ref):
        @pl.when(pl.program_id(0) == 0)
        def _():
            dw_ref[...] = jnp.zeros_like(dw_ref)

        prev = jnp.where(pl.program_id(0) > 0, p_ref[...], 0.0)
        acc, shifted = _conv_taps(jnp.concatenate([prev, x_ref[...]], axis=0), w_ref)
        y = _silu(acc)
        d_refs = (dq_ref, dk_ref, dv_ref)
        for which in range(3):
            for h in range(HEADS):
                lo = (which * HEADS + h) * HEAD_DIM
                sl = slice(lo, lo + HEAD_DIM)
                dn = d_refs[which][h]
                if which < 2:
                    seg = y[:, sl]
                    rho = lax.rsqrt(jnp.sum(seg * seg, axis=-1, keepdims=True) + EPS)
                    nrm = seg * rho
                    if which == 0:
                        dn = dn * _Q_SCALE
                    dn = rho * (dn - nrm * jnp.sum(dn * nrm, axis=-1, keepdims=True))
                dacc = dn * _dsilu(acc[:, sl])
                da_ref[:, sl] = dacc
                for s in range(CONV_K):
                    dw_ref[pl.ds(CONV_K - 1 - s, 1), sl] += jnp.sum(dacc * shifted[s][:, sl], axis=0, keepdims=True)

    head = pl.BlockSpec((HEADS, tt, HEAD_DIM), lambda i: (0, i, 0))
    return pl.pallas_call(
        body, name=name, grid=(t // tt,),
        in_specs=[pl.BlockSpec((tt, cw), lambda i: (i, 0)),
                  pl.BlockSpec((SUBLANES, cw), lambda i: (jnp.maximum(i * hb - 1, 0), 0)),
                  head, head, head, pl.BlockSpec((CONV_K, cw), lambda i: (0, 0))],
        out_specs=[pl.BlockSpec((tt, cw), lambda i: (i, 0)), pl.BlockSpec((CONV_K, cw), lambda i: (0, 0))],
        out_shape=[jax.ShapeDtypeStruct((t, cw), F32), jax.ShapeDtypeStruct((CONV_K, cw), F32)],
        compiler_params=_params(("arbitrary",)))(qkv, qkv, dq, dk, dv, w)


def _conv_bwd_in(dacc, w, dproj, name):
    t, cw = dacc.shape
    tt = _tile(t, (256, 128))
    hb = tt // SUBLANES
    nt = t // tt
    rows = tt + SUBLANES

    def body(d_ref, n_ref, w_ref, _, o_ref):
        cur = d_ref[...]
        nxt = jnp.where(pl.program_id(0) < nt - 1, n_ref[...], 0.0)
        ext = jnp.concatenate([cur, nxt], axis=0)
        acc = cur * w_ref[pl.ds(CONV_K - 1, 1), :]
        for s in range(1, CONV_K):
            acc = acc + pltpu.roll(ext, rows - s, 0)[:tt] * w_ref[pl.ds(CONV_K - 1 - s, 1), :]
        o_ref[...] = acc.astype(o_ref.dtype)

    return pl.pallas_call(
        body, name=name, grid=(nt,),
        in_specs=[pl.BlockSpec((tt, cw), lambda i: (i, 0)),
                  pl.BlockSpec((SUBLANES, cw), lambda i: (jnp.minimum((i + 1) * hb, t // SUBLANES - 1), 0)),
                  pl.BlockSpec((CONV_K, cw), lambda i: (0, 0)), _ANY],
        out_specs=pl.BlockSpec((tt, cw), lambda i: (i, 0)), out_shape=jax.ShapeDtypeStruct(dproj.shape, dproj.dtype),
        input_output_aliases={3: 0}, compiler_params=_params(("parallel",)))(dacc, dacc, w, dproj)


_INV_BASE_SHIFT = 3


def _inv_unit_lower(a, eye):
    c = GDN_CHUNK
    ri = lax.broadcasted_iota(jnp.int32, (c, c), 0)
    ci = lax.broadcasted_iota(jnp.int32, (c, c), 1)
    same = lambda sh: (ri >> sh) == (ci >> sh)
    x = jnp.where(same(_INV_BASE_SHIFT), -a, 0.0)
    p = jnp.where(eye, 1.0, 0.0) + x
    xs = _split(x)
    x2 = _mm3(xs, xs)
    x2s, ps = _split(x2), _split(p)
    r = _mm3(x2s, tuple(jnp.concatenate([u, v], axis=-1) for u, v in zip(x2s, ps)))
    x4, p = r[..., :c], p + r[..., c:]
    p = p + _mm3(_split(x4), _split(p))
    for sh in range(_INV_BASE_SHIFT, c.bit_length() - 1):
        off = jnp.where(same(sh + 1) & jnp.logical_not(same(sh)), a, 0.0)
        ps = _split(p)
        p = p - _mm3(ps, _split(_mm3(_split(off), ps)))
    return p


def _split(a):
    hi = a.astype(BF16)
    return hi, (a - hi.astype(F32)).astype(BF16)


def _dot_heads(u, v, dims):
    if u.ndim == 3:
        return jnp.stack([_dot_heads(u[j], v[j], dims) for j in range(u.shape[0])])
    return lax.dot_general(u, v, dims, preferred_element_type=F32)


def _mm3(a, b):
    return _dot_heads(a[0], b[0], _NN) + (_dot_heads(a[0], b[1], _NN) + _dot_heads(a[1], b[0], _NN))


def _hmm(a, b, dims=_NN):
    return _dot_heads(a.astype(MXU_DTYPE), b.astype(MXU_DTYPE), dims)


def _rowsum(x):
    return jnp.sum(x, axis=-1, keepdims=True)


def _colsum(x):
    return jnp.sum(x, axis=-2, keepdims=True)


class _Pre:
    pass


def _gdn_pre(q, k, v, araw, braw, alog, dtb, t_mat=None):
    c = GDN_CHUNK
    p = _Pre()
    p.tril, p.strict, p.eye = _tri_masks(c)
    p.to_col = lambda row: _rowsum(jnp.where(p.eye, row, 0.0))
    p.to_row = lambda col: _colsum(jnp.where(p.eye, col, 0.0))
    p.a_neg = -jnp.exp(alog + jnp.zeros((1, c), F32))
    p.xg = araw + dtb
    p.g_row = p.a_neg * _softplus(p.xg)
    p.beta_row = _sigmoid(braw)
    p.beta = p.to_col(p.beta_row)
    gam = _rowsum(jnp.where(p.tril, p.g_row, 0.0))
    gam_last = _rowsum(p.g_row)
    p.dm = jnp.where(p.tril, jnp.exp(jnp.where(p.tril, gam - p.to_row(gam), 0.0)), 0.0)
    p.e, p.ek, p.el = jnp.exp(gam), jnp.exp(gam_last - gam), jnp.exp(gam_last)
    p.kb = k * p.beta
    p.kk = _hmm(p.kb, k, _NT)
    p.t = _inv_unit_lower(jnp.where(p.strict, p.kk * p.dm, 0.0), p.eye) if t_mat is None else t_mat
    p.vb, p.kbe = v * p.beta, p.kb * p.e
    uw = _hmm(p.t, jnp.concatenate([p.vb, p.kbe], axis=-1))
    p.u, p.w = uw[..., :v.shape[-1]], uw[..., v.shape[-1]:]
    p.qk0 = _hmm(q, k, _NT)
    p.qk = p.qk0 * p.dm
    p.qd, p.kd = q * p.e, k * p.ek
    return p


GDN_HEADS_PER_STEP = 8


def _head_scalars(ref, hb):
    h0 = pl.program_id(0) * hb
    return jnp.stack([jnp.full((1, 1), ref[h0 + j], F32) for j in range(hb)])


def _gdn_specs(n, reverse):
    c, dk, hb = GDN_CHUNK, HEAD_DIM, GDN_HEADS_PER_STEP
    ix = (lambda i: n - 1 - i) if reverse else (lambda i: i)
    smem = pl.BlockSpec(memory_space=pltpu.SMEM)
    qkv = [pl.BlockSpec((None, hb, c, dk), functools.partial(lambda w, h, i: (w, h, ix(i), 0), w)) for w in range(3)]
    row = pl.BlockSpec((hb, None, 1, c), lambda h, i: (h, ix(i), 0, 0))
    tok = pl.BlockSpec((hb, c, dk), lambda h, i: (h, ix(i), 0))
    state = pl.BlockSpec((hb, None, dk, dk), lambda h, i: (h, ix(i), 0, 0))
    return smem, qkv, row, tok, state


def _gdn_fwd(qkv_h, araw, braw, alog, dtb, name, side=_NoSide):
    _, hh, t, dk = qkv_h.shape
    n, hb = t // GDN_CHUNK, GDN_HEADS_PER_STEP
    smem, qkv, row, tok, state = _gdn_specs(n, False)
    grid = (hh // hb, n)

    def body(*refs):
        main, side_start, side_finish = _side_hooks(side, refs, 7, 3, 1, grid)
        alog_ref, dt_ref, q_ref, k_ref, v_ref, a_ref, b_ref, o_ref, so_ref, to_ref, s_ref = main
        side_start()

        @pl.when(pl.program_id(1) == 0)
        def _():
            s_ref[...] = jnp.zeros_like(s_ref)

        p = _gdn_pre(q_ref[...], k_ref[...], v_ref[...], a_ref[...], b_ref[...],
                     _head_scalars(alog_ref, hb), _head_scalars(dt_ref, hb))
        s = s_ref[...]
        vn = p.u - _hmm(p.w, s)
        o_ref[...] = _hmm(p.qd, s) + _hmm(p.qk, vn)
        so_ref[...] = s
        to_ref[...] = p.t
        s_ref[...] = s * p.el + _hmm(p.kd, vn, _TN)
        side_finish()

    mats = jax.ShapeDtypeStruct((hh, n, dk, dk), F32)
    return _carrier_call(
        body, name, grid, [smem, smem] + qkv + [row, row], [tok, state, state],
        [jax.ShapeDtypeStruct((hh, t, dk), F32), mats, mats],
        [pltpu.VMEM((hb, dk, dk), F32)], side, (alog, dtb, qkv_h, qkv_h, qkv_h, araw, braw))


def _gdn_bwd(qkv_h, araw, braw, alog, dtb, states, t_mats, do, name, side=_NoSide):
    _, hh, t, dk = qkv_h.shape
    c, hb = GDN_CHUNK, GDN_HEADS_PER_STEP
    n = t // c
    smem, qkv, row, tok, state = _gdn_specs(n, True)
    acc = pl.BlockSpec((hb, 1, LANES), lambda h, i: (h, 0, 0))
    grid = (hh // hb, n)

    def body(*refs):
        main, side_start, side_finish = _side_hooks(side, refs, 10, 7, 1, grid)
        (alog_ref, dt_ref, q_ref, k_ref, v_ref, a_ref, b_ref, s_ref, t_ref, do_ref,
         dq_ref, dk_ref, dv_ref, da_ref, db_ref, dal_ref, ddt_ref, ds_ref) = main
        side_start()

        @pl.when(pl.program_id(1) == 0)
        def _():
            ds_ref[...] = jnp.zeros_like(ds_ref)
            dal_ref[...] = jnp.zeros_like(dal_ref)
            ddt_ref[...] = jnp.zeros_like(ddt_ref)

        q, k, v = q_ref[...], k_ref[...], v_ref[...]
        p = _gdn_pre(q, k, v, a_ref[...], b_ref[...], _head_scalars(alog_ref, hb), _head_scalars(dt_ref, hb),
                     t_ref[...])
        s, do, dsp = s_ref[...], do_ref[...], ds_ref[...]
        vn = p.u - _hmm(p.w, s)
        dqd = _hmm(do, s, _NT)
        dqk = _hmm(do, vn, _NT)
        dvn = _hmm(p.qk, do, _TN) + _hmm(p.kd, dsp)
        dkd = _hmm(vn, dsp, _NT)
        d_el = _colsum(_rowsum(s * dsp))
        ds_ref[...] = dsp * p.el + _hmm(p.qd, do, _TN) - _hmm(p.w, dvn, _TN)
        dw = -_hmm(dvn, s, _NT)
        d_t = _hmm(dvn, p.vb, _NT) + _hmm(dw, p.kbe, _NT)
        dvb, dkbe = _hmm(p.t, dvn, _TN), _hmm(p.t, dw, _TN)
        d_a = jnp.where(p.strict, -_hmm(p.t, _hmm(d_t, p.t, _NT), _TN), 0.0)
        dkk = d_a * p.dm
        dqk0 = dqk * p.dm
        ddm = d_a * p.kk + dqk * p.qk0
        dkb = _hmm(dkk, k) + dkbe * p.e
        dq_ref[...] = _hmm(dqk0, k) + dqd * p.e
        dk_ref[...] = _hmm(dkk, p.kb, _TN) + _hmm(dqk0, q, _TN) + dkd * p.ek + dkb * p.beta
        dv_ref[...] = dvb * p.beta
        dbeta = _rowsum(dkb * k) + _rowsum(dvb * v)
        d_e = _rowsum(dqd * q) + _rowsum(dkbe * p.kb)
        d_ek = _rowsum(dkd * k)
        m = ddm * p.dm
        dgam = d_e * p.e - d_ek * p.ek + _rowsum(m) - p.to_col(_colsum(m))
        dgam_last = _colsum(d_ek * p.ek) + d_el * p.el
        dg_row = _colsum(jnp.where(p.tril, dgam, 0.0)) + dgam_last
        da_row = dg_row * p.a_neg * _sigmoid(p.xg)
        da_ref[...] = da_row
        db_ref[...] = p.to_row(dbeta) * p.beta_row * (1.0 - p.beta_row)
        dal_ref[...] += _rowsum(dg_row * p.g_row)
        ddt_ref[...] += _rowsum(da_row)
        side_finish()

    tok_shape = jax.ShapeDtypeStruct((hh, t, dk), F32)
    row_shape = jax.ShapeDtypeStruct((hh, n, 1, c), F32)
    acc_shape = jax.ShapeDtypeStruct((hh, 1, LANES), F32)
    return _carrier_call(
        body, name, grid, [smem, smem] + qkv + [row, row, state, state, tok], [tok, tok, tok, row, row, acc, acc],
        [tok_shape, tok_shape, tok_shape, row_shape, row_shape, acc_shape, acc_shape],
        [pltpu.VMEM((hb, dk, dk), F32)], side, (alog, dtb, qkv_h, qkv_h, qkv_h, araw, braw, states, t_mats, do))


def _gdn_post_fwd(o, proj, zcol, g_o, name):
    hh, t, dv = o.shape
    tt = _tile(t, _ROW_TILES)
    zblk = zcol // (hh * dv)

    def body(o_ref, z_ref, g_ref, y_ref):
        for h in range(hh):
            sl = slice(h * dv, (h + 1) * dv)
            ov = o_ref[h]
            r = lax.rsqrt(jnp.mean(ov * ov, axis=-1, keepdims=True) + EPS)
            y_ref[:, sl] = (ov * r * g_ref[...] * _silu(z_ref[:, sl])).astype(y_ref.dtype)

    return pl.pallas_call(
        body, name=name, grid=(t // tt,),
        in_specs=[pl.BlockSpec((hh, tt, dv), lambda i: (0, i, 0)), pl.BlockSpec((tt, hh * dv), lambda i: (i, zblk)),
                  pl.BlockSpec((1, dv), lambda i: (0, 0))],
        out_specs=pl.BlockSpec((tt, hh * dv), lambda i: (i, 0)),
        out_shape=jax.ShapeDtypeStruct((t, hh * dv), MXU_DTYPE), compiler_params=_params(("parallel",)))(o, proj, g_o)


def _gdn_post_bwd(o, proj, zcol, dy, g_o, dproj, name):
    hh, t, dv = o.shape
    tt = _tile(t, _ROW_TILES)
    zblk = zcol // (hh * dv)

    def body(o_ref, z_ref, dy_ref, g_ref, _, do_ref, dz_ref, dg_ref):
        @pl.when(pl.program_id(0) == 0)
        def _():
            dg_ref[...] = jnp.zeros_like(dg_ref)

        gv = g_ref[...]
        for h in range(hh):
            sl = slice(h * dv, (h + 1) * dv)
            ov, zz, dy = o_ref[h], z_ref[:, sl], dy_ref[:, sl]
            r = lax.rsqrt(jnp.mean(ov * ov, axis=-1, keepdims=True) + EPS)
            oh = ov * r
            dz_ref[:, sl] = (dy * oh * gv * _dsilu(zz)).astype(dz_ref.dtype)
            don = dy * _silu(zz)
            dg_ref[...] += _colsum(don * oh)
            doh = don * gv
            do_ref[h] = r * (doh - oh * jnp.mean(doh * oh, axis=-1, keepdims=True))

    return pl.pallas_call(
        body, name=name, grid=(t // tt,),
        in_specs=[pl.BlockSpec((hh, tt, dv), lambda i: (0, i, 0)), pl.BlockSpec((tt, hh * dv), lambda i: (i, zblk)),
                  pl.BlockSpec((tt, hh * dv), lambda i: (i, 0)), pl.BlockSpec((1, dv), lambda i: (0, 0)), _ANY],
        out_specs=[pl.BlockSpec((hh, tt, dv), lambda i: (0, i, 0)), pl.BlockSpec((tt, hh * dv), lambda i: (i, zblk)),
                   pl.BlockSpec((1, dv), lambda i: (0, 0))],
        out_shape=[jax.ShapeDtypeStruct((hh, t, dv), F32), jax.ShapeDtypeStruct(dproj.shape, dproj.dtype),
                   jax.ShapeDtypeStruct((1, dv), F32)],
        input_output_aliases={4: 1}, compiler_params=_params(("arbitrary",)))(o, proj, dy, g_o, dproj)


def _write_cols(dst, src, col, name):
    t, w = src.shape
    tt = _tile(t, _ROW_TILES)

    def body(s_ref, _, o_ref):
        o_ref[...] = s_ref[...].astype(o_ref.dtype)

    return pl.pallas_call(
        body, name=name, grid=(t // tt,), in_specs=[pl.BlockSpec((tt, w), lambda i: (i, 0)), _ANY],
        out_specs=pl.BlockSpec((tt, w), lambda i: (i, col // w)), out_shape=jax.ShapeDtypeStruct(dst.shape, dst.dtype),
        input_output_aliases={1: 0}, compiler_params=_params(("parallel",)))(src, dst)


def _adamw(g, w, m, v):
    m = ADAM_B1 * m + (1.0 - ADAM_B1) * g
    v = ADAM_B2 * v + (1.0 - ADAM_B2) * (g * g)
    m_hat = m / (1.0 - ADAM_B1 ** ADAM_STEP)
    v_hat = v / (1.0 - ADAM_B2 ** ADAM_STEP)
    return -ADAM_LR * (m_hat / (jnp.sqrt(v_hat) + ADAM_EPS) + ADAM_WD * w), m, v


def _ada_fwd(c_all, ada_w, name):
    nl, d, cols = ada_w.shape
    b = c_all.shape[0]

    def body(c_ref, w_ref, o_ref):
        o_ref[...] = _mm_hi(_silu(c_ref[...]), w_ref[...])

    return pl.pallas_call(
        body, name=name, grid=(nl,),
        in_specs=[pl.BlockSpec((b, d), lambda i: (0, 0)), pl.BlockSpec((None, d, cols), lambda i: (i, 0, 0))],
        out_specs=pl.BlockSpec((None, b, cols), lambda i: (i, 0, 0)),
        out_shape=jax.ShapeDtypeStruct((nl, b, cols), F32), compiler_params=_params(("parallel",)))(c_all, ada_w)


def _ada_bwd(c_col, dm, w, m, v, name):
    nl, d, cols = w.shape
    b = c_col.shape[0]
    tr = _tile(d, (256, 128))

    def body(c_ref, dm_ref, w_ref, m_ref, v_ref, g_ref, dl_ref, mo_ref, vo_ref):
        g = _silu(c_ref[0]) * dm_ref[pl.ds(0, 1), :]
        for j in range(1, b):
            g = g + _silu(c_ref[j]) * dm_ref[pl.ds(j, 1), :]
        g_ref[...] = g
        dl_ref[...], mo_ref[...], vo_ref[...] = _adamw(g, w_ref[...], m_ref[...], v_ref[...])

    blk = pl.BlockSpec((None, tr, cols), lambda l, i: (l, i, 0))
    shape = jax.ShapeDtypeStruct((nl, d, cols), F32)
    return pl.pallas_call(
        body, name=name, grid=(nl, d // tr),
        in_specs=[pl.BlockSpec((b, tr, 1), lambda l, i: (0, i, 0)), pl.BlockSpec((None, b, cols), lambda l, i: (l, 0, 0)),
                  blk, blk, blk],
        out_specs=[blk, blk, blk, blk], out_shape=[shape] * 4,
        compiler_params=_params(("parallel", "parallel")))(c_col, dm, w, m, v)


def _sum_adam(parts, w, m, v, name):
    nl, npart, r, cdim = parts.shape
    tr = _tile(r, (256, 128))

    def body(p_ref, w_ref, m_ref, v_ref, g_ref, dl_ref, mo_ref, vo_ref):
        g = p_ref[0].astype(F32)
        for j in range(1, npart):
            g = g + p_ref[j].astype(F32)
        g_ref[...] = g
        dl_ref[...], mo_ref[...], vo_ref[...] = _adamw(g, w_ref[...], m_ref[...], v_ref[...])

    blk = pl.BlockSpec((None, tr, cdim), lambda l, i: (l, i, 0))
    shape = jax.ShapeDtypeStruct((nl, r, cdim), F32)
    return pl.pallas_call(
        body, name=name, grid=(nl, r // tr),
        in_specs=[pl.BlockSpec((None, npart, tr, cdim), lambda l, i: (l, 0, i, 0)), blk, blk, blk],
        out_specs=[blk, blk, blk, blk], out_shape=[shape] * 4,
        compiler_params=_params(("parallel", "parallel")))(parts, w, m, v)


def _cols_from_blocks(g, plan, width, name):
    _, r, cdim = g.shape
    tr = _tile(r, (256, 128))
    covered = sorted((dst, dst + n) for _, _, n, dst in plan)
    holes, pos = [], 0
    for a, b in covered:
        if a > pos:
            holes.append((pos, a))
        pos = max(pos, b)
    if pos < width:
        holes.append((pos, width))

    def body(g_ref, o_ref):
        for a, b in holes:
            o_ref[:, a:b] = jnp.zeros((tr, b - a), o_ref.dtype)
        for j, src, n, dst in plan:
            o_ref[:, dst:dst + n] = g_ref[j, :, src:src + n]

    return pl.pallas_call(
        body, name=name, grid=(r // tr,), in_specs=[pl.BlockSpec((N_DEV, tr, cdim), lambda i: (0, i, 0))],
        out_specs=pl.BlockSpec((tr, width), lambda i: (i, 0)), out_shape=jax.ShapeDtypeStruct((r, width), g.dtype),
        compiler_params=_params(("parallel",)))(g)


def _blocks_from_cols(w, plan, cdim, name):
    r, width = w.shape
    tr = _tile(r, (256, 128))

    def body(w_ref, o_ref):
        for j, src, n, dst in plan:
            o_ref[j, :, src:src + n] = w_ref[:, dst:dst + n]

    return pl.pallas_call(
        body, name=name, grid=(r // tr,), in_specs=[pl.BlockSpec((tr, width), lambda i: (i, 0))],
        out_specs=pl.BlockSpec((N_DEV, tr, cdim), lambda i: (0, i, 0)),
        out_shape=jax.ShapeDtypeStruct((N_DEV, r, cdim), w.dtype), compiler_params=_params(("parallel",)))(w)


def _pair_sum(x, tmp, core, name):
    _, r, cdim = x.shape
    tr = _tile(r, (256, 128))

    def body(core_ref, x_ref, t_ref, o_ref):
        o_ref[...] = (x_ref[...] + t_ref[...]).astype(o_ref.dtype)

    grid_spec = pltpu.PrefetchScalarGridSpec(
        num_scalar_prefetch=1, grid=(N_DEV // 2, r // tr),
        in_specs=[pl.BlockSpec((None, tr, cdim), lambda ch, i, core_ref: (2 * ch + core_ref[0], i, 0)),
                  pl.BlockSpec((None, tr, cdim), lambda ch, i, core_ref: (ch, i, 0))],
        out_specs=pl.BlockSpec((None, tr, cdim), lambda ch, i, core_ref: (ch, i, 0)))
    return pl.pallas_call(
        body, name=name, grid_spec=grid_spec, out_shape=jax.ShapeDtypeStruct((N_DEV // 2, r, cdim), WIRE_DTYPE),
        compiler_params=_params(("parallel", "parallel")))(core, x, tmp)


_ANY = pl.BlockSpec(memory_space=pl.ANY)
_CHIP_FLIPS = ((1, 0), (0, 1), (1, 1))


def _coords():
    return lax.axis_index("x"), lax.axis_index("y"), lax.axis_index("c")


def _flip(v, f):
    return 1 - v if f else v


def _a2a_direct(xs, name):
    n, ncp = len(xs), N_DEV - 1

    def body(*refs):
        ins, outs = refs[:n], refs[n:2 * n]
        send, recv, loc = refs[2 * n:]
        x, y, c = _coords()
        me = 4 * x + 2 * y + c
        local = [pltpu.make_async_copy(ins[i].at[me], outs[i].at[me], loc.at[i]) for i in range(n)]
        for cp in local:
            cp.start()
        remote = []
        for i in range(n):
            for k in range(1, N_DEV):
                px, py, pc = _flip(x, k & 4), _flip(y, k & 2), _flip(c, k & 1)
                cp = pltpu.make_async_remote_copy(
                    src_ref=ins[i].at[4 * px + 2 * py + pc], dst_ref=outs[i].at[me],
                    send_sem=send.at[i * ncp + k - 1], recv_sem=recv.at[i * ncp + k - 1],
                    device_id=(px, py, pc), device_id_type=MESH)
                cp.start()
                remote.append(cp)
        for cp in remote:
            cp.wait()
        for cp in local:
            cp.wait()

    return pl.pallas_call(
        body, name=name, in_specs=[_ANY] * n, out_specs=[_ANY] * n,
        out_shape=[jax.ShapeDtypeStruct(a.shape, a.dtype) for a in xs],
        scratch_shapes=[pltpu.SemaphoreType.DMA((n * ncp,)), pltpu.SemaphoreType.DMA((n * ncp,)),
                        pltpu.SemaphoreType.DMA((n,))])(*xs)


class _AllGatherSide:
    def __init__(self, blocks):
        self.operands = list(blocks)
        n = self.n = len(self.operands)
        self.n_in = self.n_out = n
        self.out_shape = [jax.ShapeDtypeStruct((N_DEV,) + a.shape, a.dtype) for a in self.operands]
        self.aliases = {}
        nici, nd2d = len(_CHIP_FLIPS), N_DEV // 2
        self.scratch = [pltpu.SemaphoreType.DMA((n * nici,)), pltpu.SemaphoreType.DMA((n * nici,)),
                        pltpu.SemaphoreType.DMA((n * nd2d,)), pltpu.SemaphoreType.DMA((n * nd2d,)),
                        pltpu.SemaphoreType.DMA((n,))]

    def _first(self, ins, outs, sems):
        send, recv, _, _, loc = sems
        x, y, c = _coords()
        me = 4 * x + 2 * y + c
        nici = len(_CHIP_FLIPS)
        local = [pltpu.make_async_copy(ins[i], outs[i].at[me], loc.at[i]) for i in range(self.n)]
        remote = [pltpu.make_async_remote_copy(
            src_ref=ins[i], dst_ref=outs[i].at[me], send_sem=send.at[i * nici + j], recv_sem=recv.at[i * nici + j],
            device_id=(_flip(x, fx), _flip(y, fy), c), device_id_type=MESH)
            for i in range(self.n) for j, (fx, fy) in enumerate(_CHIP_FLIPS)]
        return local + remote

    def _second(self, outs, sems):
        _, _, send, recv, _ = sems
        x, y, c = _coords()
        nd2d = N_DEV // 2
        return [pltpu.make_async_remote_copy(
            src_ref=outs[i].at[2 * ch + c], dst_ref=outs[i].at[2 * ch + c], send_sem=send.at[i * nd2d + ch],
            recv_sem=recv.at[i * nd2d + ch], device_id=(x, y, 1 - c), device_id_type=MESH)
            for i in range(self.n) for ch in range(nd2d)]

    def start(self, ins, outs, sems):
        for cp in self._first(ins, outs, sems):
            cp.start()

    def finish(self, ins, outs, sems):
        for cp in self._first(ins, outs, sems):
            cp.wait()
        second = self._second(outs, sems)
        for cp in second:
            cp.start()
        for cp in second:
            cp.wait()


class _ReduceScatterIciSide:
    def __init__(self, sums, accs, layer):
        self.operands = list(sums) + list(accs)
        n = self.n = len(sums)
        self.layer = layer
        self.n_in, self.n_out = 2 * n, n
        self.out_shape = [jax.ShapeDtypeStruct(a.shape, a.dtype) for a in accs]
        self.aliases = {n + i: i for i in range(n)}
        nici = len(_CHIP_FLIPS)
        self.scratch = [pltpu.SemaphoreType.DMA((n * nici,)), pltpu.SemaphoreType.DMA((n * nici,)),
                        pltpu.SemaphoreType.DMA((n,))]

    def _copies(self, ins, outs, sems):
        send, recv, loc = sems
        x, y, c = _coords()
        chip = 2 * x + y
        nici = len(_CHIP_FLIPS)
        local = [pltpu.make_async_copy(ins[i].at[chip], outs[i].at[self.layer, chip], loc.at[i])
                 for i in range(self.n)]
        remote = [pltpu.make_async_remote_copy(
            src_ref=ins[i].at[2 * _flip(x, fx) + _flip(y, fy)], dst_ref=outs[i].at[self.layer, chip],
            send_sem=send.at[i * nici + j], recv_sem=recv.at[i * nici + j],
            device_id=(_flip(x, fx), _flip(y, fy), c), device_id_type=MESH)
            for i in range(self.n) for j, (fx, fy) in enumerate(_CHIP_FLIPS)]
        return local + remote

    def start(self, ins, outs, sems):
        for cp in self._copies(ins, outs, sems):
            cp.start()

    def finish(self, ins, outs, sems):
        for cp in self._copies(ins, outs, sems):
            cp.wait()


def _run_side(side, name):
    def body(*refs):
        ins, outs = refs[:side.n_in], refs[side.n_in:side.n_in + side.n_out]
        sems = refs[side.n_in + side.n_out:]
        side.start(ins, outs, sems)
        side.finish(ins, outs, sems)

    return pl.pallas_call(
        body, name=name, in_specs=[_ANY] * side.n_in, out_specs=[_ANY] * side.n_out, out_shape=side.out_shape,
        input_output_aliases=side.aliases, scratch_shapes=side.scratch)(*side.operands)


class _ReduceScatterD2dSide:
    def __init__(self, parts):
        self.operands = list(parts)
        n = self.n = len(self.operands)
        self.n_in = self.n_out = n
        nd2d = N_DEV // 2
        self.out_shape = [jax.ShapeDtypeStruct((nd2d,) + a.shape[1:], a.dtype) for a in self.operands]
        self.aliases = {}
        self.scratch = [pltpu.SemaphoreType.DMA((n * nd2d,)), pltpu.SemaphoreType.DMA((n * nd2d,))]

    def _copies(self, ins, outs, sems):
        send, recv = sems
        x, y, c = _coords()
        nd2d = N_DEV // 2
        return [pltpu.make_async_remote_copy(
            src_ref=ins[i].at[2 * ch + 1 - c], dst_ref=outs[i].at[ch], send_sem=send.at[i * nd2d + ch],
            recv_sem=recv.at[i * nd2d + ch], device_id=(x, y, 1 - c), device_id_type=MESH)
            for i in range(self.n) for ch in range(nd2d)]

    def start(self, ins, outs, sems):
        for cp in self._copies(ins, outs, sems):
            cp.start()

    def finish(self, ins, outs, sems):
        for cp in self._copies(ins, outs, sems):
            cp.wait()


_PACK_ROWS = 256


def _pack(arrs):
    flat = jnp.concatenate([a.reshape(-1) for a in arrs])
    quantum = _PACK_ROWS * LANES
    total = -(-flat.shape[0] // quantum) * quantum
    return jnp.pad(flat, (0, total - flat.shape[0])).reshape(-1, LANES)


def _unpack(packed, like):
    flat, out, pos = packed.reshape(-1), [], 0
    for a in like:
        out.append(flat[pos:pos + a.size].reshape(a.shape))
        pos += a.size
    return out


def kernel(x, c, ada_w, ada_b, norm1_g, w_in, conv_w, spatial_w, spatial_b, v_norm_g, a_log, dt_bias, o_norm_g, w_branch_a, w_branch_b, w_out, norm2_g, w_ffn_in, w_ffn_out, final_g, loss_target, m_ada_w, m_ada_b, m_norm1_g, m_w_in, m_conv_w, m_spatial_w, m_spatial_b, m_v_norm_g, m_a_log, m_dt_bias, m_o_norm_g, m_w_branch_a, m_w_branch_b, m_w_out, m_norm2_g, m_w_ffn_in, m_w_ffn_out, m_final_g, v_ada_w, v_ada_b, v_norm1_g, v_w_in, v_conv_w, v_spatial_w, v_spatial_b, v_v_norm_g, v_a_log, v_dt_bias, v_o_norm_g, v_w_branch_a, v_w_branch_b, v_w_out, v_norm2_g, v_w_ffn_in, v_w_ffn_out, v_final_g):
    nl, d = ada_w.shape[0], x.shape[2]
    t = x.shape[1]
    nchunk = t // GDN_CHUNK
    xi, yi, ci = _coords()
    me = 4 * xi + 2 * yi + ci
    core = jnp.reshape(ci, (1,)).astype(jnp.int32)
    x0, target = x[0], loss_target[0]
    wcols = 3 * HEADS * HEAD_DIM
    lay = _ProjLayout(d)
    in_pieces = lay.pieces(w_in.shape[2])
    fi_shard = w_ffn_in.shape[2]
    fi_pieces = [(j, 0, fi_shard, fi_shard * j) for j in range(N_DEV)]

    c_all, cw_all = _a2a_direct([jnp.broadcast_to(c[None], (N_DEV,) + c.shape),
                                 jnp.broadcast_to(conv_w[None], (N_DEV,) + conv_w.shape)], "gather_small")
    c_all = c_all[:, 0]
    conv_full = cw_all.transpose(1, 2, 0, 3).reshape(nl, CONV_K, wcols)
    modp = _ada_fwd(c_all, ada_w, "ada_fwd")
    (modx,) = _a2a_direct([modp.transpose(1, 0, 2)], "mod_exchange")
    mod = (modx.transpose(1, 0, 2).reshape(nl, 6 * d) + ada_b).reshape(nl, 6, 1, d)

    big = (w_in, w_branch_a, w_branch_b, w_out, w_ffn_in, w_ffn_out)
    big_wire = [w.astype(WIRE_DTYPE) for w in big]
    gather_in = lambda i: _AllGatherSide([big_wire[0][i]])
    gather_early = lambda i: _AllGatherSide([big_wire[k][i] for k in (1, 2, 3, 5)])
    gather_late = lambda i: _AllGatherSide([big_wire[4][i]] + ([big_wire[0][i + 1]] if i + 1 < nl else []))
    row_full = lambda g: g.reshape(-1, g.shape[2])
    padded_in = lambda g: _cols_from_blocks(g, in_pieces, lay.width, "w_in_cols")
    w_pads = [padded_in(_run_side(gather_in(0), "ag_first")[0])] + [None] * (nl - 1)
    weights = [None] * nl

    def rows_of(proj, lo):
        return proj[:, lo:lo + HEADS].T.reshape(HEADS, nchunk, 1, GDN_CHUNK)

    saved = []
    x_cur, delta, gt_prev = x0, None, None
    for i in range(nl):
        sh1, sc1, gt1, sh2, sc2, gt2 = (mod[i, k] for k in range(6))
        s = dict(gt1=gt1, gt2=gt2, sc1=sc1, sc2=sc2)
        s["x_in"], s["h"] = _resid_norm(x_cur, delta, gt_prev, norm1_g[i][None], sc1, sh1, "norm1_fwd")
        s["proj"], g_a, g_b, g_o, g_fo = _matmul(s["h"], w_pads[i], "nn", "proj_fwd", side=gather_early(i))
        s["b_col"] = spatial_b[i][:, :, None]
        s["ya"] = _mixer_a_fwd(s["proj"], lay.uv, spatial_w[i], s["b_col"], v_norm_g[i][None], "mixer_a_fwd")
        s["qkv_h"] = _conv_fwd(s["proj"], conv_full[i], "conv_fwd")
        s["braw"], s["araw"] = rows_of(s["proj"], lay.ba), rows_of(s["proj"], lay.ba + HEADS)
        s["o"], s["states"], s["t_mats"], g_fi, *g_in = _gdn_fwd(
            s["qkv_h"], s["araw"], s["braw"], a_log[i], dt_bias[i], "gdn_fwd", gather_late(i))
        if g_in:
            w_pads[i + 1] = padded_in(g_in[0])
        weights[i] = (row_full(g_a), row_full(g_b), row_full(g_o),
                      _cols_from_blocks(g_fi, fi_pieces, N_DEV * fi_shard, "w_ffn_in_cols"), row_full(g_fo))
        w_a, w_b, w_o, w_fi, w_fo = weights[i]
        s["yb"] = _gdn_post_fwd(s["o"], s["proj"], lay.z, o_norm_g[i][None], "gdn_post_fwd")
        s["pa"] = _matmul(s["ya"], w_a, "nn", "branch_a_fwd")
        s["pb"] = _matmul(s["yb"], w_b, "nn", "branch_b_fwd")
        s["merged"] = _merge_fwd(s["pa"], s["pb"], s["proj"], lay.gates, "merge_fwd")
        s["mo"] = _matmul(s["merged"], w_o, "nn", "out_fwd")
        s["x1"], s["h2"] = _resid_norm(s["x_in"], s["mo"], gt1, norm2_g[i][None], sc2, sh2, "norm2_fwd")
        s["gu"] = _matmul(s["h2"], w_fi, "nn", "ffn_in_fwd")
        s["a"] = _swiglu_fwd(s["gu"], "swiglu_fwd")
        s["fo"] = _matmul(s["a"], w_fo, "nn", "ffn_out_fwd")
        saved.append(s)
        x_cur, delta, gt_prev = s["x1"], s["fo"], gt2
    dx, d_final_g, loss_tile = _final_loss(x_cur, delta, gt_prev, final_g[None], target, "final_loss")
    loss = lax.psum(loss_tile[0, 0], ("x", "y", "c"))

    big_shapes = [(d, w_in.shape[2]), w_branch_a.shape[1:], w_branch_b.shape[1:], w_out.shape[1:],
                  (d, w_ffn_in.shape[2]), w_ffn_out.shape[1:]]
    accs = [lax.empty((nl, N_DEV // 2) + tuple(sh), WIRE_DTYPE) for sh in big_shapes]
    row_blocks = lambda g: g.reshape(N_DEV, -1, g.shape[1])
    dmod, small = [None] * nl, [None] * nl
    d_conv = [None] * nl
    parts, sums = None, None
    beside_gdn, beside_dw, beside_dx = (0,), (4,), (1, 2, 3, 5)

    def scatter_side(idx, layer):
        if sums is None:
            return _NoSide
        return _ReduceScatterIciSide([sums[k] for k in idx], [accs[k] for k in idx], layer)

    def scattered_into(accs, idx, new):
        accs = list(accs)
        for k, a in zip(idx, new):
            accs[k] = a
        return accs

    for i in reversed(range(nl)):
        s = saved[i]
        w_a, w_b, w_o, w_fi, w_fo = weights[i]
        dfo, dgt2 = _gate_bwd(dx, s["fo"], s["gt2"], "gate2_bwd")
        g_fo = _matmul(s["a"], dfo, "tn", "ffn_out_dw")
        da = _matmul(dfo, w_fo, "nt", "ffn_out_dx")
        dgu = _swiglu_bwd(s["gu"], da, "swiglu_bwd")
        if parts is None:
            g_fi = _matmul(s["h2"], dgu, "tn", "ffn_in_dw")
        else:
            g_fi, *other = _matmul(s["h2"], dgu, "tn", "ffn_in_dw", side=_ReduceScatterD2dSide(parts))
            sums = [_pair_sum(p, o, core, "rs_pair_sum_%d" % k) for k, (p, o) in enumerate(zip(parts, other))]
        dh2 = _matmul(dgu, w_fi, "nt", "ffn_in_dx")
        dx1, dsh2, dsc2, dg2 = _norm_bwd(s["x1"], dh2, dx, norm2_g[i][None], s["sc2"], "norm2_bwd")
        dmo, dgt1 = _gate_bwd(dx1, s["mo"], s["gt1"], "gate1_bwd")
        g_o = _matmul(s["merged"], dmo, "tn", "out_dw")
        dmerged = _matmul(dmo, w_o, "nt", "out_dx")
        dproj = lax.empty((t, lay.width), MXU_DTYPE)
        dpa, dpb, dproj = _merge_bwd(dmerged, s["pa"], s["pb"], s["proj"], lay.gates, dproj, "merge_bwd")
        g_a = _matmul(s["ya"], dpa, "tn", "branch_a_dw")
        dya = _matmul(dpa, w_a, "nt", "branch_a_dx")
        g_b = _matmul(s["yb"], dpb, "tn", "branch_b_dw")
        dyb = _matmul(dpb, w_b, "nt", "branch_b_dx")
        dproj, d_ws, d_bs, d_gv = _mixer_a_bwd(s["proj"], lay.uv, dya, spatial_w[i], jnp.swapaxes(spatial_w[i], 1, 2),
                                               s["b_col"], v_norm_g[i][None], dproj, "mixer_a_bwd")
        do, dproj, d_go = _gdn_post_bwd(s["o"], s["proj"], lay.z, dyb, o_norm_g[i][None], dproj, "gdn_post_bwd")
        dq, dk, dv, d_ar, d_br, d_al, d_dt, *scattered = _gdn_bwd(
            s["qkv_h"], s["araw"], s["braw"], a_log[i], dt_bias[i], s["states"], s["t_mats"], do, "gdn_bwd",
            scatter_side(beside_gdn, i + 1))
        accs = scattered_into(accs, beside_gdn, scattered)
        dacc, d_conv[i] = _conv_bwd_pre(s["proj"], dq, dk, dv, conv_full[i], "conv_bwd_pre")
        dproj = _conv_bwd_in(dacc, conv_full[i], dproj, "conv_bwd_in")
        cols = lambda r: r.reshape(HEADS, t).T
        dba = jnp.pad(jnp.concatenate([cols(d_br), cols(d_ar)], axis=1), ((0, 0), (0, LANES - 2 * HEADS)))
        dproj = _write_cols(dproj, dba, lay.ba, "dproj_ba")
        if sums is None:
            g_pad = _matmul(s["h"], dproj, "tn", "proj_dw")
            dh = _matmul(dproj, w_pads[i], "nt", "proj_dx")
        else:
            g_pad, *scattered = _matmul(s["h"], dproj, "tn", "proj_dw", side=scatter_side(beside_dw, i + 1))
            accs = scattered_into(accs, beside_dw, scattered)
            dh, *scattered = _matmul(dproj, w_pads[i], "nt", "proj_dx", side=scatter_side(beside_dx, i + 1))
            accs = scattered_into(accs, beside_dx, scattered)
        dx, dsh1, dsc1, dg1 = _norm_bwd(s["x_in"], dh, dx1, norm1_g[i][None], s["sc1"], "norm1_bwd")
        dmod[i] = jnp.concatenate([dsh1, dsc1, dgt1, dsh2, dsc2, dgt2], axis=1)[0]
        small[i] = (dg1[0], d_ws, d_bs[:, :, 0], d_gv[0], d_al[:, 0, 0], d_dt[:, 0, 0], d_go[0], dg2[0])
        parts = [_blocks_from_cols(g_pad, in_pieces, w_in.shape[2], "w_in_blocks"), row_blocks(g_a), row_blocks(g_b),
                 row_blocks(g_o), _blocks_from_cols(g_fi, fi_pieces, fi_shard, "w_ffn_in_blocks"), row_blocks(g_fo)]
    other = _run_side(_ReduceScatterD2dSide(parts), "rs_d2d_last")
    sums = [_pair_sum(p, o, core, "rs_pair_sum_%d" % k) for k, (p, o) in enumerate(zip(parts, other))]
    accs = _run_side(_ReduceScatterIciSide(sums, accs, 0), "rs_ici_last")

    dmod = jnp.stack(dmod)
    sm = [jnp.stack([small[i][k] for i in range(nl)]) for k in range(8)]
    rep_w = (ada_b, norm1_g, spatial_w, spatial_b, v_norm_g, a_log, dt_bias, o_norm_g, norm2_g, final_g)
    rep_m = (m_ada_b, m_norm1_g, m_spatial_w, m_spatial_b, m_v_norm_g, m_a_log, m_dt_bias, m_o_norm_g, m_norm2_g, m_final_g)
    rep_v = (v_ada_b, v_norm1_g, v_spatial_w, v_spatial_b, v_v_norm_g, v_a_log, v_dt_bias, v_o_norm_g, v_norm2_g, v_final_g)
    rep_g = (dmod, sm[0], sm[1], sm[2], sm[3], sm[4], sm[5], sm[6], sm[7], d_final_g[0])
    packed = _pack(rep_g)
    d_conv_blocks = jnp.stack(d_conv).reshape(nl, CONV_K, N_DEV, -1).transpose(2, 0, 1, 3).reshape(N_DEV, -1, LANES)
    dmod_blocks = dmod.reshape(nl, N_DEV, -1).transpose(1, 0, 2)
    rep_all, conv_all, dmod_all = _a2a_direct(
        [jnp.broadcast_to(packed[None], (N_DEV,) + packed.shape), d_conv_blocks, dmod_blocks], "small_grads")
    rep_out = _sum_adam(rep_all[None], _pack(rep_w)[None], _pack(rep_m)[None], _pack(rep_v)[None], "adam_small")
    rep_out = [_unpack(o[0], rep_w) for o in rep_out]
    conv_out = _sum_adam(conv_all[None], conv_w.reshape(1, -1, LANES), m_conv_w.reshape(1, -1, LANES),
                         v_conv_w.reshape(1, -1, LANES), "adam_conv")
    conv_out = [o.reshape(conv_w.shape) for o in conv_out]
    ada_out = _ada_bwd(c_all[:, :, None], dmod_all.transpose(1, 0, 2), ada_w, m_ada_w, v_ada_w, "ada_bwd_adam")
    big_m = (m_w_in, m_w_branch_a, m_w_branch_b, m_w_out, m_w_ffn_in, m_w_ffn_out)
    big_v = (v_w_in, v_w_branch_a, v_w_branch_b, v_w_out, v_w_ffn_in, v_w_ffn_out)
    big_out = [_sum_adam(accs[k], big[k], big_m[k], big_v[k], "adam_big_%d" % k) for k in range(6)]

    def ordered(kind):
        rep = rep_out[kind]
        return (ada_out[kind], rep[0], rep[1], big_out[0][kind], conv_out[kind], rep[2], rep[3], rep[4], rep[5],
                rep[6], rep[7], big_out[1][kind], big_out[2][kind], big_out[3][kind], rep[8], big_out[4][kind],
                big_out[5][kind], rep[9])

    return (loss, dx[None]) + ordered(0) + ordered(1) + ordered(2) + ordered(3)
```

```python
import functools

import jax
import jax.numpy as jnp
from jax import lax
from jax.experimental import pallas as pl
from jax.experimental.pallas import tpu as pltpu

F32 = jnp.float32
BF16 = jnp.bfloat16
MXU_DTYPE = BF16
WIRE_DTYPE = BF16
EPS = 1e-6
LANES = 128
SUBLANES = 8
GDN_CHUNK = 128
A_CHUNK = 128
GROUPS = 8
HEADS = 8
HEAD_DIM = 128
CONV_K = 4
N_DEV = 8
VMEM_LIMIT = 48 * 1024 * 1024
MESH = pl.DeviceIdType.MESH

ADAM_LR = 0.001
ADAM_B1 = 0.9
ADAM_B2 = 0.999
ADAM_EPS = 1e-08
ADAM_WD = 0.01
ADAM_STEP = 10

_NN = (((1,), (0,)), ((), ()))
_NT = (((1,), (1,)), ((), ()))
_TN = (((0,), (0,)), ((), ()))


def _mm(a, b, dims=_NN):
    return lax.dot_general(a.astype(MXU_DTYPE), b.astype(MXU_DTYPE), dims, preferred_element_type=F32)


def _mm_hi(a, b):
    return lax.dot_general(a, b, _NN, precision=lax.Precision.HIGHEST, preferred_element_type=F32)


def _tile(n, cands):
    for c in cands:
        if n % c == 0:
            return c
    return n


def _params(sem=None):
    return pltpu.CompilerParams(dimension_semantics=sem, vmem_limit_bytes=VMEM_LIMIT)


def _sigmoid(x):
    return 1.0 / (1.0 + jnp.exp(-x))


def _silu(x):
    return x * _sigmoid(x)


def _dsilu(x):
    s = _sigmoid(x)
    return s * (1.0 + x * (1.0 - s))


_GELU_C = 0.7978845608028654
_GELU_A = 0.044715


def _gelu(x):
    return 0.5 * x * (1.0 + jnp.tanh(_GELU_C * (x + _GELU_A * x * x * x)))


def _dgelu(x):
    t = jnp.tanh(_GELU_C * (x + _GELU_A * x * x * x))
    return 0.5 * (1.0 + t) + 0.5 * x * (1.0 - t * t) * _GELU_C * (1.0 + 3.0 * _GELU_A * x * x)


def _softplus(x):
    return jnp.maximum(x, 0.0) + jnp.log(1.0 + jnp.exp(-jnp.abs(x)))


_MM_TILES = (1024, 1408, 1664, 512, 256, 128)


class _NoSide:
    operands, out_shape, scratch, aliases, n_in, n_out = [], [], [], {}, 0, 0


def _side_hooks(side, refs, n_main_in, n_main_out, n_main_scratch, grid):
    a = n_main_in + side.n_in
    b = a + n_main_out + side.n_out
    ins, outs, sems = refs[n_main_in:a], refs[a + n_main_out:b], refs[b + n_main_scratch:]
    main = refs[:n_main_in] + refs[a:a + n_main_out] + refs[b:b + n_main_scratch]
    ids = [pl.program_id(k) for k in range(len(grid))]

    def start():
        if side.n_in:
            pl.when(functools.reduce(jnp.logical_and, [i == 0 for i in ids]))(lambda: side.start(ins, outs, sems))

    def finish():
        if side.n_in:
            last = functools.reduce(jnp.logical_and, [i == g - 1 for i, g in zip(ids, grid)])
            pl.when(last)(lambda: side.finish(ins, outs, sems))

    return main, start, finish


def _carrier_call(body, name, grid, in_specs, out_specs, out_shape, scratch, side, args):
    aliases = {len(in_specs) + k: len(out_specs) + v for k, v in side.aliases.items()}
    return pl.pallas_call(
        body, name=name, grid=grid, in_specs=list(in_specs) + [_ANY] * side.n_in,
        out_specs=list(out_specs) + [_ANY] * side.n_out, out_shape=list(out_shape) + list(side.out_shape),
        scratch_shapes=list(scratch) + list(side.scratch), input_output_aliases=aliases,
        compiler_params=_params(("arbitrary",) * len(grid)))(*args, *side.operands)


def _matmul(a, b, mode, name, out_dtype=F32, side=_NoSide):
    if mode == "nn":
        (m, k), n = a.shape, b.shape[1]
    elif mode == "nt":
        (m, k), n = a.shape, b.shape[0]
    else:
        (k, m), n = a.shape, b.shape[1]
    tm, tn, tk = _tile(m, _MM_TILES), _tile(n, _MM_TILES), _tile(k, _MM_TILES)
    nk = k // tk
    grid = (m // tm, n // tn, nk)
    dims = {"nn": _NN, "nt": _NT, "tn": _TN}[mode]

    def body(*refs):
        (a_ref, b_ref, o_ref, acc_ref), side_start, side_finish = _side_hooks(side, refs, 2, 1, 1, grid)
        kk = pl.program_id(2)
        side_start()
        if nk == 1:
            o_ref[...] = _mm(a_ref[...], b_ref[...], dims).astype(o_ref.dtype)
        else:
            @pl.when(kk == 0)
            def _():
                acc_ref[...] = _mm(a_ref[...], b_ref[...], dims)

            @pl.when(jnp.logical_and(kk > 0, kk < nk - 1))
            def _():
                acc_ref[...] += _mm(a_ref[...], b_ref[...], dims)

            @pl.when(kk == nk - 1)
            def _():
                o_ref[...] = (acc_ref[...] + _mm(a_ref[...], b_ref[...], dims)).astype(o_ref.dtype)

        side_finish()

    a_spec = (pl.BlockSpec((tk, tm), lambda i, j, l: (l, i)) if mode == "tn"
              else pl.BlockSpec((tm, tk), lambda i, j, l: (i, l)))
    b_spec = (pl.BlockSpec((tn, tk), lambda i, j, l: (j, l)) if mode == "nt"
              else pl.BlockSpec((tk, tn), lambda i, j, l: (l, j)))
    o_spec = pl.BlockSpec((tm, tn), lambda i, j, l: (i, j))
    out = _carrier_call(body, name, grid, [a_spec, b_spec], [o_spec], [jax.ShapeDtypeStruct((m, n), out_dtype)],
                        [pltpu.VMEM((tm, tn) if nk > 1 else (SUBLANES, LANES), F32)], side, (a, b))
    return out if side.n_in else out[0]


_ROW_TILES = (512, 256, 128)


def _resid_norm(x, delta, gt, g, sc, sh, name):
    t, d = x.shape
    tt = _tile(t, _ROW_TILES)
    has = delta is not None

    def body(*refs):
        if has:
            x_ref, d_ref, gt_ref, g_ref, sc_ref, sh_ref, xo_ref, h_ref = refs
            xv = x_ref[...] + gt_ref[...] * d_ref[...]
            xo_ref[...] = xv
        else:
            x_ref, g_ref, sc_ref, sh_ref, h_ref = refs
            xv = x_ref[...]
        r = lax.rsqrt(jnp.mean(xv * xv, axis=-1, keepdims=True) + EPS)
        y = xv * r * g_ref[...]
        h_ref[...] = (y * (1.0 + sc_ref[...]) + sh_ref[...]).astype(h_ref.dtype)

    row = pl.BlockSpec((tt, d), lambda i: (i, 0))
    vec = pl.BlockSpec((1, d), lambda i: (0, 0))
    if has:
        return pl.pallas_call(
            body, name=name, grid=(t // tt,), in_specs=[row, row, vec, vec, vec, vec], out_specs=[row, row],
            out_shape=[jax.ShapeDtypeStruct((t, d), F32), jax.ShapeDtypeStruct((t, d), MXU_DTYPE)],
            compiler_params=_params(("parallel",)))(x, delta, gt, g, sc, sh)
    h = pl.pallas_call(
        body, name=name + "_first", grid=(t // tt,), in_specs=[row, vec, vec, vec], out_specs=row,
        out_shape=jax.ShapeDtypeStruct((t, d), MXU_DTYPE), compiler_params=_params(("parallel",)))(x, g, sc, sh)
    return x, h


def _final_loss(x, delta, gt, g, target, name):
    t, d = x.shape
    tt = _tile(t, _ROW_TILES)

    def body(x_ref, d_ref, gt_ref, g_ref, tg_ref, dx_ref, dg_ref, loss_ref):
        @pl.when(pl.program_id(0) == 0)
        def _():
            dg_ref[...] = jnp.zeros_like(dg_ref)
            loss_ref[...] = jnp.zeros_like(loss_ref)

        xv = x_ref[...] + gt_ref[...] * d_ref[...]
        r = lax.rsqrt(jnp.mean(xv * xv, axis=-1, keepdims=True) + EPS)
        xh = xv * r
        diff = xh * g_ref[...] - tg_ref[...]
        loss_ref[...] += jnp.sum(diff * diff) * (0.5 / d)
        dy = diff * (1.0 / d)
        dg_ref[...] += jnp.sum(dy * xh, axis=0, keepdims=True)
        dxh = dy * g_ref[...]
        dx_ref[...] = r * (dxh - xh * jnp.mean(dxh * xh, axis=-1, keepdims=True))

    row = pl.BlockSpec((tt, d), lambda i: (i, 0))
    vec = pl.BlockSpec((1, d), lambda i: (0, 0))
    tile = pl.BlockSpec((SUBLANES, LANES), lambda i: (0, 0))
    return pl.pallas_call(
        body, name=name, grid=(t // tt,), in_specs=[row, row, vec, vec, row], out_specs=[row, vec, tile],
        out_shape=[jax.ShapeDtypeStruct((t, d), F32), jax.ShapeDtypeStruct((1, d), F32),
                   jax.ShapeDtypeStruct((SUBLANES, LANES), F32)],
        compiler_params=_params(("arbitrary",)))(x, delta, gt, g, target)


def _norm_bwd(x, dh, dres, g, sc, name):
    t, d = x.shape
    tt = _tile(t, _ROW_TILES)

    def body(x_ref, dh_ref, dr_ref, g_ref, sc_ref, dx_ref, dsh_ref, dsc_ref, dg_ref):
        @pl.when(pl.program_id(0) == 0)
        def _():
            dsh_ref[...] = jnp.zeros_like(dsh_ref)
            dsc_ref[...] = jnp.zeros_like(dsc_ref)
            dg_ref[...] = jnp.zeros_like(dg_ref)

        xv, dh = x_ref[...], dh_ref[...]
        r = lax.rsqrt(jnp.mean(xv * xv, axis=-1, keepdims=True) + EPS)
        xh = xv * r
        gv, sc1 = g_ref[...], 1.0 + sc_ref[...]
        dsh_ref[...] += jnp.sum(dh, axis=0, keepdims=True)
        dsc_ref[...] += jnp.sum(dh * xh, axis=0, keepdims=True) * gv
        dg_ref[...] += jnp.sum(dh * xh, axis=0, keepdims=True) * sc1
        dxh = dh * (gv * sc1)
        dx_ref[...] = dr_ref[...] + r * (dxh - xh * jnp.mean(dxh * xh, axis=-1, keepdims=True))

    row = pl.BlockSpec((tt, d), lambda i: (i, 0))
    vec = pl.BlockSpec((1, d), lambda i: (0, 0))
    vshape = jax.ShapeDtypeStruct((1, d), F32)
    return pl.pallas_call(
        body, name=name, grid=(t // tt,), in_specs=[row, row, row, vec, vec], out_specs=[row, vec, vec, vec],
        out_shape=[jax.ShapeDtypeStruct((t, d), F32), vshape, vshape, vshape],
        compiler_params=_params(("arbitrary",)))(x, dh, dres, g, sc)


def _gate_bwd(dxo, branch, gt, name):
    t, d = dxo.shape
    tt = _tile(t, _ROW_TILES)

    def body(dx_ref, br_ref, gt_ref, db_ref, dgt_ref):
        @pl.when(pl.program_id(0) == 0)
        def _():
            dgt_ref[...] = jnp.zeros_like(dgt_ref)

        dx = dx_ref[...]
        db_ref[...] = (dx * gt_ref[...]).astype(db_ref.dtype)
        dgt_ref[...] += jnp.sum(dx * br_ref[...], axis=0, keepdims=True)

    row = pl.BlockSpec((tt, d), lambda i: (i, 0))
    vec = pl.BlockSpec((1, d), lambda i: (0, 0))
    return pl.pallas_call(
        body, name=name, grid=(t // tt,), in_specs=[row, row, vec], out_specs=[row, vec],
        out_shape=[jax.ShapeDtypeStruct((t, d), MXU_DTYPE), jax.ShapeDtypeStruct((1, d), F32)],
        compiler_params=_params(("arbitrary",)))(dxo, branch, gt)


def _swiglu_fwd(gu, name):
    t, f2 = gu.shape
    f = f2 // 2
    tt = _tile(t, (256, 128))

    def body(g_ref, u_ref, o_ref):
        o_ref[...] = (_silu(g_ref[...]) * u_ref[...]).astype(o_ref.dtype)

    return pl.pallas_call(
        body, name=name, grid=(t // tt,),
        in_specs=[pl.BlockSpec((tt, f), lambda i: (i, 0)), pl.BlockSpec((tt, f), lambda i: (i, 1))],
        out_specs=pl.BlockSpec((tt, f), lambda i: (i, 0)), out_shape=jax.ShapeDtypeStruct((t, f), MXU_DTYPE),
        compiler_params=_params(("parallel",)))(gu, gu)


def _swiglu_bwd(gu, da, name):
    t, f2 = gu.shape
    f = f2 // 2
    tt = _tile(t, (256, 128))

    def body(g_ref, u_ref, da_ref, o_ref):
        gate, da = g_ref[...], da_ref[...]
        o_ref[:, :f] = (da * u_ref[...] * _dsilu(gate)).astype(o_ref.dtype)
        o_ref[:, f:] = (da * _silu(gate)).astype(o_ref.dtype)

    return pl.pallas_call(
        body, name=name, grid=(t // tt,),
        in_specs=[pl.BlockSpec((tt, f), lambda i: (i, 0)), pl.BlockSpec((tt, f), lambda i: (i, 1)),
                  pl.BlockSpec((tt, f), lambda i: (i, 0))],
        out_specs=pl.BlockSpec((tt, f2), lambda i: (i, 0)), out_shape=jax.ShapeDtypeStruct((t, f2), MXU_DTYPE),
        compiler_params=_params(("parallel",)))(gu, gu, da)


class _ProjLayout:
    def __init__(self, d):
        wc = 3 * HEADS * HEAD_DIM
        self.d, self.wc = d, wc
        self.qkv, self.z, self.uv, self.gates, self.ba = 0, wc, wc + d, wc + 3 * d, wc + 5 * d
        self.width = self.ba + LANES
        assert self.z % d == 0 and self.uv % (2 * d) == 0 and self.gates % (2 * d) == 0 and self.ba % LANES == 0

    def pieces(self, shard):
        d, wc, out, lo = self.d, self.wc, [], 0
        for length, dst in ((2 * d, self.uv), (wc, self.qkv), (d, self.z), (2 * HEADS, self.ba), (2 * d, self.gates)):
            pos = lo
            while pos < lo + length:
                j = pos // shard
                n = min(lo + length, (j + 1) * shard) - pos
                out.append((j, pos - j * shard, n, dst + pos - lo))
                pos += n
            lo += length
        return out


def _merge_fwd(pa, pb, proj, gcol, name):
    t, d = pa.shape
    tt = _tile(t, _ROW_TILES)

    def body(pa_ref, pb_ref, ga_ref, gb_ref, o_ref):
        o_ref[...] = (_sigmoid(ga_ref[...]) * pa_ref[...] + _sigmoid(gb_ref[...]) * pb_ref[...]).astype(o_ref.dtype)

    row = pl.BlockSpec((tt, d), lambda i: (i, 0))
    gate = lambda k: pl.BlockSpec((tt, d), lambda i: (i, gcol // d + k))
    return pl.pallas_call(
        body, name=name, grid=(t // tt,), in_specs=[row, row, gate(0), gate(1)], out_specs=row,
        out_shape=jax.ShapeDtypeStruct((t, d), MXU_DTYPE), compiler_params=_params(("parallel",)))(pa, pb, proj, proj)


def _merge_bwd(dm, pa, pb, proj, gcol, dproj, name):
    t, d = pa.shape
    tt = _tile(t, _ROW_TILES)

    def body(dm_ref, pa_ref, pb_ref, ga_ref, gb_ref, _, dpa_ref, dpb_ref, dg_ref):
        dm = dm_ref[...]
        sa, sb = _sigmoid(ga_ref[...]), _sigmoid(gb_ref[...])
        dpa_ref[...] = (dm * sa).astype(dpa_ref.dtype)
        dpb_ref[...] = (dm * sb).astype(dpb_ref.dtype)
        dg_ref[:, :d] = (dm * pa_ref[...] * sa * (1.0 - sa)).astype(dg_ref.dtype)
        dg_ref[:, d:] = (dm * pb_ref[...] * sb * (1.0 - sb)).astype(dg_ref.dtype)

    row = pl.BlockSpec((tt, d), lambda i: (i, 0))
    gate = lambda k: pl.BlockSpec((tt, d), lambda i: (i, gcol // d + k))
    wide = pl.BlockSpec((tt, 2 * d), lambda i: (i, gcol // (2 * d)))
    return pl.pallas_call(
        body, name=name, grid=(t // tt,), in_specs=[row, row, row, gate(0), gate(1), _ANY], out_specs=[row, row, wide],
        out_shape=[jax.ShapeDtypeStruct((t, d), MXU_DTYPE), jax.ShapeDtypeStruct((t, d), MXU_DTYPE),
                   jax.ShapeDtypeStruct(dproj.shape, dproj.dtype)],
        input_output_aliases={5: 2}, compiler_params=_params(("parallel",)))(dm, pa, pb, proj, proj, dproj)


def _tri_masks(n):
    ri = lax.broadcasted_iota(jnp.int32, (n, n), 0)
    ci = lax.broadcasted_iota(jnp.int32, (n, n), 1)
    return ri >= ci, ri > ci, ri == ci


def _mixer_a_fwd(proj, ucol, w_s, b_col, g_v, name):
    t, w = proj.shape[0], g_v.shape[1]
    c = A_CHUNK

    def body(u_ref, v_ref, w_ref, b_ref, gv_ref, y_ref):
        tril, _, _ = _tri_masks(c)
        ug, vg = _gelu(u_ref[...]), _gelu(v_ref[...])
        for g in range(GROUPS):
            sl = slice(g * c, (g + 1) * c)
            vt = vg[:, sl]
            r = lax.rsqrt(jnp.mean(vt * vt, axis=-1, keepdims=True) + EPS)
            vn = vt * r * gv_ref[:, sl]
            s = _mm(jnp.where(tril, w_ref[g], 0.0), vn) + b_ref[g]
            y_ref[:, sl] = (ug[:, sl] * s).astype(y_ref.dtype)

    return pl.pallas_call(
        body, name=name, grid=(t // c,),
        in_specs=[pl.BlockSpec((c, w), lambda i: (i, ucol // w)), pl.BlockSpec((c, w), lambda i: (i, ucol // w + 1)),
                  pl.BlockSpec((GROUPS, c, c), lambda i: (0, 0, 0)), pl.BlockSpec((GROUPS, c, 1), lambda i: (0, 0, 0)),
                  pl.BlockSpec((1, w), lambda i: (0, 0))],
        out_specs=pl.BlockSpec((c, w), lambda i: (i, 0)), out_shape=jax.ShapeDtypeStruct((t, w), MXU_DTYPE),
        compiler_params=_params(("parallel",)))(proj, proj, w_s, b_col, g_v)


def _mixer_a_bwd(proj, ucol, dy, w_s, w_st, b_col, g_v, dproj, name):
    t, w = proj.shape[0], g_v.shape[1]
    w2 = 2 * w
    c = A_CHUNK

    def body(u_ref, v_ref, dy_ref, w_ref, wt_ref, b_ref, gv_ref, _, duv_ref, dw_ref, db_ref, dgv_ref):
        @pl.when(pl.program_id(0) == 0)
        def _():
            dw_ref[...] = jnp.zeros_like(dw_ref)
            db_ref[...] = jnp.zeros_like(db_ref)
            dgv_ref[...] = jnp.zeros_like(dgv_ref)

        tril, _, _ = _tri_masks(c)
        triu = lax.broadcasted_iota(jnp.int32, (c, c), 0) <= lax.broadcasted_iota(jnp.int32, (c, c), 1)
        up, vp = u_ref[...], v_ref[...]
        ug, vg = _gelu(up), _gelu(vp)
        for g in range(GROUPS):
            sl = slice(g * c, (g + 1) * c)
            vt = vg[:, sl]
            r = lax.rsqrt(jnp.mean(vt * vt, axis=-1, keepdims=True) + EPS)
            vh = vt * r
            gv = gv_ref[:, sl]
            vn = vh * gv
            s = _mm(jnp.where(tril, w_ref[g], 0.0), vn) + b_ref[g]
            dy = dy_ref[:, sl]
            ds = dy * ug[:, sl]
            dw_ref[g] += jnp.where(tril, _mm(ds, vn, _NT), 0.0)
            db_ref[g] += jnp.sum(ds, axis=1, keepdims=True)
            dvn = _mm(jnp.where(triu, wt_ref[g], 0.0), ds)
            dgv_ref[:, sl] += jnp.sum(dvn * vh, axis=0, keepdims=True)
            dvh = dvn * gv
            dvt = r * (dvh - vh * jnp.mean(dvh * vh, axis=-1, keepdims=True))
            duv_ref[:, sl] = (dy * s * _dgelu(up[:, sl])).astype(duv_ref.dtype)
            duv_ref[:, w + g * c:w + (g + 1) * c] = (dvt * _dgelu(vp[:, sl])).astype(duv_ref.dtype)

    full3 = lambda shape: pl.BlockSpec(shape, lambda i: (0, 0, 0))
    return pl.pallas_call(
        body, name=name, grid=(t // c,),
        in_specs=[pl.BlockSpec((c, w), lambda i: (i, ucol // w)), pl.BlockSpec((c, w), lambda i: (i, ucol // w + 1)),
                  pl.BlockSpec((c, w), lambda i: (i, 0)), full3((GROUPS, c, c)), full3((GROUPS, c, c)),
                  full3((GROUPS, c, 1)), pl.BlockSpec((1, w), lambda i: (0, 0)), _ANY],
        out_specs=[pl.BlockSpec((c, w2), lambda i: (i, ucol // w2)), full3((GROUPS, c, c)), full3((GROUPS, c, 1)),
                   pl.BlockSpec((1, w), lambda i: (0, 0))],
        out_shape=[jax.ShapeDtypeStruct(dproj.shape, dproj.dtype), jax.ShapeDtypeStruct((GROUPS, c, c), F32),
                   jax.ShapeDtypeStruct((GROUPS, c, 1), F32), jax.ShapeDtypeStruct((1, w), F32)],
        input_output_aliases={7: 0},
        compiler_params=_params(("arbitrary",)))(proj, proj, dy, w_s, w_st, b_col, g_v, dproj)


_Q_SCALE = HEAD_DIM ** -0.5


def _conv_taps(ext, w_ref):
    shifted = [ext[SUBLANES:]] + [pltpu.roll(ext, s, 0)[SUBLANES:] for s in range(1, CONV_K)]
    acc = shifted[0] * w_ref[pl.ds(CONV_K - 1, 1), :]
    for s in range(1, CONV_K):
        acc = acc + shifted[s] * w_ref[pl.ds(CONV_K - 1 - s, 1), :]
    return acc, shifted


def _conv_fwd(qkv, w, name):
    t, cw = qkv.shape[0], w.shape[1]
    tt = _tile(t, (256, 128))
    hb = tt // SUBLANES

    def body(x_ref, p_ref, w_ref, o_ref):
        prev = jnp.where(pl.program_id(0) > 0, p_ref[...], 0.0)
        acc, _ = _conv_taps(jnp.concatenate([prev, x_ref[...]], axis=0), w_ref)
        y = _silu(acc)
        for which in range(3):
            for h in range(HEADS):
                lo = (which * HEADS + h) * HEAD_DIM
                seg = y[:, lo:lo + HEAD_DIM]
                if which < 2:
                    seg = seg * lax.rsqrt(jnp.sum(seg * seg, axis=-1, keepdims=True) + EPS)
                if which == 0:
                    seg = seg * _Q_SCALE
                o_ref[which, h] = seg

    return pl.pallas_call(
        body, name=name, grid=(t // tt,),
        in_specs=[pl.BlockSpec((tt, cw), lambda i: (i, 0)),
                  pl.BlockSpec((SUBLANES, cw), lambda i: (jnp.maximum(i * hb - 1, 0), 0)),
                  pl.BlockSpec((CONV_K, cw), lambda i: (0, 0))],
        out_specs=pl.BlockSpec((3, HEADS, tt, HEAD_DIM), lambda i: (0, 0, i, 0)),
        out_shape=jax.ShapeDtypeStruct((3, HEADS, t, HEAD_DIM), F32),
        compiler_params=_params(("parallel",)))(qkv, qkv, w)


def _conv_bwd_pre(qkv, dq, dk, dv, w, name):
    t, cw = qkv.shape[0], w.shape[1]
    tt = _tile(t, (256, 128))
    hb = tt // SUBLANES

    def body(x_ref, p_ref, dq_ref, dk_ref, dv_ref, w_ref, da_ref, dw_ref):
        @pl.when(pl.program_id(0) == 0)
        def _():
            dw_ref[...] = jnp.zeros_like(dw_ref)

        prev = jnp.where(pl.program_id(0) > 0, p_ref[...], 0.0)
        acc, shifted = _conv_taps(jnp.concatenate([prev, x_ref[...]], axis=0), w_ref)
        y = _silu(acc)
        d_refs = (dq_ref, dk_ref, dv_ref)
        for which in range(3):
            for h in range(HEADS):
                lo = (which * HEADS + h) * HEAD_DIM
                sl = slice(lo, lo + HEAD_DIM)
                dn = d_refs[which][h]
                if which < 2:
                    seg = y[:, sl]
                    rho = lax.rsqrt(jnp.sum(seg * seg, axis=-1, keepdims=True) + EPS)
                    nrm = seg * rho
                    if which == 0:
                        dn = dn * _Q_SCALE
                    dn = rho * (dn - nrm * jnp.sum(dn * nrm, axis=-1, keepdims=True))
                dacc = dn * _dsilu(acc[:, sl])
                da_ref[:, sl] = dacc
                for s in range(CONV_K):
                    dw_ref[pl.ds(CONV_K - 1 - s, 1), sl] += jnp.sum(dacc * shifted[s][:, sl], axis=0, keepdims=True)

    head = pl.BlockSpec((HEADS, tt, HEAD_DIM), lambda i: (0, i, 0))
    return pl.pallas_call(
        body, name=name, grid=(t // tt,),
        in_specs=[pl.BlockSpec((tt, cw), lambda i: (i, 0)),
                  pl.BlockSpec((SUBLANES, cw), lambda i: (jnp.maximum(i * hb - 1, 0), 0)),
                  head, head, head, pl.BlockSpec((CONV_K, cw), lambda i: (0, 0))],
        out_specs=[pl.BlockSpec((tt, cw), lambda i: (i, 0)), pl.BlockSpec((CONV_K, cw), lambda i: (0, 0))],
        out_shape=[jax.ShapeDtypeStruct((t, cw), F32), jax.ShapeDtypeStruct((CONV_K, cw), F32)],
        compiler_params=_params(("arbitrary",)))(qkv, qkv, dq, dk, dv, w)


def _conv_bwd_in(dacc, w, dproj, name):
    t, cw = dacc.shape
    tt = _tile(t, (256, 128))
    hb = tt // SUBLANES
    nt = t // tt
    rows = tt + SUBLANES

    def body(d_ref, n_ref, w_ref, _, o_ref):
        cur = d_ref[...]
        nxt = jnp.where(pl.program_id(0) < nt - 1, n_ref[...], 0.0)
        ext = jnp.concatenate([cur, nxt], axis=0)
        acc = cur * w_ref[pl.ds(CONV_K - 1, 1), :]
        for s in range(1, CONV_K):
            acc = acc + pltpu.roll(ext, rows - s, 0)[:tt] * w_ref[pl.ds(CONV_K - 1 - s, 1), :]
        o_ref[...] = acc.astype(o_ref.dtype)

    return pl.pallas_call(
        body, name=name, grid=(nt,),
        in_specs=[pl.BlockSpec((tt, cw), lambda i: (i, 0)),
                  pl.BlockSpec((SUBLANES, cw), lambda i: (jnp.minimum((i + 1) * hb, t // SUBLANES - 1), 0)),
                  pl.BlockSpec((CONV_K, cw), lambda i: (0, 0)), _ANY],
        out_specs=pl.BlockSpec((tt, cw), lambda i: (i, 0)), out_shape=jax.ShapeDtypeStruct(dproj.shape, dproj.dtype),
        input_output_aliases={3: 0}, compiler_params=_params(("parallel",)))(dacc, dacc, w, dproj)


_INV_BASE_SHIFT = 3


def _inv_unit_lower(a, eye):
    c = GDN_CHUNK
    ri = lax.broadcasted_iota(jnp.int32, (c, c), 0)
    ci = lax.broadcasted_iota(jnp.int32, (c, c), 1)
    same = lambda sh: (ri >> sh) == (ci >> sh)
    x = jnp.where(same(_INV_BASE_SHIFT), -a, 0.0)
    p = jnp.where(eye, 1.0, 0.0) + x
    xs = _split(x)
    x2 = _mm3(xs, xs)
    x2s, ps = _split(x2), _split(p)
    r = _mm3(x2s, tuple(jnp.concatenate([u, v], axis=-1) for u, v in zip(x2s, ps)))
    x4, p = r[..., :c], p + r[..., c:]
    p = p + _mm3(_split(x4), _split(p))
    for sh in range(_INV_BASE_SHIFT, c.bit_length() - 1):
        off = jnp.where(same(sh + 1) & jnp.logical_not(same(sh)), a, 0.0)
        ps = _split(p)
        p = p - _mm3(ps, _split(_mm3(_split(off), ps)))
    return p


def _split(a):
    hi = a.astype(BF16)
    return hi, (a - hi.astype(F32)).astype(BF16)


def _dot_heads(u, v, dims):
    if u.ndim == 3:
        return jnp.stack([_dot_heads(u[j], v[j], dims) for j in range(u.shape[0])])
    return lax.dot_general(u, v, dims, preferred_element_type=F32)


def _mm3(a, b):
    return _dot_heads(a[0], b[0], _NN) + (_dot_heads(a[0], b[1], _NN) + _dot_heads(a[1], b[0], _NN))


def _hmm(a, b, dims=_NN):
    return _dot_heads(a.astype(MXU_DTYPE), b.astype(MXU_DTYPE), dims)


def _rowsum(x):
    return jnp.sum(x, axis=-1, keepdims=True)


def _colsum(x):
    return jnp.sum(x, axis=-2, keepdims=True)


class _Pre:
    pass


def _gdn_pre(q, k, v, araw, braw, alog, dtb, t_mat=None):
    c = GDN_CHUNK
    p = _Pre()
    p.tril, p.strict, p.eye = _tri_masks(c)
    p.to_col = lambda row: _rowsum(jnp.where(p.eye, row, 0.0))
    p.to_row = lambda col: _colsum(jnp.where(p.eye, col, 0.0))
    p.a_neg = -jnp.exp(alog + jnp.zeros((1, c), F32))
    p.xg = araw + dtb
    p.g_row = p.a_neg * _softplus(p.xg)
    p.beta_row = _sigmoid(braw)
    p.beta = p.to_col(p.beta_row)
    gam = _rowsum(jnp.where(p.tril, p.g_row, 0.0))
    gam_last = _rowsum(p.g_row)
    p.dm = jnp.where(p.tril, jnp.exp(jnp.where(p.tril, gam - p.to_row(gam), 0.0)), 0.0)
    p.e, p.ek, p.el = jnp.exp(gam), jnp.exp(gam_last - gam), jnp.exp(gam_last)
    p.kb = k * p.beta
    p.kk = _hmm(p.kb, k, _NT)
    p.t = _inv_unit_lower(jnp.where(p.strict, p.kk * p.dm, 0.0), p.eye) if t_mat is None else t_mat
    p.vb, p.kbe = v * p.beta, p.kb * p.e
    uw = _hmm(p.t, jnp.concatenate([p.vb, p.kbe], axis=-1))
    p.u, p.w = uw[..., :v.shape[-1]], uw[..., v.shape[-1]:]
    p.qk0 = _hmm(q, k, _NT)
    p.qk = p.qk0 * p.dm
    p.qd, p.kd = q * p.e, k * p.ek
    return p


GDN_HEADS_PER_STEP = 8


def _head_scalars(ref, hb):
    h0 = pl.program_id(0) * hb
    return jnp.stack([jnp.full((1, 1), ref[h0 + j], F32) for j in range(hb)])


def _gdn_specs(n, reverse):
    c, dk, hb = GDN_CHUNK, HEAD_DIM, GDN_HEADS_PER_STEP
    ix = (lambda i: n - 1 - i) if reverse else (lambda i: i)
    smem = pl.BlockSpec(memory_space=pltpu.SMEM)
    qkv = [pl.BlockSpec((None, hb, c, dk), functools.partial(lambda w, h, i: (w, h, ix(i), 0), w)) for w in range(3)]
    row = pl.BlockSpec((hb, None, 1, c), lambda h, i: (h, ix(i), 0, 0))
    tok = pl.BlockSpec((hb, c, dk), lambda h, i: (h, ix(i), 0))
    state = pl.BlockSpec((hb, None, dk, dk), lambda h, i: (h, ix(i), 0, 0))
    return smem, qkv, row, tok, state


def _gdn_fwd(qkv_h, araw, braw, alog, dtb, name, side=_NoSide):
    _, hh, t, dk = qkv_h.shape
    n, hb = t // GDN_CHUNK, GDN_HEADS_PER_STEP
    smem, qkv, row, tok, state = _gdn_specs(n, False)
    grid = (hh // hb, n)

    def body(*refs):
        main, side_start, side_finish = _side_hooks(side, refs, 7, 3, 1, grid)
        alog_ref, dt_ref, q_ref, k_ref, v_ref, a_ref, b_ref, o_ref, so_ref, to_ref, s_ref = main
        side_start()

        @pl.when(pl.program_id(1) == 0)
        def _():
            s_ref[...] = jnp.zeros_like(s_ref)

        p = _gdn_pre(q_ref[...], k_ref[...], v_ref[...], a_ref[...], b_ref[...],
                     _head_scalars(alog_ref, hb), _head_scalars(dt_ref, hb))
        s = s_ref[...]
        vn = p.u - _hmm(p.w, s)
        o_ref[...] = _hmm(p.qd, s) + _hmm(p.qk, vn)
        so_ref[...] = s
        to_ref[...] = p.t
        s_ref[...] = s * p.el + _hmm(p.kd, vn, _TN)
        side_finish()

    mats = jax.ShapeDtypeStruct((hh, n, dk, dk), F32)
    return _carrier_call(
        body, name, grid, [smem, smem] + qkv + [row, row], [tok, state, state],
        [jax.ShapeDtypeStruct((hh, t, dk), F32), mats, mats],
        [pltpu.VMEM((hb, dk, dk), F32)], side, (alog, dtb, qkv_h, qkv_h, qkv_h, araw, braw))


def _gdn_bwd(qkv_h, araw, braw, alog, dtb, states, t_mats, do, name, side=_NoSide):
    _, hh, t, dk = qkv_h.shape
    c, hb = GDN_CHUNK, GDN_HEADS_PER_STEP
    n = t // c
    smem, qkv, row, tok, state = _gdn_specs(n, True)
    acc = pl.BlockSpec((hb, 1, LANES), lambda h, i: (h, 0, 0))
    grid = (hh // hb, n)

    def body(*refs):
        main, side_start, side_finish = _side_hooks(side, refs, 10, 7, 1, grid)
        (alog_ref, dt_ref, q_ref, k_ref, v_ref, a_ref, b_ref, s_ref, t_ref, do_ref,
         dq_ref, dk_ref, dv_ref, da_ref, db_ref, dal_ref, ddt_ref, ds_ref) = main
        side_start()

        @pl.when(pl.program_id(1) == 0)
        def _():
            ds_ref[...] = jnp.zeros_like(ds_ref)
            dal_ref[...] = jnp.zeros_like(dal_ref)
            ddt_ref[...] = jnp.zeros_like(ddt_ref)

        q, k, v = q_ref[...], k_ref[...], v_ref[...]
        p = _gdn_pre(q, k, v, a_ref[...], b_ref[...], _head_scalars(alog_ref, hb), _head_scalars(dt_ref, hb),
                     t_ref[...])
        s, do, dsp = s_ref[...], do_ref[...], ds_ref[...]
        vn = p.u - _hmm(p.w, s)
        dqd = _hmm(do, s, _NT)
        dqk = _hmm(do, vn, _NT)
        dvn = _hmm(p.qk, do, _TN) + _hmm(p.kd, dsp)
        dkd = _hmm(vn, dsp, _NT)
        d_el = _colsum(_rowsum(s * dsp))
        ds_ref[...] = dsp * p.el + _hmm(p.qd, do, _TN) - _hmm(p.w, dvn, _TN)
        dw = -_hmm(dvn, s, _NT)
        d_t = _hmm(dvn, p.vb, _NT) + _hmm(dw, p.kbe, _NT)
        dvb, dkbe = _hmm(p.t, dvn, _TN), _hmm(p.t, dw, _TN)
        d_a = jnp.where(p.strict, -_hmm(p.t, _hmm(d_t, p.t, _NT), _TN), 0.0)
        dkk = d_a * p.dm
        dqk0 = dqk * p.dm
        ddm = d_a * p.kk + dqk * p.qk0
        dkb = _hmm(dkk, k) + dkbe * p.e
        dq_ref[...] = _hmm(dqk0, k) + dqd * p.e
        dk_ref[...] = _hmm(dkk, p.kb, _TN) + _hmm(dqk0, q, _TN) + dkd * p.ek + dkb * p.beta
        dv_ref[...] = dvb * p.beta
        dbeta = _rowsum(dkb * k) + _rowsum(dvb * v)
        d_e = _rowsum(dqd * q) + _rowsum(dkbe * p.kb)
        d_ek = _rowsum(dkd * k)
        m = ddm * p.dm
        dgam = d_e * p.e - d_ek * p.ek + _rowsum(m) - p.to_col(_colsum(m))
        dgam_last = _colsum(d_ek * p.ek) + d_el * p.el
        dg_row = _colsum(jnp.where(p.tril, dgam, 0.0)) + dgam_last
        da_row = dg_row * p.a_neg * _sigmoid(p.xg)
        da_ref[...] = da_row
        db_ref[...] = p.to_row(dbeta) * p.beta_row * (1.0 - p.beta_row)
        dal_ref[...] += _rowsum(dg_row * p.g_row)
        ddt_ref[...] += _rowsum(da_row)
        side_finish()

    tok_shape = jax.ShapeDtypeStruct((hh, t, dk), F32)
    row_shape = jax.ShapeDtypeStruct((hh, n, 1, c), F32)
    acc_shape = jax.ShapeDtypeStruct((hh, 1, LANES), F32)
    return _carrier_call(
        body, name, grid, [smem, smem] + qkv + [row, row, state, state, tok], [tok, tok, tok, row, row, acc, acc],
        [tok_shape, tok_shape, tok_shape, row_shape, row_shape, acc_shape, acc_shape],
        [pltpu.VMEM((hb, dk, dk), F32)], side, (alog, dtb, qkv_h, qkv_h, qkv_h, araw, braw, states, t_mats, do))


def _gdn_post_fwd(o, proj, zcol, g_o, name):
    hh, t, dv = o.shape
    tt = _tile(t, _ROW_TILES)
    zblk = zcol // (hh * dv)

    def body(o_ref, z_ref, g_ref, y_ref):
        for h in range(hh):
            sl = slice(h * dv, (h + 1) * dv)
            ov = o_ref[h]
            r = lax.rsqrt(jnp.mean(ov * ov, axis=-1, keepdims=True) + EPS)
            y_ref[:, sl] = (ov * r * g_ref[...] * _silu(z_ref[:, sl])).astype(y_ref.dtype)

    return pl.pallas_call(
        body, name=name, grid=(t // tt,),
        in_specs=[pl.BlockSpec((hh, tt, dv), lambda i: (0, i, 0)), pl.BlockSpec((tt, hh * dv), lambda i: (i, zblk)),
                  pl.BlockSpec((1, dv), lambda i: (0, 0))],
        out_specs=pl.BlockSpec((tt, hh * dv), lambda i: (i, 0)),
        out_shape=jax.ShapeDtypeStruct((t, hh * dv), MXU_DTYPE), compiler_params=_params(("parallel",)))(o, proj, g_o)


def _gdn_post_bwd(o, proj, zcol, dy, g_o, dproj, name):
    hh, t, dv = o.shape
    tt = _tile(t, _ROW_TILES)
    zblk = zcol // (hh * dv)

    def body(o_ref, z_ref, dy_ref, g_ref, _, do_ref, dz_ref, dg_ref):
        @pl.when(pl.program_id(0) == 0)
        def _():
            dg_ref[...] = jnp.zeros_like(dg_ref)

        gv = g_ref[...]
        for h in range(hh):
            sl = slice(h * dv, (h + 1) * dv)
            ov, zz, dy = o_ref[h], z_ref[:, sl], dy_ref[:, sl]
            r = lax.rsqrt(jnp.mean(ov * ov, axis=-1, keepdims=True) + EPS)
            oh = ov * r
            dz_ref[:, sl] = (dy * oh * gv * _dsilu(zz)).astype(dz_ref.dtype)
            don = dy * _silu(zz)
            dg_ref[...] += _colsum(don * oh)
            doh = don * gv
            do_ref[h] = r * (doh - oh * jnp.mean(doh * oh, axis=-1, keepdims=True))

    return pl.pallas_call(
        body, name=name, grid=(t // tt,),
        in_specs=[pl.BlockSpec((hh, tt, dv), lambda i: (0, i, 0)), pl.BlockSpec((tt, hh * dv), lambda i: (i, zblk)),
                  pl.BlockSpec((tt, hh * dv), lambda i: (i, 0)), pl.BlockSpec((1, dv), lambda i: (0, 0)), _ANY],
        out_specs=[pl.BlockSpec((hh, tt, dv), lambda i: (0, i, 0)), pl.BlockSpec((tt, hh * dv), lambda i: (i, zblk)),
                   pl.BlockSpec((1, dv), lambda i: (0, 0))],
        out_shape=[jax.ShapeDtypeStruct((hh, t, dv), F32), jax.ShapeDtypeStruct(dproj.shape, dproj.dtype),
                   jax.ShapeDtypeStruct((1, dv), F32)],
        input_output_aliases={4: 1}, compiler_params=_params(("arbitrary",)))(o, proj, dy, g_o, dproj)


def _cols_as_rows(x, col, name):
    t = x.shape[0]
    tt = _tile(t, _ROW_TILES)

    def body(x_ref, o_ref):
        o_ref[...] = x_ref[...].T

    return pl.pallas_call(
        body, name=name, grid=(t // tt,), in_specs=[pl.BlockSpec((tt, LANES), lambda i: (i, col // LANES))],
        out_specs=pl.BlockSpec((LANES, tt), lambda i: (0, i)), out_shape=jax.ShapeDtypeStruct((LANES, t), x.dtype),
        compiler_params=_params(("parallel",)))(x)


def _rows_into_cols(dst, rows, col, name):
    t = dst.shape[0]
    tt = _tile(t, _ROW_TILES)

    def body(r_ref, _, o_ref):
        o_ref[...] = r_ref[...].T.astype(o_ref.dtype)

    return pl.pallas_call(
        body, name=name, grid=(t // tt,), in_specs=[pl.BlockSpec((LANES, tt), lambda i: (0, i)), _ANY],
        out_specs=pl.BlockSpec((tt, LANES), lambda i: (i, col // LANES)),
        out_shape=jax.ShapeDtypeStruct(dst.shape, dst.dtype), input_output_aliases={1: 0},
        compiler_params=_params(("parallel",)))(rows, dst)


def _adamw(g, w, m, v):
    m = ADAM_B1 * m + (1.0 - ADAM_B1) * g
    v = ADAM_B2 * v + (1.0 - ADAM_B2) * (g * g)
    m_hat = m / (1.0 - ADAM_B1 ** ADAM_STEP)
    v_hat = v / (1.0 - ADAM_B2 ** ADAM_STEP)
    return -ADAM_LR * (m_hat / (jnp.sqrt(v_hat) + ADAM_EPS) + ADAM_WD * w), m, v


def _ada_fwd(c_all, ada_w, name):
    nl, d, cols = ada_w.shape
    b = c_all.shape[0]

    def body(c_ref, w_ref, o_ref):
        o_ref[...] = _mm_hi(_silu(c_ref[...]), w_ref[...])

    return pl.pallas_call(
        body, name=name, grid=(nl,),
        in_specs=[pl.BlockSpec((b, d), lambda i: (0, 0)), pl.BlockSpec((None, d, cols), lambda i: (i, 0, 0))],
        out_specs=pl.BlockSpec((None, b, cols), lambda i: (i, 0, 0)),
        out_shape=jax.ShapeDtypeStruct((nl, b, cols), F32), compiler_params=_params(("parallel",)))(c_all, ada_w)


def _ada_bwd(c_col, dm, w, m, v, name):
    nl, d, cols = w.shape
    b = c_col.shape[0]
    tr = _tile(d, (256, 128))

    def body(c_ref, dm_ref, w_ref, m_ref, v_ref, g_ref, dl_ref, mo_ref, vo_ref):
        g = _silu(c_ref[0]) * dm_ref[pl.ds(0, 1), :]
        for j in range(1, b):
            g = g + _silu(c_ref[j]) * dm_ref[pl.ds(j, 1), :]
        g_ref[...] = g
        dl_ref[...], mo_ref[...], vo_ref[...] = _adamw(g, w_ref[...], m_ref[...], v_ref[...])

    blk = pl.BlockSpec((None, tr, cols), lambda l, i: (l, i, 0))
    shape = jax.ShapeDtypeStruct((nl, d, cols), F32)
    return pl.pallas_call(
        body, name=name, grid=(nl, d // tr),
        in_specs=[pl.BlockSpec((b, tr, 1), lambda l, i: (0, i, 0)), pl.BlockSpec((None, b, cols), lambda l, i: (l, 0, 0)),
                  blk, blk, blk],
        out_specs=[blk, blk, blk, blk], out_shape=[shape] * 4,
        compiler_params=_params(("parallel", "parallel")))(c_col, dm, w, m, v)


def _sum_adam(parts, w, m, v, name):
    nl, npart, r, cdim = parts.shape
    tr = _tile(r, (256, 128))

    def body(p_ref, w_ref, m_ref, v_ref, g_ref, dl_ref, mo_ref, vo_ref):
        g = p_ref[0].astype(F32)
        for j in range(1, npart):
            g = g + p_ref[j].astype(F32)
        g_ref[...] = g
        dl_ref[...], mo_ref[...], vo_ref[...] = _adamw(g, w_ref[...], m_ref[...], v_ref[...])

    blk = pl.BlockSpec((None, tr, cdim), lambda l, i: (l, i, 0))
    shape = jax.ShapeDtypeStruct((nl, r, cdim), F32)
    return pl.pallas_call(
        body, name=name, grid=(nl, r // tr),
        in_specs=[pl.BlockSpec((None, npart, tr, cdim), lambda l, i: (l, 0, i, 0)), blk, blk, blk],
        out_specs=[blk, blk, blk, blk], out_shape=[shape] * 4,
        compiler_params=_params(("parallel", "parallel")))(parts, w, m, v)


def _cols_from_blocks(g, plan, width, name):
    _, r, cdim = g.shape
    tr = _tile(r, (256, 128))
    covered = sorted((dst, dst + n) for _, _, n, dst in plan)
    holes, pos = [], 0
    for a, b in covered:
        if a > pos:
            holes.append((pos, a))
        pos = max(pos, b)
    if pos < width:
        holes.append((pos, width))

    def body(g_ref, o_ref):
        for a, b in holes:
            o_ref[:, a:b] = jnp.zeros((tr, b - a), o_ref.dtype)
        for j, src, n, dst in plan:
            o_ref[:, dst:dst + n] = g_ref[j, :, src:src + n]

    return pl.pallas_call(
        body, name=name, grid=(r // tr,), in_specs=[pl.BlockSpec((N_DEV, tr, cdim), lambda i: (0, i, 0))],
        out_specs=pl.BlockSpec((tr, width), lambda i: (i, 0)), out_shape=jax.ShapeDtypeStruct((r, width), g.dtype),
        compiler_params=_params(("parallel",)))(g)


def _blocks_from_cols(w, plan, cdim, name):
    r, width = w.shape
    tr = _tile(r, (256, 128))

    def body(w_ref, o_ref):
        for j, src, n, dst in plan:
            o_ref[j, :, src:src + n] = w_ref[:, dst:dst + n]

    return pl.pallas_call(
        body, name=name, grid=(r // tr,), in_specs=[pl.BlockSpec((tr, width), lambda i: (i, 0))],
        out_specs=pl.BlockSpec((N_DEV, tr, cdim), lambda i: (0, i, 0)),
        out_shape=jax.ShapeDtypeStruct((N_DEV, r, cdim), w.dtype), compiler_params=_params(("parallel",)))(w)


def _pair_sum(x, tmp, core, name):
    _, r, cdim = x.shape
    tr = _tile(r, (256, 128))

    def body(core_ref, x_ref, t_ref, o_ref):
        o_ref[...] = (x_ref[...] + t_ref[...]).astype(o_ref.dtype)

    grid_spec = pltpu.PrefetchScalarGridSpec(
        num_scalar_prefetch=1, grid=(N_DEV // 2, r // tr),
        in_specs=[pl.BlockSpec((None, tr, cdim), lambda ch, i, core_ref: (2 * ch + core_ref[0], i, 0)),
                  pl.BlockSpec((None, tr, cdim), lambda ch, i, core_ref: (ch, i, 0))],
        out_specs=pl.BlockSpec((None, tr, cdim), lambda ch, i, core_ref: (ch, i, 0)))
    return pl.pallas_call(
        body, name=name, grid_spec=grid_spec, out_shape=jax.ShapeDtypeStruct((N_DEV // 2, r, cdim), WIRE_DTYPE),
        compiler_params=_params(("parallel", "parallel")))(core, x, tmp)


_ANY = pl.BlockSpec(memory_space=pl.ANY)
_CHIP_FLIPS = ((1, 0), (0, 1), (1, 1))


def _coords():
    return lax.axis_index("x"), lax.axis_index("y"), lax.axis_index("c")


def _flip(v, f):
    return 1 - v if f else v


def _a2a_direct(xs, name):
    n, ncp = len(xs), N_DEV - 1

    def body(*refs):
        ins, outs = refs[:n], refs[n:2 * n]
        send, recv, loc = refs[2 * n:]
        x, y, c = _coords()
        me = 4 * x + 2 * y + c
        local = [pltpu.make_async_copy(ins[i].at[me], outs[i].at[me], loc.at[i]) for i in range(n)]
        for cp in local:
            cp.start()
        remote = []
        for i in range(n):
            for k in range(1, N_DEV):
                px, py, pc = _flip(x, k & 4), _flip(y, k & 2), _flip(c, k & 1)
                cp = pltpu.make_async_remote_copy(
                    src_ref=ins[i].at[4 * px + 2 * py + pc], dst_ref=outs[i].at[me],
                    send_sem=send.at[i * ncp + k - 1], recv_sem=recv.at[i * ncp + k - 1],
                    device_id=(px, py, pc), device_id_type=MESH)
                cp.start()
                remote.append(cp)
        for cp in remote:
            cp.wait()
        for cp in local:
            cp.wait()

    return pl.pallas_call(
        body, name=name, in_specs=[_ANY] * n, out_specs=[_ANY] * n,
        out_shape=[jax.ShapeDtypeStruct(a.shape, a.dtype) for a in xs],
        scratch_shapes=[pltpu.SemaphoreType.DMA((n * ncp,)), pltpu.SemaphoreType.DMA((n * ncp,)),
                        pltpu.SemaphoreType.DMA((n,))])(*xs)


class _AllGatherSide:
    def __init__(self, blocks):
        self.operands = list(blocks)
        n = self.n = len(self.operands)
        self.n_in = self.n_out = n
        self.out_shape = [jax.ShapeDtypeStruct((N_DEV,) + a.shape, a.dtype) for a in self.operands]
        self.aliases = {}
        nici, nd2d = len(_CHIP_FLIPS), N_DEV // 2
        self.scratch = [pltpu.SemaphoreType.DMA((n * nici,)), pltpu.SemaphoreType.DMA((n * nici,)),
                        pltpu.SemaphoreType.DMA((n * nd2d,)), pltpu.SemaphoreType.DMA((n * nd2d,)),
                        pltpu.SemaphoreType.DMA((n,))]

    def _first(self, ins, outs, sems):
        send, recv, _, _, loc = sems
        x, y, c = _coords()
        me = 4 * x + 2 * y + c
        nici = len(_CHIP_FLIPS)
        local = [pltpu.make_async_copy(ins[i], outs[i].at[me], loc.at[i]) for i in range(self.n)]
        remote = [pltpu.make_async_remote_copy(
            src_ref=ins[i], dst_ref=outs[i].at[me], send_sem=send.at[i * nici + j], recv_sem=recv.at[i * nici + j],
            device_id=(_flip(x, fx), _flip(y, fy), c), device_id_type=MESH)
            for i in range(self.n) for j, (fx, fy) in enumerate(_CHIP_FLIPS)]
        return local + remote

    def _second(self, outs, sems):
        _, _, send, recv, _ = sems
        x, y, c = _coords()
        nd2d = N_DEV // 2
        return [pltpu.make_async_remote_copy(
            src_ref=outs[i].at[2 * ch + c], dst_ref=outs[i].at[2 * ch + c], send_sem=send.at[i * nd2d + ch],
            recv_sem=recv.at[i * nd2d + ch], device_id=(x, y, 1 - c), device_id_type=MESH)
            for i in range(self.n) for ch in range(nd2d)]

    def start(self, ins, outs, sems):
        for cp in self._first(ins, outs, sems):
            cp.start()

    def finish(self, ins, outs, sems):
        for cp in self._first(ins, outs, sems):
            cp.wait()
        second = self._second(outs, sems)
        for cp in second:
            cp.start()
        for cp in second:
            cp.wait()


class _ReduceScatterIciSide:
    def __init__(self, sums, accs, layer):
        self.operands = list(sums) + list(accs)
        n = self.n = len(sums)
        self.layer = layer
        self.n_in, self.n_out = 2 * n, n
        self.out_shape = [jax.ShapeDtypeStruct(a.shape, a.dtype) for a in accs]
        self.aliases = {n + i: i for i in range(n)}
        nici = len(_CHIP_FLIPS)
        self.scratch = [pltpu.SemaphoreType.DMA((n * nici,)), pltpu.SemaphoreType.DMA((n * nici,)),
                        pltpu.SemaphoreType.DMA((n,))]

    def _copies(self, ins, outs, sems):
        send, recv, loc = sems
        x, y, c = _coords()
        chip = 2 * x + y
        nici = len(_CHIP_FLIPS)
        local = [pltpu.make_async_copy(ins[i].at[chip], outs[i].at[self.layer, chip], loc.at[i])
                 for i in range(self.n)]
        remote = [pltpu.make_async_remote_copy(
            src_ref=ins[i].at[2 * _flip(x, fx) + _flip(y, fy)], dst_ref=outs[i].at[self.layer, chip],
            send_sem=send.at[i * nici + j], recv_sem=recv.at[i * nici + j],
            device_id=(_flip(x, fx), _flip(y, fy), c), device_id_type=MESH)
            for i in range(self.n) for j, (fx, fy) in enumerate(_CHIP_FLIPS)]
        return local + remote

    def start(self, ins, outs, sems):
        for cp in self._copies(ins, outs, sems):
            cp.start()

    def finish(self, ins, outs, sems):
        for cp in self._copies(ins, outs, sems):
            cp.wait()


def _run_side(side, name):
    def body(*refs):
        ins, outs = refs[:side.n_in], refs[side.n_in:side.n_in + side.n_out]
        sems = refs[side.n_in + side.n_out:]
        side.start(ins, outs, sems)
        side.finish(ins, outs, sems)

    return pl.pallas_call(
        body, name=name, in_specs=[_ANY] * side.n_in, out_specs=[_ANY] * side.n_out, out_shape=side.out_shape,
        input_output_aliases=side.aliases, scratch_shapes=side.scratch)(*side.operands)


class _ReduceScatterD2dSide:
    def __init__(self, parts):
        self.operands = list(parts)
        n = self.n = len(self.operands)
        self.n_in = self.n_out = n
        nd2d = N_DEV // 2
        self.out_shape = [jax.ShapeDtypeStruct((nd2d,) + a.shape[1:], a.dtype) for a in self.operands]
        self.aliases = {}
        self.scratch = [pltpu.SemaphoreType.DMA((n * nd2d,)), pltpu.SemaphoreType.DMA((n * nd2d,))]

    def _copies(self, ins, outs, sems):
        send, recv = sems
        x, y, c = _coords()
        nd2d = N_DEV // 2
        return [pltpu.make_async_remote_copy(
            src_ref=ins[i].at[2 * ch + 1 - c], dst_ref=outs[i].at[ch], send_sem=send.at[i * nd2d + ch],
            recv_sem=recv.at[i * nd2d + ch], device_id=(x, y, 1 - c), device_id_type=MESH)
            for i in range(self.n) for ch in range(nd2d)]

    def start(self, ins, outs, sems):
        for cp in self._copies(ins, outs, sems):
            cp.start()

    def finish(self, ins, outs, sems):
        for cp in self._copies(ins, outs, sems):
            cp.wait()


_PACK_ROWS = 256


def _pack(arrs):
    flat = jnp.concatenate([a.reshape(-1) for a in arrs])
    quantum = _PACK_ROWS * LANES
    total = -(-flat.shape[0] // quantum) * quantum
    return jnp.pad(flat, (0, total - flat.shape[0])).reshape(-1, LANES)


def _unpack(packed, like):
    flat, out, pos = packed.reshape(-1), [], 0
    for a in like:
        out.append(flat[pos:pos + a.size].reshape(a.shape))
        pos += a.size
    return out


def kernel(x, c, ada_w, ada_b, norm1_g, w_in, conv_w, spatial_w, spatial_b, v_norm_g, a_log, dt_bias, o_norm_g, w_branch_a, w_branch_b, w_out, norm2_g, w_ffn_in, w_ffn_out, final_g, loss_target, m_ada_w, m_ada_b, m_norm1_g, m_w_in, m_conv_w, m_spatial_w, m_spatial_b, m_v_norm_g, m_a_log, m_dt_bias, m_o_norm_g, m_w_branch_a, m_w_branch_b, m_w_out, m_norm2_g, m_w_ffn_in, m_w_ffn_out, m_final_g, v_ada_w, v_ada_b, v_norm1_g, v_w_in, v_conv_w, v_spatial_w, v_spatial_b, v_v_norm_g, v_a_log, v_dt_bias, v_o_norm_g, v_w_branch_a, v_w_branch_b, v_w_out, v_norm2_g, v_w_ffn_in, v_w_ffn_out, v_final_g):
    nl, d = ada_w.shape[0], x.shape[2]
    t = x.shape[1]
    nchunk = t // GDN_CHUNK
    xi, yi, ci = _coords()
    me = 4 * xi + 2 * yi + ci
    core = jnp.reshape(ci, (1,)).astype(jnp.int32)
    x0, target = x[0], loss_target[0]
    wcols = 3 * HEADS * HEAD_DIM
    lay = _ProjLayout(d)
    in_pieces = lay.pieces(w_in.shape[2])
    fi_shard = w_ffn_in.shape[2]
    fi_pieces = [(j, 0, fi_shard, fi_shard * j) for j in range(N_DEV)]

    c_all, cw_all = _a2a_direct([jnp.broadcast_to(c[None], (N_DEV,) + c.shape),
                                 jnp.broadcast_to(conv_w[None], (N_DEV,) + conv_w.shape)], "gather_small")
    c_all = c_all[:, 0]
    conv_full = cw_all.transpose(1, 2, 0, 3).reshape(nl, CONV_K, wcols)
    modp = _ada_fwd(c_all, ada_w, "ada_fwd")
    (modx,) = _a2a_direct([modp.transpose(1, 0, 2)], "mod_exchange")
    mod = (modx.transpose(1, 0, 2).reshape(nl, 6 * d) + ada_b).reshape(nl, 6, 1, d)

    big = (w_in, w_branch_a, w_branch_b, w_out, w_ffn_in, w_ffn_out)
    big_wire = [w.astype(WIRE_DTYPE) for w in big]
    gather_in = lambda i: _AllGatherSide([big_wire[0][i]])
    gather_early = lambda i: _AllGatherSide([big_wire[k][i] for k in (1, 2, 3, 5)])
    gather_late = lambda i: _AllGatherSide([big_wire[4][i]] + ([big_wire[0][i + 1]] if i + 1 < nl else []))
    row_full = lambda g: g.reshape(-1, g.shape[2])
    padded_in = lambda g: _cols_from_blocks(g, in_pieces, lay.width, "w_in_cols")
    w_pads = [padded_in(_run_side(gather_in(0), "ag_first")[0])] + [None] * (nl - 1)
    weights = [None] * nl

    def rows_of(ba_rows, lo):
        return ba_rows[lo:lo + HEADS].reshape(HEADS, nchunk, 1, GDN_CHUNK)

    saved = []
    x_cur, delta, gt_prev = x0, None, None
    for i in range(nl):
        sh1, sc1, gt1, sh2, sc2, gt2 = (mod[i, k] for k in range(6))
        s = dict(gt1=gt1, gt2=gt2, sc1=sc1, sc2=sc2)
        s["x_in"], s["h"] = _resid_norm(x_cur, delta, gt_prev, norm1_g[i][None], sc1, sh1, "norm1_fwd")
        s["proj"], g_a, g_b, g_o, g_fo = _matmul(s["h"], w_pads[i], "nn", "proj_fwd", side=gather_early(i))
        s["b_col"] = spatial_b[i][:, :, None]
        s["ya"] = _mixer_a_fwd(s["proj"], lay.uv, spatial_w[i], s["b_col"], v_norm_g[i][None], "mixer_a_fwd")
        s["qkv_h"] = _conv_fwd(s["proj"], conv_full[i], "conv_fwd")
        ba_rows = _cols_as_rows(s["proj"], lay.ba, "ba_rows")
        s["braw"], s["araw"] = rows_of(ba_rows, 0), rows_of(ba_rows, HEADS)
        s["o"], s["states"], s["t_mats"], g_fi, *g_in = _gdn_fwd(
            s["qkv_h"], s["araw"], s["braw"], a_log[i], dt_bias[i], "gdn_fwd", gather_late(i))
        if g_in:
            w_pads[i + 1] = padded_in(g_in[0])
        weights[i] = (row_full(g_a), row_full(g_b), row_full(g_o),
                      _cols_from_blocks(g_fi, fi_pieces, N_DEV * fi_shard, "w_ffn_in_cols"), row_full(g_fo))
        w_a, w_b, w_o, w_fi, w_fo = weights[i]
        s["yb"] = _gdn_post_fwd(s["o"], s["proj"], lay.z, o_norm_g[i][None], "gdn_post_fwd")
        s["pa"] = _matmul(s["ya"], w_a, "nn", "branch_a_fwd")
        s["pb"] = _matmul(s["yb"], w_b, "nn", "branch_b_fwd")
        s["merged"] = _merge_fwd(s["pa"], s["pb"], s["proj"], lay.gates, "merge_fwd")
        s["mo"] = _matmul(s["merged"], w_o, "nn", "out_fwd")
        s["x1"], s["h2"] = _resid_norm(s["x_in"], s["mo"], gt1, norm2_g[i][None], sc2, sh2, "norm2_fwd")
        s["gu"] = _matmul(s["h2"], w_fi, "nn", "ffn_in_fwd")
        s["a"] = _swiglu_fwd(s["gu"], "swiglu_fwd")
        s["fo"] = _matmul(s["a"], w_fo, "nn", "ffn_out_fwd")
        saved.append(s)
        x_cur, delta, gt_prev = s["x1"], s["fo"], gt2
    dx, d_final_g, loss_tile = _final_loss(x_cur, delta, gt_prev, final_g[None], target, "final_loss")
    loss = lax.psum(loss_tile[0, 0], ("x", "y", "c"))

    big_shapes = [(d, w_in.shape[2]), w_branch_a.shape[1:], w_branch_b.shape[1:], w_out.shape[1:],
                  (d, w_ffn_in.shape[2]), w_ffn_out.shape[1:]]
    accs = [lax.empty((nl, N_DEV // 2) + tuple(sh), WIRE_DTYPE) for sh in big_shapes]
    row_blocks = lambda g: g.reshape(N_DEV, -1, g.shape[1])
    dmod, small = [None] * nl, [None] * nl
    d_conv = [None] * nl
    parts, sums = None, None
    beside_gdn, beside_dw, beside_dx = (0,), (4,), (1, 2, 3, 5)
    rep_parts = [None] * nl
    rep_pack = lambda i: _pack((dmod[i],) + small[i])

    def scatter_side(idx, layer):
        if sums is None:
            return _NoSide
        return _ReduceScatterIciSide([sums[k] for k in idx], [accs[k] for k in idx], layer)

    def scattered_into(accs, idx, new):
        accs = list(accs)
        for k, a in zip(idx, new):
            accs[k] = a
        return accs

    for i in reversed(range(nl)):
        s = saved[i]
        w_a, w_b, w_o, w_fi, w_fo = weights[i]
        dfo, dgt2 = _gate_bwd(dx, s["fo"], s["gt2"], "gate2_bwd")
        g_fo = _matmul(s["a"], dfo, "tn", "ffn_out_dw")
        da = _matmul(dfo, w_fo, "nt", "ffn_out_dx")
        dgu = _swiglu_bwd(s["gu"], da, "swiglu_bwd")
        if parts is None:
            g_fi = _matmul(s["h2"], dgu, "tn", "ffn_in_dw")
        else:
            g_fi, *other = _matmul(s["h2"], dgu, "tn", "ffn_in_dw", side=_ReduceScatterD2dSide(parts))
            sums = [_pair_sum(p, o, core, "rs_pair_sum_%d" % k) for k, (p, o) in enumerate(zip(parts, other))]
        if i + 1 < nl:
            dh2, rep_parts[i + 1] = _matmul(dgu, w_fi, "nt", "ffn_in_dx", side=_AllGatherSide([rep_pack(i + 1)]))
        else:
            dh2 = _matmul(dgu, w_fi, "nt", "ffn_in_dx")
        dx1, dsh2, dsc2, dg2 = _norm_bwd(s["x1"], dh2, dx, norm2_g[i][None], s["sc2"], "norm2_bwd")
        dmo, dgt1 = _gate_bwd(dx1, s["mo"], s["gt1"], "gate1_bwd")
        g_o = _matmul(s["merged"], dmo, "tn", "out_dw")
        dmerged = _matmul(dmo, w_o, "nt", "out_dx")
        dproj = lax.empty((t, lay.width), MXU_DTYPE)
        dpa, dpb, dproj = _merge_bwd(dmerged, s["pa"], s["pb"], s["proj"], lay.gates, dproj, "merge_bwd")
        g_a = _matmul(s["ya"], dpa, "tn", "branch_a_dw")
        dya = _matmul(dpa, w_a, "nt", "branch_a_dx")
        g_b = _matmul(s["yb"], dpb, "tn", "branch_b_dw")
        dyb = _matmul(dpb, w_b, "nt", "branch_b_dx")
        dproj, d_ws, d_bs, d_gv = _mixer_a_bwd(s["proj"], lay.uv, dya, spatial_w[i], jnp.swapaxes(spatial_w[i], 1, 2),
                                               s["b_col"], v_norm_g[i][None], dproj, "mixer_a_bwd")
        do, dproj, d_go = _gdn_post_bwd(s["o"], s["proj"], lay.z, dyb, o_norm_g[i][None], dproj, "gdn_post_bwd")
        dq, dk, dv, d_ar, d_br, d_al, d_dt, *scattered = _gdn_bwd(
            s["qkv_h"], s["araw"], s["braw"], a_log[i], dt_bias[i], s["states"], s["t_mats"], do, "gdn_bwd",
            scatter_side(beside_gdn, i + 1))
        accs = scattered_into(accs, beside_gdn, scattered)
        dacc, d_conv[i] = _conv_bwd_pre(s["proj"], dq, dk, dv, conv_full[i], "conv_bwd_pre")
        dproj = _conv_bwd_in(dacc, conv_full[i], dproj, "conv_bwd_in")
        dba_rows = jnp.pad(jnp.concatenate([d_br.reshape(HEADS, t), d_ar.reshape(HEADS, t)]),
                           ((0, LANES - 2 * HEADS), (0, 0)))
        dproj = _rows_into_cols(dproj, dba_rows, lay.ba, "dproj_ba")
        if sums is None:
            g_pad = _matmul(s["h"], dproj, "tn", "proj_dw")
            dh = _matmul(dproj, w_pads[i], "nt", "proj_dx")
        else:
            g_pad, *scattered = _matmul(s["h"], dproj, "tn", "proj_dw", side=scatter_side(beside_dw, i + 1))
            accs = scattered_into(accs, beside_dw, scattered)
            dh, *scattered = _matmul(dproj, w_pads[i], "nt", "proj_dx", side=scatter_side(beside_dx, i + 1))
            accs = scattered_into(accs, beside_dx, scattered)
        dx, dsh1, dsc1, dg1 = _norm_bwd(s["x_in"], dh, dx1, norm1_g[i][None], s["sc1"], "norm1_bwd")
        dmod[i] = jnp.concatenate([dsh1, dsc1, dgt1, dsh2, dsc2, dgt2], axis=1)[0]
        small[i] = (dg1[0], d_ws, d_bs[:, :, 0], d_gv[0], d_al[:, 0, 0], d_dt[:, 0, 0], d_go[0], dg2[0])
        parts = [_blocks_from_cols(g_pad, in_pieces, w_in.shape[2], "w_in_blocks"), row_blocks(g_a), row_blocks(g_b),
                 row_blocks(g_o), _blocks_from_cols(g_fi, fi_pieces, fi_shard, "w_ffn_in_blocks"), row_blocks(g_fo)]
    other = _run_side(_ReduceScatterD2dSide(parts), "rs_d2d_last")
    sums = [_pair_sum(p, o, core, "rs_pair_sum_%d" % k) for k, (p, o) in enumerate(zip(parts, other))]
    accs = _run_side(_ReduceScatterIciSide(sums, accs, 0), "rs_ici_last")

    rep_w = (ada_b, norm1_g, spatial_w, spatial_b, v_norm_g, a_log, dt_bias, o_norm_g, norm2_g)
    rep_m = (m_ada_b, m_norm1_g, m_spatial_w, m_spatial_b, m_v_norm_g, m_a_log, m_dt_bias, m_o_norm_g, m_norm2_g)
    rep_v = (v_ada_b, v_norm1_g, v_spatial_w, v_spatial_b, v_v_norm_g, v_a_log, v_dt_bias, v_o_norm_g, v_norm2_g)
    rep_parts[0], fin_parts = _run_side(_AllGatherSide([rep_pack(0), _pack([d_final_g[0]])]), "small_grads_last")
    dmod = jnp.stack(dmod)
    d_conv_blocks = jnp.stack(d_conv).reshape(nl, CONV_K, N_DEV, -1).transpose(2, 0, 1, 3).reshape(N_DEV, -1, LANES)
    dmod_blocks = dmod.reshape(nl, N_DEV, -1).transpose(1, 0, 2)
    conv_all, dmod_all = _a2a_direct([d_conv_blocks, dmod_blocks], "small_grads_scatter")
    by_layer = lambda arrs: jnp.stack([_pack([a[i] for a in arrs]) for i in range(nl)])
    rep_out = _sum_adam(jnp.stack(rep_parts), by_layer(rep_w), by_layer(rep_m), by_layer(rep_v), "adam_small")
    fin_out = _sum_adam(fin_parts[None], _pack([final_g])[None], _pack([m_final_g])[None], _pack([v_final_g])[None],
                        "adam_final_g")
    layer_like = [a[0] for a in rep_w]
    rep_out = [[jnp.stack(per_layer) for per_layer in zip(*[_unpack(o[i], layer_like) for i in range(nl)])]
               + _unpack(f[0], [final_g]) for o, f in zip(rep_out, fin_out)]
    conv_out = _sum_adam(conv_all[None], conv_w.reshape(1, -1, LANES), m_conv_w.reshape(1, -1, LANES),
                         v_conv_w.reshape(1, -1, LANES), "adam_conv")
    conv_out = [o.reshape(conv_w.shape) for o in conv_out]
    ada_out = _ada_bwd(c_all[:, :, None], dmod_all.transpose(1, 0, 2), ada_w, m_ada_w, v_ada_w, "ada_bwd_adam")
    big_m = (m_w_in, m_w_branch_a, m_w_branch_b, m_w_out, m_w_ffn_in, m_w_ffn_out)
    big_v = (v_w_in, v_w_branch_a, v_w_branch_b, v_w_out, v_w_ffn_in, v_w_ffn_out)
    big_out = [_sum_adam(accs[k], big[k], big_m[k], big_v[k], "adam_big_%d" % k) for k in range(6)]

    def ordered(kind):
        rep = rep_out[kind]
        return (ada_out[kind], rep[0], rep[1], big_out[0][kind], conv_out[kind], rep[2], rep[3], rep[4], rep[5],
                rep[6], rep[7], big_out[1][kind], big_out[2][kind], big_out[3][kind], rep[8], big_out[4][kind],
                big_out[5][kind], rep[9])

    return (loss, dx[None]) + ordered(0) + ordered(1) + ordered(2) + ordered(3)
```

```python
import functools

import jax
import jax.numpy as jnp
from jax import lax
from jax.experimental import pallas as pl
from jax.experimental.pallas import tpu as pltpu

F32 = jnp.float32
BF16 = jnp.bfloat16
MXU_DTYPE = BF16
WIRE_DTYPE = BF16
EPS = 1e-6
LANES = 128
SUBLANES = 8
GDN_CHUNK = 128
A_CHUNK = 128
GROUPS = 8
HEADS = 8
HEAD_DIM = 128
CONV_K = 4
N_DEV = 8
VMEM_LIMIT = 48 * 1024 * 1024
MESH = pl.DeviceIdType.MESH

ADAM_LR = 0.001
ADAM_B1 = 0.9
ADAM_B2 = 0.999
ADAM_EPS = 1e-08
ADAM_WD = 0.01
ADAM_STEP = 10

_NN = (((1,), (0,)), ((), ()))
_NT = (((1,), (1,)), ((), ()))
_TN = (((0,), (0,)), ((), ()))


def _mm(a, b, dims=_NN):
    return lax.dot_general(a.astype(MXU_DTYPE), b.astype(MXU_DTYPE), dims, preferred_element_type=F32)


def _mm_hi(a, b):
    return lax.dot_general(a, b, _NN, precision=lax.Precision.HIGHEST, preferred_element_type=F32)


def _tile(n, cands):
    for c in cands:
        if n % c == 0:
            return c
    return n


def _params(sem=None):
    return pltpu.CompilerParams(dimension_semantics=sem, vmem_limit_bytes=VMEM_LIMIT)


def _sigmoid(x):
    return 1.0 / (1.0 + jnp.exp(-x))


def _silu(x):
    return x * _sigmoid(x)


def _dsilu(x):
    s = _sigmoid(x)
    return s * (1.0 + x * (1.0 - s))


_GELU_C = 0.7978845608028654
_GELU_A = 0.044715


def _gelu(x):
    return 0.5 * x * (1.0 + jnp.tanh(_GELU_C * (x + _GELU_A * x * x * x)))


def _gelu_and_slope(x):
    t = jnp.tanh(_GELU_C * (x + _GELU_A * x * x * x))
    return 0.5 * x * (1.0 + t), 0.5 * (1.0 + t) + 0.5 * x * (1.0 - t * t) * _GELU_C * (1.0 + 3.0 * _GELU_A * x * x)


def _softplus(x):
    return jnp.maximum(x, 0.0) + jnp.log(1.0 + jnp.exp(-jnp.abs(x)))


_MM_TILES = (1024, 1408, 1664, 512, 256, 128)


class _NoSide:
    operands, out_shape, scratch, aliases, n_in, n_out = [], [], [], {}, 0, 0


def _side_hooks(side, refs, n_main_in, n_main_out, n_main_scratch, grid):
    a = n_main_in + side.n_in
    b = a + n_main_out + side.n_out
    ins, outs, sems = refs[n_main_in:a], refs[a + n_main_out:b], refs[b + n_main_scratch:]
    main = refs[:n_main_in] + refs[a:a + n_main_out] + refs[b:b + n_main_scratch]
    ids = [pl.program_id(k) for k in range(len(grid))]

    def start():
        if side.n_in:
            pl.when(functools.reduce(jnp.logical_and, [i == 0 for i in ids]))(lambda: side.start(ins, outs, sems))

    def finish():
        if side.n_in:
            last = functools.reduce(jnp.logical_and, [i == g - 1 for i, g in zip(ids, grid)])
            pl.when(last)(lambda: side.finish(ins, outs, sems))

    return main, start, finish


def _carrier_call(body, name, grid, in_specs, out_specs, out_shape, scratch, side, args):
    aliases = {len(in_specs) + k: len(out_specs) + v for k, v in side.aliases.items()}
    return pl.pallas_call(
        body, name=name, grid=grid, in_specs=list(in_specs) + [_ANY] * side.n_in,
        out_specs=list(out_specs) + [_ANY] * side.n_out, out_shape=list(out_shape) + list(side.out_shape),
        scratch_shapes=list(scratch) + list(side.scratch), input_output_aliases=aliases,
        compiler_params=_params(("arbitrary",) * len(grid)))(*args, *side.operands)


def _matmul(a, b, mode, name, out_dtype=F32, side=_NoSide):
    if mode == "nn":
        (m, k), n = a.shape, b.shape[1]
    elif mode == "nt":
        (m, k), n = a.shape, b.shape[0]
    else:
        (k, m), n = a.shape, b.shape[1]
    tm, tn, tk = _tile(m, _MM_TILES), _tile(n, _MM_TILES), _tile(k, _MM_TILES)
    nk = k // tk
    grid = (m // tm, n // tn, nk)
    dims = {"nn": _NN, "nt": _NT, "tn": _TN}[mode]

    def body(*refs):
        (a_ref, b_ref, o_ref, acc_ref), side_start, side_finish = _side_hooks(side, refs, 2, 1, 1, grid)
        kk = pl.program_id(2)
        side_start()
        if nk == 1:
            o_ref[...] = _mm(a_ref[...], b_ref[...], dims).astype(o_ref.dtype)
        else:
            @pl.when(kk == 0)
            def _():
                acc_ref[...] = _mm(a_ref[...], b_ref[...], dims)

            @pl.when(jnp.logical_and(kk > 0, kk < nk - 1))
            def _():
                acc_ref[...] += _mm(a_ref[...], b_ref[...], dims)

            @pl.when(kk == nk - 1)
            def _():
                o_ref[...] = (acc_ref[...] + _mm(a_ref[...], b_ref[...], dims)).astype(o_ref.dtype)

        side_finish()

    a_spec = (pl.BlockSpec((tk, tm), lambda i, j, l: (l, i)) if mode == "tn"
              else pl.BlockSpec((tm, tk), lambda i, j, l: (i, l)))
    b_spec = (pl.BlockSpec((tn, tk), lambda i, j, l: (j, l)) if mode == "nt"
              else pl.BlockSpec((tk, tn), lambda i, j, l: (l, j)))
    o_spec = pl.BlockSpec((tm, tn), lambda i, j, l: (i, j))
    out = _carrier_call(body, name, grid, [a_spec, b_spec], [o_spec], [jax.ShapeDtypeStruct((m, n), out_dtype)],
                        [pltpu.VMEM((tm, tn) if nk > 1 else (SUBLANES, LANES), F32)], side, (a, b))
    return out if side.n_in else out[0]


_ROW_TILES = (512, 256, 128)


def _resid_norm(x, delta, gt, g, sc, sh, name):
    t, d = x.shape
    tt = _tile(t, _ROW_TILES)
    has = delta is not None

    def body(*refs):
        if has:
            x_ref, d_ref, gt_ref, g_ref, sc_ref, sh_ref, xo_ref, h_ref = refs
            xv = x_ref[...] + gt_ref[...] * d_ref[...]
            xo_ref[...] = xv
        else:
            x_ref, g_ref, sc_ref, sh_ref, h_ref = refs
            xv = x_ref[...]
        r = lax.rsqrt(jnp.mean(xv * xv, axis=-1, keepdims=True) + EPS)
        y = xv * r * g_ref[...]
        h_ref[...] = (y * (1.0 + sc_ref[...]) + sh_ref[...]).astype(h_ref.dtype)

    row = pl.BlockSpec((tt, d), lambda i: (i, 0))
    vec = pl.BlockSpec((1, d), lambda i: (0, 0))
    if has:
        return pl.pallas_call(
            body, name=name, grid=(t // tt,), in_specs=[row, row, vec, vec, vec, vec], out_specs=[row, row],
            out_shape=[jax.ShapeDtypeStruct((t, d), F32), jax.ShapeDtypeStruct((t, d), MXU_DTYPE)],
            compiler_params=_params(("parallel",)))(x, delta, gt, g, sc, sh)
    h = pl.pallas_call(
        body, name=name + "_first", grid=(t // tt,), in_specs=[row, vec, vec, vec], out_specs=row,
        out_shape=jax.ShapeDtypeStruct((t, d), MXU_DTYPE), compiler_params=_params(("parallel",)))(x, g, sc, sh)
    return x, h


def _final_loss(x, delta, gt, g, target, name):
    t, d = x.shape
    tt = _tile(t, _ROW_TILES)

    def body(x_ref, d_ref, gt_ref, g_ref, tg_ref, dx_ref, dg_ref, loss_ref):
        @pl.when(pl.program_id(0) == 0)
        def _():
            dg_ref[...] = jnp.zeros_like(dg_ref)
            loss_ref[...] = jnp.zeros_like(loss_ref)

        xv = x_ref[...] + gt_ref[...] * d_ref[...]
        r = lax.rsqrt(jnp.mean(xv * xv, axis=-1, keepdims=True) + EPS)
        xh = xv * r
        diff = xh * g_ref[...] - tg_ref[...]
        loss_ref[...] += jnp.sum(diff * diff) * (0.5 / d)
        dy = diff * (1.0 / d)
        dg_ref[...] += jnp.sum(dy * xh, axis=0, keepdims=True)
        dxh = dy * g_ref[...]
        dx_ref[...] = r * (dxh - xh * jnp.mean(dxh * xh, axis=-1, keepdims=True))

    row = pl.BlockSpec((tt, d), lambda i: (i, 0))
    vec = pl.BlockSpec((1, d), lambda i: (0, 0))
    tile = pl.BlockSpec((SUBLANES, LANES), lambda i: (0, 0))
    return pl.pallas_call(
        body, name=name, grid=(t // tt,), in_specs=[row, row, vec, vec, row], out_specs=[row, vec, tile],
        out_shape=[jax.ShapeDtypeStruct((t, d), F32), jax.ShapeDtypeStruct((1, d), F32),
                   jax.ShapeDtypeStruct((SUBLANES, LANES), F32)],
        compiler_params=_params(("arbitrary",)))(x, delta, gt, g, target)


def _norm_bwd(x, dh, dres, g, sc, name):
    t, d = x.shape
    tt = _tile(t, _ROW_TILES)

    def body(x_ref, dh_ref, dr_ref, g_ref, sc_ref, dx_ref, dsh_ref, dsc_ref, dg_ref):
        @pl.when(pl.program_id(0) == 0)
        def _():
            dsh_ref[...] = jnp.zeros_like(dsh_ref)
            dsc_ref[...] = jnp.zeros_like(dsc_ref)
            dg_ref[...] = jnp.zeros_like(dg_ref)

        xv, dh = x_ref[...], dh_ref[...]
        r = lax.rsqrt(jnp.mean(xv * xv, axis=-1, keepdims=True) + EPS)
        xh = xv * r
        gv, sc1 = g_ref[...], 1.0 + sc_ref[...]
        dsh_ref[...] += jnp.sum(dh, axis=0, keepdims=True)
        dsc_ref[...] += jnp.sum(dh * xh, axis=0, keepdims=True) * gv
        dg_ref[...] += jnp.sum(dh * xh, axis=0, keepdims=True) * sc1
        dxh = dh * (gv * sc1)
        dx_ref[...] = dr_ref[...] + r * (dxh - xh * jnp.mean(dxh * xh, axis=-1, keepdims=True))

    row = pl.BlockSpec((tt, d), lambda i: (i, 0))
    vec = pl.BlockSpec((1, d), lambda i: (0, 0))
    vshape = jax.ShapeDtypeStruct((1, d), F32)
    return pl.pallas_call(
        body, name=name, grid=(t // tt,), in_specs=[row, row, row, vec, vec], out_specs=[row, vec, vec, vec],
        out_shape=[jax.ShapeDtypeStruct((t, d), F32), vshape, vshape, vshape],
        compiler_params=_params(("arbitrary",)))(x, dh, dres, g, sc)


def _gate_bwd(dxo, branch, gt, name):
    t, d = dxo.shape
    tt = _tile(t, _ROW_TILES)

    def body(dx_ref, br_ref, gt_ref, db_ref, dgt_ref):
        @pl.when(pl.program_id(0) == 0)
        def _():
            dgt_ref[...] = jnp.zeros_like(dgt_ref)

        dx = dx_ref[...]
        db_ref[...] = (dx * gt_ref[...]).astype(db_ref.dtype)
        dgt_ref[...] += jnp.sum(dx * br_ref[...], axis=0, keepdims=True)

    row = pl.BlockSpec((tt, d), lambda i: (i, 0))
    vec = pl.BlockSpec((1, d), lambda i: (0, 0))
    return pl.pallas_call(
        body, name=name, grid=(t // tt,), in_specs=[row, row, vec], out_specs=[row, vec],
        out_shape=[jax.ShapeDtypeStruct((t, d), MXU_DTYPE), jax.ShapeDtypeStruct((1, d), F32)],
        compiler_params=_params(("arbitrary",)))(dxo, branch, gt)


def _swiglu_fwd(gu, name):
    t, f2 = gu.shape
    f = f2 // 2
    tt = _tile(t, (256, 128))

    def body(g_ref, u_ref, o_ref):
        o_ref[...] = (_silu(g_ref[...]) * u_ref[...]).astype(o_ref.dtype)

    return pl.pallas_call(
        body, name=name, grid=(t // tt,),
        in_specs=[pl.BlockSpec((tt, f), lambda i: (i, 0)), pl.BlockSpec((tt, f), lambda i: (i, 1))],
        out_specs=pl.BlockSpec((tt, f), lambda i: (i, 0)), out_shape=jax.ShapeDtypeStruct((t, f), MXU_DTYPE),
        compiler_params=_params(("parallel",)))(gu, gu)


def _swiglu_bwd(gu, da, name):
    t, f2 = gu.shape
    f = f2 // 2
    tt = _tile(t, (256, 128))

    def body(g_ref, u_ref, da_ref, o_ref):
        gate, da = g_ref[...], da_ref[...]
        sg = _sigmoid(gate)
        o_ref[:, :f] = (da * u_ref[...] * (sg * (1.0 + gate * (1.0 - sg)))).astype(o_ref.dtype)
        o_ref[:, f:] = (da * (gate * sg)).astype(o_ref.dtype)

    return pl.pallas_call(
        body, name=name, grid=(t // tt,),
        in_specs=[pl.BlockSpec((tt, f), lambda i: (i, 0)), pl.BlockSpec((tt, f), lambda i: (i, 1)),
                  pl.BlockSpec((tt, f), lambda i: (i, 0))],
        out_specs=pl.BlockSpec((tt, f2), lambda i: (i, 0)), out_shape=jax.ShapeDtypeStruct((t, f2), MXU_DTYPE),
        compiler_params=_params(("parallel",)))(gu, gu, da)


class _ProjLayout:
    def __init__(self, d):
        wc = 3 * HEADS * HEAD_DIM
        self.d, self.wc = d, wc
        self.qkv, self.z, self.uv, self.gates, self.ba = 0, wc, wc + d, wc + 3 * d, wc + 5 * d
        self.width = self.ba + LANES
        assert self.z % d == 0 and self.uv % (2 * d) == 0 and self.gates % (2 * d) == 0 and self.ba % LANES == 0

    def pieces(self, shard):
        d, wc, out, lo = self.d, self.wc, [], 0
        for length, dst in ((2 * d, self.uv), (wc, self.qkv), (d, self.z), (2 * HEADS, self.ba), (2 * d, self.gates)):
            pos = lo
            while pos < lo + length:
                j = pos // shard
                n = min(lo + length, (j + 1) * shard) - pos
                out.append((j, pos - j * shard, n, dst + pos - lo))
                pos += n
            lo += length
        return out


def _merge_fwd(pa, pb, proj, gcol, name):
    t, d = pa.shape
    tt = _tile(t, _ROW_TILES)

    def body(pa_ref, pb_ref, ga_ref, gb_ref, o_ref):
        o_ref[...] = (_sigmoid(ga_ref[...]) * pa_ref[...] + _sigmoid(gb_ref[...]) * pb_ref[...]).astype(o_ref.dtype)

    row = pl.BlockSpec((tt, d), lambda i: (i, 0))
    gate = lambda k: pl.BlockSpec((tt, d), lambda i: (i, gcol // d + k))
    return pl.pallas_call(
        body, name=name, grid=(t // tt,), in_specs=[row, row, gate(0), gate(1)], out_specs=row,
        out_shape=jax.ShapeDtypeStruct((t, d), MXU_DTYPE), compiler_params=_params(("parallel",)))(pa, pb, proj, proj)


def _merge_bwd(dm, pa, pb, proj, gcol, dproj, name):
    t, d = pa.shape
    tt = _tile(t, _ROW_TILES)

    def body(dm_ref, pa_ref, pb_ref, ga_ref, gb_ref, _, dpa_ref, dpb_ref, dg_ref):
        dm = dm_ref[...]
        sa, sb = _sigmoid(ga_ref[...]), _sigmoid(gb_ref[...])
        dpa_ref[...] = (dm * sa).astype(dpa_ref.dtype)
        dpb_ref[...] = (dm * sb).astype(dpb_ref.dtype)
        dg_ref[:, :d] = (dm * pa_ref[...] * sa * (1.0 - sa)).astype(dg_ref.dtype)
        dg_ref[:, d:] = (dm * pb_ref[...] * sb * (1.0 - sb)).astype(dg_ref.dtype)

    row = pl.BlockSpec((tt, d), lambda i: (i, 0))
    gate = lambda k: pl.BlockSpec((tt, d), lambda i: (i, gcol // d + k))
    wide = pl.BlockSpec((tt, 2 * d), lambda i: (i, gcol // (2 * d)))
    return pl.pallas_call(
        body, name=name, grid=(t // tt,), in_specs=[row, row, row, gate(0), gate(1), _ANY], out_specs=[row, row, wide],
        out_shape=[jax.ShapeDtypeStruct((t, d), MXU_DTYPE), jax.ShapeDtypeStruct((t, d), MXU_DTYPE),
                   jax.ShapeDtypeStruct(dproj.shape, dproj.dtype)],
        input_output_aliases={5: 2}, compiler_params=_params(("parallel",)))(dm, pa, pb, proj, proj, dproj)


def _tri_masks(n):
    ri = lax.broadcasted_iota(jnp.int32, (n, n), 0)
    ci = lax.broadcasted_iota(jnp.int32, (n, n), 1)
    return ri >= ci, ri > ci, ri == ci


def _mixer_a_fwd(proj, ucol, w_s, b_col, g_v, name):
    t, w = proj.shape[0], g_v.shape[1]
    c = A_CHUNK

    def body(u_ref, v_ref, w_ref, b_ref, gv_ref, y_ref):
        tril, _, _ = _tri_masks(c)
        ug, vg = _gelu(u_ref[...]), _gelu(v_ref[...])
        for g in range(GROUPS):
            sl = slice(g * c, (g + 1) * c)
            vt = vg[:, sl]
            r = lax.rsqrt(jnp.mean(vt * vt, axis=-1, keepdims=True) + EPS)
            vn = vt * r * gv_ref[:, sl]
            s = _mm(jnp.where(tril, w_ref[g], 0.0), vn) + b_ref[g]
            y_ref[:, sl] = (ug[:, sl] * s).astype(y_ref.dtype)

    return pl.pallas_call(
        body, name=name, grid=(t // c,),
        in_specs=[pl.BlockSpec((c, w), lambda i: (i, ucol // w)), pl.BlockSpec((c, w), lambda i: (i, ucol // w + 1)),
                  pl.BlockSpec((GROUPS, c, c), lambda i: (0, 0, 0)), pl.BlockSpec((GROUPS, c, 1), lambda i: (0, 0, 0)),
                  pl.BlockSpec((1, w), lambda i: (0, 0))],
        out_specs=pl.BlockSpec((c, w), lambda i: (i, 0)), out_shape=jax.ShapeDtypeStruct((t, w), MXU_DTYPE),
        compiler_params=_params(("parallel",)))(proj, proj, w_s, b_col, g_v)


def _mixer_a_bwd(proj, ucol, dy, w_s, w_st, b_col, g_v, dproj, name):
    t, w = proj.shape[0], g_v.shape[1]
    w2 = 2 * w
    c = A_CHUNK

    def body(u_ref, v_ref, dy_ref, w_ref, wt_ref, b_ref, gv_ref, _, duv_ref, dw_ref, db_ref, dgv_ref):
        @pl.when(pl.program_id(0) == 0)
        def _():
            dw_ref[...] = jnp.zeros_like(dw_ref)
            db_ref[...] = jnp.zeros_like(db_ref)
            dgv_ref[...] = jnp.zeros_like(dgv_ref)

        tril, _, _ = _tri_masks(c)
        triu = lax.broadcasted_iota(jnp.int32, (c, c), 0) <= lax.broadcasted_iota(jnp.int32, (c, c), 1)
        (ug, dug), (vg, dvg) = _gelu_and_slope(u_ref[...]), _gelu_and_slope(v_ref[...])
        for g in range(GROUPS):
            sl = slice(g * c, (g + 1) * c)
            vt = vg[:, sl]
            r = lax.rsqrt(jnp.mean(vt * vt, axis=-1, keepdims=True) + EPS)
            vh = vt * r
            gv = gv_ref[:, sl]
            vn = vh * gv
            s = _mm(jnp.where(tril, w_ref[g], 0.0), vn) + b_ref[g]
            dy = dy_ref[:, sl]
            ds = dy * ug[:, sl]
            dw_ref[g] += jnp.where(tril, _mm(ds, vn, _NT), 0.0)
            db_ref[g] += jnp.sum(ds, axis=1, keepdims=True)
            dvn = _mm(jnp.where(triu, wt_ref[g], 0.0), ds)
            dgv_ref[:, sl] += jnp.sum(dvn * vh, axis=0, keepdims=True)
            dvh = dvn * gv
            dvt = r * (dvh - vh * jnp.mean(dvh * vh, axis=-1, keepdims=True))
            duv_ref[:, sl] = (dy * s * dug[:, sl]).astype(duv_ref.dtype)
            duv_ref[:, w + g * c:w + (g + 1) * c] = (dvt * dvg[:, sl]).astype(duv_ref.dtype)

    full3 = lambda shape: pl.BlockSpec(shape, lambda i: (0, 0, 0))
    return pl.pallas_call(
        body, name=name, grid=(t // c,),
        in_specs=[pl.BlockSpec((c, w), lambda i: (i, ucol // w)), pl.BlockSpec((c, w), lambda i: (i, ucol // w + 1)),
                  pl.BlockSpec((c, w), lambda i: (i, 0)), full3((GROUPS, c, c)), full3((GROUPS, c, c)),
                  full3((GROUPS, c, 1)), pl.BlockSpec((1, w), lambda i: (0, 0)), _ANY],
        out_specs=[pl.BlockSpec((c, w2), lambda i: (i, ucol // w2)), full3((GROUPS, c, c)), full3((GROUPS, c, 1)),
                   pl.BlockSpec((1, w), lambda i: (0, 0))],
        out_shape=[jax.ShapeDtypeStruct(dproj.shape, dproj.dtype), jax.ShapeDtypeStruct((GROUPS, c, c), F32),
                   jax.ShapeDtypeStruct((GROUPS, c, 1), F32), jax.ShapeDtypeStruct((1, w), F32)],
        input_output_aliases={7: 0},
        compiler_params=_params(("arbitrary",)))(proj, proj, dy, w_s, w_st, b_col, g_v, dproj)


_Q_SCALE = HEAD_DIM ** -0.5


def _conv_taps(ext, w_ref):
    shifted = [ext[SUBLANES:]] + [pltpu.roll(ext, s, 0)[SUBLANES:] for s in range(1, CONV_K)]
    acc = shifted[0] * w_ref[pl.ds(CONV_K - 1, 1), :]
    for s in range(1, CONV_K):
        acc = acc + shifted[s] * w_ref[pl.ds(CONV_K - 1 - s, 1), :]
    return acc, shifted


def _conv_fwd(qkv, w, name):
    t, cw = qkv.shape[0], w.shape[1]
    tt = _tile(t, (256, 128))
    hb = tt // SUBLANES

    def body(x_ref, p_ref, w_ref, o_ref):
        prev = jnp.where(pl.program_id(0) > 0, p_ref[...], 0.0)
        acc, _ = _conv_taps(jnp.concatenate([prev, x_ref[...]], axis=0), w_ref)
        y = _silu(acc)
        for which in range(3):
            for h in range(HEADS):
                lo = (which * HEADS + h) * HEAD_DIM
                seg = y[:, lo:lo + HEAD_DIM]
                if which < 2:
                    seg = seg * lax.rsqrt(jnp.sum(seg * seg, axis=-1, keepdims=True) + EPS)
                if which == 0:
                    seg = seg * _Q_SCALE
                o_ref[which, h] = seg

    return pl.pallas_call(
        body, name=name, grid=(t // tt,),
        in_specs=[pl.BlockSpec((tt, cw), lambda i: (i, 0)),
                  pl.BlockSpec((SUBLANES, cw), lambda i: (jnp.maximum(i * hb - 1, 0), 0)),
                  pl.BlockSpec((CONV_K, cw), lambda i: (0, 0))],
        out_specs=pl.BlockSpec((3, HEADS, tt, HEAD_DIM), lambda i: (0, 0, i, 0)),
        out_shape=jax.ShapeDtypeStruct((3, HEADS, t, HEAD_DIM), F32),
        compiler_params=_params(("parallel",)))(qkv, qkv, w)


def _conv_bwd_pre(qkv, dq, dk, dv, w, name):
    t, cw = qkv.shape[0], w.shape[1]
    tt = _tile(t, (256, 128))
    hb = tt // SUBLANES

    def body(x_ref, p_ref, dq_ref, dk_ref, dv_ref, w_ref, da_ref, dw_ref):
        @pl.when(pl.program_id(0) == 0)
        def _():
            dw_ref[...] = jnp.zeros_like(dw_ref)

        prev = jnp.where(pl.program_id(0) > 0, p_ref[...], 0.0)
        acc, shifted = _conv_taps(jnp.concatenate([prev, x_ref[...]], axis=0), w_ref)
        sg = _sigmoid(acc)
        y = acc * sg
        dsilu = sg * (1.0 + acc * (1.0 - sg))
        d_refs = (dq_ref, dk_ref, dv_ref)
        for which in range(3):
            for h in range(HEADS):
                lo = (which * HEADS + h) * HEAD_DIM
                sl = slice(lo, lo + HEAD_DIM)
                dn = d_refs[which][h]
                if which < 2:
                    seg = y[:, sl]
                    rho = lax.rsqrt(jnp.sum(seg * seg, axis=-1, keepdims=True) + EPS)
                    nrm = seg * rho
                    if which == 0:
                        dn = dn * _Q_SCALE
                    dn = rho * (dn - nrm * jnp.sum(dn * nrm, axis=-1, keepdims=True))
                dacc = dn * dsilu[:, sl]
                da_ref[:, sl] = dacc
                for s in range(CONV_K):
                    dw_ref[pl.ds(CONV_K - 1 - s, 1), sl] += jnp.sum(dacc * shifted[s][:, sl], axis=0, keepdims=True)

    head = pl.BlockSpec((HEADS, tt, HEAD_DIM), lambda i: (0, i, 0))
    return pl.pallas_call(
        body, name=name, grid=(t // tt,),
        in_specs=[pl.BlockSpec((tt, cw), lambda i: (i, 0)),
                  pl.BlockSpec((SUBLANES, cw), lambda i: (jnp.maximum(i * hb - 1, 0), 0)),
                  head, head, head, pl.BlockSpec((CONV_K, cw), lambda i: (0, 0))],
        out_specs=[pl.BlockSpec((tt, cw), lambda i: (i, 0)), pl.BlockSpec((CONV_K, cw), lambda i: (0, 0))],
        out_shape=[jax.ShapeDtypeStruct((t, cw), F32), jax.ShapeDtypeStruct((CONV_K, cw), F32)],
        compiler_params=_params(("arbitrary",)))(qkv, qkv, dq, dk, dv, w)


def _conv_bwd_in(dacc, w, dproj, name):
    t, cw = dacc.shape
    tt = _tile(t, (256, 128))
    hb = tt // SUBLANES
    nt = t // tt
    rows = tt + SUBLANES

    def body(d_ref, n_ref, w_ref, _, o_ref):
        cur = d_ref[...]
        nxt = jnp.where(pl.program_id(0) < nt - 1, n_ref[...], 0.0)
        ext = jnp.concatenate([cur, nxt], axis=0)
        acc = cur * w_ref[pl.ds(CONV_K - 1, 1), :]
        for s in range(1, CONV_K):
            acc = acc + pltpu.roll(ext, rows - s, 0)[:tt] * w_ref[pl.ds(CONV_K - 1 - s, 1), :]
        o_ref[...] = acc.astype(o_ref.dtype)

    return pl.pallas_call(
        body, name=name, grid=(nt,),
        in_specs=[pl.BlockSpec((tt, cw), lambda i: (i, 0)),
                  pl.BlockSpec((SUBLANES, cw), lambda i: (jnp.minimum((i + 1) * hb, t // SUBLANES - 1), 0)),
                  pl.BlockSpec((CONV_K, cw), lambda i: (0, 0)), _ANY],
        out_specs=pl.BlockSpec((tt, cw), lambda i: (i, 0)), out_shape=jax.ShapeDtypeStruct(dproj.shape, dproj.dtype),
        input_output_aliases={3: 0}, compiler_params=_params(("parallel",)))(dacc, dacc, w, dproj)


_INV_BASE_SHIFT = 3


def _inv_unit_lower(a, eye):
    c = GDN_CHUNK
    ri = lax.broadcasted_iota(jnp.int32, (c, c), 0)
    ci = lax.broadcasted_iota(jnp.int32, (c, c), 1)
    same = lambda sh: (ri >> sh) == (ci >> sh)
    x = jnp.where(same(_INV_BASE_SHIFT), -a, 0.0)
    p = jnp.where(eye, 1.0, 0.0) + x
    xs = _split(x)
    x2 = _mm3(xs, xs)
    x2s, ps = _split(x2), _split(p)
    r = _mm3(x2s, tuple(jnp.concatenate([u, v], axis=-1) for u, v in zip(x2s, ps)))
    x4, p = r[..., :c], p + r[..., c:]
    p = p + _mm3(_split(x4), _split(p))
    for sh in range(_INV_BASE_SHIFT, c.bit_length() - 1):
        off = jnp.where(same(sh + 1) & jnp.logical_not(same(sh)), a, 0.0)
        ps = _split(p)
        p = p - _mm3(ps, _split(_mm3(_split(off), ps)))
    return p


def _split(a):
    hi = a.astype(BF16)
    return hi, (a - hi.astype(F32)).astype(BF16)


def _dot_heads(u, v, dims):
    if u.ndim == 3:
        return jnp.stack([_dot_heads(u[j], v[j], dims) for j in range(u.shape[0])])
    return lax.dot_general(u, v, dims, preferred_element_type=F32)


def _mm3(a, b):
    return _dot_heads(a[0], b[0], _NN) + (_dot_heads(a[0], b[1], _NN) + _dot_heads(a[1], b[0], _NN))


def _hmm(a, b, dims=_NN):
    return _dot_heads(a.astype(MXU_DTYPE), b.astype(MXU_DTYPE), dims)


def _rowsum(x):
    return jnp.sum(x, axis=-1, keepdims=True)


def _colsum(x):
    return jnp.sum(x, axis=-2, keepdims=True)


class _Pre:
    pass


def _gdn_pre(q, k, v, araw, braw, alog, dtb, t_mat=None):
    c = GDN_CHUNK
    p = _Pre()
    p.tril, p.strict, p.eye = _tri_masks(c)
    p.to_col = lambda row: _rowsum(jnp.where(p.eye, row, 0.0))
    p.to_row = lambda col: _colsum(jnp.where(p.eye, col, 0.0))
    p.a_neg = -jnp.exp(alog + jnp.zeros((1, c), F32))
    p.xg = araw + dtb
    p.g_row = p.a_neg * _softplus(p.xg)
    p.beta_row = _sigmoid(braw)
    p.beta = p.to_col(p.beta_row)
    gam = _rowsum(jnp.where(p.tril, p.g_row, 0.0))
    gam_last = _rowsum(p.g_row)
    p.dm = jnp.where(p.tril, jnp.exp(jnp.where(p.tril, gam - p.to_row(gam), 0.0)), 0.0)
    p.e, p.ek, p.el = jnp.exp(gam), jnp.exp(gam_last - gam), jnp.exp(gam_last)
    p.kb = k * p.beta
    p.kk = _hmm(p.kb, k, _NT)
    p.t = _inv_unit_lower(jnp.where(p.strict, p.kk * p.dm, 0.0), p.eye) if t_mat is None else t_mat
    p.vb, p.kbe = v * p.beta, p.kb * p.e
    uw = _hmm(p.t, jnp.concatenate([p.vb, p.kbe], axis=-1))
    p.u, p.w = uw[..., :v.shape[-1]], uw[..., v.shape[-1]:]
    p.qk0 = _hmm(q, k, _NT)
    p.qk = p.qk0 * p.dm
    p.qd, p.kd = q * p.e, k * p.ek
    return p


GDN_HEADS_PER_STEP = 8


def _head_scalars(ref, hb):
    h0 = pl.program_id(0) * hb
    return jnp.stack([jnp.full((1, 1), ref[h0 + j], F32) for j in range(hb)])


def _gdn_specs(n, reverse):
    c, dk, hb = GDN_CHUNK, HEAD_DIM, GDN_HEADS_PER_STEP
    ix = (lambda i: n - 1 - i) if reverse else (lambda i: i)
    smem = pl.BlockSpec(memory_space=pltpu.SMEM)
    qkv = [pl.BlockSpec((None, hb, c, dk), functools.partial(lambda w, h, i: (w, h, ix(i), 0), w)) for w in range(3)]
    row = pl.BlockSpec((hb, None, 1, c), lambda h, i: (h, ix(i), 0, 0))
    tok = pl.BlockSpec((hb, c, dk), lambda h, i: (h, ix(i), 0))
    state = pl.BlockSpec((hb, None, dk, dk), lambda h, i: (h, ix(i), 0, 0))
    return smem, qkv, row, tok, state


def _gdn_fwd(qkv_h, araw, braw, alog, dtb, name, side=_NoSide):
    _, hh, t, dk = qkv_h.shape
    n, hb = t // GDN_CHUNK, GDN_HEADS_PER_STEP
    smem, qkv, row, tok, state = _gdn_specs(n, False)
    grid = (hh // hb, n)

    def body(*refs):
        main, side_start, side_finish = _side_hooks(side, refs, 7, 3, 1, grid)
        alog_ref, dt_ref, q_ref, k_ref, v_ref, a_ref, b_ref, o_ref, so_ref, to_ref, s_ref = main
        side_start()

        @pl.when(pl.program_id(1) == 0)
        def _():
            s_ref[...] = jnp.zeros_like(s_ref)

        p = _gdn_pre(q_ref[...], k_ref[...], v_ref[...], a_ref[...], b_ref[...],
                     _head_scalars(alog_ref, hb), _head_scalars(dt_ref, hb))
        s = s_ref[...]
        vn = p.u - _hmm(p.w, s)
        o_ref[...] = _hmm(p.qd, s) + _hmm(p.qk, vn)
        so_ref[...] = s
        to_ref[...] = p.t
        s_ref[...] = s * p.el + _hmm(p.kd, vn, _TN)
        side_finish()

    mats = jax.ShapeDtypeStruct((hh, n, dk, dk), F32)
    return _carrier_call(
        body, name, grid, [smem, smem] + qkv + [row, row], [tok, state, state],
        [jax.ShapeDtypeStruct((hh, t, dk), F32), mats, mats],
        [pltpu.VMEM((hb, dk, dk), F32)], side, (alog, dtb, qkv_h, qkv_h, qkv_h, araw, braw))


def _gdn_bwd(qkv_h, araw, braw, alog, dtb, states, t_mats, do, name, side=_NoSide):
    _, hh, t, dk = qkv_h.shape
    c, hb = GDN_CHUNK, GDN_HEADS_PER_STEP
    n = t // c
    smem, qkv, row, tok, state = _gdn_specs(n, True)
    acc = pl.BlockSpec((hb, 1, LANES), lambda h, i: (h, 0, 0))
    grid = (hh // hb, n)

    def body(*refs):
        main, side_start, side_finish = _side_hooks(side, refs, 10, 7, 1, grid)
        (alog_ref, dt_ref, q_ref, k_ref, v_ref, a_ref, b_ref, s_ref, t_ref, do_ref,
         dq_ref, dk_ref, dv_ref, da_ref, db_ref, dal_ref, ddt_ref, ds_ref) = main
        side_start()

        @pl.when(pl.program_id(1) == 0)
        def _():
            ds_ref[...] = jnp.zeros_like(ds_ref)
            dal_ref[...] = jnp.zeros_like(dal_ref)
            ddt_ref[...] = jnp.zeros_like(ddt_ref)

        q, k, v = q_ref[...], k_ref[...], v_ref[...]
        p = _gdn_pre(q, k, v, a_ref[...], b_ref[...], _head_scalars(alog_ref, hb), _head_scalars(dt_ref, hb),
                     t_ref[...])
        s, do, dsp = s_ref[...], do_ref[...], ds_ref[...]
        vn = p.u - _hmm(p.w, s)
        dqd = _hmm(do, s, _NT)
        dqk = _hmm(do, vn, _NT)
        dvn = _hmm(p.qk, do, _TN) + _hmm(p.kd, dsp)
        dkd = _hmm(vn, dsp, _NT)
        d_el = _colsum(_rowsum(s * dsp))
        ds_ref[...] = dsp * p.el + _hmm(p.qd, do, _TN) - _hmm(p.w, dvn, _TN)
        dw = -_hmm(dvn, s, _NT)
        d_t = _hmm(dvn, p.vb, _NT) + _hmm(dw, p.kbe, _NT)
        dvb, dkbe = _hmm(p.t, dvn, _TN), _hmm(p.t, dw, _TN)
        d_a = jnp.where(p.strict, -_hmm(p.t, _hmm(d_t, p.t, _NT), _TN), 0.0)
        dkk = d_a * p.dm
        dqk0 = dqk * p.dm
        ddm = d_a * p.kk + dqk * p.qk0
        dkb = _hmm(dkk, k) + dkbe * p.e
        dq_ref[...] = _hmm(dqk0, k) + dqd * p.e
        dk_ref[...] = _hmm(dkk, p.kb, _TN) + _hmm(dqk0, q, _TN) + dkd * p.ek + dkb * p.beta
        dv_ref[...] = dvb * p.beta
        dbeta = _rowsum(dkb * k) + _rowsum(dvb * v)
        d_e = _rowsum(dqd * q) + _rowsum(dkbe * p.kb)
        d_ek = _rowsum(dkd * k)
        m = ddm * p.dm
        dgam = d_e * p.e - d_ek * p.ek + _rowsum(m) - p.to_col(_colsum(m))
        dgam_last = _colsum(d_ek * p.ek) + d_el * p.el
        dg_row = _colsum(jnp.where(p.tril, dgam, 0.0)) + dgam_last
        da_row = dg_row * p.a_neg * _sigmoid(p.xg)
        da_ref[...] = da_row
        db_ref[...] = p.to_row(dbeta) * p.beta_row * (1.0 - p.beta_row)
        dal_ref[...] += _rowsum(dg_row * p.g_row)
        ddt_ref[...] += _rowsum(da_row)
        side_finish()

    tok_shape = jax.ShapeDtypeStruct((hh, t, dk), F32)
    row_shape = jax.ShapeDtypeStruct((hh, n, 1, c), F32)
    acc_shape = jax.ShapeDtypeStruct((hh, 1, LANES), F32)
    return _carrier_call(
        body, name, grid, [smem, smem] + qkv + [row, row, state, state, tok], [tok, tok, tok, row, row, acc, acc],
        [tok_shape, tok_shape, tok_shape, row_shape, row_shape, acc_shape, acc_shape],
        [pltpu.VMEM((hb, dk, dk), F32)], side, (alog, dtb, qkv_h, qkv_h, qkv_h, araw, braw, states, t_mats, do))


def _gdn_post_fwd(o, proj, zcol, g_o, name):
    hh, t, dv = o.shape
    tt = _tile(t, _ROW_TILES)
    zblk = zcol // (hh * dv)

    def body(o_ref, z_ref, g_ref, y_ref):
        for h in range(hh):
            sl = slice(h * dv, (h + 1) * dv)
            ov = o_ref[h]
            r = lax.rsqrt(jnp.mean(ov * ov, axis=-1, keepdims=True) + EPS)
            y_ref[:, sl] = (ov * r * g_ref[...] * _silu(z_ref[:, sl])).astype(y_ref.dtype)

    return pl.pallas_call(
        body, name=name, grid=(t // tt,),
        in_specs=[pl.BlockSpec((hh, tt, dv), lambda i: (0, i, 0)), pl.BlockSpec((tt, hh * dv), lambda i: (i, zblk)),
                  pl.BlockSpec((1, dv), lambda i: (0, 0))],
        out_specs=pl.BlockSpec((tt, hh * dv), lambda i: (i, 0)),
        out_shape=jax.ShapeDtypeStruct((t, hh * dv), MXU_DTYPE), compiler_params=_params(("parallel",)))(o, proj, g_o)


def _gdn_post_bwd(o, proj, zcol, dy, g_o, dproj, name):
    hh, t, dv = o.shape
    tt = _tile(t, _ROW_TILES)
    zblk = zcol // (hh * dv)

    def body(o_ref, z_ref, dy_ref, g_ref, _, do_ref, dz_ref, dg_ref):
        @pl.when(pl.program_id(0) == 0)
        def _():
            dg_ref[...] = jnp.zeros_like(dg_ref)

        gv = g_ref[...]
        for h in range(hh):
            sl = slice(h * dv, (h + 1) * dv)
            ov, zz, dy = o_ref[h], z_ref[:, sl], dy_ref[:, sl]
            r = lax.rsqrt(jnp.mean(ov * ov, axis=-1, keepdims=True) + EPS)
            oh = ov * r
            sg = _sigmoid(zz)
            dz_ref[:, sl] = (dy * oh * gv * (sg * (1.0 + zz * (1.0 - sg)))).astype(dz_ref.dtype)
            don = dy * (zz * sg)
            dg_ref[...] += _colsum(don * oh)
            doh = don * gv
            do_ref[h] = r * (doh - oh * jnp.mean(doh * oh, axis=-1, keepdims=True))

    return pl.pallas_call(
        body, name=name, grid=(t // tt,),
        in_specs=[pl.BlockSpec((hh, tt, dv), lambda i: (0, i, 0)), pl.BlockSpec((tt, hh * dv), lambda i: (i, zblk)),
                  pl.BlockSpec((tt, hh * dv), lambda i: (i, 0)), pl.BlockSpec((1, dv), lambda i: (0, 0)), _ANY],
        out_specs=[pl.BlockSpec((hh, tt, dv), lambda i: (0, i, 0)), pl.BlockSpec((tt, hh * dv), lambda i: (i, zblk)),
                   pl.BlockSpec((1, dv), lambda i: (0, 0))],
        out_shape=[jax.ShapeDtypeStruct((hh, t, dv), F32), jax.ShapeDtypeStruct(dproj.shape, dproj.dtype),
                   jax.ShapeDtypeStruct((1, dv), F32)],
        input_output_aliases={4: 1}, compiler_params=_params(("arbitrary",)))(o, proj, dy, g_o, dproj)


def _cols_as_rows(x, col, name):
    t = x.shape[0]
    tt = _tile(t, _ROW_TILES)

    def body(x_ref, o_ref):
        o_ref[...] = x_ref[...].T

    return pl.pallas_call(
        body, name=name, grid=(t // tt,), in_specs=[pl.BlockSpec((tt, LANES), lambda i: (i, col // LANES))],
        out_specs=pl.BlockSpec((LANES, tt), lambda i: (0, i)), out_shape=jax.ShapeDtypeStruct((LANES, t), x.dtype),
        compiler_params=_params(("parallel",)))(x)


def _rows_into_cols(dst, rows, col, name):
    t = dst.shape[0]
    tt = _tile(t, _ROW_TILES)

    def body(r_ref, _, o_ref):
        o_ref[...] = r_ref[...].T.astype(o_ref.dtype)

    return pl.pallas_call(
        body, name=name, grid=(t // tt,), in_specs=[pl.BlockSpec((LANES, tt), lambda i: (0, i)), _ANY],
        out_specs=pl.BlockSpec((tt, LANES), lambda i: (i, col // LANES)),
        out_shape=jax.ShapeDtypeStruct(dst.shape, dst.dtype), input_output_aliases={1: 0},
        compiler_params=_params(("parallel",)))(rows, dst)


def _adamw(g, w, m, v):
    m = ADAM_B1 * m + (1.0 - ADAM_B1) * g
    v = ADAM_B2 * v + (1.0 - ADAM_B2) * (g * g)
    m_hat = m / (1.0 - ADAM_B1 ** ADAM_STEP)
    v_hat = v / (1.0 - ADAM_B2 ** ADAM_STEP)
    return -ADAM_LR * (m_hat / (jnp.sqrt(v_hat) + ADAM_EPS) + ADAM_WD * w), m, v


def _ada_fwd(c_all, ada_w, name):
    nl, d, cols = ada_w.shape
    b = c_all.shape[0]

    def body(c_ref, w_ref, o_ref):
        o_ref[...] = _mm_hi(_silu(c_ref[...]), w_ref[...])

    return pl.pallas_call(
        body, name=name, grid=(nl,),
        in_specs=[pl.BlockSpec((b, d), lambda i: (0, 0)), pl.BlockSpec((None, d, cols), lambda i: (i, 0, 0))],
        out_specs=pl.BlockSpec((None, b, cols), lambda i: (i, 0, 0)),
        out_shape=jax.ShapeDtypeStruct((nl, b, cols), F32), compiler_params=_params(("parallel",)))(c_all, ada_w)


def _ada_bwd(c_col, dm, w, m, v, name):
    nl, d, cols = w.shape
    b = c_col.shape[0]
    tr = _tile(d, (256, 128))

    def body(c_ref, dm_ref, w_ref, m_ref, v_ref, g_ref, dl_ref, mo_ref, vo_ref):
        g = _silu(c_ref[0]) * dm_ref[pl.ds(0, 1), :]
        for j in range(1, b):
            g = g + _silu(c_ref[j]) * dm_ref[pl.ds(j, 1), :]
        g_ref[...] = g
        dl_ref[...], mo_ref[...], vo_ref[...] = _adamw(g, w_ref[...], m_ref[...], v_ref[...])

    blk = pl.BlockSpec((None, tr, cols), lambda l, i: (l, i, 0))
    shape = jax.ShapeDtypeStruct((nl, d, cols), F32)
    return pl.pallas_call(
        body, name=name, grid=(nl, d // tr),
        in_specs=[pl.BlockSpec((b, tr, 1), lambda l, i: (0, i, 0)), pl.BlockSpec((None, b, cols), lambda l, i: (l, 0, 0)),
                  blk, blk, blk],
        out_specs=[blk, blk, blk, blk], out_shape=[shape] * 4,
        compiler_params=_params(("parallel", "parallel")))(c_col, dm, w, m, v)


def _sum_adam(parts, w, m, v, name):
    nl, npart, r, cdim = parts.shape
    tr = _tile(r, (256, 128))

    def body(p_ref, w_ref, m_ref, v_ref, g_ref, dl_ref, mo_ref, vo_ref):
        g = p_ref[0].astype(F32)
        for j in range(1, npart):
            g = g + p_ref[j].astype(F32)
        g_ref[...] = g
        dl_ref[...], mo_ref[...], vo_ref[...] = _adamw(g, w_ref[...], m_ref[...], v_ref[...])

    blk = pl.BlockSpec((None, tr, cdim), lambda l, i: (l, i, 0))
    shape = jax.ShapeDtypeStruct((nl, r, cdim), F32)
    return pl.pallas_call(
        body, name=name, grid=(nl, r // tr),
        in_specs=[pl.BlockSpec((None, npart, tr, cdim), lambda l, i: (l, 0, i, 0)), blk, blk, blk],
        out_specs=[blk, blk, blk, blk], out_shape=[shape] * 4,
        compiler_params=_params(("parallel", "parallel")))(parts, w, m, v)


def _cols_from_blocks(g, plan, width, name):
    _, r, cdim = g.shape
    tr = _tile(r, (256, 128))
    covered = sorted((dst, dst + n) for _, _, n, dst in plan)
    holes, pos = [], 0
    for a, b in covered:
        if a > pos:
            holes.append((pos, a))
        pos = max(pos, b)
    if pos < width:
        holes.append((pos, width))

    def body(g_ref, o_ref):
        for a, b in holes:
            o_ref[:, a:b] = jnp.zeros((tr, b - a), o_ref.dtype)
        for j, src, n, dst in plan:
            o_ref[:, dst:dst + n] = g_ref[j, :, src:src + n]

    return pl.pallas_call(
        body, name=name, grid=(r // tr,), in_specs=[pl.BlockSpec((N_DEV, tr, cdim), lambda i: (0, i, 0))],
        out_specs=pl.BlockSpec((tr, width), lambda i: (i, 0)), out_shape=jax.ShapeDtypeStruct((r, width), g.dtype),
        compiler_params=_params(("parallel",)))(g)


def _blocks_from_cols(w, plan, cdim, name):
    r, width = w.shape
    tr = _tile(r, (256, 128))

    def body(w_ref, o_ref):
        for j, src, n, dst in plan:
            o_ref[j, :, src:src + n] = w_ref[:, dst:dst + n]

    return pl.pallas_call(
        body, name=name, grid=(r // tr,), in_specs=[pl.BlockSpec((tr, width), lambda i: (i, 0))],
        out_specs=pl.BlockSpec((N_DEV, tr, cdim), lambda i: (0, i, 0)),
        out_shape=jax.ShapeDtypeStruct((N_DEV, r, cdim), w.dtype), compiler_params=_params(("parallel",)))(w)


def _pair_sum(x, tmp, core, name):
    _, r, cdim = x.shape
    tr = _tile(r, (256, 128))

    def body(core_ref, x_ref, t_ref, o_ref):
        o_ref[...] = (x_ref[...] + t_ref[...]).astype(o_ref.dtype)

    grid_spec = pltpu.PrefetchScalarGridSpec(
        num_scalar_prefetch=1, grid=(N_DEV // 2, r // tr),
        in_specs=[pl.BlockSpec((None, tr, cdim), lambda ch, i, core_ref: (2 * ch + core_ref[0], i, 0)),
                  pl.BlockSpec((None, tr, cdim), lambda ch, i, core_ref: (ch, i, 0))],
        out_specs=pl.BlockSpec((None, tr, cdim), lambda ch, i, core_ref: (ch, i, 0)))
    return pl.pallas_call(
        body, name=name, grid_spec=grid_spec, out_shape=jax.ShapeDtypeStruct((N_DEV // 2, r, cdim), WIRE_DTYPE),
        compiler_params=_params(("parallel", "parallel")))(core, x, tmp)


_ANY = pl.BlockSpec(memory_space=pl.ANY)
_CHIP_FLIPS = ((1, 0), (0, 1), (1, 1))


def _coords():
    return lax.axis_index("x"), lax.axis_index("y"), lax.axis_index("c")


def _flip(v, f):
    return 1 - v if f else v


def _a2a_direct(xs, name):
    n, ncp = len(xs), N_DEV - 1

    def body(*refs):
        ins, outs = refs[:n], refs[n:2 * n]
        send, recv, loc = refs[2 * n:]
        x, y, c = _coords()
        me = 4 * x + 2 * y + c
        local = [pltpu.make_async_copy(ins[i].at[me], outs[i].at[me], loc.at[i]) for i in range(n)]
        for cp in local:
            cp.start()
        remote = []
        for i in range(n):
            for k in range(1, N_DEV):
                px, py, pc = _flip(x, k & 4), _flip(y, k & 2), _flip(c, k & 1)
                cp = pltpu.make_async_remote_copy(
                    src_ref=ins[i].at[4 * px + 2 * py + pc], dst_ref=outs[i].at[me],
                    send_sem=send.at[i * ncp + k - 1], recv_sem=recv.at[i * ncp + k - 1],
                    device_id=(px, py, pc), device_id_type=MESH)
                cp.start()
                remote.append(cp)
        for cp in remote:
            cp.wait()
        for cp in local:
            cp.wait()

    return pl.pallas_call(
        body, name=name, in_specs=[_ANY] * n, out_specs=[_ANY] * n,
        out_shape=[jax.ShapeDtypeStruct(a.shape, a.dtype) for a in xs],
        scratch_shapes=[pltpu.SemaphoreType.DMA((n * ncp,)), pltpu.SemaphoreType.DMA((n * ncp,)),
                        pltpu.SemaphoreType.DMA((n,))])(*xs)


class _AllGatherSide:
    def __init__(self, blocks):
        self.operands = list(blocks)
        n = self.n = len(self.operands)
        self.n_in = self.n_out = n
        self.out_shape = [jax.ShapeDtypeStruct((N_DEV,) + a.shape, a.dtype) for a in self.operands]
        self.aliases = {}
        nici, nd2d = len(_CHIP_FLIPS), N_DEV // 2
        self.scratch = [pltpu.SemaphoreType.DMA((n * nici,)), pltpu.SemaphoreType.DMA((n * nici,)),
                        pltpu.SemaphoreType.DMA((n * nd2d,)), pltpu.SemaphoreType.DMA((n * nd2d,)),
                        pltpu.SemaphoreType.DMA((n,))]

    def _first(self, ins, outs, sems):
        send, recv, _, _, loc = sems
        x, y, c = _coords()
        me = 4 * x + 2 * y + c
        nici = len(_CHIP_FLIPS)
        local = [pltpu.make_async_copy(ins[i], outs[i].at[me], loc.at[i]) for i in range(self.n)]
        remote = [pltpu.make_async_remote_copy(
            src_ref=ins[i], dst_ref=outs[i].at[me], send_sem=send.at[i * nici + j], recv_sem=recv.at[i * nici + j],
            device_id=(_flip(x, fx), _flip(y, fy), c), device_id_type=MESH)
            for i in range(self.n) for j, (fx, fy) in enumerate(_CHIP_FLIPS)]
        return local + remote

    def _second(self, outs, sems):
        _, _, send, recv, _ = sems
        x, y, c = _coords()
        nd2d = N_DEV // 2
        return [pltpu.make_async_remote_copy(
            src_ref=outs[i].at[2 * ch + c], dst_ref=outs[i].at[2 * ch + c], send_sem=send.at[i * nd2d + ch],
            recv_sem=recv.at[i * nd2d + ch], device_id=(x, y, 1 - c), device_id_type=MESH)
            for i in range(self.n) for ch in range(nd2d)]

    def start(self, ins, outs, sems):
        for cp in self._first(ins, outs, sems):
            cp.start()

    def finish(self, ins, outs, sems):
        for cp in self._first(ins, outs, sems):
            cp.wait()
        second = self._second(outs, sems)
        for cp in second:
            cp.start()
        for cp in second:
            cp.wait()


class _ReduceScatterIciSide:
    def __init__(self, sums, accs, layer):
        self.operands = list(sums) + list(accs)
        n = self.n = len(sums)
        self.layer = layer
        self.n_in, self.n_out = 2 * n, n
        self.out_shape = [jax.ShapeDtypeStruct(a.shape, a.dtype) for a in accs]
        self.aliases = {n + i: i for i in range(n)}
        nici = len(_CHIP_FLIPS)
        self.scratch = [pltpu.SemaphoreType.DMA((n * nici,)), pltpu.SemaphoreType.DMA((n * nici,)),
                        pltpu.SemaphoreType.DMA((n,))]

    def _copies(self, ins, outs, sems):
        send, recv, loc = sems
        x, y, c = _coords()
        chip = 2 * x + y
        nici = len(_CHIP_FLIPS)
        local = [pltpu.make_async_copy(ins[i].at[chip], outs[i].at[self.layer, chip], loc.at[i])
                 for i in range(self.n)]
        remote = [pltpu.make_async_remote_copy(
            src_ref=ins[i].at[2 * _flip(x, fx) + _flip(y, fy)], dst_ref=outs[i].at[self.layer, chip],
            send_sem=send.at[i * nici + j], recv_sem=recv.at[i * nici + j],
            device_id=(_flip(x, fx), _flip(y, fy), c), device_id_type=MESH)
            for i in range(self.n) for j, (fx, fy) in enumerate(_CHIP_FLIPS)]
        return local + remote

    def start(self, ins, outs, sems):
        for cp in self._copies(ins, outs, sems):
            cp.start()

    def finish(self, ins, outs, sems):
        for cp in self._copies(ins, outs, sems):
            cp.wait()


def _run_side(side, name):
    def body(*refs):
        ins, outs = refs[:side.n_in], refs[side.n_in:side.n_in + side.n_out]
        sems = refs[side.n_in + side.n_out:]
        side.start(ins, outs, sems)
        side.finish(ins, outs, sems)

    return pl.pallas_call(
        body, name=name, in_specs=[_ANY] * side.n_in, out_specs=[_ANY] * side.n_out, out_shape=side.out_shape,
        input_output_aliases=side.aliases, scratch_shapes=side.scratch)(*side.operands)


class _ReduceScatterD2dSide:
    def __init__(self, parts):
        self.operands = list(parts)
        n = self.n = len(self.operands)
        self.n_in = self.n_out = n
        nd2d = N_DEV // 2
        self.out_shape = [jax.ShapeDtypeStruct((nd2d,) + a.shape[1:], a.dtype) for a in self.operands]
        self.aliases = {}
        self.scratch = [pltpu.SemaphoreType.DMA((n * nd2d,)), pltpu.SemaphoreType.DMA((n * nd2d,))]

    def _copies(self, ins, outs, sems):
        send, recv = sems
        x, y, c = _coords()
        nd2d = N_DEV // 2
        return [pltpu.make_async_remote_copy(
            src_ref=ins[i].at[2 * ch + 1 - c], dst_ref=outs[i].at[ch], send_sem=send.at[i * nd2d + ch],
            recv_sem=recv.at[i * nd2d + ch], device_id=(x, y, 1 - c), device_id_type=MESH)
            for i in range(self.n) for ch in range(nd2d)]

    def start(self, ins, outs, sems):
        for cp in self._copies(ins, outs, sems):
            cp.start()

    def finish(self, ins, outs, sems):
        for cp in self._copies(ins, outs, sems):
            cp.wait()


_PACK_ROWS = 256


def _pack(arrs):
    flat = jnp.concatenate([a.reshape(-1) for a in arrs])
    quantum = _PACK_ROWS * LANES
    total = -(-flat.shape[0] // quantum) * quantum
    return jnp.pad(flat, (0, total - flat.shape[0])).reshape(-1, LANES)


def _unpack(packed, like):
    flat, out, pos = packed.reshape(-1), [], 0
    for a in like:
        out.append(flat[pos:pos + a.size].reshape(a.shape))
        pos += a.size
    return out


def kernel(x, c, ada_w, ada_b, norm1_g, w_in, conv_w, spatial_w, spatial_b, v_norm_g, a_log, dt_bias, o_norm_g, w_branch_a, w_branch_b, w_out, norm2_g, w_ffn_in, w_ffn_out, final_g, loss_target, m_ada_w, m_ada_b, m_norm1_g, m_w_in, m_conv_w, m_spatial_w, m_spatial_b, m_v_norm_g, m_a_log, m_dt_bias, m_o_norm_g, m_w_branch_a, m_w_branch_b, m_w_out, m_norm2_g, m_w_ffn_in, m_w_ffn_out, m_final_g, v_ada_w, v_ada_b, v_norm1_g, v_w_in, v_conv_w, v_spatial_w, v_spatial_b, v_v_norm_g, v_a_log, v_dt_bias, v_o_norm_g, v_w_branch_a, v_w_branch_b, v_w_out, v_norm2_g, v_w_ffn_in, v_w_ffn_out, v_final_g):
    nl, d = ada_w.shape[0], x.shape[2]
    t = x.shape[1]
    nchunk = t // GDN_CHUNK
    xi, yi, ci = _coords()
    me = 4 * xi + 2 * yi + ci
    core = jnp.reshape(ci, (1,)).astype(jnp.int32)
    x0, target = x[0], loss_target[0]
    wcols = 3 * HEADS * HEAD_DIM
    lay = _ProjLayout(d)
    in_pieces = lay.pieces(w_in.shape[2])
    fi_shard = w_ffn_in.shape[2]
    fi_pieces = [(j, 0, fi_shard, fi_shard * j) for j in range(N_DEV)]

    c_all, cw_all = _a2a_direct([jnp.broadcast_to(c[None], (N_DEV,) + c.shape),
                                 jnp.broadcast_to(conv_w[None], (N_DEV,) + conv_w.shape)], "gather_small")
    c_all = c_all[:, 0]
    conv_full = cw_all.transpose(1, 2, 0, 3).reshape(nl, CONV_K, wcols)
    modp = _ada_fwd(c_all, ada_w, "ada_fwd")
    (modx,) = _a2a_direct([modp.transpose(1, 0, 2)], "mod_exchange")
    mod = (modx.transpose(1, 0, 2).reshape(nl, 6 * d) + ada_b).reshape(nl, 6, 1, d)

    big = (w_in, w_branch_a, w_branch_b, w_out, w_ffn_in, w_ffn_out)
    big_wire = [w.astype(WIRE_DTYPE) for w in big]
    gather_in = lambda i: _AllGatherSide([big_wire[0][i]])
    gather_early = lambda i: _AllGatherSide([big_wire[k][i] for k in (1, 2, 3, 5)])
    gather_late = lambda i: _AllGatherSide([big_wire[4][i]] + ([big_wire[0][i + 1]] if i + 1 < nl else []))
    row_full = lambda g: g.reshape(-1, g.shape[2])
    padded_in = lambda g: _cols_from_blocks(g, in_pieces, lay.width, "w_in_cols")
    w_pads = [padded_in(_run_side(gather_in(0), "ag_first")[0])] + [None] * (nl - 1)
    weights = [None] * nl

    def rows_of(ba_rows, lo):
        return ba_rows[lo:lo + HEADS].reshape(HEADS, nchunk, 1, GDN_CHUNK)

    saved = []
    x_cur, delta, gt_prev = x0, None, None
    for i in range(nl):
        sh1, sc1, gt1, sh2, sc2, gt2 = (mod[i, k] for k in range(6))
        s = dict(gt1=gt1, gt2=gt2, sc1=sc1, sc2=sc2)
        s["x_in"], s["h"] = _resid_norm(x_cur, delta, gt_prev, norm1_g[i][None], sc1, sh1, "norm1_fwd")
        s["proj"], g_a, g_b, g_o, g_fo = _matmul(s["h"], w_pads[i], "nn", "proj_fwd", side=gather_early(i))
        s["b_col"] = spatial_b[i][:, :, None]
        s["ya"] = _mixer_a_fwd(s["proj"], lay.uv, spatial_w[i], s["b_col"], v_norm_g[i][None], "mixer_a_fwd")
        s["qkv_h"] = _conv_fwd(s["proj"], conv_full[i], "conv_fwd")
        ba_rows = _cols_as_rows(s["proj"], lay.ba, "ba_rows")
        s["braw"], s["araw"] = rows_of(ba_rows, 0), rows_of(ba_rows, HEADS)
        s["o"], s["states"], s["t_mats"], g_fi, *g_in = _gdn_fwd(
            s["qkv_h"], s["araw"], s["braw"], a_log[i], dt_bias[i], "gdn_fwd", gather_late(i))
        if g_in:
            w_pads[i + 1] = padded_in(g_in[0])
        weights[i] = (row_full(g_a), row_full(g_b), row_full(g_o),
                      _cols_from_blocks(g_fi, fi_pieces, N_DEV * fi_shard, "w_ffn_in_cols"), row_full(g_fo))
        w_a, w_b, w_o, w_fi, w_fo = weights[i]
        s["yb"] = _gdn_post_fwd(s["o"], s["proj"], lay.z, o_norm_g[i][None], "gdn_post_fwd")
        s["pa"] = _matmul(s["ya"], w_a, "nn", "branch_a_fwd")
        s["pb"] = _matmul(s["yb"], w_b, "nn", "branch_b_fwd")
        s["merged"] = _merge_fwd(s["pa"], s["pb"], s["proj"], lay.gates, "merge_fwd")
        s["mo"] = _matmul(s["merged"], w_o, "nn", "out_fwd")
        s["x1"], s["h2"] = _resid_norm(s["x_in"], s["mo"], gt1, norm2_g[i][None], sc2, sh2, "norm2_fwd")
        s["gu"] = _matmul(s["h2"], w_fi, "nn", "ffn_in_fwd")
        s["a"] = _swiglu_fwd(s["gu"], "swiglu_fwd")
        s["fo"] = _matmul(s["a"], w_fo, "nn", "ffn_out_fwd")
        saved.append(s)
        x_cur, delta, gt_prev = s["x1"], s["fo"], gt2
    dx, d_final_g, loss_tile = _final_loss(x_cur, delta, gt_prev, final_g[None], target, "final_loss")
    loss = lax.psum(loss_tile[0, 0], ("x", "y", "c"))

    big_shapes = [(d, w_in.shape[2]), w_branch_a.shape[1:], w_branch_b.shape[1:], w_out.shape[1:],
                  (d, w_ffn_in.shape[2]), w_ffn_out.shape[1:]]
    accs = [lax.empty((nl, N_DEV // 2) + tuple(sh), WIRE_DTYPE) for sh in big_shapes]
    row_blocks = lambda g: g.reshape(N_DEV, -1, g.shape[1])
    dmod, small = [None] * nl, [None] * nl
    d_conv = [None] * nl
    parts, sums = None, None
    beside_gdn, beside_dw, beside_dx = (0,), (4,), (1, 2, 3, 5)
    rep_parts = [None] * nl
    rep_pack = lambda i: _pack((dmod[i],) + small[i])

    def scatter_side(idx, layer):
        if sums is None:
            return _NoSide
        return _ReduceScatterIciSide([sums[k] for k in idx], [accs[k] for k in idx], layer)

    def scattered_into(accs, idx, new):
        accs = list(accs)
        for k, a in zip(idx, new):
            accs[k] = a
        return accs

    for i in reversed(range(nl)):
        s = saved[i]
        w_a, w_b, w_o, w_fi, w_fo = weights[i]
        dfo, dgt2 = _gate_bwd(dx, s["fo"], s["gt2"], "gate2_bwd")
        g_fo = _matmul(s["a"], dfo, "tn", "ffn_out_dw")
        da = _matmul(dfo, w_fo, "nt", "ffn_out_dx")
        dgu = _swiglu_bwd(s["gu"], da, "swiglu_bwd")
        if parts is None:
            g_fi = _matmul(s["h2"], dgu, "tn", "ffn_in_dw")
        else:
            g_fi, *other = _matmul(s["h2"], dgu, "tn", "ffn_in_dw", side=_ReduceScatterD2dSide(parts))
            sums = [_pair_sum(p, o, core, "rs_pair_sum_%d" % k) for k, (p, o) in enumerate(zip(parts, other))]
        if i + 1 < nl:
            dh2, rep_parts[i + 1] = _matmul(dgu, w_fi, "nt", "ffn_in_dx", side=_AllGatherSide([rep_pack(i + 1)]))
        else:
            dh2 = _matmul(dgu, w_fi, "nt", "ffn_in_dx")
        dx1, dsh2, dsc2, dg2 = _norm_bwd(s["x1"], dh2, dx, norm2_g[i][None], s["sc2"], "norm2_bwd")
        dmo, dgt1 = _gate_bwd(dx1, s["mo"], s["gt1"], "gate1_bwd")
        g_o = _matmul(s["merged"], dmo, "tn", "out_dw")
        dmerged = _matmul(dmo, w_o, "nt", "out_dx")
        dproj = lax.empty((t, lay.width), MXU_DTYPE)
        dpa, dpb, dproj = _merge_bwd(dmerged, s["pa"], s["pb"], s["proj"], lay.gates, dproj, "merge_bwd")
        g_a = _matmul(s["ya"], dpa, "tn", "branch_a_dw")
        dya = _matmul(dpa, w_a, "nt", "branch_a_dx")
        g_b = _matmul(s["yb"], dpb, "tn", "branch_b_dw")
        dyb = _matmul(dpb, w_b, "nt", "branch_b_dx")
        dproj, d_ws, d_bs, d_gv = _mixer_a_bwd(s["proj"], lay.uv, dya, spatial_w[i], jnp.swapaxes(spatial_w[i], 1, 2),
                                               s["b_col"], v_norm_g[i][None], dproj, "mixer_a_bwd")
        do, dproj, d_go = _gdn_post_bwd(s["o"], s["proj"], lay.z, dyb, o_norm_g[i][None], dproj, "gdn_post_bwd")
        dq, dk, dv, d_ar, d_br, d_al, d_dt, *scattered = _gdn_bwd(
            s["qkv_h"], s["araw"], s["braw"], a_log[i], dt_bias[i], s["states"], s["t_mats"], do, "gdn_bwd",
            scatter_side(beside_gdn, i + 1))
        accs = scattered_into(accs, beside_gdn, scattered)
        dacc, d_conv[i] = _conv_bwd_pre(s["proj"], dq, dk, dv, conv_full[i], "conv_bwd_pre")
        dproj = _conv_bwd_in(dacc, conv_full[i], dproj, "conv_bwd_in")
        dba_rows = jnp.pad(jnp.concatenate([d_br.reshape(HEADS, t), d_ar.reshape(HEADS, t)]),
                           ((0, LANES - 2 * HEADS), (0, 0)))
        dproj = _rows_into_cols(dproj, dba_rows, lay.ba, "dproj_ba")
        if sums is None:
            g_pad = _matmul(s["h"], dproj, "tn", "proj_dw")
            dh = _matmul(dproj, w_pads[i], "nt", "proj_dx")
        else:
            g_pad, *scattered = _matmul(s["h"], dproj, "tn", "proj_dw", side=scatter_side(beside_dw, i + 1))
            accs = scattered_into(accs, beside_dw, scattered)
            dh, *scattered = _matmul(dproj, w_pads[i], "nt", "proj_dx", side=scatter_side(beside_dx, i + 1))
            accs = scattered_into(accs, beside_dx, scattered)
        dx, dsh1, dsc1, dg1 = _norm_bwd(s["x_in"], dh, dx1, norm1_g[i][None], s["sc1"], "norm1_bwd")
        dmod[i] = jnp.concatenate([dsh1, dsc1, dgt1, dsh2, dsc2, dgt2], axis=1)[0]
        small[i] = (dg1[0], d_ws, d_bs[:, :, 0], d_gv[0], d_al[:, 0, 0], d_dt[:, 0, 0], d_go[0], dg2[0])
        parts = [_blocks_from_cols(g_pad, in_pieces, w_in.shape[2], "w_in_blocks"), row_blocks(g_a), row_blocks(g_b),
                 row_blocks(g_o), _blocks_from_cols(g_fi, fi_pieces, fi_shard, "w_ffn_in_blocks"), row_blocks(g_fo)]
    other = _run_side(_ReduceScatterD2dSide(parts), "rs_d2d_last")
    sums = [_pair_sum(p, o, core, "rs_pair_sum_%d" % k) for k, (p, o) in enumerate(zip(parts, other))]
    accs = _run_side(_ReduceScatterIciSide(sums, accs, 0), "rs_ici_last")

    rep_w = (ada_b, norm1_g, spatial_w, spatial_b, v_norm_g, a_log, dt_bias, o_norm_g, norm2_g)
    rep_m = (m_ada_b, m_norm1_g, m_spatial_w, m_spatial_b, m_v_norm_g, m_a_log, m_dt_bias, m_o_norm_g, m_norm2_g)
    rep_v = (v_ada_b, v_norm1_g, v_spatial_w, v_spatial_b, v_v_norm_g, v_a_log, v_dt_bias, v_o_norm_g, v_norm2_g)
    rep_parts[0], fin_parts = _run_side(_AllGatherSide([rep_pack(0), _pack([d_final_g[0]])]), "small_grads_last")
    dmod = jnp.stack(dmod)
    d_conv_blocks = jnp.stack(d_conv).reshape(nl, CONV_K, N_DEV, -1).transpose(2, 0, 1, 3).reshape(N_DEV, -1, LANES)
    dmod_blocks = dmod.reshape(nl, N_DEV, -1).transpose(1, 0, 2)
    conv_all, dmod_all = _a2a_direct([d_conv_blocks, dmod_blocks], "small_grads_scatter")
    by_layer = lambda arrs: jnp.stack([_pack([a[i] for a in arrs]) for i in range(nl)])
    rep_out = _sum_adam(jnp.stack(rep_parts), by_layer(rep_w), by_layer(rep_m), by_layer(rep_v), "adam_small")
    fin_out = _sum_adam(fin_parts[None], _pack([final_g])[None], _pack([m_final_g])[None], _pack([v_final_g])[None],
                        "adam_final_g")
    layer_like = [a[0] for a in rep_w]
    rep_out = [[jnp.stack(per_layer) for per_layer in zip(*[_unpack(o[i], layer_like) for i in range(nl)])]
               + _unpack(f[0], [final_g]) for o, f in zip(rep_out, fin_out)]
    conv_out = _sum_adam(conv_all[None], conv_w.reshape(1, -1, LANES), m_conv_w.reshape(1, -1, LANES),
                         v_conv_w.reshape(1, -1, LANES), "adam_conv")
    conv_out = [o.reshape(conv_w.shape) for o in conv_out]
    ada_out = _ada_bwd(c_all[:, :, None], dmod_all.transpose(1, 0, 2), ada_w, m_ada_w, v_ada_w, "ada_bwd_adam")
    big_m = (m_w_in, m_w_branch_a, m_w_branch_b, m_w_out, m_w_ffn_in, m_w_ffn_out)
    big_v = (v_w_in, v_w_branch_a, v_w_branch_b, v_w_out, v_w_ffn_in, v_w_ffn_out)
    big_out = [_sum_adam(accs[k], big[k], big_m[k], big_v[k], "adam_big_%d" % k) for k in range(6)]

    def ordered(kind):
        rep = rep_out[kind]
        return (ada_out[kind], rep[0], rep[1], big_out[0][kind], conv_out[kind], rep[2], rep[3], rep[4], rep[5],
                rep[6], rep[7], big_out[1][kind], big_out[2][kind], big_out[3][kind], rep[8], big_out[4][kind],
                big_out[5][kind], rep[9])

    return (loss, dx[None]) + ordered(0) + ordered(1) + ordered(2) + ordered(3)
```

```python
import functools

import jax
import jax.numpy as jnp
from jax import lax
from jax.experimental import pallas as pl
from jax.experimental.pallas import tpu as pltpu

F32 = jnp.float32
BF16 = jnp.bfloat16
MXU_DTYPE = BF16
WIRE_DTYPE = BF16
ACT_DTYPE = BF16
EPS = 1e-6
LANES = 128
SUBLANES = 8
GDN_CHUNK = 128
A_CHUNK = 128
GROUPS = 8
HEADS = 8
HEAD_DIM = 128
CONV_K = 4
N_DEV = 8
VMEM_LIMIT = 48 * 1024 * 1024
MESH = pl.DeviceIdType.MESH

ADAM_LR = 0.001
ADAM_B1 = 0.9
ADAM_B2 = 0.999
ADAM_EPS = 1e-08
ADAM_WD = 0.01
ADAM_STEP = 10

_NN = (((1,), (0,)), ((), ()))
_NT = (((1,), (1,)), ((), ()))
_TN = (((0,), (0,)), ((), ()))


def _mm(a, b, dims=_NN):
    return lax.dot_general(a.astype(MXU_DTYPE), b.astype(MXU_DTYPE), dims, preferred_element_type=F32)


def _mm_hi(a, b):
    return lax.dot_general(a, b, _NN, precision=lax.Precision.HIGHEST, preferred_element_type=F32)


def _tile(n, cands):
    for c in cands:
        if n % c == 0:
            return c
    return n


def _params(sem=None):
    return pltpu.CompilerParams(dimension_semantics=sem, vmem_limit_bytes=VMEM_LIMIT)


def _sigmoid(x):
    return 1.0 / (1.0 + jnp.exp(-x))


def _silu(x):
    return x * _sigmoid(x)


def _dsilu(x):
    s = _sigmoid(x)
    return s * (1.0 + x * (1.0 - s))


_GELU_C = 0.7978845608028654
_GELU_A = 0.044715


def _gelu(x):
    return 0.5 * x * (1.0 + jnp.tanh(_GELU_C * (x + _GELU_A * x * x * x)))


def _gelu_and_slope(x):
    t = jnp.tanh(_GELU_C * (x + _GELU_A * x * x * x))
    return 0.5 * x * (1.0 + t), 0.5 * (1.0 + t) + 0.5 * x * (1.0 - t * t) * _GELU_C * (1.0 + 3.0 * _GELU_A * x * x)


def _softplus(x):
    return jnp.maximum(x, 0.0) + jnp.log(1.0 + jnp.exp(-jnp.abs(x)))


_MM_TILES = (1024, 1408, 1664, 512, 256, 128)


class _NoSide:
    operands, out_shape, scratch, aliases, n_in, n_out = [], [], [], {}, 0, 0


def _side_hooks(side, refs, n_main_in, n_main_out, n_main_scratch, grid):
    a = n_main_in + side.n_in
    b = a + n_main_out + side.n_out
    ins, outs, sems = refs[n_main_in:a], refs[a + n_main_out:b], refs[b + n_main_scratch:]
    main = refs[:n_main_in] + refs[a:a + n_main_out] + refs[b:b + n_main_scratch]
    ids = [pl.program_id(k) for k in range(len(grid))]

    def start():
        if side.n_in:
            pl.when(functools.reduce(jnp.logical_and, [i == 0 for i in ids]))(lambda: side.start(ins, outs, sems))

    def finish():
        if side.n_in:
            last = functools.reduce(jnp.logical_and, [i == g - 1 for i, g in zip(ids, grid)])
            pl.when(last)(lambda: side.finish(ins, outs, sems))

    return main, start, finish


def _carrier_call(body, name, grid, in_specs, out_specs, out_shape, scratch, side, args):
    aliases = {len(in_specs) + k: len(out_specs) + v for k, v in side.aliases.items()}
    return pl.pallas_call(
        body, name=name, grid=grid, in_specs=list(in_specs) + [_ANY] * side.n_in,
        out_specs=list(out_specs) + [_ANY] * side.n_out, out_shape=list(out_shape) + list(side.out_shape),
        scratch_shapes=list(scratch) + list(side.scratch), input_output_aliases=aliases,
        compiler_params=_params(("arbitrary",) * len(grid)))(*args, *side.operands)


def _matmul(a, b, mode, name, out_dtype=F32, side=_NoSide):
    if mode == "nn":
        (m, k), n = a.shape, b.shape[1]
    elif mode == "nt":
        (m, k), n = a.shape, b.shape[0]
    else:
        (k, m), n = a.shape, b.shape[1]
    tm, tn, tk = _tile(m, _MM_TILES), _tile(n, _MM_TILES), _tile(k, _MM_TILES)
    nk = k // tk
    grid = (m // tm, n // tn, nk)
    dims = {"nn": _NN, "nt": _NT, "tn": _TN}[mode]

    def body(*refs):
        (a_ref, b_ref, o_ref, acc_ref), side_start, side_finish = _side_hooks(side, refs, 2, 1, 1, grid)
        kk = pl.program_id(2)
        side_start()
        if nk == 1:
            o_ref[...] = _mm(a_ref[...], b_ref[...], dims).astype(o_ref.dtype)
        else:
            @pl.when(kk == 0)
            def _():
                acc_ref[...] = _mm(a_ref[...], b_ref[...], dims)

            @pl.when(jnp.logical_and(kk > 0, kk < nk - 1))
            def _():
                acc_ref[...] += _mm(a_ref[...], b_ref[...], dims)

            @pl.when(kk == nk - 1)
            def _():
                o_ref[...] = (acc_ref[...] + _mm(a_ref[...], b_ref[...], dims)).astype(o_ref.dtype)

        side_finish()

    a_spec = (pl.BlockSpec((tk, tm), lambda i, j, l: (l, i)) if mode == "tn"
              else pl.BlockSpec((tm, tk), lambda i, j, l: (i, l)))
    b_spec = (pl.BlockSpec((tn, tk), lambda i, j, l: (j, l)) if mode == "nt"
              else pl.BlockSpec((tk, tn), lambda i, j, l: (l, j)))
    o_spec = pl.BlockSpec((tm, tn), lambda i, j, l: (i, j))
    out = _carrier_call(body, name, grid, [a_spec, b_spec], [o_spec], [jax.ShapeDtypeStruct((m, n), out_dtype)],
                        [pltpu.VMEM((tm, tn) if nk > 1 else (SUBLANES, LANES), F32)], side, (a, b))
    return out if side.n_in else out[0]


_ROW_TILES = (512, 256, 128)


def _resid_norm(x, delta, gt, g, sc, sh, name):
    t, d = x.shape
    tt = _tile(t, _ROW_TILES)
    has = delta is not None

    def body(*refs):
        if has:
            x_ref, d_ref, gt_ref, g_ref, sc_ref, sh_ref, xo_ref, h_ref = refs
            xv = x_ref[...] + gt_ref[...] * d_ref[...]
            xo_ref[...] = xv
        else:
            x_ref, g_ref, sc_ref, sh_ref, h_ref = refs
            xv = x_ref[...]
        r = lax.rsqrt(jnp.mean(xv * xv, axis=-1, keepdims=True) + EPS)
        y = xv * r * g_ref[...]
        h_ref[...] = (y * (1.0 + sc_ref[...]) + sh_ref[...]).astype(h_ref.dtype)

    row = pl.BlockSpec((tt, d), lambda i: (i, 0))
    vec = pl.BlockSpec((1, d), lambda i: (0, 0))
    if has:
        return pl.pallas_call(
            body, name=name, grid=(t // tt,), in_specs=[row, row, vec, vec, vec, vec], out_specs=[row, row],
            out_shape=[jax.ShapeDtypeStruct((t, d), F32), jax.ShapeDtypeStruct((t, d), MXU_DTYPE)],
            compiler_params=_params(("parallel",)))(x, delta, gt, g, sc, sh)
    h = pl.pallas_call(
        body, name=name + "_first", grid=(t // tt,), in_specs=[row, vec, vec, vec], out_specs=row,
        out_shape=jax.ShapeDtypeStruct((t, d), MXU_DTYPE), compiler_params=_params(("parallel",)))(x, g, sc, sh)
    return x, h


def _final_loss(x, delta, gt, g, target, name):
    t, d = x.shape
    tt = _tile(t, _ROW_TILES)

    def body(x_ref, d_ref, gt_ref, g_ref, tg_ref, dx_ref, dg_ref, loss_ref):
        @pl.when(pl.program_id(0) == 0)
        def _():
            dg_ref[...] = jnp.zeros_like(dg_ref)
            loss_ref[...] = jnp.zeros_like(loss_ref)

        xv = x_ref[...] + gt_ref[...] * d_ref[...]
        r = lax.rsqrt(jnp.mean(xv * xv, axis=-1, keepdims=True) + EPS)
        xh = xv * r
        diff = xh * g_ref[...] - tg_ref[...]
        loss_ref[...] += jnp.sum(diff * diff) * (0.5 / d)
        dy = diff * (1.0 / d)
        dg_ref[...] += jnp.sum(dy * xh, axis=0, keepdims=True)
        dxh = dy * g_ref[...]
        dx_ref[...] = r * (dxh - xh * jnp.mean(dxh * xh, axis=-1, keepdims=True))

    row = pl.BlockSpec((tt, d), lambda i: (i, 0))
    vec = pl.BlockSpec((1, d), lambda i: (0, 0))
    tile = pl.BlockSpec((SUBLANES, LANES), lambda i: (0, 0))
    return pl.pallas_call(
        body, name=name, grid=(t // tt,), in_specs=[row, row, vec, vec, row], out_specs=[row, vec, tile],
        out_shape=[jax.ShapeDtypeStruct((t, d), F32), jax.ShapeDtypeStruct((1, d), F32),
                   jax.ShapeDtypeStruct((SUBLANES, LANES), F32)],
        compiler_params=_params(("arbitrary",)))(x, delta, gt, g, target)


def _norm_bwd(x, dh, dres, g, sc, name):
    t, d = x.shape
    tt = _tile(t, _ROW_TILES)

    def body(x_ref, dh_ref, dr_ref, g_ref, sc_ref, dx_ref, dsh_ref, dsc_ref, dg_ref):
        @pl.when(pl.program_id(0) == 0)
        def _():
            dsh_ref[...] = jnp.zeros_like(dsh_ref)
            dsc_ref[...] = jnp.zeros_like(dsc_ref)
            dg_ref[...] = jnp.zeros_like(dg_ref)

        xv, dh = x_ref[...], dh_ref[...]
        r = lax.rsqrt(jnp.mean(xv * xv, axis=-1, keepdims=True) + EPS)
        xh = xv * r
        gv, sc1 = g_ref[...], 1.0 + sc_ref[...]
        dsh_ref[...] += jnp.sum(dh, axis=0, keepdims=True)
        dsc_ref[...] += jnp.sum(dh * xh, axis=0, keepdims=True) * gv
        dg_ref[...] += jnp.sum(dh * xh, axis=0, keepdims=True) * sc1
        dxh = dh * (gv * sc1)
        dx_ref[...] = dr_ref[...] + r * (dxh - xh * jnp.mean(dxh * xh, axis=-1, keepdims=True))

    row = pl.BlockSpec((tt, d), lambda i: (i, 0))
    vec = pl.BlockSpec((1, d), lambda i: (0, 0))
    vshape = jax.ShapeDtypeStruct((1, d), F32)
    return pl.pallas_call(
        body, name=name, grid=(t // tt,), in_specs=[row, row, row, vec, vec], out_specs=[row, vec, vec, vec],
        out_shape=[jax.ShapeDtypeStruct((t, d), F32), vshape, vshape, vshape],
        compiler_params=_params(("arbitrary",)))(x, dh, dres, g, sc)


def _gate_bwd(dxo, branch, gt, name):
    t, d = dxo.shape
    tt = _tile(t, _ROW_TILES)

    def body(dx_ref, br_ref, gt_ref, db_ref, dgt_ref):
        @pl.when(pl.program_id(0) == 0)
        def _():
            dgt_ref[...] = jnp.zeros_like(dgt_ref)

        dx = dx_ref[...]
        db_ref[...] = (dx * gt_ref[...]).astype(db_ref.dtype)
        dgt_ref[...] += jnp.sum(dx * br_ref[...], axis=0, keepdims=True)

    row = pl.BlockSpec((tt, d), lambda i: (i, 0))
    vec = pl.BlockSpec((1, d), lambda i: (0, 0))
    return pl.pallas_call(
        body, name=name, grid=(t // tt,), in_specs=[row, row, vec], out_specs=[row, vec],
        out_shape=[jax.ShapeDtypeStruct((t, d), MXU_DTYPE), jax.ShapeDtypeStruct((1, d), F32)],
        compiler_params=_params(("arbitrary",)))(dxo, branch, gt)


def _swiglu_fwd(gu, name):
    t, f2 = gu.shape
    f = f2 // 2
    tt = _tile(t, (256, 128))

    def body(g_ref, u_ref, o_ref):
        o_ref[...] = (_silu(g_ref[...].astype(F32)) * u_ref[...].astype(F32)).astype(o_ref.dtype)

    return pl.pallas_call(
        body, name=name, grid=(t // tt,),
        in_specs=[pl.BlockSpec((tt, f), lambda i: (i, 0)), pl.BlockSpec((tt, f), lambda i: (i, 1))],
        out_specs=pl.BlockSpec((tt, f), lambda i: (i, 0)), out_shape=jax.ShapeDtypeStruct((t, f), MXU_DTYPE),
        compiler_params=_params(("parallel",)))(gu, gu)


def _swiglu_bwd(gu, da, name):
    t, f2 = gu.shape
    f = f2 // 2
    tt = _tile(t, (256, 128))

    def body(g_ref, u_ref, da_ref, o_ref):
        gate, da = g_ref[...].astype(F32), da_ref[...]
        sg = _sigmoid(gate)
        o_ref[:, :f] = (da * u_ref[...].astype(F32) * (sg * (1.0 + gate * (1.0 - sg)))).astype(o_ref.dtype)
        o_ref[:, f:] = (da * (gate * sg)).astype(o_ref.dtype)

    return pl.pallas_call(
        body, name=name, grid=(t // tt,),
        in_specs=[pl.BlockSpec((tt, f), lambda i: (i, 0)), pl.BlockSpec((tt, f), lambda i: (i, 1)),
                  pl.BlockSpec((tt, f), lambda i: (i, 0))],
        out_specs=pl.BlockSpec((tt, f2), lambda i: (i, 0)), out_shape=jax.ShapeDtypeStruct((t, f2), MXU_DTYPE),
        compiler_params=_params(("parallel",)))(gu, gu, da)


class _ProjLayout:
    def __init__(self, d):
        wc = 3 * HEADS * HEAD_DIM
        self.d, self.wc = d, wc
        self.qkv, self.z, self.uv, self.gates, self.ba = 0, wc, wc + d, wc + 3 * d, wc + 5 * d
        self.width = self.ba + LANES
        assert self.z % d == 0 and self.uv % (2 * d) == 0 and self.gates % (2 * d) == 0 and self.ba % LANES == 0

    def pieces(self, shard):
        d, wc, out, lo = self.d, self.wc, [], 0
        for length, dst in ((2 * d, self.uv), (wc, self.qkv), (d, self.z), (2 * HEADS, self.ba), (2 * d, self.gates)):
            pos = lo
            while pos < lo + length:
                j = pos // shard
                n = min(lo + length, (j + 1) * shard) - pos
                out.append((j, pos - j * shard, n, dst + pos - lo))
                pos += n
            lo += length
        return out


def _merge_fwd(pa, pb, proj, gcol, name):
    t, d = pa.shape
    tt = _tile(t, _ROW_TILES)

    def body(pa_ref, pb_ref, ga_ref, gb_ref, o_ref):
        sa, sb = _sigmoid(ga_ref[...].astype(F32)), _sigmoid(gb_ref[...].astype(F32))
        o_ref[...] = (sa * pa_ref[...] + sb * pb_ref[...]).astype(o_ref.dtype)

    row = pl.BlockSpec((tt, d), lambda i: (i, 0))
    gate = lambda k: pl.BlockSpec((tt, d), lambda i: (i, gcol // d + k))
    return pl.pallas_call(
        body, name=name, grid=(t // tt,), in_specs=[row, row, gate(0), gate(1)], out_specs=row,
        out_shape=jax.ShapeDtypeStruct((t, d), MXU_DTYPE), compiler_params=_params(("parallel",)))(pa, pb, proj, proj)


def _merge_bwd(dm, pa, pb, proj, gcol, dproj, name):
    t, d = pa.shape
    tt = _tile(t, _ROW_TILES)

    def body(dm_ref, pa_ref, pb_ref, ga_ref, gb_ref, _, dpa_ref, dpb_ref, dg_ref):
        dm = dm_ref[...]
        sa, sb = _sigmoid(ga_ref[...].astype(F32)), _sigmoid(gb_ref[...].astype(F32))
        dpa_ref[...] = (dm * sa).astype(dpa_ref.dtype)
        dpb_ref[...] = (dm * sb).astype(dpb_ref.dtype)
        dg_ref[:, :d] = (dm * pa_ref[...] * sa * (1.0 - sa)).astype(dg_ref.dtype)
        dg_ref[:, d:] = (dm * pb_ref[...] * sb * (1.0 - sb)).astype(dg_ref.dtype)

    row = pl.BlockSpec((tt, d), lambda i: (i, 0))
    gate = lambda k: pl.BlockSpec((tt, d), lambda i: (i, gcol // d + k))
    wide = pl.BlockSpec((tt, 2 * d), lambda i: (i, gcol // (2 * d)))
    return pl.pallas_call(
        body, name=name, grid=(t // tt,), in_specs=[row, row, row, gate(0), gate(1), _ANY], out_specs=[row, row, wide],
        out_shape=[jax.ShapeDtypeStruct((t, d), MXU_DTYPE), jax.ShapeDtypeStruct((t, d), MXU_DTYPE),
                   jax.ShapeDtypeStruct(dproj.shape, dproj.dtype)],
        input_output_aliases={5: 2}, compiler_params=_params(("parallel",)))(dm, pa, pb, proj, proj, dproj)


def _tri_masks(n):
    ri = lax.broadcasted_iota(jnp.int32, (n, n), 0)
    ci = lax.broadcasted_iota(jnp.int32, (n, n), 1)
    return ri >= ci, ri > ci, ri == ci


def _mixer_a_fwd(proj, ucol, w_s, b_col, g_v, name):
    t, w = proj.shape[0], g_v.shape[1]
    c = A_CHUNK

    def body(u_ref, v_ref, w_ref, b_ref, gv_ref, y_ref):
        tril, _, _ = _tri_masks(c)
        ug, vg = _gelu(u_ref[...].astype(F32)), _gelu(v_ref[...].astype(F32))
        for g in range(GROUPS):
            sl = slice(g * c, (g + 1) * c)
            vt = vg[:, sl]
            r = lax.rsqrt(jnp.mean(vt * vt, axis=-1, keepdims=True) + EPS)
            vn = vt * r * gv_ref[:, sl]
            s = _mm(jnp.where(tril, w_ref[g], 0.0), vn) + b_ref[g]
            y_ref[:, sl] = (ug[:, sl] * s).astype(y_ref.dtype)

    return pl.pallas_call(
        body, name=name, grid=(t // c,),
        in_specs=[pl.BlockSpec((c, w), lambda i: (i, ucol // w)), pl.BlockSpec((c, w), lambda i: (i, ucol // w + 1)),
                  pl.BlockSpec((GROUPS, c, c), lambda i: (0, 0, 0)), pl.BlockSpec((GROUPS, c, 1), lambda i: (0, 0, 0)),
                  pl.BlockSpec((1, w), lambda i: (0, 0))],
        out_specs=pl.BlockSpec((c, w), lambda i: (i, 0)), out_shape=jax.ShapeDtypeStruct((t, w), MXU_DTYPE),
        compiler_params=_params(("parallel",)))(proj, proj, w_s, b_col, g_v)


def _mixer_a_bwd(proj, ucol, dy, w_s, w_st, b_col, g_v, dproj, name):
    t, w = proj.shape[0], g_v.shape[1]
    w2 = 2 * w
    c = A_CHUNK

    def body(u_ref, v_ref, dy_ref, w_ref, wt_ref, b_ref, gv_ref, _, duv_ref, dw_ref, db_ref, dgv_ref):
        @pl.when(pl.program_id(0) == 0)
        def _():
            dw_ref[...] = jnp.zeros_like(dw_ref)
            db_ref[...] = jnp.zeros_like(db_ref)
            dgv_ref[...] = jnp.zeros_like(dgv_ref)

        tril, _, _ = _tri_masks(c)
        triu = lax.broadcasted_iota(jnp.int32, (c, c), 0) <= lax.broadcasted_iota(jnp.int32, (c, c), 1)
        (ug, dug), (vg, dvg) = _gelu_and_slope(u_ref[...].astype(F32)), _gelu_and_slope(v_ref[...].astype(F32))
        for g in range(GROUPS):
            sl = slice(g * c, (g + 1) * c)
            vt = vg[:, sl]
            r = lax.rsqrt(jnp.mean(vt * vt, axis=-1, keepdims=True) + EPS)
            vh = vt * r
            gv = gv_ref[:, sl]
            vn = vh * gv
            s = _mm(jnp.where(tril, w_ref[g], 0.0), vn) + b_ref[g]
            dy = dy_ref[:, sl]
            ds = dy * ug[:, sl]
            dw_ref[g] += jnp.where(tril, _mm(ds, vn, _NT), 0.0)
            db_ref[g] += jnp.sum(ds, axis=1, keepdims=True)
            dvn = _mm(jnp.where(triu, wt_ref[g], 0.0), ds)
            dgv_ref[:, sl] += jnp.sum(dvn * vh, axis=0, keepdims=True)
            dvh = dvn * gv
            dvt = r * (dvh - vh * jnp.mean(dvh * vh, axis=-1, keepdims=True))
            duv_ref[:, sl] = (dy * s * dug[:, sl]).astype(duv_ref.dtype)
            duv_ref[:, w + g * c:w + (g + 1) * c] = (dvt * dvg[:, sl]).astype(duv_ref.dtype)

    full3 = lambda shape: pl.BlockSpec(shape, lambda i: (0, 0, 0))
    return pl.pallas_call(
        body, name=name, grid=(t // c,),
        in_specs=[pl.BlockSpec((c, w), lambda i: (i, ucol // w)), pl.BlockSpec((c, w), lambda i: (i, ucol // w + 1)),
                  pl.BlockSpec((c, w), lambda i: (i, 0)), full3((GROUPS, c, c)), full3((GROUPS, c, c)),
                  full3((GROUPS, c, 1)), pl.BlockSpec((1, w), lambda i: (0, 0)), _ANY],
        out_specs=[pl.BlockSpec((c, w2), lambda i: (i, ucol // w2)), full3((GROUPS, c, c)), full3((GROUPS, c, 1)),
                   pl.BlockSpec((1, w), lambda i: (0, 0))],
        out_shape=[jax.ShapeDtypeStruct(dproj.shape, dproj.dtype), jax.ShapeDtypeStruct((GROUPS, c, c), F32),
                   jax.ShapeDtypeStruct((GROUPS, c, 1), F32), jax.ShapeDtypeStruct((1, w), F32)],
        input_output_aliases={7: 0},
        compiler_params=_params(("arbitrary",)))(proj, proj, dy, w_s, w_st, b_col, g_v, dproj)


_Q_SCALE = HEAD_DIM ** -0.5


CONV_HALO = 16


def _conv_taps(x_ref, p_ref, w_ref):
    prev = jnp.where(pl.program_id(0) > 0, p_ref[...].astype(F32), 0.0)
    ext = jnp.concatenate([prev, x_ref[...].astype(F32)], axis=0)
    shifted = [ext[CONV_HALO:]] + [pltpu.roll(ext, s, 0)[CONV_HALO:] for s in range(1, CONV_K)]
    acc = shifted[0] * w_ref[pl.ds(CONV_K - 1, 1), :]
    for s in range(1, CONV_K):
        acc = acc + shifted[s] * w_ref[pl.ds(CONV_K - 1 - s, 1), :]
    return acc, shifted


def _conv_fwd(qkv, w, name):
    t, cw = qkv.shape[0], w.shape[1]
    tt = _tile(t, (256, 128))
    hb = tt // CONV_HALO

    def body(x_ref, p_ref, w_ref, o_ref):
        acc, _ = _conv_taps(x_ref, p_ref, w_ref)
        y = _silu(acc)
        for which in range(3):
            for h in range(HEADS):
                lo = (which * HEADS + h) * HEAD_DIM
                seg = y[:, lo:lo + HEAD_DIM]
                if which < 2:
                    seg = seg * lax.rsqrt(jnp.sum(seg * seg, axis=-1, keepdims=True) + EPS)
                if which == 0:
                    seg = seg * _Q_SCALE
                o_ref[which, h] = seg

    return pl.pallas_call(
        body, name=name, grid=(t // tt,),
        in_specs=[pl.BlockSpec((tt, cw), lambda i: (i, 0)),
                  pl.BlockSpec((CONV_HALO, cw), lambda i: (jnp.maximum(i * hb - 1, 0), 0)),
                  pl.BlockSpec((CONV_K, cw), lambda i: (0, 0))],
        out_specs=pl.BlockSpec((3, HEADS, tt, HEAD_DIM), lambda i: (0, 0, i, 0)),
        out_shape=jax.ShapeDtypeStruct((3, HEADS, t, HEAD_DIM), F32),
        compiler_params=_params(("parallel",)))(qkv, qkv, w)


def _conv_bwd_pre(qkv, dq, dk, dv, w, name):
    t, cw = qkv.shape[0], w.shape[1]
    tt = _tile(t, (256, 128))
    hb = tt // CONV_HALO

    def body(x_ref, p_ref, dq_ref, dk_ref, dv_ref, w_ref, da_ref, dw_ref):
        @pl.when(pl.program_id(0) == 0)
        def _():
            dw_ref[...] = jnp.zeros_like(dw_ref)

        acc, shifted = _conv_taps(x_ref, p_ref, w_ref)
        sg = _sigmoid(acc)
        y = acc * sg
        dsilu = sg * (1.0 + acc * (1.0 - sg))
        d_refs = (dq_ref, dk_ref, dv_ref)
        for which in range(3):
            for h in range(HEADS):
                lo = (which * HEADS + h) * HEAD_DIM
                sl = slice(lo, lo + HEAD_DIM)
                dn = d_refs[which][h]
                if which < 2:
                    seg = y[:, sl]
                    rho = lax.rsqrt(jnp.sum(seg * seg, axis=-1, keepdims=True) + EPS)
                    nrm = seg * rho
                    if which == 0:
                        dn = dn * _Q_SCALE
                    dn = rho * (dn - nrm * jnp.sum(dn * nrm, axis=-1, keepdims=True))
                dacc = dn * dsilu[:, sl]
                da_ref[:, sl] = dacc
                for s in range(CONV_K):
                    dw_ref[pl.ds(CONV_K - 1 - s, 1), sl] += jnp.sum(dacc * shifted[s][:, sl], axis=0, keepdims=True)

    head = pl.BlockSpec((HEADS, tt, HEAD_DIM), lambda i: (0, i, 0))
    return pl.pallas_call(
        body, name=name, grid=(t // tt,),
        in_specs=[pl.BlockSpec((tt, cw), lambda i: (i, 0)),
                  pl.BlockSpec((CONV_HALO, cw), lambda i: (jnp.maximum(i * hb - 1, 0), 0)),
                  head, head, head, pl.BlockSpec((CONV_K, cw), lambda i: (0, 0))],
        out_specs=[pl.BlockSpec((tt, cw), lambda i: (i, 0)), pl.BlockSpec((CONV_K, cw), lambda i: (0, 0))],
        out_shape=[jax.ShapeDtypeStruct((t, cw), F32), jax.ShapeDtypeStruct((CONV_K, cw), F32)],
        compiler_params=_params(("arbitrary",)))(qkv, qkv, dq, dk, dv, w)


def _conv_bwd_in(dacc, w, dproj, name):
    t, cw = dacc.shape
    tt = _tile(t, (256, 128))
    hb = tt // SUBLANES
    nt = t // tt
    rows = tt + SUBLANES

    def body(d_ref, n_ref, w_ref, _, o_ref):
        cur = d_ref[...]
        nxt = jnp.where(pl.program_id(0) < nt - 1, n_ref[...], 0.0)
        ext = jnp.concatenate([cur, nxt], axis=0)
        acc = cur * w_ref[pl.ds(CONV_K - 1, 1), :]
        for s in range(1, CONV_K):
            acc = acc + pltpu.roll(ext, rows - s, 0)[:tt] * w_ref[pl.ds(CONV_K - 1 - s, 1), :]
        o_ref[...] = acc.astype(o_ref.dtype)

    return pl.pallas_call(
        body, name=name, grid=(nt,),
        in_specs=[pl.BlockSpec((tt, cw), lambda i: (i, 0)),
                  pl.BlockSpec((SUBLANES, cw), lambda i: (jnp.minimum((i + 1) * hb, t // SUBLANES - 1), 0)),
                  pl.BlockSpec((CONV_K, cw), lambda i: (0, 0)), _ANY],
        out_specs=pl.BlockSpec((tt, cw), lambda i: (i, 0)), out_shape=jax.ShapeDtypeStruct(dproj.shape, dproj.dtype),
        input_output_aliases={3: 0}, compiler_params=_params(("parallel",)))(dacc, dacc, w, dproj)


_INV_BASE_SHIFT = 3


def _inv_unit_lower(a, eye):
    c = GDN_CHUNK
    ri = lax.broadcasted_iota(jnp.int32, (c, c), 0)
    ci = lax.broadcasted_iota(jnp.int32, (c, c), 1)
    same = lambda sh: (ri >> sh) == (ci >> sh)
    x = jnp.where(same(_INV_BASE_SHIFT), -a, 0.0)
    p = jnp.where(eye, 1.0, 0.0) + x
    xs = _split(x)
    x2 = _mm3(xs, xs)
    x2s, ps = _split(x2), _split(p)
    r = _mm3(x2s, tuple(jnp.concatenate([u, v], axis=-1) for u, v in zip(x2s, ps)))
    x4, p = r[..., :c], p + r[..., c:]
    p = p + _mm3(_split(x4), _split(p))
    for sh in range(_INV_BASE_SHIFT, c.bit_length() - 1):
        off = jnp.where(same(sh + 1) & jnp.logical_not(same(sh)), a, 0.0)
        ps = _split(p)
        p = p - _mm3(ps, _split(_mm3(_split(off), ps)))
    return p


def _split(a):
    hi = a.astype(BF16)
    return hi, (a - hi.astype(F32)).astype(BF16)


def _dot_heads(u, v, dims):
    if u.ndim == 3:
        return jnp.stack([_dot_heads(u[j], v[j], dims) for j in range(u.shape[0])])
    return lax.dot_general(u, v, dims, preferred_element_type=F32)


def _mm3(a, b):
    return _dot_heads(a[0], b[0], _NN) + (_dot_heads(a[0], b[1], _NN) + _dot_heads(a[1], b[0], _NN))


def _hmm(a, b, dims=_NN):
    return _dot_heads(a.astype(MXU_DTYPE), b.astype(MXU_DTYPE), dims)


def _rowsum(x):
    return jnp.sum(x, axis=-1, keepdims=True)


def _colsum(x):
    return jnp.sum(x, axis=-2, keepdims=True)


class _Pre:
    pass


def _gdn_pre(q, k, v, araw, braw, alog, dtb, t_mat=None):
    c = GDN_CHUNK
    p = _Pre()
    p.tril, p.strict, p.eye = _tri_masks(c)
    p.to_col = lambda row: _rowsum(jnp.where(p.eye, row, 0.0))
    p.to_row = lambda col: _colsum(jnp.where(p.eye, col, 0.0))
    p.a_neg = -jnp.exp(alog + jnp.zeros((1, c), F32))
    p.xg = araw + dtb
    p.g_row = p.a_neg * _softplus(p.xg)
    p.beta_row = _sigmoid(braw)
    p.beta = p.to_col(p.beta_row)
    gam = _rowsum(jnp.where(p.tril, p.g_row, 0.0))
    gam_last = _rowsum(p.g_row)
    p.dm = jnp.where(p.tril, jnp.exp(jnp.where(p.tril, gam - p.to_row(gam), 0.0)), 0.0)
    p.e, p.ek, p.el = jnp.exp(gam), jnp.exp(gam_last - gam), jnp.exp(gam_last)
    p.kb = k * p.beta
    p.kk = _hmm(p.kb, k, _NT)
    p.t = _inv_unit_lower(jnp.where(p.strict, p.kk * p.dm, 0.0), p.eye) if t_mat is None else t_mat
    p.vb, p.kbe = v * p.beta, p.kb * p.e
    uw = _hmm(p.t, jnp.concatenate([p.vb, p.kbe], axis=-1))
    p.u, p.w = uw[..., :v.shape[-1]], uw[..., v.shape[-1]:]
    p.qk0 = _hmm(q, k, _NT)
    p.qk = p.qk0 * p.dm
    p.qd, p.kd = q * p.e, k * p.ek
    return p


GDN_HEADS_PER_STEP = 8


def _head_scalars(ref, hb):
    h0 = pl.program_id(0) * hb
    return jnp.stack([jnp.full((1, 1), ref[h0 + j], F32) for j in range(hb)])


def _gdn_specs(n, reverse):
    c, dk, hb = GDN_CHUNK, HEAD_DIM, GDN_HEADS_PER_STEP
    ix = (lambda i: n - 1 - i) if reverse else (lambda i: i)
    smem = pl.BlockSpec(memory_space=pltpu.SMEM)
    qkv = [pl.BlockSpec((None, hb, c, dk), functools.partial(lambda w, h, i: (w, h, ix(i), 0), w)) for w in range(3)]
    row = pl.BlockSpec((hb, None, 1, c), lambda h, i: (h, ix(i), 0, 0))
    tok = pl.BlockSpec((hb, c, dk), lambda h, i: (h, ix(i), 0))
    state = pl.BlockSpec((hb, None, dk, dk), lambda h, i: (h, ix(i), 0, 0))
    return smem, qkv, row, tok, state


def _gdn_fwd(qkv_h, araw, braw, alog, dtb, name, side=_NoSide):
    _, hh, t, dk = qkv_h.shape
    n, hb = t // GDN_CHUNK, GDN_HEADS_PER_STEP
    smem, qkv, row, tok, state = _gdn_specs(n, False)
    grid = (hh // hb, n)

    def body(*refs):
        main, side_start, side_finish = _side_hooks(side, refs, 7, 3, 1, grid)
        alog_ref, dt_ref, q_ref, k_ref, v_ref, a_ref, b_ref, o_ref, so_ref, to_ref, s_ref = main
        side_start()

        @pl.when(pl.program_id(1) == 0)
        def _():
            s_ref[...] = jnp.zeros_like(s_ref)

        p = _gdn_pre(q_ref[...], k_ref[...], v_ref[...], a_ref[...], b_ref[...],
                     _head_scalars(alog_ref, hb), _head_scalars(dt_ref, hb))
        s = s_ref[...]
        vn = p.u - _hmm(p.w, s)
        o_ref[...] = _hmm(p.qd, s) + _hmm(p.qk, vn)
        so_ref[...] = s
        to_ref[...] = p.t
        s_ref[...] = s * p.el + _hmm(p.kd, vn, _TN)
        side_finish()

    mats = jax.ShapeDtypeStruct((hh, n, dk, dk), F32)
    return _carrier_call(
        body, name, grid, [smem, smem] + qkv + [row, row], [tok, state, state],
        [jax.ShapeDtypeStruct((hh, t, dk), F32), mats, mats],
        [pltpu.VMEM((hb, dk, dk), F32)], side, (alog, dtb, qkv_h, qkv_h, qkv_h, araw, braw))


def _gdn_bwd(qkv_h, araw, braw, alog, dtb, states, t_mats, do, name, side=_NoSide):
    _, hh, t, dk = qkv_h.shape
    c, hb = GDN_CHUNK, GDN_HEADS_PER_STEP
    n = t // c
    smem, qkv, row, tok, state = _gdn_specs(n, True)
    acc = pl.BlockSpec((hb, 1, LANES), lambda h, i: (h, 0, 0))
    grid = (hh // hb, n)

    def body(*refs):
        main, side_start, side_finish = _side_hooks(side, refs, 10, 7, 1, grid)
        (alog_ref, dt_ref, q_ref, k_ref, v_ref, a_ref, b_ref, s_ref, t_ref, do_ref,
         dq_ref, dk_ref, dv_ref, da_ref, db_ref, dal_ref, ddt_ref, ds_ref) = main
        side_start()

        @pl.when(pl.program_id(1) == 0)
        def _():
            ds_ref[...] = jnp.zeros_like(ds_ref)
            dal_ref[...] = jnp.zeros_like(dal_ref)
            ddt_ref[...] = jnp.zeros_like(ddt_ref)

        q, k, v = q_ref[...], k_ref[...], v_ref[...]
        p = _gdn_pre(q, k, v, a_ref[...], b_ref[...], _head_scalars(alog_ref, hb), _head_scalars(dt_ref, hb),
                     t_ref[...])
        s, do, dsp = s_ref[...], do_ref[...], ds_ref[...]
        vn = p.u - _hmm(p.w, s)
        dqd = _hmm(do, s, _NT)
        dqk = _hmm(do, vn, _NT)
        dvn = _hmm(p.qk, do, _TN) + _hmm(p.kd, dsp)
        dkd = _hmm(vn, dsp, _NT)
        d_el = _colsum(_rowsum(s * dsp))
        ds_ref[...] = dsp * p.el + _hmm(p.qd, do, _TN) - _hmm(p.w, dvn, _TN)
        dw = -_hmm(dvn, s, _NT)
        d_t = _hmm(dvn, p.vb, _NT) + _hmm(dw, p.kbe, _NT)
        dvb, dkbe = _hmm(p.t, dvn, _TN), _hmm(p.t, dw, _TN)
        d_a = jnp.where(p.strict, -_hmm(p.t, _hmm(d_t, p.t, _NT), _TN), 0.0)
        dkk = d_a * p.dm
        dqk0 = dqk * p.dm
        ddm = d_a * p.kk + dqk * p.qk0
        dkb = _hmm(dkk, k) + dkbe * p.e
        dq_ref[...] = _hmm(dqk0, k) + dqd * p.e
        dk_ref[...] = _hmm(dkk, p.kb, _TN) + _hmm(dqk0, q, _TN) + dkd * p.ek + dkb * p.beta
        dv_ref[...] = dvb * p.beta
        dbeta = _rowsum(dkb * k) + _rowsum(dvb * v)
        d_e = _rowsum(dqd * q) + _rowsum(dkbe * p.kb)
        d_ek = _rowsum(dkd * k)
        m = ddm * p.dm
        dgam = d_e * p.e - d_ek * p.ek + _rowsum(m) - p.to_col(_colsum(m))
        dgam_last = _colsum(d_ek * p.ek) + d_el * p.el
        dg_row = _colsum(jnp.where(p.tril, dgam, 0.0)) + dgam_last
        da_row = dg_row * p.a_neg * _sigmoid(p.xg)
        da_ref[...] = da_row
        db_ref[...] = p.to_row(dbeta) * p.beta_row * (1.0 - p.beta_row)
        dal_ref[...] += _rowsum(dg_row * p.g_row)
        ddt_ref[...] += _rowsum(da_row)
        side_finish()

    tok_shape = jax.ShapeDtypeStruct((hh, t, dk), F32)
    row_shape = jax.ShapeDtypeStruct((hh, n, 1, c), F32)
    acc_shape = jax.ShapeDtypeStruct((hh, 1, LANES), F32)
    return _carrier_call(
        body, name, grid, [smem, smem] + qkv + [row, row, state, state, tok], [tok, tok, tok, row, row, acc, acc],
        [tok_shape, tok_shape, tok_shape, row_shape, row_shape, acc_shape, acc_shape],
        [pltpu.VMEM((hb, dk, dk), F32)], side, (alog, dtb, qkv_h, qkv_h, qkv_h, araw, braw, states, t_mats, do))


def _gdn_post_fwd(o, proj, zcol, g_o, name):
    hh, t, dv = o.shape
    tt = _tile(t, _ROW_TILES)
    zblk = zcol // (hh * dv)

    def body(o_ref, z_ref, g_ref, y_ref):
        for h in range(hh):
            sl = slice(h * dv, (h + 1) * dv)
            ov = o_ref[h]
            r = lax.rsqrt(jnp.mean(ov * ov, axis=-1, keepdims=True) + EPS)
            y_ref[:, sl] = (ov * r * g_ref[...] * _silu(z_ref[:, sl].astype(F32))).astype(y_ref.dtype)

    return pl.pallas_call(
        body, name=name, grid=(t // tt,),
        in_specs=[pl.BlockSpec((hh, tt, dv), lambda i: (0, i, 0)), pl.BlockSpec((tt, hh * dv), lambda i: (i, zblk)),
                  pl.BlockSpec((1, dv), lambda i: (0, 0))],
        out_specs=pl.BlockSpec((tt, hh * dv), lambda i: (i, 0)),
        out_shape=jax.ShapeDtypeStruct((t, hh * dv), MXU_DTYPE), compiler_params=_params(("parallel",)))(o, proj, g_o)


def _gdn_post_bwd(o, proj, zcol, dy, g_o, dproj, name):
    hh, t, dv = o.shape
    tt = _tile(t, _ROW_TILES)
    zblk = zcol // (hh * dv)

    def body(o_ref, z_ref, dy_ref, g_ref, _, do_ref, dz_ref, dg_ref):
        @pl.when(pl.program_id(0) == 0)
        def _():
            dg_ref[...] = jnp.zeros_like(dg_ref)

        gv = g_ref[...]
        for h in range(hh):
            sl = slice(h * dv, (h + 1) * dv)
            ov, zz, dy = o_ref[h], z_ref[:, sl].astype(F32), dy_ref[:, sl]
            r = lax.rsqrt(jnp.mean(ov * ov, axis=-1, keepdims=True) + EPS)
            oh = ov * r
            sg = _sigmoid(zz)
            dz_ref[:, sl] = (dy * oh * gv * (sg * (1.0 + zz * (1.0 - sg)))).astype(dz_ref.dtype)
            don = dy * (zz * sg)
            dg_ref[...] += _colsum(don * oh)
            doh = don * gv
            do_ref[h] = r * (doh - oh * jnp.mean(doh * oh, axis=-1, keepdims=True))

    return pl.pallas_call(
        body, name=name, grid=(t // tt,),
        in_specs=[pl.BlockSpec((hh, tt, dv), lambda i: (0, i, 0)), pl.BlockSpec((tt, hh * dv), lambda i: (i, zblk)),
                  pl.BlockSpec((tt, hh * dv), lambda i: (i, 0)), pl.BlockSpec((1, dv), lambda i: (0, 0)), _ANY],
        out_specs=[pl.BlockSpec((hh, tt, dv), lambda i: (0, i, 0)), pl.BlockSpec((tt, hh * dv), lambda i: (i, zblk)),
                   pl.BlockSpec((1, dv), lambda i: (0, 0))],
        out_shape=[jax.ShapeDtypeStruct((hh, t, dv), F32), jax.ShapeDtypeStruct(dproj.shape, dproj.dtype),
                   jax.ShapeDtypeStruct((1, dv), F32)],
        input_output_aliases={4: 1}, compiler_params=_params(("arbitrary",)))(o, proj, dy, g_o, dproj)


def _cols_as_rows(x, col, name):
    t = x.shape[0]
    tt = _tile(t, _ROW_TILES)

    def body(x_ref, o_ref):
        o_ref[...] = x_ref[...].T

    return pl.pallas_call(
        body, name=name, grid=(t // tt,), in_specs=[pl.BlockSpec((tt, LANES), lambda i: (i, col // LANES))],
        out_specs=pl.BlockSpec((LANES, tt), lambda i: (0, i)), out_shape=jax.ShapeDtypeStruct((LANES, t), x.dtype),
        compiler_params=_params(("parallel",)))(x)


def _rows_into_cols(dst, rows, col, name):
    t = dst.shape[0]
    tt = _tile(t, _ROW_TILES)

    def body(r_ref, _, o_ref):
        o_ref[...] = r_ref[...].T.astype(o_ref.dtype)

    return pl.pallas_call(
        body, name=name, grid=(t // tt,), in_specs=[pl.BlockSpec((LANES, tt), lambda i: (0, i)), _ANY],
        out_specs=pl.BlockSpec((tt, LANES), lambda i: (i, col // LANES)),
        out_shape=jax.ShapeDtypeStruct(dst.shape, dst.dtype), input_output_aliases={1: 0},
        compiler_params=_params(("parallel",)))(rows, dst)


def _adamw(g, w, m, v):
    m = ADAM_B1 * m + (1.0 - ADAM_B1) * g
    v = ADAM_B2 * v + (1.0 - ADAM_B2) * (g * g)
    m_hat = m / (1.0 - ADAM_B1 ** ADAM_STEP)
    v_hat = v / (1.0 - ADAM_B2 ** ADAM_STEP)
    return -ADAM_LR * (m_hat / (jnp.sqrt(v_hat) + ADAM_EPS) + ADAM_WD * w), m, v


def _ada_fwd(c_all, ada_w, name):
    nl, d, cols = ada_w.shape
    b = c_all.shape[0]

    def body(c_ref, w_ref, o_ref):
        o_ref[...] = _mm_hi(_silu(c_ref[...]), w_ref[...])

    return pl.pallas_call(
        body, name=name, grid=(nl,),
        in_specs=[pl.BlockSpec((b, d), lambda i: (0, 0)), pl.BlockSpec((None, d, cols), lambda i: (i, 0, 0))],
        out_specs=pl.BlockSpec((None, b, cols), lambda i: (i, 0, 0)),
        out_shape=jax.ShapeDtypeStruct((nl, b, cols), F32), compiler_params=_params(("parallel",)))(c_all, ada_w)


def _ada_bwd(c_col, dm, w, m, v, name):
    nl, d, cols = w.shape
    b = c_col.shape[0]
    tr = _tile(d, (256, 128))

    def body(c_ref, dm_ref, w_ref, m_ref, v_ref, g_ref, dl_ref, mo_ref, vo_ref):
        g = _silu(c_ref[0]) * dm_ref[pl.ds(0, 1), :]
        for j in range(1, b):
            g = g + _silu(c_ref[j]) * dm_ref[pl.ds(j, 1), :]
        g_ref[...] = g
        dl_ref[...], mo_ref[...], vo_ref[...] = _adamw(g, w_ref[...], m_ref[...], v_ref[...])

    blk = pl.BlockSpec((None, tr, cols), lambda l, i: (l, i, 0))
    shape = jax.ShapeDtypeStruct((nl, d, cols), F32)
    return pl.pallas_call(
        body, name=name, grid=(nl, d // tr),
        in_specs=[pl.BlockSpec((b, tr, 1), lambda l, i: (0, i, 0)), pl.BlockSpec((None, b, cols), lambda l, i: (l, 0, 0)),
                  blk, blk, blk],
        out_specs=[blk, blk, blk, blk], out_shape=[shape] * 4,
        compiler_params=_params(("parallel", "parallel")))(c_col, dm, w, m, v)


def _sum_adam(parts, w, m, v, name):
    nl, npart, r, cdim = parts.shape
    tr = _tile(r, (256, 128))

    def body(p_ref, w_ref, m_ref, v_ref, g_ref, dl_ref, mo_ref, vo_ref):
        g = p_ref[0].astype(F32)
        for j in range(1, npart):
            g = g + p_ref[j].astype(F32)
        g_ref[...] = g
        dl_ref[...], mo_ref[...], vo_ref[...] = _adamw(g, w_ref[...], m_ref[...], v_ref[...])

    blk = pl.BlockSpec((None, tr, cdim), lambda l, i: (l, i, 0))
    shape = jax.ShapeDtypeStruct((nl, r, cdim), F32)
    return pl.pallas_call(
        body, name=name, grid=(nl, r // tr),
        in_specs=[pl.BlockSpec((None, npart, tr, cdim), lambda l, i: (l, 0, i, 0)), blk, blk, blk],
        out_specs=[blk, blk, blk, blk], out_shape=[shape] * 4,
        compiler_params=_params(("parallel", "parallel")))(parts, w, m, v)


def _cols_from_blocks(g, plan, width, name):
    _, r, cdim = g.shape
    tr = _tile(r, (256, 128))
    covered = sorted((dst, dst + n) for _, _, n, dst in plan)
    holes, pos = [], 0
    for a, b in covered:
        if a > pos:
            holes.append((pos, a))
        pos = max(pos, b)
    if pos < width:
        holes.append((pos, width))

    def body(g_ref, o_ref):
        for a, b in holes:
            o_ref[:, a:b] = jnp.zeros((tr, b - a), o_ref.dtype)
        for j, src, n, dst in plan:
            o_ref[:, dst:dst + n] = g_ref[j, :, src:src + n]

    return pl.pallas_call(
        body, name=name, grid=(r // tr,), in_specs=[pl.BlockSpec((N_DEV, tr, cdim), lambda i: (0, i, 0))],
        out_specs=pl.BlockSpec((tr, width), lambda i: (i, 0)), out_shape=jax.ShapeDtypeStruct((r, width), g.dtype),
        compiler_params=_params(("parallel",)))(g)


def _blocks_from_cols(w, plan, cdim, name):
    r, width = w.shape
    tr = _tile(r, (256, 128))

    def body(w_ref, o_ref):
        for j, src, n, dst in plan:
            o_ref[j, :, src:src + n] = w_ref[:, dst:dst + n]

    return pl.pallas_call(
        body, name=name, grid=(r // tr,), in_specs=[pl.BlockSpec((tr, width), lambda i: (i, 0))],
        out_specs=pl.BlockSpec((N_DEV, tr, cdim), lambda i: (0, i, 0)),
        out_shape=jax.ShapeDtypeStruct((N_DEV, r, cdim), w.dtype), compiler_params=_params(("parallel",)))(w)


def _pair_sum(x, tmp, core, name):
    _, r, cdim = x.shape
    tr = _tile(r, (256, 128))

    def body(core_ref, x_ref, t_ref, o_ref):
        o_ref[...] = (x_ref[...] + t_ref[...]).astype(o_ref.dtype)

    grid_spec = pltpu.PrefetchScalarGridSpec(
        num_scalar_prefetch=1, grid=(N_DEV // 2, r // tr),
        in_specs=[pl.BlockSpec((None, tr, cdim), lambda ch, i, core_ref: (2 * ch + core_ref[0], i, 0)),
                  pl.BlockSpec((None, tr, cdim), lambda ch, i, core_ref: (ch, i, 0))],
        out_specs=pl.BlockSpec((None, tr, cdim), lambda ch, i, core_ref: (ch, i, 0)))
    return pl.pallas_call(
        body, name=name, grid_spec=grid_spec, out_shape=jax.ShapeDtypeStruct((N_DEV // 2, r, cdim), WIRE_DTYPE),
        compiler_params=_params(("parallel", "parallel")))(core, x, tmp)


_ANY = pl.BlockSpec(memory_space=pl.ANY)
_CHIP_FLIPS = ((1, 0), (0, 1), (1, 1))


def _coords():
    return lax.axis_index("x"), lax.axis_index("y"), lax.axis_index("c")


def _flip(v, f):
    return 1 - v if f else v


def _a2a_direct(xs, name):
    n, ncp = len(xs), N_DEV - 1

    def body(*refs):
        ins, outs = refs[:n], refs[n:2 * n]
        send, recv, loc = refs[2 * n:]
        x, y, c = _coords()
        me = 4 * x + 2 * y + c
        local = [pltpu.make_async_copy(ins[i].at[me], outs[i].at[me], loc.at[i]) for i in range(n)]
        for cp in local:
            cp.start()
        remote = []
        for i in range(n):
            for k in range(1, N_DEV):
                px, py, pc = _flip(x, k & 4), _flip(y, k & 2), _flip(c, k & 1)
                cp = pltpu.make_async_remote_copy(
                    src_ref=ins[i].at[4 * px + 2 * py + pc], dst_ref=outs[i].at[me],
                    send_sem=send.at[i * ncp + k - 1], recv_sem=recv.at[i * ncp + k - 1],
                    device_id=(px, py, pc), device_id_type=MESH)
                cp.start()
                remote.append(cp)
        for cp in remote:
            cp.wait()
        for cp in local:
            cp.wait()

    return pl.pallas_call(
        body, name=name, in_specs=[_ANY] * n, out_specs=[_ANY] * n,
        out_shape=[jax.ShapeDtypeStruct(a.shape, a.dtype) for a in xs],
        scratch_shapes=[pltpu.SemaphoreType.DMA((n * ncp,)), pltpu.SemaphoreType.DMA((n * ncp,)),
                        pltpu.SemaphoreType.DMA((n,))])(*xs)


class _AllGatherSide:
    def __init__(self, blocks):
        self.operands = list(blocks)
        n = self.n = len(self.operands)
        self.n_in = self.n_out = n
        self.out_shape = [jax.ShapeDtypeStruct((N_DEV,) + a.shape, a.dtype) for a in self.operands]
        self.aliases = {}
        nici, nd2d = len(_CHIP_FLIPS), N_DEV // 2
        self.scratch = [pltpu.SemaphoreType.DMA((n * nici,)), pltpu.SemaphoreType.DMA((n * nici,)),
                        pltpu.SemaphoreType.DMA((n * nd2d,)), pltpu.SemaphoreType.DMA((n * nd2d,)),
                        pltpu.SemaphoreType.DMA((n,))]

    def _first(self, ins, outs, sems):
        send, recv, _, _, loc = sems
        x, y, c = _coords()
        me = 4 * x + 2 * y + c
        nici = len(_CHIP_FLIPS)
        local = [pltpu.make_async_copy(ins[i], outs[i].at[me], loc.at[i]) for i in range(self.n)]
        remote = [pltpu.make_async_remote_copy(
            src_ref=ins[i], dst_ref=outs[i].at[me], send_sem=send.at[i * nici + j], recv_sem=recv.at[i * nici + j],
            device_id=(_flip(x, fx), _flip(y, fy), c), device_id_type=MESH)
            for i in range(self.n) for j, (fx, fy) in enumerate(_CHIP_FLIPS)]
        return local + remote

    def _second(self, outs, sems):
        _, _, send, recv, _ = sems
        x, y, c = _coords()
        nd2d = N_DEV // 2
        return [pltpu.make_async_remote_copy(
            src_ref=outs[i].at[2 * ch + c], dst_ref=outs[i].at[2 * ch + c], send_sem=send.at[i * nd2d + ch],
            recv_sem=recv.at[i * nd2d + ch], device_id=(x, y, 1 - c), device_id_type=MESH)
            for i in range(self.n) for ch in range(nd2d)]

    def start(self, ins, outs, sems):
        for cp in self._first(ins, outs, sems):
            cp.start()

    def finish(self, ins, outs, sems):
        for cp in self._first(ins, outs, sems):
            cp.wait()
        second = self._second(outs, sems)
        for cp in second:
            cp.start()
        for cp in second:
            cp.wait()


class _ReduceScatterIciSide:
    def __init__(self, sums, accs, layer):
        self.operands = list(sums) + list(accs)
        n = self.n = len(sums)
        self.layer = layer
        self.n_in, self.n_out = 2 * n, n
        self.out_shape = [jax.ShapeDtypeStruct(a.shape, a.dtype) for a in accs]
        self.aliases = {n + i: i for i in range(n)}
        nici = len(_CHIP_FLIPS)
        self.scratch = [pltpu.SemaphoreType.DMA((n * nici,)), pltpu.SemaphoreType.DMA((n * nici,)),
                        pltpu.SemaphoreType.DMA((n,))]

    def _copies(self, ins, outs, sems):
        send, recv, loc = sems
        x, y, c = _coords()
        chip = 2 * x + y
        nici = len(_CHIP_FLIPS)
        local = [pltpu.make_async_copy(ins[i].at[chip], outs[i].at[self.layer, chip], loc.at[i])
                 for i in range(self.n)]
        remote = [pltpu.make_async_remote_copy(
            src_ref=ins[i].at[2 * _flip(x, fx) + _flip(y, fy)], dst_ref=outs[i].at[self.layer, chip],
            send_sem=send.at[i * nici + j], recv_sem=recv.at[i * nici + j],
            device_id=(_flip(x, fx), _flip(y, fy), c), device_id_type=MESH)
            for i in range(self.n) for j, (fx, fy) in enumerate(_CHIP_FLIPS)]
        return local + remote

    def start(self, ins, outs, sems):
        for cp in self._copies(ins, outs, sems):
            cp.start()

    def finish(self, ins, outs, sems):
        for cp in self._copies(ins, outs, sems):
            cp.wait()


def _run_side(side, name):
    def body(*refs):
        ins, outs = refs[:side.n_in], refs[side.n_in:side.n_in + side.n_out]
        sems = refs[side.n_in + side.n_out:]
        side.start(ins, outs, sems)
        side.finish(ins, outs, sems)

    return pl.pallas_call(
        body, name=name, in_specs=[_ANY] * side.n_in, out_specs=[_ANY] * side.n_out, out_shape=side.out_shape,
        input_output_aliases=side.aliases, scratch_shapes=side.scratch)(*side.operands)


class _ReduceScatterD2dSide:
    def __init__(self, parts):
        self.operands = list(parts)
        n = self.n = len(self.operands)
        self.n_in = self.n_out = n
        nd2d = N_DEV // 2
        self.out_shape = [jax.ShapeDtypeStruct((nd2d,) + a.shape[1:], a.dtype) for a in self.operands]
        self.aliases = {}
        self.scratch = [pltpu.SemaphoreType.DMA((n * nd2d,)), pltpu.SemaphoreType.DMA((n * nd2d,))]

    def _copies(self, ins, outs, sems):
        send, recv = sems
        x, y, c = _coords()
        nd2d = N_DEV // 2
        return [pltpu.make_async_remote_copy(
            src_ref=ins[i].at[2 * ch + 1 - c], dst_ref=outs[i].at[ch], send_sem=send.at[i * nd2d + ch],
            recv_sem=recv.at[i * nd2d + ch], device_id=(x, y, 1 - c), device_id_type=MESH)
            for i in range(self.n) for ch in range(nd2d)]

    def start(self, ins, outs, sems):
        for cp in self._copies(ins, outs, sems):
            cp.start()

    def finish(self, ins, outs, sems):
        for cp in self._copies(ins, outs, sems):
            cp.wait()


_PACK_ROWS = 256


def _pack(arrs):
    flat = jnp.concatenate([a.reshape(-1) for a in arrs])
    quantum = _PACK_ROWS * LANES
    total = -(-flat.shape[0] // quantum) * quantum
    return jnp.pad(flat, (0, total - flat.shape[0])).reshape(-1, LANES)


def _unpack(packed, like):
    flat, out, pos = packed.reshape(-1), [], 0
    for a in like:
        out.append(flat[pos:pos + a.size].reshape(a.shape))
        pos += a.size
    return out


def kernel(x, c, ada_w, ada_b, norm1_g, w_in, conv_w, spatial_w, spatial_b, v_norm_g, a_log, dt_bias, o_norm_g, w_branch_a, w_branch_b, w_out, norm2_g, w_ffn_in, w_ffn_out, final_g, loss_target, m_ada_w, m_ada_b, m_norm1_g, m_w_in, m_conv_w, m_spatial_w, m_spatial_b, m_v_norm_g, m_a_log, m_dt_bias, m_o_norm_g, m_w_branch_a, m_w_branch_b, m_w_out, m_norm2_g, m_w_ffn_in, m_w_ffn_out, m_final_g, v_ada_w, v_ada_b, v_norm1_g, v_w_in, v_conv_w, v_spatial_w, v_spatial_b, v_v_norm_g, v_a_log, v_dt_bias, v_o_norm_g, v_w_branch_a, v_w_branch_b, v_w_out, v_norm2_g, v_w_ffn_in, v_w_ffn_out, v_final_g):
    nl, d = ada_w.shape[0], x.shape[2]
    t = x.shape[1]
    nchunk = t // GDN_CHUNK
    xi, yi, ci = _coords()
    me = 4 * xi + 2 * yi + ci
    core = jnp.reshape(ci, (1,)).astype(jnp.int32)
    x0, target = x[0], loss_target[0]
    wcols = 3 * HEADS * HEAD_DIM
    lay = _ProjLayout(d)
    in_pieces = lay.pieces(w_in.shape[2])
    fi_shard = w_ffn_in.shape[2]
    fi_pieces = [(j, 0, fi_shard, fi_shard * j) for j in range(N_DEV)]

    c_all, cw_all = _a2a_direct([jnp.broadcast_to(c[None], (N_DEV,) + c.shape),
                                 jnp.broadcast_to(conv_w[None], (N_DEV,) + conv_w.shape)], "gather_small")
    c_all = c_all[:, 0]
    conv_full = cw_all.transpose(1, 2, 0, 3).reshape(nl, CONV_K, wcols)
    modp = _ada_fwd(c_all, ada_w, "ada_fwd")
    (modx,) = _a2a_direct([modp.transpose(1, 0, 2)], "mod_exchange")
    mod = (modx.transpose(1, 0, 2).reshape(nl, 6 * d) + ada_b).reshape(nl, 6, 1, d)

    big = (w_in, w_branch_a, w_branch_b, w_out, w_ffn_in, w_ffn_out)
    big_wire = [w.astype(WIRE_DTYPE) for w in big]
    gather_in = lambda i: _AllGatherSide([big_wire[0][i]])
    gather_early = lambda i: _AllGatherSide([big_wire[k][i] for k in (1, 2, 3, 5)])
    gather_late = lambda i: _AllGatherSide([big_wire[4][i]] + ([big_wire[0][i + 1]] if i + 1 < nl else []))
    row_full = lambda g: g.reshape(-1, g.shape[2])
    padded_in = lambda g: _cols_from_blocks(g, in_pieces, lay.width, "w_in_cols")
    w_pads = [padded_in(_run_side(gather_in(0), "ag_first")[0])] + [None] * (nl - 1)
    weights = [None] * nl

    def rows_of(ba_rows, lo):
        return ba_rows[lo:lo + HEADS].reshape(HEADS, nchunk, 1, GDN_CHUNK)

    saved = []
    x_cur, delta, gt_prev = x0, None, None
    for i in range(nl):
        sh1, sc1, gt1, sh2, sc2, gt2 = (mod[i, k] for k in range(6))
        s = dict(gt1=gt1, gt2=gt2, sc1=sc1, sc2=sc2)
        s["x_in"], s["h"] = _resid_norm(x_cur, delta, gt_prev, norm1_g[i][None], sc1, sh1, "norm1_fwd")
        s["proj"], g_a, g_b, g_o, g_fo = _matmul(s["h"], w_pads[i], "nn", "proj_fwd", out_dtype=ACT_DTYPE,
                                                 side=gather_early(i))
        ba = _matmul(s["h"], w_pads[i][:, lay.ba:], "nn", "proj_ba_fwd")
        s["b_col"] = spatial_b[i][:, :, None]
        s["ya"] = _mixer_a_fwd(s["proj"], lay.uv, spatial_w[i], s["b_col"], v_norm_g[i][None], "mixer_a_fwd")
        s["qkv_h"] = _conv_fwd(s["proj"], conv_full[i], "conv_fwd")
        ba_rows = _cols_as_rows(ba, 0, "ba_rows")
        s["braw"], s["araw"] = rows_of(ba_rows, 0), rows_of(ba_rows, HEADS)
        s["o"], s["states"], s["t_mats"], g_fi, *g_in = _gdn_fwd(
            s["qkv_h"], s["araw"], s["braw"], a_log[i], dt_bias[i], "gdn_fwd", gather_late(i))
        if g_in:
            w_pads[i + 1] = padded_in(g_in[0])
        weights[i] = (row_full(g_a), row_full(g_b), row_full(g_o),
                      _cols_from_blocks(g_fi, fi_pieces, N_DEV * fi_shard, "w_ffn_in_cols"), row_full(g_fo))
        w_a, w_b, w_o, w_fi, w_fo = weights[i]
        s["yb"] = _gdn_post_fwd(s["o"], s["proj"], lay.z, o_norm_g[i][None], "gdn_post_fwd")
        s["pa"] = _matmul(s["ya"], w_a, "nn", "branch_a_fwd")
        s["pb"] = _matmul(s["yb"], w_b, "nn", "branch_b_fwd")
        s["merged"] = _merge_fwd(s["pa"], s["pb"], s["proj"], lay.gates, "merge_fwd")
        s["mo"] = _matmul(s["merged"], w_o, "nn", "out_fwd")
        s["x1"], s["h2"] = _resid_norm(s["x_in"], s["mo"], gt1, norm2_g[i][None], sc2, sh2, "norm2_fwd")
        s["gu"] = _matmul(s["h2"], w_fi, "nn", "ffn_in_fwd", out_dtype=ACT_DTYPE)
        s["a"] = _swiglu_fwd(s["gu"], "swiglu_fwd")
        s["fo"] = _matmul(s["a"], w_fo, "nn", "ffn_out_fwd")
        saved.append(s)
        x_cur, delta, gt_prev = s["x1"], s["fo"], gt2
    dx, d_final_g, loss_tile = _final_loss(x_cur, delta, gt_prev, final_g[None], target, "final_loss")
    loss = lax.psum(loss_tile[0, 0], ("x", "y", "c"))

    big_shapes = [(d, w_in.shape[2]), w_branch_a.shape[1:], w_branch_b.shape[1:], w_out.shape[1:],
                  (d, w_ffn_in.shape[2]), w_ffn_out.shape[1:]]
    accs = [lax.empty((nl, N_DEV // 2) + tuple(sh), WIRE_DTYPE) for sh in big_shapes]
    row_blocks = lambda g: g.reshape(N_DEV, -1, g.shape[1])
    dmod, small = [None] * nl, [None] * nl
    d_conv = [None] * nl
    parts, sums = None, None
    beside_gdn, beside_dw, beside_dx = (0,), (4,), (1, 2, 3, 5)
    rep_parts = [None] * nl
    rep_pack = lambda i: _pack((dmod[i],) + small[i])

    def scatter_side(idx, layer):
        if sums is None:
            return _NoSide
        return _ReduceScatterIciSide([sums[k] for k in idx], [accs[k] for k in idx], layer)

    def scattered_into(accs, idx, new):
        accs = list(accs)
        for k, a in zip(idx, new):
            accs[k] = a
        return accs

    for i in reversed(range(nl)):
        s = saved[i]
        w_a, w_b, w_o, w_fi, w_fo = weights[i]
        dfo, dgt2 = _gate_bwd(dx, s["fo"], s["gt2"], "gate2_bwd")
        g_fo = _matmul(s["a"], dfo, "tn", "ffn_out_dw")
        da = _matmul(dfo, w_fo, "nt", "ffn_out_dx")
        dgu = _swiglu_bwd(s["gu"], da, "swiglu_bwd")
        if parts is None:
            g_fi = _matmul(s["h2"], dgu, "tn", "ffn_in_dw")
        else:
            g_fi, *other = _matmul(s["h2"], dgu, "tn", "ffn_in_dw", side=_ReduceScatterD2dSide(parts))
            sums = [_pair_sum(p, o, core, "rs_pair_sum_%d" % k) for k, (p, o) in enumerate(zip(parts, other))]
        if i + 1 < nl:
            dh2, rep_parts[i + 1] = _matmul(dgu, w_fi, "nt", "ffn_in_dx", side=_AllGatherSide([rep_pack(i + 1)]))
        else:
            dh2 = _matmul(dgu, w_fi, "nt", "ffn_in_dx")
        dx1, dsh2, dsc2, dg2 = _norm_bwd(s["x1"], dh2, dx, norm2_g[i][None], s["sc2"], "norm2_bwd")
        dmo, dgt1 = _gate_bwd(dx1, s["mo"], s["gt1"], "gate1_bwd")
        g_o = _matmul(s["merged"], dmo, "tn", "out_dw")
        dmerged = _matmul(dmo, w_o, "nt", "out_dx")
        dproj = lax.empty((t, lay.width), MXU_DTYPE)
        dpa, dpb, dproj = _merge_bwd(dmerged, s["pa"], s["pb"], s["proj"], lay.gates, dproj, "merge_bwd")
        g_a = _matmul(s["ya"], dpa, "tn", "branch_a_dw")
        dya = _matmul(dpa, w_a, "nt", "branch_a_dx")
        g_b = _matmul(s["yb"], dpb, "tn", "branch_b_dw")
        dyb = _matmul(dpb, w_b, "nt", "branch_b_dx")
        dproj, d_ws, d_bs, d_gv = _mixer_a_bwd(s["proj"], lay.uv, dya, spatial_w[i], jnp.swapaxes(spatial_w[i], 1, 2),
                                               s["b_col"], v_norm_g[i][None], dproj, "mixer_a_bwd")
        do, dproj, d_go = _gdn_post_bwd(s["o"], s["proj"], lay.z, dyb, o_norm_g[i][None], dproj, "gdn_post_bwd")
        dq, dk, dv, d_ar, d_br, d_al, d_dt, *scattered = _gdn_bwd(
            s["qkv_h"], s["araw"], s["braw"], a_log[i], dt_bias[i], s["states"], s["t_mats"], do, "gdn_bwd",
            scatter_side(beside_gdn, i + 1))
        accs = scattered_into(accs, beside_gdn, scattered)
        dacc, d_conv[i] = _conv_bwd_pre(s["proj"], dq, dk, dv, conv_full[i], "conv_bwd_pre")
        dproj = _conv_bwd_in(dacc, conv_full[i], dproj, "conv_bwd_in")
        dba_rows = jnp.pad(jnp.concatenate([d_br.reshape(HEADS, t), d_ar.reshape(HEADS, t)]),
                           ((0, LANES - 2 * HEADS), (0, 0)))
        dproj = _rows_into_cols(dproj, dba_rows, lay.ba, "dproj_ba")
        if sums is None:
            g_pad = _matmul(s["h"], dproj, "tn", "proj_dw")
            dh = _matmul(dproj, w_pads[i], "nt", "proj_dx")
        else:
            g_pad, *scattered = _matmul(s["h"], dproj, "tn", "proj_dw", side=scatter_side(beside_dw, i + 1))
            accs = scattered_into(accs, beside_dw, scattered)
            dh, *scattered = _matmul(dproj, w_pads[i], "nt", "proj_dx", side=scatter_side(beside_dx, i + 1))
            accs = scattered_into(accs, beside_dx, scattered)
        dx, dsh1, dsc1, dg1 = _norm_bwd(s["x_in"], dh, dx1, norm1_g[i][None], s["sc1"], "norm1_bwd")
        dmod[i] = jnp.concatenate([dsh1, dsc1, dgt1, dsh2, dsc2, dgt2], axis=1)[0]
        small[i] = (dg1[0], d_ws, d_bs[:, :, 0], d_gv[0], d_al[:, 0, 0], d_dt[:, 0, 0], d_go[0], dg2[0])
        parts = [_blocks_from_cols(g_pad, in_pieces, w_in.shape[2], "w_in_blocks"), row_blocks(g_a), row_blocks(g_b),
                 row_blocks(g_o), _blocks_from_cols(g_fi, fi_pieces, fi_shard, "w_ffn_in_blocks"), row_blocks(g_fo)]
    other = _run_side(_ReduceScatterD2dSide(parts), "rs_d2d_last")
    sums = [_pair_sum(p, o, core, "rs_pair_sum_%d" % k) for k, (p, o) in enumerate(zip(parts, other))]
    accs = _run_side(_ReduceScatterIciSide(sums, accs, 0), "rs_ici_last")

    rep_w = (ada_b, norm1_g, spatial_w, spatial_b, v_norm_g, a_log, dt_bias, o_norm_g, norm2_g)
    rep_m = (m_ada_b, m_norm1_g, m_spatial_w, m_spatial_b, m_v_norm_g, m_a_log, m_dt_bias, m_o_norm_g, m_norm2_g)
    rep_v = (v_ada_b, v_norm1_g, v_spatial_w, v_spatial_b, v_v_norm_g, v_a_log, v_dt_bias, v_o_norm_g, v_norm2_g)
    rep_parts[0], fin_parts = _run_side(_AllGatherSide([rep_pack(0), _pack([d_final_g[0]])]), "small_grads_last")
    dmod = jnp.stack(dmod)
    d_conv_blocks = jnp.stack(d_conv).reshape(nl, CONV_K, N_DEV, -1).transpose(2, 0, 1, 3).reshape(N_DEV, -1, LANES)
    dmod_blocks = dmod.reshape(nl, N_DEV, -1).transpose(1, 0, 2)
    conv_all, dmod_all = _a2a_direct([d_conv_blocks, dmod_blocks], "small_grads_scatter")
    by_layer = lambda arrs: jnp.stack([_pack([a[i] for a in arrs]) for i in range(nl)])
    rep_out = _sum_adam(jnp.stack(rep_parts), by_layer(rep_w), by_layer(rep_m), by_layer(rep_v), "adam_small")
    fin_out = _sum_adam(fin_parts[None], _pack([final_g])[None], _pack([m_final_g])[None], _pack([v_final_g])[None],
                        "adam_final_g")
    layer_like = [a[0] for a in rep_w]
    rep_out = [[jnp.stack(per_layer) for per_layer in zip(*[_unpack(o[i], layer_like) for i in range(nl)])]
               + _unpack(f[0], [final_g]) for o, f in zip(rep_out, fin_out)]
    conv_out = _sum_adam(conv_all[None], conv_w.reshape(1, -1, LANES), m_conv_w.reshape(1, -1, LANES),
                         v_conv_w.reshape(1, -1, LANES), "adam_conv")
    conv_out = [o.reshape(conv_w.shape) for o in conv_out]
    ada_out = _ada_bwd(c_all[:, :, None], dmod_all.transpose(1, 0, 2), ada_w, m_ada_w, v_ada_w, "ada_bwd_adam")
    big_m = (m_w_in, m_w_branch_a, m_w_branch_b, m_w_out, m_w_ffn_in, m_w_ffn_out)
    big_v = (v_w_in, v_w_branch_a, v_w_branch_b, v_w_out, v_w_ffn_in, v_w_ffn_out)
    big_out = [_sum_adam(accs[k], big[k], big_m[k], big_v[k], "adam_big_%d" % k) for k in range(6)]

    def ordered(kind):
        rep = rep_out[kind]
        return (ada_out[kind], rep[0], rep[1], big_out[0][kind], conv_out[kind], rep[2], rep[3], rep[4], rep[5],
                rep[6], rep[7], big_out[1][kind], big_out[2][kind], big_out[3][kind], rep[8], big_out[4][kind],
                big_out[5][kind], rep[9])

    return (loss, dx[None]) + ordered(0) + ordered(1) + ordered(2) + ordered(3)
```

```python
import functools

import jax
import jax.numpy as jnp
from jax import lax
from jax.experimental import pallas as pl
from jax.experimental.pallas import tpu as pltpu

F32 = jnp.float32
BF16 = jnp.bfloat16
MXU_DTYPE = BF16
WIRE_DTYPE = BF16
ACT_DTYPE = BF16
EPS = 1e-6
LANES = 128
SUBLANES = 8
GDN_CHUNK = 128
A_CHUNK = 128
GROUPS = 8
HEADS = 8
HEAD_DIM = 128
CONV_K = 4
N_DEV = 8
VMEM_LIMIT = 48 * 1024 * 1024
MESH = pl.DeviceIdType.MESH

ADAM_LR = 0.001
ADAM_B1 = 0.9
ADAM_B2 = 0.999
ADAM_EPS = 1e-08
ADAM_WD = 0.01
ADAM_STEP = 10

_NN = (((1,), (0,)), ((), ()))
_NT = (((1,), (1,)), ((), ()))
_TN = (((0,), (0,)), ((), ()))


def _mm(a, b, dims=_NN):
    return lax.dot_general(a.astype(MXU_DTYPE), b.astype(MXU_DTYPE), dims, preferred_element_type=F32)


def _mm_hi(a, b):
    return lax.dot_general(a, b, _NN, precision=lax.Precision.HIGHEST, preferred_element_type=F32)


def _tile(n, cands):
    for c in cands:
        if n % c == 0:
            return c
    return n


def _params(sem=None):
    return pltpu.CompilerParams(dimension_semantics=sem, vmem_limit_bytes=VMEM_LIMIT)


def _sigmoid(x):
    return 1.0 / (1.0 + jnp.exp(-x))


def _silu(x):
    return x * _sigmoid(x)


def _dsilu(x):
    s = _sigmoid(x)
    return s * (1.0 + x * (1.0 - s))


_GELU_C = 0.7978845608028654
_GELU_A = 0.044715


def _gelu(x):
    return 0.5 * x * (1.0 + jnp.tanh(_GELU_C * (x + _GELU_A * x * x * x)))


def _gelu_and_slope(x):
    t = jnp.tanh(_GELU_C * (x + _GELU_A * x * x * x))
    return 0.5 * x * (1.0 + t), 0.5 * (1.0 + t) + 0.5 * x * (1.0 - t * t) * _GELU_C * (1.0 + 3.0 * _GELU_A * x * x)


def _softplus(x):
    return jnp.maximum(x, 0.0) + jnp.log(1.0 + jnp.exp(-jnp.abs(x)))


_MM_TILES = (1024, 1408, 1664, 512, 256, 128)


class _NoSide:
    operands, out_shape, scratch, aliases, n_in, n_out = [], [], [], {}, 0, 0


def _side_hooks(side, refs, n_main_in, n_main_out, n_main_scratch, grid):
    a = n_main_in + side.n_in
    b = a + n_main_out + side.n_out
    ins, outs, sems = refs[n_main_in:a], refs[a + n_main_out:b], refs[b + n_main_scratch:]
    main = refs[:n_main_in] + refs[a:a + n_main_out] + refs[b:b + n_main_scratch]
    ids = [pl.program_id(k) for k in range(len(grid))]

    def start():
        if side.n_in:
            pl.when(functools.reduce(jnp.logical_and, [i == 0 for i in ids]))(lambda: side.start(ins, outs, sems))

    def finish():
        if side.n_in:
            last = functools.reduce(jnp.logical_and, [i == g - 1 for i, g in zip(ids, grid)])
            pl.when(last)(lambda: side.finish(ins, outs, sems))

    return main, start, finish


def _carrier_call(body, name, grid, in_specs, out_specs, out_shape, scratch, side, args):
    aliases = {len(in_specs) + k: len(out_specs) + v for k, v in side.aliases.items()}
    return pl.pallas_call(
        body, name=name, grid=grid, in_specs=list(in_specs) + [_ANY] * side.n_in,
        out_specs=list(out_specs) + [_ANY] * side.n_out, out_shape=list(out_shape) + list(side.out_shape),
        scratch_shapes=list(scratch) + list(side.scratch), input_output_aliases=aliases,
        compiler_params=_params(("arbitrary",) * len(grid)))(*args, *side.operands)


_MM_VMEM_BUDGET = 44 * 1024 * 1024


def _matmul_tiles(mode, m, n, k, out_bytes):
    tk = _tile(k, _MM_TILES)
    tm = _tile(m, _MM_TILES)
    in_bytes = jnp.dtype(MXU_DTYPE).itemsize
    for tn in _MM_TILES:
        if n % tn:
            continue
        need = 2 * in_bytes * (tm * tk + tk * tn) + tm * tn * (2 * out_bytes + (4 if k > tk else 0))
        if need <= _MM_VMEM_BUDGET:
            return tm, tn, tk
    return tm, _tile(n, (LANES,)), tk


def _matmul(a, b, mode, name, out_dtype=F32, side=_NoSide):
    if mode == "nn":
        (m, k), n = a.shape, b.shape[1]
    elif mode == "nt":
        (m, k), n = a.shape, b.shape[0]
    else:
        (k, m), n = a.shape, b.shape[1]
    tm, tn, tk = _matmul_tiles(mode, m, n, k, jnp.dtype(out_dtype).itemsize)
    nk = k // tk
    grid = (m // tm, n // tn, nk)
    dims = {"nn": _NN, "nt": _NT, "tn": _TN}[mode]

    def body(*refs):
        (a_ref, b_ref, o_ref, acc_ref), side_start, side_finish = _side_hooks(side, refs, 2, 1, 1, grid)
        kk = pl.program_id(2)
        side_start()
        if nk == 1:
            o_ref[...] = _mm(a_ref[...], b_ref[...], dims).astype(o_ref.dtype)
        else:
            @pl.when(kk == 0)
            def _():
                acc_ref[...] = _mm(a_ref[...], b_ref[...], dims)

            @pl.when(jnp.logical_and(kk > 0, kk < nk - 1))
            def _():
                acc_ref[...] += _mm(a_ref[...], b_ref[...], dims)

            @pl.when(kk == nk - 1)
            def _():
                o_ref[...] = (acc_ref[...] + _mm(a_ref[...], b_ref[...], dims)).astype(o_ref.dtype)

        side_finish()

    a_spec = (pl.BlockSpec((tk, tm), lambda i, j, l: (l, i)) if mode == "tn"
              else pl.BlockSpec((tm, tk), lambda i, j, l: (i, l)))
    b_spec = (pl.BlockSpec((tn, tk), lambda i, j, l: (j, l)) if mode == "nt"
              else pl.BlockSpec((tk, tn), lambda i, j, l: (l, j)))
    o_spec = pl.BlockSpec((tm, tn), lambda i, j, l: (i, j))
    out = _carrier_call(body, name, grid, [a_spec, b_spec], [o_spec], [jax.ShapeDtypeStruct((m, n), out_dtype)],
                        [pltpu.VMEM((tm, tn) if nk > 1 else (SUBLANES, LANES), F32)], side, (a, b))
    return out if side.n_in else out[0]


_ROW_TILES = (512, 256, 128)


def _resid_norm(x, delta, gt, g, sc, sh, name):
    t, d = x.shape
    tt = _tile(t, _ROW_TILES)
    has = delta is not None

    def body(*refs):
        if has:
            x_ref, d_ref, gt_ref, g_ref, sc_ref, sh_ref, xo_ref, h_ref = refs
            xv = x_ref[...] + gt_ref[...] * d_ref[...]
            xo_ref[...] = xv
        else:
            x_ref, g_ref, sc_ref, sh_ref, h_ref = refs
            xv = x_ref[...]
        r = lax.rsqrt(jnp.mean(xv * xv, axis=-1, keepdims=True) + EPS)
        y = xv * r * g_ref[...]
        h_ref[...] = (y * (1.0 + sc_ref[...]) + sh_ref[...]).astype(h_ref.dtype)

    row = pl.BlockSpec((tt, d), lambda i: (i, 0))
    vec = pl.BlockSpec((1, d), lambda i: (0, 0))
    if has:
        return pl.pallas_call(
            body, name=name, grid=(t // tt,), in_specs=[row, row, vec, vec, vec, vec], out_specs=[row, row],
            out_shape=[jax.ShapeDtypeStruct((t, d), F32), jax.ShapeDtypeStruct((t, d), MXU_DTYPE)],
            compiler_params=_params(("parallel",)))(x, delta, gt, g, sc, sh)
    h = pl.pallas_call(
        body, name=name + "_first", grid=(t // tt,), in_specs=[row, vec, vec, vec], out_specs=row,
        out_shape=jax.ShapeDtypeStruct((t, d), MXU_DTYPE), compiler_params=_params(("parallel",)))(x, g, sc, sh)
    return x, h


def _final_loss(x, delta, gt, g, target, name):
    t, d = x.shape
    tt = _tile(t, _ROW_TILES)

    def body(x_ref, d_ref, gt_ref, g_ref, tg_ref, dx_ref, dg_ref, loss_ref):
        @pl.when(pl.program_id(0) == 0)
        def _():
            dg_ref[...] = jnp.zeros_like(dg_ref)
            loss_ref[...] = jnp.zeros_like(loss_ref)

        xv = x_ref[...] + gt_ref[...] * d_ref[...]
        r = lax.rsqrt(jnp.mean(xv * xv, axis=-1, keepdims=True) + EPS)
        xh = xv * r
        diff = xh * g_ref[...] - tg_ref[...]
        loss_ref[...] += jnp.sum(diff * diff) * (0.5 / d)
        dy = diff * (1.0 / d)
        dg_ref[...] += jnp.sum(dy * xh, axis=0, keepdims=True)
        dxh = dy * g_ref[...]
        dx_ref[...] = r * (dxh - xh * jnp.mean(dxh * xh, axis=-1, keepdims=True))

    row = pl.BlockSpec((tt, d), lambda i: (i, 0))
    vec = pl.BlockSpec((1, d), lambda i: (0, 0))
    tile = pl.BlockSpec((SUBLANES, LANES), lambda i: (0, 0))
    return pl.pallas_call(
        body, name=name, grid=(t // tt,), in_specs=[row, row, vec, vec, row], out_specs=[row, vec, tile],
        out_shape=[jax.ShapeDtypeStruct((t, d), F32), jax.ShapeDtypeStruct((1, d), F32),
                   jax.ShapeDtypeStruct((SUBLANES, LANES), F32)],
        compiler_params=_params(("arbitrary",)))(x, delta, gt, g, target)


def _norm_bwd(x, dh, dres, g, sc, name):
    t, d = x.shape
    tt = _tile(t, _ROW_TILES)

    def body(x_ref, dh_ref, dr_ref, g_ref, sc_ref, dx_ref, dsh_ref, dsc_ref, dg_ref):
        @pl.when(pl.program_id(0) == 0)
        def _():
            dsh_ref[...] = jnp.zeros_like(dsh_ref)
            dsc_ref[...] = jnp.zeros_like(dsc_ref)
            dg_ref[...] = jnp.zeros_like(dg_ref)

        xv, dh = x_ref[...], dh_ref[...]
        r = lax.rsqrt(jnp.mean(xv * xv, axis=-1, keepdims=True) + EPS)
        xh = xv * r
        gv, sc1 = g_ref[...], 1.0 + sc_ref[...]
        dsh_ref[...] += jnp.sum(dh, axis=0, keepdims=True)
        dsc_ref[...] += jnp.sum(dh * xh, axis=0, keepdims=True) * gv
        dg_ref[...] += jnp.sum(dh * xh, axis=0, keepdims=True) * sc1
        dxh = dh * (gv * sc1)
        dx_ref[...] = dr_ref[...] + r * (dxh - xh * jnp.mean(dxh * xh, axis=-1, keepdims=True))

    row = pl.BlockSpec((tt, d), lambda i: (i, 0))
    vec = pl.BlockSpec((1, d), lambda i: (0, 0))
    vshape = jax.ShapeDtypeStruct((1, d), F32)
    return pl.pallas_call(
        body, name=name, grid=(t // tt,), in_specs=[row, row, row, vec, vec], out_specs=[row, vec, vec, vec],
        out_shape=[jax.ShapeDtypeStruct((t, d), F32), vshape, vshape, vshape],
        compiler_params=_params(("arbitrary",)))(x, dh, dres, g, sc)


def _gate_bwd(dxo, branch, gt, name):
    t, d = dxo.shape
    tt = _tile(t, _ROW_TILES)

    def body(dx_ref, br_ref, gt_ref, db_ref, dgt_ref):
        @pl.when(pl.program_id(0) == 0)
        def _():
            dgt_ref[...] = jnp.zeros_like(dgt_ref)

        dx = dx_ref[...]
        db_ref[...] = (dx * gt_ref[...]).astype(db_ref.dtype)
        dgt_ref[...] += jnp.sum(dx * br_ref[...], axis=0, keepdims=True)

    row = pl.BlockSpec((tt, d), lambda i: (i, 0))
    vec = pl.BlockSpec((1, d), lambda i: (0, 0))
    return pl.pallas_call(
        body, name=name, grid=(t // tt,), in_specs=[row, row, vec], out_specs=[row, vec],
        out_shape=[jax.ShapeDtypeStruct((t, d), MXU_DTYPE), jax.ShapeDtypeStruct((1, d), F32)],
        compiler_params=_params(("arbitrary",)))(dxo, branch, gt)


def _swiglu_fwd(gu, name):
    t, f2 = gu.shape
    f = f2 // 2
    tt = _tile(t, (256, 128))

    def body(g_ref, u_ref, o_ref):
        o_ref[...] = (_silu(g_ref[...].astype(F32)) * u_ref[...].astype(F32)).astype(o_ref.dtype)

    return pl.pallas_call(
        body, name=name, grid=(t // tt,),
        in_specs=[pl.BlockSpec((tt, f), lambda i: (i, 0)), pl.BlockSpec((tt, f), lambda i: (i, 1))],
        out_specs=pl.BlockSpec((tt, f), lambda i: (i, 0)), out_shape=jax.ShapeDtypeStruct((t, f), MXU_DTYPE),
        compiler_params=_params(("parallel",)))(gu, gu)


def _swiglu_bwd(gu, da, name):
    t, f2 = gu.shape
    f = f2 // 2
    tt = _tile(t, (256, 128))

    def body(g_ref, u_ref, da_ref, o_ref):
        gate, da = g_ref[...].astype(F32), da_ref[...]
        sg = _sigmoid(gate)
        o_ref[:, :f] = (da * u_ref[...].astype(F32) * (sg * (1.0 + gate * (1.0 - sg)))).astype(o_ref.dtype)
        o_ref[:, f:] = (da * (gate * sg)).astype(o_ref.dtype)

    return pl.pallas_call(
        body, name=name, grid=(t // tt,),
        in_specs=[pl.BlockSpec((tt, f), lambda i: (i, 0)), pl.BlockSpec((tt, f), lambda i: (i, 1)),
                  pl.BlockSpec((tt, f), lambda i: (i, 0))],
        out_specs=pl.BlockSpec((tt, f2), lambda i: (i, 0)), out_shape=jax.ShapeDtypeStruct((t, f2), MXU_DTYPE),
        compiler_params=_params(("parallel",)))(gu, gu, da)


class _ProjLayout:
    def __init__(self, d):
        wc = 3 * HEADS * HEAD_DIM
        self.d, self.wc = d, wc
        self.qkv, self.z, self.uv, self.gates, self.ba = 0, wc, wc + d, wc + 3 * d, wc + 5 * d
        self.width = self.ba + LANES
        assert self.z % d == 0 and self.uv % (2 * d) == 0 and self.gates % (2 * d) == 0 and self.ba % LANES == 0

    def pieces(self, shard):
        d, wc, out, lo = self.d, self.wc, [], 0
        for length, dst in ((2 * d, self.uv), (wc, self.qkv), (d, self.z), (2 * HEADS, self.ba), (2 * d, self.gates)):
            pos = lo
            while pos < lo + length:
                j = pos // shard
                n = min(lo + length, (j + 1) * shard) - pos
                out.append((j, pos - j * shard, n, dst + pos - lo))
                pos += n
            lo += length
        return out


def _merge_fwd(pa, pb, proj, gcol, name):
    t, d = pa.shape
    tt = _tile(t, _ROW_TILES)

    def body(pa_ref, pb_ref, ga_ref, gb_ref, o_ref):
        sa, sb = _sigmoid(ga_ref[...].astype(F32)), _sigmoid(gb_ref[...].astype(F32))
        o_ref[...] = (sa * pa_ref[...] + sb * pb_ref[...]).astype(o_ref.dtype)

    row = pl.BlockSpec((tt, d), lambda i: (i, 0))
    gate = lambda k: pl.BlockSpec((tt, d), lambda i: (i, gcol // d + k))
    return pl.pallas_call(
        body, name=name, grid=(t // tt,), in_specs=[row, row, gate(0), gate(1)], out_specs=row,
        out_shape=jax.ShapeDtypeStruct((t, d), MXU_DTYPE), compiler_params=_params(("parallel",)))(pa, pb, proj, proj)


def _merge_bwd(dm, pa, pb, proj, gcol, dproj, name):
    t, d = pa.shape
    tt = _tile(t, _ROW_TILES)

    def body(dm_ref, pa_ref, pb_ref, ga_ref, gb_ref, _, dpa_ref, dpb_ref, dg_ref):
        dm = dm_ref[...]
        sa, sb = _sigmoid(ga_ref[...].astype(F32)), _sigmoid(gb_ref[...].astype(F32))
        dpa_ref[...] = (dm * sa).astype(dpa_ref.dtype)
        dpb_ref[...] = (dm * sb).astype(dpb_ref.dtype)
        dg_ref[:, :d] = (dm * pa_ref[...] * sa * (1.0 - sa)).astype(dg_ref.dtype)
        dg_ref[:, d:] = (dm * pb_ref[...] * sb * (1.0 - sb)).astype(dg_ref.dtype)

    row = pl.BlockSpec((tt, d), lambda i: (i, 0))
    gate = lambda k: pl.BlockSpec((tt, d), lambda i: (i, gcol // d + k))
    wide = pl.BlockSpec((tt, 2 * d), lambda i: (i, gcol // (2 * d)))
    return pl.pallas_call(
        body, name=name, grid=(t // tt,), in_specs=[row, row, row, gate(0), gate(1), _ANY], out_specs=[row, row, wide],
        out_shape=[jax.ShapeDtypeStruct((t, d), MXU_DTYPE), jax.ShapeDtypeStruct((t, d), MXU_DTYPE),
                   jax.ShapeDtypeStruct(dproj.shape, dproj.dtype)],
        input_output_aliases={5: 2}, compiler_params=_params(("parallel",)))(dm, pa, pb, proj, proj, dproj)


def _tri_masks(n):
    ri = lax.broadcasted_iota(jnp.int32, (n, n), 0)
    ci = lax.broadcasted_iota(jnp.int32, (n, n), 1)
    return ri >= ci, ri > ci, ri == ci


def _mixer_a_fwd(proj, ucol, w_s, b_col, g_v, name):
    t, w = proj.shape[0], g_v.shape[1]
    c = A_CHUNK

    def body(u_ref, v_ref, w_ref, b_ref, gv_ref, y_ref):
        tril, _, _ = _tri_masks(c)
        ug, vg = _gelu(u_ref[...].astype(F32)), _gelu(v_ref[...].astype(F32))
        for g in range(GROUPS):
            sl = slice(g * c, (g + 1) * c)
            vt = vg[:, sl]
            r = lax.rsqrt(jnp.mean(vt * vt, axis=-1, keepdims=True) + EPS)
            vn = vt * r * gv_ref[:, sl]
            s = _mm(jnp.where(tril, w_ref[g], 0.0), vn) + b_ref[g]
            y_ref[:, sl] = (ug[:, sl] * s).astype(y_ref.dtype)

    return pl.pallas_call(
        body, name=name, grid=(t // c,),
        in_specs=[pl.BlockSpec((c, w), lambda i: (i, ucol // w)), pl.BlockSpec((c, w), lambda i: (i, ucol // w + 1)),
                  pl.BlockSpec((GROUPS, c, c), lambda i: (0, 0, 0)), pl.BlockSpec((GROUPS, c, 1), lambda i: (0, 0, 0)),
                  pl.BlockSpec((1, w), lambda i: (0, 0))],
        out_specs=pl.BlockSpec((c, w), lambda i: (i, 0)), out_shape=jax.ShapeDtypeStruct((t, w), MXU_DTYPE),
        compiler_params=_params(("parallel",)))(proj, proj, w_s, b_col, g_v)


def _mixer_a_bwd(proj, ucol, dy, w_s, w_st, b_col, g_v, dproj, name):
    t, w = proj.shape[0], g_v.shape[1]
    w2 = 2 * w
    c = A_CHUNK

    def body(u_ref, v_ref, dy_ref, w_ref, wt_ref, b_ref, gv_ref, _, duv_ref, dw_ref, db_ref, dgv_ref):
        @pl.when(pl.program_id(0) == 0)
        def _():
            dw_ref[...] = jnp.zeros_like(dw_ref)
            db_ref[...] = jnp.zeros_like(db_ref)
            dgv_ref[...] = jnp.zeros_like(dgv_ref)

        tril, _, _ = _tri_masks(c)
        triu = lax.broadcasted_iota(jnp.int32, (c, c), 0) <= lax.broadcasted_iota(jnp.int32, (c, c), 1)
        (ug, dug), (vg, dvg) = _gelu_and_slope(u_ref[...].astype(F32)), _gelu_and_slope(v_ref[...].astype(F32))
        for g in range(GROUPS):
            sl = slice(g * c, (g + 1) * c)
            vt = vg[:, sl]
            r = lax.rsqrt(jnp.mean(vt * vt, axis=-1, keepdims=True) + EPS)
            vh = vt * r
            gv = gv_ref[:, sl]
            vn = vh * gv
            s = _mm(jnp.where(tril, w_ref[g], 0.0), vn) + b_ref[g]
            dy = dy_ref[:, sl]
            ds = dy * ug[:, sl]
            dw_ref[g] += jnp.where(tril, _mm(ds, vn, _NT), 0.0)
            db_ref[g] += jnp.sum(ds, axis=1, keepdims=True)
            dvn = _mm(jnp.where(triu, wt_ref[g], 0.0), ds)
            dgv_ref[:, sl] += jnp.sum(dvn * vh, axis=0, keepdims=True)
            dvh = dvn * gv
            dvt = r * (dvh - vh * jnp.mean(dvh * vh, axis=-1, keepdims=True))
            duv_ref[:, sl] = (dy * s * dug[:, sl]).astype(duv_ref.dtype)
            duv_ref[:, w + g * c:w + (g + 1) * c] = (dvt * dvg[:, sl]).astype(duv_ref.dtype)

    full3 = lambda shape: pl.BlockSpec(shape, lambda i: (0, 0, 0))
    return pl.pallas_call(
        body, name=name, grid=(t // c,),
        in_specs=[pl.BlockSpec((c, w), lambda i: (i, ucol // w)), pl.BlockSpec((c, w), lambda i: (i, ucol // w + 1)),
                  pl.BlockSpec((c, w), lambda i: (i, 0)), full3((GROUPS, c, c)), full3((GROUPS, c, c)),
                  full3((GROUPS, c, 1)), pl.BlockSpec((1, w), lambda i: (0, 0)), _ANY],
        out_specs=[pl.BlockSpec((c, w2), lambda i: (i, ucol // w2)), full3((GROUPS, c, c)), full3((GROUPS, c, 1)),
                   pl.BlockSpec((1, w), lambda i: (0, 0))],
        out_shape=[jax.ShapeDtypeStruct(dproj.shape, dproj.dtype), jax.ShapeDtypeStruct((GROUPS, c, c), F32),
                   jax.ShapeDtypeStruct((GROUPS, c, 1), F32), jax.ShapeDtypeStruct((1, w), F32)],
        input_output_aliases={7: 0},
        compiler_params=_params(("arbitrary",)))(proj, proj, dy, w_s, w_st, b_col, g_v, dproj)


_Q_SCALE = HEAD_DIM ** -0.5


CONV_HALO = 16


def _conv_taps(x_ref, p_ref, w_ref):
    prev = jnp.where(pl.program_id(0) > 0, p_ref[...].astype(F32), 0.0)
    ext = jnp.concatenate([prev, x_ref[...].astype(F32)], axis=0)
    shifted = [ext[CONV_HALO:]] + [pltpu.roll(ext, s, 0)[CONV_HALO:] for s in range(1, CONV_K)]
    acc = shifted[0] * w_ref[pl.ds(CONV_K - 1, 1), :]
    for s in range(1, CONV_K):
        acc = acc + shifted[s] * w_ref[pl.ds(CONV_K - 1 - s, 1), :]
    return acc, shifted


def _conv_fwd(qkv, w, name):
    t, cw = qkv.shape[0], w.shape[1]
    tt = _tile(t, (256, 128))
    hb = tt // CONV_HALO

    def body(x_ref, p_ref, w_ref, o_ref):
        acc, _ = _conv_taps(x_ref, p_ref, w_ref)
        y = _silu(acc)
        for which in range(3):
            for h in range(HEADS):
                lo = (which * HEADS + h) * HEAD_DIM
                seg = y[:, lo:lo + HEAD_DIM]
                if which < 2:
                    seg = seg * lax.rsqrt(jnp.sum(seg * seg, axis=-1, keepdims=True) + EPS)
                if which == 0:
                    seg = seg * _Q_SCALE
                o_ref[which, h] = seg

    return pl.pallas_call(
        body, name=name, grid=(t // tt,),
        in_specs=[pl.BlockSpec((tt, cw), lambda i: (i, 0)),
                  pl.BlockSpec((CONV_HALO, cw), lambda i: (jnp.maximum(i * hb - 1, 0), 0)),
                  pl.BlockSpec((CONV_K, cw), lambda i: (0, 0))],
        out_specs=pl.BlockSpec((3, HEADS, tt, HEAD_DIM), lambda i: (0, 0, i, 0)),
        out_shape=jax.ShapeDtypeStruct((3, HEADS, t, HEAD_DIM), F32),
        compiler_params=_params(("parallel",)))(qkv, qkv, w)


def _conv_bwd_pre(qkv, dq, dk, dv, w, name):
    t, cw = qkv.shape[0], w.shape[1]
    tt = _tile(t, (256, 128))
    hb = tt // CONV_HALO

    def body(x_ref, p_ref, dq_ref, dk_ref, dv_ref, w_ref, da_ref, dw_ref):
        @pl.when(pl.program_id(0) == 0)
        def _():
            dw_ref[...] = jnp.zeros_like(dw_ref)

        acc, shifted = _conv_taps(x_ref, p_ref, w_ref)
        sg = _sigmoid(acc)
        y = acc * sg
        dsilu = sg * (1.0 + acc * (1.0 - sg))
        d_refs = (dq_ref, dk_ref, dv_ref)
        for which in range(3):
            for h in range(HEADS):
                lo = (which * HEADS + h) * HEAD_DIM
                sl = slice(lo, lo + HEAD_DIM)
                dn = d_refs[which][h]
                if which < 2:
                    seg = y[:, sl]
                    rho = lax.rsqrt(jnp.sum(seg * seg, axis=-1, keepdims=True) + EPS)
                    nrm = seg * rho
                    if which == 0:
                        dn = dn * _Q_SCALE
                    dn = rho * (dn - nrm * jnp.sum(dn * nrm, axis=-1, keepdims=True))
                dacc = dn * dsilu[:, sl]
                da_ref[:, sl] = dacc
                for s in range(CONV_K):
                    dw_ref[pl.ds(CONV_K - 1 - s, 1), sl] += jnp.sum(dacc * shifted[s][:, sl], axis=0, keepdims=True)

    head = pl.BlockSpec((HEADS, tt, HEAD_DIM), lambda i: (0, i, 0))
    return pl.pallas_call(
        body, name=name, grid=(t // tt,),
        in_specs=[pl.BlockSpec((tt, cw), lambda i: (i, 0)),
                  pl.BlockSpec((CONV_HALO, cw), lambda i: (jnp.maximum(i * hb - 1, 0), 0)),
                  head, head, head, pl.BlockSpec((CONV_K, cw), lambda i: (0, 0))],
        out_specs=[pl.BlockSpec((tt, cw), lambda i: (i, 0)), pl.BlockSpec((CONV_K, cw), lambda i: (0, 0))],
        out_shape=[jax.ShapeDtypeStruct((t, cw), F32), jax.ShapeDtypeStruct((CONV_K, cw), F32)],
        compiler_params=_params(("arbitrary",)))(qkv, qkv, dq, dk, dv, w)


def _conv_bwd_in(dacc, w, dproj, name):
    t, cw = dacc.shape
    tt = _tile(t, (256, 128))
    hb = tt // SUBLANES
    nt = t // tt
    rows = tt + SUBLANES

    def body(d_ref, n_ref, w_ref, _, o_ref):
        cur = d_ref[...]
        nxt = jnp.where(pl.program_id(0) < nt - 1, n_ref[...], 0.0)
        ext = jnp.concatenate([cur, nxt], axis=0)
        acc = cur * w_ref[pl.ds(CONV_K - 1, 1), :]
        for s in range(1, CONV_K):
            acc = acc + pltpu.roll(ext, rows - s, 0)[:tt] * w_ref[pl.ds(CONV_K - 1 - s, 1), :]
        o_ref[...] = acc.astype(o_ref.dtype)

    return pl.pallas_call(
        body, name=name, grid=(nt,),
        in_specs=[pl.BlockSpec((tt, cw), lambda i: (i, 0)),
                  pl.BlockSpec((SUBLANES, cw), lambda i: (jnp.minimum((i + 1) * hb, t // SUBLANES - 1), 0)),
                  pl.BlockSpec((CONV_K, cw), lambda i: (0, 0)), _ANY],
        out_specs=pl.BlockSpec((tt, cw), lambda i: (i, 0)), out_shape=jax.ShapeDtypeStruct(dproj.shape, dproj.dtype),
        input_output_aliases={3: 0}, compiler_params=_params(("parallel",)))(dacc, dacc, w, dproj)


_INV_BASE_SHIFT = 3


def _inv_unit_lower(a, eye):
    c = GDN_CHUNK
    ri = lax.broadcasted_iota(jnp.int32, (c, c), 0)
    ci = lax.broadcasted_iota(jnp.int32, (c, c), 1)
    same = lambda sh: (ri >> sh) == (ci >> sh)
    x = jnp.where(same(_INV_BASE_SHIFT), -a, 0.0)
    p = jnp.where(eye, 1.0, 0.0) + x
    xs = _split(x)
    x2 = _mm3(xs, xs)
    x2s, ps = _split(x2), _split(p)
    r = _mm3(x2s, tuple(jnp.concatenate([u, v], axis=-1) for u, v in zip(x2s, ps)))
    x4, p = r[..., :c], p + r[..., c:]
    p = p + _mm3(_split(x4), _split(p))
    for sh in range(_INV_BASE_SHIFT, c.bit_length() - 1):
        off = jnp.where(same(sh + 1) & jnp.logical_not(same(sh)), a, 0.0)
        ps = _split(p)
        p = p - _mm3(ps, _split(_mm3(_split(off), ps)))
    return p


def _split(a):
    hi = a.astype(BF16)
    return hi, (a - hi.astype(F32)).astype(BF16)


def _dot_heads(u, v, dims):
    if u.ndim == 3:
        return jnp.stack([_dot_heads(u[j], v[j], dims) for j in range(u.shape[0])])
    return lax.dot_general(u, v, dims, preferred_element_type=F32)


def _mm3(a, b):
    return _dot_heads(a[0], b[0], _NN) + (_dot_heads(a[0], b[1], _NN) + _dot_heads(a[1], b[0], _NN))


def _hmm(a, b, dims=_NN):
    return _dot_heads(a.astype(MXU_DTYPE), b.astype(MXU_DTYPE), dims)


def _rowsum(x):
    return jnp.sum(x, axis=-1, keepdims=True)


def _colsum(x):
    return jnp.sum(x, axis=-2, keepdims=True)


class _Pre:
    pass


def _gdn_pre(q, k, v, araw, braw, alog, dtb, t_mat=None):
    c = GDN_CHUNK
    p = _Pre()
    p.tril, p.strict, p.eye = _tri_masks(c)
    p.to_col = lambda row: _rowsum(jnp.where(p.eye, row, 0.0))
    p.to_row = lambda col: _colsum(jnp.where(p.eye, col, 0.0))
    p.a_neg = -jnp.exp(alog + jnp.zeros((1, c), F32))
    p.xg = araw + dtb
    p.g_row = p.a_neg * _softplus(p.xg)
    p.beta_row = _sigmoid(braw)
    p.beta = p.to_col(p.beta_row)
    gam = _rowsum(jnp.where(p.tril, p.g_row, 0.0))
    gam_last = _rowsum(p.g_row)
    p.dm = jnp.where(p.tril, jnp.exp(jnp.where(p.tril, gam - p.to_row(gam), 0.0)), 0.0)
    p.e, p.ek, p.el = jnp.exp(gam), jnp.exp(gam_last - gam), jnp.exp(gam_last)
    p.kb = k * p.beta
    p.kk = _hmm(p.kb, k, _NT)
    p.t = _inv_unit_lower(jnp.where(p.strict, p.kk * p.dm, 0.0), p.eye) if t_mat is None else t_mat
    p.vb, p.kbe = v * p.beta, p.kb * p.e
    uw = _hmm(p.t, jnp.concatenate([p.vb, p.kbe], axis=-1))
    p.u, p.w = uw[..., :v.shape[-1]], uw[..., v.shape[-1]:]
    p.qk0 = _hmm(q, k, _NT)
    p.qk = p.qk0 * p.dm
    p.qd, p.kd = q * p.e, k * p.ek
    return p


GDN_HEADS_PER_STEP = 8


def _head_scalars(ref, hb):
    h0 = pl.program_id(0) * hb
    return jnp.stack([jnp.full((1, 1), ref[h0 + j], F32) for j in range(hb)])


def _gdn_specs(n, reverse):
    c, dk, hb = GDN_CHUNK, HEAD_DIM, GDN_HEADS_PER_STEP
    ix = (lambda i: n - 1 - i) if reverse else (lambda i: i)
    smem = pl.BlockSpec(memory_space=pltpu.SMEM)
    qkv = [pl.BlockSpec((None, hb, c, dk), functools.partial(lambda w, h, i: (w, h, ix(i), 0), w)) for w in range(3)]
    row = pl.BlockSpec((hb, None, 1, c), lambda h, i: (h, ix(i), 0, 0))
    tok = pl.BlockSpec((hb, c, dk), lambda h, i: (h, ix(i), 0))
    state = pl.BlockSpec((hb, None, dk, dk), lambda h, i: (h, ix(i), 0, 0))
    return smem, qkv, row, tok, state


def _gdn_fwd(qkv_h, araw, braw, alog, dtb, name, side=_NoSide):
    _, hh, t, dk = qkv_h.shape
    n, hb = t // GDN_CHUNK, GDN_HEADS_PER_STEP
    smem, qkv, row, tok, state = _gdn_specs(n, False)
    grid = (hh // hb, n)

    def body(*refs):
        main, side_start, side_finish = _side_hooks(side, refs, 7, 3, 1, grid)
        alog_ref, dt_ref, q_ref, k_ref, v_ref, a_ref, b_ref, o_ref, so_ref, to_ref, s_ref = main
        side_start()

        @pl.when(pl.program_id(1) == 0)
        def _():
            s_ref[...] = jnp.zeros_like(s_ref)

        p = _gdn_pre(q_ref[...], k_ref[...], v_ref[...], a_ref[...], b_ref[...],
                     _head_scalars(alog_ref, hb), _head_scalars(dt_ref, hb))
        s = s_ref[...]
        vn = p.u - _hmm(p.w, s)
        o_ref[...] = _hmm(p.qd, s) + _hmm(p.qk, vn)
        so_ref[...] = s
        to_ref[...] = p.t
        s_ref[...] = s * p.el + _hmm(p.kd, vn, _TN)
        side_finish()

    mats = jax.ShapeDtypeStruct((hh, n, dk, dk), F32)
    return _carrier_call(
        body, name, grid, [smem, smem] + qkv + [row, row], [tok, state, state],
        [jax.ShapeDtypeStruct((hh, t, dk), F32), mats, mats],
        [pltpu.VMEM((hb, dk, dk), F32)], side, (alog, dtb, qkv_h, qkv_h, qkv_h, araw, braw))


def _gdn_bwd(qkv_h, araw, braw, alog, dtb, states, t_mats, do, name, side=_NoSide):
    _, hh, t, dk = qkv_h.shape
    c, hb = GDN_CHUNK, GDN_HEADS_PER_STEP
    n = t // c
    smem, qkv, row, tok, state = _gdn_specs(n, True)
    acc = pl.BlockSpec((hb, 1, LANES), lambda h, i: (h, 0, 0))
    grid = (hh // hb, n)

    def body(*refs):
        main, side_start, side_finish = _side_hooks(side, refs, 10, 7, 1, grid)
        (alog_ref, dt_ref, q_ref, k_ref, v_ref, a_ref, b_ref, s_ref, t_ref, do_ref,
         dq_ref, dk_ref, dv_ref, da_ref, db_ref, dal_ref, ddt_ref, ds_ref) = main
        side_start()

        @pl.when(pl.program_id(1) == 0)
        def _():
            ds_ref[...] = jnp.zeros_like(ds_ref)
            dal_ref[...] = jnp.zeros_like(dal_ref)
            ddt_ref[...] = jnp.zeros_like(ddt_ref)

        q, k, v = q_ref[...], k_ref[...], v_ref[...]
        p = _gdn_pre(q, k, v, a_ref[...], b_ref[...], _head_scalars(alog_ref, hb), _head_scalars(dt_ref, hb),
                     t_ref[...])
        s, do, dsp = s_ref[...], do_ref[...], ds_ref[...]
        vn = p.u - _hmm(p.w, s)
        dqd = _hmm(do, s, _NT)
        dqk = _hmm(do, vn, _NT)
        dvn = _hmm(p.qk, do, _TN) + _hmm(p.kd, dsp)
        dkd = _hmm(vn, dsp, _NT)
        d_el = _colsum(_rowsum(s * dsp))
        ds_ref[...] = dsp * p.el + _hmm(p.qd, do, _TN) - _hmm(p.w, dvn, _TN)
        dw = -_hmm(dvn, s, _NT)
        d_t = _hmm(dvn, p.vb, _NT) + _hmm(dw, p.kbe, _NT)
        dvb, dkbe = _hmm(p.t, dvn, _TN), _hmm(p.t, dw, _TN)
        d_a = jnp.where(p.strict, -_hmm(p.t, _hmm(d_t, p.t, _NT), _TN), 0.0)
        dkk = d_a * p.dm
        dqk0 = dqk * p.dm
        ddm = d_a * p.kk + dqk * p.qk0
        dkb = _hmm(dkk, k) + dkbe * p.e
        dq_ref[...] = _hmm(dqk0, k) + dqd * p.e
        dk_ref[...] = _hmm(dkk, p.kb, _TN) + _hmm(dqk0, q, _TN) + dkd * p.ek + dkb * p.beta
        dv_ref[...] = dvb * p.beta
        dbeta = _rowsum(dkb * k) + _rowsum(dvb * v)
        d_e = _rowsum(dqd * q) + _rowsum(dkbe * p.kb)
        d_ek = _rowsum(dkd * k)
        m = ddm * p.dm
        dgam = d_e * p.e - d_ek * p.ek + _rowsum(m) - p.to_col(_colsum(m))
        dgam_last = _colsum(d_ek * p.ek) + d_el * p.el
        dg_row = _colsum(jnp.where(p.tril, dgam, 0.0)) + dgam_last
        da_row = dg_row * p.a_neg * _sigmoid(p.xg)
        da_ref[...] = da_row
        db_ref[...] = p.to_row(dbeta) * p.beta_row * (1.0 - p.beta_row)
        dal_ref[...] += _rowsum(dg_row * p.g_row)
        ddt_ref[...] += _rowsum(da_row)
        side_finish()

    tok_shape = jax.ShapeDtypeStruct((hh, t, dk), F32)
    row_shape = jax.ShapeDtypeStruct((hh, n, 1, c), F32)
    acc_shape = jax.ShapeDtypeStruct((hh, 1, LANES), F32)
    return _carrier_call(
        body, name, grid, [smem, smem] + qkv + [row, row, state, state, tok], [tok, tok, tok, row, row, acc, acc],
        [tok_shape, tok_shape, tok_shape, row_shape, row_shape, acc_shape, acc_shape],
        [pltpu.VMEM((hb, dk, dk), F32)], side, (alog, dtb, qkv_h, qkv_h, qkv_h, araw, braw, states, t_mats, do))


def _gdn_post_fwd(o, proj, zcol, g_o, name):
    hh, t, dv = o.shape
    tt = _tile(t, _ROW_TILES)
    zblk = zcol // (hh * dv)

    def body(o_ref, z_ref, g_ref, y_ref):
        for h in range(hh):
            sl = slice(h * dv, (h + 1) * dv)
            ov = o_ref[h]
            r = lax.rsqrt(jnp.mean(ov * ov, axis=-1, keepdims=True) + EPS)
            y_ref[:, sl] = (ov * r * g_ref[...] * _silu(z_ref[:, sl].astype(F32))).astype(y_ref.dtype)

    return pl.pallas_call(
        body, name=name, grid=(t // tt,),
        in_specs=[pl.BlockSpec((hh, tt, dv), lambda i: (0, i, 0)), pl.BlockSpec((tt, hh * dv), lambda i: (i, zblk)),
                  pl.BlockSpec((1, dv), lambda i: (0, 0))],
        out_specs=pl.BlockSpec((tt, hh * dv), lambda i: (i, 0)),
        out_shape=jax.ShapeDtypeStruct((t, hh * dv), MXU_DTYPE), compiler_params=_params(("parallel",)))(o, proj, g_o)


def _gdn_post_bwd(o, proj, zcol, dy, g_o, dproj, name):
    hh, t, dv = o.shape
    tt = _tile(t, _ROW_TILES)
    zblk = zcol // (hh * dv)

    def body(o_ref, z_ref, dy_ref, g_ref, _, do_ref, dz_ref, dg_ref):
        @pl.when(pl.program_id(0) == 0)
        def _():
            dg_ref[...] = jnp.zeros_like(dg_ref)

        gv = g_ref[...]
        for h in range(hh):
            sl = slice(h * dv, (h + 1) * dv)
            ov, zz, dy = o_ref[h], z_ref[:, sl].astype(F32), dy_ref[:, sl]
            r = lax.rsqrt(jnp.mean(ov * ov, axis=-1, keepdims=True) + EPS)
            oh = ov * r
            sg = _sigmoid(zz)
            dz_ref[:, sl] = (dy * oh * gv * (sg * (1.0 + zz * (1.0 - sg)))).astype(dz_ref.dtype)
            don = dy * (zz * sg)
            dg_ref[...] += _colsum(don * oh)
            doh = don * gv
            do_ref[h] = r * (doh - oh * jnp.mean(doh * oh, axis=-1, keepdims=True))

    return pl.pallas_call(
        body, name=name, grid=(t // tt,),
        in_specs=[pl.BlockSpec((hh, tt, dv), lambda i: (0, i, 0)), pl.BlockSpec((tt, hh * dv), lambda i: (i, zblk)),
                  pl.BlockSpec((tt, hh * dv), lambda i: (i, 0)), pl.BlockSpec((1, dv), lambda i: (0, 0)), _ANY],
        out_specs=[pl.BlockSpec((hh, tt, dv), lambda i: (0, i, 0)), pl.BlockSpec((tt, hh * dv), lambda i: (i, zblk)),
                   pl.BlockSpec((1, dv), lambda i: (0, 0))],
        out_shape=[jax.ShapeDtypeStruct((hh, t, dv), F32), jax.ShapeDtypeStruct(dproj.shape, dproj.dtype),
                   jax.ShapeDtypeStruct((1, dv), F32)],
        input_output_aliases={4: 1}, compiler_params=_params(("arbitrary",)))(o, proj, dy, g_o, dproj)


def _cols_as_rows(x, col, name):
    t = x.shape[0]
    tt = _tile(t, _ROW_TILES)

    def body(x_ref, o_ref):
        o_ref[...] = x_ref[...].T

    return pl.pallas_call(
        body, name=name, grid=(t // tt,), in_specs=[pl.BlockSpec((tt, LANES), lambda i: (i, col // LANES))],
        out_specs=pl.BlockSpec((LANES, tt), lambda i: (0, i)), out_shape=jax.ShapeDtypeStruct((LANES, t), x.dtype),
        compiler_params=_params(("parallel",)))(x)


def _rows_into_cols(dst, rows, col, name):
    t = dst.shape[0]
    tt = _tile(t, _ROW_TILES)

    def body(r_ref, _, o_ref):
        o_ref[...] = r_ref[...].T.astype(o_ref.dtype)

    return pl.pallas_call(
        body, name=name, grid=(t // tt,), in_specs=[pl.BlockSpec((LANES, tt), lambda i: (0, i)), _ANY],
        out_specs=pl.BlockSpec((tt, LANES), lambda i: (i, col // LANES)),
        out_shape=jax.ShapeDtypeStruct(dst.shape, dst.dtype), input_output_aliases={1: 0},
        compiler_params=_params(("parallel",)))(rows, dst)


def _adamw(g, w, m, v):
    m = ADAM_B1 * m + (1.0 - ADAM_B1) * g
    v = ADAM_B2 * v + (1.0 - ADAM_B2) * (g * g)
    m_hat = m / (1.0 - ADAM_B1 ** ADAM_STEP)
    v_hat = v / (1.0 - ADAM_B2 ** ADAM_STEP)
    return -ADAM_LR * (m_hat / (jnp.sqrt(v_hat) + ADAM_EPS) + ADAM_WD * w), m, v


def _ada_fwd(c_all, ada_w, name):
    nl, d, cols = ada_w.shape
    b = c_all.shape[0]

    def body(c_ref, w_ref, o_ref):
        o_ref[...] = _mm_hi(_silu(c_ref[...]), w_ref[...])

    return pl.pallas_call(
        body, name=name, grid=(nl,),
        in_specs=[pl.BlockSpec((b, d), lambda i: (0, 0)), pl.BlockSpec((None, d, cols), lambda i: (i, 0, 0))],
        out_specs=pl.BlockSpec((None, b, cols), lambda i: (i, 0, 0)),
        out_shape=jax.ShapeDtypeStruct((nl, b, cols), F32), compiler_params=_params(("parallel",)))(c_all, ada_w)


def _ada_bwd(c_col, dm, w, m, v, name):
    nl, d, cols = w.shape
    b = c_col.shape[0]
    tr = _tile(d, (256, 128))

    def body(c_ref, dm_ref, w_ref, m_ref, v_ref, g_ref, dl_ref, mo_ref, vo_ref):
        g = _silu(c_ref[0]) * dm_ref[pl.ds(0, 1), :]
        for j in range(1, b):
            g = g + _silu(c_ref[j]) * dm_ref[pl.ds(j, 1), :]
        g_ref[...] = g
        dl_ref[...], mo_ref[...], vo_ref[...] = _adamw(g, w_ref[...], m_ref[...], v_ref[...])

    blk = pl.BlockSpec((None, tr, cols), lambda l, i: (l, i, 0))
    shape = jax.ShapeDtypeStruct((nl, d, cols), F32)
    return pl.pallas_call(
        body, name=name, grid=(nl, d // tr),
        in_specs=[pl.BlockSpec((b, tr, 1), lambda l, i: (0, i, 0)), pl.BlockSpec((None, b, cols), lambda l, i: (l, 0, 0)),
                  blk, blk, blk],
        out_specs=[blk, blk, blk, blk], out_shape=[shape] * 4,
        compiler_params=_params(("parallel", "parallel")))(c_col, dm, w, m, v)


_GRAD_ROW_TILES = (256, 128, 176, 88)


def _sum_adam(parts, w, m, v, name):
    nl, npart, r, cdim = parts.shape
    tr = _tile(r, _GRAD_ROW_TILES)

    def body(p_ref, w_ref, m_ref, v_ref, g_ref, dl_ref, mo_ref, vo_ref):
        g = p_ref[0].astype(F32)
        for j in range(1, npart):
            g = g + p_ref[j].astype(F32)
        g_ref[...] = g
        dl_ref[...], mo_ref[...], vo_ref[...] = _adamw(g, w_ref[...], m_ref[...], v_ref[...])

    blk = pl.BlockSpec((None, tr, cdim), lambda l, i: (l, i, 0))
    shape = jax.ShapeDtypeStruct((nl, r, cdim), F32)
    return pl.pallas_call(
        body, name=name, grid=(nl, r // tr),
        in_specs=[pl.BlockSpec((None, npart, tr, cdim), lambda l, i: (l, 0, i, 0)), blk, blk, blk],
        out_specs=[blk, blk, blk, blk], out_shape=[shape] * 4,
        compiler_params=_params(("parallel", "parallel")))(parts, w, m, v)


def _cols_from_blocks(g, plan, width, name):
    _, r, cdim = g.shape
    tr = _tile(r, (256, 128))
    covered = sorted((dst, dst + n) for _, _, n, dst in plan)
    holes, pos = [], 0
    for a, b in covered:
        if a > pos:
            holes.append((pos, a))
        pos = max(pos, b)
    if pos < width:
        holes.append((pos, width))

    def body(g_ref, o_ref):
        for a, b in holes:
            o_ref[:, a:b] = jnp.zeros((tr, b - a), o_ref.dtype)
        for j, src, n, dst in plan:
            o_ref[:, dst:dst + n] = g_ref[j, :, src:src + n]

    return pl.pallas_call(
        body, name=name, grid=(r // tr,), in_specs=[pl.BlockSpec((N_DEV, tr, cdim), lambda i: (0, i, 0))],
        out_specs=pl.BlockSpec((tr, width), lambda i: (i, 0)), out_shape=jax.ShapeDtypeStruct((r, width), g.dtype),
        compiler_params=_params(("parallel",)))(g)


def _blocks_from_cols(w, plan, cdim, name):
    r, width = w.shape
    tr = _tile(r, (256, 128))

    def body(w_ref, o_ref):
        for j, src, n, dst in plan:
            o_ref[j, :, src:src + n] = w_ref[:, dst:dst + n]

    return pl.pallas_call(
        body, name=name, grid=(r // tr,), in_specs=[pl.BlockSpec((tr, width), lambda i: (i, 0))],
        out_specs=pl.BlockSpec((N_DEV, tr, cdim), lambda i: (0, i, 0)),
        out_shape=jax.ShapeDtypeStruct((N_DEV, r, cdim), w.dtype), compiler_params=_params(("parallel",)))(w)


def _pair_sum(x, tmp, core, name):
    _, r, cdim = x.shape
    tr = _tile(r, _GRAD_ROW_TILES)

    def body(core_ref, x_ref, t_ref, o_ref):
        o_ref[...] = (x_ref[...] + t_ref[...]).astype(o_ref.dtype)

    grid_spec = pltpu.PrefetchScalarGridSpec(
        num_scalar_prefetch=1, grid=(N_DEV // 2, r // tr),
        in_specs=[pl.BlockSpec((None, tr, cdim), lambda ch, i, core_ref: (2 * ch + core_ref[0], i, 0)),
                  pl.BlockSpec((None, tr, cdim), lambda ch, i, core_ref: (ch, i, 0))],
        out_specs=pl.BlockSpec((None, tr, cdim), lambda ch, i, core_ref: (ch, i, 0)))
    return pl.pallas_call(
        body, name=name, grid_spec=grid_spec, out_shape=jax.ShapeDtypeStruct((N_DEV // 2, r, cdim), WIRE_DTYPE),
        compiler_params=_params(("parallel", "parallel")))(core, x, tmp)


_ANY = pl.BlockSpec(memory_space=pl.ANY)
_CHIP_FLIPS = ((1, 0), (0, 1), (1, 1))


def _coords():
    return lax.axis_index("x"), lax.axis_index("y"), lax.axis_index("c")


def _flip(v, f):
    return 1 - v if f else v


def _a2a_direct(xs, name):
    n, ncp = len(xs), N_DEV - 1

    def body(*refs):
        ins, outs = refs[:n], refs[n:2 * n]
        send, recv, loc = refs[2 * n:]
        x, y, c = _coords()
        me = 4 * x + 2 * y + c
        local = [pltpu.make_async_copy(ins[i].at[me], outs[i].at[me], loc.at[i]) for i in range(n)]
        for cp in local:
            cp.start()
        remote = []
        for i in range(n):
            for k in range(1, N_DEV):
                px, py, pc = _flip(x, k & 4), _flip(y, k & 2), _flip(c, k & 1)
                cp = pltpu.make_async_remote_copy(
                    src_ref=ins[i].at[4 * px + 2 * py + pc], dst_ref=outs[i].at[me],
                    send_sem=send.at[i * ncp + k - 1], recv_sem=recv.at[i * ncp + k - 1],
                    device_id=(px, py, pc), device_id_type=MESH)
                cp.start()
                remote.append(cp)
        for cp in remote:
            cp.wait()
        for cp in local:
            cp.wait()

    return pl.pallas_call(
        body, name=name, in_specs=[_ANY] * n, out_specs=[_ANY] * n,
        out_shape=[jax.ShapeDtypeStruct(a.shape, a.dtype) for a in xs],
        scratch_shapes=[pltpu.SemaphoreType.DMA((n * ncp,)), pltpu.SemaphoreType.DMA((n * ncp,)),
                        pltpu.SemaphoreType.DMA((n,))])(*xs)


class _AllGatherSide:
    def __init__(self, blocks):
        self.operands = list(blocks)
        n = self.n = len(self.operands)
        self.n_in = self.n_out = n
        self.out_shape = [jax.ShapeDtypeStruct((N_DEV,) + a.shape, a.dtype) for a in self.operands]
        self.aliases = {}
        nici, nd2d = len(_CHIP_FLIPS), N_DEV // 2
        self.scratch = [pltpu.SemaphoreType.DMA((n * nici,)), pltpu.SemaphoreType.DMA((n * nici,)),
                        pltpu.SemaphoreType.DMA((n * nd2d,)), pltpu.SemaphoreType.DMA((n * nd2d,)),
                        pltpu.SemaphoreType.DMA((n,))]

    def _first(self, ins, outs, sems):
        send, recv, _, _, loc = sems
        x, y, c = _coords()
        me = 4 * x + 2 * y + c
        nici = len(_CHIP_FLIPS)
        local = [pltpu.make_async_copy(ins[i], outs[i].at[me], loc.at[i]) for i in range(self.n)]
        remote = [pltpu.make_async_remote_copy(
            src_ref=ins[i], dst_ref=outs[i].at[me], send_sem=send.at[i * nici + j], recv_sem=recv.at[i * nici + j],
            device_id=(_flip(x, fx), _flip(y, fy), c), device_id_type=MESH)
            for i in range(self.n) for j, (fx, fy) in enumerate(_CHIP_FLIPS)]
        return local + remote

    def _second(self, outs, sems):
        _, _, send, recv, _ = sems
        x, y, c = _coords()
        nd2d = N_DEV // 2
        return [pltpu.make_async_remote_copy(
            src_ref=outs[i].at[2 * ch + c], dst_ref=outs[i].at[2 * ch + c], send_sem=send.at[i * nd2d + ch],
            recv_sem=recv.at[i * nd2d + ch], device_id=(x, y, 1 - c), device_id_type=MESH)
            for i in range(self.n) for ch in range(nd2d)]

    def start(self, ins, outs, sems):
        for cp in self._first(ins, outs, sems):
            cp.start()

    def finish(self, ins, outs, sems):
        for cp in self._first(ins, outs, sems):
            cp.wait()
        second = self._second(outs, sems)
        for cp in second:
            cp.start()
        for cp in second:
            cp.wait()


class _ReduceScatterIciSide:
    def __init__(self, sums, accs, layer):
        self.operands = list(sums) + list(accs)
        n = self.n = len(sums)
        self.layer = layer
        self.n_in, self.n_out = 2 * n, n
        self.out_shape = [jax.ShapeDtypeStruct(a.shape, a.dtype) for a in accs]
        self.aliases = {n + i: i for i in range(n)}
        nici = len(_CHIP_FLIPS)
        self.scratch = [pltpu.SemaphoreType.DMA((n * nici,)), pltpu.SemaphoreType.DMA((n * nici,)),
                        pltpu.SemaphoreType.DMA((n,))]

    def _copies(self, ins, outs, sems):
        send, recv, loc = sems
        x, y, c = _coords()
        chip = 2 * x + y
        nici = len(_CHIP_FLIPS)
        local = [pltpu.make_async_copy(ins[i].at[chip], outs[i].at[self.layer, chip], loc.at[i])
                 for i in range(self.n)]
        remote = [pltpu.make_async_remote_copy(
            src_ref=ins[i].at[2 * _flip(x, fx) + _flip(y, fy)], dst_ref=outs[i].at[self.layer, chip],
            send_sem=send.at[i * nici + j], recv_sem=recv.at[i * nici + j],
            device_id=(_flip(x, fx), _flip(y, fy), c), device_id_type=MESH)
            for i in range(self.n) for j, (fx, fy) in enumerate(_CHIP_FLIPS)]
        return local + remote

    def start(self, ins, outs, sems):
        for cp in self._copies(ins, outs, sems):
            cp.start()

    def finish(self, ins, outs, sems):
        for cp in self._copies(ins, outs, sems):
            cp.wait()


def _run_side(side, name):
    def body(*refs):
        ins, outs = refs[:side.n_in], refs[side.n_in:side.n_in + side.n_out]
        sems = refs[side.n_in + side.n_out:]
        side.start(ins, outs, sems)
        side.finish(ins, outs, sems)

    return pl.pallas_call(
        body, name=name, in_specs=[_ANY] * side.n_in, out_specs=[_ANY] * side.n_out, out_shape=side.out_shape,
        input_output_aliases=side.aliases, scratch_shapes=side.scratch)(*side.operands)


class _ReduceScatterD2dSide:
    def __init__(self, parts):
        self.operands = list(parts)
        n = self.n = len(self.operands)
        self.n_in = self.n_out = n
        nd2d = N_DEV // 2
        self.out_shape = [jax.ShapeDtypeStruct((nd2d,) + a.shape[1:], a.dtype) for a in self.operands]
        self.aliases = {}
        self.scratch = [pltpu.SemaphoreType.DMA((n * nd2d,)), pltpu.SemaphoreType.DMA((n * nd2d,))]

    def _copies(self, ins, outs, sems):
        send, recv = sems
        x, y, c = _coords()
        nd2d = N_DEV // 2
        return [pltpu.make_async_remote_copy(
            src_ref=ins[i].at[2 * ch + 1 - c], dst_ref=outs[i].at[ch], send_sem=send.at[i * nd2d + ch],
            recv_sem=recv.at[i * nd2d + ch], device_id=(x, y, 1 - c), device_id_type=MESH)
            for i in range(self.n) for ch in range(nd2d)]

    def start(self, ins, outs, sems):
        for cp in self._copies(ins, outs, sems):
            cp.start()

    def finish(self, ins, outs, sems):
        for cp in self._copies(ins, outs, sems):
            cp.wait()


_PACK_ROWS = 256


def _pack(arrs):
    flat = jnp.concatenate([a.reshape(-1) for a in arrs])
    quantum = _PACK_ROWS * LANES
    total = -(-flat.shape[0] // quantum) * quantum
    return jnp.pad(flat, (0, total - flat.shape[0])).reshape(-1, LANES)


def _unpack(packed, like):
    flat, out, pos = packed.reshape(-1), [], 0
    for a in like:
        out.append(flat[pos:pos + a.size].reshape(a.shape))
        pos += a.size
    return out


def kernel(x, c, ada_w, ada_b, norm1_g, w_in, conv_w, spatial_w, spatial_b, v_norm_g, a_log, dt_bias, o_norm_g, w_branch_a, w_branch_b, w_out, norm2_g, w_ffn_in, w_ffn_out, final_g, loss_target, m_ada_w, m_ada_b, m_norm1_g, m_w_in, m_conv_w, m_spatial_w, m_spatial_b, m_v_norm_g, m_a_log, m_dt_bias, m_o_norm_g, m_w_branch_a, m_w_branch_b, m_w_out, m_norm2_g, m_w_ffn_in, m_w_ffn_out, m_final_g, v_ada_w, v_ada_b, v_norm1_g, v_w_in, v_conv_w, v_spatial_w, v_spatial_b, v_v_norm_g, v_a_log, v_dt_bias, v_o_norm_g, v_w_branch_a, v_w_branch_b, v_w_out, v_norm2_g, v_w_ffn_in, v_w_ffn_out, v_final_g):
    nl, d = ada_w.shape[0], x.shape[2]
    t = x.shape[1]
    nchunk = t // GDN_CHUNK
    xi, yi, ci = _coords()
    me = 4 * xi + 2 * yi + ci
    core = jnp.reshape(ci, (1,)).astype(jnp.int32)
    x0, target = x[0], loss_target[0]
    wcols = 3 * HEADS * HEAD_DIM
    lay = _ProjLayout(d)
    in_pieces = lay.pieces(w_in.shape[2])
    fi_shard = w_ffn_in.shape[2]
    fi_pieces = [(j, 0, fi_shard, fi_shard * j) for j in range(N_DEV)]

    c_all, cw_all = _a2a_direct([jnp.broadcast_to(c[None], (N_DEV,) + c.shape),
                                 jnp.broadcast_to(conv_w[None], (N_DEV,) + conv_w.shape)], "gather_small")
    c_all = c_all[:, 0]
    conv_full = cw_all.transpose(1, 2, 0, 3).reshape(nl, CONV_K, wcols)
    modp = _ada_fwd(c_all, ada_w, "ada_fwd")
    (modx,) = _a2a_direct([modp.transpose(1, 0, 2)], "mod_exchange")
    mod = (modx.transpose(1, 0, 2).reshape(nl, 6 * d) + ada_b).reshape(nl, 6, 1, d)

    big = (w_in, w_branch_a, w_branch_b, w_out, w_ffn_in, w_ffn_out)
    big_wire = [w.astype(WIRE_DTYPE) for w in big]
    gather_in = lambda i: _AllGatherSide([big_wire[0][i]])
    gather_early = lambda i: _AllGatherSide([big_wire[k][i] for k in (1, 2, 3, 5)])
    gather_late = lambda i: _AllGatherSide([big_wire[4][i]] + ([big_wire[0][i + 1]] if i + 1 < nl else []))
    row_full = lambda g: g.reshape(-1, g.shape[2])
    padded_in = lambda g: _cols_from_blocks(g, in_pieces, lay.width, "w_in_cols")
    w_pads = [padded_in(_run_side(gather_in(0), "ag_first")[0])] + [None] * (nl - 1)
    weights = [None] * nl

    def rows_of(ba_rows, lo):
        return ba_rows[lo:lo + HEADS].reshape(HEADS, nchunk, 1, GDN_CHUNK)

    saved = []
    x_cur, delta, gt_prev = x0, None, None
    for i in range(nl):
        sh1, sc1, gt1, sh2, sc2, gt2 = (mod[i, k] for k in range(6))
        s = dict(gt1=gt1, gt2=gt2, sc1=sc1, sc2=sc2)
        s["x_in"], s["h"] = _resid_norm(x_cur, delta, gt_prev, norm1_g[i][None], sc1, sh1, "norm1_fwd")
        s["proj"], g_a, g_b, g_o, g_fo = _matmul(s["h"], w_pads[i], "nn", "proj_fwd", out_dtype=ACT_DTYPE,
                                                 side=gather_early(i))
        ba = _matmul(s["h"], w_pads[i][:, lay.ba:], "nn", "proj_ba_fwd")
        s["b_col"] = spatial_b[i][:, :, None]
        s["ya"] = _mixer_a_fwd(s["proj"], lay.uv, spatial_w[i], s["b_col"], v_norm_g[i][None], "mixer_a_fwd")
        s["qkv_h"] = _conv_fwd(s["proj"], conv_full[i], "conv_fwd")
        ba_rows = _cols_as_rows(ba, 0, "ba_rows")
        s["braw"], s["araw"] = rows_of(ba_rows, 0), rows_of(ba_rows, HEADS)
        s["o"], s["states"], s["t_mats"], g_fi, *g_in = _gdn_fwd(
            s["qkv_h"], s["araw"], s["braw"], a_log[i], dt_bias[i], "gdn_fwd", gather_late(i))
        if g_in:
            w_pads[i + 1] = padded_in(g_in[0])
        weights[i] = (row_full(g_a), row_full(g_b), row_full(g_o),
                      _cols_from_blocks(g_fi, fi_pieces, N_DEV * fi_shard, "w_ffn_in_cols"), row_full(g_fo))
        w_a, w_b, w_o, w_fi, w_fo = weights[i]
        s["yb"] = _gdn_post_fwd(s["o"], s["proj"], lay.z, o_norm_g[i][None], "gdn_post_fwd")
        s["pa"] = _matmul(s["ya"], w_a, "nn", "branch_a_fwd")
        s["pb"] = _matmul(s["yb"], w_b, "nn", "branch_b_fwd")
        s["merged"] = _merge_fwd(s["pa"], s["pb"], s["proj"], lay.gates, "merge_fwd")
        s["mo"] = _matmul(s["merged"], w_o, "nn", "out_fwd")
        s["x1"], s["h2"] = _resid_norm(s["x_in"], s["mo"], gt1, norm2_g[i][None], sc2, sh2, "norm2_fwd")
        s["gu"] = _matmul(s["h2"], w_fi, "nn", "ffn_in_fwd", out_dtype=ACT_DTYPE)
        s["a"] = _swiglu_fwd(s["gu"], "swiglu_fwd")
        s["fo"] = _matmul(s["a"], w_fo, "nn", "ffn_out_fwd")
        saved.append(s)
        x_cur, delta, gt_prev = s["x1"], s["fo"], gt2
    dx, d_final_g, loss_tile = _final_loss(x_cur, delta, gt_prev, final_g[None], target, "final_loss")
    loss = lax.psum(loss_tile[0, 0], ("x", "y", "c"))

    big_shapes = [(d, w_in.shape[2]), w_branch_a.shape[1:], w_branch_b.shape[1:], w_out.shape[1:],
                  (w_ffn_in.shape[2], d), w_ffn_out.shape[1:]]
    accs = [lax.empty((nl, N_DEV // 2) + tuple(sh), WIRE_DTYPE) for sh in big_shapes]
    row_blocks = lambda g: g.reshape(N_DEV, -1, g.shape[1])
    dmod, small = [None] * nl, [None] * nl
    d_conv = [None] * nl
    parts, sums = None, None
    beside_gdn, beside_dw, beside_dx = (0,), (4,), (1, 2, 3, 5)
    rep_parts = [None] * nl
    rep_pack = lambda i: _pack((dmod[i],) + small[i])

    def scatter_side(idx, layer):
        if sums is None:
            return _NoSide
        return _ReduceScatterIciSide([sums[k] for k in idx], [accs[k] for k in idx], layer)

    def scattered_into(accs, idx, new):
        accs = list(accs)
        for k, a in zip(idx, new):
            accs[k] = a
        return accs

    for i in reversed(range(nl)):
        s = saved[i]
        w_a, w_b, w_o, w_fi, w_fo = weights[i]
        dfo, dgt2 = _gate_bwd(dx, s["fo"], s["gt2"], "gate2_bwd")
        g_fo = _matmul(s["a"], dfo, "tn", "ffn_out_dw")
        da = _matmul(dfo, w_fo, "nt", "ffn_out_dx")
        dgu = _swiglu_bwd(s["gu"], da, "swiglu_bwd")
        if parts is None:
            g_fi = _matmul(dgu, s["h2"], "tn", "ffn_in_dw")
        else:
            g_fi, *other = _matmul(dgu, s["h2"], "tn", "ffn_in_dw", side=_ReduceScatterD2dSide(parts))
            sums = [_pair_sum(p, o, core, "rs_pair_sum_%d" % k) for k, (p, o) in enumerate(zip(parts, other))]
        if i + 1 < nl:
            dh2, rep_parts[i + 1] = _matmul(dgu, w_fi, "nt", "ffn_in_dx", side=_AllGatherSide([rep_pack(i + 1)]))
        else:
            dh2 = _matmul(dgu, w_fi, "nt", "ffn_in_dx")
        dx1, dsh2, dsc2, dg2 = _norm_bwd(s["x1"], dh2, dx, norm2_g[i][None], s["sc2"], "norm2_bwd")
        dmo, dgt1 = _gate_bwd(dx1, s["mo"], s["gt1"], "gate1_bwd")
        g_o = _matmul(s["merged"], dmo, "tn", "out_dw")
        dmerged = _matmul(dmo, w_o, "nt", "out_dx")
        dproj = lax.empty((t, lay.width), MXU_DTYPE)
        dpa, dpb, dproj = _merge_bwd(dmerged, s["pa"], s["pb"], s["proj"], lay.gates, dproj, "merge_bwd")
        g_a = _matmul(s["ya"], dpa, "tn", "branch_a_dw")
        dya = _matmul(dpa, w_a, "nt", "branch_a_dx")
        g_b = _matmul(s["yb"], dpb, "tn", "branch_b_dw")
        dyb = _matmul(dpb, w_b, "nt", "branch_b_dx")
        dproj, d_ws, d_bs, d_gv = _mixer_a_bwd(s["proj"], lay.uv, dya, spatial_w[i], jnp.swapaxes(spatial_w[i], 1, 2),
                                               s["b_col"], v_norm_g[i][None], dproj, "mixer_a_bwd")
        do, dproj, d_go = _gdn_post_bwd(s["o"], s["proj"], lay.z, dyb, o_norm_g[i][None], dproj, "gdn_post_bwd")
        dq, dk, dv, d_ar, d_br, d_al, d_dt, *scattered = _gdn_bwd(
            s["qkv_h"], s["araw"], s["braw"], a_log[i], dt_bias[i], s["states"], s["t_mats"], do, "gdn_bwd",
            scatter_side(beside_gdn, i + 1))
        accs = scattered_into(accs, beside_gdn, scattered)
        dacc, d_conv[i] = _conv_bwd_pre(s["proj"], dq, dk, dv, conv_full[i], "conv_bwd_pre")
        dproj = _conv_bwd_in(dacc, conv_full[i], dproj, "conv_bwd_in")
        dba_rows = jnp.pad(jnp.concatenate([d_br.reshape(HEADS, t), d_ar.reshape(HEADS, t)]),
                           ((0, LANES - 2 * HEADS), (0, 0)))
        dproj = _rows_into_cols(dproj, dba_rows, lay.ba, "dproj_ba")
        if sums is None:
            g_pad = _matmul(s["h"], dproj, "tn", "proj_dw")
            dh = _matmul(dproj, w_pads[i], "nt", "proj_dx")
        else:
            g_pad, *scattered = _matmul(s["h"], dproj, "tn", "proj_dw", side=scatter_side(beside_dw, i + 1))
            accs = scattered_into(accs, beside_dw, scattered)
            dh, *scattered = _matmul(dproj, w_pads[i], "nt", "proj_dx", side=scatter_side(beside_dx, i + 1))
            accs = scattered_into(accs, beside_dx, scattered)
        dx, dsh1, dsc1, dg1 = _norm_bwd(s["x_in"], dh, dx1, norm1_g[i][None], s["sc1"], "norm1_bwd")
        dmod[i] = jnp.concatenate([dsh1, dsc1, dgt1, dsh2, dsc2, dgt2], axis=1)[0]
        small[i] = (dg1[0], d_ws, d_bs[:, :, 0], d_gv[0], d_al[:, 0, 0], d_dt[:, 0, 0], d_go[0], dg2[0])
        parts = [_blocks_from_cols(g_pad, in_pieces, w_in.shape[2], "w_in_blocks"), row_blocks(g_a), row_blocks(g_b),
                 row_blocks(g_o), row_blocks(g_fi), row_blocks(g_fo)]
    other = _run_side(_ReduceScatterD2dSide(parts), "rs_d2d_last")
    sums = [_pair_sum(p, o, core, "rs_pair_sum_%d" % k) for k, (p, o) in enumerate(zip(parts, other))]
    accs = _run_side(_ReduceScatterIciSide(sums, accs, 0), "rs_ici_last")

    rep_w = (ada_b, norm1_g, spatial_w, spatial_b, v_norm_g, a_log, dt_bias, o_norm_g, norm2_g)
    rep_m = (m_ada_b, m_norm1_g, m_spatial_w, m_spatial_b, m_v_norm_g, m_a_log, m_dt_bias, m_o_norm_g, m_norm2_g)
    rep_v = (v_ada_b, v_norm1_g, v_spatial_w, v_spatial_b, v_v_norm_g, v_a_log, v_dt_bias, v_o_norm_g, v_norm2_g)
    rep_parts[0], fin_parts = _run_side(_AllGatherSide([rep_pack(0), _pack([d_final_g[0]])]), "small_grads_last")
    dmod = jnp.stack(dmod)
    d_conv_blocks = jnp.stack(d_conv).reshape(nl, CONV_K, N_DEV, -1).transpose(2, 0, 1, 3).reshape(N_DEV, -1, LANES)
    dmod_blocks = dmod.reshape(nl, N_DEV, -1).transpose(1, 0, 2)
    conv_all, dmod_all = _a2a_direct([d_conv_blocks, dmod_blocks], "small_grads_scatter")
    by_layer = lambda arrs: jnp.stack([_pack([a[i] for a in arrs]) for i in range(nl)])
    rep_out = _sum_adam(jnp.stack(rep_parts), by_layer(rep_w), by_layer(rep_m), by_layer(rep_v), "adam_small")
    fin_out = _sum_adam(fin_parts[None], _pack([final_g])[None], _pack([m_final_g])[None], _pack([v_final_g])[None],
                        "adam_final_g")
    layer_like = [a[0] for a in rep_w]
    rep_out = [[jnp.stack(per_layer) for per_layer in zip(*[_unpack(o[i], layer_like) for i in range(nl)])]
               + _unpack(f[0], [final_g]) for o, f in zip(rep_out, fin_out)]
    conv_out = _sum_adam(conv_all[None], conv_w.reshape(1, -1, LANES), m_conv_w.reshape(1, -1, LANES),
                         v_conv_w.reshape(1, -1, LANES), "adam_conv")
    conv_out = [o.reshape(conv_w.shape) for o in conv_out]
    ada_out = _ada_bwd(c_all[:, :, None], dmod_all.transpose(1, 0, 2), ada_w, m_ada_w, v_ada_w, "ada_bwd_adam")
    big_m = (m_w_in, m_w_branch_a, m_w_branch_b, m_w_out, m_w_ffn_in, m_w_ffn_out)
    big_v = (v_w_in, v_w_branch_a, v_w_branch_b, v_w_out, v_w_ffn_in, v_w_ffn_out)
    turn = lambda k, a: jnp.swapaxes(a, 1, 2) if k == 4 else a
    big_out = [[turn(k, o) for o in _sum_adam(accs[k], turn(k, big[k]), turn(k, big_m[k]), turn(k, big_v[k]),
                                             "adam_big_%d" % k)] for k in range(6)]

    def ordered(kind):
        rep = rep_out[kind]
        return (ada_out[kind], rep[0], rep[1], big_out[0][kind], conv_out[kind], rep[2], rep[3], rep[4], rep[5],
                rep[6], rep[7], big_out[1][kind], big_out[2][kind], big_out[3][kind], rep[8], big_out[4][kind],
                big_out[5][kind], rep[9])

    return (loss, dx[None]) + ordered(0) + ordered(1) + ordered(2) + ordered(3)
```

```python
import functools

import jax
import jax.numpy as jnp
from jax import lax
from jax.experimental import pallas as pl
from jax.experimental.pallas import tpu as pltpu

F32 = jnp.float32
BF16 = jnp.bfloat16
MXU_DTYPE = BF16
WIRE_DTYPE = BF16
ACT_DTYPE = BF16
EPS = 1e-6
LANES = 128
SUBLANES = 8
GDN_CHUNK = 128
A_CHUNK = 128
GROUPS = 8
HEADS = 8
HEAD_DIM = 128
CONV_K = 4
N_DEV = 8
VMEM_LIMIT = 48 * 1024 * 1024
MESH = pl.DeviceIdType.MESH

ADAM_LR = 0.001
ADAM_B1 = 0.9
ADAM_B2 = 0.999
ADAM_EPS = 1e-08
ADAM_WD = 0.01
ADAM_STEP = 10

_NN = (((1,), (0,)), ((), ()))
_NT = (((1,), (1,)), ((), ()))
_TN = (((0,), (0,)), ((), ()))


def _mm(a, b, dims=_NN):
    return lax.dot_general(a.astype(MXU_DTYPE), b.astype(MXU_DTYPE), dims, preferred_element_type=F32)


def _mm_hi(a, b):
    return lax.dot_general(a, b, _NN, precision=lax.Precision.HIGHEST, preferred_element_type=F32)


def _tile(n, cands):
    for c in cands:
        if n % c == 0:
            return c
    return n


def _params(sem=None):
    return pltpu.CompilerParams(dimension_semantics=sem, vmem_limit_bytes=VMEM_LIMIT)


def _sigmoid(x):
    return 1.0 / (1.0 + jnp.exp(-x))


def _silu(x):
    return x * _sigmoid(x)


_GELU_C = 0.7978845608028654
_GELU_A = 0.044715


def _gelu(x):
    return 0.5 * x * (1.0 + jnp.tanh(_GELU_C * (x + _GELU_A * x * x * x)))


def _gelu_and_slope(x):
    t = jnp.tanh(_GELU_C * (x + _GELU_A * x * x * x))
    return 0.5 * x * (1.0 + t), 0.5 * (1.0 + t) + 0.5 * x * (1.0 - t * t) * _GELU_C * (1.0 + 3.0 * _GELU_A * x * x)


def _softplus(x):
    return jnp.maximum(x, 0.0) + jnp.log(1.0 + jnp.exp(-jnp.abs(x)))


_MM_TILES = (1024, 1408, 1664, 512, 256, 128)


class _NoSide:
    operands, out_shape, scratch, aliases, n_in, n_out = [], [], [], {}, 0, 0


def _side_hooks(side, refs, n_main_in, n_main_out, n_main_scratch, grid):
    a = n_main_in + side.n_in
    b = a + n_main_out + side.n_out
    ins, outs, sems = refs[n_main_in:a], refs[a + n_main_out:b], refs[b + n_main_scratch:]
    main = refs[:n_main_in] + refs[a:a + n_main_out] + refs[b:b + n_main_scratch]
    ids = [pl.program_id(k) for k in range(len(grid))]

    def start():
        if side.n_in:
            pl.when(functools.reduce(jnp.logical_and, [i == 0 for i in ids]))(lambda: side.start(ins, outs, sems))

    def finish():
        if side.n_in:
            last = functools.reduce(jnp.logical_and, [i == g - 1 for i, g in zip(ids, grid)])
            pl.when(last)(lambda: side.finish(ins, outs, sems))

    return main, start, finish


def _carrier_call(body, name, grid, in_specs, out_specs, out_shape, scratch, side, args):
    aliases = {len(in_specs) + k: len(out_specs) + v for k, v in side.aliases.items()}
    return pl.pallas_call(
        body, name=name, grid=grid, in_specs=list(in_specs) + [_ANY] * side.n_in,
        out_specs=list(out_specs) + [_ANY] * side.n_out, out_shape=list(out_shape) + list(side.out_shape),
        scratch_shapes=list(scratch) + list(side.scratch), input_output_aliases=aliases,
        compiler_params=_params(("arbitrary",) * len(grid)))(*args, *side.operands)


_MM_VMEM_BUDGET = 44 * 1024 * 1024


def _matmul_tiles(mode, m, n, k, out_bytes):
    tk = _tile(k, _MM_TILES)
    tm = _tile(m, _MM_TILES)
    in_bytes = jnp.dtype(MXU_DTYPE).itemsize
    for tn in _MM_TILES:
        if n % tn:
            continue
        need = 2 * in_bytes * (tm * tk + tk * tn) + tm * tn * (2 * out_bytes + (4 if k > tk else 0))
        if need <= _MM_VMEM_BUDGET:
            return tm, tn, tk
    return tm, _tile(n, (LANES,)), tk


def _matmul(a, b, mode, name, out_dtype=F32, side=_NoSide):
    if mode == "nn":
        (m, k), n = a.shape, b.shape[1]
    elif mode == "nt":
        (m, k), n = a.shape, b.shape[0]
    else:
        (k, m), n = a.shape, b.shape[1]
    tm, tn, tk = _matmul_tiles(mode, m, n, k, jnp.dtype(out_dtype).itemsize)
    nk = k // tk
    grid = (m // tm, n // tn, nk)
    dims = {"nn": _NN, "nt": _NT, "tn": _TN}[mode]

    def body(*refs):
        (a_ref, b_ref, o_ref, acc_ref), side_start, side_finish = _side_hooks(side, refs, 2, 1, 1, grid)
        kk = pl.program_id(2)
        side_start()
        if nk == 1:
            o_ref[...] = _mm(a_ref[...], b_ref[...], dims).astype(o_ref.dtype)
        else:
            @pl.when(kk == 0)
            def _():
                acc_ref[...] = _mm(a_ref[...], b_ref[...], dims)

            @pl.when(jnp.logical_and(kk > 0, kk < nk - 1))
            def _():
                acc_ref[...] += _mm(a_ref[...], b_ref[...], dims)

            @pl.when(kk == nk - 1)
            def _():
                o_ref[...] = (acc_ref[...] + _mm(a_ref[...], b_ref[...], dims)).astype(o_ref.dtype)

        side_finish()

    a_spec = (pl.BlockSpec((tk, tm), lambda i, j, l: (l, i)) if mode == "tn"
              else pl.BlockSpec((tm, tk), lambda i, j, l: (i, l)))
    b_spec = (pl.BlockSpec((tn, tk), lambda i, j, l: (j, l)) if mode == "nt"
              else pl.BlockSpec((tk, tn), lambda i, j, l: (l, j)))
    o_spec = pl.BlockSpec((tm, tn), lambda i, j, l: (i, j))
    out = _carrier_call(body, name, grid, [a_spec, b_spec], [o_spec], [jax.ShapeDtypeStruct((m, n), out_dtype)],
                        [pltpu.VMEM((tm, tn) if nk > 1 else (SUBLANES, LANES), F32)], side, (a, b))
    return out if side.n_in else out[0]


_ROW_TILES = (512, 256, 128)


def _resid_norm(x, delta, gt, g, sc, sh, name):
    t, d = x.shape
    tt = _tile(t, _ROW_TILES)
    has = delta is not None

    def body(*refs):
        if has:
            x_ref, d_ref, gt_ref, g_ref, sc_ref, sh_ref, xo_ref, h_ref = refs
            xv = x_ref[...] + gt_ref[...] * d_ref[...]
            xo_ref[...] = xv
        else:
            x_ref, g_ref, sc_ref, sh_ref, h_ref = refs
            xv = x_ref[...]
        r = lax.rsqrt(jnp.mean(xv * xv, axis=-1, keepdims=True) + EPS)
        y = xv * r * g_ref[...]
        h_ref[...] = (y * (1.0 + sc_ref[...]) + sh_ref[...]).astype(h_ref.dtype)

    row = pl.BlockSpec((tt, d), lambda i: (i, 0))
    vec = pl.BlockSpec((1, d), lambda i: (0, 0))
    if has:
        return pl.pallas_call(
            body, name=name, grid=(t // tt,), in_specs=[row, row, vec, vec, vec, vec], out_specs=[row, row],
            out_shape=[jax.ShapeDtypeStruct((t, d), F32), jax.ShapeDtypeStruct((t, d), MXU_DTYPE)],
            compiler_params=_params(("parallel",)))(x, delta, gt, g, sc, sh)
    h = pl.pallas_call(
        body, name=name + "_first", grid=(t // tt,), in_specs=[row, vec, vec, vec], out_specs=row,
        out_shape=jax.ShapeDtypeStruct((t, d), MXU_DTYPE), compiler_params=_params(("parallel",)))(x, g, sc, sh)
    return x, h


def _final_loss(x, delta, gt, g, target, name):
    t, d = x.shape
    tt = _tile(t, _ROW_TILES)

    def body(x_ref, d_ref, gt_ref, g_ref, tg_ref, dx_ref, dg_ref, loss_ref):
        @pl.when(pl.program_id(0) == 0)
        def _():
            dg_ref[...] = jnp.zeros_like(dg_ref)
            loss_ref[...] = jnp.zeros_like(loss_ref)

        xv = x_ref[...] + gt_ref[...] * d_ref[...]
        r = lax.rsqrt(jnp.mean(xv * xv, axis=-1, keepdims=True) + EPS)
        xh = xv * r
        diff = xh * g_ref[...] - tg_ref[...]
        loss_ref[...] += jnp.sum(diff * diff) * (0.5 / d)
        dy = diff * (1.0 / d)
        dg_ref[...] += jnp.sum(dy * xh, axis=0, keepdims=True)
        dxh = dy * g_ref[...]
        dx_ref[...] = r * (dxh - xh * jnp.mean(dxh * xh, axis=-1, keepdims=True))

    row = pl.BlockSpec((tt, d), lambda i: (i, 0))
    vec = pl.BlockSpec((1, d), lambda i: (0, 0))
    tile = pl.BlockSpec((SUBLANES, LANES), lambda i: (0, 0))
    return pl.pallas_call(
        body, name=name, grid=(t // tt,), in_specs=[row, row, vec, vec, row], out_specs=[row, vec, tile],
        out_shape=[jax.ShapeDtypeStruct((t, d), F32), jax.ShapeDtypeStruct((1, d), F32),
                   jax.ShapeDtypeStruct((SUBLANES, LANES), F32)],
        compiler_params=_params(("arbitrary",)))(x, delta, gt, g, target)


def _norm_bwd(x, dh, dres, g, sc, name, gate=None):
    t, d = x.shape
    tt = _tile(t, _ROW_TILES)
    gated = gate is not None

    def body(*refs):
        x_ref, dh_ref, dr_ref, g_ref, sc_ref = refs[:5]
        dx_ref, dsh_ref, dsc_ref, dg_ref = refs[5 + 2 * gated:9 + 2 * gated]

        @pl.when(pl.program_id(0) == 0)
        def _():
            for acc_ref in refs[6 + 2 * gated:9 + 2 * gated] + refs[10 + 2 * gated:]:
                acc_ref[...] = jnp.zeros_like(acc_ref)

        xv, dh = x_ref[...], dh_ref[...]
        r = lax.rsqrt(jnp.mean(xv * xv, axis=-1, keepdims=True) + EPS)
        xh = xv * r
        gv, sc1 = g_ref[...], 1.0 + sc_ref[...]
        dsh_ref[...] += jnp.sum(dh, axis=0, keepdims=True)
        dsc_ref[...] += jnp.sum(dh * xh, axis=0, keepdims=True) * gv
        dg_ref[...] += jnp.sum(dh * xh, axis=0, keepdims=True) * sc1
        dxh = dh * (gv * sc1)
        dx = dr_ref[...] + r * (dxh - xh * jnp.mean(dxh * xh, axis=-1, keepdims=True))
        dx_ref[...] = dx
        if gated:
            br_ref, gt_ref, db_ref, dgt_ref = refs[5], refs[6], refs[11], refs[12]
            db_ref[...] = (dx * gt_ref[...]).astype(db_ref.dtype)
            dgt_ref[...] += jnp.sum(dx * br_ref[...], axis=0, keepdims=True)

    row = pl.BlockSpec((tt, d), lambda i: (i, 0))
    vec = pl.BlockSpec((1, d), lambda i: (0, 0))
    vshape = jax.ShapeDtypeStruct((1, d), F32)
    in_specs, out_specs = [row, row, row, vec, vec], [row, vec, vec, vec]
    out_shape = [jax.ShapeDtypeStruct((t, d), F32), vshape, vshape, vshape]
    if gated:
        in_specs, out_specs = in_specs + [row, vec], out_specs + [row, vec]
        out_shape = out_shape + [jax.ShapeDtypeStruct((t, d), MXU_DTYPE), vshape]
    return pl.pallas_call(
        body, name=name + ("_gate" if gated else ""), grid=(t // tt,), in_specs=in_specs, out_specs=out_specs,
        out_shape=out_shape, compiler_params=_params(("arbitrary",)))(x, dh, dres, g, sc, *(gate or ()))


def _gate_bwd(dxo, branch, gt, name):
    t, d = dxo.shape
    tt = _tile(t, _ROW_TILES)

    def body(dx_ref, br_ref, gt_ref, db_ref, dgt_ref):
        @pl.when(pl.program_id(0) == 0)
        def _():
            dgt_ref[...] = jnp.zeros_like(dgt_ref)

        dx = dx_ref[...]
        db_ref[...] = (dx * gt_ref[...]).astype(db_ref.dtype)
        dgt_ref[...] += jnp.sum(dx * br_ref[...], axis=0, keepdims=True)

    row = pl.BlockSpec((tt, d), lambda i: (i, 0))
    vec = pl.BlockSpec((1, d), lambda i: (0, 0))
    return pl.pallas_call(
        body, name=name, grid=(t // tt,), in_specs=[row, row, vec], out_specs=[row, vec],
        out_shape=[jax.ShapeDtypeStruct((t, d), MXU_DTYPE), jax.ShapeDtypeStruct((1, d), F32)],
        compiler_params=_params(("arbitrary",)))(dxo, branch, gt)


def _swiglu_fwd(gu, name):
    t, f2 = gu.shape
    f = f2 // 2
    tt = _tile(t, (256, 128))

    def body(g_ref, u_ref, o_ref):
        o_ref[...] = (_silu(g_ref[...].astype(F32)) * u_ref[...].astype(F32)).astype(o_ref.dtype)

    return pl.pallas_call(
        body, name=name, grid=(t // tt,),
        in_specs=[pl.BlockSpec((tt, f), lambda i: (i, 0)), pl.BlockSpec((tt, f), lambda i: (i, 1))],
        out_specs=pl.BlockSpec((tt, f), lambda i: (i, 0)), out_shape=jax.ShapeDtypeStruct((t, f), MXU_DTYPE),
        compiler_params=_params(("parallel",)))(gu, gu)


def _swiglu_bwd(gu, da, name):
    t, f2 = gu.shape
    f = f2 // 2
    tt = _tile(t, (256, 128))

    def body(g_ref, u_ref, da_ref, o_ref):
        gate, da = g_ref[...].astype(F32), da_ref[...]
        sg = _sigmoid(gate)
        o_ref[:, :f] = (da * u_ref[...].astype(F32) * (sg * (1.0 + gate * (1.0 - sg)))).astype(o_ref.dtype)
        o_ref[:, f:] = (da * (gate * sg)).astype(o_ref.dtype)

    return pl.pallas_call(
        body, name=name, grid=(t // tt,),
        in_specs=[pl.BlockSpec((tt, f), lambda i: (i, 0)), pl.BlockSpec((tt, f), lambda i: (i, 1)),
                  pl.BlockSpec((tt, f), lambda i: (i, 0))],
        out_specs=pl.BlockSpec((tt, f2), lambda i: (i, 0)), out_shape=jax.ShapeDtypeStruct((t, f2), MXU_DTYPE),
        compiler_params=_params(("parallel",)))(gu, gu, da)


class _ProjLayout:
    def __init__(self, d):
        wc = 3 * HEADS * HEAD_DIM
        self.d, self.wc = d, wc
        self.qkv, self.z, self.uv, self.gates, self.ba = 0, wc, wc + d, wc + 3 * d, wc + 5 * d
        self.width = self.ba + LANES
        assert self.z % d == 0 and self.uv % (2 * d) == 0 and self.gates % (2 * d) == 0 and self.ba % LANES == 0

    def pieces(self, shard):
        d, wc, out, lo = self.d, self.wc, [], 0
        for length, dst in ((2 * d, self.uv), (wc, self.qkv), (d, self.z), (2 * HEADS, self.ba), (2 * d, self.gates)):
            pos = lo
            while pos < lo + length:
                j = pos // shard
                n = min(lo + length, (j + 1) * shard) - pos
                out.append((j, pos - j * shard, n, dst + pos - lo))
                pos += n
            lo += length
        return out


def _merge_fwd(pa, pb, proj, gcol, name):
    t, d = pa.shape
    tt = _tile(t, _ROW_TILES)

    def body(pa_ref, pb_ref, ga_ref, gb_ref, o_ref):
        sa, sb = _sigmoid(ga_ref[...].astype(F32)), _sigmoid(gb_ref[...].astype(F32))
        o_ref[...] = (sa * pa_ref[...] + sb * pb_ref[...]).astype(o_ref.dtype)

    row = pl.BlockSpec((tt, d), lambda i: (i, 0))
    gate = lambda k: pl.BlockSpec((tt, d), lambda i: (i, gcol // d + k))
    return pl.pallas_call(
        body, name=name, grid=(t // tt,), in_specs=[row, row, gate(0), gate(1)], out_specs=row,
        out_shape=jax.ShapeDtypeStruct((t, d), MXU_DTYPE), compiler_params=_params(("parallel",)))(pa, pb, proj, proj)


def _merge_bwd(dm, pa, pb, proj, gcol, dproj, name):
    t, d = pa.shape
    tt = _tile(t, _ROW_TILES)

    def body(dm_ref, pa_ref, pb_ref, ga_ref, gb_ref, _, dpa_ref, dpb_ref, dg_ref):
        dm = dm_ref[...]
        sa, sb = _sigmoid(ga_ref[...].astype(F32)), _sigmoid(gb_ref[...].astype(F32))
        dpa_ref[...] = (dm * sa).astype(dpa_ref.dtype)
        dpb_ref[...] = (dm * sb).astype(dpb_ref.dtype)
        dg_ref[:, :d] = (dm * pa_ref[...] * sa * (1.0 - sa)).astype(dg_ref.dtype)
        dg_ref[:, d:] = (dm * pb_ref[...] * sb * (1.0 - sb)).astype(dg_ref.dtype)

    row = pl.BlockSpec((tt, d), lambda i: (i, 0))
    gate = lambda k: pl.BlockSpec((tt, d), lambda i: (i, gcol // d + k))
    wide = pl.BlockSpec((tt, 2 * d), lambda i: (i, gcol // (2 * d)))
    return pl.pallas_call(
        body, name=name, grid=(t // tt,), in_specs=[row, row, row, gate(0), gate(1), _ANY], out_specs=[row, row, wide],
        out_shape=[jax.ShapeDtypeStruct((t, d), MXU_DTYPE), jax.ShapeDtypeStruct((t, d), MXU_DTYPE),
                   jax.ShapeDtypeStruct(dproj.shape, dproj.dtype)],
        input_output_aliases={5: 2}, compiler_params=_params(("parallel",)))(dm, pa, pb, proj, proj, dproj)


def _tri_masks(n):
    ri = lax.broadcasted_iota(jnp.int32, (n, n), 0)
    ci = lax.broadcasted_iota(jnp.int32, (n, n), 1)
    return ri >= ci, ri > ci, ri == ci


def _mixer_a_fwd(proj, ucol, w_s, b_col, g_v, name):
    t, w = proj.shape[0], g_v.shape[1]
    c = A_CHUNK

    def body(u_ref, v_ref, w_ref, b_ref, gv_ref, y_ref):
        tril, _, _ = _tri_masks(c)
        ug, vg = _gelu(u_ref[...].astype(F32)), _gelu(v_ref[...].astype(F32))
        for g in range(GROUPS):
            sl = slice(g * c, (g + 1) * c)
            vt = vg[:, sl]
            r = lax.rsqrt(jnp.mean(vt * vt, axis=-1, keepdims=True) + EPS)
            vn = vt * r * gv_ref[:, sl]
            s = _mm(jnp.where(tril, w_ref[g], 0.0), vn) + b_ref[g]
            y_ref[:, sl] = (ug[:, sl] * s).astype(y_ref.dtype)

    return pl.pallas_call(
        body, name=name, grid=(t // c,),
        in_specs=[pl.BlockSpec((c, w), lambda i: (i, ucol // w)), pl.BlockSpec((c, w), lambda i: (i, ucol // w + 1)),
                  pl.BlockSpec((GROUPS, c, c), lambda i: (0, 0, 0)), pl.BlockSpec((GROUPS, c, 1), lambda i: (0, 0, 0)),
                  pl.BlockSpec((1, w), lambda i: (0, 0))],
        out_specs=pl.BlockSpec((c, w), lambda i: (i, 0)), out_shape=jax.ShapeDtypeStruct((t, w), MXU_DTYPE),
        compiler_params=_params(("parallel",)))(proj, proj, w_s, b_col, g_v)


def _mixer_a_bwd(proj, ucol, dy, w_s, w_st, b_col, g_v, dproj, name):
    t, w = proj.shape[0], g_v.shape[1]
    w2 = 2 * w
    c = A_CHUNK

    def body(u_ref, v_ref, dy_ref, w_ref, wt_ref, b_ref, gv_ref, _, duv_ref, dw_ref, db_ref, dgv_ref):
        @pl.when(pl.program_id(0) == 0)
        def _():
            dw_ref[...] = jnp.zeros_like(dw_ref)
            db_ref[...] = jnp.zeros_like(db_ref)
            dgv_ref[...] = jnp.zeros_like(dgv_ref)

        tril, _, _ = _tri_masks(c)
        triu = lax.broadcasted_iota(jnp.int32, (c, c), 0) <= lax.broadcasted_iota(jnp.int32, (c, c), 1)
        (ug, dug), (vg, dvg) = _gelu_and_slope(u_ref[...].astype(F32)), _gelu_and_slope(v_ref[...].astype(F32))
        for g in range(GROUPS):
            sl = slice(g * c, (g + 1) * c)
            vt = vg[:, sl]
            r = lax.rsqrt(jnp.mean(vt * vt, axis=-1, keepdims=True) + EPS)
            vh = vt * r
            gv = gv_ref[:, sl]
            vn = vh * gv
            s = _mm(jnp.where(tril, w_ref[g], 0.0), vn) + b_ref[g]
            dy = dy_ref[:, sl]
            ds = dy * ug[:, sl]
            dw_ref[g] += jnp.where(tril, _mm(ds, vn, _NT), 0.0)
            db_ref[g] += jnp.sum(ds, axis=1, keepdims=True)
            dvn = _mm(jnp.where(triu, wt_ref[g], 0.0), ds)
            dgv_ref[:, sl] += jnp.sum(dvn * vh, axis=0, keepdims=True)
            dvh = dvn * gv
            dvt = r * (dvh - vh * jnp.mean(dvh * vh, axis=-1, keepdims=True))
            duv_ref[:, sl] = (dy * s * dug[:, sl]).astype(duv_ref.dtype)
            duv_ref[:, w + g * c:w + (g + 1) * c] = (dvt * dvg[:, sl]).astype(duv_ref.dtype)

    full3 = lambda shape: pl.BlockSpec(shape, lambda i: (0, 0, 0))
    return pl.pallas_call(
        body, name=name, grid=(t // c,),
        in_specs=[pl.BlockSpec((c, w), lambda i: (i, ucol // w)), pl.BlockSpec((c, w), lambda i: (i, ucol // w + 1)),
                  pl.BlockSpec((c, w), lambda i: (i, 0)), full3((GROUPS, c, c)), full3((GROUPS, c, c)),
                  full3((GROUPS, c, 1)), pl.BlockSpec((1, w), lambda i: (0, 0)), _ANY],
        out_specs=[pl.BlockSpec((c, w2), lambda i: (i, ucol // w2)), full3((GROUPS, c, c)), full3((GROUPS, c, 1)),
                   pl.BlockSpec((1, w), lambda i: (0, 0))],
        out_shape=[jax.ShapeDtypeStruct(dproj.shape, dproj.dtype), jax.ShapeDtypeStruct((GROUPS, c, c), F32),
                   jax.ShapeDtypeStruct((GROUPS, c, 1), F32), jax.ShapeDtypeStruct((1, w), F32)],
        input_output_aliases={7: 0},
        compiler_params=_params(("arbitrary",)))(proj, proj, dy, w_s, w_st, b_col, g_v, dproj)


_Q_SCALE = HEAD_DIM ** -0.5


CONV_HALO = 16


def _conv_taps(x_ref, p_ref, w_ref):
    prev = jnp.where(pl.program_id(0) > 0, p_ref[...].astype(F32), 0.0)
    ext = jnp.concatenate([prev, x_ref[...].astype(F32)], axis=0)
    shifted = [ext[CONV_HALO:]] + [pltpu.roll(ext, s, 0)[CONV_HALO:] for s in range(1, CONV_K)]
    acc = shifted[0] * w_ref[pl.ds(CONV_K - 1, 1), :]
    for s in range(1, CONV_K):
        acc = acc + shifted[s] * w_ref[pl.ds(CONV_K - 1 - s, 1), :]
    return acc, shifted


def _conv_fwd(qkv, w, name):
    t, cw = qkv.shape[0], w.shape[1]
    tt = _tile(t, (256, 128))
    hb = tt // CONV_HALO

    def body(x_ref, p_ref, w_ref, o_ref):
        acc, _ = _conv_taps(x_ref, p_ref, w_ref)
        y = _silu(acc)
        for which in range(3):
            for h in range(HEADS):
                lo = (which * HEADS + h) * HEAD_DIM
                seg = y[:, lo:lo + HEAD_DIM]
                if which < 2:
                    seg = seg * lax.rsqrt(jnp.sum(seg * seg, axis=-1, keepdims=True) + EPS)
                if which == 0:
                    seg = seg * _Q_SCALE
                o_ref[which, h] = seg

    return pl.pallas_call(
        body, name=name, grid=(t // tt,),
        in_specs=[pl.BlockSpec((tt, cw), lambda i: (i, 0)),
                  pl.BlockSpec((CONV_HALO, cw), lambda i: (jnp.maximum(i * hb - 1, 0), 0)),
                  pl.BlockSpec((CONV_K, cw), lambda i: (0, 0))],
        out_specs=pl.BlockSpec((3, HEADS, tt, HEAD_DIM), lambda i: (0, 0, i, 0)),
        out_shape=jax.ShapeDtypeStruct((3, HEADS, t, HEAD_DIM), F32),
        compiler_params=_params(("parallel",)))(qkv, qkv, w)


def _conv_bwd_pre(qkv, dq, dk, dv, w, name):
    t, cw = qkv.shape[0], w.shape[1]
    tt = _tile(t, (256, 128))
    hb = tt // CONV_HALO

    def body(x_ref, p_ref, dq_ref, dk_ref, dv_ref, w_ref, da_ref, dw_ref):
        @pl.when(pl.program_id(0) == 0)
        def _():
            dw_ref[...] = jnp.zeros_like(dw_ref)

        acc, shifted = _conv_taps(x_ref, p_ref, w_ref)
        sg = _sigmoid(acc)
        y = acc * sg
        dsilu = sg * (1.0 + acc * (1.0 - sg))
        d_refs = (dq_ref, dk_ref, dv_ref)
        for which in range(3):
            for h in range(HEADS):
                lo = (which * HEADS + h) * HEAD_DIM
                sl = slice(lo, lo + HEAD_DIM)
                dn = d_refs[which][h]
                if which < 2:
                    seg = y[:, sl]
                    rho = lax.rsqrt(jnp.sum(seg * seg, axis=-1, keepdims=True) + EPS)
                    nrm = seg * rho
                    if which == 0:
                        dn = dn * _Q_SCALE
                    dn = rho * (dn - nrm * jnp.sum(dn * nrm, axis=-1, keepdims=True))
                dacc = dn * dsilu[:, sl]
                da_ref[:, sl] = dacc
                for s in range(CONV_K):
                    dw_ref[pl.ds(CONV_K - 1 - s, 1), sl] += jnp.sum(dacc * shifted[s][:, sl], axis=0, keepdims=True)

    head = pl.BlockSpec((HEADS, tt, HEAD_DIM), lambda i: (0, i, 0))
    return pl.pallas_call(
        body, name=name, grid=(t // tt,),
        in_specs=[pl.BlockSpec((tt, cw), lambda i: (i, 0)),
                  pl.BlockSpec((CONV_HALO, cw), lambda i: (jnp.maximum(i * hb - 1, 0), 0)),
                  head, head, head, pl.BlockSpec((CONV_K, cw), lambda i: (0, 0))],
        out_specs=[pl.BlockSpec((tt, cw), lambda i: (i, 0)), pl.BlockSpec((CONV_K, cw), lambda i: (0, 0))],
        out_shape=[jax.ShapeDtypeStruct((t, cw), F32), jax.ShapeDtypeStruct((CONV_K, cw), F32)],
        compiler_params=_params(("arbitrary",)))(qkv, qkv, dq, dk, dv, w)


def _conv_bwd_in(dacc, w, dproj, name):
    t, cw = dacc.shape
    tt = _tile(t, (256, 128))
    hb = tt // SUBLANES
    nt = t // tt
    rows = tt + SUBLANES

    def body(d_ref, n_ref, w_ref, _, o_ref):
        cur = d_ref[...]
        nxt = jnp.where(pl.program_id(0) < nt - 1, n_ref[...], 0.0)
        ext = jnp.concatenate([cur, nxt], axis=0)
        acc = cur * w_ref[pl.ds(CONV_K - 1, 1), :]
        for s in range(1, CONV_K):
            acc = acc + pltpu.roll(ext, rows - s, 0)[:tt] * w_ref[pl.ds(CONV_K - 1 - s, 1), :]
        o_ref[...] = acc.astype(o_ref.dtype)

    return pl.pallas_call(
        body, name=name, grid=(nt,),
        in_specs=[pl.BlockSpec((tt, cw), lambda i: (i, 0)),
                  pl.BlockSpec((SUBLANES, cw), lambda i: (jnp.minimum((i + 1) * hb, t // SUBLANES - 1), 0)),
                  pl.BlockSpec((CONV_K, cw), lambda i: (0, 0)), _ANY],
        out_specs=pl.BlockSpec((tt, cw), lambda i: (i, 0)), out_shape=jax.ShapeDtypeStruct(dproj.shape, dproj.dtype),
        input_output_aliases={3: 0}, compiler_params=_params(("parallel",)))(dacc, dacc, w, dproj)


_INV_BASE_SHIFT = 3


def _inv_unit_lower(a, eye):
    c = GDN_CHUNK
    ri = lax.broadcasted_iota(jnp.int32, (c, c), 0)
    ci = lax.broadcasted_iota(jnp.int32, (c, c), 1)
    same = lambda sh: (ri >> sh) == (ci >> sh)
    x = jnp.where(same(_INV_BASE_SHIFT), -a, 0.0)
    p = jnp.where(eye, 1.0, 0.0) + x
    xs = _split(x)
    x2 = _mm3(xs, xs)
    x2s, ps = _split(x2), _split(p)
    r = _mm3(x2s, tuple(jnp.concatenate([u, v], axis=-1) for u, v in zip(x2s, ps)))
    x4, p = r[..., :c], p + r[..., c:]
    p = p + _mm3(_split(x4), _split(p))
    for sh in range(_INV_BASE_SHIFT, c.bit_length() - 1):
        off = jnp.where(same(sh + 1) & jnp.logical_not(same(sh)), a, 0.0)
        ps = _split(p)
        p = p - _mm3(ps, _split(_mm3(_split(off), ps)))
    return p


def _split(a):
    hi = a.astype(BF16)
    return hi, (a - hi.astype(F32)).astype(BF16)


def _dot_heads(u, v, dims):
    if u.ndim == 3:
        return jnp.stack([_dot_heads(u[j], v[j], dims) for j in range(u.shape[0])])
    return lax.dot_general(u, v, dims, preferred_element_type=F32)


def _mm3(a, b):
    return _dot_heads(a[0], b[0], _NN) + (_dot_heads(a[0], b[1], _NN) + _dot_heads(a[1], b[0], _NN))


def _hmm(a, b, dims=_NN):
    return _dot_heads(a.astype(MXU_DTYPE), b.astype(MXU_DTYPE), dims)


def _rowsum(x):
    return jnp.sum(x, axis=-1, keepdims=True)


def _colsum(x):
    return jnp.sum(x, axis=-2, keepdims=True)


class _Pre:
    pass


def _gdn_pre(q, k, v, araw, braw, alog, dtb, t_mat=None):
    c = GDN_CHUNK
    p = _Pre()
    p.tril, p.strict, p.eye = _tri_masks(c)
    p.to_col = lambda row: _rowsum(jnp.where(p.eye, row, 0.0))
    p.to_row = lambda col: _colsum(jnp.where(p.eye, col, 0.0))
    p.a_neg = -jnp.exp(alog + jnp.zeros((1, c), F32))
    p.xg = araw + dtb
    p.g_row = p.a_neg * _softplus(p.xg)
    p.beta_row = _sigmoid(braw)
    p.beta = p.to_col(p.beta_row)
    gam = _rowsum(jnp.where(p.tril, p.g_row, 0.0))
    gam_last = _rowsum(p.g_row)
    p.dm = jnp.where(p.tril, jnp.exp(jnp.where(p.tril, gam - p.to_row(gam), 0.0)), 0.0)
    p.e, p.ek, p.el = jnp.exp(gam), jnp.exp(gam_last - gam), jnp.exp(gam_last)
    p.kb = k * p.beta
    p.kk = _hmm(p.kb, k, _NT)
    p.t = _inv_unit_lower(jnp.where(p.strict, p.kk * p.dm, 0.0), p.eye) if t_mat is None else t_mat
    p.vb, p.kbe = v * p.beta, p.kb * p.e
    uw = _hmm(p.t, jnp.concatenate([p.vb, p.kbe], axis=-1))
    p.u, p.w = uw[..., :v.shape[-1]], uw[..., v.shape[-1]:]
    p.qk0 = _hmm(q, k, _NT)
    p.qk = p.qk0 * p.dm
    p.qd, p.kd = q * p.e, k * p.ek
    return p


GDN_HEADS_PER_STEP = 8


def _head_scalars(ref, hb):
    h0 = pl.program_id(0) * hb
    return jnp.stack([jnp.full((1, 1), ref[h0 + j], F32) for j in range(hb)])


def _gdn_specs(n, reverse):
    c, dk, hb = GDN_CHUNK, HEAD_DIM, GDN_HEADS_PER_STEP
    ix = (lambda i: n - 1 - i) if reverse else (lambda i: i)
    smem = pl.BlockSpec(memory_space=pltpu.SMEM)
    qkv = [pl.BlockSpec((None, hb, c, dk), functools.partial(lambda w, h, i: (w, h, ix(i), 0), w)) for w in range(3)]
    row = pl.BlockSpec((hb, None, 1, c), lambda h, i: (h, ix(i), 0, 0))
    tok = pl.BlockSpec((hb, c, dk), lambda h, i: (h, ix(i), 0))
    state = pl.BlockSpec((hb, None, dk, dk), lambda h, i: (h, ix(i), 0, 0))
    return smem, qkv, row, tok, state


def _gdn_fwd(qkv_h, araw, braw, alog, dtb, name, side=_NoSide):
    _, hh, t, dk = qkv_h.shape
    n, hb = t // GDN_CHUNK, GDN_HEADS_PER_STEP
    smem, qkv, row, tok, state = _gdn_specs(n, False)
    grid = (hh // hb, n)

    def body(*refs):
        main, side_start, side_finish = _side_hooks(side, refs, 7, 3, 1, grid)
        alog_ref, dt_ref, q_ref, k_ref, v_ref, a_ref, b_ref, o_ref, so_ref, to_ref, s_ref = main
        side_start()

        @pl.when(pl.program_id(1) == 0)
        def _():
            s_ref[...] = jnp.zeros_like(s_ref)

        p = _gdn_pre(q_ref[...], k_ref[...], v_ref[...], a_ref[...], b_ref[...],
                     _head_scalars(alog_ref, hb), _head_scalars(dt_ref, hb))
        s = s_ref[...]
        vn = p.u - _hmm(p.w, s)
        o_ref[...] = _hmm(p.qd, s) + _hmm(p.qk, vn)
        so_ref[...] = s
        to_ref[...] = p.t
        s_ref[...] = s * p.el + _hmm(p.kd, vn, _TN)
        side_finish()

    mats = jax.ShapeDtypeStruct((hh, n, dk, dk), F32)
    return _carrier_call(
        body, name, grid, [smem, smem] + qkv + [row, row], [tok, state, state],
        [jax.ShapeDtypeStruct((hh, t, dk), F32), mats, mats],
        [pltpu.VMEM((hb, dk, dk), F32)], side, (alog, dtb, qkv_h, qkv_h, qkv_h, araw, braw))


def _gdn_bwd(qkv_h, araw, braw, alog, dtb, states, t_mats, do, name, side=_NoSide):
    _, hh, t, dk = qkv_h.shape
    c, hb = GDN_CHUNK, GDN_HEADS_PER_STEP
    n = t // c
    smem, qkv, row, tok, state = _gdn_specs(n, True)
    acc = pl.BlockSpec((hb, 1, LANES), lambda h, i: (h, 0, 0))
    grid = (hh // hb, n)

    def body(*refs):
        main, side_start, side_finish = _side_hooks(side, refs, 10, 7, 1, grid)
        (alog_ref, dt_ref, q_ref, k_ref, v_ref, a_ref, b_ref, s_ref, t_ref, do_ref,
         dq_ref, dk_ref, dv_ref, da_ref, db_ref, dal_ref, ddt_ref, ds_ref) = main
        side_start()

        @pl.when(pl.program_id(1) == 0)
        def _():
            ds_ref[...] = jnp.zeros_like(ds_ref)
            dal_ref[...] = jnp.zeros_like(dal_ref)
            ddt_ref[...] = jnp.zeros_like(ddt_ref)

        q, k, v = q_ref[...], k_ref[...], v_ref[...]
        p = _gdn_pre(q, k, v, a_ref[...], b_ref[...], _head_scalars(alog_ref, hb), _head_scalars(dt_ref, hb),
                     t_ref[...])
        s, do, dsp = s_ref[...], do_ref[...], ds_ref[...]
        vn = p.u - _hmm(p.w, s)
        dqd = _hmm(do, s, _NT)
        dqk = _hmm(do, vn, _NT)
        dvn = _hmm(p.qk, do, _TN) + _hmm(p.kd, dsp)
        dkd = _hmm(vn, dsp, _NT)
        d_el = _colsum(_rowsum(s * dsp))
        ds_ref[...] = dsp * p.el + _hmm(p.qd, do, _TN) - _hmm(p.w, dvn, _TN)
        dw = -_hmm(dvn, s, _NT)
        d_t = _hmm(dvn, p.vb, _NT) + _hmm(dw, p.kbe, _NT)
        dvb, dkbe = _hmm(p.t, dvn, _TN), _hmm(p.t, dw, _TN)
        d_a = jnp.where(p.strict, -_hmm(p.t, _hmm(d_t, p.t, _NT), _TN), 0.0)
        dkk = d_a * p.dm
        dqk0 = dqk * p.dm
        ddm = d_a * p.kk + dqk * p.qk0
        dkb = _hmm(dkk, k) + dkbe * p.e
        dq_ref[...] = _hmm(dqk0, k) + dqd * p.e
        dk_ref[...] = _hmm(dkk, p.kb, _TN) + _hmm(dqk0, q, _TN) + dkd * p.ek + dkb * p.beta
        dv_ref[...] = dvb * p.beta
        dbeta = _rowsum(dkb * k) + _rowsum(dvb * v)
        d_e = _rowsum(dqd * q) + _rowsum(dkbe * p.kb)
        d_ek = _rowsum(dkd * k)
        m = ddm * p.dm
        dgam = d_e * p.e - d_ek * p.ek + _rowsum(m) - p.to_col(_colsum(m))
        dgam_last = _colsum(d_ek * p.ek) + d_el * p.el
        dg_row = _colsum(jnp.where(p.tril, dgam, 0.0)) + dgam_last
        da_row = dg_row * p.a_neg * _sigmoid(p.xg)
        da_ref[...] = da_row
        db_ref[...] = p.to_row(dbeta) * p.beta_row * (1.0 - p.beta_row)
        dal_ref[...] += _rowsum(dg_row * p.g_row)
        ddt_ref[...] += _rowsum(da_row)
        side_finish()

    tok_shape = jax.ShapeDtypeStruct((hh, t, dk), F32)
    row_shape = jax.ShapeDtypeStruct((hh, n, 1, c), F32)
    acc_shape = jax.ShapeDtypeStruct((hh, 1, LANES), F32)
    return _carrier_call(
        body, name, grid, [smem, smem] + qkv + [row, row, state, state, tok], [tok, tok, tok, row, row, acc, acc],
        [tok_shape, tok_shape, tok_shape, row_shape, row_shape, acc_shape, acc_shape],
        [pltpu.VMEM((hb, dk, dk), F32)], side, (alog, dtb, qkv_h, qkv_h, qkv_h, araw, braw, states, t_mats, do))


def _gdn_post_fwd(o, proj, zcol, g_o, name):
    hh, t, dv = o.shape
    tt = _tile(t, _ROW_TILES)
    zblk = zcol // (hh * dv)

    def body(o_ref, z_ref, g_ref, y_ref):
        for h in range(hh):
            sl = slice(h * dv, (h + 1) * dv)
            ov = o_ref[h]
            r = lax.rsqrt(jnp.mean(ov * ov, axis=-1, keepdims=True) + EPS)
            y_ref[:, sl] = (ov * r * g_ref[...] * _silu(z_ref[:, sl].astype(F32))).astype(y_ref.dtype)

    return pl.pallas_call(
        body, name=name, grid=(t // tt,),
        in_specs=[pl.BlockSpec((hh, tt, dv), lambda i: (0, i, 0)), pl.BlockSpec((tt, hh * dv), lambda i: (i, zblk)),
                  pl.BlockSpec((1, dv), lambda i: (0, 0))],
        out_specs=pl.BlockSpec((tt, hh * dv), lambda i: (i, 0)),
        out_shape=jax.ShapeDtypeStruct((t, hh * dv), MXU_DTYPE), compiler_params=_params(("parallel",)))(o, proj, g_o)


def _gdn_post_bwd(o, proj, zcol, dy, g_o, dproj, name):
    hh, t, dv = o.shape
    tt = _tile(t, _ROW_TILES)
    zblk = zcol // (hh * dv)

    def body(o_ref, z_ref, dy_ref, g_ref, _, do_ref, dz_ref, dg_ref):
        @pl.when(pl.program_id(0) == 0)
        def _():
            dg_ref[...] = jnp.zeros_like(dg_ref)

        gv = g_ref[...]
        for h in range(hh):
            sl = slice(h * dv, (h + 1) * dv)
            ov, zz, dy = o_ref[h], z_ref[:, sl].astype(F32), dy_ref[:, sl]
            r = lax.rsqrt(jnp.mean(ov * ov, axis=-1, keepdims=True) + EPS)
            oh = ov * r
            sg = _sigmoid(zz)
            dz_ref[:, sl] = (dy * oh * gv * (sg * (1.0 + zz * (1.0 - sg)))).astype(dz_ref.dtype)
            don = dy * (zz * sg)
            dg_ref[...] += _colsum(don * oh)
            doh = don * gv
            do_ref[h] = r * (doh - oh * jnp.mean(doh * oh, axis=-1, keepdims=True))

    return pl.pallas_call(
        body, name=name, grid=(t // tt,),
        in_specs=[pl.BlockSpec((hh, tt, dv), lambda i: (0, i, 0)), pl.BlockSpec((tt, hh * dv), lambda i: (i, zblk)),
                  pl.BlockSpec((tt, hh * dv), lambda i: (i, 0)), pl.BlockSpec((1, dv), lambda i: (0, 0)), _ANY],
        out_specs=[pl.BlockSpec((hh, tt, dv), lambda i: (0, i, 0)), pl.BlockSpec((tt, hh * dv), lambda i: (i, zblk)),
                   pl.BlockSpec((1, dv), lambda i: (0, 0))],
        out_shape=[jax.ShapeDtypeStruct((hh, t, dv), F32), jax.ShapeDtypeStruct(dproj.shape, dproj.dtype),
                   jax.ShapeDtypeStruct((1, dv), F32)],
        input_output_aliases={4: 1}, compiler_params=_params(("arbitrary",)))(o, proj, dy, g_o, dproj)


def _cols_as_rows(x, col, name):
    t = x.shape[0]
    tt = _tile(t, _ROW_TILES)

    def body(x_ref, o_ref):
        o_ref[...] = x_ref[...].T

    return pl.pallas_call(
        body, name=name, grid=(t // tt,), in_specs=[pl.BlockSpec((tt, LANES), lambda i: (i, col // LANES))],
        out_specs=pl.BlockSpec((LANES, tt), lambda i: (0, i)), out_shape=jax.ShapeDtypeStruct((LANES, t), x.dtype),
        compiler_params=_params(("parallel",)))(x)


def _rows_into_cols(dst, rows, col, name):
    t = dst.shape[0]
    tt = _tile(t, _ROW_TILES)

    def body(r_ref, _, o_ref):
        o_ref[...] = r_ref[...].T.astype(o_ref.dtype)

    return pl.pallas_call(
        body, name=name, grid=(t // tt,), in_specs=[pl.BlockSpec((LANES, tt), lambda i: (0, i)), _ANY],
        out_specs=pl.BlockSpec((tt, LANES), lambda i: (i, col // LANES)),
        out_shape=jax.ShapeDtypeStruct(dst.shape, dst.dtype), input_output_aliases={1: 0},
        compiler_params=_params(("parallel",)))(rows, dst)


def _adamw(g, w, m, v):
    m = ADAM_B1 * m + (1.0 - ADAM_B1) * g
    v = ADAM_B2 * v + (1.0 - ADAM_B2) * (g * g)
    m_hat = m / (1.0 - ADAM_B1 ** ADAM_STEP)
    v_hat = v / (1.0 - ADAM_B2 ** ADAM_STEP)
    return -ADAM_LR * (m_hat / (jnp.sqrt(v_hat) + ADAM_EPS) + ADAM_WD * w), m, v


def _ada_fwd(c_all, ada_w, name):
    nl, d, cols = ada_w.shape
    b = c_all.shape[0]

    def body(c_ref, w_ref, o_ref):
        o_ref[...] = _mm_hi(_silu(c_ref[...]), w_ref[...])

    return pl.pallas_call(
        body, name=name, grid=(nl,),
        in_specs=[pl.BlockSpec((b, d), lambda i: (0, 0)), pl.BlockSpec((None, d, cols), lambda i: (i, 0, 0))],
        out_specs=pl.BlockSpec((None, b, cols), lambda i: (i, 0, 0)),
        out_shape=jax.ShapeDtypeStruct((nl, b, cols), F32), compiler_params=_params(("parallel",)))(c_all, ada_w)


def _ada_bwd(c_col, dm, w, m, v, name):
    nl, d, cols = w.shape
    b = c_col.shape[0]
    tr = _tile(d, (256, 128))

    def body(c_ref, dm_ref, w_ref, m_ref, v_ref, g_ref, dl_ref, mo_ref, vo_ref):
        g = _silu(c_ref[0]) * dm_ref[pl.ds(0, 1), :]
        for j in range(1, b):
            g = g + _silu(c_ref[j]) * dm_ref[pl.ds(j, 1), :]
        g_ref[...] = g
        dl_ref[...], mo_ref[...], vo_ref[...] = _adamw(g, w_ref[...], m_ref[...], v_ref[...])

    blk = pl.BlockSpec((None, tr, cols), lambda l, i: (l, i, 0))
    shape = jax.ShapeDtypeStruct((nl, d, cols), F32)
    return pl.pallas_call(
        body, name=name, grid=(nl, d // tr),
        in_specs=[pl.BlockSpec((b, tr, 1), lambda l, i: (0, i, 0)), pl.BlockSpec((None, b, cols), lambda l, i: (l, 0, 0)),
                  blk, blk, blk],
        out_specs=[blk, blk, blk, blk], out_shape=[shape] * 4,
        compiler_params=_params(("parallel", "parallel")))(c_col, dm, w, m, v)


_GRAD_ROW_TILES = (256, 128, 176, 88)


def _sum_adam(parts, w, m, v, name):
    nl, npart, r, cdim = parts.shape
    tr = _tile(r, _GRAD_ROW_TILES)

    def body(p_ref, w_ref, m_ref, v_ref, g_ref, dl_ref, mo_ref, vo_ref):
        g = p_ref[0].astype(F32)
        for j in range(1, npart):
            g = g + p_ref[j].astype(F32)
        g_ref[...] = g
        dl_ref[...], mo_ref[...], vo_ref[...] = _adamw(g, w_ref[...], m_ref[...], v_ref[...])

    blk = pl.BlockSpec((None, tr, cdim), lambda l, i: (l, i, 0))
    shape = jax.ShapeDtypeStruct((nl, r, cdim), F32)
    return pl.pallas_call(
        body, name=name, grid=(nl, r // tr),
        in_specs=[pl.BlockSpec((None, npart, tr, cdim), lambda l, i: (l, 0, i, 0)), blk, blk, blk],
        out_specs=[blk, blk, blk, blk], out_shape=[shape] * 4,
        compiler_params=_params(("parallel", "parallel")))(parts, w, m, v)


def _cols_from_blocks(g, plan, width, name):
    _, r, cdim = g.shape
    tr = _tile(r, (256, 128))
    covered = sorted((dst, dst + n) for _, _, n, dst in plan)
    holes, pos = [], 0
    for a, b in covered:
        if a > pos:
            holes.append((pos, a))
        pos = max(pos, b)
    if pos < width:
        holes.append((pos, width))

    def body(g_ref, o_ref):
        for a, b in holes:
            o_ref[:, a:b] = jnp.zeros((tr, b - a), o_ref.dtype)
        for j, src, n, dst in plan:
            o_ref[:, dst:dst + n] = g_ref[j, :, src:src + n]

    return pl.pallas_call(
        body, name=name, grid=(r // tr,), in_specs=[pl.BlockSpec((N_DEV, tr, cdim), lambda i: (0, i, 0))],
        out_specs=pl.BlockSpec((tr, width), lambda i: (i, 0)), out_shape=jax.ShapeDtypeStruct((r, width), g.dtype),
        compiler_params=_params(("parallel",)))(g)


def _blocks_from_cols(w, plan, cdim, name):
    r, width = w.shape
    tr = _tile(r, (256, 128))

    def body(w_ref, o_ref):
        for j, src, n, dst in plan:
            o_ref[j, :, src:src + n] = w_ref[:, dst:dst + n]

    return pl.pallas_call(
        body, name=name, grid=(r // tr,), in_specs=[pl.BlockSpec((tr, width), lambda i: (i, 0))],
        out_specs=pl.BlockSpec((N_DEV, tr, cdim), lambda i: (0, i, 0)),
        out_shape=jax.ShapeDtypeStruct((N_DEV, r, cdim), w.dtype), compiler_params=_params(("parallel",)))(w)


def _pair_sum(x, tmp, core, name):
    _, r, cdim = x.shape
    tr = _tile(r, _GRAD_ROW_TILES)

    def body(core_ref, x_ref, t_ref, o_ref):
        o_ref[...] = (x_ref[...] + t_ref[...]).astype(o_ref.dtype)

    grid_spec = pltpu.PrefetchScalarGridSpec(
        num_scalar_prefetch=1, grid=(N_DEV // 2, r // tr),
        in_specs=[pl.BlockSpec((None, tr, cdim), lambda ch, i, core_ref: (2 * ch + core_ref[0], i, 0)),
                  pl.BlockSpec((None, tr, cdim), lambda ch, i, core_ref: (ch, i, 0))],
        out_specs=pl.BlockSpec((None, tr, cdim), lambda ch, i, core_ref: (ch, i, 0)))
    return pl.pallas_call(
        body, name=name, grid_spec=grid_spec, out_shape=jax.ShapeDtypeStruct((N_DEV // 2, r, cdim), WIRE_DTYPE),
        compiler_params=_params(("parallel", "parallel")))(core, x, tmp)


_ANY = pl.BlockSpec(memory_space=pl.ANY)
_CHIP_FLIPS = ((1, 0), (0, 1), (1, 1))


def _coords():
    return lax.axis_index("x"), lax.axis_index("y"), lax.axis_index("c")


def _flip(v, f):
    return 1 - v if f else v


def _a2a_direct(xs, name):
    n, ncp = len(xs), N_DEV - 1

    def body(*refs):
        ins, outs = refs[:n], refs[n:2 * n]
        send, recv, loc = refs[2 * n:]
        x, y, c = _coords()
        me = 4 * x + 2 * y + c
        local = [pltpu.make_async_copy(ins[i].at[me], outs[i].at[me], loc.at[i]) for i in range(n)]
        for cp in local:
            cp.start()
        remote = []
        for i in range(n):
            for k in range(1, N_DEV):
                px, py, pc = _flip(x, k & 4), _flip(y, k & 2), _flip(c, k & 1)
                cp = pltpu.make_async_remote_copy(
                    src_ref=ins[i].at[4 * px + 2 * py + pc], dst_ref=outs[i].at[me],
                    send_sem=send.at[i * ncp + k - 1], recv_sem=recv.at[i * ncp + k - 1],
                    device_id=(px, py, pc), device_id_type=MESH)
                cp.start()
                remote.append(cp)
        for cp in remote:
            cp.wait()
        for cp in local:
            cp.wait()

    return pl.pallas_call(
        body, name=name, in_specs=[_ANY] * n, out_specs=[_ANY] * n,
        out_shape=[jax.ShapeDtypeStruct(a.shape, a.dtype) for a in xs],
        scratch_shapes=[pltpu.SemaphoreType.DMA((n * ncp,)), pltpu.SemaphoreType.DMA((n * ncp,)),
                        pltpu.SemaphoreType.DMA((n,))])(*xs)


class _AllGatherSide:
    def __init__(self, blocks):
        self.operands = list(blocks)
        n = self.n = len(self.operands)
        self.n_in = self.n_out = n
        self.out_shape = [jax.ShapeDtypeStruct((N_DEV,) + a.shape, a.dtype) for a in self.operands]
        self.aliases = {}
        nici, nd2d = len(_CHIP_FLIPS), N_DEV // 2
        self.scratch = [pltpu.SemaphoreType.DMA((n * nici,)), pltpu.SemaphoreType.DMA((n * nici,)),
                        pltpu.SemaphoreType.DMA((n * nd2d,)), pltpu.SemaphoreType.DMA((n * nd2d,)),
                        pltpu.SemaphoreType.DMA((n,))]

    def _first(self, ins, outs, sems):
        send, recv, _, _, loc = sems
        x, y, c = _coords()
        me = 4 * x + 2 * y + c
        nici = len(_CHIP_FLIPS)
        local = [pltpu.make_async_copy(ins[i], outs[i].at[me], loc.at[i]) for i in range(self.n)]
        remote = [pltpu.make_async_remote_copy(
            src_ref=ins[i], dst_ref=outs[i].at[me], send_sem=send.at[i * nici + j], recv_sem=recv.at[i * nici + j],
            device_id=(_flip(x, fx), _flip(y, fy), c), device_id_type=MESH)
            for i in range(self.n) for j, (fx, fy) in enumerate(_CHIP_FLIPS)]
        return local + remote

    def _second(self, outs, sems):
        _, _, send, recv, _ = sems
        x, y, c = _coords()
        nd2d = N_DEV // 2
        return [pltpu.make_async_remote_copy(
            src_ref=outs[i].at[2 * ch + c], dst_ref=outs[i].at[2 * ch + c], send_sem=send.at[i * nd2d + ch],
            recv_sem=recv.at[i * nd2d + ch], device_id=(x, y, 1 - c), device_id_type=MESH)
            for i in range(self.n) for ch in range(nd2d)]

    def start(self, ins, outs, sems):
        for cp in self._first(ins, outs, sems):
            cp.start()

    def finish(self, ins, outs, sems):
        for cp in self._first(ins, outs, sems):
            cp.wait()
        second = self._second(outs, sems)
        for cp in second:
            cp.start()
        for cp in second:
            cp.wait()


class _ReduceScatterIciSide:
    def __init__(self, sums, accs, layer):
        self.operands = list(sums) + list(accs)
        n = self.n = len(sums)
        self.layer = layer
        self.n_in, self.n_out = 2 * n, n
        self.out_shape = [jax.ShapeDtypeStruct(a.shape, a.dtype) for a in accs]
        self.aliases = {n + i: i for i in range(n)}
        nici = len(_CHIP_FLIPS)
        self.scratch = [pltpu.SemaphoreType.DMA((n * nici,)), pltpu.SemaphoreType.DMA((n * nici,)),
                        pltpu.SemaphoreType.DMA((n,))]

    def _copies(self, ins, outs, sems):
        send, recv, loc = sems
        x, y, c = _coords()
        chip = 2 * x + y
        nici = len(_CHIP_FLIPS)
        local = [pltpu.make_async_copy(ins[i].at[chip], outs[i].at[self.layer, chip], loc.at[i])
                 for i in range(self.n)]
        remote = [pltpu.make_async_remote_copy(
            src_ref=ins[i].at[2 * _flip(x, fx) + _flip(y, fy)], dst_ref=outs[i].at[self.layer, chip],
            send_sem=send.at[i * nici + j], recv_sem=recv.at[i * nici + j],
            device_id=(_flip(x, fx), _flip(y, fy), c), device_id_type=MESH)
            for i in range(self.n) for j, (fx, fy) in enumerate(_CHIP_FLIPS)]
        return local + remote

    def start(self, ins, outs, sems):
        for cp in self._copies(ins, outs, sems):
            cp.start()

    def finish(self, ins, outs, sems):
        for cp in self._copies(ins, outs, sems):
            cp.wait()


def _run_side(side, name):
    def body(*refs):
        ins, outs = refs[:side.n_in], refs[side.n_in:side.n_in + side.n_out]
        sems = refs[side.n_in + side.n_out:]
        side.start(ins, outs, sems)
        side.finish(ins, outs, sems)

    return pl.pallas_call(
        body, name=name, in_specs=[_ANY] * side.n_in, out_specs=[_ANY] * side.n_out, out_shape=side.out_shape,
        input_output_aliases=side.aliases, scratch_shapes=side.scratch)(*side.operands)


class _ReduceScatterD2dSide:
    def __init__(self, parts):
        self.operands = list(parts)
        n = self.n = len(self.operands)
        self.n_in = self.n_out = n
        nd2d = N_DEV // 2
        self.out_shape = [jax.ShapeDtypeStruct((nd2d,) + a.shape[1:], a.dtype) for a in self.operands]
        self.aliases = {}
        self.scratch = [pltpu.SemaphoreType.DMA((n * nd2d,)), pltpu.SemaphoreType.DMA((n * nd2d,))]

    def _copies(self, ins, outs, sems):
        send, recv = sems
        x, y, c = _coords()
        nd2d = N_DEV // 2
        return [pltpu.make_async_remote_copy(
            src_ref=ins[i].at[2 * ch + 1 - c], dst_ref=outs[i].at[ch], send_sem=send.at[i * nd2d + ch],
            recv_sem=recv.at[i * nd2d + ch], device_id=(x, y, 1 - c), device_id_type=MESH)
            for i in range(self.n) for ch in range(nd2d)]

    def start(self, ins, outs, sems):
        for cp in self._copies(ins, outs, sems):
            cp.start()

    def finish(self, ins, outs, sems):
        for cp in self._copies(ins, outs, sems):
            cp.wait()


_PACK_ROWS = 256


def _pack(arrs):
    flat = jnp.concatenate([a.reshape(-1) for a in arrs])
    quantum = _PACK_ROWS * LANES
    total = -(-flat.shape[0] // quantum) * quantum
    return jnp.pad(flat, (0, total - flat.shape[0])).reshape(-1, LANES)


def _unpack(packed, like):
    flat, out, pos = packed.reshape(-1), [], 0
    for a in like:
        out.append(flat[pos:pos + a.size].reshape(a.shape))
        pos += a.size
    return out


def kernel(x, c, ada_w, ada_b, norm1_g, w_in, conv_w, spatial_w, spatial_b, v_norm_g, a_log, dt_bias, o_norm_g, w_branch_a, w_branch_b, w_out, norm2_g, w_ffn_in, w_ffn_out, final_g, loss_target, m_ada_w, m_ada_b, m_norm1_g, m_w_in, m_conv_w, m_spatial_w, m_spatial_b, m_v_norm_g, m_a_log, m_dt_bias, m_o_norm_g, m_w_branch_a, m_w_branch_b, m_w_out, m_norm2_g, m_w_ffn_in, m_w_ffn_out, m_final_g, v_ada_w, v_ada_b, v_norm1_g, v_w_in, v_conv_w, v_spatial_w, v_spatial_b, v_v_norm_g, v_a_log, v_dt_bias, v_o_norm_g, v_w_branch_a, v_w_branch_b, v_w_out, v_norm2_g, v_w_ffn_in, v_w_ffn_out, v_final_g):
    nl, d = ada_w.shape[0], x.shape[2]
    t = x.shape[1]
    nchunk = t // GDN_CHUNK
    xi, yi, ci = _coords()
    me = 4 * xi + 2 * yi + ci
    core = jnp.reshape(ci, (1,)).astype(jnp.int32)
    x0, target = x[0], loss_target[0]
    wcols = 3 * HEADS * HEAD_DIM
    lay = _ProjLayout(d)
    in_pieces = lay.pieces(w_in.shape[2])
    fi_shard = w_ffn_in.shape[2]
    fi_pieces = [(j, 0, fi_shard, fi_shard * j) for j in range(N_DEV)]

    c_all, cw_all = _a2a_direct([jnp.broadcast_to(c[None], (N_DEV,) + c.shape),
                                 jnp.broadcast_to(conv_w[None], (N_DEV,) + conv_w.shape)], "gather_small")
    c_all = c_all[:, 0]
    conv_full = cw_all.transpose(1, 2, 0, 3).reshape(nl, CONV_K, wcols)
    modp = _ada_fwd(c_all, ada_w, "ada_fwd")
    (modx,) = _a2a_direct([modp.transpose(1, 0, 2)], "mod_exchange")
    mod = (modx.transpose(1, 0, 2).reshape(nl, 6 * d) + ada_b).reshape(nl, 6, 1, d)

    big = (w_in, w_branch_a, w_branch_b, w_out, w_ffn_in, w_ffn_out)
    big_wire = [w.astype(WIRE_DTYPE) for w in big]
    gather_in = lambda i: _AllGatherSide([big_wire[0][i]])
    gather_early = lambda i: _AllGatherSide([big_wire[k][i] for k in (1, 2, 3, 5)])
    gather_late = lambda i: _AllGatherSide([big_wire[4][i]] + ([big_wire[0][i + 1]] if i + 1 < nl else []))
    row_full = lambda g: g.reshape(-1, g.shape[2])
    padded_in = lambda g: _cols_from_blocks(g, in_pieces, lay.width, "w_in_cols")
    w_pads = [padded_in(_run_side(gather_in(0), "ag_first")[0])] + [None] * (nl - 1)
    weights = [None] * nl

    def rows_of(ba_rows, lo):
        return ba_rows[lo:lo + HEADS].reshape(HEADS, nchunk, 1, GDN_CHUNK)

    saved = []
    x_cur, delta, gt_prev = x0, None, None
    for i in range(nl):
        sh1, sc1, gt1, sh2, sc2, gt2 = (mod[i, k] for k in range(6))
        s = dict(gt1=gt1, gt2=gt2, sc1=sc1, sc2=sc2)
        s["x_in"], s["h"] = _resid_norm(x_cur, delta, gt_prev, norm1_g[i][None], sc1, sh1, "norm1_fwd")
        s["proj"], g_a, g_b, g_o, g_fo = _matmul(s["h"], w_pads[i], "nn", "proj_fwd", out_dtype=ACT_DTYPE,
                                                 side=gather_early(i))
        ba = _matmul(s["h"], w_pads[i][:, lay.ba:], "nn", "proj_ba_fwd")
        s["b_col"] = spatial_b[i][:, :, None]
        s["ya"] = _mixer_a_fwd(s["proj"], lay.uv, spatial_w[i], s["b_col"], v_norm_g[i][None], "mixer_a_fwd")
        s["qkv_h"] = _conv_fwd(s["proj"], conv_full[i], "conv_fwd")
        ba_rows = _cols_as_rows(ba, 0, "ba_rows")
        s["braw"], s["araw"] = rows_of(ba_rows, 0), rows_of(ba_rows, HEADS)
        s["o"], s["states"], s["t_mats"], g_fi, *g_in = _gdn_fwd(
            s["qkv_h"], s["araw"], s["braw"], a_log[i], dt_bias[i], "gdn_fwd", gather_late(i))
        if g_in:
            w_pads[i + 1] = padded_in(g_in[0])
        weights[i] = (row_full(g_a), row_full(g_b), row_full(g_o),
                      _cols_from_blocks(g_fi, fi_pieces, N_DEV * fi_shard, "w_ffn_in_cols"), row_full(g_fo))
        w_a, w_b, w_o, w_fi, w_fo = weights[i]
        s["yb"] = _gdn_post_fwd(s["o"], s["proj"], lay.z, o_norm_g[i][None], "gdn_post_fwd")
        s["pa"] = _matmul(s["ya"], w_a, "nn", "branch_a_fwd")
        s["pb"] = _matmul(s["yb"], w_b, "nn", "branch_b_fwd")
        s["merged"] = _merge_fwd(s["pa"], s["pb"], s["proj"], lay.gates, "merge_fwd")
        s["mo"] = _matmul(s["merged"], w_o, "nn", "out_fwd")
        s["x1"], s["h2"] = _resid_norm(s["x_in"], s["mo"], gt1, norm2_g[i][None], sc2, sh2, "norm2_fwd")
        s["gu"] = _matmul(s["h2"], w_fi, "nn", "ffn_in_fwd", out_dtype=ACT_DTYPE)
        s["a"] = _swiglu_fwd(s["gu"], "swiglu_fwd")
        s["fo"] = _matmul(s["a"], w_fo, "nn", "ffn_out_fwd")
        saved.append(s)
        x_cur, delta, gt_prev = s["x1"], s["fo"], gt2
    dx, d_final_g, loss_tile = _final_loss(x_cur, delta, gt_prev, final_g[None], target, "final_loss")
    loss = lax.psum(loss_tile[0, 0], ("x", "y", "c"))

    big_shapes = [(d, w_in.shape[2]), w_branch_a.shape[1:], w_branch_b.shape[1:], w_out.shape[1:],
                  (w_ffn_in.shape[2], d), w_ffn_out.shape[1:]]
    accs = [lax.empty((nl, N_DEV // 2) + tuple(sh), WIRE_DTYPE) for sh in big_shapes]
    row_blocks = lambda g: g.reshape(N_DEV, -1, g.shape[1])
    dmod, small = [None] * nl, [None] * nl
    d_conv = [None] * nl
    parts, sums = None, None
    beside_gdn, beside_dw, beside_dx = (0,), (4,), (1, 2, 3, 5)
    rep_parts = [None] * nl
    rep_pack = lambda i: _pack((dmod[i],) + small[i])

    def scatter_side(idx, layer):
        if sums is None:
            return _NoSide
        return _ReduceScatterIciSide([sums[k] for k in idx], [accs[k] for k in idx], layer)

    def scattered_into(accs, idx, new):
        accs = list(accs)
        for k, a in zip(idx, new):
            accs[k] = a
        return accs

    for i in reversed(range(nl)):
        s = saved[i]
        w_a, w_b, w_o, w_fi, w_fo = weights[i]
        if i == nl - 1:
            dfo, dgt2 = _gate_bwd(dx, s["fo"], s["gt2"], "gate2_bwd")
        else:
            dgt2 = dgt2_before
        g_fo = _matmul(s["a"], dfo, "tn", "ffn_out_dw")
        da = _matmul(dfo, w_fo, "nt", "ffn_out_dx")
        dgu = _swiglu_bwd(s["gu"], da, "swiglu_bwd")
        if parts is None:
            g_fi = _matmul(dgu, s["h2"], "tn", "ffn_in_dw")
        else:
            g_fi, *other = _matmul(dgu, s["h2"], "tn", "ffn_in_dw", side=_ReduceScatterD2dSide(parts))
            sums = [_pair_sum(p, o, core, "rs_pair_sum_%d" % k) for k, (p, o) in enumerate(zip(parts, other))]
        if i + 1 < nl:
            dh2, rep_parts[i + 1] = _matmul(dgu, w_fi, "nt", "ffn_in_dx", side=_AllGatherSide([rep_pack(i + 1)]))
        else:
            dh2 = _matmul(dgu, w_fi, "nt", "ffn_in_dx")
        dx1, dsh2, dsc2, dg2, dmo, dgt1 = _norm_bwd(s["x1"], dh2, dx, norm2_g[i][None], s["sc2"], "norm2_bwd",
                                                    gate=(s["mo"], s["gt1"]))
        g_o = _matmul(s["merged"], dmo, "tn", "out_dw")
        dmerged = _matmul(dmo, w_o, "nt", "out_dx")
        dproj = lax.empty((t, lay.width), MXU_DTYPE)
        dpa, dpb, dproj = _merge_bwd(dmerged, s["pa"], s["pb"], s["proj"], lay.gates, dproj, "merge_bwd")
        g_a = _matmul(s["ya"], dpa, "tn", "branch_a_dw")
        dya = _matmul(dpa, w_a, "nt", "branch_a_dx")
        g_b = _matmul(s["yb"], dpb, "tn", "branch_b_dw")
        dyb = _matmul(dpb, w_b, "nt", "branch_b_dx")
        dproj, d_ws, d_bs, d_gv = _mixer_a_bwd(s["proj"], lay.uv, dya, spatial_w[i], jnp.swapaxes(spatial_w[i], 1, 2),
                                               s["b_col"], v_norm_g[i][None], dproj, "mixer_a_bwd")
        do, dproj, d_go = _gdn_post_bwd(s["o"], s["proj"], lay.z, dyb, o_norm_g[i][None], dproj, "gdn_post_bwd")
        dq, dk, dv, d_ar, d_br, d_al, d_dt, *scattered = _gdn_bwd(
            s["qkv_h"], s["araw"], s["braw"], a_log[i], dt_bias[i], s["states"], s["t_mats"], do, "gdn_bwd",
            scatter_side(beside_gdn, i + 1))
        accs = scattered_into(accs, beside_gdn, scattered)
        dacc, d_conv[i] = _conv_bwd_pre(s["proj"], dq, dk, dv, conv_full[i], "conv_bwd_pre")
        dproj = _conv_bwd_in(dacc, conv_full[i], dproj, "conv_bwd_in")
        dba_rows = jnp.pad(jnp.concatenate([d_br.reshape(HEADS, t), d_ar.reshape(HEADS, t)]),
                           ((0, LANES - 2 * HEADS), (0, 0)))
        dproj = _rows_into_cols(dproj, dba_rows, lay.ba, "dproj_ba")
        if sums is None:
            g_pad = _matmul(s["h"], dproj, "tn", "proj_dw")
            dh = _matmul(dproj, w_pads[i], "nt", "proj_dx")
        else:
            g_pad, *scattered = _matmul(s["h"], dproj, "tn", "proj_dw", side=scatter_side(beside_dw, i + 1))
            accs = scattered_into(accs, beside_dw, scattered)
            dh, *scattered = _matmul(dproj, w_pads[i], "nt", "proj_dx", side=scatter_side(beside_dx, i + 1))
            accs = scattered_into(accs, beside_dx, scattered)
        if i > 0:
            dx, dsh1, dsc1, dg1, dfo, dgt2_before = _norm_bwd(s["x_in"], dh, dx1, norm1_g[i][None], s["sc1"], "norm1_bwd",
                                                              gate=(saved[i - 1]["fo"], saved[i - 1]["gt2"]))
        else:
            dx, dsh1, dsc1, dg1 = _norm_bwd(s["x_in"], dh, dx1, norm1_g[i][None], s["sc1"], "norm1_bwd")
        dmod[i] = jnp.concatenate([dsh1, dsc1, dgt1, dsh2, dsc2, dgt2], axis=1)[0]
        small[i] = (dg1[0], d_ws, d_bs[:, :, 0], d_gv[0], d_al[:, 0, 0], d_dt[:, 0, 0], d_go[0], dg2[0])
        parts = [_blocks_from_cols(g_pad, in_pieces, w_in.shape[2], "w_in_blocks"), row_blocks(g_a), row_blocks(g_b),
                 row_blocks(g_o), row_blocks(g_fi), row_blocks(g_fo)]
    other = _run_side(_ReduceScatterD2dSide(parts), "rs_d2d_last")
    sums = [_pair_sum(p, o, core, "rs_pair_sum_%d" % k) for k, (p, o) in enumerate(zip(parts, other))]
    accs = _run_side(_ReduceScatterIciSide(sums, accs, 0), "rs_ici_last")

    rep_w = (ada_b, norm1_g, spatial_w, spatial_b, v_norm_g, a_log, dt_bias, o_norm_g, norm2_g)
    rep_m = (m_ada_b, m_norm1_g, m_spatial_w, m_spatial_b, m_v_norm_g, m_a_log, m_dt_bias, m_o_norm_g, m_norm2_g)
    rep_v = (v_ada_b, v_norm1_g, v_spatial_w, v_spatial_b, v_v_norm_g, v_a_log, v_dt_bias, v_o_norm_g, v_norm2_g)
    rep_parts[0], fin_parts = _run_side(_AllGatherSide([rep_pack(0), _pack([d_final_g[0]])]), "small_grads_last")
    dmod = jnp.stack(dmod)
    d_conv_blocks = jnp.stack(d_conv).reshape(nl, CONV_K, N_DEV, -1).transpose(2, 0, 1, 3).reshape(N_DEV, -1, LANES)
    dmod_blocks = dmod.reshape(nl, N_DEV, -1).transpose(1, 0, 2)
    conv_all, dmod_all = _a2a_direct([d_conv_blocks, dmod_blocks], "small_grads_scatter")
    by_layer = lambda arrs: jnp.stack([_pack([a[i] for a in arrs]) for i in range(nl)])
    rep_out = _sum_adam(jnp.stack(rep_parts), by_layer(rep_w), by_layer(rep_m), by_layer(rep_v), "adam_small")
    fin_out = _sum_adam(fin_parts[None], _pack([final_g])[None], _pack([m_final_g])[None], _pack([v_final_g])[None],
                        "adam_final_g")
    layer_like = [a[0] for a in rep_w]
    rep_out = [[jnp.stack(per_layer) for per_layer in zip(*[_unpack(o[i], layer_like) for i in range(nl)])]
               + _unpack(f[0], [final_g]) for o, f in zip(rep_out, fin_out)]
    conv_out = _sum_adam(conv_all[None], conv_w.reshape(1, -1, LANES), m_conv_w.reshape(1, -1, LANES),
                         v_conv_w.reshape(1, -1, LANES), "adam_conv")
    conv_out = [o.reshape(conv_w.shape) for o in conv_out]
    ada_out = _ada_bwd(c_all[:, :, None], dmod_all.transpose(1, 0, 2), ada_w, m_ada_w, v_ada_w, "ada_bwd_adam")
    big_m = (m_w_in, m_w_branch_a, m_w_branch_b, m_w_out, m_w_ffn_in, m_w_ffn_out)
    big_v = (v_w_in, v_w_branch_a, v_w_branch_b, v_w_out, v_w_ffn_in, v_w_ffn_out)
    turn = lambda k, a: jnp.swapaxes(a, 1, 2) if k == 4 else a
    big_out = [[turn(k, o) for o in _sum_adam(accs[k], turn(k, big[k]), turn(k, big_m[k]), turn(k, big_v[k]),
                                             "adam_big_%d" % k)] for k in range(6)]

    def ordered(kind):
        rep = rep_out[kind]
        return (ada_out[kind], rep[0], rep[1], big_out[0][kind], conv_out[kind], rep[2], rep[3], rep[4], rep[5],
                rep[6], rep[7], big_out[1][kind], big_out[2][kind], big_out[3][kind], rep[8], big_out[4][kind],
                big_out[5][kind], rep[9])

    return (loss, dx[None]) + ordered(0) + ordered(1) + ordered(2) + ordered(3)
```

```python
import functools

import jax
import jax.numpy as jnp
from jax import lax
from jax.experimental import pallas as pl
from jax.experimental.pallas import tpu as pltpu

F32 = jnp.float32
BF16 = jnp.bfloat16
MXU_DTYPE = BF16
WIRE_DTYPE = BF16
ACT_DTYPE = BF16
EPS = 1e-6
LANES = 128
SUBLANES = 8
GDN_CHUNK = 128
A_CHUNK = 128
GROUPS = 8
HEADS = 8
HEAD_DIM = 128
CONV_K = 4
N_DEV = 8
VMEM_LIMIT = 48 * 1024 * 1024
MESH = pl.DeviceIdType.MESH

ADAM_LR = 0.001
ADAM_B1 = 0.9
ADAM_B2 = 0.999
ADAM_EPS = 1e-08
ADAM_WD = 0.01
ADAM_STEP = 10

_NN = (((1,), (0,)), ((), ()))
_NT = (((1,), (1,)), ((), ()))
_TN = (((0,), (0,)), ((), ()))


def _mm(a, b, dims=_NN):
    return lax.dot_general(a.astype(MXU_DTYPE), b.astype(MXU_DTYPE), dims, preferred_element_type=F32)


def _mm_hi(a, b):
    return lax.dot_general(a, b, _NN, precision=lax.Precision.HIGHEST, preferred_element_type=F32)


def _tile(n, cands):
    for c in cands:
        if n % c == 0:
            return c
    return n


def _params(sem=None):
    return pltpu.CompilerParams(dimension_semantics=sem, vmem_limit_bytes=VMEM_LIMIT)


def _sigmoid(x):
    return 1.0 / (1.0 + jnp.exp(-x))


def _silu(x):
    return x * _sigmoid(x)


_GELU_C = 0.7978845608028654
_GELU_A = 0.044715


def _gelu(x):
    return 0.5 * x * (1.0 + jnp.tanh(_GELU_C * (x + _GELU_A * x * x * x)))


def _gelu_and_slope(x):
    t = jnp.tanh(_GELU_C * (x + _GELU_A * x * x * x))
    return 0.5 * x * (1.0 + t), 0.5 * (1.0 + t) + 0.5 * x * (1.0 - t * t) * _GELU_C * (1.0 + 3.0 * _GELU_A * x * x)


def _softplus(x):
    return jnp.maximum(x, 0.0) + jnp.log(1.0 + jnp.exp(-jnp.abs(x)))


_MM_TILES = (1024, 1408, 1664, 512, 256, 128)


class _NoSide:
    operands, out_shape, scratch, aliases, n_in, n_out = [], [], [], {}, 0, 0


def _side_hooks(side, refs, n_main_in, n_main_out, n_main_scratch, grid):
    a = n_main_in + side.n_in
    b = a + n_main_out + side.n_out
    ins, outs, sems = refs[n_main_in:a], refs[a + n_main_out:b], refs[b + n_main_scratch:]
    main = refs[:n_main_in] + refs[a:a + n_main_out] + refs[b:b + n_main_scratch]
    ids = [pl.program_id(k) for k in range(len(grid))]

    def start():
        if side.n_in:
            pl.when(functools.reduce(jnp.logical_and, [i == 0 for i in ids]))(lambda: side.start(ins, outs, sems))

    def finish():
        if side.n_in:
            last = functools.reduce(jnp.logical_and, [i == g - 1 for i, g in zip(ids, grid)])
            pl.when(last)(lambda: side.finish(ins, outs, sems))

    return main, start, finish


def _carrier_call(body, name, grid, in_specs, out_specs, out_shape, scratch, side, args):
    aliases = {len(in_specs) + k: len(out_specs) + v for k, v in side.aliases.items()}
    return pl.pallas_call(
        body, name=name, grid=grid, in_specs=list(in_specs) + [_ANY] * side.n_in,
        out_specs=list(out_specs) + [_ANY] * side.n_out, out_shape=list(out_shape) + list(side.out_shape),
        scratch_shapes=list(scratch) + list(side.scratch), input_output_aliases=aliases,
        compiler_params=_params(("arbitrary",) * len(grid)))(*args, *side.operands)


_MM_VMEM_BUDGET = 44 * 1024 * 1024


def _matmul_tiles(mode, m, n, k, out_bytes):
    tk = _tile(k, _MM_TILES)
    tm = _tile(m, _MM_TILES)
    in_bytes = jnp.dtype(MXU_DTYPE).itemsize
    for tn in _MM_TILES:
        if n % tn:
            continue
        need = 2 * in_bytes * (tm * tk + tk * tn) + tm * tn * (2 * out_bytes + (4 if k > tk else 0))
        if need <= _MM_VMEM_BUDGET:
            return tm, tn, tk
    return tm, _tile(n, (LANES,)), tk


def _matmul(a, b, mode, name, out_dtype=F32, side=_NoSide):
    if mode == "nn":
        (m, k), n = a.shape, b.shape[1]
    elif mode == "nt":
        (m, k), n = a.shape, b.shape[0]
    else:
        (k, m), n = a.shape, b.shape[1]
    tm, tn, tk = _matmul_tiles(mode, m, n, k, jnp.dtype(out_dtype).itemsize)
    nk = k // tk
    grid = (m // tm, n // tn, nk)
    dims = {"nn": _NN, "nt": _NT, "tn": _TN}[mode]

    def body(*refs):
        (a_ref, b_ref, o_ref, acc_ref), side_start, side_finish = _side_hooks(side, refs, 2, 1, 1, grid)
        kk = pl.program_id(2)
        side_start()
        if nk == 1:
            o_ref[...] = _mm(a_ref[...], b_ref[...], dims).astype(o_ref.dtype)
        else:
            @pl.when(kk == 0)
            def _():
                acc_ref[...] = _mm(a_ref[...], b_ref[...], dims)

            @pl.when(jnp.logical_and(kk > 0, kk < nk - 1))
            def _():
                acc_ref[...] += _mm(a_ref[...], b_ref[...], dims)

            @pl.when(kk == nk - 1)
            def _():
                o_ref[...] = (acc_ref[...] + _mm(a_ref[...], b_ref[...], dims)).astype(o_ref.dtype)

        side_finish()

    a_spec = (pl.BlockSpec((tk, tm), lambda i, j, l: (l, i)) if mode == "tn"
              else pl.BlockSpec((tm, tk), lambda i, j, l: (i, l)))
    b_spec = (pl.BlockSpec((tn, tk), lambda i, j, l: (j, l)) if mode == "nt"
              else pl.BlockSpec((tk, tn), lambda i, j, l: (l, j)))
    o_spec = pl.BlockSpec((tm, tn), lambda i, j, l: (i, j))
    out = _carrier_call(body, name, grid, [a_spec, b_spec], [o_spec], [jax.ShapeDtypeStruct((m, n), out_dtype)],
                        [pltpu.VMEM((tm, tn) if nk > 1 else (SUBLANES, LANES), F32)], side, (a, b))
    return out if side.n_in else out[0]


_ROW_TILES = (512, 256, 128)


def _resid_norm(x, delta, gt, g, sc, sh, name):
    t, d = x.shape
    tt = _tile(t, _ROW_TILES)
    has = delta is not None

    def body(*refs):
        if has:
            x_ref, d_ref, gt_ref, g_ref, sc_ref, sh_ref, xo_ref, h_ref = refs
            xv = x_ref[...] + gt_ref[...] * d_ref[...]
            xo_ref[...] = xv
        else:
            x_ref, g_ref, sc_ref, sh_ref, h_ref = refs
            xv = x_ref[...]
        r = lax.rsqrt(jnp.mean(xv * xv, axis=-1, keepdims=True) + EPS)
        y = xv * r * g_ref[...]
        h_ref[...] = (y * (1.0 + sc_ref[...]) + sh_ref[...]).astype(h_ref.dtype)

    row = pl.BlockSpec((tt, d), lambda i: (i, 0))
    vec = pl.BlockSpec((1, d), lambda i: (0, 0))
    if has:
        return pl.pallas_call(
            body, name=name, grid=(t // tt,), in_specs=[row, row, vec, vec, vec, vec], out_specs=[row, row],
            out_shape=[jax.ShapeDtypeStruct((t, d), F32), jax.ShapeDtypeStruct((t, d), MXU_DTYPE)],
            compiler_params=_params(("parallel",)))(x, delta, gt, g, sc, sh)
    h = pl.pallas_call(
        body, name=name + "_first", grid=(t // tt,), in_specs=[row, vec, vec, vec], out_specs=row,
        out_shape=jax.ShapeDtypeStruct((t, d), MXU_DTYPE), compiler_params=_params(("parallel",)))(x, g, sc, sh)
    return x, h


def _final_loss(x, delta, gt, g, target, name):
    t, d = x.shape
    tt = _tile(t, _ROW_TILES)

    def body(x_ref, d_ref, gt_ref, g_ref, tg_ref, dx_ref, dg_ref, loss_ref):
        @pl.when(pl.program_id(0) == 0)
        def _():
            dg_ref[...] = jnp.zeros_like(dg_ref)
            loss_ref[...] = jnp.zeros_like(loss_ref)

        xv = x_ref[...] + gt_ref[...] * d_ref[...]
        r = lax.rsqrt(jnp.mean(xv * xv, axis=-1, keepdims=True) + EPS)
        xh = xv * r
        diff = xh * g_ref[...] - tg_ref[...]
        loss_ref[...] += jnp.sum(diff * diff) * (0.5 / d)
        dy = diff * (1.0 / d)
        dg_ref[...] += jnp.sum(dy * xh, axis=0, keepdims=True)
        dxh = dy * g_ref[...]
        dx_ref[...] = r * (dxh - xh * jnp.mean(dxh * xh, axis=-1, keepdims=True))

    row = pl.BlockSpec((tt, d), lambda i: (i, 0))
    vec = pl.BlockSpec((1, d), lambda i: (0, 0))
    tile = pl.BlockSpec((SUBLANES, LANES), lambda i: (0, 0))
    return pl.pallas_call(
        body, name=name, grid=(t // tt,), in_specs=[row, row, vec, vec, row], out_specs=[row, vec, tile],
        out_shape=[jax.ShapeDtypeStruct((t, d), F32), jax.ShapeDtypeStruct((1, d), F32),
                   jax.ShapeDtypeStruct((SUBLANES, LANES), F32)],
        compiler_params=_params(("arbitrary",)))(x, delta, gt, g, target)


def _norm_bwd(x, dh, dres, g, sc, name, gate=None):
    t, d = x.shape
    tt = _tile(t, _ROW_TILES)
    gated = gate is not None

    def body(*refs):
        x_ref, dh_ref, dr_ref, g_ref, sc_ref = refs[:5]
        dx_ref, dsh_ref, dsc_ref, dg_ref = refs[5 + 2 * gated:9 + 2 * gated]

        @pl.when(pl.program_id(0) == 0)
        def _():
            for acc_ref in refs[6 + 2 * gated:9 + 2 * gated] + refs[10 + 2 * gated:]:
                acc_ref[...] = jnp.zeros_like(acc_ref)

        xv, dh = x_ref[...], dh_ref[...]
        r = lax.rsqrt(jnp.mean(xv * xv, axis=-1, keepdims=True) + EPS)
        xh = xv * r
        gv, sc1 = g_ref[...], 1.0 + sc_ref[...]
        dsh_ref[...] += jnp.sum(dh, axis=0, keepdims=True)
        dsc_ref[...] += jnp.sum(dh * xh, axis=0, keepdims=True) * gv
        dg_ref[...] += jnp.sum(dh * xh, axis=0, keepdims=True) * sc1
        dxh = dh * (gv * sc1)
        dx = dr_ref[...] + r * (dxh - xh * jnp.mean(dxh * xh, axis=-1, keepdims=True))
        dx_ref[...] = dx
        if gated:
            br_ref, gt_ref, db_ref, dgt_ref = refs[5], refs[6], refs[11], refs[12]
            db_ref[...] = (dx * gt_ref[...]).astype(db_ref.dtype)
            dgt_ref[...] += jnp.sum(dx * br_ref[...], axis=0, keepdims=True)

    row = pl.BlockSpec((tt, d), lambda i: (i, 0))
    vec = pl.BlockSpec((1, d), lambda i: (0, 0))
    vshape = jax.ShapeDtypeStruct((1, d), F32)
    in_specs, out_specs = [row, row, row, vec, vec], [row, vec, vec, vec]
    out_shape = [jax.ShapeDtypeStruct((t, d), F32), vshape, vshape, vshape]
    if gated:
        in_specs, out_specs = in_specs + [row, vec], out_specs + [row, vec]
        out_shape = out_shape + [jax.ShapeDtypeStruct((t, d), MXU_DTYPE), vshape]
    return pl.pallas_call(
        body, name=name + ("_gate" if gated else ""), grid=(t // tt,), in_specs=in_specs, out_specs=out_specs,
        out_shape=out_shape, compiler_params=_params(("arbitrary",)))(x, dh, dres, g, sc, *(gate or ()))


def _gate_bwd(dxo, branch, gt, name):
    t, d = dxo.shape
    tt = _tile(t, _ROW_TILES)

    def body(dx_ref, br_ref, gt_ref, db_ref, dgt_ref):
        @pl.when(pl.program_id(0) == 0)
        def _():
            dgt_ref[...] = jnp.zeros_like(dgt_ref)

        dx = dx_ref[...]
        db_ref[...] = (dx * gt_ref[...]).astype(db_ref.dtype)
        dgt_ref[...] += jnp.sum(dx * br_ref[...], axis=0, keepdims=True)

    row = pl.BlockSpec((tt, d), lambda i: (i, 0))
    vec = pl.BlockSpec((1, d), lambda i: (0, 0))
    return pl.pallas_call(
        body, name=name, grid=(t // tt,), in_specs=[row, row, vec], out_specs=[row, vec],
        out_shape=[jax.ShapeDtypeStruct((t, d), MXU_DTYPE), jax.ShapeDtypeStruct((1, d), F32)],
        compiler_params=_params(("arbitrary",)))(dxo, branch, gt)


def _swiglu_fwd(gu, name):
    t, f2 = gu.shape
    f = f2 // 2
    tt = _tile(t, (256, 128))

    def body(g_ref, u_ref, o_ref):
        o_ref[...] = (_silu(g_ref[...].astype(F32)) * u_ref[...].astype(F32)).astype(o_ref.dtype)

    return pl.pallas_call(
        body, name=name, grid=(t // tt,),
        in_specs=[pl.BlockSpec((tt, f), lambda i: (i, 0)), pl.BlockSpec((tt, f), lambda i: (i, 1))],
        out_specs=pl.BlockSpec((tt, f), lambda i: (i, 0)), out_shape=jax.ShapeDtypeStruct((t, f), MXU_DTYPE),
        compiler_params=_params(("parallel",)))(gu, gu)


def _swiglu_bwd(gu, da, name):
    t, f2 = gu.shape
    f = f2 // 2
    tt = _tile(t, (256, 128))

    def body(g_ref, u_ref, da_ref, o_ref):
        gate, da = g_ref[...].astype(F32), da_ref[...]
        sg = _sigmoid(gate)
        o_ref[:, :f] = (da * u_ref[...].astype(F32) * (sg * (1.0 + gate * (1.0 - sg)))).astype(o_ref.dtype)
        o_ref[:, f:] = (da * (gate * sg)).astype(o_ref.dtype)

    return pl.pallas_call(
        body, name=name, grid=(t // tt,),
        in_specs=[pl.BlockSpec((tt, f), lambda i: (i, 0)), pl.BlockSpec((tt, f), lambda i: (i, 1)),
                  pl.BlockSpec((tt, f), lambda i: (i, 0))],
        out_specs=pl.BlockSpec((tt, f2), lambda i: (i, 0)), out_shape=jax.ShapeDtypeStruct((t, f2), MXU_DTYPE),
        compiler_params=_params(("parallel",)))(gu, gu, da)


class _ProjLayout:
    def __init__(self, d):
        wc = 3 * HEADS * HEAD_DIM
        self.d, self.wc = d, wc
        self.qkv, self.z, self.uv, self.gates, self.ba = 0, wc, wc + d, wc + 3 * d, wc + 5 * d
        self.width = self.ba + LANES
        assert self.z % d == 0 and self.uv % (2 * d) == 0 and self.gates % (2 * d) == 0 and self.ba % LANES == 0

    def pieces(self, shard):
        d, wc, out, lo = self.d, self.wc, [], 0
        for length, dst in ((2 * d, self.uv), (wc, self.qkv), (d, self.z), (2 * HEADS, self.ba), (2 * d, self.gates)):
            pos = lo
            while pos < lo + length:
                j = pos // shard
                n = min(lo + length, (j + 1) * shard) - pos
                out.append((j, pos - j * shard, n, dst + pos - lo))
                pos += n
            lo += length
        return out


def _merge_fwd(pa, pb, proj, gcol, name):
    t, d = pa.shape
    tt = _tile(t, _ROW_TILES)

    def body(pa_ref, pb_ref, ga_ref, gb_ref, o_ref):
        sa, sb = _sigmoid(ga_ref[...].astype(F32)), _sigmoid(gb_ref[...].astype(F32))
        o_ref[...] = (sa * pa_ref[...] + sb * pb_ref[...]).astype(o_ref.dtype)

    row = pl.BlockSpec((tt, d), lambda i: (i, 0))
    gate = lambda k: pl.BlockSpec((tt, d), lambda i: (i, gcol // d + k))
    return pl.pallas_call(
        body, name=name, grid=(t // tt,), in_specs=[row, row, gate(0), gate(1)], out_specs=row,
        out_shape=jax.ShapeDtypeStruct((t, d), MXU_DTYPE), compiler_params=_params(("parallel",)))(pa, pb, proj, proj)


def _merge_bwd(dm, pa, pb, proj, gcol, dproj, name):
    t, d = pa.shape
    tt = _tile(t, _ROW_TILES)

    def body(dm_ref, pa_ref, pb_ref, ga_ref, gb_ref, _, dpa_ref, dpb_ref, dg_ref):
        dm = dm_ref[...]
        sa, sb = _sigmoid(ga_ref[...].astype(F32)), _sigmoid(gb_ref[...].astype(F32))
        dpa_ref[...] = (dm * sa).astype(dpa_ref.dtype)
        dpb_ref[...] = (dm * sb).astype(dpb_ref.dtype)
        dg_ref[:, :d] = (dm * pa_ref[...] * sa * (1.0 - sa)).astype(dg_ref.dtype)
        dg_ref[:, d:] = (dm * pb_ref[...] * sb * (1.0 - sb)).astype(dg_ref.dtype)

    row = pl.BlockSpec((tt, d), lambda i: (i, 0))
    gate = lambda k: pl.BlockSpec((tt, d), lambda i: (i, gcol // d + k))
    wide = pl.BlockSpec((tt, 2 * d), lambda i: (i, gcol // (2 * d)))
    return pl.pallas_call(
        body, name=name, grid=(t // tt,), in_specs=[row, row, row, gate(0), gate(1), _ANY], out_specs=[row, row, wide],
        out_shape=[jax.ShapeDtypeStruct((t, d), MXU_DTYPE), jax.ShapeDtypeStruct((t, d), MXU_DTYPE),
                   jax.ShapeDtypeStruct(dproj.shape, dproj.dtype)],
        input_output_aliases={5: 2}, compiler_params=_params(("parallel",)))(dm, pa, pb, proj, proj, dproj)


def _tri_masks(n):
    ri = lax.broadcasted_iota(jnp.int32, (n, n), 0)
    ci = lax.broadcasted_iota(jnp.int32, (n, n), 1)
    return ri >= ci, ri > ci, ri == ci


def _mixer_a_fwd(proj, ucol, w_s, b_col, g_v, name):
    t, w = proj.shape[0], g_v.shape[1]
    c = A_CHUNK

    def body(u_ref, v_ref, w_ref, b_ref, gv_ref, y_ref):
        tril, _, _ = _tri_masks(c)
        ug, vg = _gelu(u_ref[...].astype(F32)), _gelu(v_ref[...].astype(F32))
        for g in range(GROUPS):
            sl = slice(g * c, (g + 1) * c)
            vt = vg[:, sl]
            r = lax.rsqrt(jnp.mean(vt * vt, axis=-1, keepdims=True) + EPS)
            vn = vt * r * gv_ref[:, sl]
            s = _mm(jnp.where(tril, w_ref[g], 0.0), vn) + b_ref[g]
            y_ref[:, sl] = (ug[:, sl] * s).astype(y_ref.dtype)

    return pl.pallas_call(
        body, name=name, grid=(t // c,),
        in_specs=[pl.BlockSpec((c, w), lambda i: (i, ucol // w)), pl.BlockSpec((c, w), lambda i: (i, ucol // w + 1)),
                  pl.BlockSpec((GROUPS, c, c), lambda i: (0, 0, 0)), pl.BlockSpec((GROUPS, c, 1), lambda i: (0, 0, 0)),
                  pl.BlockSpec((1, w), lambda i: (0, 0))],
        out_specs=pl.BlockSpec((c, w), lambda i: (i, 0)), out_shape=jax.ShapeDtypeStruct((t, w), MXU_DTYPE),
        compiler_params=_params(("parallel",)))(proj, proj, w_s, b_col, g_v)


def _mixer_a_bwd(proj, ucol, dy, w_s, w_st, b_col, g_v, dproj, name):
    t, w = proj.shape[0], g_v.shape[1]
    w2 = 2 * w
    c = A_CHUNK

    def body(u_ref, v_ref, dy_ref, w_ref, wt_ref, b_ref, gv_ref, _, duv_ref, dw_ref, db_ref, dgv_ref):
        @pl.when(pl.program_id(0) == 0)
        def _():
            dw_ref[...] = jnp.zeros_like(dw_ref)
            db_ref[...] = jnp.zeros_like(db_ref)
            dgv_ref[...] = jnp.zeros_like(dgv_ref)

        tril, _, _ = _tri_masks(c)
        triu = lax.broadcasted_iota(jnp.int32, (c, c), 0) <= lax.broadcasted_iota(jnp.int32, (c, c), 1)
        (ug, dug), (vg, dvg) = _gelu_and_slope(u_ref[...].astype(F32)), _gelu_and_slope(v_ref[...].astype(F32))
        for g in range(GROUPS):
            sl = slice(g * c, (g + 1) * c)
            vt = vg[:, sl]
            r = lax.rsqrt(jnp.mean(vt * vt, axis=-1, keepdims=True) + EPS)
            vh = vt * r
            gv = gv_ref[:, sl]
            vn = vh * gv
            s = _mm(jnp.where(tril, w_ref[g], 0.0), vn) + b_ref[g]
            dy = dy_ref[:, sl]
            ds = dy * ug[:, sl]
            dw_ref[g] += jnp.where(tril, _mm(ds, vn, _NT), 0.0)
            db_ref[g] += jnp.sum(ds, axis=1, keepdims=True)
            dvn = _mm(jnp.where(triu, wt_ref[g], 0.0), ds)
            dgv_ref[:, sl] += jnp.sum(dvn * vh, axis=0, keepdims=True)
            dvh = dvn * gv
            dvt = r * (dvh - vh * jnp.mean(dvh * vh, axis=-1, keepdims=True))
            duv_ref[:, sl] = (dy * s * dug[:, sl]).astype(duv_ref.dtype)
            duv_ref[:, w + g * c:w + (g + 1) * c] = (dvt * dvg[:, sl]).astype(duv_ref.dtype)

    full3 = lambda shape: pl.BlockSpec(shape, lambda i: (0, 0, 0))
    return pl.pallas_call(
        body, name=name, grid=(t // c,),
        in_specs=[pl.BlockSpec((c, w), lambda i: (i, ucol // w)), pl.BlockSpec((c, w), lambda i: (i, ucol // w + 1)),
                  pl.BlockSpec((c, w), lambda i: (i, 0)), full3((GROUPS, c, c)), full3((GROUPS, c, c)),
                  full3((GROUPS, c, 1)), pl.BlockSpec((1, w), lambda i: (0, 0)), _ANY],
        out_specs=[pl.BlockSpec((c, w2), lambda i: (i, ucol // w2)), full3((GROUPS, c, c)), full3((GROUPS, c, 1)),
                   pl.BlockSpec((1, w), lambda i: (0, 0))],
        out_shape=[jax.ShapeDtypeStruct(dproj.shape, dproj.dtype), jax.ShapeDtypeStruct((GROUPS, c, c), F32),
                   jax.ShapeDtypeStruct((GROUPS, c, 1), F32), jax.ShapeDtypeStruct((1, w), F32)],
        input_output_aliases={7: 0},
        compiler_params=_params(("arbitrary",)))(proj, proj, dy, w_s, w_st, b_col, g_v, dproj)


_Q_SCALE = HEAD_DIM ** -0.5


CONV_HALO = 16


def _conv_taps(x_ref, p_ref, w_ref):
    prev = jnp.where(pl.program_id(0) > 0, p_ref[...].astype(F32), 0.0)
    ext = jnp.concatenate([prev, x_ref[...].astype(F32)], axis=0)
    shifted = [ext[CONV_HALO:]] + [pltpu.roll(ext, s, 0)[CONV_HALO:] for s in range(1, CONV_K)]
    acc = shifted[0] * w_ref[pl.ds(CONV_K - 1, 1), :]
    for s in range(1, CONV_K):
        acc = acc + shifted[s] * w_ref[pl.ds(CONV_K - 1 - s, 1), :]
    return acc, shifted


def _conv_fwd(qkv, w, name):
    t, cw = qkv.shape[0], w.shape[1]
    tt = _tile(t, (256, 128))
    hb = tt // CONV_HALO

    def body(x_ref, p_ref, w_ref, o_ref):
        acc, _ = _conv_taps(x_ref, p_ref, w_ref)
        y = _silu(acc)
        for which in range(3):
            for h in range(HEADS):
                lo = (which * HEADS + h) * HEAD_DIM
                seg = y[:, lo:lo + HEAD_DIM]
                if which < 2:
                    seg = seg * lax.rsqrt(jnp.sum(seg * seg, axis=-1, keepdims=True) + EPS)
                if which == 0:
                    seg = seg * _Q_SCALE
                o_ref[which, h] = seg

    return pl.pallas_call(
        body, name=name, grid=(t // tt,),
        in_specs=[pl.BlockSpec((tt, cw), lambda i: (i, 0)),
                  pl.BlockSpec((CONV_HALO, cw), lambda i: (jnp.maximum(i * hb - 1, 0), 0)),
                  pl.BlockSpec((CONV_K, cw), lambda i: (0, 0))],
        out_specs=pl.BlockSpec((3, HEADS, tt, HEAD_DIM), lambda i: (0, 0, i, 0)),
        out_shape=jax.ShapeDtypeStruct((3, HEADS, t, HEAD_DIM), F32),
        compiler_params=_params(("parallel",)))(qkv, qkv, w)


def _conv_bwd_pre(qkv, dq, dk, dv, w, name):
    t, cw = qkv.shape[0], w.shape[1]
    tt = _tile(t, (256, 128))
    hb = tt // CONV_HALO

    def body(x_ref, p_ref, dq_ref, dk_ref, dv_ref, w_ref, da_ref, dw_ref):
        @pl.when(pl.program_id(0) == 0)
        def _():
            dw_ref[...] = jnp.zeros_like(dw_ref)

        acc, shifted = _conv_taps(x_ref, p_ref, w_ref)
        sg = _sigmoid(acc)
        y = acc * sg
        dsilu = sg * (1.0 + acc * (1.0 - sg))
        d_refs = (dq_ref, dk_ref, dv_ref)
        for which in range(3):
            for h in range(HEADS):
                lo = (which * HEADS + h) * HEAD_DIM
                sl = slice(lo, lo + HEAD_DIM)
                dn = d_refs[which][h]
                if which < 2:
                    seg = y[:, sl]
                    rho = lax.rsqrt(jnp.sum(seg * seg, axis=-1, keepdims=True) + EPS)
                    nrm = seg * rho
                    if which == 0:
                        dn = dn * _Q_SCALE
                    dn = rho * (dn - nrm * jnp.sum(dn * nrm, axis=-1, keepdims=True))
                dacc = dn * dsilu[:, sl]
                da_ref[:, sl] = dacc
                for s in range(CONV_K):
                    dw_ref[pl.ds(CONV_K - 1 - s, 1), sl] += jnp.sum(dacc * shifted[s][:, sl], axis=0, keepdims=True)

    head = pl.BlockSpec((HEADS, tt, HEAD_DIM), lambda i: (0, i, 0))
    return pl.pallas_call(
        body, name=name, grid=(t // tt,),
        in_specs=[pl.BlockSpec((tt, cw), lambda i: (i, 0)),
                  pl.BlockSpec((CONV_HALO, cw), lambda i: (jnp.maximum(i * hb - 1, 0), 0)),
                  head, head, head, pl.BlockSpec((CONV_K, cw), lambda i: (0, 0))],
        out_specs=[pl.BlockSpec((tt, cw), lambda i: (i, 0)), pl.BlockSpec((CONV_K, cw), lambda i: (0, 0))],
        out_shape=[jax.ShapeDtypeStruct((t, cw), F32), jax.ShapeDtypeStruct((CONV_K, cw), F32)],
        compiler_params=_params(("arbitrary",)))(qkv, qkv, dq, dk, dv, w)


def _conv_bwd_in(dacc, w, dproj, name):
    t, cw = dacc.shape
    tt = _tile(t, (256, 128))
    hb = tt // SUBLANES
    nt = t // tt
    rows = tt + SUBLANES

    def body(d_ref, n_ref, w_ref, _, o_ref):
        cur = d_ref[...]
        nxt = jnp.where(pl.program_id(0) < nt - 1, n_ref[...], 0.0)
        ext = jnp.concatenate([cur, nxt], axis=0)
        acc = cur * w_ref[pl.ds(CONV_K - 1, 1), :]
        for s in range(1, CONV_K):
            acc = acc + pltpu.roll(ext, rows - s, 0)[:tt] * w_ref[pl.ds(CONV_K - 1 - s, 1), :]
        o_ref[...] = acc.astype(o_ref.dtype)

    return pl.pallas_call(
        body, name=name, grid=(nt,),
        in_specs=[pl.BlockSpec((tt, cw), lambda i: (i, 0)),
                  pl.BlockSpec((SUBLANES, cw), lambda i: (jnp.minimum((i + 1) * hb, t // SUBLANES - 1), 0)),
                  pl.BlockSpec((CONV_K, cw), lambda i: (0, 0)), _ANY],
        out_specs=pl.BlockSpec((tt, cw), lambda i: (i, 0)), out_shape=jax.ShapeDtypeStruct(dproj.shape, dproj.dtype),
        input_output_aliases={3: 0}, compiler_params=_params(("parallel",)))(dacc, dacc, w, dproj)


_INV_BASE_SHIFT = 3


def _inv_unit_lower(a, eye):
    c = GDN_CHUNK
    ri = lax.broadcasted_iota(jnp.int32, (c, c), 0)
    ci = lax.broadcasted_iota(jnp.int32, (c, c), 1)
    same = lambda sh: (ri >> sh) == (ci >> sh)
    x = jnp.where(same(_INV_BASE_SHIFT), -a, 0.0)
    p = jnp.where(eye, 1.0, 0.0) + x
    xs = _split(x)
    x2 = _mm3(xs, xs)
    x2s, ps = _split(x2), _split(p)
    r = _mm3(x2s, tuple(jnp.concatenate([u, v], axis=-1) for u, v in zip(x2s, ps)))
    x4, p = r[..., :c], p + r[..., c:]
    p = p + _mm3(_split(x4), _split(p))
    for sh in range(_INV_BASE_SHIFT, c.bit_length() - 1):
        off = jnp.where(same(sh + 1) & jnp.logical_not(same(sh)), a, 0.0)
        ps = _split(p)
        p = p - _mm3(ps, _split(_mm3(_split(off), ps)))
    return p


def _split(a):
    hi = a.astype(BF16)
    return hi, (a - hi.astype(F32)).astype(BF16)


def _dot_heads(u, v, dims):
    if u.ndim == 3:
        return jnp.stack([_dot_heads(u[j], v[j], dims) for j in range(u.shape[0])])
    return lax.dot_general(u, v, dims, preferred_element_type=F32)


def _mm3(a, b):
    return _dot_heads(a[0], b[0], _NN) + (_dot_heads(a[0], b[1], _NN) + _dot_heads(a[1], b[0], _NN))


def _hmm(a, b, dims=_NN):
    return _dot_heads(a.astype(MXU_DTYPE), b.astype(MXU_DTYPE), dims)


def _rowsum(x):
    return jnp.sum(x, axis=-1, keepdims=True)


def _colsum(x):
    return jnp.sum(x, axis=-2, keepdims=True)


class _Pre:
    pass


def _gdn_pre(q, k, v, araw, braw, alog, dtb, t_mat=None):
    c = GDN_CHUNK
    p = _Pre()
    p.tril, p.strict, p.eye = _tri_masks(c)
    p.to_col = lambda row: _rowsum(jnp.where(p.eye, row, 0.0))
    p.to_row = lambda col: _colsum(jnp.where(p.eye, col, 0.0))
    p.a_neg = -jnp.exp(alog + jnp.zeros((1, c), F32))
    p.xg = araw + dtb
    p.g_row = p.a_neg * _softplus(p.xg)
    p.beta_row = _sigmoid(braw)
    p.beta = p.to_col(p.beta_row)
    gam = _rowsum(jnp.where(p.tril, p.g_row, 0.0))
    gam_last = _rowsum(p.g_row)
    p.dm = jnp.where(p.tril, jnp.exp(jnp.where(p.tril, gam - p.to_row(gam), 0.0)), 0.0)
    p.e, p.ek, p.el = jnp.exp(gam), jnp.exp(gam_last - gam), jnp.exp(gam_last)
    p.kb = k * p.beta
    p.kk = _hmm(p.kb, k, _NT)
    p.t = _inv_unit_lower(jnp.where(p.strict, p.kk * p.dm, 0.0), p.eye) if t_mat is None else t_mat
    p.vb, p.kbe = v * p.beta, p.kb * p.e
    uw = _hmm(p.t, jnp.concatenate([p.vb, p.kbe], axis=-1))
    p.u, p.w = uw[..., :v.shape[-1]], uw[..., v.shape[-1]:]
    p.qk0 = _hmm(q, k, _NT)
    p.qk = p.qk0 * p.dm
    p.qd, p.kd = q * p.e, k * p.ek
    return p


GDN_HEADS_PER_STEP = 8


def _head_scalars(ref, hb):
    h0 = pl.program_id(0) * hb
    return jnp.stack([jnp.full((1, 1), ref[h0 + j], F32) for j in range(hb)])


def _gdn_specs(n, reverse):
    c, dk, hb = GDN_CHUNK, HEAD_DIM, GDN_HEADS_PER_STEP
    ix = (lambda i: n - 1 - i) if reverse else (lambda i: i)
    smem = pl.BlockSpec(memory_space=pltpu.SMEM)
    qkv = [pl.BlockSpec((None, hb, c, dk), functools.partial(lambda w, h, i: (w, h, ix(i), 0), w)) for w in range(3)]
    row = pl.BlockSpec((hb, None, 1, c), lambda h, i: (h, ix(i), 0, 0))
    tok = pl.BlockSpec((hb, c, dk), lambda h, i: (h, ix(i), 0))
    state = pl.BlockSpec((hb, None, dk, dk), lambda h, i: (h, ix(i), 0, 0))
    return smem, qkv, row, tok, state


def _gdn_fwd(qkv_h, araw, braw, alog, dtb, name, side=_NoSide):
    _, hh, t, dk = qkv_h.shape
    n, hb = t // GDN_CHUNK, GDN_HEADS_PER_STEP
    smem, qkv, row, tok, state = _gdn_specs(n, False)
    grid = (hh // hb, n)

    def body(*refs):
        main, side_start, side_finish = _side_hooks(side, refs, 7, 3, 1, grid)
        alog_ref, dt_ref, q_ref, k_ref, v_ref, a_ref, b_ref, o_ref, so_ref, to_ref, s_ref = main
        side_start()

        @pl.when(pl.program_id(1) == 0)
        def _():
            s_ref[...] = jnp.zeros_like(s_ref)

        p = _gdn_pre(q_ref[...], k_ref[...], v_ref[...], a_ref[...], b_ref[...],
                     _head_scalars(alog_ref, hb), _head_scalars(dt_ref, hb))
        s = s_ref[...]
        vn = p.u - _hmm(p.w, s)
        o_ref[...] = _hmm(p.qd, s) + _hmm(p.qk, vn)
        so_ref[...] = s
        to_ref[...] = p.t
        s_ref[...] = s * p.el + _hmm(p.kd, vn, _TN)
        side_finish()

    mats = jax.ShapeDtypeStruct((hh, n, dk, dk), F32)
    return _carrier_call(
        body, name, grid, [smem, smem] + qkv + [row, row], [tok, state, state],
        [jax.ShapeDtypeStruct((hh, t, dk), F32), mats, mats],
        [pltpu.VMEM((hb, dk, dk), F32)], side, (alog, dtb, qkv_h, qkv_h, qkv_h, araw, braw))


def _gdn_bwd(qkv_h, araw, braw, alog, dtb, states, t_mats, do, name, side=_NoSide):
    _, hh, t, dk = qkv_h.shape
    c, hb = GDN_CHUNK, GDN_HEADS_PER_STEP
    n = t // c
    smem, qkv, row, tok, state = _gdn_specs(n, True)
    acc = pl.BlockSpec((hb, 1, LANES), lambda h, i: (h, 0, 0))
    grid = (hh // hb, n)

    def body(*refs):
        main, side_start, side_finish = _side_hooks(side, refs, 10, 7, 1, grid)
        (alog_ref, dt_ref, q_ref, k_ref, v_ref, a_ref, b_ref, s_ref, t_ref, do_ref,
         dq_ref, dk_ref, dv_ref, da_ref, db_ref, dal_ref, ddt_ref, ds_ref) = main
        side_start()

        @pl.when(pl.program_id(1) == 0)
        def _():
            ds_ref[...] = jnp.zeros_like(ds_ref)
            dal_ref[...] = jnp.zeros_like(dal_ref)
            ddt_ref[...] = jnp.zeros_like(ddt_ref)

        q, k, v = q_ref[...], k_ref[...], v_ref[...]
        p = _gdn_pre(q, k, v, a_ref[...], b_ref[...], _head_scalars(alog_ref, hb), _head_scalars(dt_ref, hb),
                     t_ref[...])
        s, do, dsp = s_ref[...], do_ref[...], ds_ref[...]
        vn = p.u - _hmm(p.w, s)
        dqd = _hmm(do, s, _NT)
        dqk = _hmm(do, vn, _NT)
        dvn = _hmm(p.qk, do, _TN) + _hmm(p.kd, dsp)
        dkd = _hmm(vn, dsp, _NT)
        d_el = _colsum(_rowsum(s * dsp))
        ds_ref[...] = dsp * p.el + _hmm(p.qd, do, _TN) - _hmm(p.w, dvn, _TN)
        dw = -_hmm(dvn, s, _NT)
        d_t = _hmm(dvn, p.vb, _NT) + _hmm(dw, p.kbe, _NT)
        dvb, dkbe = _hmm(p.t, dvn, _TN), _hmm(p.t, dw, _TN)
        d_a = jnp.where(p.strict, -_hmm(p.t, _hmm(d_t, p.t, _NT), _TN), 0.0)
        dkk = d_a * p.dm
        dqk0 = dqk * p.dm
        ddm = d_a * p.kk + dqk * p.qk0
        dkb = _hmm(dkk, k) + dkbe * p.e
        dq_ref[...] = _hmm(dqk0, k) + dqd * p.e
        dk_ref[...] = _hmm(dkk, p.kb, _TN) + _hmm(dqk0, q, _TN) + dkd * p.ek + dkb * p.beta
        dv_ref[...] = dvb * p.beta
        dbeta = _rowsum(dkb * k) + _rowsum(dvb * v)
        d_e = _rowsum(dqd * q) + _rowsum(dkbe * p.kb)
        d_ek = _rowsum(dkd * k)
        m = ddm * p.dm
        dgam = d_e * p.e - d_ek * p.ek + _rowsum(m) - p.to_col(_colsum(m))
        dgam_last = _colsum(d_ek * p.ek) + d_el * p.el
        dg_row = _colsum(jnp.where(p.tril, dgam, 0.0)) + dgam_last
        da_row = dg_row * p.a_neg * _sigmoid(p.xg)
        da_ref[...] = da_row
        db_ref[...] = p.to_row(dbeta) * p.beta_row * (1.0 - p.beta_row)
        dal_ref[...] += _rowsum(dg_row * p.g_row)
        ddt_ref[...] += _rowsum(da_row)
        side_finish()

    tok_shape = jax.ShapeDtypeStruct((hh, t, dk), F32)
    row_shape = jax.ShapeDtypeStruct((hh, n, 1, c), F32)
    acc_shape = jax.ShapeDtypeStruct((hh, 1, LANES), F32)
    return _carrier_call(
        body, name, grid, [smem, smem] + qkv + [row, row, state, state, tok], [tok, tok, tok, row, row, acc, acc],
        [tok_shape, tok_shape, tok_shape, row_shape, row_shape, acc_shape, acc_shape],
        [pltpu.VMEM((hb, dk, dk), F32)], side, (alog, dtb, qkv_h, qkv_h, qkv_h, araw, braw, states, t_mats, do))


def _gdn_post_fwd(o, proj, zcol, g_o, name):
    hh, t, dv = o.shape
    tt = _tile(t, _ROW_TILES)
    zblk = zcol // (hh * dv)

    def body(o_ref, z_ref, g_ref, y_ref):
        for h in range(hh):
            sl = slice(h * dv, (h + 1) * dv)
            ov = o_ref[h]
            r = lax.rsqrt(jnp.mean(ov * ov, axis=-1, keepdims=True) + EPS)
            y_ref[:, sl] = (ov * r * g_ref[...] * _silu(z_ref[:, sl].astype(F32))).astype(y_ref.dtype)

    return pl.pallas_call(
        body, name=name, grid=(t // tt,),
        in_specs=[pl.BlockSpec((hh, tt, dv), lambda i: (0, i, 0)), pl.BlockSpec((tt, hh * dv), lambda i: (i, zblk)),
                  pl.BlockSpec((1, dv), lambda i: (0, 0))],
        out_specs=pl.BlockSpec((tt, hh * dv), lambda i: (i, 0)),
        out_shape=jax.ShapeDtypeStruct((t, hh * dv), MXU_DTYPE), compiler_params=_params(("parallel",)))(o, proj, g_o)


def _gdn_post_bwd(o, proj, zcol, dy, g_o, dproj, name):
    hh, t, dv = o.shape
    tt = _tile(t, _ROW_TILES)
    zblk = zcol // (hh * dv)

    def body(o_ref, z_ref, dy_ref, g_ref, _, do_ref, dz_ref, dg_ref):
        @pl.when(pl.program_id(0) == 0)
        def _():
            dg_ref[...] = jnp.zeros_like(dg_ref)

        gv = g_ref[...]
        for h in range(hh):
            sl = slice(h * dv, (h + 1) * dv)
            ov, zz, dy = o_ref[h], z_ref[:, sl].astype(F32), dy_ref[:, sl]
            r = lax.rsqrt(jnp.mean(ov * ov, axis=-1, keepdims=True) + EPS)
            oh = ov * r
            sg = _sigmoid(zz)
            dz_ref[:, sl] = (dy * oh * gv * (sg * (1.0 + zz * (1.0 - sg)))).astype(dz_ref.dtype)
            don = dy * (zz * sg)
            dg_ref[...] += _colsum(don * oh)
            doh = don * gv
            do_ref[h] = r * (doh - oh * jnp.mean(doh * oh, axis=-1, keepdims=True))

    return pl.pallas_call(
        body, name=name, grid=(t // tt,),
        in_specs=[pl.BlockSpec((hh, tt, dv), lambda i: (0, i, 0)), pl.BlockSpec((tt, hh * dv), lambda i: (i, zblk)),
                  pl.BlockSpec((tt, hh * dv), lambda i: (i, 0)), pl.BlockSpec((1, dv), lambda i: (0, 0)), _ANY],
        out_specs=[pl.BlockSpec((hh, tt, dv), lambda i: (0, i, 0)), pl.BlockSpec((tt, hh * dv), lambda i: (i, zblk)),
                   pl.BlockSpec((1, dv), lambda i: (0, 0))],
        out_shape=[jax.ShapeDtypeStruct((hh, t, dv), F32), jax.ShapeDtypeStruct(dproj.shape, dproj.dtype),
                   jax.ShapeDtypeStruct((1, dv), F32)],
        input_output_aliases={4: 1}, compiler_params=_params(("arbitrary",)))(o, proj, dy, g_o, dproj)


def _cols_as_rows(x, col, name):
    t = x.shape[0]
    tt = _tile(t, _ROW_TILES)

    def body(x_ref, o_ref):
        o_ref[...] = x_ref[...].T

    return pl.pallas_call(
        body, name=name, grid=(t // tt,), in_specs=[pl.BlockSpec((tt, LANES), lambda i: (i, col // LANES))],
        out_specs=pl.BlockSpec((LANES, tt), lambda i: (0, i)), out_shape=jax.ShapeDtypeStruct((LANES, t), x.dtype),
        compiler_params=_params(("parallel",)))(x)


def _rows_into_cols(dst, rows, col, name):
    t = dst.shape[0]
    tt = _tile(t, _ROW_TILES)

    def body(r_ref, _, o_ref):
        o_ref[...] = r_ref[...].T.astype(o_ref.dtype)

    return pl.pallas_call(
        body, name=name, grid=(t // tt,), in_specs=[pl.BlockSpec((LANES, tt), lambda i: (0, i)), _ANY],
        out_specs=pl.BlockSpec((tt, LANES), lambda i: (i, col // LANES)),
        out_shape=jax.ShapeDtypeStruct(dst.shape, dst.dtype), input_output_aliases={1: 0},
        compiler_params=_params(("parallel",)))(rows, dst)


def _adamw(g, w, m, v):
    m = ADAM_B1 * m + (1.0 - ADAM_B1) * g
    v = ADAM_B2 * v + (1.0 - ADAM_B2) * (g * g)
    m_hat = m / (1.0 - ADAM_B1 ** ADAM_STEP)
    v_hat = v / (1.0 - ADAM_B2 ** ADAM_STEP)
    return -ADAM_LR * (m_hat / (jnp.sqrt(v_hat) + ADAM_EPS) + ADAM_WD * w), m, v


def _ada_fwd(c_all, ada_w, name):
    nl, d, cols = ada_w.shape
    b = c_all.shape[0]

    def body(c_ref, w_ref, o_ref):
        o_ref[...] = _mm_hi(_silu(c_ref[...]), w_ref[...])

    return pl.pallas_call(
        body, name=name, grid=(nl,),
        in_specs=[pl.BlockSpec((b, d), lambda i: (0, 0)), pl.BlockSpec((None, d, cols), lambda i: (i, 0, 0))],
        out_specs=pl.BlockSpec((None, b, cols), lambda i: (i, 0, 0)),
        out_shape=jax.ShapeDtypeStruct((nl, b, cols), F32), compiler_params=_params(("parallel",)))(c_all, ada_w)


def _ada_bwd(c_col, dm, w, m, v, name):
    nl, d, cols = w.shape
    b = c_col.shape[0]
    tr = _tile(d, (256, 128))

    def body(c_ref, dm_ref, w_ref, m_ref, v_ref, g_ref, dl_ref, mo_ref, vo_ref):
        g = _silu(c_ref[0]) * dm_ref[pl.ds(0, 1), :]
        for j in range(1, b):
            g = g + _silu(c_ref[j]) * dm_ref[pl.ds(j, 1), :]
        g_ref[...] = g
        dl_ref[...], mo_ref[...], vo_ref[...] = _adamw(g, w_ref[...], m_ref[...], v_ref[...])

    blk = pl.BlockSpec((None, tr, cols), lambda l, i: (l, i, 0))
    shape = jax.ShapeDtypeStruct((nl, d, cols), F32)
    return pl.pallas_call(
        body, name=name, grid=(nl, d // tr),
        in_specs=[pl.BlockSpec((b, tr, 1), lambda l, i: (0, i, 0)), pl.BlockSpec((None, b, cols), lambda l, i: (l, 0, 0)),
                  blk, blk, blk],
        out_specs=[blk, blk, blk, blk], out_shape=[shape] * 4,
        compiler_params=_params(("parallel", "parallel")))(c_col, dm, w, m, v)


_GRAD_ROW_TILES = (256, 128, 176, 88)


def _sum_adam(parts, w, m, v, name):
    nl, npart, r, cdim = parts.shape
    tr = _tile(r, _GRAD_ROW_TILES)

    def body(p_ref, w_ref, m_ref, v_ref, g_ref, dl_ref, mo_ref, vo_ref):
        g = p_ref[0].astype(F32)
        for j in range(1, npart):
            g = g + p_ref[j].astype(F32)
        g_ref[...] = g
        dl_ref[...], mo_ref[...], vo_ref[...] = _adamw(g, w_ref[...], m_ref[...], v_ref[...])

    blk = pl.BlockSpec((None, tr, cdim), lambda l, i: (l, i, 0))
    shape = jax.ShapeDtypeStruct((nl, r, cdim), F32)
    return pl.pallas_call(
        body, name=name, grid=(nl, r // tr),
        in_specs=[pl.BlockSpec((None, npart, tr, cdim), lambda l, i: (l, 0, i, 0)), blk, blk, blk],
        out_specs=[blk, blk, blk, blk], out_shape=[shape] * 4,
        compiler_params=_params(("parallel", "parallel")))(parts, w, m, v)


def _cols_from_blocks(g, plan, width, name):
    _, r, cdim = g.shape
    tr = _tile(r, (256, 128))
    covered = sorted((dst, dst + n) for _, _, n, dst in plan)
    holes, pos = [], 0
    for a, b in covered:
        if a > pos:
            holes.append((pos, a))
        pos = max(pos, b)
    if pos < width:
        holes.append((pos, width))

    def body(g_ref, o_ref):
        for a, b in holes:
            o_ref[:, a:b] = jnp.zeros((tr, b - a), o_ref.dtype)
        for j, src, n, dst in plan:
            o_ref[:, dst:dst + n] = g_ref[j, :, src:src + n]

    return pl.pallas_call(
        body, name=name, grid=(r // tr,), in_specs=[pl.BlockSpec((N_DEV, tr, cdim), lambda i: (0, i, 0))],
        out_specs=pl.BlockSpec((tr, width), lambda i: (i, 0)), out_shape=jax.ShapeDtypeStruct((r, width), g.dtype),
        compiler_params=_params(("parallel",)))(g)


def _blocks_from_cols(w, plan, cdim, name):
    r, width = w.shape
    tr = _tile(r, (256, 128))

    def body(w_ref, o_ref):
        for j, src, n, dst in plan:
            o_ref[j, :, src:src + n] = w_ref[:, dst:dst + n]

    return pl.pallas_call(
        body, name=name, grid=(r // tr,), in_specs=[pl.BlockSpec((tr, width), lambda i: (i, 0))],
        out_specs=pl.BlockSpec((N_DEV, tr, cdim), lambda i: (0, i, 0)),
        out_shape=jax.ShapeDtypeStruct((N_DEV, r, cdim), w.dtype), compiler_params=_params(("parallel",)))(w)


def _pair_sum(x, tmp, core, name):
    _, r, cdim = x.shape
    tr = _tile(r, _GRAD_ROW_TILES)

    def body(core_ref, x_ref, t_ref, o_ref):
        o_ref[...] = (x_ref[...] + t_ref[...]).astype(o_ref.dtype)

    grid_spec = pltpu.PrefetchScalarGridSpec(
        num_scalar_prefetch=1, grid=(N_DEV // 2, r // tr),
        in_specs=[pl.BlockSpec((None, tr, cdim), lambda ch, i, core_ref: (2 * ch + core_ref[0], i, 0)),
                  pl.BlockSpec((None, tr, cdim), lambda ch, i, core_ref: (ch, i, 0))],
        out_specs=pl.BlockSpec((None, tr, cdim), lambda ch, i, core_ref: (ch, i, 0)))
    return pl.pallas_call(
        body, name=name, grid_spec=grid_spec, out_shape=jax.ShapeDtypeStruct((N_DEV // 2, r, cdim), WIRE_DTYPE),
        compiler_params=_params(("parallel", "parallel")))(core, x, tmp)


_ANY = pl.BlockSpec(memory_space=pl.ANY)
_CHIP_FLIPS = ((1, 0), (0, 1), (1, 1))


def _coords():
    return lax.axis_index("x"), lax.axis_index("y"), lax.axis_index("c")


def _flip(v, f):
    return 1 - v if f else v


def _a2a_direct(xs, name):
    n, ncp = len(xs), N_DEV - 1

    def body(*refs):
        ins, outs = refs[:n], refs[n:2 * n]
        send, recv, loc = refs[2 * n:]
        x, y, c = _coords()
        me = 4 * x + 2 * y + c
        local = [pltpu.make_async_copy(ins[i].at[me], outs[i].at[me], loc.at[i]) for i in range(n)]
        for cp in local:
            cp.start()
        remote = []
        for i in range(n):
            for k in range(1, N_DEV):
                px, py, pc = _flip(x, k & 4), _flip(y, k & 2), _flip(c, k & 1)
                cp = pltpu.make_async_remote_copy(
                    src_ref=ins[i].at[4 * px + 2 * py + pc], dst_ref=outs[i].at[me],
                    send_sem=send.at[i * ncp + k - 1], recv_sem=recv.at[i * ncp + k - 1],
                    device_id=(px, py, pc), device_id_type=MESH)
                cp.start()
                remote.append(cp)
        for cp in remote:
            cp.wait()
        for cp in local:
            cp.wait()

    return pl.pallas_call(
        body, name=name, in_specs=[_ANY] * n, out_specs=[_ANY] * n,
        out_shape=[jax.ShapeDtypeStruct(a.shape, a.dtype) for a in xs],
        scratch_shapes=[pltpu.SemaphoreType.DMA((n * ncp,)), pltpu.SemaphoreType.DMA((n * ncp,)),
                        pltpu.SemaphoreType.DMA((n,))])(*xs)


class _AllGatherSide:
    def __init__(self, blocks):
        self.operands = list(blocks)
        n = self.n = len(self.operands)
        self.n_in = self.n_out = n
        self.out_shape = [jax.ShapeDtypeStruct((N_DEV,) + a.shape, a.dtype) for a in self.operands]
        self.aliases = {}
        nici, nd2d = len(_CHIP_FLIPS), N_DEV // 2
        self.scratch = [pltpu.SemaphoreType.DMA((n * nici,)), pltpu.SemaphoreType.DMA((n * nici,)),
                        pltpu.SemaphoreType.DMA((n * nd2d,)), pltpu.SemaphoreType.DMA((n * nd2d,)),
                        pltpu.SemaphoreType.DMA((n,))]

    def _first(self, ins, outs, sems):
        send, recv, _, _, loc = sems
        x, y, c = _coords()
        me = 4 * x + 2 * y + c
        nici = len(_CHIP_FLIPS)
        local = [pltpu.make_async_copy(ins[i], outs[i].at[me], loc.at[i]) for i in range(self.n)]
        remote = [pltpu.make_async_remote_copy(
            src_ref=ins[i], dst_ref=outs[i].at[me], send_sem=send.at[i * nici + j], recv_sem=recv.at[i * nici + j],
            device_id=(_flip(x, fx), _flip(y, fy), c), device_id_type=MESH)
            for i in range(self.n) for j, (fx, fy) in enumerate(_CHIP_FLIPS)]
        return local + remote

    def _second(self, outs, sems):
        _, _, send, recv, _ = sems
        x, y, c = _coords()
        nd2d = N_DEV // 2
        return [pltpu.make_async_remote_copy(
            src_ref=outs[i].at[2 * ch + c], dst_ref=outs[i].at[2 * ch + c], send_sem=send.at[i * nd2d + ch],
            recv_sem=recv.at[i * nd2d + ch], device_id=(x, y, 1 - c), device_id_type=MESH)
            for i in range(self.n) for ch in range(nd2d)]

    def start(self, ins, outs, sems):
        for cp in self._first(ins, outs, sems):
            cp.start()

    def finish(self, ins, outs, sems):
        for cp in self._first(ins, outs, sems):
            cp.wait()
        second = self._second(outs, sems)
        for cp in second:
            cp.start()
        for cp in second:
            cp.wait()


class _ReduceScatterIciSide:
    def __init__(self, sums, accs, layer):
        self.operands = list(sums) + list(accs)
        n = self.n = len(sums)
        self.layer = layer
        self.n_in, self.n_out = 2 * n, n
        self.out_shape = [jax.ShapeDtypeStruct(a.shape, a.dtype) for a in accs]
        self.aliases = {n + i: i for i in range(n)}
        nici = len(_CHIP_FLIPS)
        self.scratch = [pltpu.SemaphoreType.DMA((n * nici,)), pltpu.SemaphoreType.DMA((n * nici,)),
                        pltpu.SemaphoreType.DMA((n,))]

    def _copies(self, ins, outs, sems):
        send, recv, loc = sems
        x, y, c = _coords()
        chip = 2 * x + y
        nici = len(_CHIP_FLIPS)
        local = [pltpu.make_async_copy(ins[i].at[chip], outs[i].at[self.layer, chip], loc.at[i])
                 for i in range(self.n)]
        remote = [pltpu.make_async_remote_copy(
            src_ref=ins[i].at[2 * _flip(x, fx) + _flip(y, fy)], dst_ref=outs[i].at[self.layer, chip],
            send_sem=send.at[i * nici + j], recv_sem=recv.at[i * nici + j],
            device_id=(_flip(x, fx), _flip(y, fy), c), device_id_type=MESH)
            for i in range(self.n) for j, (fx, fy) in enumerate(_CHIP_FLIPS)]
        return local + remote

    def start(self, ins, outs, sems):
        for cp in self._copies(ins, outs, sems):
            cp.start()

    def finish(self, ins, outs, sems):
        for cp in self._copies(ins, outs, sems):
            cp.wait()


def _run_side(side, name):
    def body(*refs):
        ins, outs = refs[:side.n_in], refs[side.n_in:side.n_in + side.n_out]
        sems = refs[side.n_in + side.n_out:]
        side.start(ins, outs, sems)
        side.finish(ins, outs, sems)

    return pl.pallas_call(
        body, name=name, in_specs=[_ANY] * side.n_in, out_specs=[_ANY] * side.n_out, out_shape=side.out_shape,
        input_output_aliases=side.aliases, scratch_shapes=side.scratch)(*side.operands)


class _ReduceScatterD2dSide:
    def __init__(self, parts):
        self.operands = list(parts)
        n = self.n = len(self.operands)
        self.n_in = self.n_out = n
        nd2d = N_DEV // 2
        self.out_shape = [jax.ShapeDtypeStruct((nd2d,) + a.shape[1:], a.dtype) for a in self.operands]
        self.aliases = {}
        self.scratch = [pltpu.SemaphoreType.DMA((n * nd2d,)), pltpu.SemaphoreType.DMA((n * nd2d,))]

    def _copies(self, ins, outs, sems):
        send, recv = sems
        x, y, c = _coords()
        nd2d = N_DEV // 2
        return [pltpu.make_async_remote_copy(
            src_ref=ins[i].at[2 * ch + 1 - c], dst_ref=outs[i].at[ch], send_sem=send.at[i * nd2d + ch],
            recv_sem=recv.at[i * nd2d + ch], device_id=(x, y, 1 - c), device_id_type=MESH)
            for i in range(self.n) for ch in range(nd2d)]

    def start(self, ins, outs, sems):
        for cp in self._copies(ins, outs, sems):
            cp.start()

    def finish(self, ins, outs, sems):
        for cp in self._copies(ins, outs, sems):
            cp.wait()


_PACK_ROWS = 256


def _pack(arrs):
    flat = jnp.concatenate([a.reshape(-1) for a in arrs])
    quantum = _PACK_ROWS * LANES
    total = -(-flat.shape[0] // quantum) * quantum
    return jnp.pad(flat, (0, total - flat.shape[0])).reshape(-1, LANES)


def _unpack(packed, like):
    flat, out, pos = packed.reshape(-1), [], 0
    for a in like:
        out.append(flat[pos:pos + a.size].reshape(a.shape))
        pos += a.size
    return out


def kernel(x, c, ada_w, ada_b, norm1_g, w_in, conv_w, spatial_w, spatial_b, v_norm_g, a_log, dt_bias, o_norm_g, w_branch_a, w_branch_b, w_out, norm2_g, w_ffn_in, w_ffn_out, final_g, loss_target, m_ada_w, m_ada_b, m_norm1_g, m_w_in, m_conv_w, m_spatial_w, m_spatial_b, m_v_norm_g, m_a_log, m_dt_bias, m_o_norm_g, m_w_branch_a, m_w_branch_b, m_w_out, m_norm2_g, m_w_ffn_in, m_w_ffn_out, m_final_g, v_ada_w, v_ada_b, v_norm1_g, v_w_in, v_conv_w, v_spatial_w, v_spatial_b, v_v_norm_g, v_a_log, v_dt_bias, v_o_norm_g, v_w_branch_a, v_w_branch_b, v_w_out, v_norm2_g, v_w_ffn_in, v_w_ffn_out, v_final_g):
    nl, d = ada_w.shape[0], x.shape[2]
    t = x.shape[1]
    nchunk = t // GDN_CHUNK
    xi, yi, ci = _coords()
    me = 4 * xi + 2 * yi + ci
    core = jnp.reshape(ci, (1,)).astype(jnp.int32)
    x0, target = x[0], loss_target[0]
    wcols = 3 * HEADS * HEAD_DIM
    lay = _ProjLayout(d)
    in_pieces = lay.pieces(w_in.shape[2])
    fi_shard = w_ffn_in.shape[2]
    fi_pieces = [(j, 0, fi_shard, fi_shard * j) for j in range(N_DEV)]

    c_all, cw_all = _a2a_direct([jnp.broadcast_to(c[None], (N_DEV,) + c.shape),
                                 jnp.broadcast_to(conv_w[None], (N_DEV,) + conv_w.shape)], "gather_small")
    c_all = c_all[:, 0]
    conv_full = cw_all.transpose(1, 2, 0, 3).reshape(nl, CONV_K, wcols)
    modp = _ada_fwd(c_all, ada_w, "ada_fwd")
    (modx,) = _a2a_direct([modp.transpose(1, 0, 2)], "mod_exchange")
    mod = (modx.transpose(1, 0, 2).reshape(nl, 6 * d) + ada_b).reshape(nl, 6, 1, d)

    big = (w_in, w_branch_a, w_branch_b, w_out, w_ffn_in, w_ffn_out)
    big_wire = [w.astype(WIRE_DTYPE) for w in big]
    gather_in = lambda i: _AllGatherSide([big_wire[0][i]])
    gather_early = lambda i: _AllGatherSide([big_wire[k][i] for k in (1, 2, 3, 4)])
    gather_late = lambda i: _AllGatherSide([big_wire[5][i]] + ([big_wire[0][i + 1]] if i + 1 < nl else []))
    row_full = lambda g: g.reshape(-1, g.shape[2])
    padded_in = lambda g: _cols_from_blocks(g, in_pieces, lay.width, "w_in_cols")
    w_pads = [padded_in(_run_side(gather_in(0), "ag_first")[0])] + [None] * (nl - 1)
    weights = [None] * nl

    def rows_of(ba_rows, lo):
        return ba_rows[lo:lo + HEADS].reshape(HEADS, nchunk, 1, GDN_CHUNK)

    saved = []
    x_cur, delta, gt_prev = x0, None, None
    for i in range(nl):
        sh1, sc1, gt1, sh2, sc2, gt2 = (mod[i, k] for k in range(6))
        s = dict(gt1=gt1, gt2=gt2, sc1=sc1, sc2=sc2)
        s["x_in"], s["h"] = _resid_norm(x_cur, delta, gt_prev, norm1_g[i][None], sc1, sh1, "norm1_fwd")
        s["proj"], g_a, g_b, g_o, g_fi = _matmul(s["h"], w_pads[i], "nn", "proj_fwd", out_dtype=ACT_DTYPE,
                                                 side=gather_early(i))
        ba = _matmul(s["h"], w_pads[i][:, lay.ba:], "nn", "proj_ba_fwd")
        s["b_col"] = spatial_b[i][:, :, None]
        s["ya"] = _mixer_a_fwd(s["proj"], lay.uv, spatial_w[i], s["b_col"], v_norm_g[i][None], "mixer_a_fwd")
        s["qkv_h"] = _conv_fwd(s["proj"], conv_full[i], "conv_fwd")
        ba_rows = _cols_as_rows(ba, 0, "ba_rows")
        s["braw"], s["araw"] = rows_of(ba_rows, 0), rows_of(ba_rows, HEADS)
        s["o"], s["states"], s["t_mats"], g_fo, *g_in = _gdn_fwd(
            s["qkv_h"], s["araw"], s["braw"], a_log[i], dt_bias[i], "gdn_fwd", gather_late(i))
        if g_in:
            w_pads[i + 1] = padded_in(g_in[0])
        weights[i] = (row_full(g_a), row_full(g_b), row_full(g_o),
                      _cols_from_blocks(g_fi, fi_pieces, N_DEV * fi_shard, "w_ffn_in_cols"), row_full(g_fo))
        w_a, w_b, w_o, w_fi, w_fo = weights[i]
        s["yb"] = _gdn_post_fwd(s["o"], s["proj"], lay.z, o_norm_g[i][None], "gdn_post_fwd")
        s["pa"] = _matmul(s["ya"], w_a, "nn", "branch_a_fwd")
        s["pb"] = _matmul(s["yb"], w_b, "nn", "branch_b_fwd")
        s["merged"] = _merge_fwd(s["pa"], s["pb"], s["proj"], lay.gates, "merge_fwd")
        s["mo"] = _matmul(s["merged"], w_o, "nn", "out_fwd")
        s["x1"], s["h2"] = _resid_norm(s["x_in"], s["mo"], gt1, norm2_g[i][None], sc2, sh2, "norm2_fwd")
        s["gu"] = _matmul(s["h2"], w_fi, "nn", "ffn_in_fwd", out_dtype=ACT_DTYPE)
        s["a"] = _swiglu_fwd(s["gu"], "swiglu_fwd")
        s["fo"] = _matmul(s["a"], w_fo, "nn", "ffn_out_fwd")
        saved.append(s)
        x_cur, delta, gt_prev = s["x1"], s["fo"], gt2
    dx, d_final_g, loss_tile = _final_loss(x_cur, delta, gt_prev, final_g[None], target, "final_loss")
    loss = lax.psum(loss_tile[0, 0], ("x", "y", "c"))

    big_shapes = [(d, w_in.shape[2]), w_branch_a.shape[1:], w_branch_b.shape[1:], w_out.shape[1:],
                  (w_ffn_in.shape[2], d), w_ffn_out.shape[1:]]
    accs = [lax.empty((nl, N_DEV // 2) + tuple(sh), WIRE_DTYPE) for sh in big_shapes]
    row_blocks = lambda g: g.reshape(N_DEV, -1, g.shape[1])
    dmod, small = [None] * nl, [None] * nl
    d_conv = [None] * nl
    parts, sums = None, None
    beside_gdn, beside_dw, beside_dx = (0,), (4,), (1, 2, 3, 5)
    rep_parts = [None] * nl
    rep_pack = lambda i: _pack((dmod[i],) + small[i])

    def scatter_side(idx, layer):
        if sums is None:
            return _NoSide
        return _ReduceScatterIciSide([sums[k] for k in idx], [accs[k] for k in idx], layer)

    def scattered_into(accs, idx, new):
        accs = list(accs)
        for k, a in zip(idx, new):
            accs[k] = a
        return accs

    for i in reversed(range(nl)):
        s = saved[i]
        w_a, w_b, w_o, w_fi, w_fo = weights[i]
        if i == nl - 1:
            dfo, dgt2 = _gate_bwd(dx, s["fo"], s["gt2"], "gate2_bwd")
        else:
            dgt2 = dgt2_before
        g_fo = _matmul(s["a"], dfo, "tn", "ffn_out_dw")
        da = _matmul(dfo, w_fo, "nt", "ffn_out_dx")
        dgu = _swiglu_bwd(s["gu"], da, "swiglu_bwd")
        if parts is None:
            g_fi = _matmul(dgu, s["h2"], "tn", "ffn_in_dw")
        else:
            g_fi, *other = _matmul(dgu, s["h2"], "tn", "ffn_in_dw", side=_ReduceScatterD2dSide(parts))
            sums = [_pair_sum(p, o, core, "rs_pair_sum_%d" % k) for k, (p, o) in enumerate(zip(parts, other))]
        if i + 1 < nl:
            dh2, rep_parts[i + 1] = _matmul(dgu, w_fi, "nt", "ffn_in_dx", side=_AllGatherSide([rep_pack(i + 1)]))
        else:
            dh2 = _matmul(dgu, w_fi, "nt", "ffn_in_dx")
        dx1, dsh2, dsc2, dg2, dmo, dgt1 = _norm_bwd(s["x1"], dh2, dx, norm2_g[i][None], s["sc2"], "norm2_bwd",
                                                    gate=(s["mo"], s["gt1"]))
        g_o = _matmul(s["merged"], dmo, "tn", "out_dw")
        dmerged = _matmul(dmo, w_o, "nt", "out_dx")
        dproj = lax.empty((t, lay.width), MXU_DTYPE)
        dpa, dpb, dproj = _merge_bwd(dmerged, s["pa"], s["pb"], s["proj"], lay.gates, dproj, "merge_bwd")
        g_a = _matmul(s["ya"], dpa, "tn", "branch_a_dw")
        dya = _matmul(dpa, w_a, "nt", "branch_a_dx")
        g_b = _matmul(s["yb"], dpb, "tn", "branch_b_dw")
        dyb = _matmul(dpb, w_b, "nt", "branch_b_dx")
        dproj, d_ws, d_bs, d_gv = _mixer_a_bwd(s["proj"], lay.uv, dya, spatial_w[i], jnp.swapaxes(spatial_w[i], 1, 2),
                                               s["b_col"], v_norm_g[i][None], dproj, "mixer_a_bwd")
        do, dproj, d_go = _gdn_post_bwd(s["o"], s["proj"], lay.z, dyb, o_norm_g[i][None], dproj, "gdn_post_bwd")
        dq, dk, dv, d_ar, d_br, d_al, d_dt, *scattered = _gdn_bwd(
            s["qkv_h"], s["araw"], s["braw"], a_log[i], dt_bias[i], s["states"], s["t_mats"], do, "gdn_bwd",
            scatter_side(beside_gdn, i + 1))
        accs = scattered_into(accs, beside_gdn, scattered)
        dacc, d_conv[i] = _conv_bwd_pre(s["proj"], dq, dk, dv, conv_full[i], "conv_bwd_pre")
        dproj = _conv_bwd_in(dacc, conv_full[i], dproj, "conv_bwd_in")
        dba_rows = jnp.pad(jnp.concatenate([d_br.reshape(HEADS, t), d_ar.reshape(HEADS, t)]),
                           ((0, LANES - 2 * HEADS), (0, 0)))
        dproj = _rows_into_cols(dproj, dba_rows, lay.ba, "dproj_ba")
        if sums is None:
            g_pad = _matmul(s["h"], dproj, "tn", "proj_dw")
            dh = _matmul(dproj, w_pads[i], "nt", "proj_dx")
        else:
            g_pad, *scattered = _matmul(s["h"], dproj, "tn", "proj_dw", side=scatter_side(beside_dw, i + 1))
            accs = scattered_into(accs, beside_dw, scattered)
            dh, *scattered = _matmul(dproj, w_pads[i], "nt", "proj_dx", side=scatter_side(beside_dx, i + 1))
            accs = scattered_into(accs, beside_dx, scattered)
        if i > 0:
            dx, dsh1, dsc1, dg1, dfo, dgt2_before = _norm_bwd(s["x_in"], dh, dx1, norm1_g[i][None], s["sc1"], "norm1_bwd",
                                                              gate=(saved[i - 1]["fo"], saved[i - 1]["gt2"]))
        else:
            dx, dsh1, dsc1, dg1 = _norm_bwd(s["x_in"], dh, dx1, norm1_g[i][None], s["sc1"], "norm1_bwd")
        dmod[i] = jnp.concatenate([dsh1, dsc1, dgt1, dsh2, dsc2, dgt2], axis=1)[0]
        small[i] = (dg1[0], d_ws, d_bs[:, :, 0], d_gv[0], d_al[:, 0, 0], d_dt[:, 0, 0], d_go[0], dg2[0])
        parts = [_blocks_from_cols(g_pad, in_pieces, w_in.shape[2], "w_in_blocks"), row_blocks(g_a), row_blocks(g_b),
                 row_blocks(g_o), row_blocks(g_fi), row_blocks(g_fo)]
    other = _run_side(_ReduceScatterD2dSide(parts), "rs_d2d_last")
    sums = [_pair_sum(p, o, core, "rs_pair_sum_%d" % k) for k, (p, o) in enumerate(zip(parts, other))]
    accs = _run_side(_ReduceScatterIciSide(sums, accs, 0), "rs_ici_last")

    rep_w = (ada_b, norm1_g, spatial_w, spatial_b, v_norm_g, a_log, dt_bias, o_norm_g, norm2_g)
    rep_m = (m_ada_b, m_norm1_g, m_spatial_w, m_spatial_b, m_v_norm_g, m_a_log, m_dt_bias, m_o_norm_g, m_norm2_g)
    rep_v = (v_ada_b, v_norm1_g, v_spatial_w, v_spatial_b, v_v_norm_g, v_a_log, v_dt_bias, v_o_norm_g, v_norm2_g)
    rep_parts[0], fin_parts = _run_side(_AllGatherSide([rep_pack(0), _pack([d_final_g[0]])]), "small_grads_last")
    dmod = jnp.stack(dmod)
    d_conv_blocks = jnp.stack(d_conv).reshape(nl, CONV_K, N_DEV, -1).transpose(2, 0, 1, 3).reshape(N_DEV, -1, LANES)
    dmod_blocks = dmod.reshape(nl, N_DEV, -1).transpose(1, 0, 2)
    conv_all, dmod_all = _a2a_direct([d_conv_blocks, dmod_blocks], "small_grads_scatter")
    by_layer = lambda arrs: jnp.stack([_pack([a[i] for a in arrs]) for i in range(nl)])
    rep_out = _sum_adam(jnp.stack(rep_parts), by_layer(rep_w), by_layer(rep_m), by_layer(rep_v), "adam_small")
    fin_out = _sum_adam(fin_parts[None], _pack([final_g])[None], _pack([m_final_g])[None], _pack([v_final_g])[None],
                        "adam_final_g")
    layer_like = [a[0] for a in rep_w]
    rep_out = [[jnp.stack(per_layer) for per_layer in zip(*[_unpack(o[i], layer_like) for i in range(nl)])]
               + _unpack(f[0], [final_g]) for o, f in zip(rep_out, fin_out)]
    conv_out = _sum_adam(conv_all[None], conv_w.reshape(1, -1, LANES), m_conv_w.reshape(1, -1, LANES),
                         v_conv_w.reshape(1, -1, LANES), "adam_conv")
    conv_out = [o.reshape(conv_w.shape) for o in conv_out]
    ada_out = _ada_bwd(c_all[:, :, None], dmod_all.transpose(1, 0, 2), ada_w, m_ada_w, v_ada_w, "ada_bwd_adam")
    big_m = (m_w_in, m_w_branch_a, m_w_branch_b, m_w_out, m_w_ffn_in, m_w_ffn_out)
    big_v = (v_w_in, v_w_branch_a, v_w_branch_b, v_w_out, v_w_ffn_in, v_w_ffn_out)
    turn = lambda k, a: jnp.swapaxes(a, 1, 2) if k == 4 else a
    big_out = [[turn(k, o) for o in _sum_adam(accs[k], turn(k, big[k]), turn(k, big_m[k]), turn(k, big_v[k]),
                                             "adam_big_%d" % k)] for k in range(6)]

    def ordered(kind):
        rep = rep_out[kind]
        return (ada_out[kind], rep[0], rep[1], big_out[0][kind], conv_out[kind], rep[2], rep[3], rep[4], rep[5],
                rep[6], rep[7], big_out[1][kind], big_out[2][kind], big_out[3][kind], rep[8], big_out[4][kind],
                big_out[5][kind], rep[9])

    return (loss, dx[None]) + ordered(0) + ordered(1) + ordered(2) + ordered(3)
```

```python
import functools

import jax
import jax.numpy as jnp
from jax import lax
from jax.experimental import pallas as pl
from jax.experimental.pallas import tpu as pltpu

F32 = jnp.float32
BF16 = jnp.bfloat16
MXU_DTYPE = BF16
WIRE_DTYPE = BF16
ACT_DTYPE = BF16
EPS = 1e-6
LANES = 128
SUBLANES = 8
GDN_CHUNK = 128
A_CHUNK = 128
GROUPS = 8
HEADS = 8
HEAD_DIM = 128
CONV_K = 4
N_DEV = 8
VMEM_LIMIT = 48 * 1024 * 1024
MESH = pl.DeviceIdType.MESH

ADAM_LR = 0.001
ADAM_B1 = 0.9
ADAM_B2 = 0.999
ADAM_EPS = 1e-08
ADAM_WD = 0.01
ADAM_STEP = 10

_NN = (((1,), (0,)), ((), ()))
_NT = (((1,), (1,)), ((), ()))
_TN = (((0,), (0,)), ((), ()))


def _mm(a, b, dims=_NN):
    return lax.dot_general(a.astype(MXU_DTYPE), b.astype(MXU_DTYPE), dims, preferred_element_type=F32)


def _mm_hi(a, b):
    return lax.dot_general(a, b, _NN, precision=lax.Precision.HIGHEST, preferred_element_type=F32)


def _tile(n, cands):
    for c in cands:
        if n % c == 0:
            return c
    return n


def _params(sem=None):
    return pltpu.CompilerParams(dimension_semantics=sem, vmem_limit_bytes=VMEM_LIMIT)


def _sigmoid(x):
    return 1.0 / (1.0 + jnp.exp(-x))


def _silu(x):
    return x * _sigmoid(x)


_GELU_C = 0.7978845608028654
_GELU_A = 0.044715


def _gelu(x):
    return 0.5 * x * (1.0 + jnp.tanh(_GELU_C * (x + _GELU_A * x * x * x)))


def _gelu_and_slope(x):
    t = jnp.tanh(_GELU_C * (x + _GELU_A * x * x * x))
    return 0.5 * x * (1.0 + t), 0.5 * (1.0 + t) + 0.5 * x * (1.0 - t * t) * _GELU_C * (1.0 + 3.0 * _GELU_A * x * x)


def _softplus(x):
    return jnp.maximum(x, 0.0) + jnp.log(1.0 + jnp.exp(-jnp.abs(x)))


_MM_TILES = (1024, 1408, 1664, 512, 256, 128)


class _NoSide:
    operands, out_shape, scratch, aliases, n_in, n_out = [], [], [], {}, 0, 0


def _side_hooks(side, refs, n_main_in, n_main_out, n_main_scratch, grid):
    a = n_main_in + side.n_in
    b = a + n_main_out + side.n_out
    ins, outs, sems = refs[n_main_in:a], refs[a + n_main_out:b], refs[b + n_main_scratch:]
    main = refs[:n_main_in] + refs[a:a + n_main_out] + refs[b:b + n_main_scratch]
    ids = [pl.program_id(k) for k in range(len(grid))]

    def start():
        if side.n_in:
            pl.when(functools.reduce(jnp.logical_and, [i == 0 for i in ids]))(lambda: side.start(ins, outs, sems))

    def finish():
        if side.n_in:
            last = functools.reduce(jnp.logical_and, [i == g - 1 for i, g in zip(ids, grid)])
            pl.when(last)(lambda: side.finish(ins, outs, sems))

    return main, start, finish


def _carrier_call(body, name, grid, in_specs, out_specs, out_shape, scratch, side, args):
    aliases = {len(in_specs) + k: len(out_specs) + v for k, v in side.aliases.items()}
    return pl.pallas_call(
        body, name=name, grid=grid, in_specs=list(in_specs) + [_ANY] * side.n_in,
        out_specs=list(out_specs) + [_ANY] * side.n_out, out_shape=list(out_shape) + list(side.out_shape),
        scratch_shapes=list(scratch) + list(side.scratch), input_output_aliases=aliases,
        compiler_params=_params(("arbitrary",) * len(grid)))(*args, *side.operands)


_MM_VMEM_BUDGET = 44 * 1024 * 1024


def _matmul_tiles(mode, m, n, k, out_bytes):
    tk = _tile(k, _MM_TILES)
    tm = _tile(m, _MM_TILES)
    in_bytes = jnp.dtype(MXU_DTYPE).itemsize
    for tn in _MM_TILES:
        if n % tn:
            continue
        need = 2 * in_bytes * (tm * tk + tk * tn) + tm * tn * (2 * out_bytes + (4 if k > tk else 0))
        if need <= _MM_VMEM_BUDGET:
            return tm, tn, tk
    return tm, _tile(n, (LANES,)), tk


def _matmul(a, b, mode, name, out_dtype=F32, side=_NoSide):
    if mode == "nn":
        (m, k), n = a.shape, b.shape[1]
    elif mode == "nt":
        (m, k), n = a.shape, b.shape[0]
    else:
        (k, m), n = a.shape, b.shape[1]
    tm, tn, tk = _matmul_tiles(mode, m, n, k, jnp.dtype(out_dtype).itemsize)
    nk = k // tk
    grid = (m // tm, n // tn, nk)
    dims = {"nn": _NN, "nt": _NT, "tn": _TN}[mode]

    def body(*refs):
        (a_ref, b_ref, o_ref, acc_ref), side_start, side_finish = _side_hooks(side, refs, 2, 1, 1, grid)
        kk = pl.program_id(2)
        side_start()
        if nk == 1:
            o_ref[...] = _mm(a_ref[...], b_ref[...], dims).astype(o_ref.dtype)
        else:
            @pl.when(kk == 0)
            def _():
                acc_ref[...] = _mm(a_ref[...], b_ref[...], dims)

            @pl.when(jnp.logical_and(kk > 0, kk < nk - 1))
            def _():
                acc_ref[...] += _mm(a_ref[...], b_ref[...], dims)

            @pl.when(kk == nk - 1)
            def _():
                o_ref[...] = (acc_ref[...] + _mm(a_ref[...], b_ref[...], dims)).astype(o_ref.dtype)

        side_finish()

    a_spec = (pl.BlockSpec((tk, tm), lambda i, j, l: (l, i)) if mode == "tn"
              else pl.BlockSpec((tm, tk), lambda i, j, l: (i, l)))
    b_spec = (pl.BlockSpec((tn, tk), lambda i, j, l: (j, l)) if mode == "nt"
              else pl.BlockSpec((tk, tn), lambda i, j, l: (l, j)))
    o_spec = pl.BlockSpec((tm, tn), lambda i, j, l: (i, j))
    out = _carrier_call(body, name, grid, [a_spec, b_spec], [o_spec], [jax.ShapeDtypeStruct((m, n), out_dtype)],
                        [pltpu.VMEM((tm, tn) if nk > 1 else (SUBLANES, LANES), F32)], side, (a, b))
    return out if side.n_in else out[0]


_ROW_TILES = (512, 256, 128)


def _resid_norm(x, delta, gt, g, sc, sh, name):
    t, d = x.shape
    tt = _tile(t, _ROW_TILES)
    has = delta is not None

    def body(*refs):
        if has:
            x_ref, d_ref, gt_ref, g_ref, sc_ref, sh_ref, xo_ref, h_ref = refs
            xv = x_ref[...] + gt_ref[...] * d_ref[...]
            xo_ref[...] = xv
        else:
            x_ref, g_ref, sc_ref, sh_ref, h_ref = refs
            xv = x_ref[...]
        r = lax.rsqrt(jnp.mean(xv * xv, axis=-1, keepdims=True) + EPS)
        y = xv * r * g_ref[...]
        h_ref[...] = (y * (1.0 + sc_ref[...]) + sh_ref[...]).astype(h_ref.dtype)

    row = pl.BlockSpec((tt, d), lambda i: (i, 0))
    vec = pl.BlockSpec((1, d), lambda i: (0, 0))
    if has:
        return pl.pallas_call(
            body, name=name, grid=(t // tt,), in_specs=[row, row, vec, vec, vec, vec], out_specs=[row, row],
            out_shape=[jax.ShapeDtypeStruct((t, d), F32), jax.ShapeDtypeStruct((t, d), MXU_DTYPE)],
            compiler_params=_params(("parallel",)))(x, delta, gt, g, sc, sh)
    h = pl.pallas_call(
        body, name=name + "_first", grid=(t // tt,), in_specs=[row, vec, vec, vec], out_specs=row,
        out_shape=jax.ShapeDtypeStruct((t, d), MXU_DTYPE), compiler_params=_params(("parallel",)))(x, g, sc, sh)
    return x, h


def _final_loss(x, delta, gt, g, target, name):
    t, d = x.shape
    tt = _tile(t, _ROW_TILES)

    def body(x_ref, d_ref, gt_ref, g_ref, tg_ref, dx_ref, dg_ref, loss_ref):
        @pl.when(pl.program_id(0) == 0)
        def _():
            dg_ref[...] = jnp.zeros_like(dg_ref)
            loss_ref[...] = jnp.zeros_like(loss_ref)

        xv = x_ref[...] + gt_ref[...] * d_ref[...]
        r = lax.rsqrt(jnp.mean(xv * xv, axis=-1, keepdims=True) + EPS)
        xh = xv * r
        diff = xh * g_ref[...] - tg_ref[...]
        loss_ref[...] += jnp.sum(diff * diff) * (0.5 / d)
        dy = diff * (1.0 / d)
        dg_ref[...] += jnp.sum(dy * xh, axis=0, keepdims=True)
        dxh = dy * g_ref[...]
        dx_ref[...] = r * (dxh - xh * jnp.mean(dxh * xh, axis=-1, keepdims=True))

    row = pl.BlockSpec((tt, d), lambda i: (i, 0))
    vec = pl.BlockSpec((1, d), lambda i: (0, 0))
    tile = pl.BlockSpec((SUBLANES, LANES), lambda i: (0, 0))
    return pl.pallas_call(
        body, name=name, grid=(t // tt,), in_specs=[row, row, vec, vec, row], out_specs=[row, vec, tile],
        out_shape=[jax.ShapeDtypeStruct((t, d), F32), jax.ShapeDtypeStruct((1, d), F32),
                   jax.ShapeDtypeStruct((SUBLANES, LANES), F32)],
        compiler_params=_params(("arbitrary",)))(x, delta, gt, g, target)


def _norm_bwd(x, dh, dres, g, sc, name, gate=None):
    t, d = x.shape
    tt = _tile(t, _ROW_TILES)
    gated = gate is not None

    def body(*refs):
        x_ref, dh_ref, dr_ref, g_ref, sc_ref = refs[:5]
        dx_ref, dsh_ref, dsc_ref, dg_ref = refs[5 + 2 * gated:9 + 2 * gated]

        @pl.when(pl.program_id(0) == 0)
        def _():
            for acc_ref in refs[6 + 2 * gated:9 + 2 * gated] + refs[10 + 2 * gated:]:
                acc_ref[...] = jnp.zeros_like(acc_ref)

        xv, dh = x_ref[...], dh_ref[...]
        r = lax.rsqrt(jnp.mean(xv * xv, axis=-1, keepdims=True) + EPS)
        xh = xv * r
        gv, sc1 = g_ref[...], 1.0 + sc_ref[...]
        dsh_ref[...] += jnp.sum(dh, axis=0, keepdims=True)
        dsc_ref[...] += jnp.sum(dh * xh, axis=0, keepdims=True) * gv
        dg_ref[...] += jnp.sum(dh * xh, axis=0, keepdims=True) * sc1
        dxh = dh * (gv * sc1)
        dx = dr_ref[...] + r * (dxh - xh * jnp.mean(dxh * xh, axis=-1, keepdims=True))
        dx_ref[...] = dx
        if gated:
            br_ref, gt_ref, db_ref, dgt_ref = refs[5], refs[6], refs[11], refs[12]
            db_ref[...] = (dx * gt_ref[...]).astype(db_ref.dtype)
            dgt_ref[...] += jnp.sum(dx * br_ref[...], axis=0, keepdims=True)

    row = pl.BlockSpec((tt, d), lambda i: (i, 0))
    vec = pl.BlockSpec((1, d), lambda i: (0, 0))
    vshape = jax.ShapeDtypeStruct((1, d), F32)
    in_specs, out_specs = [row, row, row, vec, vec], [row, vec, vec, vec]
    out_shape = [jax.ShapeDtypeStruct((t, d), F32), vshape, vshape, vshape]
    if gated:
        in_specs, out_specs = in_specs + [row, vec], out_specs + [row, vec]
        out_shape = out_shape + [jax.ShapeDtypeStruct((t, d), MXU_DTYPE), vshape]
    return pl.pallas_call(
        body, name=name + ("_gate" if gated else ""), grid=(t // tt,), in_specs=in_specs, out_specs=out_specs,
        out_shape=out_shape, compiler_params=_params(("arbitrary",)))(x, dh, dres, g, sc, *(gate or ()))


def _gate_bwd(dxo, branch, gt, name):
    t, d = dxo.shape
    tt = _tile(t, _ROW_TILES)

    def body(dx_ref, br_ref, gt_ref, db_ref, dgt_ref):
        @pl.when(pl.program_id(0) == 0)
        def _():
            dgt_ref[...] = jnp.zeros_like(dgt_ref)

        dx = dx_ref[...]
        db_ref[...] = (dx * gt_ref[...]).astype(db_ref.dtype)
        dgt_ref[...] += jnp.sum(dx * br_ref[...], axis=0, keepdims=True)

    row = pl.BlockSpec((tt, d), lambda i: (i, 0))
    vec = pl.BlockSpec((1, d), lambda i: (0, 0))
    return pl.pallas_call(
        body, name=name, grid=(t // tt,), in_specs=[row, row, vec], out_specs=[row, vec],
        out_shape=[jax.ShapeDtypeStruct((t, d), MXU_DTYPE), jax.ShapeDtypeStruct((1, d), F32)],
        compiler_params=_params(("arbitrary",)))(dxo, branch, gt)


def _ffn_in_swiglu(h, w, name):
    t, k = h.shape
    f = w.shape[1] // 2
    tm, tn = _tile(t, _MM_TILES), _tile(f, _MM_TILES)
    nj = f // tn

    def body(h_ref, wg_ref, wu_ref, gu_ref, a_ref):
        hv = h_ref[...]
        gu_ref[0] = _mm(hv, wg_ref[...]).astype(gu_ref.dtype)
        gu_ref[1] = _mm(hv, wu_ref[...]).astype(gu_ref.dtype)
        a_ref[...] = (_silu(gu_ref[0].astype(F32)) * gu_ref[1].astype(F32)).astype(a_ref.dtype)

    return pl.pallas_call(
        body, name=name, grid=(t // tm, nj),
        in_specs=[pl.BlockSpec((tm, k), lambda i, j: (i, 0)), pl.BlockSpec((k, tn), lambda i, j: (0, j)),
                  pl.BlockSpec((k, tn), lambda i, j: (0, j + nj))],
        out_specs=[pl.BlockSpec((2, tm, tn), lambda i, j: (0, i, j)), pl.BlockSpec((tm, tn), lambda i, j: (i, j))],
        out_shape=[jax.ShapeDtypeStruct((2, t, f), ACT_DTYPE), jax.ShapeDtypeStruct((t, f), MXU_DTYPE)],
        compiler_params=_params(("parallel", "parallel")))(h, w, w)


def _swiglu_bwd(gu, da, name):
    _, t, f = gu.shape
    f2 = 2 * f
    tt = _tile(t, (256, 128))

    def body(g_ref, u_ref, da_ref, o_ref):
        gate, da = g_ref[...].astype(F32), da_ref[...]
        sg = _sigmoid(gate)
        o_ref[:, :f] = (da * u_ref[...].astype(F32) * (sg * (1.0 + gate * (1.0 - sg)))).astype(o_ref.dtype)
        o_ref[:, f:] = (da * (gate * sg)).astype(o_ref.dtype)

    return pl.pallas_call(
        body, name=name, grid=(t // tt,),
        in_specs=[pl.BlockSpec((None, tt, f), lambda i: (0, i, 0)), pl.BlockSpec((None, tt, f), lambda i: (1, i, 0)),
                  pl.BlockSpec((tt, f), lambda i: (i, 0))],
        out_specs=pl.BlockSpec((tt, f2), lambda i: (i, 0)), out_shape=jax.ShapeDtypeStruct((t, f2), MXU_DTYPE),
        compiler_params=_params(("parallel",)))(gu, gu, da)


class _ProjLayout:
    def __init__(self, d):
        wc = 3 * HEADS * HEAD_DIM
        self.d, self.wc = d, wc
        self.qkv, self.z, self.uv, self.gates, self.ba = 0, wc, wc + d, wc + 3 * d, wc + 5 * d
        self.width = self.ba + LANES
        assert self.z % d == 0 and self.uv % (2 * d) == 0 and self.gates % (2 * d) == 0 and self.ba % LANES == 0

    def pieces(self, shard):
        d, wc, out, lo = self.d, self.wc, [], 0
        for length, dst in ((2 * d, self.uv), (wc, self.qkv), (d, self.z), (2 * HEADS, self.ba), (2 * d, self.gates)):
            pos = lo
            while pos < lo + length:
                j = pos // shard
                n = min(lo + length, (j + 1) * shard) - pos
                out.append((j, pos - j * shard, n, dst + pos - lo))
                pos += n
            lo += length
        return out


def _merge_fwd(pa, pb, proj, gcol, name):
    t, d = pa.shape
    tt = _tile(t, _ROW_TILES)

    def body(pa_ref, pb_ref, ga_ref, gb_ref, o_ref):
        sa, sb = _sigmoid(ga_ref[...].astype(F32)), _sigmoid(gb_ref[...].astype(F32))
        o_ref[...] = (sa * pa_ref[...] + sb * pb_ref[...]).astype(o_ref.dtype)

    row = pl.BlockSpec((tt, d), lambda i: (i, 0))
    gate = lambda k: pl.BlockSpec((tt, d), lambda i: (i, gcol // d + k))
    return pl.pallas_call(
        body, name=name, grid=(t // tt,), in_specs=[row, row, gate(0), gate(1)], out_specs=row,
        out_shape=jax.ShapeDtypeStruct((t, d), MXU_DTYPE), compiler_params=_params(("parallel",)))(pa, pb, proj, proj)


def _merge_bwd(dm, pa, pb, proj, gcol, dproj, name):
    t, d = pa.shape
    tt = _tile(t, _ROW_TILES)

    def body(dm_ref, pa_ref, pb_ref, ga_ref, gb_ref, _, dpa_ref, dpb_ref, dg_ref):
        dm = dm_ref[...]
        sa, sb = _sigmoid(ga_ref[...].astype(F32)), _sigmoid(gb_ref[...].astype(F32))
        dpa_ref[...] = (dm * sa).astype(dpa_ref.dtype)
        dpb_ref[...] = (dm * sb).astype(dpb_ref.dtype)
        dg_ref[:, :d] = (dm * pa_ref[...] * sa * (1.0 - sa)).astype(dg_ref.dtype)
        dg_ref[:, d:] = (dm * pb_ref[...] * sb * (1.0 - sb)).astype(dg_ref.dtype)

    row = pl.BlockSpec((tt, d), lambda i: (i, 0))
    gate = lambda k: pl.BlockSpec((tt, d), lambda i: (i, gcol // d + k))
    wide = pl.BlockSpec((tt, 2 * d), lambda i: (i, gcol // (2 * d)))
    return pl.pallas_call(
        body, name=name, grid=(t // tt,), in_specs=[row, row, row, gate(0), gate(1), _ANY], out_specs=[row, row, wide],
        out_shape=[jax.ShapeDtypeStruct((t, d), MXU_DTYPE), jax.ShapeDtypeStruct((t, d), MXU_DTYPE),
                   jax.ShapeDtypeStruct(dproj.shape, dproj.dtype)],
        input_output_aliases={5: 2}, compiler_params=_params(("parallel",)))(dm, pa, pb, proj, proj, dproj)


def _tri_masks(n):
    ri = lax.broadcasted_iota(jnp.int32, (n, n), 0)
    ci = lax.broadcasted_iota(jnp.int32, (n, n), 1)
    return ri >= ci, ri > ci, ri == ci


def _mixer_a_fwd(proj, ucol, w_s, b_col, g_v, name):
    t, w = proj.shape[0], g_v.shape[1]
    c = A_CHUNK

    def body(u_ref, v_ref, w_ref, b_ref, gv_ref, y_ref):
        tril, _, _ = _tri_masks(c)
        ug, vg = _gelu(u_ref[...].astype(F32)), _gelu(v_ref[...].astype(F32))
        for g in range(GROUPS):
            sl = slice(g * c, (g + 1) * c)
            vt = vg[:, sl]
            r = lax.rsqrt(jnp.mean(vt * vt, axis=-1, keepdims=True) + EPS)
            vn = vt * r * gv_ref[:, sl]
            s = _mm(jnp.where(tril, w_ref[g], 0.0), vn) + b_ref[g]
            y_ref[:, sl] = (ug[:, sl] * s).astype(y_ref.dtype)

    return pl.pallas_call(
        body, name=name, grid=(t // c,),
        in_specs=[pl.BlockSpec((c, w), lambda i: (i, ucol // w)), pl.BlockSpec((c, w), lambda i: (i, ucol // w + 1)),
                  pl.BlockSpec((GROUPS, c, c), lambda i: (0, 0, 0)), pl.BlockSpec((GROUPS, c, 1), lambda i: (0, 0, 0)),
                  pl.BlockSpec((1, w), lambda i: (0, 0))],
        out_specs=pl.BlockSpec((c, w), lambda i: (i, 0)), out_shape=jax.ShapeDtypeStruct((t, w), MXU_DTYPE),
        compiler_params=_params(("parallel",)))(proj, proj, w_s, b_col, g_v)


def _mixer_a_bwd(proj, ucol, dy, w_s, w_st, b_col, g_v, dproj, name):
    t, w = proj.shape[0], g_v.shape[1]
    w2 = 2 * w
    c = A_CHUNK

    def body(u_ref, v_ref, dy_ref, w_ref, wt_ref, b_ref, gv_ref, _, duv_ref, dw_ref, db_ref, dgv_ref):
        @pl.when(pl.program_id(0) == 0)
        def _():
            dw_ref[...] = jnp.zeros_like(dw_ref)
            db_ref[...] = jnp.zeros_like(db_ref)
            dgv_ref[...] = jnp.zeros_like(dgv_ref)

        tril, _, _ = _tri_masks(c)
        triu = lax.broadcasted_iota(jnp.int32, (c, c), 0) <= lax.broadcasted_iota(jnp.int32, (c, c), 1)
        (ug, dug), (vg, dvg) = _gelu_and_slope(u_ref[...].astype(F32)), _gelu_and_slope(v_ref[...].astype(F32))
        for g in range(GROUPS):
            sl = slice(g * c, (g + 1) * c)
            vt = vg[:, sl]
            r = lax.rsqrt(jnp.mean(vt * vt, axis=-1, keepdims=True) + EPS)
            vh = vt * r
            gv = gv_ref[:, sl]
            vn = vh * gv
            s = _mm(jnp.where(tril, w_ref[g], 0.0), vn) + b_ref[g]
            dy = dy_ref[:, sl]
            ds = dy * ug[:, sl]
            dw_ref[g] += jnp.where(tril, _mm(ds, vn, _NT), 0.0)
            db_ref[g] += jnp.sum(ds, axis=1, keepdims=True)
            dvn = _mm(jnp.where(triu, wt_ref[g], 0.0), ds)
            dgv_ref[:, sl] += jnp.sum(dvn * vh, axis=0, keepdims=True)
            dvh = dvn * gv
            dvt = r * (dvh - vh * jnp.mean(dvh * vh, axis=-1, keepdims=True))
            duv_ref[:, sl] = (dy * s * dug[:, sl]).astype(duv_ref.dtype)
            duv_ref[:, w + g * c:w + (g + 1) * c] = (dvt * dvg[:, sl]).astype(duv_ref.dtype)

    full3 = lambda shape: pl.BlockSpec(shape, lambda i: (0, 0, 0))
    return pl.pallas_call(
        body, name=name, grid=(t // c,),
        in_specs=[pl.BlockSpec((c, w), lambda i: (i, ucol // w)), pl.BlockSpec((c, w), lambda i: (i, ucol // w + 1)),
                  pl.BlockSpec((c, w), lambda i: (i, 0)), full3((GROUPS, c, c)), full3((GROUPS, c, c)),
                  full3((GROUPS, c, 1)), pl.BlockSpec((1, w), lambda i: (0, 0)), _ANY],
        out_specs=[pl.BlockSpec((c, w2), lambda i: (i, ucol // w2)), full3((GROUPS, c, c)), full3((GROUPS, c, 1)),
                   pl.BlockSpec((1, w), lambda i: (0, 0))],
        out_shape=[jax.ShapeDtypeStruct(dproj.shape, dproj.dtype), jax.ShapeDtypeStruct((GROUPS, c, c), F32),
                   jax.ShapeDtypeStruct((GROUPS, c, 1), F32), jax.ShapeDtypeStruct((1, w), F32)],
        input_output_aliases={7: 0},
        compiler_params=_params(("arbitrary",)))(proj, proj, dy, w_s, w_st, b_col, g_v, dproj)


_Q_SCALE = HEAD_DIM ** -0.5


CONV_HALO = 16


def _conv_taps(x_ref, p_ref, w_ref):
    prev = jnp.where(pl.program_id(0) > 0, p_ref[...].astype(F32), 0.0)
    ext = jnp.concatenate([prev, x_ref[...].astype(F32)], axis=0)
    shifted = [ext[CONV_HALO:]] + [pltpu.roll(ext, s, 0)[CONV_HALO:] for s in range(1, CONV_K)]
    acc = shifted[0] * w_ref[pl.ds(CONV_K - 1, 1), :]
    for s in range(1, CONV_K):
        acc = acc + shifted[s] * w_ref[pl.ds(CONV_K - 1 - s, 1), :]
    return acc, shifted


def _conv_fwd(qkv, w, name):
    t, cw = qkv.shape[0], w.shape[1]
    tt = _tile(t, (256, 128))
    hb = tt // CONV_HALO

    def body(x_ref, p_ref, w_ref, o_ref):
        acc, _ = _conv_taps(x_ref, p_ref, w_ref)
        y = _silu(acc)
        for which in range(3):
            for h in range(HEADS):
                lo = (which * HEADS + h) * HEAD_DIM
                seg = y[:, lo:lo + HEAD_DIM]
                if which < 2:
                    seg = seg * lax.rsqrt(jnp.sum(seg * seg, axis=-1, keepdims=True) + EPS)
                if which == 0:
                    seg = seg * _Q_SCALE
                o_ref[which, h] = seg

    return pl.pallas_call(
        body, name=name, grid=(t // tt,),
        in_specs=[pl.BlockSpec((tt, cw), lambda i: (i, 0)),
                  pl.BlockSpec((CONV_HALO, cw), lambda i: (jnp.maximum(i * hb - 1, 0), 0)),
                  pl.BlockSpec((CONV_K, cw), lambda i: (0, 0))],
        out_specs=pl.BlockSpec((3, HEADS, tt, HEAD_DIM), lambda i: (0, 0, i, 0)),
        out_shape=jax.ShapeDtypeStruct((3, HEADS, t, HEAD_DIM), F32),
        compiler_params=_params(("parallel",)))(qkv, qkv, w)


def _conv_bwd_pre(qkv, dq, dk, dv, w, name):
    t, cw = qkv.shape[0], w.shape[1]
    tt = _tile(t, (256, 128))
    hb = tt // CONV_HALO

    def body(x_ref, p_ref, dq_ref, dk_ref, dv_ref, w_ref, da_ref, dw_ref):
        @pl.when(pl.program_id(0) == 0)
        def _():
            dw_ref[...] = jnp.zeros_like(dw_ref)

        acc, shifted = _conv_taps(x_ref, p_ref, w_ref)
        sg = _sigmoid(acc)
        y = acc * sg
        dsilu = sg * (1.0 + acc * (1.0 - sg))
        d_refs = (dq_ref, dk_ref, dv_ref)
        for which in range(3):
            for h in range(HEADS):
                lo = (which * HEADS + h) * HEAD_DIM
                sl = slice(lo, lo + HEAD_DIM)
                dn = d_refs[which][h]
                if which < 2:
                    seg = y[:, sl]
                    rho = lax.rsqrt(jnp.sum(seg * seg, axis=-1, keepdims=True) + EPS)
                    nrm = seg * rho
                    if which == 0:
                        dn = dn * _Q_SCALE
                    dn = rho * (dn - nrm * jnp.sum(dn * nrm, axis=-1, keepdims=True))
                dacc = dn * dsilu[:, sl]
                da_ref[:, sl] = dacc
                for s in range(CONV_K):
                    dw_ref[pl.ds(CONV_K - 1 - s, 1), sl] += jnp.sum(dacc * shifted[s][:, sl], axis=0, keepdims=True)

    head = pl.BlockSpec((HEADS, tt, HEAD_DIM), lambda i: (0, i, 0))
    return pl.pallas_call(
        body, name=name, grid=(t // tt,),
        in_specs=[pl.BlockSpec((tt, cw), lambda i: (i, 0)),
                  pl.BlockSpec((CONV_HALO, cw), lambda i: (jnp.maximum(i * hb - 1, 0), 0)),
                  head, head, head, pl.BlockSpec((CONV_K, cw), lambda i: (0, 0))],
        out_specs=[pl.BlockSpec((tt, cw), lambda i: (i, 0)), pl.BlockSpec((CONV_K, cw), lambda i: (0, 0))],
        out_shape=[jax.ShapeDtypeStruct((t, cw), F32), jax.ShapeDtypeStruct((CONV_K, cw), F32)],
        compiler_params=_params(("arbitrary",)))(qkv, qkv, dq, dk, dv, w)


def _conv_bwd_in(dacc, w, dproj, name):
    t, cw = dacc.shape
    tt = _tile(t, (256, 128))
    hb = tt // SUBLANES
    nt = t // tt
    rows = tt + SUBLANES

    def body(d_ref, n_ref, w_ref, _, o_ref):
        cur = d_ref[...]
        nxt = jnp.where(pl.program_id(0) < nt - 1, n_ref[...], 0.0)
        ext = jnp.concatenate([cur, nxt], axis=0)
        acc = cur * w_ref[pl.ds(CONV_K - 1, 1), :]
        for s in range(1, CONV_K):
            acc = acc + pltpu.roll(ext, rows - s, 0)[:tt] * w_ref[pl.ds(CONV_K - 1 - s, 1), :]
        o_ref[...] = acc.astype(o_ref.dtype)

    return pl.pallas_call(
        body, name=name, grid=(nt,),
        in_specs=[pl.BlockSpec((tt, cw), lambda i: (i, 0)),
                  pl.BlockSpec((SUBLANES, cw), lambda i: (jnp.minimum((i + 1) * hb, t // SUBLANES - 1), 0)),
                  pl.BlockSpec((CONV_K, cw), lambda i: (0, 0)), _ANY],
        out_specs=pl.BlockSpec((tt, cw), lambda i: (i, 0)), out_shape=jax.ShapeDtypeStruct(dproj.shape, dproj.dtype),
        input_output_aliases={3: 0}, compiler_params=_params(("parallel",)))(dacc, dacc, w, dproj)


_INV_BASE_SHIFT = 3


def _inv_unit_lower(a, eye):
    c = GDN_CHUNK
    ri = lax.broadcasted_iota(jnp.int32, (c, c), 0)
    ci = lax.broadcasted_iota(jnp.int32, (c, c), 1)
    same = lambda sh: (ri >> sh) == (ci >> sh)
    x = jnp.where(same(_INV_BASE_SHIFT), -a, 0.0)
    p = jnp.where(eye, 1.0, 0.0) + x
    xs = _split(x)
    x2 = _mm3(xs, xs)
    x2s, ps = _split(x2), _split(p)
    r = _mm3(x2s, tuple(jnp.concatenate([u, v], axis=-1) for u, v in zip(x2s, ps)))
    x4, p = r[..., :c], p + r[..., c:]
    p = p + _mm3(_split(x4), _split(p))
    for sh in range(_INV_BASE_SHIFT, c.bit_length() - 1):
        off = jnp.where(same(sh + 1) & jnp.logical_not(same(sh)), a, 0.0)
        ps = _split(p)
        p = p - _mm3(ps, _split(_mm3(_split(off), ps)))
    return p


def _split(a):
    hi = a.astype(BF16)
    return hi, (a - hi.astype(F32)).astype(BF16)


def _dot_heads(u, v, dims):
    if u.ndim == 3:
        return jnp.stack([_dot_heads(u[j], v[j], dims) for j in range(u.shape[0])])
    return lax.dot_general(u, v, dims, preferred_element_type=F32)


def _mm3(a, b):
    return _dot_heads(a[0], b[0], _NN) + (_dot_heads(a[0], b[1], _NN) + _dot_heads(a[1], b[0], _NN))


def _hmm(a, b, dims=_NN):
    return _dot_heads(a.astype(MXU_DTYPE), b.astype(MXU_DTYPE), dims)


def _rowsum(x):
    return jnp.sum(x, axis=-1, keepdims=True)


def _colsum(x):
    return jnp.sum(x, axis=-2, keepdims=True)


class _Pre:
    pass


def _gdn_pre(q, k, v, araw, braw, alog, dtb, t_mat=None):
    c = GDN_CHUNK
    p = _Pre()
    p.tril, p.strict, p.eye = _tri_masks(c)
    p.to_col = lambda row: _rowsum(jnp.where(p.eye, row, 0.0))
    p.to_row = lambda col: _colsum(jnp.where(p.eye, col, 0.0))
    p.a_neg = -jnp.exp(alog + jnp.zeros((1, c), F32))
    p.xg = araw + dtb
    p.g_row = p.a_neg * _softplus(p.xg)
    p.beta_row = _sigmoid(braw)
    p.beta = p.to_col(p.beta_row)
    gam = _rowsum(jnp.where(p.tril, p.g_row, 0.0))
    gam_last = _rowsum(p.g_row)
    p.dm = jnp.where(p.tril, jnp.exp(jnp.where(p.tril, gam - p.to_row(gam), 0.0)), 0.0)
    p.e, p.ek, p.el = jnp.exp(gam), jnp.exp(gam_last - gam), jnp.exp(gam_last)
    p.kb = k * p.beta
    p.kk = _hmm(p.kb, k, _NT)
    p.t = _inv_unit_lower(jnp.where(p.strict, p.kk * p.dm, 0.0), p.eye) if t_mat is None else t_mat
    p.vb, p.kbe = v * p.beta, p.kb * p.e
    uw = _hmm(p.t, jnp.concatenate([p.vb, p.kbe], axis=-1))
    p.u, p.w = uw[..., :v.shape[-1]], uw[..., v.shape[-1]:]
    p.qk0 = _hmm(q, k, _NT)
    p.qk = p.qk0 * p.dm
    p.qd, p.kd = q * p.e, k * p.ek
    return p


GDN_HEADS_PER_STEP = 8


def _head_scalars(ref, hb):
    h0 = pl.program_id(0) * hb
    return jnp.stack([jnp.full((1, 1), ref[h0 + j], F32) for j in range(hb)])


def _gdn_specs(n, reverse):
    c, dk, hb = GDN_CHUNK, HEAD_DIM, GDN_HEADS_PER_STEP
    ix = (lambda i: n - 1 - i) if reverse else (lambda i: i)
    smem = pl.BlockSpec(memory_space=pltpu.SMEM)
    qkv = [pl.BlockSpec((None, hb, c, dk), functools.partial(lambda w, h, i: (w, h, ix(i), 0), w)) for w in range(3)]
    row = pl.BlockSpec((hb, None, 1, c), lambda h, i: (h, ix(i), 0, 0))
    tok = pl.BlockSpec((hb, c, dk), lambda h, i: (h, ix(i), 0))
    state = pl.BlockSpec((hb, None, dk, dk), lambda h, i: (h, ix(i), 0, 0))
    return smem, qkv, row, tok, state


def _gdn_fwd(qkv_h, araw, braw, alog, dtb, name, side=_NoSide):
    _, hh, t, dk = qkv_h.shape
    n, hb = t // GDN_CHUNK, GDN_HEADS_PER_STEP
    smem, qkv, row, tok, state = _gdn_specs(n, False)
    grid = (hh // hb, n)

    def body(*refs):
        main, side_start, side_finish = _side_hooks(side, refs, 7, 3, 1, grid)
        alog_ref, dt_ref, q_ref, k_ref, v_ref, a_ref, b_ref, o_ref, so_ref, to_ref, s_ref = main
        side_start()

        @pl.when(pl.program_id(1) == 0)
        def _():
            s_ref[...] = jnp.zeros_like(s_ref)

        p = _gdn_pre(q_ref[...], k_ref[...], v_ref[...], a_ref[...], b_ref[...],
                     _head_scalars(alog_ref, hb), _head_scalars(dt_ref, hb))
        s = s_ref[...]
        vn = p.u - _hmm(p.w, s)
        o_ref[...] = _hmm(p.qd, s) + _hmm(p.qk, vn)
        so_ref[...] = s
        to_ref[...] = p.t
        s_ref[...] = s * p.el + _hmm(p.kd, vn, _TN)
        side_finish()

    mats = jax.ShapeDtypeStruct((hh, n, dk, dk), F32)
    return _carrier_call(
        body, name, grid, [smem, smem] + qkv + [row, row], [tok, state, state],
        [jax.ShapeDtypeStruct((hh, t, dk), F32), mats, mats],
        [pltpu.VMEM((hb, dk, dk), F32)], side, (alog, dtb, qkv_h, qkv_h, qkv_h, araw, braw))


def _gdn_bwd(qkv_h, araw, braw, alog, dtb, states, t_mats, do, name, side=_NoSide):
    _, hh, t, dk = qkv_h.shape
    c, hb = GDN_CHUNK, GDN_HEADS_PER_STEP
    n = t // c
    smem, qkv, row, tok, state = _gdn_specs(n, True)
    acc = pl.BlockSpec((hb, 1, LANES), lambda h, i: (h, 0, 0))
    grid = (hh // hb, n)

    def body(*refs):
        main, side_start, side_finish = _side_hooks(side, refs, 10, 7, 1, grid)
        (alog_ref, dt_ref, q_ref, k_ref, v_ref, a_ref, b_ref, s_ref, t_ref, do_ref,
         dq_ref, dk_ref, dv_ref, da_ref, db_ref, dal_ref, ddt_ref, ds_ref) = main
        side_start()

        @pl.when(pl.program_id(1) == 0)
        def _():
            ds_ref[...] = jnp.zeros_like(ds_ref)
            dal_ref[...] = jnp.zeros_like(dal_ref)
            ddt_ref[...] = jnp.zeros_like(ddt_ref)

        q, k, v = q_ref[...], k_ref[...], v_ref[...]
        p = _gdn_pre(q, k, v, a_ref[...], b_ref[...], _head_scalars(alog_ref, hb), _head_scalars(dt_ref, hb),
                     t_ref[...])
        s, do, dsp = s_ref[...], do_ref[...], ds_ref[...]
        vn = p.u - _hmm(p.w, s)
        dqd = _hmm(do, s, _NT)
        dqk = _hmm(do, vn, _NT)
        dvn = _hmm(p.qk, do, _TN) + _hmm(p.kd, dsp)
        dkd = _hmm(vn, dsp, _NT)
        d_el = _colsum(_rowsum(s * dsp))
        ds_ref[...] = dsp * p.el + _hmm(p.qd, do, _TN) - _hmm(p.w, dvn, _TN)
        dw = -_hmm(dvn, s, _NT)
        d_t = _hmm(dvn, p.vb, _NT) + _hmm(dw, p.kbe, _NT)
        dvb, dkbe = _hmm(p.t, dvn, _TN), _hmm(p.t, dw, _TN)
        d_a = jnp.where(p.strict, -_hmm(p.t, _hmm(d_t, p.t, _NT), _TN), 0.0)
        dkk = d_a * p.dm
        dqk0 = dqk * p.dm
        ddm = d_a * p.kk + dqk * p.qk0
        dkb = _hmm(dkk, k) + dkbe * p.e
        dq_ref[...] = _hmm(dqk0, k) + dqd * p.e
        dk_ref[...] = _hmm(dkk, p.kb, _TN) + _hmm(dqk0, q, _TN) + dkd * p.ek + dkb * p.beta
        dv_ref[...] = dvb * p.beta
        dbeta = _rowsum(dkb * k) + _rowsum(dvb * v)
        d_e = _rowsum(dqd * q) + _rowsum(dkbe * p.kb)
        d_ek = _rowsum(dkd * k)
        m = ddm * p.dm
        dgam = d_e * p.e - d_ek * p.ek + _rowsum(m) - p.to_col(_colsum(m))
        dgam_last = _colsum(d_ek * p.ek) + d_el * p.el
        dg_row = _colsum(jnp.where(p.tril, dgam, 0.0)) + dgam_last
        da_row = dg_row * p.a_neg * _sigmoid(p.xg)
        da_ref[...] = da_row
        db_ref[...] = p.to_row(dbeta) * p.beta_row * (1.0 - p.beta_row)
        dal_ref[...] += _rowsum(dg_row * p.g_row)
        ddt_ref[...] += _rowsum(da_row)
        side_finish()

    tok_shape = jax.ShapeDtypeStruct((hh, t, dk), F32)
    row_shape = jax.ShapeDtypeStruct((hh, n, 1, c), F32)
    acc_shape = jax.ShapeDtypeStruct((hh, 1, LANES), F32)
    return _carrier_call(
        body, name, grid, [smem, smem] + qkv + [row, row, state, state, tok], [tok, tok, tok, row, row, acc, acc],
        [tok_shape, tok_shape, tok_shape, row_shape, row_shape, acc_shape, acc_shape],
        [pltpu.VMEM((hb, dk, dk), F32)], side, (alog, dtb, qkv_h, qkv_h, qkv_h, araw, braw, states, t_mats, do))


def _gdn_post_fwd(o, proj, zcol, g_o, name):
    hh, t, dv = o.shape
    tt = _tile(t, _ROW_TILES)
    zblk = zcol // (hh * dv)

    def body(o_ref, z_ref, g_ref, y_ref):
        for h in range(hh):
            sl = slice(h * dv, (h + 1) * dv)
            ov = o_ref[h]
            r = lax.rsqrt(jnp.mean(ov * ov, axis=-1, keepdims=True) + EPS)
            y_ref[:, sl] = (ov * r * g_ref[...] * _silu(z_ref[:, sl].astype(F32))).astype(y_ref.dtype)

    return pl.pallas_call(
        body, name=name, grid=(t // tt,),
        in_specs=[pl.BlockSpec((hh, tt, dv), lambda i: (0, i, 0)), pl.BlockSpec((tt, hh * dv), lambda i: (i, zblk)),
                  pl.BlockSpec((1, dv), lambda i: (0, 0))],
        out_specs=pl.BlockSpec((tt, hh * dv), lambda i: (i, 0)),
        out_shape=jax.ShapeDtypeStruct((t, hh * dv), MXU_DTYPE), compiler_params=_params(("parallel",)))(o, proj, g_o)


def _gdn_post_bwd(o, proj, zcol, dy, g_o, dproj, name):
    hh, t, dv = o.shape
    tt = _tile(t, _ROW_TILES)
    zblk = zcol // (hh * dv)

    def body(o_ref, z_ref, dy_ref, g_ref, _, do_ref, dz_ref, dg_ref):
        @pl.when(pl.program_id(0) == 0)
        def _():
            dg_ref[...] = jnp.zeros_like(dg_ref)

        gv = g_ref[...]
        for h in range(hh):
            sl = slice(h * dv, (h + 1) * dv)
            ov, zz, dy = o_ref[h], z_ref[:, sl].astype(F32), dy_ref[:, sl]
            r = lax.rsqrt(jnp.mean(ov * ov, axis=-1, keepdims=True) + EPS)
            oh = ov * r
            sg = _sigmoid(zz)
            dz_ref[:, sl] = (dy * oh * gv * (sg * (1.0 + zz * (1.0 - sg)))).astype(dz_ref.dtype)
            don = dy * (zz * sg)
            dg_ref[...] += _colsum(don * oh)
            doh = don * gv
            do_ref[h] = r * (doh - oh * jnp.mean(doh * oh, axis=-1, keepdims=True))

    return pl.pallas_call(
        body, name=name, grid=(t // tt,),
        in_specs=[pl.BlockSpec((hh, tt, dv), lambda i: (0, i, 0)), pl.BlockSpec((tt, hh * dv), lambda i: (i, zblk)),
                  pl.BlockSpec((tt, hh * dv), lambda i: (i, 0)), pl.BlockSpec((1, dv), lambda i: (0, 0)), _ANY],
        out_specs=[pl.BlockSpec((hh, tt, dv), lambda i: (0, i, 0)), pl.BlockSpec((tt, hh * dv), lambda i: (i, zblk)),
                   pl.BlockSpec((1, dv), lambda i: (0, 0))],
        out_shape=[jax.ShapeDtypeStruct((hh, t, dv), F32), jax.ShapeDtypeStruct(dproj.shape, dproj.dtype),
                   jax.ShapeDtypeStruct((1, dv), F32)],
        input_output_aliases={4: 1}, compiler_params=_params(("arbitrary",)))(o, proj, dy, g_o, dproj)


def _cols_as_rows(x, col, name):
    t = x.shape[0]
    tt = _tile(t, _ROW_TILES)

    def body(x_ref, o_ref):
        o_ref[...] = x_ref[...].T

    return pl.pallas_call(
        body, name=name, grid=(t // tt,), in_specs=[pl.BlockSpec((tt, LANES), lambda i: (i, col // LANES))],
        out_specs=pl.BlockSpec((LANES, tt), lambda i: (0, i)), out_shape=jax.ShapeDtypeStruct((LANES, t), x.dtype),
        compiler_params=_params(("parallel",)))(x)


def _rows_into_cols(dst, rows, col, name):
    t = dst.shape[0]
    tt = _tile(t, _ROW_TILES)

    def body(r_ref, _, o_ref):
        o_ref[...] = r_ref[...].T.astype(o_ref.dtype)

    return pl.pallas_call(
        body, name=name, grid=(t // tt,), in_specs=[pl.BlockSpec((LANES, tt), lambda i: (0, i)), _ANY],
        out_specs=pl.BlockSpec((tt, LANES), lambda i: (i, col // LANES)),
        out_shape=jax.ShapeDtypeStruct(dst.shape, dst.dtype), input_output_aliases={1: 0},
        compiler_params=_params(("parallel",)))(rows, dst)


def _adamw(g, w, m, v):
    m = ADAM_B1 * m + (1.0 - ADAM_B1) * g
    v = ADAM_B2 * v + (1.0 - ADAM_B2) * (g * g)
    m_hat = m / (1.0 - ADAM_B1 ** ADAM_STEP)
    v_hat = v / (1.0 - ADAM_B2 ** ADAM_STEP)
    return -ADAM_LR * (m_hat / (jnp.sqrt(v_hat) + ADAM_EPS) + ADAM_WD * w), m, v


def _ada_fwd(c_all, ada_w, name):
    nl, d, cols = ada_w.shape
    b = c_all.shape[0]

    def body(c_ref, w_ref, o_ref):
        o_ref[...] = _mm_hi(_silu(c_ref[...]), w_ref[...])

    return pl.pallas_call(
        body, name=name, grid=(nl,),
        in_specs=[pl.BlockSpec((b, d), lambda i: (0, 0)), pl.BlockSpec((None, d, cols), lambda i: (i, 0, 0))],
        out_specs=pl.BlockSpec((None, b, cols), lambda i: (i, 0, 0)),
        out_shape=jax.ShapeDtypeStruct((nl, b, cols), F32), compiler_params=_params(("parallel",)))(c_all, ada_w)


def _ada_bwd(c_col, dm, w, m, v, name):
    nl, d, cols = w.shape
    b = c_col.shape[0]
    tr = _tile(d, (256, 128))

    def body(c_ref, dm_ref, w_ref, m_ref, v_ref, g_ref, dl_ref, mo_ref, vo_ref):
        g = _silu(c_ref[0]) * dm_ref[pl.ds(0, 1), :]
        for j in range(1, b):
            g = g + _silu(c_ref[j]) * dm_ref[pl.ds(j, 1), :]
        g_ref[...] = g
        dl_ref[...], mo_ref[...], vo_ref[...] = _adamw(g, w_ref[...], m_ref[...], v_ref[...])

    blk = pl.BlockSpec((None, tr, cols), lambda l, i: (l, i, 0))
    shape = jax.ShapeDtypeStruct((nl, d, cols), F32)
    return pl.pallas_call(
        body, name=name, grid=(nl, d // tr),
        in_specs=[pl.BlockSpec((b, tr, 1), lambda l, i: (0, i, 0)), pl.BlockSpec((None, b, cols), lambda l, i: (l, 0, 0)),
                  blk, blk, blk],
        out_specs=[blk, blk, blk, blk], out_shape=[shape] * 4,
        compiler_params=_params(("parallel", "parallel")))(c_col, dm, w, m, v)


_GRAD_ROW_TILES = (256, 128, 176, 88)


def _sum_adam(parts, w, m, v, name):
    nl, npart, r, cdim = parts.shape
    tr = _tile(r, _GRAD_ROW_TILES)

    def body(p_ref, w_ref, m_ref, v_ref, g_ref, dl_ref, mo_ref, vo_ref):
        g = p_ref[0].astype(F32)
        for j in range(1, npart):
            g = g + p_ref[j].astype(F32)
        g_ref[...] = g
        dl_ref[...], mo_ref[...], vo_ref[...] = _adamw(g, w_ref[...], m_ref[...], v_ref[...])

    blk = pl.BlockSpec((None, tr, cdim), lambda l, i: (l, i, 0))
    shape = jax.ShapeDtypeStruct((nl, r, cdim), F32)
    return pl.pallas_call(
        body, name=name, grid=(nl, r // tr),
        in_specs=[pl.BlockSpec((None, npart, tr, cdim), lambda l, i: (l, 0, i, 0)), blk, blk, blk],
        out_specs=[blk, blk, blk, blk], out_shape=[shape] * 4,
        compiler_params=_params(("parallel", "parallel")))(parts, w, m, v)


def _cols_from_blocks(g, plan, width, name):
    _, r, cdim = g.shape
    tr = _tile(r, (256, 128))
    covered = sorted((dst, dst + n) for _, _, n, dst in plan)
    holes, pos = [], 0
    for a, b in covered:
        if a > pos:
            holes.append((pos, a))
        pos = max(pos, b)
    if pos < width:
        holes.append((pos, width))

    def body(g_ref, o_ref):
        for a, b in holes:
            o_ref[:, a:b] = jnp.zeros((tr, b - a), o_ref.dtype)
        for j, src, n, dst in plan:
            o_ref[:, dst:dst + n] = g_ref[j, :, src:src + n]

    return pl.pallas_call(
        body, name=name, grid=(r // tr,), in_specs=[pl.BlockSpec((N_DEV, tr, cdim), lambda i: (0, i, 0))],
        out_specs=pl.BlockSpec((tr, width), lambda i: (i, 0)), out_shape=jax.ShapeDtypeStruct((r, width), g.dtype),
        compiler_params=_params(("parallel",)))(g)


def _blocks_from_cols(w, plan, cdim, name):
    r, width = w.shape
    tr = _tile(r, (256, 128))

    def body(w_ref, o_ref):
        for j, src, n, dst in plan:
            o_ref[j, :, src:src + n] = w_ref[:, dst:dst + n]

    return pl.pallas_call(
        body, name=name, grid=(r // tr,), in_specs=[pl.BlockSpec((tr, width), lambda i: (i, 0))],
        out_specs=pl.BlockSpec((N_DEV, tr, cdim), lambda i: (0, i, 0)),
        out_shape=jax.ShapeDtypeStruct((N_DEV, r, cdim), w.dtype), compiler_params=_params(("parallel",)))(w)


def _pair_sum(x, tmp, core, name):
    _, r, cdim = x.shape
    tr = _tile(r, _GRAD_ROW_TILES)

    def body(core_ref, x_ref, t_ref, o_ref):
        o_ref[...] = (x_ref[...] + t_ref[...]).astype(o_ref.dtype)

    grid_spec = pltpu.PrefetchScalarGridSpec(
        num_scalar_prefetch=1, grid=(N_DEV // 2, r // tr),
        in_specs=[pl.BlockSpec((None, tr, cdim), lambda ch, i, core_ref: (2 * ch + core_ref[0], i, 0)),
                  pl.BlockSpec((None, tr, cdim), lambda ch, i, core_ref: (ch, i, 0))],
        out_specs=pl.BlockSpec((None, tr, cdim), lambda ch, i, core_ref: (ch, i, 0)))
    return pl.pallas_call(
        body, name=name, grid_spec=grid_spec, out_shape=jax.ShapeDtypeStruct((N_DEV // 2, r, cdim), WIRE_DTYPE),
        compiler_params=_params(("parallel", "parallel")))(core, x, tmp)


_ANY = pl.BlockSpec(memory_space=pl.ANY)
_CHIP_FLIPS = ((1, 0), (0, 1), (1, 1))


def _coords():
    return lax.axis_index("x"), lax.axis_index("y"), lax.axis_index("c")


def _flip(v, f):
    return 1 - v if f else v


def _a2a_direct(xs, name):
    n, ncp = len(xs), N_DEV - 1

    def body(*refs):
        ins, outs = refs[:n], refs[n:2 * n]
        send, recv, loc = refs[2 * n:]
        x, y, c = _coords()
        me = 4 * x + 2 * y + c
        local = [pltpu.make_async_copy(ins[i].at[me], outs[i].at[me], loc.at[i]) for i in range(n)]
        for cp in local:
            cp.start()
        remote = []
        for i in range(n):
            for k in range(1, N_DEV):
                px, py, pc = _flip(x, k & 4), _flip(y, k & 2), _flip(c, k & 1)
                cp = pltpu.make_async_remote_copy(
                    src_ref=ins[i].at[4 * px + 2 * py + pc], dst_ref=outs[i].at[me],
                    send_sem=send.at[i * ncp + k - 1], recv_sem=recv.at[i * ncp + k - 1],
                    device_id=(px, py, pc), device_id_type=MESH)
                cp.start()
                remote.append(cp)
        for cp in remote:
            cp.wait()
        for cp in local:
            cp.wait()

    return pl.pallas_call(
        body, name=name, in_specs=[_ANY] * n, out_specs=[_ANY] * n,
        out_shape=[jax.ShapeDtypeStruct(a.shape, a.dtype) for a in xs],
        scratch_shapes=[pltpu.SemaphoreType.DMA((n * ncp,)), pltpu.SemaphoreType.DMA((n * ncp,)),
                        pltpu.SemaphoreType.DMA((n,))])(*xs)


class _AllGatherSide:
    def __init__(self, blocks):
        self.operands = list(blocks)
        n = self.n = len(self.operands)
        self.n_in = self.n_out = n
        self.out_shape = [jax.ShapeDtypeStruct((N_DEV,) + a.shape, a.dtype) for a in self.operands]
        self.aliases = {}
        nici, nd2d = len(_CHIP_FLIPS), N_DEV // 2
        self.scratch = [pltpu.SemaphoreType.DMA((n * nici,)), pltpu.SemaphoreType.DMA((n * nici,)),
                        pltpu.SemaphoreType.DMA((n * nd2d,)), pltpu.SemaphoreType.DMA((n * nd2d,)),
                        pltpu.SemaphoreType.DMA((n,))]

    def _first(self, ins, outs, sems):
        send, recv, _, _, loc = sems
        x, y, c = _coords()
        me = 4 * x + 2 * y + c
        nici = len(_CHIP_FLIPS)
        local = [pltpu.make_async_copy(ins[i], outs[i].at[me], loc.at[i]) for i in range(self.n)]
        remote = [pltpu.make_async_remote_copy(
            src_ref=ins[i], dst_ref=outs[i].at[me], send_sem=send.at[i * nici + j], recv_sem=recv.at[i * nici + j],
            device_id=(_flip(x, fx), _flip(y, fy), c), device_id_type=MESH)
            for i in range(self.n) for j, (fx, fy) in enumerate(_CHIP_FLIPS)]
        return local + remote

    def _second(self, outs, sems):
        _, _, send, recv, _ = sems
        x, y, c = _coords()
        nd2d = N_DEV // 2
        return [pltpu.make_async_remote_copy(
            src_ref=outs[i].at[2 * ch + c], dst_ref=outs[i].at[2 * ch + c], send_sem=send.at[i * nd2d + ch],
            recv_sem=recv.at[i * nd2d + ch], device_id=(x, y, 1 - c), device_id_type=MESH)
            for i in range(self.n) for ch in range(nd2d)]

    def start(self, ins, outs, sems):
        for cp in self._first(ins, outs, sems):
            cp.start()

    def finish(self, ins, outs, sems):
        for cp in self._first(ins, outs, sems):
            cp.wait()
        second = self._second(outs, sems)
        for cp in second:
            cp.start()
        for cp in second:
            cp.wait()


class _ReduceScatterIciSide:
    def __init__(self, sums, accs, layer):
        self.operands = list(sums) + list(accs)
        n = self.n = len(sums)
        self.layer = layer
        self.n_in, self.n_out = 2 * n, n
        self.out_shape = [jax.ShapeDtypeStruct(a.shape, a.dtype) for a in accs]
        self.aliases = {n + i: i for i in range(n)}
        nici = len(_CHIP_FLIPS)
        self.scratch = [pltpu.SemaphoreType.DMA((n * nici,)), pltpu.SemaphoreType.DMA((n * nici,)),
                        pltpu.SemaphoreType.DMA((n,))]

    def _copies(self, ins, outs, sems):
        send, recv, loc = sems
        x, y, c = _coords()
        chip = 2 * x + y
        nici = len(_CHIP_FLIPS)
        local = [pltpu.make_async_copy(ins[i].at[chip], outs[i].at[self.layer, chip], loc.at[i])
                 for i in range(self.n)]
        remote = [pltpu.make_async_remote_copy(
            src_ref=ins[i].at[2 * _flip(x, fx) + _flip(y, fy)], dst_ref=outs[i].at[self.layer, chip],
            send_sem=send.at[i * nici + j], recv_sem=recv.at[i * nici + j],
            device_id=(_flip(x, fx), _flip(y, fy), c), device_id_type=MESH)
            for i in range(self.n) for j, (fx, fy) in enumerate(_CHIP_FLIPS)]
        return local + remote

    def start(self, ins, outs, sems):
        for cp in self._copies(ins, outs, sems):
            cp.start()

    def finish(self, ins, outs, sems):
        for cp in self._copies(ins, outs, sems):
            cp.wait()


def _run_side(side, name):
    def body(*refs):
        ins, outs = refs[:side.n_in], refs[side.n_in:side.n_in + side.n_out]
        sems = refs[side.n_in + side.n_out:]
        side.start(ins, outs, sems)
        side.finish(ins, outs, sems)

    return pl.pallas_call(
        body, name=name, in_specs=[_ANY] * side.n_in, out_specs=[_ANY] * side.n_out, out_shape=side.out_shape,
        input_output_aliases=side.aliases, scratch_shapes=side.scratch)(*side.operands)


class _ReduceScatterD2dSide:
    def __init__(self, parts):
        self.operands = list(parts)
        n = self.n = len(self.operands)
        self.n_in = self.n_out = n
        nd2d = N_DEV // 2
        self.out_shape = [jax.ShapeDtypeStruct((nd2d,) + a.shape[1:], a.dtype) for a in self.operands]
        self.aliases = {}
        self.scratch = [pltpu.SemaphoreType.DMA((n * nd2d,)), pltpu.SemaphoreType.DMA((n * nd2d,))]

    def _copies(self, ins, outs, sems):
        send, recv = sems
        x, y, c = _coords()
        nd2d = N_DEV // 2
        return [pltpu.make_async_remote_copy(
            src_ref=ins[i].at[2 * ch + 1 - c], dst_ref=outs[i].at[ch], send_sem=send.at[i * nd2d + ch],
            recv_sem=recv.at[i * nd2d + ch], device_id=(x, y, 1 - c), device_id_type=MESH)
            for i in range(self.n) for ch in range(nd2d)]

    def start(self, ins, outs, sems):
        for cp in self._copies(ins, outs, sems):
            cp.start()

    def finish(self, ins, outs, sems):
        for cp in self._copies(ins, outs, sems):
            cp.wait()


_PACK_ROWS = 256


def _pack(arrs):
    flat = jnp.concatenate([a.reshape(-1) for a in arrs])
    quantum = _PACK_ROWS * LANES
    total = -(-flat.shape[0] // quantum) * quantum
    return jnp.pad(flat, (0, total - flat.shape[0])).reshape(-1, LANES)


def _unpack(packed, like):
    flat, out, pos = packed.reshape(-1), [], 0
    for a in like:
        out.append(flat[pos:pos + a.size].reshape(a.shape))
        pos += a.size
    return out


def kernel(x, c, ada_w, ada_b, norm1_g, w_in, conv_w, spatial_w, spatial_b, v_norm_g, a_log, dt_bias, o_norm_g, w_branch_a, w_branch_b, w_out, norm2_g, w_ffn_in, w_ffn_out, final_g, loss_target, m_ada_w, m_ada_b, m_norm1_g, m_w_in, m_conv_w, m_spatial_w, m_spatial_b, m_v_norm_g, m_a_log, m_dt_bias, m_o_norm_g, m_w_branch_a, m_w_branch_b, m_w_out, m_norm2_g, m_w_ffn_in, m_w_ffn_out, m_final_g, v_ada_w, v_ada_b, v_norm1_g, v_w_in, v_conv_w, v_spatial_w, v_spatial_b, v_v_norm_g, v_a_log, v_dt_bias, v_o_norm_g, v_w_branch_a, v_w_branch_b, v_w_out, v_norm2_g, v_w_ffn_in, v_w_ffn_out, v_final_g):
    nl, d = ada_w.shape[0], x.shape[2]
    t = x.shape[1]
    nchunk = t // GDN_CHUNK
    xi, yi, ci = _coords()
    me = 4 * xi + 2 * yi + ci
    core = jnp.reshape(ci, (1,)).astype(jnp.int32)
    x0, target = x[0], loss_target[0]
    wcols = 3 * HEADS * HEAD_DIM
    lay = _ProjLayout(d)
    in_pieces = lay.pieces(w_in.shape[2])
    fi_shard = w_ffn_in.shape[2]
    fi_pieces = [(j, 0, fi_shard, fi_shard * j) for j in range(N_DEV)]

    c_all, cw_all = _a2a_direct([jnp.broadcast_to(c[None], (N_DEV,) + c.shape),
                                 jnp.broadcast_to(conv_w[None], (N_DEV,) + conv_w.shape)], "gather_small")
    c_all = c_all[:, 0]
    conv_full = cw_all.transpose(1, 2, 0, 3).reshape(nl, CONV_K, wcols)
    modp = _ada_fwd(c_all, ada_w, "ada_fwd")
    (modx,) = _a2a_direct([modp.transpose(1, 0, 2)], "mod_exchange")
    mod = (modx.transpose(1, 0, 2).reshape(nl, 6 * d) + ada_b).reshape(nl, 6, 1, d)

    big = (w_in, w_branch_a, w_branch_b, w_out, w_ffn_in, w_ffn_out)
    big_wire = [w.astype(WIRE_DTYPE) for w in big]
    gather_in = lambda i: _AllGatherSide([big_wire[0][i]])
    gather_early = lambda i: _AllGatherSide([big_wire[k][i] for k in (1, 2, 3, 5)])
    gather_late = lambda i: _AllGatherSide([big_wire[4][i]] + ([big_wire[0][i + 1]] if i + 1 < nl else []))
    row_full = lambda g: g.reshape(-1, g.shape[2])
    padded_in = lambda g: _cols_from_blocks(g, in_pieces, lay.width, "w_in_cols")
    w_pads = [padded_in(_run_side(gather_in(0), "ag_first")[0])] + [None] * (nl - 1)
    weights = [None] * nl

    def rows_of(ba_rows, lo):
        return ba_rows[lo:lo + HEADS].reshape(HEADS, nchunk, 1, GDN_CHUNK)

    saved = []
    x_cur, delta, gt_prev = x0, None, None
    for i in range(nl):
        sh1, sc1, gt1, sh2, sc2, gt2 = (mod[i, k] for k in range(6))
        s = dict(gt1=gt1, gt2=gt2, sc1=sc1, sc2=sc2)
        s["x_in"], s["h"] = _resid_norm(x_cur, delta, gt_prev, norm1_g[i][None], sc1, sh1, "norm1_fwd")
        s["proj"], g_a, g_b, g_o, g_fo = _matmul(s["h"], w_pads[i], "nn", "proj_fwd", out_dtype=ACT_DTYPE,
                                                 side=gather_early(i))
        ba = _matmul(s["h"], w_pads[i][:, lay.ba:], "nn", "proj_ba_fwd")
        s["b_col"] = spatial_b[i][:, :, None]
        s["ya"] = _mixer_a_fwd(s["proj"], lay.uv, spatial_w[i], s["b_col"], v_norm_g[i][None], "mixer_a_fwd")
        s["qkv_h"] = _conv_fwd(s["proj"], conv_full[i], "conv_fwd")
        ba_rows = _cols_as_rows(ba, 0, "ba_rows")
        s["braw"], s["araw"] = rows_of(ba_rows, 0), rows_of(ba_rows, HEADS)
        s["o"], s["states"], s["t_mats"], g_fi, *g_in = _gdn_fwd(
            s["qkv_h"], s["araw"], s["braw"], a_log[i], dt_bias[i], "gdn_fwd", gather_late(i))
        if g_in:
            w_pads[i + 1] = padded_in(g_in[0])
        weights[i] = (row_full(g_a), row_full(g_b), row_full(g_o),
                      _cols_from_blocks(g_fi, fi_pieces, N_DEV * fi_shard, "w_ffn_in_cols"), row_full(g_fo))
        w_a, w_b, w_o, w_fi, w_fo = weights[i]
        s["yb"] = _gdn_post_fwd(s["o"], s["proj"], lay.z, o_norm_g[i][None], "gdn_post_fwd")
        s["pa"] = _matmul(s["ya"], w_a, "nn", "branch_a_fwd")
        s["pb"] = _matmul(s["yb"], w_b, "nn", "branch_b_fwd")
        s["merged"] = _merge_fwd(s["pa"], s["pb"], s["proj"], lay.gates, "merge_fwd")
        s["mo"] = _matmul(s["merged"], w_o, "nn", "out_fwd")
        s["x1"], s["h2"] = _resid_norm(s["x_in"], s["mo"], gt1, norm2_g[i][None], sc2, sh2, "norm2_fwd")
        s["gu"], s["a"] = _ffn_in_swiglu(s["h2"], w_fi, "ffn_in_swiglu_fwd")
        s["fo"] = _matmul(s["a"], w_fo, "nn", "ffn_out_fwd")
        saved.append(s)
        x_cur, delta, gt_prev = s["x1"], s["fo"], gt2
    dx, d_final_g, loss_tile = _final_loss(x_cur, delta, gt_prev, final_g[None], target, "final_loss")
    loss = lax.psum(loss_tile[0, 0], ("x", "y", "c"))

    big_shapes = [(d, w_in.shape[2]), w_branch_a.shape[1:], w_branch_b.shape[1:], w_out.shape[1:],
                  (w_ffn_in.shape[2], d), w_ffn_out.shape[1:]]
    accs = [lax.empty((nl, N_DEV // 2) + tuple(sh), WIRE_DTYPE) for sh in big_shapes]
    row_blocks = lambda g: g.reshape(N_DEV, -1, g.shape[1])
    dmod, small = [None] * nl, [None] * nl
    d_conv = [None] * nl
    parts, sums = None, None
    beside_gdn, beside_dw, beside_dx = (0,), (4,), (1, 2, 3, 5)
    rep_parts = [None] * nl
    rep_pack = lambda i: _pack((dmod[i],) + small[i])

    def scatter_side(idx, layer):
        if sums is None:
            return _NoSide
        return _ReduceScatterIciSide([sums[k] for k in idx], [accs[k] for k in idx], layer)

    def scattered_into(accs, idx, new):
        accs = list(accs)
        for k, a in zip(idx, new):
            accs[k] = a
        return accs

    for i in reversed(range(nl)):
        s = saved[i]
        w_a, w_b, w_o, w_fi, w_fo = weights[i]
        if i == nl - 1:
            dfo, dgt2 = _gate_bwd(dx, s["fo"], s["gt2"], "gate2_bwd")
        else:
            dgt2 = dgt2_before
        g_fo = _matmul(s["a"], dfo, "tn", "ffn_out_dw")
        da = _matmul(dfo, w_fo, "nt", "ffn_out_dx")
        dgu = _swiglu_bwd(s["gu"], da, "swiglu_bwd")
        if parts is None:
            g_fi = _matmul(dgu, s["h2"], "tn", "ffn_in_dw")
        else:
            g_fi, *other = _matmul(dgu, s["h2"], "tn", "ffn_in_dw", side=_ReduceScatterD2dSide(parts))
            sums = [_pair_sum(p, o, core, "rs_pair_sum_%d" % k) for k, (p, o) in enumerate(zip(parts, other))]
        if i + 1 < nl:
            dh2, rep_parts[i + 1] = _matmul(dgu, w_fi, "nt", "ffn_in_dx", side=_AllGatherSide([rep_pack(i + 1)]))
        else:
            dh2 = _matmul(dgu, w_fi, "nt", "ffn_in_dx")
        dx1, dsh2, dsc2, dg2, dmo, dgt1 = _norm_bwd(s["x1"], dh2, dx, norm2_g[i][None], s["sc2"], "norm2_bwd",
                                                    gate=(s["mo"], s["gt1"]))
        g_o = _matmul(s["merged"], dmo, "tn", "out_dw")
        dmerged = _matmul(dmo, w_o, "nt", "out_dx")
        dproj = lax.empty((t, lay.width), MXU_DTYPE)
        dpa, dpb, dproj = _merge_bwd(dmerged, s["pa"], s["pb"], s["proj"], lay.gates, dproj, "merge_bwd")
        g_a = _matmul(s["ya"], dpa, "tn", "branch_a_dw")
        dya = _matmul(dpa, w_a, "nt", "branch_a_dx")
        g_b = _matmul(s["yb"], dpb, "tn", "branch_b_dw")
        dyb = _matmul(dpb, w_b, "nt", "branch_b_dx")
        dproj, d_ws, d_bs, d_gv = _mixer_a_bwd(s["proj"], lay.uv, dya, spatial_w[i], jnp.swapaxes(spatial_w[i], 1, 2),
                                               s["b_col"], v_norm_g[i][None], dproj, "mixer_a_bwd")
        do, dproj, d_go = _gdn_post_bwd(s["o"], s["proj"], lay.z, dyb, o_norm_g[i][None], dproj, "gdn_post_bwd")
        dq, dk, dv, d_ar, d_br, d_al, d_dt, *scattered = _gdn_bwd(
            s["qkv_h"], s["araw"], s["braw"], a_log[i], dt_bias[i], s["states"], s["t_mats"], do, "gdn_bwd",
            scatter_side(beside_gdn, i + 1))
        accs = scattered_into(accs, beside_gdn, scattered)
        dacc, d_conv[i] = _conv_bwd_pre(s["proj"], dq, dk, dv, conv_full[i], "conv_bwd_pre")
        dproj = _conv_bwd_in(dacc, conv_full[i], dproj, "conv_bwd_in")
        dba_rows = jnp.pad(jnp.concatenate([d_br.reshape(HEADS, t), d_ar.reshape(HEADS, t)]),
                           ((0, LANES - 2 * HEADS), (0, 0)))
        dproj = _rows_into_cols(dproj, dba_rows, lay.ba, "dproj_ba")
        if sums is None:
            g_pad = _matmul(s["h"], dproj, "tn", "proj_dw")
            dh = _matmul(dproj, w_pads[i], "nt", "proj_dx")
        else:
            g_pad, *scattered = _matmul(s["h"], dproj, "tn", "proj_dw", side=scatter_side(beside_dw, i + 1))
            accs = scattered_into(accs, beside_dw, scattered)
            dh, *scattered = _matmul(dproj, w_pads[i], "nt", "proj_dx", side=scatter_side(beside_dx, i + 1))
            accs = scattered_into(accs, beside_dx, scattered)
        if i > 0:
            dx, dsh1, dsc1, dg1, dfo, dgt2_before = _norm_bwd(s["x_in"], dh, dx1, norm1_g[i][None], s["sc1"], "norm1_bwd",
                                                              gate=(saved[i - 1]["fo"], saved[i - 1]["gt2"]))
        else:
            dx, dsh1, dsc1, dg1 = _norm_bwd(s["x_in"], dh, dx1, norm1_g[i][None], s["sc1"], "norm1_bwd")
        dmod[i] = jnp.concatenate([dsh1, dsc1, dgt1, dsh2, dsc2, dgt2], axis=1)[0]
        small[i] = (dg1[0], d_ws, d_bs[:, :, 0], d_gv[0], d_al[:, 0, 0], d_dt[:, 0, 0], d_go[0], dg2[0])
        parts = [_blocks_from_cols(g_pad, in_pieces, w_in.shape[2], "w_in_blocks"), row_blocks(g_a), row_blocks(g_b),
                 row_blocks(g_o), row_blocks(g_fi), row_blocks(g_fo)]
    other = _run_side(_ReduceScatterD2dSide(parts), "rs_d2d_last")
    sums = [_pair_sum(p, o, core, "rs_pair_sum_%d" % k) for k, (p, o) in enumerate(zip(parts, other))]
    accs = _run_side(_ReduceScatterIciSide(sums, accs, 0), "rs_ici_last")

    rep_w = (ada_b, norm1_g, spatial_w, spatial_b, v_norm_g, a_log, dt_bias, o_norm_g, norm2_g)
    rep_m = (m_ada_b, m_norm1_g, m_spatial_w, m_spatial_b, m_v_norm_g, m_a_log, m_dt_bias, m_o_norm_g, m_norm2_g)
    rep_v = (v_ada_b, v_norm1_g, v_spatial_w, v_spatial_b, v_v_norm_g, v_a_log, v_dt_bias, v_o_norm_g, v_norm2_g)
    rep_parts[0], fin_parts = _run_side(_AllGatherSide([rep_pack(0), _pack([d_final_g[0]])]), "small_grads_last")
    dmod = jnp.stack(dmod)
    d_conv_blocks = jnp.stack(d_conv).reshape(nl, CONV_K, N_DEV, -1).transpose(2, 0, 1, 3).reshape(N_DEV, -1, LANES)
    dmod_blocks = dmod.reshape(nl, N_DEV, -1).transpose(1, 0, 2)
    conv_all, dmod_all = _a2a_direct([d_conv_blocks, dmod_blocks], "small_grads_scatter")
    by_layer = lambda arrs: jnp.stack([_pack([a[i] for a in arrs]) for i in range(nl)])
    rep_out = _sum_adam(jnp.stack(rep_parts), by_layer(rep_w), by_layer(rep_m), by_layer(rep_v), "adam_small")
    fin_out = _sum_adam(fin_parts[None], _pack([final_g])[None], _pack([m_final_g])[None], _pack([v_final_g])[None],
                        "adam_final_g")
    layer_like = [a[0] for a in rep_w]
    rep_out = [[jnp.stack(per_layer) for per_layer in zip(*[_unpack(o[i], layer_like) for i in range(nl)])]
               + _unpack(f[0], [final_g]) for o, f in zip(rep_out, fin_out)]
    conv_out = _sum_adam(conv_all[None], conv_w.reshape(1, -1, LANES), m_conv_w.reshape(1, -1, LANES),
                         v_conv_w.reshape(1, -1, LANES), "adam_conv")
    conv_out = [o.reshape(conv_w.shape) for o in conv_out]
    ada_out = _ada_bwd(c_all[:, :, None], dmod_all.transpose(1, 0, 2), ada_w, m_ada_w, v_ada_w, "ada_bwd_adam")
    big_m = (m_w_in, m_w_branch_a, m_w_branch_b, m_w_out, m_w_ffn_in, m_w_ffn_out)
    big_v = (v_w_in, v_w_branch_a, v_w_branch_b, v_w_out, v_w_ffn_in, v_w_ffn_out)
    turn = lambda k, a: jnp.swapaxes(a, 1, 2) if k == 4 else a
    big_out = [[turn(k, o) for o in _sum_adam(accs[k], turn(k, big[k]), turn(k, big_m[k]), turn(k, big_v[k]),
                                             "adam_big_%d" % k)] for k in range(6)]

    def ordered(kind):
        rep = rep_out[kind]
        return (ada_out[kind], rep[0], rep[1], big_out[0][kind], conv_out[kind], rep[2], rep[3], rep[4], rep[5],
                rep[6], rep[7], big_out[1][kind], big_out[2][kind], big_out[3][kind], rep[8], big_out[4][kind],
                big_out[5][kind], rep[9])

    return (loss, dx[None]) + ordered(0) + ordered(1) + ordered(2) + ordered(3)
```

```python
import functools

import jax
import jax.numpy as jnp
from jax import lax
from jax.experimental import pallas as pl
from jax.experimental.pallas import tpu as pltpu

F32 = jnp.float32
BF16 = jnp.bfloat16
MXU_DTYPE = BF16
WIRE_DTYPE = BF16
ACT_DTYPE = BF16
EPS = 1e-6
LANES = 128
SUBLANES = 8
GDN_CHUNK = 128
A_CHUNK = 128
GROUPS = 8
HEADS = 8
HEAD_DIM = 128
CONV_K = 4
N_DEV = 8
VMEM_LIMIT = 48 * 1024 * 1024
MESH = pl.DeviceIdType.MESH

ADAM_LR = 0.001
ADAM_B1 = 0.9
ADAM_B2 = 0.999
ADAM_EPS = 1e-08
ADAM_WD = 0.01
ADAM_STEP = 10

_NN = (((1,), (0,)), ((), ()))
_NT = (((1,), (1,)), ((), ()))
_TN = (((0,), (0,)), ((), ()))


def _mm(a, b, dims=_NN):
    return lax.dot_general(a.astype(MXU_DTYPE), b.astype(MXU_DTYPE), dims, preferred_element_type=F32)


def _mm_hi(a, b):
    return lax.dot_general(a, b, _NN, precision=lax.Precision.HIGHEST, preferred_element_type=F32)


def _tile(n, cands):
    for c in cands:
        if n % c == 0:
            return c
    return n


def _params(sem=None):
    return pltpu.CompilerParams(dimension_semantics=sem, vmem_limit_bytes=VMEM_LIMIT)


def _sigmoid(x):
    return 1.0 / (1.0 + jnp.exp(-x))


def _silu(x):
    return x * _sigmoid(x)


_GELU_C = 0.7978845608028654
_GELU_A = 0.044715


def _gelu(x):
    return 0.5 * x * (1.0 + jnp.tanh(_GELU_C * (x + _GELU_A * x * x * x)))


def _gelu_and_slope(x):
    t = jnp.tanh(_GELU_C * (x + _GELU_A * x * x * x))
    return 0.5 * x * (1.0 + t), 0.5 * (1.0 + t) + 0.5 * x * (1.0 - t * t) * _GELU_C * (1.0 + 3.0 * _GELU_A * x * x)


def _softplus(x):
    return jnp.maximum(x, 0.0) + jnp.log(1.0 + jnp.exp(-jnp.abs(x)))


_MM_TILES = (1024, 1408, 1664, 512, 256, 128)


class _NoSide:
    operands, out_shape, scratch, aliases, n_in, n_out = [], [], [], {}, 0, 0


def _side_hooks(side, refs, n_main_in, n_main_out, n_main_scratch, grid):
    a = n_main_in + side.n_in
    b = a + n_main_out + side.n_out
    ins, outs, sems = refs[n_main_in:a], refs[a + n_main_out:b], refs[b + n_main_scratch:]
    main = refs[:n_main_in] + refs[a:a + n_main_out] + refs[b:b + n_main_scratch]
    ids = [pl.program_id(k) for k in range(len(grid))]

    def start():
        if side.n_in:
            pl.when(functools.reduce(jnp.logical_and, [i == 0 for i in ids]))(lambda: side.start(ins, outs, sems))

    def finish():
        if side.n_in:
            last = functools.reduce(jnp.logical_and, [i == g - 1 for i, g in zip(ids, grid)])
            pl.when(last)(lambda: side.finish(ins, outs, sems))

    return main, start, finish


def _carrier_call(body, name, grid, in_specs, out_specs, out_shape, scratch, side, args):
    aliases = {len(in_specs) + k: len(out_specs) + v for k, v in side.aliases.items()}
    return pl.pallas_call(
        body, name=name, grid=grid, in_specs=list(in_specs) + [_ANY] * side.n_in,
        out_specs=list(out_specs) + [_ANY] * side.n_out, out_shape=list(out_shape) + list(side.out_shape),
        scratch_shapes=list(scratch) + list(side.scratch), input_output_aliases=aliases,
        compiler_params=_params(("arbitrary",) * len(grid)))(*args, *side.operands)


_MM_VMEM_BUDGET = 44 * 1024 * 1024


def _matmul_tiles(mode, m, n, k, out_bytes):
    tk = _tile(k, _MM_TILES)
    tm = _tile(m, _MM_TILES)
    in_bytes = jnp.dtype(MXU_DTYPE).itemsize
    for tn in _MM_TILES:
        if n % tn:
            continue
        need = 2 * in_bytes * (tm * tk + tk * tn) + tm * tn * (2 * out_bytes + (4 if k > tk else 0))
        if need <= _MM_VMEM_BUDGET:
            return tm, tn, tk
    return tm, _tile(n, (LANES,)), tk


def _matmul(a, b, mode, name, out_dtype=F32, side=_NoSide):
    planes, plane_cols = (a.shape[0], a.shape[2]) if a.ndim == 3 else (1, a.shape[1])
    a_shape = (a.shape[-2], planes * plane_cols)
    if mode == "nn":
        (m, k), n = a_shape, b.shape[1]
    elif mode == "nt":
        (m, k), n = a_shape, b.shape[0]
    else:
        (k, m), n = a_shape, b.shape[1]
    tm, tn, tk = _matmul_tiles(mode, m, n, k, jnp.dtype(out_dtype).itemsize)
    nk = k // tk
    grid = (m // tm, n // tn, nk)
    dims = {"nn": _NN, "nt": _NT, "tn": _TN}[mode]

    def body(*refs):
        (a_ref, b_ref, o_ref, acc_ref), side_start, side_finish = _side_hooks(side, refs, 2, 1, 1, grid)
        kk = pl.program_id(2)
        side_start()
        if nk == 1:
            o_ref[...] = _mm(a_ref[...], b_ref[...], dims).astype(o_ref.dtype)
        else:
            @pl.when(kk == 0)
            def _():
                acc_ref[...] = _mm(a_ref[...], b_ref[...], dims)

            @pl.when(jnp.logical_and(kk > 0, kk < nk - 1))
            def _():
                acc_ref[...] += _mm(a_ref[...], b_ref[...], dims)

            @pl.when(kk == nk - 1)
            def _():
                o_ref[...] = (acc_ref[...] + _mm(a_ref[...], b_ref[...], dims)).astype(o_ref.dtype)

        side_finish()

    if a.ndim == 3:
        per = plane_cols // (tm if mode == "tn" else tk)
        a_spec = (pl.BlockSpec((None, tk, tm), lambda i, j, l: (i // per, l, i % per)) if mode == "tn"
                  else pl.BlockSpec((None, tm, tk), lambda i, j, l: (l // per, i, l % per)))
    else:
        a_spec = (pl.BlockSpec((tk, tm), lambda i, j, l: (l, i)) if mode == "tn"
                  else pl.BlockSpec((tm, tk), lambda i, j, l: (i, l)))
    b_spec = (pl.BlockSpec((tn, tk), lambda i, j, l: (j, l)) if mode == "nt"
              else pl.BlockSpec((tk, tn), lambda i, j, l: (l, j)))
    o_spec = pl.BlockSpec((tm, tn), lambda i, j, l: (i, j))
    out = _carrier_call(body, name, grid, [a_spec, b_spec], [o_spec], [jax.ShapeDtypeStruct((m, n), out_dtype)],
                        [pltpu.VMEM((tm, tn) if nk > 1 else (SUBLANES, LANES), F32)], side, (a, b))
    return out if side.n_in else out[0]


_ROW_TILES = (512, 256, 128)


def _resid_norm(x, delta, gt, g, sc, sh, name):
    t, d = x.shape
    tt = _tile(t, _ROW_TILES)
    has = delta is not None

    def body(*refs):
        if has:
            x_ref, d_ref, gt_ref, g_ref, sc_ref, sh_ref, xo_ref, h_ref = refs
            xv = x_ref[...] + gt_ref[...] * d_ref[...]
            xo_ref[...] = xv
        else:
            x_ref, g_ref, sc_ref, sh_ref, h_ref = refs
            xv = x_ref[...]
        r = lax.rsqrt(jnp.mean(xv * xv, axis=-1, keepdims=True) + EPS)
        y = xv * r * g_ref[...]
        h_ref[...] = (y * (1.0 + sc_ref[...]) + sh_ref[...]).astype(h_ref.dtype)

    row = pl.BlockSpec((tt, d), lambda i: (i, 0))
    vec = pl.BlockSpec((1, d), lambda i: (0, 0))
    if has:
        return pl.pallas_call(
            body, name=name, grid=(t // tt,), in_specs=[row, row, vec, vec, vec, vec], out_specs=[row, row],
            out_shape=[jax.ShapeDtypeStruct((t, d), F32), jax.ShapeDtypeStruct((t, d), MXU_DTYPE)],
            compiler_params=_params(("parallel",)))(x, delta, gt, g, sc, sh)
    h = pl.pallas_call(
        body, name=name + "_first", grid=(t // tt,), in_specs=[row, vec, vec, vec], out_specs=row,
        out_shape=jax.ShapeDtypeStruct((t, d), MXU_DTYPE), compiler_params=_params(("parallel",)))(x, g, sc, sh)
    return x, h


def _final_loss(x, delta, gt, g, target, name):
    t, d = x.shape
    tt = _tile(t, _ROW_TILES)

    def body(x_ref, d_ref, gt_ref, g_ref, tg_ref, dx_ref, dg_ref, loss_ref):
        @pl.when(pl.program_id(0) == 0)
        def _():
            dg_ref[...] = jnp.zeros_like(dg_ref)
            loss_ref[...] = jnp.zeros_like(loss_ref)

        xv = x_ref[...] + gt_ref[...] * d_ref[...]
        r = lax.rsqrt(jnp.mean(xv * xv, axis=-1, keepdims=True) + EPS)
        xh = xv * r
        diff = xh * g_ref[...] - tg_ref[...]
        loss_ref[...] += jnp.sum(diff * diff) * (0.5 / d)
        dy = diff * (1.0 / d)
        dg_ref[...] += jnp.sum(dy * xh, axis=0, keepdims=True)
        dxh = dy * g_ref[...]
        dx_ref[...] = r * (dxh - xh * jnp.mean(dxh * xh, axis=-1, keepdims=True))

    row = pl.BlockSpec((tt, d), lambda i: (i, 0))
    vec = pl.BlockSpec((1, d), lambda i: (0, 0))
    tile = pl.BlockSpec((SUBLANES, LANES), lambda i: (0, 0))
    return pl.pallas_call(
        body, name=name, grid=(t // tt,), in_specs=[row, row, vec, vec, row], out_specs=[row, vec, tile],
        out_shape=[jax.ShapeDtypeStruct((t, d), F32), jax.ShapeDtypeStruct((1, d), F32),
                   jax.ShapeDtypeStruct((SUBLANES, LANES), F32)],
        compiler_params=_params(("arbitrary",)))(x, delta, gt, g, target)


def _norm_bwd(x, dh, dres, g, sc, name, gate=None):
    t, d = x.shape
    tt = _tile(t, _ROW_TILES)
    gated = gate is not None

    def body(*refs):
        x_ref, dh_ref, dr_ref, g_ref, sc_ref = refs[:5]
        dx_ref, dsh_ref, dsc_ref, dg_ref = refs[5 + 2 * gated:9 + 2 * gated]

        @pl.when(pl.program_id(0) == 0)
        def _():
            for acc_ref in refs[6 + 2 * gated:9 + 2 * gated] + refs[10 + 2 * gated:]:
                acc_ref[...] = jnp.zeros_like(acc_ref)

        xv, dh = x_ref[...], dh_ref[...]
        r = lax.rsqrt(jnp.mean(xv * xv, axis=-1, keepdims=True) + EPS)
        xh = xv * r
        gv, sc1 = g_ref[...], 1.0 + sc_ref[...]
        dsh_ref[...] += jnp.sum(dh, axis=0, keepdims=True)
        dsc_ref[...] += jnp.sum(dh * xh, axis=0, keepdims=True) * gv
        dg_ref[...] += jnp.sum(dh * xh, axis=0, keepdims=True) * sc1
        dxh = dh * (gv * sc1)
        dx = dr_ref[...] + r * (dxh - xh * jnp.mean(dxh * xh, axis=-1, keepdims=True))
        dx_ref[...] = dx
        if gated:
            br_ref, gt_ref, db_ref, dgt_ref = refs[5], refs[6], refs[11], refs[12]
            db_ref[...] = (dx * gt_ref[...]).astype(db_ref.dtype)
            dgt_ref[...] += jnp.sum(dx * br_ref[...], axis=0, keepdims=True)

    row = pl.BlockSpec((tt, d), lambda i: (i, 0))
    vec = pl.BlockSpec((1, d), lambda i: (0, 0))
    vshape = jax.ShapeDtypeStruct((1, d), F32)
    in_specs, out_specs = [row, row, row, vec, vec], [row, vec, vec, vec]
    out_shape = [jax.ShapeDtypeStruct((t, d), F32), vshape, vshape, vshape]
    if gated:
        in_specs, out_specs = in_specs + [row, vec], out_specs + [row, vec]
        out_shape = out_shape + [jax.ShapeDtypeStruct((t, d), MXU_DTYPE), vshape]
    return pl.pallas_call(
        body, name=name + ("_gate" if gated else ""), grid=(t // tt,), in_specs=in_specs, out_specs=out_specs,
        out_shape=out_shape, compiler_params=_params(("arbitrary",)))(x, dh, dres, g, sc, *(gate or ()))


def _gate_bwd(dxo, branch, gt, name):
    t, d = dxo.shape
    tt = _tile(t, _ROW_TILES)

    def body(dx_ref, br_ref, gt_ref, db_ref, dgt_ref):
        @pl.when(pl.program_id(0) == 0)
        def _():
            dgt_ref[...] = jnp.zeros_like(dgt_ref)

        dx = dx_ref[...]
        db_ref[...] = (dx * gt_ref[...]).astype(db_ref.dtype)
        dgt_ref[...] += jnp.sum(dx * br_ref[...], axis=0, keepdims=True)

    row = pl.BlockSpec((tt, d), lambda i: (i, 0))
    vec = pl.BlockSpec((1, d), lambda i: (0, 0))
    return pl.pallas_call(
        body, name=name, grid=(t // tt,), in_specs=[row, row, vec], out_specs=[row, vec],
        out_shape=[jax.ShapeDtypeStruct((t, d), MXU_DTYPE), jax.ShapeDtypeStruct((1, d), F32)],
        compiler_params=_params(("arbitrary",)))(dxo, branch, gt)


def _ffn_in_swiglu(h, w, name):
    t, k = h.shape
    f = w.shape[1] // 2
    tm, tn = _tile(t, _MM_TILES), _tile(f, _MM_TILES)
    nj = f // tn

    def body(h_ref, wg_ref, wu_ref, gu_ref, a_ref):
        hv = h_ref[...]
        gu_ref[0] = _mm(hv, wg_ref[...]).astype(gu_ref.dtype)
        gu_ref[1] = _mm(hv, wu_ref[...]).astype(gu_ref.dtype)
        a_ref[...] = (_silu(gu_ref[0].astype(F32)) * gu_ref[1].astype(F32)).astype(a_ref.dtype)

    return pl.pallas_call(
        body, name=name, grid=(t // tm, nj),
        in_specs=[pl.BlockSpec((tm, k), lambda i, j: (i, 0)), pl.BlockSpec((k, tn), lambda i, j: (0, j)),
                  pl.BlockSpec((k, tn), lambda i, j: (0, j + nj))],
        out_specs=[pl.BlockSpec((2, tm, tn), lambda i, j: (0, i, j)), pl.BlockSpec((tm, tn), lambda i, j: (i, j))],
        out_shape=[jax.ShapeDtypeStruct((2, t, f), ACT_DTYPE), jax.ShapeDtypeStruct((t, f), MXU_DTYPE)],
        compiler_params=_params(("parallel", "parallel")))(h, w, w)


def _ffn_out_dx_swiglu(dfo, w, gu, name):
    t, k = dfo.shape
    f = w.shape[0]
    tm, tn = _tile(t, _ROW_TILES), _tile(f, _MM_TILES)

    def body(d_ref, w_ref, gu_ref, o_ref):
        da = _mm(d_ref[...], w_ref[...], _NT)
        gate = gu_ref[0].astype(F32)
        sg = _sigmoid(gate)
        o_ref[0] = (da * gu_ref[1].astype(F32) * (sg * (1.0 + gate * (1.0 - sg)))).astype(o_ref.dtype)
        o_ref[1] = (da * (gate * sg)).astype(o_ref.dtype)

    planes = pl.BlockSpec((2, tm, tn), lambda i, j: (0, i, j))
    return pl.pallas_call(
        body, name=name, grid=(t // tm, f // tn),
        in_specs=[pl.BlockSpec((tm, k), lambda i, j: (i, 0)), pl.BlockSpec((tn, k), lambda i, j: (j, 0)), planes],
        out_specs=planes, out_shape=jax.ShapeDtypeStruct((2, t, f), MXU_DTYPE),
        compiler_params=_params(("parallel", "parallel")))(dfo, w, gu)


class _ProjLayout:
    def __init__(self, d):
        wc = 3 * HEADS * HEAD_DIM
        self.d, self.wc = d, wc
        self.qkv, self.z, self.uv, self.gates, self.ba = 0, wc, wc + d, wc + 3 * d, wc + 5 * d
        self.width = self.ba + LANES
        assert self.z % d == 0 and self.uv % (2 * d) == 0 and self.gates % (2 * d) == 0 and self.ba % LANES == 0

    def pieces(self, shard):
        d, wc, out, lo = self.d, self.wc, [], 0
        for length, dst in ((2 * d, self.uv), (wc, self.qkv), (d, self.z), (2 * HEADS, self.ba), (2 * d, self.gates)):
            pos = lo
            while pos < lo + length:
                j = pos // shard
                n = min(lo + length, (j + 1) * shard) - pos
                out.append((j, pos - j * shard, n, dst + pos - lo))
                pos += n
            lo += length
        return out


def _merge_fwd(pa, pb, proj, gcol, name):
    t, d = pa.shape
    tt = _tile(t, _ROW_TILES)

    def body(pa_ref, pb_ref, ga_ref, gb_ref, o_ref):
        sa, sb = _sigmoid(ga_ref[...].astype(F32)), _sigmoid(gb_ref[...].astype(F32))
        o_ref[...] = (sa * pa_ref[...] + sb * pb_ref[...]).astype(o_ref.dtype)

    row = pl.BlockSpec((tt, d), lambda i: (i, 0))
    gate = lambda k: pl.BlockSpec((tt, d), lambda i: (i, gcol // d + k))
    return pl.pallas_call(
        body, name=name, grid=(t // tt,), in_specs=[row, row, gate(0), gate(1)], out_specs=row,
        out_shape=jax.ShapeDtypeStruct((t, d), MXU_DTYPE), compiler_params=_params(("parallel",)))(pa, pb, proj, proj)


def _merge_bwd(dm, pa, pb, proj, gcol, dproj, name):
    t, d = pa.shape
    tt = _tile(t, _ROW_TILES)

    def body(dm_ref, pa_ref, pb_ref, ga_ref, gb_ref, _, dpa_ref, dpb_ref, dg_ref):
        dm = dm_ref[...]
        sa, sb = _sigmoid(ga_ref[...].astype(F32)), _sigmoid(gb_ref[...].astype(F32))
        dpa_ref[...] = (dm * sa).astype(dpa_ref.dtype)
        dpb_ref[...] = (dm * sb).astype(dpb_ref.dtype)
        dg_ref[:, :d] = (dm * pa_ref[...] * sa * (1.0 - sa)).astype(dg_ref.dtype)
        dg_ref[:, d:] = (dm * pb_ref[...] * sb * (1.0 - sb)).astype(dg_ref.dtype)

    row = pl.BlockSpec((tt, d), lambda i: (i, 0))
    gate = lambda k: pl.BlockSpec((tt, d), lambda i: (i, gcol // d + k))
    wide = pl.BlockSpec((tt, 2 * d), lambda i: (i, gcol // (2 * d)))
    return pl.pallas_call(
        body, name=name, grid=(t // tt,), in_specs=[row, row, row, gate(0), gate(1), _ANY], out_specs=[row, row, wide],
        out_shape=[jax.ShapeDtypeStruct((t, d), MXU_DTYPE), jax.ShapeDtypeStruct((t, d), MXU_DTYPE),
                   jax.ShapeDtypeStruct(dproj.shape, dproj.dtype)],
        input_output_aliases={5: 2}, compiler_params=_params(("parallel",)))(dm, pa, pb, proj, proj, dproj)


def _tri_masks(n):
    ri = lax.broadcasted_iota(jnp.int32, (n, n), 0)
    ci = lax.broadcasted_iota(jnp.int32, (n, n), 1)
    return ri >= ci, ri > ci, ri == ci


def _mixer_a_fwd(proj, ucol, w_s, b_col, g_v, name):
    t, w = proj.shape[0], g_v.shape[1]
    c = A_CHUNK

    def body(u_ref, v_ref, w_ref, b_ref, gv_ref, y_ref):
        tril, _, _ = _tri_masks(c)
        ug, vg = _gelu(u_ref[...].astype(F32)), _gelu(v_ref[...].astype(F32))
        for g in range(GROUPS):
            sl = slice(g * c, (g + 1) * c)
            vt = vg[:, sl]
            r = lax.rsqrt(jnp.mean(vt * vt, axis=-1, keepdims=True) + EPS)
            vn = vt * r * gv_ref[:, sl]
            s = _mm(jnp.where(tril, w_ref[g], 0.0), vn) + b_ref[g]
            y_ref[:, sl] = (ug[:, sl] * s).astype(y_ref.dtype)

    return pl.pallas_call(
        body, name=name, grid=(t // c,),
        in_specs=[pl.BlockSpec((c, w), lambda i: (i, ucol // w)), pl.BlockSpec((c, w), lambda i: (i, ucol // w + 1)),
                  pl.BlockSpec((GROUPS, c, c), lambda i: (0, 0, 0)), pl.BlockSpec((GROUPS, c, 1), lambda i: (0, 0, 0)),
                  pl.BlockSpec((1, w), lambda i: (0, 0))],
        out_specs=pl.BlockSpec((c, w), lambda i: (i, 0)), out_shape=jax.ShapeDtypeStruct((t, w), MXU_DTYPE),
        compiler_params=_params(("parallel",)))(proj, proj, w_s, b_col, g_v)


def _mixer_a_bwd(proj, ucol, dy, w_s, w_st, b_col, g_v, dproj, name):
    t, w = proj.shape[0], g_v.shape[1]
    w2 = 2 * w
    c = A_CHUNK

    def body(u_ref, v_ref, dy_ref, w_ref, wt_ref, b_ref, gv_ref, _, duv_ref, dw_ref, db_ref, dgv_ref):
        @pl.when(pl.program_id(0) == 0)
        def _():
            dw_ref[...] = jnp.zeros_like(dw_ref)
            db_ref[...] = jnp.zeros_like(db_ref)
            dgv_ref[...] = jnp.zeros_like(dgv_ref)

        tril, _, _ = _tri_masks(c)
        triu = lax.broadcasted_iota(jnp.int32, (c, c), 0) <= lax.broadcasted_iota(jnp.int32, (c, c), 1)
        (ug, dug), (vg, dvg) = _gelu_and_slope(u_ref[...].astype(F32)), _gelu_and_slope(v_ref[...].astype(F32))
        for g in range(GROUPS):
            sl = slice(g * c, (g + 1) * c)
            vt = vg[:, sl]
            r = lax.rsqrt(jnp.mean(vt * vt, axis=-1, keepdims=True) + EPS)
            vh = vt * r
            gv = gv_ref[:, sl]
            vn = vh * gv
            s = _mm(jnp.where(tril, w_ref[g], 0.0), vn) + b_ref[g]
            dy = dy_ref[:, sl]
            ds = dy * ug[:, sl]
            dw_ref[g] += jnp.where(tril, _mm(ds, vn, _NT), 0.0)
            db_ref[g] += jnp.sum(ds, axis=1, keepdims=True)
            dvn = _mm(jnp.where(triu, wt_ref[g], 0.0), ds)
            dgv_ref[:, sl] += jnp.sum(dvn * vh, axis=0, keepdims=True)
            dvh = dvn * gv
            dvt = r * (dvh - vh * jnp.mean(dvh * vh, axis=-1, keepdims=True))
            duv_ref[:, sl] = (dy * s * dug[:, sl]).astype(duv_ref.dtype)
            duv_ref[:, w + g * c:w + (g + 1) * c] = (dvt * dvg[:, sl]).astype(duv_ref.dtype)

    full3 = lambda shape: pl.BlockSpec(shape, lambda i: (0, 0, 0))
    return pl.pallas_call(
        body, name=name, grid=(t // c,),
        in_specs=[pl.BlockSpec((c, w), lambda i: (i, ucol // w)), pl.BlockSpec((c, w), lambda i: (i, ucol // w + 1)),
                  pl.BlockSpec((c, w), lambda i: (i, 0)), full3((GROUPS, c, c)), full3((GROUPS, c, c)),
                  full3((GROUPS, c, 1)), pl.BlockSpec((1, w), lambda i: (0, 0)), _ANY],
        out_specs=[pl.BlockSpec((c, w2), lambda i: (i, ucol // w2)), full3((GROUPS, c, c)), full3((GROUPS, c, 1)),
                   pl.BlockSpec((1, w), lambda i: (0, 0))],
        out_shape=[jax.ShapeDtypeStruct(dproj.shape, dproj.dtype), jax.ShapeDtypeStruct((GROUPS, c, c), F32),
                   jax.ShapeDtypeStruct((GROUPS, c, 1), F32), jax.ShapeDtypeStruct((1, w), F32)],
        input_output_aliases={7: 0},
        compiler_params=_params(("arbitrary",)))(proj, proj, dy, w_s, w_st, b_col, g_v, dproj)


_Q_SCALE = HEAD_DIM ** -0.5


CONV_HALO = 16


def _conv_taps(x_ref, p_ref, w_ref):
    prev = jnp.where(pl.program_id(0) > 0, p_ref[...].astype(F32), 0.0)
    ext = jnp.concatenate([prev, x_ref[...].astype(F32)], axis=0)
    shifted = [ext[CONV_HALO:]] + [pltpu.roll(ext, s, 0)[CONV_HALO:] for s in range(1, CONV_K)]
    acc = shifted[0] * w_ref[pl.ds(CONV_K - 1, 1), :]
    for s in range(1, CONV_K):
        acc = acc + shifted[s] * w_ref[pl.ds(CONV_K - 1 - s, 1), :]
    return acc, shifted


def _conv_fwd(qkv, w, name):
    t, cw = qkv.shape[0], w.shape[1]
    tt = _tile(t, (256, 128))
    hb = tt // CONV_HALO

    def body(x_ref, p_ref, w_ref, o_ref):
        acc, _ = _conv_taps(x_ref, p_ref, w_ref)
        y = _silu(acc)
        for which in range(3):
            for h in range(HEADS):
                lo = (which * HEADS + h) * HEAD_DIM
                seg = y[:, lo:lo + HEAD_DIM]
                if which < 2:
                    seg = seg * lax.rsqrt(jnp.sum(seg * seg, axis=-1, keepdims=True) + EPS)
                if which == 0:
                    seg = seg * _Q_SCALE
                o_ref[which, h] = seg

    return pl.pallas_call(
        body, name=name, grid=(t // tt,),
        in_specs=[pl.BlockSpec((tt, cw), lambda i: (i, 0)),
                  pl.BlockSpec((CONV_HALO, cw), lambda i: (jnp.maximum(i * hb - 1, 0), 0)),
                  pl.BlockSpec((CONV_K, cw), lambda i: (0, 0))],
        out_specs=pl.BlockSpec((3, HEADS, tt, HEAD_DIM), lambda i: (0, 0, i, 0)),
        out_shape=jax.ShapeDtypeStruct((3, HEADS, t, HEAD_DIM), F32),
        compiler_params=_params(("parallel",)))(qkv, qkv, w)


def _conv_bwd_pre(qkv, dq, dk, dv, w, name):
    t, cw = qkv.shape[0], w.shape[1]
    tt = _tile(t, (256, 128))
    hb = tt // CONV_HALO

    def body(x_ref, p_ref, dq_ref, dk_ref, dv_ref, w_ref, da_ref, dw_ref):
        @pl.when(pl.program_id(0) == 0)
        def _():
            dw_ref[...] = jnp.zeros_like(dw_ref)

        acc, shifted = _conv_taps(x_ref, p_ref, w_ref)
        sg = _sigmoid(acc)
        y = acc * sg
        dsilu = sg * (1.0 + acc * (1.0 - sg))
        d_refs = (dq_ref, dk_ref, dv_ref)
        for which in range(3):
            for h in range(HEADS):
                lo = (which * HEADS + h) * HEAD_DIM
                sl = slice(lo, lo + HEAD_DIM)
                dn = d_refs[which][h]
                if which < 2:
                    seg = y[:, sl]
                    rho = lax.rsqrt(jnp.sum(seg * seg, axis=-1, keepdims=True) + EPS)
                    nrm = seg * rho
                    if which == 0:
                        dn = dn * _Q_SCALE
                    dn = rho * (dn - nrm * jnp.sum(dn * nrm, axis=-1, keepdims=True))
                dacc = dn * dsilu[:, sl]
                da_ref[:, sl] = dacc
                for s in range(CONV_K):
                    dw_ref[pl.ds(CONV_K - 1 - s, 1), sl] += jnp.sum(dacc * shifted[s][:, sl], axis=0, keepdims=True)

    head = pl.BlockSpec((HEADS, tt, HEAD_DIM), lambda i: (0, i, 0))
    return pl.pallas_call(
        body, name=name, grid=(t // tt,),
        in_specs=[pl.BlockSpec((tt, cw), lambda i: (i, 0)),
                  pl.BlockSpec((CONV_HALO, cw), lambda i: (jnp.maximum(i * hb - 1, 0), 0)),
                  head, head, head, pl.BlockSpec((CONV_K, cw), lambda i: (0, 0))],
        out_specs=[pl.BlockSpec((tt, cw), lambda i: (i, 0)), pl.BlockSpec((CONV_K, cw), lambda i: (0, 0))],
        out_shape=[jax.ShapeDtypeStruct((t, cw), F32), jax.ShapeDtypeStruct((CONV_K, cw), F32)],
        compiler_params=_params(("arbitrary",)))(qkv, qkv, dq, dk, dv, w)


def _conv_bwd_in(dacc, w, dproj, name):
    t, cw = dacc.shape
    tt = _tile(t, (256, 128))
    hb = tt // SUBLANES
    nt = t // tt
    rows = tt + SUBLANES

    def body(d_ref, n_ref, w_ref, _, o_ref):
        cur = d_ref[...]
        nxt = jnp.where(pl.program_id(0) < nt - 1, n_ref[...], 0.0)
        ext = jnp.concatenate([cur, nxt], axis=0)
        acc = cur * w_ref[pl.ds(CONV_K - 1, 1), :]
        for s in range(1, CONV_K):
            acc = acc + pltpu.roll(ext, rows - s, 0)[:tt] * w_ref[pl.ds(CONV_K - 1 - s, 1), :]
        o_ref[...] = acc.astype(o_ref.dtype)

    return pl.pallas_call(
        body, name=name, grid=(nt,),
        in_specs=[pl.BlockSpec((tt, cw), lambda i: (i, 0)),
                  pl.BlockSpec((SUBLANES, cw), lambda i: (jnp.minimum((i + 1) * hb, t // SUBLANES - 1), 0)),
                  pl.BlockSpec((CONV_K, cw), lambda i: (0, 0)), _ANY],
        out_specs=pl.BlockSpec((tt, cw), lambda i: (i, 0)), out_shape=jax.ShapeDtypeStruct(dproj.shape, dproj.dtype),
        input_output_aliases={3: 0}, compiler_params=_params(("parallel",)))(dacc, dacc, w, dproj)


_INV_BASE_SHIFT = 3


def _inv_unit_lower(a, eye):
    c = GDN_CHUNK
    ri = lax.broadcasted_iota(jnp.int32, (c, c), 0)
    ci = lax.broadcasted_iota(jnp.int32, (c, c), 1)
    same = lambda sh: (ri >> sh) == (ci >> sh)
    x = jnp.where(same(_INV_BASE_SHIFT), -a, 0.0)
    p = jnp.where(eye, 1.0, 0.0) + x
    xs = _split(x)
    x2 = _mm3(xs, xs)
    x2s, ps = _split(x2), _split(p)
    r = _mm3(x2s, tuple(jnp.concatenate([u, v], axis=-1) for u, v in zip(x2s, ps)))
    x4, p = r[..., :c], p + r[..., c:]
    p = p + _mm3(_split(x4), _split(p))
    for sh in range(_INV_BASE_SHIFT, c.bit_length() - 1):
        off = jnp.where(same(sh + 1) & jnp.logical_not(same(sh)), a, 0.0)
        ps = _split(p)
        p = p - _mm3(ps, _split(_mm3(_split(off), ps)))
    return p


def _split(a):
    hi = a.astype(BF16)
    return hi, (a - hi.astype(F32)).astype(BF16)


def _dot_heads(u, v, dims):
    if u.ndim == 3:
        return jnp.stack([_dot_heads(u[j], v[j], dims) for j in range(u.shape[0])])
    return lax.dot_general(u, v, dims, preferred_element_type=F32)


def _mm3(a, b):
    return _dot_heads(a[0], b[0], _NN) + (_dot_heads(a[0], b[1], _NN) + _dot_heads(a[1], b[0], _NN))


def _hmm(a, b, dims=_NN):
    return _dot_heads(a.astype(MXU_DTYPE), b.astype(MXU_DTYPE), dims)


def _rowsum(x):
    return jnp.sum(x, axis=-1, keepdims=True)


def _colsum(x):
    return jnp.sum(x, axis=-2, keepdims=True)


class _Pre:
    pass


def _gdn_pre(q, k, v, araw, braw, alog, dtb, t_mat=None):
    c = GDN_CHUNK
    p = _Pre()
    p.tril, p.strict, p.eye = _tri_masks(c)
    p.to_col = lambda row: _rowsum(jnp.where(p.eye, row, 0.0))
    p.to_row = lambda col: _colsum(jnp.where(p.eye, col, 0.0))
    p.a_neg = -jnp.exp(alog + jnp.zeros((1, c), F32))
    p.xg = araw + dtb
    p.g_row = p.a_neg * _softplus(p.xg)
    p.beta_row = _sigmoid(braw)
    p.beta = p.to_col(p.beta_row)
    gam = _rowsum(jnp.where(p.tril, p.g_row, 0.0))
    gam_last = _rowsum(p.g_row)
    p.dm = jnp.where(p.tril, jnp.exp(jnp.where(p.tril, gam - p.to_row(gam), 0.0)), 0.0)
    p.e, p.ek, p.el = jnp.exp(gam), jnp.exp(gam_last - gam), jnp.exp(gam_last)
    p.kb = k * p.beta
    p.kk = _hmm(p.kb, k, _NT)
    p.t = _inv_unit_lower(jnp.where(p.strict, p.kk * p.dm, 0.0), p.eye) if t_mat is None else t_mat
    p.vb, p.kbe = v * p.beta, p.kb * p.e
    uw = _hmm(p.t, jnp.concatenate([p.vb, p.kbe], axis=-1))
    p.u, p.w = uw[..., :v.shape[-1]], uw[..., v.shape[-1]:]
    p.qk0 = _hmm(q, k, _NT)
    p.qk = p.qk0 * p.dm
    p.qd, p.kd = q * p.e, k * p.ek
    return p


GDN_HEADS_PER_STEP = 8


def _head_scalars(ref, hb):
    h0 = pl.program_id(0) * hb
    return jnp.stack([jnp.full((1, 1), ref[h0 + j], F32) for j in range(hb)])


def _gdn_specs(n, reverse):
    c, dk, hb = GDN_CHUNK, HEAD_DIM, GDN_HEADS_PER_STEP
    ix = (lambda i: n - 1 - i) if reverse else (lambda i: i)
    smem = pl.BlockSpec(memory_space=pltpu.SMEM)
    qkv = [pl.BlockSpec((None, hb, c, dk), functools.partial(lambda w, h, i: (w, h, ix(i), 0), w)) for w in range(3)]
    row = pl.BlockSpec((hb, None, 1, c), lambda h, i: (h, ix(i), 0, 0))
    tok = pl.BlockSpec((hb, c, dk), lambda h, i: (h, ix(i), 0))
    state = pl.BlockSpec((hb, None, dk, dk), lambda h, i: (h, ix(i), 0, 0))
    return smem, qkv, row, tok, state


def _gdn_fwd(qkv_h, araw, braw, alog, dtb, name, side=_NoSide):
    _, hh, t, dk = qkv_h.shape
    n, hb = t // GDN_CHUNK, GDN_HEADS_PER_STEP
    smem, qkv, row, tok, state = _gdn_specs(n, False)
    grid = (hh // hb, n)

    def body(*refs):
        main, side_start, side_finish = _side_hooks(side, refs, 7, 3, 1, grid)
        alog_ref, dt_ref, q_ref, k_ref, v_ref, a_ref, b_ref, o_ref, so_ref, to_ref, s_ref = main
        side_start()

        @pl.when(pl.program_id(1) == 0)
        def _():
            s_ref[...] = jnp.zeros_like(s_ref)

        p = _gdn_pre(q_ref[...], k_ref[...], v_ref[...], a_ref[...], b_ref[...],
                     _head_scalars(alog_ref, hb), _head_scalars(dt_ref, hb))
        s = s_ref[...]
        vn = p.u - _hmm(p.w, s)
        o_ref[...] = _hmm(p.qd, s) + _hmm(p.qk, vn)
        so_ref[...] = s
        to_ref[...] = p.t
        s_ref[...] = s * p.el + _hmm(p.kd, vn, _TN)
        side_finish()

    mats = jax.ShapeDtypeStruct((hh, n, dk, dk), F32)
    return _carrier_call(
        body, name, grid, [smem, smem] + qkv + [row, row], [tok, state, state],
        [jax.ShapeDtypeStruct((hh, t, dk), F32), mats, mats],
        [pltpu.VMEM((hb, dk, dk), F32)], side, (alog, dtb, qkv_h, qkv_h, qkv_h, araw, braw))


def _gdn_bwd(qkv_h, araw, braw, alog, dtb, states, t_mats, do, name, side=_NoSide):
    _, hh, t, dk = qkv_h.shape
    c, hb = GDN_CHUNK, GDN_HEADS_PER_STEP
    n = t // c
    smem, qkv, row, tok, state = _gdn_specs(n, True)
    acc = pl.BlockSpec((hb, 1, LANES), lambda h, i: (h, 0, 0))
    grid = (hh // hb, n)

    def body(*refs):
        main, side_start, side_finish = _side_hooks(side, refs, 10, 7, 1, grid)
        (alog_ref, dt_ref, q_ref, k_ref, v_ref, a_ref, b_ref, s_ref, t_ref, do_ref,
         dq_ref, dk_ref, dv_ref, da_ref, db_ref, dal_ref, ddt_ref, ds_ref) = main
        side_start()

        @pl.when(pl.program_id(1) == 0)
        def _():
            ds_ref[...] = jnp.zeros_like(ds_ref)
            dal_ref[...] = jnp.zeros_like(dal_ref)
            ddt_ref[...] = jnp.zeros_like(ddt_ref)

        q, k, v = q_ref[...], k_ref[...], v_ref[...]
        p = _gdn_pre(q, k, v, a_ref[...], b_ref[...], _head_scalars(alog_ref, hb), _head_scalars(dt_ref, hb),
                     t_ref[...])
        s, do, dsp = s_ref[...], do_ref[...], ds_ref[...]
        vn = p.u - _hmm(p.w, s)
        dqd = _hmm(do, s, _NT)
        dqk = _hmm(do, vn, _NT)
        dvn = _hmm(p.qk, do, _TN) + _hmm(p.kd, dsp)
        dkd = _hmm(vn, dsp, _NT)
        d_el = _colsum(_rowsum(s * dsp))
        ds_ref[...] = dsp * p.el + _hmm(p.qd, do, _TN) - _hmm(p.w, dvn, _TN)
        dw = -_hmm(dvn, s, _NT)
        d_t = _hmm(dvn, p.vb, _NT) + _hmm(dw, p.kbe, _NT)
        dvb, dkbe = _hmm(p.t, dvn, _TN), _hmm(p.t, dw, _TN)
        d_a = jnp.where(p.strict, -_hmm(p.t, _hmm(d_t, p.t, _NT), _TN), 0.0)
        dkk = d_a * p.dm
        dqk0 = dqk * p.dm
        ddm = d_a * p.kk + dqk * p.qk0
        dkb = _hmm(dkk, k) + dkbe * p.e
        dq_ref[...] = _hmm(dqk0, k) + dqd * p.e
        dk_ref[...] = _hmm(dkk, p.kb, _TN) + _hmm(dqk0, q, _TN) + dkd * p.ek + dkb * p.beta
        dv_ref[...] = dvb * p.beta
        dbeta = _rowsum(dkb * k) + _rowsum(dvb * v)
        d_e = _rowsum(dqd * q) + _rowsum(dkbe * p.kb)
        d_ek = _rowsum(dkd * k)
        m = ddm * p.dm
        dgam = d_e * p.e - d_ek * p.ek + _rowsum(m) - p.to_col(_colsum(m))
        dgam_last = _colsum(d_ek * p.ek) + d_el * p.el
        dg_row = _colsum(jnp.where(p.tril, dgam, 0.0)) + dgam_last
        da_row = dg_row * p.a_neg * _sigmoid(p.xg)
        da_ref[...] = da_row
        db_ref[...] = p.to_row(dbeta) * p.beta_row * (1.0 - p.beta_row)
        dal_ref[...] += _rowsum(dg_row * p.g_row)
        ddt_ref[...] += _rowsum(da_row)
        side_finish()

    tok_shape = jax.ShapeDtypeStruct((hh, t, dk), F32)
    row_shape = jax.ShapeDtypeStruct((hh, n, 1, c), F32)
    acc_shape = jax.ShapeDtypeStruct((hh, 1, LANES), F32)
    return _carrier_call(
        body, name, grid, [smem, smem] + qkv + [row, row, state, state, tok], [tok, tok, tok, row, row, acc, acc],
        [tok_shape, tok_shape, tok_shape, row_shape, row_shape, acc_shape, acc_shape],
        [pltpu.VMEM((hb, dk, dk), F32)], side, (alog, dtb, qkv_h, qkv_h, qkv_h, araw, braw, states, t_mats, do))


def _gdn_post_fwd(o, proj, zcol, g_o, name):
    hh, t, dv = o.shape
    tt = _tile(t, _ROW_TILES)
    zblk = zcol // (hh * dv)

    def body(o_ref, z_ref, g_ref, y_ref):
        for h in range(hh):
            sl = slice(h * dv, (h + 1) * dv)
            ov = o_ref[h]
            r = lax.rsqrt(jnp.mean(ov * ov, axis=-1, keepdims=True) + EPS)
            y_ref[:, sl] = (ov * r * g_ref[...] * _silu(z_ref[:, sl].astype(F32))).astype(y_ref.dtype)

    return pl.pallas_call(
        body, name=name, grid=(t // tt,),
        in_specs=[pl.BlockSpec((hh, tt, dv), lambda i: (0, i, 0)), pl.BlockSpec((tt, hh * dv), lambda i: (i, zblk)),
                  pl.BlockSpec((1, dv), lambda i: (0, 0))],
        out_specs=pl.BlockSpec((tt, hh * dv), lambda i: (i, 0)),
        out_shape=jax.ShapeDtypeStruct((t, hh * dv), MXU_DTYPE), compiler_params=_params(("parallel",)))(o, proj, g_o)


def _gdn_post_bwd(o, proj, zcol, dy, g_o, dproj, name):
    hh, t, dv = o.shape
    tt = _tile(t, _ROW_TILES)
    zblk = zcol // (hh * dv)

    def body(o_ref, z_ref, dy_ref, g_ref, _, do_ref, dz_ref, dg_ref):
        @pl.when(pl.program_id(0) == 0)
        def _():
            dg_ref[...] = jnp.zeros_like(dg_ref)

        gv = g_ref[...]
        for h in range(hh):
            sl = slice(h * dv, (h + 1) * dv)
            ov, zz, dy = o_ref[h], z_ref[:, sl].astype(F32), dy_ref[:, sl]
            r = lax.rsqrt(jnp.mean(ov * ov, axis=-1, keepdims=True) + EPS)
            oh = ov * r
            sg = _sigmoid(zz)
            dz_ref[:, sl] = (dy * oh * gv * (sg * (1.0 + zz * (1.0 - sg)))).astype(dz_ref.dtype)
            don = dy * (zz * sg)
            dg_ref[...] += _colsum(don * oh)
            doh = don * gv
            do_ref[h] = r * (doh - oh * jnp.mean(doh * oh, axis=-1, keepdims=True))

    return pl.pallas_call(
        body, name=name, grid=(t // tt,),
        in_specs=[pl.BlockSpec((hh, tt, dv), lambda i: (0, i, 0)), pl.BlockSpec((tt, hh * dv), lambda i: (i, zblk)),
                  pl.BlockSpec((tt, hh * dv), lambda i: (i, 0)), pl.BlockSpec((1, dv), lambda i: (0, 0)), _ANY],
        out_specs=[pl.BlockSpec((hh, tt, dv), lambda i: (0, i, 0)), pl.BlockSpec((tt, hh * dv), lambda i: (i, zblk)),
                   pl.BlockSpec((1, dv), lambda i: (0, 0))],
        out_shape=[jax.ShapeDtypeStruct((hh, t, dv), F32), jax.ShapeDtypeStruct(dproj.shape, dproj.dtype),
                   jax.ShapeDtypeStruct((1, dv), F32)],
        input_output_aliases={4: 1}, compiler_params=_params(("arbitrary",)))(o, proj, dy, g_o, dproj)


def _cols_as_rows(x, col, name):
    t = x.shape[0]
    tt = _tile(t, _ROW_TILES)

    def body(x_ref, o_ref):
        o_ref[...] = x_ref[...].T

    return pl.pallas_call(
        body, name=name, grid=(t // tt,), in_specs=[pl.BlockSpec((tt, LANES), lambda i: (i, col // LANES))],
        out_specs=pl.BlockSpec((LANES, tt), lambda i: (0, i)), out_shape=jax.ShapeDtypeStruct((LANES, t), x.dtype),
        compiler_params=_params(("parallel",)))(x)


def _rows_into_cols(dst, rows, col, name):
    t = dst.shape[0]
    tt = _tile(t, _ROW_TILES)

    def body(r_ref, _, o_ref):
        o_ref[...] = r_ref[...].T.astype(o_ref.dtype)

    return pl.pallas_call(
        body, name=name, grid=(t // tt,), in_specs=[pl.BlockSpec((LANES, tt), lambda i: (0, i)), _ANY],
        out_specs=pl.BlockSpec((tt, LANES), lambda i: (i, col // LANES)),
        out_shape=jax.ShapeDtypeStruct(dst.shape, dst.dtype), input_output_aliases={1: 0},
        compiler_params=_params(("parallel",)))(rows, dst)


def _adamw(g, w, m, v):
    m = ADAM_B1 * m + (1.0 - ADAM_B1) * g
    v = ADAM_B2 * v + (1.0 - ADAM_B2) * (g * g)
    m_hat = m / (1.0 - ADAM_B1 ** ADAM_STEP)
    v_hat = v / (1.0 - ADAM_B2 ** ADAM_STEP)
    return -ADAM_LR * (m_hat / (jnp.sqrt(v_hat) + ADAM_EPS) + ADAM_WD * w), m, v


def _ada_fwd(c_all, ada_w, name):
    nl, d, cols = ada_w.shape
    b = c_all.shape[0]

    def body(c_ref, w_ref, o_ref):
        o_ref[...] = _mm_hi(_silu(c_ref[...]), w_ref[...])

    return pl.pallas_call(
        body, name=name, grid=(nl,),
        in_specs=[pl.BlockSpec((b, d), lambda i: (0, 0)), pl.BlockSpec((None, d, cols), lambda i: (i, 0, 0))],
        out_specs=pl.BlockSpec((None, b, cols), lambda i: (i, 0, 0)),
        out_shape=jax.ShapeDtypeStruct((nl, b, cols), F32), compiler_params=_params(("parallel",)))(c_all, ada_w)


def _ada_bwd(c_col, dm, w, m, v, name):
    nl, d, cols = w.shape
    b = c_col.shape[0]
    tr = _tile(d, (256, 128))

    def body(c_ref, dm_ref, w_ref, m_ref, v_ref, g_ref, dl_ref, mo_ref, vo_ref):
        g = _silu(c_ref[0]) * dm_ref[pl.ds(0, 1), :]
        for j in range(1, b):
            g = g + _silu(c_ref[j]) * dm_ref[pl.ds(j, 1), :]
        g_ref[...] = g
        dl_ref[...], mo_ref[...], vo_ref[...] = _adamw(g, w_ref[...], m_ref[...], v_ref[...])

    blk = pl.BlockSpec((None, tr, cols), lambda l, i: (l, i, 0))
    shape = jax.ShapeDtypeStruct((nl, d, cols), F32)
    return pl.pallas_call(
        body, name=name, grid=(nl, d // tr),
        in_specs=[pl.BlockSpec((b, tr, 1), lambda l, i: (0, i, 0)), pl.BlockSpec((None, b, cols), lambda l, i: (l, 0, 0)),
                  blk, blk, blk],
        out_specs=[blk, blk, blk, blk], out_shape=[shape] * 4,
        compiler_params=_params(("parallel", "parallel")))(c_col, dm, w, m, v)


_GRAD_ROW_TILES = (256, 128, 176, 88)


def _sum_adam(parts, w, m, v, name):
    nl, npart, r, cdim = parts.shape
    tr = _tile(r, _GRAD_ROW_TILES)

    def body(p_ref, w_ref, m_ref, v_ref, g_ref, dl_ref, mo_ref, vo_ref):
        g = p_ref[0].astype(F32)
        for j in range(1, npart):
            g = g + p_ref[j].astype(F32)
        g_ref[...] = g
        dl_ref[...], mo_ref[...], vo_ref[...] = _adamw(g, w_ref[...], m_ref[...], v_ref[...])

    blk = pl.BlockSpec((None, tr, cdim), lambda l, i: (l, i, 0))
    shape = jax.ShapeDtypeStruct((nl, r, cdim), F32)
    return pl.pallas_call(
        body, name=name, grid=(nl, r // tr),
        in_specs=[pl.BlockSpec((None, npart, tr, cdim), lambda l, i: (l, 0, i, 0)), blk, blk, blk],
        out_specs=[blk, blk, blk, blk], out_shape=[shape] * 4,
        compiler_params=_params(("parallel", "parallel")))(parts, w, m, v)


def _cols_from_blocks(g, plan, width, name):
    _, r, cdim = g.shape
    tr = _tile(r, (256, 128))
    covered = sorted((dst, dst + n) for _, _, n, dst in plan)
    holes, pos = [], 0
    for a, b in covered:
        if a > pos:
            holes.append((pos, a))
        pos = max(pos, b)
    if pos < width:
        holes.append((pos, width))

    def body(g_ref, o_ref):
        for a, b in holes:
            o_ref[:, a:b] = jnp.zeros((tr, b - a), o_ref.dtype)
        for j, src, n, dst in plan:
            o_ref[:, dst:dst + n] = g_ref[j, :, src:src + n]

    return pl.pallas_call(
        body, name=name, grid=(r // tr,), in_specs=[pl.BlockSpec((N_DEV, tr, cdim), lambda i: (0, i, 0))],
        out_specs=pl.BlockSpec((tr, width), lambda i: (i, 0)), out_shape=jax.ShapeDtypeStruct((r, width), g.dtype),
        compiler_params=_params(("parallel",)))(g)


def _blocks_from_cols(w, plan, cdim, name):
    r, width = w.shape
    tr = _tile(r, (256, 128))

    def body(w_ref, o_ref):
        for j, src, n, dst in plan:
            o_ref[j, :, src:src + n] = w_ref[:, dst:dst + n]

    return pl.pallas_call(
        body, name=name, grid=(r // tr,), in_specs=[pl.BlockSpec((tr, width), lambda i: (i, 0))],
        out_specs=pl.BlockSpec((N_DEV, tr, cdim), lambda i: (0, i, 0)),
        out_shape=jax.ShapeDtypeStruct((N_DEV, r, cdim), w.dtype), compiler_params=_params(("parallel",)))(w)


def _pair_sum(x, tmp, core, name):
    _, r, cdim = x.shape
    tr = _tile(r, _GRAD_ROW_TILES)

    def body(core_ref, x_ref, t_ref, o_ref):
        o_ref[...] = (x_ref[...] + t_ref[...]).astype(o_ref.dtype)

    grid_spec = pltpu.PrefetchScalarGridSpec(
        num_scalar_prefetch=1, grid=(N_DEV // 2, r // tr),
        in_specs=[pl.BlockSpec((None, tr, cdim), lambda ch, i, core_ref: (2 * ch + core_ref[0], i, 0)),
                  pl.BlockSpec((None, tr, cdim), lambda ch, i, core_ref: (ch, i, 0))],
        out_specs=pl.BlockSpec((None, tr, cdim), lambda ch, i, core_ref: (ch, i, 0)))
    return pl.pallas_call(
        body, name=name, grid_spec=grid_spec, out_shape=jax.ShapeDtypeStruct((N_DEV // 2, r, cdim), WIRE_DTYPE),
        compiler_params=_params(("parallel", "parallel")))(core, x, tmp)


_ANY = pl.BlockSpec(memory_space=pl.ANY)
_CHIP_FLIPS = ((1, 0), (0, 1), (1, 1))


def _coords():
    return lax.axis_index("x"), lax.axis_index("y"), lax.axis_index("c")


def _flip(v, f):
    return 1 - v if f else v


def _a2a_direct(xs, name):
    n, ncp = len(xs), N_DEV - 1

    def body(*refs):
        ins, outs = refs[:n], refs[n:2 * n]
        send, recv, loc = refs[2 * n:]
        x, y, c = _coords()
        me = 4 * x + 2 * y + c
        local = [pltpu.make_async_copy(ins[i].at[me], outs[i].at[me], loc.at[i]) for i in range(n)]
        for cp in local:
            cp.start()
        remote = []
        for i in range(n):
            for k in range(1, N_DEV):
                px, py, pc = _flip(x, k & 4), _flip(y, k & 2), _flip(c, k & 1)
                cp = pltpu.make_async_remote_copy(
                    src_ref=ins[i].at[4 * px + 2 * py + pc], dst_ref=outs[i].at[me],
                    send_sem=send.at[i * ncp + k - 1], recv_sem=recv.at[i * ncp + k - 1],
                    device_id=(px, py, pc), device_id_type=MESH)
                cp.start()
                remote.append(cp)
        for cp in remote:
            cp.wait()
        for cp in local:
            cp.wait()

    return pl.pallas_call(
        body, name=name, in_specs=[_ANY] * n, out_specs=[_ANY] * n,
        out_shape=[jax.ShapeDtypeStruct(a.shape, a.dtype) for a in xs],
        scratch_shapes=[pltpu.SemaphoreType.DMA((n * ncp,)), pltpu.SemaphoreType.DMA((n * ncp,)),
                        pltpu.SemaphoreType.DMA((n,))])(*xs)


class _AllGatherSide:
    def __init__(self, blocks):
        self.operands = list(blocks)
        n = self.n = len(self.operands)
        self.n_in = self.n_out = n
        self.out_shape = [jax.ShapeDtypeStruct((N_DEV,) + a.shape, a.dtype) for a in self.operands]
        self.aliases = {}
        nici, nd2d = len(_CHIP_FLIPS), N_DEV // 2
        self.scratch = [pltpu.SemaphoreType.DMA((n * nici,)), pltpu.SemaphoreType.DMA((n * nici,)),
                        pltpu.SemaphoreType.DMA((n * nd2d,)), pltpu.SemaphoreType.DMA((n * nd2d,)),
                        pltpu.SemaphoreType.DMA((n,))]

    def _first(self, ins, outs, sems):
        send, recv, _, _, loc = sems
        x, y, c = _coords()
        me = 4 * x + 2 * y + c
        nici = len(_CHIP_FLIPS)
        local = [pltpu.make_async_copy(ins[i], outs[i].at[me], loc.at[i]) for i in range(self.n)]
        remote = [pltpu.make_async_remote_copy(
            src_ref=ins[i], dst_ref=outs[i].at[me], send_sem=send.at[i * nici + j], recv_sem=recv.at[i * nici + j],
            device_id=(_flip(x, fx), _flip(y, fy), c), device_id_type=MESH)
            for i in range(self.n) for j, (fx, fy) in enumerate(_CHIP_FLIPS)]
        return local + remote

    def _second(self, outs, sems):
        _, _, send, recv, _ = sems
        x, y, c = _coords()
        nd2d = N_DEV // 2
        return [pltpu.make_async_remote_copy(
            src_ref=outs[i].at[2 * ch + c], dst_ref=outs[i].at[2 * ch + c], send_sem=send.at[i * nd2d + ch],
            recv_sem=recv.at[i * nd2d + ch], device_id=(x, y, 1 - c), device_id_type=MESH)
            for i in range(self.n) for ch in range(nd2d)]

    def start(self, ins, outs, sems):
        for cp in self._first(ins, outs, sems):
            cp.start()

    def finish(self, ins, outs, sems):
        for cp in self._first(ins, outs, sems):
            cp.wait()
        second = self._second(outs, sems)
        for cp in second:
            cp.start()
        for cp in second:
            cp.wait()


class _ReduceScatterIciSide:
    def __init__(self, sums, accs, layer):
        self.operands = list(sums) + list(accs)
        n = self.n = len(sums)
        self.layer = layer
        self.n_in, self.n_out = 2 * n, n
        self.out_shape = [jax.ShapeDtypeStruct(a.shape, a.dtype) for a in accs]
        self.aliases = {n + i: i for i in range(n)}
        nici = len(_CHIP_FLIPS)
        self.scratch = [pltpu.SemaphoreType.DMA((n * nici,)), pltpu.SemaphoreType.DMA((n * nici,)),
                        pltpu.SemaphoreType.DMA((n,))]

    def _copies(self, ins, outs, sems):
        send, recv, loc = sems
        x, y, c = _coords()
        chip = 2 * x + y
        nici = len(_CHIP_FLIPS)
        local = [pltpu.make_async_copy(ins[i].at[chip], outs[i].at[self.layer, chip], loc.at[i])
                 for i in range(self.n)]
        remote = [pltpu.make_async_remote_copy(
            src_ref=ins[i].at[2 * _flip(x, fx) + _flip(y, fy)], dst_ref=outs[i].at[self.layer, chip],
            send_sem=send.at[i * nici + j], recv_sem=recv.at[i * nici + j],
            device_id=(_flip(x, fx), _flip(y, fy), c), device_id_type=MESH)
            for i in range(self.n) for j, (fx, fy) in enumerate(_CHIP_FLIPS)]
        return local + remote

    def start(self, ins, outs, sems):
        for cp in self._copies(ins, outs, sems):
            cp.start()

    def finish(self, ins, outs, sems):
        for cp in self._copies(ins, outs, sems):
            cp.wait()


def _run_side(side, name):
    def body(*refs):
        ins, outs = refs[:side.n_in], refs[side.n_in:side.n_in + side.n_out]
        sems = refs[side.n_in + side.n_out:]
        side.start(ins, outs, sems)
        side.finish(ins, outs, sems)

    return pl.pallas_call(
        body, name=name, in_specs=[_ANY] * side.n_in, out_specs=[_ANY] * side.n_out, out_shape=side.out_shape,
        input_output_aliases=side.aliases, scratch_shapes=side.scratch)(*side.operands)


class _ReduceScatterD2dSide:
    def __init__(self, parts):
        self.operands = list(parts)
        n = self.n = len(self.operands)
        self.n_in = self.n_out = n
        nd2d = N_DEV // 2
        self.out_shape = [jax.ShapeDtypeStruct((nd2d,) + a.shape[1:], a.dtype) for a in self.operands]
        self.aliases = {}
        self.scratch = [pltpu.SemaphoreType.DMA((n * nd2d,)), pltpu.SemaphoreType.DMA((n * nd2d,))]

    def _copies(self, ins, outs, sems):
        send, recv = sems
        x, y, c = _coords()
        nd2d = N_DEV // 2
        return [pltpu.make_async_remote_copy(
            src_ref=ins[i].at[2 * ch + 1 - c], dst_ref=outs[i].at[ch], send_sem=send.at[i * nd2d + ch],
            recv_sem=recv.at[i * nd2d + ch], device_id=(x, y, 1 - c), device_id_type=MESH)
            for i in range(self.n) for ch in range(nd2d)]

    def start(self, ins, outs, sems):
        for cp in self._copies(ins, outs, sems):
            cp.start()

    def finish(self, ins, outs, sems):
        for cp in self._copies(ins, outs, sems):
            cp.wait()


_PACK_ROWS = 256


def _pack(arrs):
    flat = jnp.concatenate([a.reshape(-1) for a in arrs])
    quantum = _PACK_ROWS * LANES
    total = -(-flat.shape[0] // quantum) * quantum
    return jnp.pad(flat, (0, total - flat.shape[0])).reshape(-1, LANES)


def _unpack(packed, like):
    flat, out, pos = packed.reshape(-1), [], 0
    for a in like:
        out.append(flat[pos:pos + a.size].reshape(a.shape))
        pos += a.size
    return out


def kernel(x, c, ada_w, ada_b, norm1_g, w_in, conv_w, spatial_w, spatial_b, v_norm_g, a_log, dt_bias, o_norm_g, w_branch_a, w_branch_b, w_out, norm2_g, w_ffn_in, w_ffn_out, final_g, loss_target, m_ada_w, m_ada_b, m_norm1_g, m_w_in, m_conv_w, m_spatial_w, m_spatial_b, m_v_norm_g, m_a_log, m_dt_bias, m_o_norm_g, m_w_branch_a, m_w_branch_b, m_w_out, m_norm2_g, m_w_ffn_in, m_w_ffn_out, m_final_g, v_ada_w, v_ada_b, v_norm1_g, v_w_in, v_conv_w, v_spatial_w, v_spatial_b, v_v_norm_g, v_a_log, v_dt_bias, v_o_norm_g, v_w_branch_a, v_w_branch_b, v_w_out, v_norm2_g, v_w_ffn_in, v_w_ffn_out, v_final_g):
    nl, d = ada_w.shape[0], x.shape[2]
    t = x.shape[1]
    nchunk = t // GDN_CHUNK
    xi, yi, ci = _coords()
    me = 4 * xi + 2 * yi + ci
    core = jnp.reshape(ci, (1,)).astype(jnp.int32)
    x0, target = x[0], loss_target[0]
    wcols = 3 * HEADS * HEAD_DIM
    lay = _ProjLayout(d)
    in_pieces = lay.pieces(w_in.shape[2])
    fi_shard = w_ffn_in.shape[2]
    fi_pieces = [(j, 0, fi_shard, fi_shard * j) for j in range(N_DEV)]

    c_all, cw_all = _a2a_direct([jnp.broadcast_to(c[None], (N_DEV,) + c.shape),
                                 jnp.broadcast_to(conv_w[None], (N_DEV,) + conv_w.shape)], "gather_small")
    c_all = c_all[:, 0]
    conv_full = cw_all.transpose(1, 2, 0, 3).reshape(nl, CONV_K, wcols)
    modp = _ada_fwd(c_all, ada_w, "ada_fwd")
    (modx,) = _a2a_direct([modp.transpose(1, 0, 2)], "mod_exchange")
    mod = (modx.transpose(1, 0, 2).reshape(nl, 6 * d) + ada_b).reshape(nl, 6, 1, d)

    big = (w_in, w_branch_a, w_branch_b, w_out, w_ffn_in, w_ffn_out)
    big_wire = [w.astype(WIRE_DTYPE) for w in big]
    gather_in = lambda i: _AllGatherSide([big_wire[0][i]])
    gather_early = lambda i: _AllGatherSide([big_wire[k][i] for k in (1, 2, 3, 5)])
    gather_late = lambda i: _AllGatherSide([big_wire[4][i]] + ([big_wire[0][i + 1]] if i + 1 < nl else []))
    row_full = lambda g: g.reshape(-1, g.shape[2])
    padded_in = lambda g: _cols_from_blocks(g, in_pieces, lay.width, "w_in_cols")
    w_pads = [padded_in(_run_side(gather_in(0), "ag_first")[0])] + [None] * (nl - 1)
    weights = [None] * nl

    def rows_of(ba_rows, lo):
        return ba_rows[lo:lo + HEADS].reshape(HEADS, nchunk, 1, GDN_CHUNK)

    saved = []
    x_cur, delta, gt_prev = x0, None, None
    for i in range(nl):
        sh1, sc1, gt1, sh2, sc2, gt2 = (mod[i, k] for k in range(6))
        s = dict(gt1=gt1, gt2=gt2, sc1=sc1, sc2=sc2)
        s["x_in"], s["h"] = _resid_norm(x_cur, delta, gt_prev, norm1_g[i][None], sc1, sh1, "norm1_fwd")
        s["proj"], g_a, g_b, g_o, g_fo = _matmul(s["h"], w_pads[i], "nn", "proj_fwd", out_dtype=ACT_DTYPE,
                                                 side=gather_early(i))
        ba = _matmul(s["h"], w_pads[i][:, lay.ba:], "nn", "proj_ba_fwd")
        s["b_col"] = spatial_b[i][:, :, None]
        s["ya"] = _mixer_a_fwd(s["proj"], lay.uv, spatial_w[i], s["b_col"], v_norm_g[i][None], "mixer_a_fwd")
        s["qkv_h"] = _conv_fwd(s["proj"], conv_full[i], "conv_fwd")
        ba_rows = _cols_as_rows(ba, 0, "ba_rows")
        s["braw"], s["araw"] = rows_of(ba_rows, 0), rows_of(ba_rows, HEADS)
        s["o"], s["states"], s["t_mats"], g_fi, *g_in = _gdn_fwd(
            s["qkv_h"], s["araw"], s["braw"], a_log[i], dt_bias[i], "gdn_fwd", gather_late(i))
        if g_in:
            w_pads[i + 1] = padded_in(g_in[0])
        weights[i] = (row_full(g_a), row_full(g_b), row_full(g_o),
                      _cols_from_blocks(g_fi, fi_pieces, N_DEV * fi_shard, "w_ffn_in_cols"), row_full(g_fo))
        w_a, w_b, w_o, w_fi, w_fo = weights[i]
        s["yb"] = _gdn_post_fwd(s["o"], s["proj"], lay.z, o_norm_g[i][None], "gdn_post_fwd")
        s["pa"] = _matmul(s["ya"], w_a, "nn", "branch_a_fwd")
        s["pb"] = _matmul(s["yb"], w_b, "nn", "branch_b_fwd")
        s["merged"] = _merge_fwd(s["pa"], s["pb"], s["proj"], lay.gates, "merge_fwd")
        s["mo"] = _matmul(s["merged"], w_o, "nn", "out_fwd")
        s["x1"], s["h2"] = _resid_norm(s["x_in"], s["mo"], gt1, norm2_g[i][None], sc2, sh2, "norm2_fwd")
        s["gu"], s["a"] = _ffn_in_swiglu(s["h2"], w_fi, "ffn_in_swiglu_fwd")
        s["fo"] = _matmul(s["a"], w_fo, "nn", "ffn_out_fwd")
        saved.append(s)
        x_cur, delta, gt_prev = s["x1"], s["fo"], gt2
    dx, d_final_g, loss_tile = _final_loss(x_cur, delta, gt_prev, final_g[None], target, "final_loss")
    loss = lax.psum(loss_tile[0, 0], ("x", "y", "c"))

    big_shapes = [(d, w_in.shape[2]), w_branch_a.shape[1:], w_branch_b.shape[1:], w_out.shape[1:],
                  (w_ffn_in.shape[2], d), w_ffn_out.shape[1:]]
    accs = [lax.empty((nl, N_DEV // 2) + tuple(sh), WIRE_DTYPE) for sh in big_shapes]
    row_blocks = lambda g: g.reshape(N_DEV, -1, g.shape[1])
    dmod, small = [None] * nl, [None] * nl
    d_conv = [None] * nl
    parts, sums = None, None
    beside_gdn, beside_dw, beside_dx = (0,), (4,), (1, 2, 3, 5)
    rep_parts = [None] * nl
    rep_pack = lambda i: _pack((dmod[i],) + small[i])

    def scatter_side(idx, layer):
        if sums is None:
            return _NoSide
        return _ReduceScatterIciSide([sums[k] for k in idx], [accs[k] for k in idx], layer)

    def scattered_into(accs, idx, new):
        accs = list(accs)
        for k, a in zip(idx, new):
            accs[k] = a
        return accs

    for i in reversed(range(nl)):
        s = saved[i]
        w_a, w_b, w_o, w_fi, w_fo = weights[i]
        if i == nl - 1:
            dfo, dgt2 = _gate_bwd(dx, s["fo"], s["gt2"], "gate2_bwd")
        else:
            dgt2 = dgt2_before
        g_fo = _matmul(s["a"], dfo, "tn", "ffn_out_dw")
        dgu = _ffn_out_dx_swiglu(dfo, w_fo, s["gu"], "ffn_out_dx_swiglu")
        if parts is None:
            g_fi = _matmul(dgu, s["h2"], "tn", "ffn_in_dw")
        else:
            g_fi, *other = _matmul(dgu, s["h2"], "tn", "ffn_in_dw", side=_ReduceScatterD2dSide(parts))
            sums = [_pair_sum(p, o, core, "rs_pair_sum_%d" % k) for k, (p, o) in enumerate(zip(parts, other))]
        if i + 1 < nl:
            dh2, rep_parts[i + 1] = _matmul(dgu, w_fi, "nt", "ffn_in_dx", side=_AllGatherSide([rep_pack(i + 1)]))
        else:
            dh2 = _matmul(dgu, w_fi, "nt", "ffn_in_dx")
        dx1, dsh2, dsc2, dg2, dmo, dgt1 = _norm_bwd(s["x1"], dh2, dx, norm2_g[i][None], s["sc2"], "norm2_bwd",
                                                    gate=(s["mo"], s["gt1"]))
        g_o = _matmul(s["merged"], dmo, "tn", "out_dw")
        dmerged = _matmul(dmo, w_o, "nt", "out_dx")
        dproj = lax.empty((t, lay.width), MXU_DTYPE)
        dpa, dpb, dproj = _merge_bwd(dmerged, s["pa"], s["pb"], s["proj"], lay.gates, dproj, "merge_bwd")
        g_a = _matmul(s["ya"], dpa, "tn", "branch_a_dw")
        dya = _matmul(dpa, w_a, "nt", "branch_a_dx")
        g_b = _matmul(s["yb"], dpb, "tn", "branch_b_dw")
        dyb = _matmul(dpb, w_b, "nt", "branch_b_dx")
        dproj, d_ws, d_bs, d_gv = _mixer_a_bwd(s["proj"], lay.uv, dya, spatial_w[i], jnp.swapaxes(spatial_w[i], 1, 2),
                                               s["b_col"], v_norm_g[i][None], dproj, "mixer_a_bwd")
        do, dproj, d_go = _gdn_post_bwd(s["o"], s["proj"], lay.z, dyb, o_norm_g[i][None], dproj, "gdn_post_bwd")
        dq, dk, dv, d_ar, d_br, d_al, d_dt, *scattered = _gdn_bwd(
            s["qkv_h"], s["araw"], s["braw"], a_log[i], dt_bias[i], s["states"], s["t_mats"], do, "gdn_bwd",
            scatter_side(beside_gdn, i + 1))
        accs = scattered_into(accs, beside_gdn, scattered)
        dacc, d_conv[i] = _conv_bwd_pre(s["proj"], dq, dk, dv, conv_full[i], "conv_bwd_pre")
        dproj = _conv_bwd_in(dacc, conv_full[i], dproj, "conv_bwd_in")
        dba_rows = jnp.pad(jnp.concatenate([d_br.reshape(HEADS, t), d_ar.reshape(HEADS, t)]),
                           ((0, LANES - 2 * HEADS), (0, 0)))
        dproj = _rows_into_cols(dproj, dba_rows, lay.ba, "dproj_ba")
        if sums is None:
            g_pad = _matmul(s["h"], dproj, "tn", "proj_dw")
            dh = _matmul(dproj, w_pads[i], "nt", "proj_dx")
        else:
            g_pad, *scattered = _matmul(s["h"], dproj, "tn", "proj_dw", side=scatter_side(beside_dw, i + 1))
            accs = scattered_into(accs, beside_dw, scattered)
            dh, *scattered = _matmul(dproj, w_pads[i], "nt", "proj_dx", side=scatter_side(beside_dx, i + 1))
            accs = scattered_into(accs, beside_dx, scattered)
        if i > 0:
            dx, dsh1, dsc1, dg1, dfo, dgt2_before = _norm_bwd(s["x_in"], dh, dx1, norm1_g[i][None], s["sc1"], "norm1_bwd",
                                                              gate=(saved[i - 1]["fo"], saved[i - 1]["gt2"]))
        else:
            dx, dsh1, dsc1, dg1 = _norm_bwd(s["x_in"], dh, dx1, norm1_g[i][None], s["sc1"], "norm1_bwd")
        dmod[i] = jnp.concatenate([dsh1, dsc1, dgt1, dsh2, dsc2, dgt2], axis=1)[0]
        small[i] = (dg1[0], d_ws, d_bs[:, :, 0], d_gv[0], d_al[:, 0, 0], d_dt[:, 0, 0], d_go[0], dg2[0])
        parts = [_blocks_from_cols(g_pad, in_pieces, w_in.shape[2], "w_in_blocks"), row_blocks(g_a), row_blocks(g_b),
                 row_blocks(g_o), row_blocks(g_fi), row_blocks(g_fo)]
    other = _run_side(_ReduceScatterD2dSide(parts), "rs_d2d_last")
    sums = [_pair_sum(p, o, core, "rs_pair_sum_%d" % k) for k, (p, o) in enumerate(zip(parts, other))]
    accs = _run_side(_ReduceScatterIciSide(sums, accs, 0), "rs_ici_last")

    rep_w = (ada_b, norm1_g, spatial_w, spatial_b, v_norm_g, a_log, dt_bias, o_norm_g, norm2_g)
    rep_m = (m_ada_b, m_norm1_g, m_spatial_w, m_spatial_b, m_v_norm_g, m_a_log, m_dt_bias, m_o_norm_g, m_norm2_g)
    rep_v = (v_ada_b, v_norm1_g, v_spatial_w, v_spatial_b, v_v_norm_g, v_a_log, v_dt_bias, v_o_norm_g, v_norm2_g)
    rep_parts[0], fin_parts = _run_side(_AllGatherSide([rep_pack(0), _pack([d_final_g[0]])]), "small_grads_last")
    dmod = jnp.stack(dmod)
    d_conv_blocks = jnp.stack(d_conv).reshape(nl, CONV_K, N_DEV, -1).transpose(2, 0, 1, 3).reshape(N_DEV, -1, LANES)
    dmod_blocks = dmod.reshape(nl, N_DEV, -1).transpose(1, 0, 2)
    conv_all, dmod_all = _a2a_direct([d_conv_blocks, dmod_blocks], "small_grads_scatter")
    by_layer = lambda arrs: jnp.stack([_pack([a[i] for a in arrs]) for i in range(nl)])
    rep_out = _sum_adam(jnp.stack(rep_parts), by_layer(rep_w), by_layer(rep_m), by_layer(rep_v), "adam_small")
    fin_out = _sum_adam(fin_parts[None], _pack([final_g])[None], _pack([m_final_g])[None], _pack([v_final_g])[None],
                        "adam_final_g")
    layer_like = [a[0] for a in rep_w]
    rep_out = [[jnp.stack(per_layer) for per_layer in zip(*[_unpack(o[i], layer_like) for i in range(nl)])]
               + _unpack(f[0], [final_g]) for o, f in zip(rep_out, fin_out)]
    conv_out = _sum_adam(conv_all[None], conv_w.reshape(1, -1, LANES), m_conv_w.reshape(1, -1, LANES),
                         v_conv_w.reshape(1, -1, LANES), "adam_conv")
    conv_out = [o.reshape(conv_w.shape) for o in conv_out]
    ada_out = _ada_bwd(c_all[:, :, None], dmod_all.transpose(1, 0, 2), ada_w, m_ada_w, v_ada_w, "ada_bwd_adam")
    big_m = (m_w_in, m_w_branch_a, m_w_branch_b, m_w_out, m_w_ffn_in, m_w_ffn_out)
    big_v = (v_w_in, v_w_branch_a, v_w_branch_b, v_w_out, v_w_ffn_in, v_w_ffn_out)
    turn = lambda k, a: jnp.swapaxes(a, 1, 2) if k == 4 else a
    big_out = [[turn(k, o) for o in _sum_adam(accs[k], turn(k, big[k]), turn(k, big_m[k]), turn(k, big_v[k]),
                                             "adam_big_%d" % k)] for k in range(6)]

    def ordered(kind):
        rep = rep_out[kind]
        return (ada_out[kind], rep[0], rep[1], big_out[0][kind], conv_out[kind], rep[2], rep[3], rep[4], rep[5],
                rep[6], rep[7], big_out[1][kind], big_out[2][kind], big_out[3][kind], rep[8], big_out[4][kind],
                big_out[5][kind], rep[9])

    return (loss, dx[None]) + ordered(0) + ordered(1) + ordered(2) + ordered(3)
```

```python
import functools

import jax
import jax.numpy as jnp
from jax import lax
from jax.experimental import pallas as pl
from jax.experimental.pallas import tpu as pltpu

F32 = jnp.float32
BF16 = jnp.bfloat16
MXU_DTYPE = BF16
WIRE_DTYPE = BF16
ACT_DTYPE = BF16
EPS = 1e-6
LANES = 128
SUBLANES = 8
GDN_CHUNK = 128
A_CHUNK = 128
GROUPS = 8
HEADS = 8
HEAD_DIM = 128
CONV_K = 4
N_DEV = 8
VMEM_LIMIT = 48 * 1024 * 1024
MESH = pl.DeviceIdType.MESH

ADAM_LR = 0.001
ADAM_B1 = 0.9
ADAM_B2 = 0.999
ADAM_EPS = 1e-08
ADAM_WD = 0.01
ADAM_STEP = 10

_NN = (((1,), (0,)), ((), ()))
_NT = (((1,), (1,)), ((), ()))
_TN = (((0,), (0,)), ((), ()))


def _mm(a, b, dims=_NN):
    return lax.dot_general(a.astype(MXU_DTYPE), b.astype(MXU_DTYPE), dims, preferred_element_type=F32)


def _mm_hi(a, b):
    return lax.dot_general(a, b, _NN, precision=lax.Precision.HIGHEST, preferred_element_type=F32)


def _tile(n, cands):
    for c in cands:
        if n % c == 0:
            return c
    return n


def _params(sem=None):
    return pltpu.CompilerParams(dimension_semantics=sem, vmem_limit_bytes=VMEM_LIMIT)


def _sigmoid(x):
    return 1.0 / (1.0 + jnp.exp(-x))


def _silu(x):
    return x * _sigmoid(x)


_GELU_C = 0.7978845608028654
_GELU_A = 0.044715


def _gelu(x):
    return 0.5 * x * (1.0 + jnp.tanh(_GELU_C * (x + _GELU_A * x * x * x)))


def _gelu_and_slope(x):
    t = jnp.tanh(_GELU_C * (x + _GELU_A * x * x * x))
    return 0.5 * x * (1.0 + t), 0.5 * (1.0 + t) + 0.5 * x * (1.0 - t * t) * _GELU_C * (1.0 + 3.0 * _GELU_A * x * x)


def _softplus(x):
    return jnp.maximum(x, 0.0) + jnp.log(1.0 + jnp.exp(-jnp.abs(x)))


_MM_TILES = (1024, 1408, 1664, 512, 256, 128)


class _NoSide:
    operands, out_shape, scratch, aliases, n_in, n_out = [], [], [], {}, 0, 0


def _side_hooks(side, refs, n_main_in, n_main_out, n_main_scratch, grid):
    a = n_main_in + side.n_in
    b = a + n_main_out + side.n_out
    ins, outs, sems = refs[n_main_in:a], refs[a + n_main_out:b], refs[b + n_main_scratch:]
    main = refs[:n_main_in] + refs[a:a + n_main_out] + refs[b:b + n_main_scratch]
    ids = [pl.program_id(k) for k in range(len(grid))]

    def start():
        if side.n_in:
            pl.when(functools.reduce(jnp.logical_and, [i == 0 for i in ids]))(lambda: side.start(ins, outs, sems))

    def finish():
        if side.n_in:
            last = functools.reduce(jnp.logical_and, [i == g - 1 for i, g in zip(ids, grid)])
            pl.when(last)(lambda: side.finish(ins, outs, sems))

    return main, start, finish


def _carrier_call(body, name, grid, in_specs, out_specs, out_shape, scratch, side, args):
    aliases = {len(in_specs) + k: len(out_specs) + v for k, v in side.aliases.items()}
    return pl.pallas_call(
        body, name=name, grid=grid, in_specs=list(in_specs) + [_ANY] * side.n_in,
        out_specs=list(out_specs) + [_ANY] * side.n_out, out_shape=list(out_shape) + list(side.out_shape),
        scratch_shapes=list(scratch) + list(side.scratch), input_output_aliases=aliases,
        compiler_params=_params(("arbitrary",) * len(grid)))(*args, *side.operands)


_MM_VMEM_BUDGET = 44 * 1024 * 1024


def _matmul_tiles(mode, m, n, k, out_bytes):
    tk = _tile(k, _MM_TILES)
    tm = _tile(m, _MM_TILES)
    in_bytes = jnp.dtype(MXU_DTYPE).itemsize
    for tn in _MM_TILES:
        if n % tn:
            continue
        need = 2 * in_bytes * (tm * tk + tk * tn) + tm * tn * (2 * out_bytes + (4 if k > tk else 0))
        if need <= _MM_VMEM_BUDGET:
            return tm, tn, tk
    return tm, _tile(n, (LANES,)), tk


def _matmul(a, b, mode, name, out_dtype=F32, side=_NoSide):
    planes, plane_cols = (a.shape[0], a.shape[2]) if a.ndim == 3 else (1, a.shape[1])
    a_shape = (a.shape[-2], planes * plane_cols)
    if mode == "nn":
        (m, k), n = a_shape, b.shape[1]
    elif mode == "nt":
        (m, k), n = a_shape, b.shape[0]
    else:
        (k, m), n = a_shape, b.shape[1]
    tm, tn, tk = _matmul_tiles(mode, m, n, k, jnp.dtype(out_dtype).itemsize)
    nk = k // tk
    grid = (m // tm, n // tn, nk)
    dims = {"nn": _NN, "nt": _NT, "tn": _TN}[mode]

    def body(*refs):
        (a_ref, b_ref, o_ref, acc_ref), side_start, side_finish = _side_hooks(side, refs, 2, 1, 1, grid)
        kk = pl.program_id(2)
        side_start()
        if nk == 1:
            o_ref[...] = _mm(a_ref[...], b_ref[...], dims).astype(o_ref.dtype)
        else:
            @pl.when(kk == 0)
            def _():
                acc_ref[...] = _mm(a_ref[...], b_ref[...], dims)

            @pl.when(jnp.logical_and(kk > 0, kk < nk - 1))
            def _():
                acc_ref[...] += _mm(a_ref[...], b_ref[...], dims)

            @pl.when(kk == nk - 1)
            def _():
                o_ref[...] = (acc_ref[...] + _mm(a_ref[...], b_ref[...], dims)).astype(o_ref.dtype)

        side_finish()

    if a.ndim == 3:
        per = plane_cols // (tm if mode == "tn" else tk)
        a_spec = (pl.BlockSpec((None, tk, tm), lambda i, j, l: (i // per, l, i % per)) if mode == "tn"
                  else pl.BlockSpec((None, tm, tk), lambda i, j, l: (l // per, i, l % per)))
    else:
        a_spec = (pl.BlockSpec((tk, tm), lambda i, j, l: (l, i)) if mode == "tn"
                  else pl.BlockSpec((tm, tk), lambda i, j, l: (i, l)))
    b_spec = (pl.BlockSpec((tn, tk), lambda i, j, l: (j, l)) if mode == "nt"
              else pl.BlockSpec((tk, tn), lambda i, j, l: (l, j)))
    o_spec = pl.BlockSpec((tm, tn), lambda i, j, l: (i, j))
    out = _carrier_call(body, name, grid, [a_spec, b_spec], [o_spec], [jax.ShapeDtypeStruct((m, n), out_dtype)],
                        [pltpu.VMEM((tm, tn) if nk > 1 else (SUBLANES, LANES), F32)], side, (a, b))
    return out if side.n_in else out[0]


_ROW_TILES = (512, 256, 128)


def _resid_norm(x, delta, gt, g, sc, sh, name):
    t, d = x.shape
    tt = _tile(t, _ROW_TILES)
    has = delta is not None

    def body(*refs):
        if has:
            x_ref, d_ref, gt_ref, g_ref, sc_ref, sh_ref, xo_ref, h_ref = refs
            xv = x_ref[...] + gt_ref[...] * d_ref[...]
            xo_ref[...] = xv
        else:
            x_ref, g_ref, sc_ref, sh_ref, h_ref = refs
            xv = x_ref[...]
        r = lax.rsqrt(jnp.mean(xv * xv, axis=-1, keepdims=True) + EPS)
        y = xv * r * g_ref[...]
        h_ref[...] = (y * (1.0 + sc_ref[...]) + sh_ref[...]).astype(h_ref.dtype)

    row = pl.BlockSpec((tt, d), lambda i: (i, 0))
    vec = pl.BlockSpec((1, d), lambda i: (0, 0))
    if has:
        return pl.pallas_call(
            body, name=name, grid=(t // tt,), in_specs=[row, row, vec, vec, vec, vec], out_specs=[row, row],
            out_shape=[jax.ShapeDtypeStruct((t, d), F32), jax.ShapeDtypeStruct((t, d), MXU_DTYPE)],
            compiler_params=_params(("parallel",)))(x, delta, gt, g, sc, sh)
    h = pl.pallas_call(
        body, name=name + "_first", grid=(t // tt,), in_specs=[row, vec, vec, vec], out_specs=row,
        out_shape=jax.ShapeDtypeStruct((t, d), MXU_DTYPE), compiler_params=_params(("parallel",)))(x, g, sc, sh)
    return x, h


def _final_loss(x, delta, gt, g, target, name):
    t, d = x.shape
    tt = _tile(t, _ROW_TILES)

    def body(x_ref, d_ref, gt_ref, g_ref, tg_ref, dx_ref, dg_ref, loss_ref):
        @pl.when(pl.program_id(0) == 0)
        def _():
            dg_ref[...] = jnp.zeros_like(dg_ref)
            loss_ref[...] = jnp.zeros_like(loss_ref)

        xv = x_ref[...] + gt_ref[...] * d_ref[...]
        r = lax.rsqrt(jnp.mean(xv * xv, axis=-1, keepdims=True) + EPS)
        xh = xv * r
        diff = xh * g_ref[...] - tg_ref[...]
        loss_ref[...] += jnp.sum(diff * diff) * (0.5 / d)
        dy = diff * (1.0 / d)
        dg_ref[...] += jnp.sum(dy * xh, axis=0, keepdims=True)
        dxh = dy * g_ref[...]
        dx_ref[...] = r * (dxh - xh * jnp.mean(dxh * xh, axis=-1, keepdims=True))

    row = pl.BlockSpec((tt, d), lambda i: (i, 0))
    vec = pl.BlockSpec((1, d), lambda i: (0, 0))
    tile = pl.BlockSpec((SUBLANES, LANES), lambda i: (0, 0))
    return pl.pallas_call(
        body, name=name, grid=(t // tt,), in_specs=[row, row, vec, vec, row], out_specs=[row, vec, tile],
        out_shape=[jax.ShapeDtypeStruct((t, d), F32), jax.ShapeDtypeStruct((1, d), F32),
                   jax.ShapeDtypeStruct((SUBLANES, LANES), F32)],
        compiler_params=_params(("arbitrary",)))(x, delta, gt, g, target)


def _norm_bwd(x, dh, dres, g, sc, name, gate=None):
    t, d = x.shape
    tt = _tile(t, _ROW_TILES)
    gated = gate is not None

    def body(*refs):
        x_ref, dh_ref, dr_ref, g_ref, sc_ref = refs[:5]
        dx_ref, dsh_ref, dsc_ref, dg_ref = refs[5 + 2 * gated:9 + 2 * gated]

        @pl.when(pl.program_id(0) == 0)
        def _():
            for acc_ref in refs[6 + 2 * gated:9 + 2 * gated] + refs[10 + 2 * gated:]:
                acc_ref[...] = jnp.zeros_like(acc_ref)

        xv, dh = x_ref[...], dh_ref[...]
        r = lax.rsqrt(jnp.mean(xv * xv, axis=-1, keepdims=True) + EPS)
        xh = xv * r
        gv, sc1 = g_ref[...], 1.0 + sc_ref[...]
        dsh_ref[...] += jnp.sum(dh, axis=0, keepdims=True)
        dsc_ref[...] += jnp.sum(dh * xh, axis=0, keepdims=True) * gv
        dg_ref[...] += jnp.sum(dh * xh, axis=0, keepdims=True) * sc1
        dxh = dh * (gv * sc1)
        dx = dr_ref[...] + r * (dxh - xh * jnp.mean(dxh * xh, axis=-1, keepdims=True))
        dx_ref[...] = dx
        if gated:
            br_ref, gt_ref, db_ref, dgt_ref = refs[5], refs[6], refs[11], refs[12]
            db_ref[...] = (dx * gt_ref[...]).astype(db_ref.dtype)
            dgt_ref[...] += jnp.sum(dx * br_ref[...], axis=0, keepdims=True)

    row = pl.BlockSpec((tt, d), lambda i: (i, 0))
    vec = pl.BlockSpec((1, d), lambda i: (0, 0))
    vshape = jax.ShapeDtypeStruct((1, d), F32)
    in_specs, out_specs = [row, row, row, vec, vec], [row, vec, vec, vec]
    out_shape = [jax.ShapeDtypeStruct((t, d), F32), vshape, vshape, vshape]
    if gated:
        in_specs, out_specs = in_specs + [row, vec], out_specs + [row, vec]
        out_shape = out_shape + [jax.ShapeDtypeStruct((t, d), MXU_DTYPE), vshape]
    return pl.pallas_call(
        body, name=name + ("_gate" if gated else ""), grid=(t // tt,), in_specs=in_specs, out_specs=out_specs,
        out_shape=out_shape, compiler_params=_params(("arbitrary",)))(x, dh, dres, g, sc, *(gate or ()))


def _gate_bwd(dxo, branch, gt, name):
    t, d = dxo.shape
    tt = _tile(t, _ROW_TILES)

    def body(dx_ref, br_ref, gt_ref, db_ref, dgt_ref):
        @pl.when(pl.program_id(0) == 0)
        def _():
            dgt_ref[...] = jnp.zeros_like(dgt_ref)

        dx = dx_ref[...]
        db_ref[...] = (dx * gt_ref[...]).astype(db_ref.dtype)
        dgt_ref[...] += jnp.sum(dx * br_ref[...], axis=0, keepdims=True)

    row = pl.BlockSpec((tt, d), lambda i: (i, 0))
    vec = pl.BlockSpec((1, d), lambda i: (0, 0))
    return pl.pallas_call(
        body, name=name, grid=(t // tt,), in_specs=[row, row, vec], out_specs=[row, vec],
        out_shape=[jax.ShapeDtypeStruct((t, d), MXU_DTYPE), jax.ShapeDtypeStruct((1, d), F32)],
        compiler_params=_params(("arbitrary",)))(dxo, branch, gt)


def _ffn_in_swiglu(h, w, name):
    t, k = h.shape
    f = w.shape[1] // 2
    tm, tn = _tile(t, _MM_TILES), _tile(f, _MM_TILES)
    nj = f // tn

    def body(h_ref, wg_ref, wu_ref, gu_ref, a_ref):
        hv = h_ref[...]
        gu_ref[0] = _mm(hv, wg_ref[...]).astype(gu_ref.dtype)
        gu_ref[1] = _mm(hv, wu_ref[...]).astype(gu_ref.dtype)
        a_ref[...] = (_silu(gu_ref[0].astype(F32)) * gu_ref[1].astype(F32)).astype(a_ref.dtype)

    return pl.pallas_call(
        body, name=name, grid=(t // tm, nj),
        in_specs=[pl.BlockSpec((tm, k), lambda i, j: (i, 0)), pl.BlockSpec((k, tn), lambda i, j: (0, j)),
                  pl.BlockSpec((k, tn), lambda i, j: (0, j + nj))],
        out_specs=[pl.BlockSpec((2, tm, tn), lambda i, j: (0, i, j)), pl.BlockSpec((tm, tn), lambda i, j: (i, j))],
        out_shape=[jax.ShapeDtypeStruct((2, t, f), ACT_DTYPE), jax.ShapeDtypeStruct((t, f), MXU_DTYPE)],
        compiler_params=_params(("parallel", "parallel")))(h, w, w)


def _ffn_out_dx_swiglu(dfo, w, gu, name):
    t, k = dfo.shape
    f = w.shape[0]
    tm, tn = _tile(t, _ROW_TILES), _tile(f, _MM_TILES)

    def body(d_ref, w_ref, gu_ref, o_ref):
        da = _mm(d_ref[...], w_ref[...], _NT)
        gate = gu_ref[0].astype(F32)
        sg = _sigmoid(gate)
        o_ref[0] = (da * gu_ref[1].astype(F32) * (sg * (1.0 + gate * (1.0 - sg)))).astype(o_ref.dtype)
        o_ref[1] = (da * (gate * sg)).astype(o_ref.dtype)

    planes = pl.BlockSpec((2, tm, tn), lambda i, j: (0, i, j))
    return pl.pallas_call(
        body, name=name, grid=(t // tm, f // tn),
        in_specs=[pl.BlockSpec((tm, k), lambda i, j: (i, 0)), pl.BlockSpec((tn, k), lambda i, j: (j, 0)), planes],
        out_specs=planes, out_shape=jax.ShapeDtypeStruct((2, t, f), MXU_DTYPE),
        compiler_params=_params(("parallel", "parallel")))(dfo, w, gu)


class _ProjLayout:
    def __init__(self, d):
        wc = 3 * HEADS * HEAD_DIM
        self.d, self.wc = d, wc
        self.qkv, self.z, self.uv, self.gates, self.ba = 0, wc, wc + d, wc + 3 * d, wc + 5 * d
        self.width = self.ba + LANES
        assert self.z % d == 0 and self.uv % (2 * d) == 0 and self.gates % (2 * d) == 0 and self.ba % LANES == 0

    def pieces(self, shard):
        d, wc, out, lo = self.d, self.wc, [], 0
        for length, dst in ((2 * d, self.uv), (wc, self.qkv), (d, self.z), (2 * HEADS, self.ba), (2 * d, self.gates)):
            pos = lo
            while pos < lo + length:
                j = pos // shard
                n = min(lo + length, (j + 1) * shard) - pos
                out.append((j, pos - j * shard, n, dst + pos - lo))
                pos += n
            lo += length
        return out


def _branches_merge(ya, w_a, yb, w_b, proj, gcol, name):
    t, k = ya.shape
    d = w_a.shape[1]
    tm, tn = _tile(t, _ROW_TILES), _tile(d, _MM_TILES)
    nj = d // tn

    def body(ya_ref, wa_ref, yb_ref, wb_ref, ga_ref, gb_ref, pa_ref, pb_ref, m_ref):
        pa, pb = _mm(ya_ref[...], wa_ref[...]), _mm(yb_ref[...], wb_ref[...])
        pa_ref[...], pb_ref[...] = pa, pb
        sa, sb = _sigmoid(ga_ref[...].astype(F32)), _sigmoid(gb_ref[...].astype(F32))
        m_ref[...] = (sa * pa + sb * pb).astype(m_ref.dtype)

    left = pl.BlockSpec((tm, k), lambda i, j: (i, 0))
    right = pl.BlockSpec((k, tn), lambda i, j: (0, j))
    gate = lambda first: pl.BlockSpec((tm, tn), lambda i, j: (i, first // tn + j))
    out = pl.BlockSpec((tm, tn), lambda i, j: (i, j))
    return pl.pallas_call(
        body, name=name, grid=(t // tm, nj), in_specs=[left, right, left, right, gate(gcol), gate(gcol + d)],
        out_specs=[out, out, out],
        out_shape=[jax.ShapeDtypeStruct((t, d), F32), jax.ShapeDtypeStruct((t, d), F32),
                   jax.ShapeDtypeStruct((t, d), MXU_DTYPE)],
        compiler_params=_params(("parallel", "parallel")))(ya, w_a, yb, w_b, proj, proj)


def _merge_bwd(dm, pa, pb, proj, gcol, dproj, name):
    t, d = pa.shape
    tt = _tile(t, _ROW_TILES)

    def body(dm_ref, pa_ref, pb_ref, ga_ref, gb_ref, _, dpa_ref, dpb_ref, dg_ref):
        dm = dm_ref[...]
        sa, sb = _sigmoid(ga_ref[...].astype(F32)), _sigmoid(gb_ref[...].astype(F32))
        dpa_ref[...] = (dm * sa).astype(dpa_ref.dtype)
        dpb_ref[...] = (dm * sb).astype(dpb_ref.dtype)
        dg_ref[:, :d] = (dm * pa_ref[...] * sa * (1.0 - sa)).astype(dg_ref.dtype)
        dg_ref[:, d:] = (dm * pb_ref[...] * sb * (1.0 - sb)).astype(dg_ref.dtype)

    row = pl.BlockSpec((tt, d), lambda i: (i, 0))
    gate = lambda k: pl.BlockSpec((tt, d), lambda i: (i, gcol // d + k))
    wide = pl.BlockSpec((tt, 2 * d), lambda i: (i, gcol // (2 * d)))
    return pl.pallas_call(
        body, name=name, grid=(t // tt,), in_specs=[row, row, row, gate(0), gate(1), _ANY], out_specs=[row, row, wide],
        out_shape=[jax.ShapeDtypeStruct((t, d), MXU_DTYPE), jax.ShapeDtypeStruct((t, d), MXU_DTYPE),
                   jax.ShapeDtypeStruct(dproj.shape, dproj.dtype)],
        input_output_aliases={5: 2}, compiler_params=_params(("parallel",)))(dm, pa, pb, proj, proj, dproj)


def _tri_masks(n):
    ri = lax.broadcasted_iota(jnp.int32, (n, n), 0)
    ci = lax.broadcasted_iota(jnp.int32, (n, n), 1)
    return ri >= ci, ri > ci, ri == ci


def _mixer_a_fwd(proj, ucol, w_s, b_col, g_v, name):
    t, w = proj.shape[0], g_v.shape[1]
    c = A_CHUNK

    def body(u_ref, v_ref, w_ref, b_ref, gv_ref, y_ref):
        tril, _, _ = _tri_masks(c)
        ug, vg = _gelu(u_ref[...].astype(F32)), _gelu(v_ref[...].astype(F32))
        for g in range(GROUPS):
            sl = slice(g * c, (g + 1) * c)
            vt = vg[:, sl]
            r = lax.rsqrt(jnp.mean(vt * vt, axis=-1, keepdims=True) + EPS)
            vn = vt * r * gv_ref[:, sl]
            s = _mm(jnp.where(tril, w_ref[g], 0.0), vn) + b_ref[g]
            y_ref[:, sl] = (ug[:, sl] * s).astype(y_ref.dtype)

    return pl.pallas_call(
        body, name=name, grid=(t // c,),
        in_specs=[pl.BlockSpec((c, w), lambda i: (i, ucol // w)), pl.BlockSpec((c, w), lambda i: (i, ucol // w + 1)),
                  pl.BlockSpec((GROUPS, c, c), lambda i: (0, 0, 0)), pl.BlockSpec((GROUPS, c, 1), lambda i: (0, 0, 0)),
                  pl.BlockSpec((1, w), lambda i: (0, 0))],
        out_specs=pl.BlockSpec((c, w), lambda i: (i, 0)), out_shape=jax.ShapeDtypeStruct((t, w), MXU_DTYPE),
        compiler_params=_params(("parallel",)))(proj, proj, w_s, b_col, g_v)


def _mixer_a_bwd(proj, ucol, dy, w_s, w_st, b_col, g_v, dproj, name):
    t, w = proj.shape[0], g_v.shape[1]
    w2 = 2 * w
    c = A_CHUNK

    def body(u_ref, v_ref, dy_ref, w_ref, wt_ref, b_ref, gv_ref, _, duv_ref, dw_ref, db_ref, dgv_ref):
        @pl.when(pl.program_id(0) == 0)
        def _():
            dw_ref[...] = jnp.zeros_like(dw_ref)
            db_ref[...] = jnp.zeros_like(db_ref)
            dgv_ref[...] = jnp.zeros_like(dgv_ref)

        tril, _, _ = _tri_masks(c)
        triu = lax.broadcasted_iota(jnp.int32, (c, c), 0) <= lax.broadcasted_iota(jnp.int32, (c, c), 1)
        (ug, dug), (vg, dvg) = _gelu_and_slope(u_ref[...].astype(F32)), _gelu_and_slope(v_ref[...].astype(F32))
        for g in range(GROUPS):
            sl = slice(g * c, (g + 1) * c)
            vt = vg[:, sl]
            r = lax.rsqrt(jnp.mean(vt * vt, axis=-1, keepdims=True) + EPS)
            vh = vt * r
            gv = gv_ref[:, sl]
            vn = vh * gv
            s = _mm(jnp.where(tril, w_ref[g], 0.0), vn) + b_ref[g]
            dy = dy_ref[:, sl]
            ds = dy * ug[:, sl]
            dw_ref[g] += jnp.where(tril, _mm(ds, vn, _NT), 0.0)
            db_ref[g] += jnp.sum(ds, axis=1, keepdims=True)
            dvn = _mm(jnp.where(triu, wt_ref[g], 0.0), ds)
            dgv_ref[:, sl] += jnp.sum(dvn * vh, axis=0, keepdims=True)
            dvh = dvn * gv
            dvt = r * (dvh - vh * jnp.mean(dvh * vh, axis=-1, keepdims=True))
            duv_ref[:, sl] = (dy * s * dug[:, sl]).astype(duv_ref.dtype)
            duv_ref[:, w + g * c:w + (g + 1) * c] = (dvt * dvg[:, sl]).astype(duv_ref.dtype)

    full3 = lambda shape: pl.BlockSpec(shape, lambda i: (0, 0, 0))
    return pl.pallas_call(
        body, name=name, grid=(t // c,),
        in_specs=[pl.BlockSpec((c, w), lambda i: (i, ucol // w)), pl.BlockSpec((c, w), lambda i: (i, ucol // w + 1)),
                  pl.BlockSpec((c, w), lambda i: (i, 0)), full3((GROUPS, c, c)), full3((GROUPS, c, c)),
                  full3((GROUPS, c, 1)), pl.BlockSpec((1, w), lambda i: (0, 0)), _ANY],
        out_specs=[pl.BlockSpec((c, w2), lambda i: (i, ucol // w2)), full3((GROUPS, c, c)), full3((GROUPS, c, 1)),
                   pl.BlockSpec((1, w), lambda i: (0, 0))],
        out_shape=[jax.ShapeDtypeStruct(dproj.shape, dproj.dtype), jax.ShapeDtypeStruct((GROUPS, c, c), F32),
                   jax.ShapeDtypeStruct((GROUPS, c, 1), F32), jax.ShapeDtypeStruct((1, w), F32)],
        input_output_aliases={7: 0},
        compiler_params=_params(("arbitrary",)))(proj, proj, dy, w_s, w_st, b_col, g_v, dproj)


_Q_SCALE = HEAD_DIM ** -0.5


CONV_HALO = 16


def _conv_taps(x_ref, p_ref, w_ref):
    prev = jnp.where(pl.program_id(0) > 0, p_ref[...].astype(F32), 0.0)
    ext = jnp.concatenate([prev, x_ref[...].astype(F32)], axis=0)
    shifted = [ext[CONV_HALO:]] + [pltpu.roll(ext, s, 0)[CONV_HALO:] for s in range(1, CONV_K)]
    acc = shifted[0] * w_ref[pl.ds(CONV_K - 1, 1), :]
    for s in range(1, CONV_K):
        acc = acc + shifted[s] * w_ref[pl.ds(CONV_K - 1 - s, 1), :]
    return acc, shifted


def _conv_fwd(qkv, w, name):
    t, cw = qkv.shape[0], w.shape[1]
    tt = _tile(t, (256, 128))
    hb = tt // CONV_HALO

    def body(x_ref, p_ref, w_ref, o_ref):
        acc, _ = _conv_taps(x_ref, p_ref, w_ref)
        y = _silu(acc)
        for which in range(3):
            for h in range(HEADS):
                lo = (which * HEADS + h) * HEAD_DIM
                seg = y[:, lo:lo + HEAD_DIM]
                if which < 2:
                    seg = seg * lax.rsqrt(jnp.sum(seg * seg, axis=-1, keepdims=True) + EPS)
                if which == 0:
                    seg = seg * _Q_SCALE
                o_ref[which, h] = seg

    return pl.pallas_call(
        body, name=name, grid=(t // tt,),
        in_specs=[pl.BlockSpec((tt, cw), lambda i: (i, 0)),
                  pl.BlockSpec((CONV_HALO, cw), lambda i: (jnp.maximum(i * hb - 1, 0), 0)),
                  pl.BlockSpec((CONV_K, cw), lambda i: (0, 0))],
        out_specs=pl.BlockSpec((3, HEADS, tt, HEAD_DIM), lambda i: (0, 0, i, 0)),
        out_shape=jax.ShapeDtypeStruct((3, HEADS, t, HEAD_DIM), F32),
        compiler_params=_params(("parallel",)))(qkv, qkv, w)


def _conv_bwd_pre(qkv, dq, dk, dv, w, name):
    t, cw = qkv.shape[0], w.shape[1]
    tt = _tile(t, (256, 128))
    hb = tt // CONV_HALO

    def body(x_ref, p_ref, dq_ref, dk_ref, dv_ref, w_ref, da_ref, dw_ref):
        @pl.when(pl.program_id(0) == 0)
        def _():
            dw_ref[...] = jnp.zeros_like(dw_ref)

        acc, shifted = _conv_taps(x_ref, p_ref, w_ref)
        sg = _sigmoid(acc)
        y = acc * sg
        dsilu = sg * (1.0 + acc * (1.0 - sg))
        d_refs = (dq_ref, dk_ref, dv_ref)
        for which in range(3):
            for h in range(HEADS):
                lo = (which * HEADS + h) * HEAD_DIM
                sl = slice(lo, lo + HEAD_DIM)
                dn = d_refs[which][h]
                if which < 2:
                    seg = y[:, sl]
                    rho = lax.rsqrt(jnp.sum(seg * seg, axis=-1, keepdims=True) + EPS)
                    nrm = seg * rho
                    if which == 0:
                        dn = dn * _Q_SCALE
                    dn = rho * (dn - nrm * jnp.sum(dn * nrm, axis=-1, keepdims=True))
                dacc = dn * dsilu[:, sl]
                da_ref[:, sl] = dacc
                for s in range(CONV_K):
                    dw_ref[pl.ds(CONV_K - 1 - s, 1), sl] += jnp.sum(dacc * shifted[s][:, sl], axis=0, keepdims=True)

    head = pl.BlockSpec((HEADS, tt, HEAD_DIM), lambda i: (0, i, 0))
    return pl.pallas_call(
        body, name=name, grid=(t // tt,),
        in_specs=[pl.BlockSpec((tt, cw), lambda i: (i, 0)),
                  pl.BlockSpec((CONV_HALO, cw), lambda i: (jnp.maximum(i * hb - 1, 0), 0)),
                  head, head, head, pl.BlockSpec((CONV_K, cw), lambda i: (0, 0))],
        out_specs=[pl.BlockSpec((tt, cw), lambda i: (i, 0)), pl.BlockSpec((CONV_K, cw), lambda i: (0, 0))],
        out_shape=[jax.ShapeDtypeStruct((t, cw), F32), jax.ShapeDtypeStruct((CONV_K, cw), F32)],
        compiler_params=_params(("arbitrary",)))(qkv, qkv, dq, dk, dv, w)


def _conv_bwd_in(dacc, w, dproj, name):
    t, cw = dacc.shape
    tt = _tile(t, (256, 128))
    hb = tt // SUBLANES
    nt = t // tt
    rows = tt + SUBLANES

    def body(d_ref, n_ref, w_ref, _, o_ref):
        cur = d_ref[...]
        nxt = jnp.where(pl.program_id(0) < nt - 1, n_ref[...], 0.0)
        ext = jnp.concatenate([cur, nxt], axis=0)
        acc = cur * w_ref[pl.ds(CONV_K - 1, 1), :]
        for s in range(1, CONV_K):
            acc = acc + pltpu.roll(ext, rows - s, 0)[:tt] * w_ref[pl.ds(CONV_K - 1 - s, 1), :]
        o_ref[...] = acc.astype(o_ref.dtype)

    return pl.pallas_call(
        body, name=name, grid=(nt,),
        in_specs=[pl.BlockSpec((tt, cw), lambda i: (i, 0)),
                  pl.BlockSpec((SUBLANES, cw), lambda i: (jnp.minimum((i + 1) * hb, t // SUBLANES - 1), 0)),
                  pl.BlockSpec((CONV_K, cw), lambda i: (0, 0)), _ANY],
        out_specs=pl.BlockSpec((tt, cw), lambda i: (i, 0)), out_shape=jax.ShapeDtypeStruct(dproj.shape, dproj.dtype),
        input_output_aliases={3: 0}, compiler_params=_params(("parallel",)))(dacc, dacc, w, dproj)


_INV_BASE_SHIFT = 3


def _inv_unit_lower(a, eye):
    c = GDN_CHUNK
    ri = lax.broadcasted_iota(jnp.int32, (c, c), 0)
    ci = lax.broadcasted_iota(jnp.int32, (c, c), 1)
    same = lambda sh: (ri >> sh) == (ci >> sh)
    x = jnp.where(same(_INV_BASE_SHIFT), -a, 0.0)
    p = jnp.where(eye, 1.0, 0.0) + x
    xs = _split(x)
    x2 = _mm3(xs, xs)
    x2s, ps = _split(x2), _split(p)
    r = _mm3(x2s, tuple(jnp.concatenate([u, v], axis=-1) for u, v in zip(x2s, ps)))
    x4, p = r[..., :c], p + r[..., c:]
    p = p + _mm3(_split(x4), _split(p))
    for sh in range(_INV_BASE_SHIFT, c.bit_length() - 1):
        off = jnp.where(same(sh + 1) & jnp.logical_not(same(sh)), a, 0.0)
        ps = _split(p)
        p = p - _mm3(ps, _split(_mm3(_split(off), ps)))
    return p


def _split(a):
    hi = a.astype(BF16)
    return hi, (a - hi.astype(F32)).astype(BF16)


def _dot_heads(u, v, dims):
    if u.ndim == 3:
        return jnp.stack([_dot_heads(u[j], v[j], dims) for j in range(u.shape[0])])
    return lax.dot_general(u, v, dims, preferred_element_type=F32)


def _mm3(a, b):
    return _dot_heads(a[0], b[0], _NN) + (_dot_heads(a[0], b[1], _NN) + _dot_heads(a[1], b[0], _NN))


def _hmm(a, b, dims=_NN):
    return _dot_heads(a.astype(MXU_DTYPE), b.astype(MXU_DTYPE), dims)


def _rowsum(x):
    return jnp.sum(x, axis=-1, keepdims=True)


def _colsum(x):
    return jnp.sum(x, axis=-2, keepdims=True)


class _Pre:
    pass


def _gdn_pre(q, k, v, araw, braw, alog, dtb, t_mat=None):
    c = GDN_CHUNK
    p = _Pre()
    p.tril, p.strict, p.eye = _tri_masks(c)
    p.to_col = lambda row: _rowsum(jnp.where(p.eye, row, 0.0))
    p.to_row = lambda col: _colsum(jnp.where(p.eye, col, 0.0))
    p.a_neg = -jnp.exp(alog + jnp.zeros((1, c), F32))
    p.xg = araw + dtb
    p.g_row = p.a_neg * _softplus(p.xg)
    p.beta_row = _sigmoid(braw)
    p.beta = p.to_col(p.beta_row)
    gam = _rowsum(jnp.where(p.tril, p.g_row, 0.0))
    gam_last = _rowsum(p.g_row)
    p.dm = jnp.where(p.tril, jnp.exp(jnp.where(p.tril, gam - p.to_row(gam), 0.0)), 0.0)
    p.e, p.ek, p.el = jnp.exp(gam), jnp.exp(gam_last - gam), jnp.exp(gam_last)
    p.kb = k * p.beta
    p.kk = _hmm(p.kb, k, _NT)
    p.t = _inv_unit_lower(jnp.where(p.strict, p.kk * p.dm, 0.0), p.eye) if t_mat is None else t_mat
    p.vb, p.kbe = v * p.beta, p.kb * p.e
    uw = _hmm(p.t, jnp.concatenate([p.vb, p.kbe], axis=-1))
    p.u, p.w = uw[..., :v.shape[-1]], uw[..., v.shape[-1]:]
    p.qk0 = _hmm(q, k, _NT)
    p.qk = p.qk0 * p.dm
    p.qd, p.kd = q * p.e, k * p.ek
    return p


GDN_HEADS_PER_STEP = 8


def _head_scalars(ref, hb):
    h0 = pl.program_id(0) * hb
    return jnp.stack([jnp.full((1, 1), ref[h0 + j], F32) for j in range(hb)])


def _gdn_specs(n, reverse):
    c, dk, hb = GDN_CHUNK, HEAD_DIM, GDN_HEADS_PER_STEP
    ix = (lambda i: n - 1 - i) if reverse else (lambda i: i)
    smem = pl.BlockSpec(memory_space=pltpu.SMEM)
    qkv = [pl.BlockSpec((None, hb, c, dk), functools.partial(lambda w, h, i: (w, h, ix(i), 0), w)) for w in range(3)]
    row = pl.BlockSpec((hb, None, 1, c), lambda h, i: (h, ix(i), 0, 0))
    tok = pl.BlockSpec((hb, c, dk), lambda h, i: (h, ix(i), 0))
    state = pl.BlockSpec((hb, None, dk, dk), lambda h, i: (h, ix(i), 0, 0))
    return smem, qkv, row, tok, state


def _gdn_fwd(qkv_h, araw, braw, alog, dtb, name, side=_NoSide):
    _, hh, t, dk = qkv_h.shape
    n, hb = t // GDN_CHUNK, GDN_HEADS_PER_STEP
    smem, qkv, row, tok, state = _gdn_specs(n, False)
    grid = (hh // hb, n)

    def body(*refs):
        main, side_start, side_finish = _side_hooks(side, refs, 7, 3, 1, grid)
        alog_ref, dt_ref, q_ref, k_ref, v_ref, a_ref, b_ref, o_ref, so_ref, to_ref, s_ref = main
        side_start()

        @pl.when(pl.program_id(1) == 0)
        def _():
            s_ref[...] = jnp.zeros_like(s_ref)

        p = _gdn_pre(q_ref[...], k_ref[...], v_ref[...], a_ref[...], b_ref[...],
                     _head_scalars(alog_ref, hb), _head_scalars(dt_ref, hb))
        s = s_ref[...]
        vn = p.u - _hmm(p.w, s)
        o_ref[...] = _hmm(p.qd, s) + _hmm(p.qk, vn)
        so_ref[...] = s
        to_ref[...] = p.t
        s_ref[...] = s * p.el + _hmm(p.kd, vn, _TN)
        side_finish()

    mats = jax.ShapeDtypeStruct((hh, n, dk, dk), F32)
    return _carrier_call(
        body, name, grid, [smem, smem] + qkv + [row, row], [tok, state, state],
        [jax.ShapeDtypeStruct((hh, t, dk), F32), mats, mats],
        [pltpu.VMEM((hb, dk, dk), F32)], side, (alog, dtb, qkv_h, qkv_h, qkv_h, araw, braw))


def _gdn_bwd(qkv_h, araw, braw, alog, dtb, states, t_mats, do, name, side=_NoSide):
    _, hh, t, dk = qkv_h.shape
    c, hb = GDN_CHUNK, GDN_HEADS_PER_STEP
    n = t // c
    smem, qkv, row, tok, state = _gdn_specs(n, True)
    acc = pl.BlockSpec((hb, 1, LANES), lambda h, i: (h, 0, 0))
    grid = (hh // hb, n)

    def body(*refs):
        main, side_start, side_finish = _side_hooks(side, refs, 10, 7, 1, grid)
        (alog_ref, dt_ref, q_ref, k_ref, v_ref, a_ref, b_ref, s_ref, t_ref, do_ref,
         dq_ref, dk_ref, dv_ref, da_ref, db_ref, dal_ref, ddt_ref, ds_ref) = main
        side_start()

        @pl.when(pl.program_id(1) == 0)
        def _():
            ds_ref[...] = jnp.zeros_like(ds_ref)
            dal_ref[...] = jnp.zeros_like(dal_ref)
            ddt_ref[...] = jnp.zeros_like(ddt_ref)

        q, k, v = q_ref[...], k_ref[...], v_ref[...]
        p = _gdn_pre(q, k, v, a_ref[...], b_ref[...], _head_scalars(alog_ref, hb), _head_scalars(dt_ref, hb),
                     t_ref[...])
        s, do, dsp = s_ref[...], do_ref[...], ds_ref[...]
        vn = p.u - _hmm(p.w, s)
        dqd = _hmm(do, s, _NT)
        dqk = _hmm(do, vn, _NT)
        dvn = _hmm(p.qk, do, _TN) + _hmm(p.kd, dsp)
        dkd = _hmm(vn, dsp, _NT)
        d_el = _colsum(_rowsum(s * dsp))
        ds_ref[...] = dsp * p.el + _hmm(p.qd, do, _TN) - _hmm(p.w, dvn, _TN)
        dw = -_hmm(dvn, s, _NT)
        d_t = _hmm(dvn, p.vb, _NT) + _hmm(dw, p.kbe, _NT)
        dvb, dkbe = _hmm(p.t, dvn, _TN), _hmm(p.t, dw, _TN)
        d_a = jnp.where(p.strict, -_hmm(p.t, _hmm(d_t, p.t, _NT), _TN), 0.0)
        dkk = d_a * p.dm
        dqk0 = dqk * p.dm
        ddm = d_a * p.kk + dqk * p.qk0
        dkb = _hmm(dkk, k) + dkbe * p.e
        dq_ref[...] = _hmm(dqk0, k) + dqd * p.e
        dk_ref[...] = _hmm(dkk, p.kb, _TN) + _hmm(dqk0, q, _TN) + dkd * p.ek + dkb * p.beta
        dv_ref[...] = dvb * p.beta
        dbeta = _rowsum(dkb * k) + _rowsum(dvb * v)
        d_e = _rowsum(dqd * q) + _rowsum(dkbe * p.kb)
        d_ek = _rowsum(dkd * k)
        m = ddm * p.dm
        dgam = d_e * p.e - d_ek * p.ek + _rowsum(m) - p.to_col(_colsum(m))
        dgam_last = _colsum(d_ek * p.ek) + d_el * p.el
        dg_row = _colsum(jnp.where(p.tril, dgam, 0.0)) + dgam_last
        da_row = dg_row * p.a_neg * _sigmoid(p.xg)
        da_ref[...] = da_row
        db_ref[...] = p.to_row(dbeta) * p.beta_row * (1.0 - p.beta_row)
        dal_ref[...] += _rowsum(dg_row * p.g_row)
        ddt_ref[...] += _rowsum(da_row)
        side_finish()

    tok_shape = jax.ShapeDtypeStruct((hh, t, dk), F32)
    row_shape = jax.ShapeDtypeStruct((hh, n, 1, c), F32)
    acc_shape = jax.ShapeDtypeStruct((hh, 1, LANES), F32)
    return _carrier_call(
        body, name, grid, [smem, smem] + qkv + [row, row, state, state, tok], [tok, tok, tok, row, row, acc, acc],
        [tok_shape, tok_shape, tok_shape, row_shape, row_shape, acc_shape, acc_shape],
        [pltpu.VMEM((hb, dk, dk), F32)], side, (alog, dtb, qkv_h, qkv_h, qkv_h, araw, braw, states, t_mats, do))


def _gdn_post_fwd(o, proj, zcol, g_o, name):
    hh, t, dv = o.shape
    tt = _tile(t, _ROW_TILES)
    zblk = zcol // (hh * dv)

    def body(o_ref, z_ref, g_ref, y_ref):
        for h in range(hh):
            sl = slice(h * dv, (h + 1) * dv)
            ov = o_ref[h]
            r = lax.rsqrt(jnp.mean(ov * ov, axis=-1, keepdims=True) + EPS)
            y_ref[:, sl] = (ov * r * g_ref[...] * _silu(z_ref[:, sl].astype(F32))).astype(y_ref.dtype)

    return pl.pallas_call(
        body, name=name, grid=(t // tt,),
        in_specs=[pl.BlockSpec((hh, tt, dv), lambda i: (0, i, 0)), pl.BlockSpec((tt, hh * dv), lambda i: (i, zblk)),
                  pl.BlockSpec((1, dv), lambda i: (0, 0))],
        out_specs=pl.BlockSpec((tt, hh * dv), lambda i: (i, 0)),
        out_shape=jax.ShapeDtypeStruct((t, hh * dv), MXU_DTYPE), compiler_params=_params(("parallel",)))(o, proj, g_o)


def _gdn_post_bwd(o, proj, zcol, dy, g_o, dproj, name):
    hh, t, dv = o.shape
    tt = _tile(t, _ROW_TILES)
    zblk = zcol // (hh * dv)

    def body(o_ref, z_ref, dy_ref, g_ref, _, do_ref, dz_ref, dg_ref):
        @pl.when(pl.program_id(0) == 0)
        def _():
            dg_ref[...] = jnp.zeros_like(dg_ref)

        gv = g_ref[...]
        for h in range(hh):
            sl = slice(h * dv, (h + 1) * dv)
            ov, zz, dy = o_ref[h], z_ref[:, sl].astype(F32), dy_ref[:, sl]
            r = lax.rsqrt(jnp.mean(ov * ov, axis=-1, keepdims=True) + EPS)
            oh = ov * r
            sg = _sigmoid(zz)
            dz_ref[:, sl] = (dy * oh * gv * (sg * (1.0 + zz * (1.0 - sg)))).astype(dz_ref.dtype)
            don = dy * (zz * sg)
            dg_ref[...] += _colsum(don * oh)
            doh = don * gv
            do_ref[h] = r * (doh - oh * jnp.mean(doh * oh, axis=-1, keepdims=True))

    return pl.pallas_call(
        body, name=name, grid=(t // tt,),
        in_specs=[pl.BlockSpec((hh, tt, dv), lambda i: (0, i, 0)), pl.BlockSpec((tt, hh * dv), lambda i: (i, zblk)),
                  pl.BlockSpec((tt, hh * dv), lambda i: (i, 0)), pl.BlockSpec((1, dv), lambda i: (0, 0)), _ANY],
        out_specs=[pl.BlockSpec((hh, tt, dv), lambda i: (0, i, 0)), pl.BlockSpec((tt, hh * dv), lambda i: (i, zblk)),
                   pl.BlockSpec((1, dv), lambda i: (0, 0))],
        out_shape=[jax.ShapeDtypeStruct((hh, t, dv), F32), jax.ShapeDtypeStruct(dproj.shape, dproj.dtype),
                   jax.ShapeDtypeStruct((1, dv), F32)],
        input_output_aliases={4: 1}, compiler_params=_params(("arbitrary",)))(o, proj, dy, g_o, dproj)


def _cols_as_rows(x, col, name):
    t = x.shape[0]
    tt = _tile(t, _ROW_TILES)

    def body(x_ref, o_ref):
        o_ref[...] = x_ref[...].T

    return pl.pallas_call(
        body, name=name, grid=(t // tt,), in_specs=[pl.BlockSpec((tt, LANES), lambda i: (i, col // LANES))],
        out_specs=pl.BlockSpec((LANES, tt), lambda i: (0, i)), out_shape=jax.ShapeDtypeStruct((LANES, t), x.dtype),
        compiler_params=_params(("parallel",)))(x)


def _rows_into_cols(dst, rows, col, name):
    t = dst.shape[0]
    tt = _tile(t, _ROW_TILES)

    def body(r_ref, _, o_ref):
        o_ref[...] = r_ref[...].T.astype(o_ref.dtype)

    return pl.pallas_call(
        body, name=name, grid=(t // tt,), in_specs=[pl.BlockSpec((LANES, tt), lambda i: (0, i)), _ANY],
        out_specs=pl.BlockSpec((tt, LANES), lambda i: (i, col // LANES)),
        out_shape=jax.ShapeDtypeStruct(dst.shape, dst.dtype), input_output_aliases={1: 0},
        compiler_params=_params(("parallel",)))(rows, dst)


def _adamw(g, w, m, v):
    m = ADAM_B1 * m + (1.0 - ADAM_B1) * g
    v = ADAM_B2 * v + (1.0 - ADAM_B2) * (g * g)
    m_hat = m / (1.0 - ADAM_B1 ** ADAM_STEP)
    v_hat = v / (1.0 - ADAM_B2 ** ADAM_STEP)
    return -ADAM_LR * (m_hat / (jnp.sqrt(v_hat) + ADAM_EPS) + ADAM_WD * w), m, v


def _ada_fwd(c_all, ada_w, name):
    nl, d, cols = ada_w.shape
    b = c_all.shape[0]

    def body(c_ref, w_ref, o_ref):
        o_ref[...] = _mm_hi(_silu(c_ref[...]), w_ref[...])

    return pl.pallas_call(
        body, name=name, grid=(nl,),
        in_specs=[pl.BlockSpec((b, d), lambda i: (0, 0)), pl.BlockSpec((None, d, cols), lambda i: (i, 0, 0))],
        out_specs=pl.BlockSpec((None, b, cols), lambda i: (i, 0, 0)),
        out_shape=jax.ShapeDtypeStruct((nl, b, cols), F32), compiler_params=_params(("parallel",)))(c_all, ada_w)


def _ada_bwd(c_col, dm, w, m, v, name):
    nl, d, cols = w.shape
    b = c_col.shape[0]
    tr = _tile(d, (256, 128))

    def body(c_ref, dm_ref, w_ref, m_ref, v_ref, g_ref, dl_ref, mo_ref, vo_ref):
        g = _silu(c_ref[0]) * dm_ref[pl.ds(0, 1), :]
        for j in range(1, b):
            g = g + _silu(c_ref[j]) * dm_ref[pl.ds(j, 1), :]
        g_ref[...] = g
        dl_ref[...], mo_ref[...], vo_ref[...] = _adamw(g, w_ref[...], m_ref[...], v_ref[...])

    blk = pl.BlockSpec((None, tr, cols), lambda l, i: (l, i, 0))
    shape = jax.ShapeDtypeStruct((nl, d, cols), F32)
    return pl.pallas_call(
        body, name=name, grid=(nl, d // tr),
        in_specs=[pl.BlockSpec((b, tr, 1), lambda l, i: (0, i, 0)), pl.BlockSpec((None, b, cols), lambda l, i: (l, 0, 0)),
                  blk, blk, blk],
        out_specs=[blk, blk, blk, blk], out_shape=[shape] * 4,
        compiler_params=_params(("parallel", "parallel")))(c_col, dm, w, m, v)


_GRAD_ROW_TILES = (256, 128, 176, 88)


def _sum_adam(parts, w, m, v, name):
    nl, npart, r, cdim = parts.shape
    tr = _tile(r, _GRAD_ROW_TILES)

    def body(p_ref, w_ref, m_ref, v_ref, g_ref, dl_ref, mo_ref, vo_ref):
        g = p_ref[0].astype(F32)
        for j in range(1, npart):
            g = g + p_ref[j].astype(F32)
        g_ref[...] = g
        dl_ref[...], mo_ref[...], vo_ref[...] = _adamw(g, w_ref[...], m_ref[...], v_ref[...])

    blk = pl.BlockSpec((None, tr, cdim), lambda l, i: (l, i, 0))
    shape = jax.ShapeDtypeStruct((nl, r, cdim), F32)
    return pl.pallas_call(
        body, name=name, grid=(nl, r // tr),
        in_specs=[pl.BlockSpec((None, npart, tr, cdim), lambda l, i: (l, 0, i, 0)), blk, blk, blk],
        out_specs=[blk, blk, blk, blk], out_shape=[shape] * 4,
        compiler_params=_params(("parallel", "parallel")))(parts, w, m, v)


def _cols_from_blocks(g, plan, width, name):
    _, r, cdim = g.shape
    tr = _tile(r, (256, 128))
    covered = sorted((dst, dst + n) for _, _, n, dst in plan)
    holes, pos = [], 0
    for a, b in covered:
        if a > pos:
            holes.append((pos, a))
        pos = max(pos, b)
    if pos < width:
        holes.append((pos, width))

    def body(g_ref, o_ref):
        for a, b in holes:
            o_ref[:, a:b] = jnp.zeros((tr, b - a), o_ref.dtype)
        for j, src, n, dst in plan:
            o_ref[:, dst:dst + n] = g_ref[j, :, src:src + n]

    return pl.pallas_call(
        body, name=name, grid=(r // tr,), in_specs=[pl.BlockSpec((N_DEV, tr, cdim), lambda i: (0, i, 0))],
        out_specs=pl.BlockSpec((tr, width), lambda i: (i, 0)), out_shape=jax.ShapeDtypeStruct((r, width), g.dtype),
        compiler_params=_params(("parallel",)))(g)


def _blocks_from_cols(w, plan, cdim, name):
    r, width = w.shape
    tr = _tile(r, (256, 128))

    def body(w_ref, o_ref):
        for j, src, n, dst in plan:
            o_ref[j, :, src:src + n] = w_ref[:, dst:dst + n]

    return pl.pallas_call(
        body, name=name, grid=(r // tr,), in_specs=[pl.BlockSpec((tr, width), lambda i: (i, 0))],
        out_specs=pl.BlockSpec((N_DEV, tr, cdim), lambda i: (0, i, 0)),
        out_shape=jax.ShapeDtypeStruct((N_DEV, r, cdim), w.dtype), compiler_params=_params(("parallel",)))(w)


def _pair_sum(x, tmp, core, name):
    _, r, cdim = x.shape
    tr = _tile(r, _GRAD_ROW_TILES)

    def body(core_ref, x_ref, t_ref, o_ref):
        o_ref[...] = (x_ref[...] + t_ref[...]).astype(o_ref.dtype)

    grid_spec = pltpu.PrefetchScalarGridSpec(
        num_scalar_prefetch=1, grid=(N_DEV // 2, r // tr),
        in_specs=[pl.BlockSpec((None, tr, cdim), lambda ch, i, core_ref: (2 * ch + core_ref[0], i, 0)),
                  pl.BlockSpec((None, tr, cdim), lambda ch, i, core_ref: (ch, i, 0))],
        out_specs=pl.BlockSpec((None, tr, cdim), lambda ch, i, core_ref: (ch, i, 0)))
    return pl.pallas_call(
        body, name=name, grid_spec=grid_spec, out_shape=jax.ShapeDtypeStruct((N_DEV // 2, r, cdim), WIRE_DTYPE),
        compiler_params=_params(("parallel", "parallel")))(core, x, tmp)


_ANY = pl.BlockSpec(memory_space=pl.ANY)
_CHIP_FLIPS = ((1, 0), (0, 1), (1, 1))


def _coords():
    return lax.axis_index("x"), lax.axis_index("y"), lax.axis_index("c")


def _flip(v, f):
    return 1 - v if f else v


def _a2a_direct(xs, name):
    n, ncp = len(xs), N_DEV - 1

    def body(*refs):
        ins, outs = refs[:n], refs[n:2 * n]
        send, recv, loc = refs[2 * n:]
        x, y, c = _coords()
        me = 4 * x + 2 * y + c
        local = [pltpu.make_async_copy(ins[i].at[me], outs[i].at[me], loc.at[i]) for i in range(n)]
        for cp in local:
            cp.start()
        remote = []
        for i in range(n):
            for k in range(1, N_DEV):
                px, py, pc = _flip(x, k & 4), _flip(y, k & 2), _flip(c, k & 1)
                cp = pltpu.make_async_remote_copy(
                    src_ref=ins[i].at[4 * px + 2 * py + pc], dst_ref=outs[i].at[me],
                    send_sem=send.at[i * ncp + k - 1], recv_sem=recv.at[i * ncp + k - 1],
                    device_id=(px, py, pc), device_id_type=MESH)
                cp.start()
                remote.append(cp)
        for cp in remote:
            cp.wait()
        for cp in local:
            cp.wait()

    return pl.pallas_call(
        body, name=name, in_specs=[_ANY] * n, out_specs=[_ANY] * n,
        out_shape=[jax.ShapeDtypeStruct(a.shape, a.dtype) for a in xs],
        scratch_shapes=[pltpu.SemaphoreType.DMA((n * ncp,)), pltpu.SemaphoreType.DMA((n * ncp,)),
                        pltpu.SemaphoreType.DMA((n,))])(*xs)


class _AllGatherSide:
    def __init__(self, blocks):
        self.operands = list(blocks)
        n = self.n = len(self.operands)
        self.n_in = self.n_out = n
        self.out_shape = [jax.ShapeDtypeStruct((N_DEV,) + a.shape, a.dtype) for a in self.operands]
        self.aliases = {}
        nici, nd2d = len(_CHIP_FLIPS), N_DEV // 2
        self.scratch = [pltpu.SemaphoreType.DMA((n * nici,)), pltpu.SemaphoreType.DMA((n * nici,)),
                        pltpu.SemaphoreType.DMA((n * nd2d,)), pltpu.SemaphoreType.DMA((n * nd2d,)),
                        pltpu.SemaphoreType.DMA((n,))]

    def _first(self, ins, outs, sems):
        send, recv, _, _, loc = sems
        x, y, c = _coords()
        me = 4 * x + 2 * y + c
        nici = len(_CHIP_FLIPS)
        local = [pltpu.make_async_copy(ins[i], outs[i].at[me], loc.at[i]) for i in range(self.n)]
        remote = [pltpu.make_async_remote_copy(
            src_ref=ins[i], dst_ref=outs[i].at[me], send_sem=send.at[i * nici + j], recv_sem=recv.at[i * nici + j],
            device_id=(_flip(x, fx), _flip(y, fy), c), device_id_type=MESH)
            for i in range(self.n) for j, (fx, fy) in enumerate(_CHIP_FLIPS)]
        return local + remote

    def _second(self, outs, sems):
        _, _, send, recv, _ = sems
        x, y, c = _coords()
        nd2d = N_DEV // 2
        return [pltpu.make_async_remote_copy(
            src_ref=outs[i].at[2 * ch + c], dst_ref=outs[i].at[2 * ch + c], send_sem=send.at[i * nd2d + ch],
            recv_sem=recv.at[i * nd2d + ch], device_id=(x, y, 1 - c), device_id_type=MESH)
            for i in range(self.n) for ch in range(nd2d)]

    def start(self, ins, outs, sems):
        for cp in self._first(ins, outs, sems):
            cp.start()

    def finish(self, ins, outs, sems):
        for cp in self._first(ins, outs, sems):
            cp.wait()
        second = self._second(outs, sems)
        for cp in second:
            cp.start()
        for cp in second:
            cp.wait()


class _ReduceScatterIciSide:
    def __init__(self, sums, accs, layer):
        self.operands = list(sums) + list(accs)
        n = self.n = len(sums)
        self.layer = layer
        self.n_in, self.n_out = 2 * n, n
        self.out_shape = [jax.ShapeDtypeStruct(a.shape, a.dtype) for a in accs]
        self.aliases = {n + i: i for i in range(n)}
        nici = len(_CHIP_FLIPS)
        self.scratch = [pltpu.SemaphoreType.DMA((n * nici,)), pltpu.SemaphoreType.DMA((n * nici,)),
                        pltpu.SemaphoreType.DMA((n,))]

    def _copies(self, ins, outs, sems):
        send, recv, loc = sems
        x, y, c = _coords()
        chip = 2 * x + y
        nici = len(_CHIP_FLIPS)
        local = [pltpu.make_async_copy(ins[i].at[chip], outs[i].at[self.layer, chip], loc.at[i])
                 for i in range(self.n)]
        remote = [pltpu.make_async_remote_copy(
            src_ref=ins[i].at[2 * _flip(x, fx) + _flip(y, fy)], dst_ref=outs[i].at[self.layer, chip],
            send_sem=send.at[i * nici + j], recv_sem=recv.at[i * nici + j],
            device_id=(_flip(x, fx), _flip(y, fy), c), device_id_type=MESH)
            for i in range(self.n) for j, (fx, fy) in enumerate(_CHIP_FLIPS)]
        return local + remote

    def start(self, ins, outs, sems):
        for cp in self._copies(ins, outs, sems):
            cp.start()

    def finish(self, ins, outs, sems):
        for cp in self._copies(ins, outs, sems):
            cp.wait()


def _run_side(side, name):
    def body(*refs):
        ins, outs = refs[:side.n_in], refs[side.n_in:side.n_in + side.n_out]
        sems = refs[side.n_in + side.n_out:]
        side.start(ins, outs, sems)
        side.finish(ins, outs, sems)

    return pl.pallas_call(
        body, name=name, in_specs=[_ANY] * side.n_in, out_specs=[_ANY] * side.n_out, out_shape=side.out_shape,
        input_output_aliases=side.aliases, scratch_shapes=side.scratch)(*side.operands)


class _ReduceScatterD2dSide:
    def __init__(self, parts):
        self.operands = list(parts)
        n = self.n = len(self.operands)
        self.n_in = self.n_out = n
        nd2d = N_DEV // 2
        self.out_shape = [jax.ShapeDtypeStruct((nd2d,) + a.shape[1:], a.dtype) for a in self.operands]
        self.aliases = {}
        self.scratch = [pltpu.SemaphoreType.DMA((n * nd2d,)), pltpu.SemaphoreType.DMA((n * nd2d,))]

    def _copies(self, ins, outs, sems):
        send, recv = sems
        x, y, c = _coords()
        nd2d = N_DEV // 2
        return [pltpu.make_async_remote_copy(
            src_ref=ins[i].at[2 * ch + 1 - c], dst_ref=outs[i].at[ch], send_sem=send.at[i * nd2d + ch],
            recv_sem=recv.at[i * nd2d + ch], device_id=(x, y, 1 - c), device_id_type=MESH)
            for i in range(self.n) for ch in range(nd2d)]

    def start(self, ins, outs, sems):
        for cp in self._copies(ins, outs, sems):
            cp.start()

    def finish(self, ins, outs, sems):
        for cp in self._copies(ins, outs, sems):
            cp.wait()


_PACK_ROWS = 256


def _pack(arrs):
    flat = jnp.concatenate([a.reshape(-1) for a in arrs])
    quantum = _PACK_ROWS * LANES
    total = -(-flat.shape[0] // quantum) * quantum
    return jnp.pad(flat, (0, total - flat.shape[0])).reshape(-1, LANES)


def _unpack(packed, like):
    flat, out, pos = packed.reshape(-1), [], 0
    for a in like:
        out.append(flat[pos:pos + a.size].reshape(a.shape))
        pos += a.size
    return out


def kernel(x, c, ada_w, ada_b, norm1_g, w_in, conv_w, spatial_w, spatial_b, v_norm_g, a_log, dt_bias, o_norm_g, w_branch_a, w_branch_b, w_out, norm2_g, w_ffn_in, w_ffn_out, final_g, loss_target, m_ada_w, m_ada_b, m_norm1_g, m_w_in, m_conv_w, m_spatial_w, m_spatial_b, m_v_norm_g, m_a_log, m_dt_bias, m_o_norm_g, m_w_branch_a, m_w_branch_b, m_w_out, m_norm2_g, m_w_ffn_in, m_w_ffn_out, m_final_g, v_ada_w, v_ada_b, v_norm1_g, v_w_in, v_conv_w, v_spatial_w, v_spatial_b, v_v_norm_g, v_a_log, v_dt_bias, v_o_norm_g, v_w_branch_a, v_w_branch_b, v_w_out, v_norm2_g, v_w_ffn_in, v_w_ffn_out, v_final_g):
    nl, d = ada_w.shape[0], x.shape[2]
    t = x.shape[1]
    nchunk = t // GDN_CHUNK
    xi, yi, ci = _coords()
    me = 4 * xi + 2 * yi + ci
    core = jnp.reshape(ci, (1,)).astype(jnp.int32)
    x0, target = x[0], loss_target[0]
    wcols = 3 * HEADS * HEAD_DIM
    lay = _ProjLayout(d)
    in_pieces = lay.pieces(w_in.shape[2])
    fi_shard = w_ffn_in.shape[2]
    fi_pieces = [(j, 0, fi_shard, fi_shard * j) for j in range(N_DEV)]

    c_all, cw_all = _a2a_direct([jnp.broadcast_to(c[None], (N_DEV,) + c.shape),
                                 jnp.broadcast_to(conv_w[None], (N_DEV,) + conv_w.shape)], "gather_small")
    c_all = c_all[:, 0]
    conv_full = cw_all.transpose(1, 2, 0, 3).reshape(nl, CONV_K, wcols)
    modp = _ada_fwd(c_all, ada_w, "ada_fwd")
    (modx,) = _a2a_direct([modp.transpose(1, 0, 2)], "mod_exchange")
    mod = (modx.transpose(1, 0, 2).reshape(nl, 6 * d) + ada_b).reshape(nl, 6, 1, d)

    big = (w_in, w_branch_a, w_branch_b, w_out, w_ffn_in, w_ffn_out)
    big_wire = [w.astype(WIRE_DTYPE) for w in big]
    gather_in = lambda i: _AllGatherSide([big_wire[0][i]])
    gather_early = lambda i: _AllGatherSide([big_wire[k][i] for k in (1, 2, 3, 5)])
    gather_late = lambda i: _AllGatherSide([big_wire[4][i]] + ([big_wire[0][i + 1]] if i + 1 < nl else []))
    row_full = lambda g: g.reshape(-1, g.shape[2])
    padded_in = lambda g: _cols_from_blocks(g, in_pieces, lay.width, "w_in_cols")
    w_pads = [padded_in(_run_side(gather_in(0), "ag_first")[0])] + [None] * (nl - 1)
    weights = [None] * nl

    def rows_of(ba_rows, lo):
        return ba_rows[lo:lo + HEADS].reshape(HEADS, nchunk, 1, GDN_CHUNK)

    saved = []
    x_cur, delta, gt_prev = x0, None, None
    for i in range(nl):
        sh1, sc1, gt1, sh2, sc2, gt2 = (mod[i, k] for k in range(6))
        s = dict(gt1=gt1, gt2=gt2, sc1=sc1, sc2=sc2)
        s["x_in"], s["h"] = _resid_norm(x_cur, delta, gt_prev, norm1_g[i][None], sc1, sh1, "norm1_fwd")
        s["proj"], g_a, g_b, g_o, g_fo = _matmul(s["h"], w_pads[i], "nn", "proj_fwd", out_dtype=ACT_DTYPE,
                                                 side=gather_early(i))
        ba = _matmul(s["h"], w_pads[i][:, lay.ba:], "nn", "proj_ba_fwd")
        s["b_col"] = spatial_b[i][:, :, None]
        s["ya"] = _mixer_a_fwd(s["proj"], lay.uv, spatial_w[i], s["b_col"], v_norm_g[i][None], "mixer_a_fwd")
        s["qkv_h"] = _conv_fwd(s["proj"], conv_full[i], "conv_fwd")
        ba_rows = _cols_as_rows(ba, 0, "ba_rows")
        s["braw"], s["araw"] = rows_of(ba_rows, 0), rows_of(ba_rows, HEADS)
        s["o"], s["states"], s["t_mats"], g_fi, *g_in = _gdn_fwd(
            s["qkv_h"], s["araw"], s["braw"], a_log[i], dt_bias[i], "gdn_fwd", gather_late(i))
        if g_in:
            w_pads[i + 1] = padded_in(g_in[0])
        weights[i] = (row_full(g_a), row_full(g_b), row_full(g_o),
                      _cols_from_blocks(g_fi, fi_pieces, N_DEV * fi_shard, "w_ffn_in_cols"), row_full(g_fo))
        w_a, w_b, w_o, w_fi, w_fo = weights[i]
        s["yb"] = _gdn_post_fwd(s["o"], s["proj"], lay.z, o_norm_g[i][None], "gdn_post_fwd")
        s["pa"], s["pb"], s["merged"] = _branches_merge(s["ya"], w_a, s["yb"], w_b, s["proj"], lay.gates,
                                                         "branches_merge_fwd")
        s["mo"] = _matmul(s["merged"], w_o, "nn", "out_fwd")
        s["x1"], s["h2"] = _resid_norm(s["x_in"], s["mo"], gt1, norm2_g[i][None], sc2, sh2, "norm2_fwd")
        s["gu"], s["a"] = _ffn_in_swiglu(s["h2"], w_fi, "ffn_in_swiglu_fwd")
        s["fo"] = _matmul(s["a"], w_fo, "nn", "ffn_out_fwd")
        saved.append(s)
        x_cur, delta, gt_prev = s["x1"], s["fo"], gt2
    dx, d_final_g, loss_tile = _final_loss(x_cur, delta, gt_prev, final_g[None], target, "final_loss")
    loss = lax.psum(loss_tile[0, 0], ("x", "y", "c"))

    big_shapes = [(d, w_in.shape[2]), w_branch_a.shape[1:], w_branch_b.shape[1:], w_out.shape[1:],
                  (w_ffn_in.shape[2], d), w_ffn_out.shape[1:]]
    accs = [lax.empty((nl, N_DEV // 2) + tuple(sh), WIRE_DTYPE) for sh in big_shapes]
    row_blocks = lambda g: g.reshape(N_DEV, -1, g.shape[1])
    dmod, small = [None] * nl, [None] * nl
    d_conv = [None] * nl
    parts, sums = None, None
    beside_gdn, beside_dw, beside_dx = (0,), (4,), (1, 2, 3, 5)
    rep_parts = [None] * nl
    rep_pack = lambda i: _pack((dmod[i],) + small[i])

    def scatter_side(idx, layer):
        if sums is None:
            return _NoSide
        return _ReduceScatterIciSide([sums[k] for k in idx], [accs[k] for k in idx], layer)

    def scattered_into(accs, idx, new):
        accs = list(accs)
        for k, a in zip(idx, new):
            accs[k] = a
        return accs

    for i in reversed(range(nl)):
        s = saved[i]
        w_a, w_b, w_o, w_fi, w_fo = weights[i]
        if i == nl - 1:
            dfo, dgt2 = _gate_bwd(dx, s["fo"], s["gt2"], "gate2_bwd")
        else:
            dgt2 = dgt2_before
        g_fo = _matmul(s["a"], dfo, "tn", "ffn_out_dw")
        dgu = _ffn_out_dx_swiglu(dfo, w_fo, s["gu"], "ffn_out_dx_swiglu")
        if parts is None:
            g_fi = _matmul(dgu, s["h2"], "tn", "ffn_in_dw")
        else:
            g_fi, *other = _matmul(dgu, s["h2"], "tn", "ffn_in_dw", side=_ReduceScatterD2dSide(parts))
            sums = [_pair_sum(p, o, core, "rs_pair_sum_%d" % k) for k, (p, o) in enumerate(zip(parts, other))]
        if i + 1 < nl:
            dh2, rep_parts[i + 1] = _matmul(dgu, w_fi, "nt", "ffn_in_dx", side=_AllGatherSide([rep_pack(i + 1)]))
        else:
            dh2 = _matmul(dgu, w_fi, "nt", "ffn_in_dx")
        dx1, dsh2, dsc2, dg2, dmo, dgt1 = _norm_bwd(s["x1"], dh2, dx, norm2_g[i][None], s["sc2"], "norm2_bwd",
                                                    gate=(s["mo"], s["gt1"]))
        g_o = _matmul(s["merged"], dmo, "tn", "out_dw")
        dmerged = _matmul(dmo, w_o, "nt", "out_dx")
        dproj = lax.empty((t, lay.width), MXU_DTYPE)
        dpa, dpb, dproj = _merge_bwd(dmerged, s["pa"], s["pb"], s["proj"], lay.gates, dproj, "merge_bwd")
        g_a = _matmul(s["ya"], dpa, "tn", "branch_a_dw")
        dya = _matmul(dpa, w_a, "nt", "branch_a_dx")
        g_b = _matmul(s["yb"], dpb, "tn", "branch_b_dw")
        dyb = _matmul(dpb, w_b, "nt", "branch_b_dx")
        dproj, d_ws, d_bs, d_gv = _mixer_a_bwd(s["proj"], lay.uv, dya, spatial_w[i], jnp.swapaxes(spatial_w[i], 1, 2),
                                               s["b_col"], v_norm_g[i][None], dproj, "mixer_a_bwd")
        do, dproj, d_go = _gdn_post_bwd(s["o"], s["proj"], lay.z, dyb, o_norm_g[i][None], dproj, "gdn_post_bwd")
        dq, dk, dv, d_ar, d_br, d_al, d_dt, *scattered = _gdn_bwd(
            s["qkv_h"], s["araw"], s["braw"], a_log[i], dt_bias[i], s["states"], s["t_mats"], do, "gdn_bwd",
            scatter_side(beside_gdn, i + 1))
        accs = scattered_into(accs, beside_gdn, scattered)
        dacc, d_conv[i] = _conv_bwd_pre(s["proj"], dq, dk, dv, conv_full[i], "conv_bwd_pre")
        dproj = _conv_bwd_in(dacc, conv_full[i], dproj, "conv_bwd_in")
        dba_rows = jnp.pad(jnp.concatenate([d_br.reshape(HEADS, t), d_ar.reshape(HEADS, t)]),
                           ((0, LANES - 2 * HEADS), (0, 0)))
        dproj = _rows_into_cols(dproj, dba_rows, lay.ba, "dproj_ba")
        if sums is None:
            g_pad = _matmul(s["h"], dproj, "tn", "proj_dw")
            dh = _matmul(dproj, w_pads[i], "nt", "proj_dx")
        else:
            g_pad, *scattered = _matmul(s["h"], dproj, "tn", "proj_dw", side=scatter_side(beside_dw, i + 1))
            accs = scattered_into(accs, beside_dw, scattered)
            dh, *scattered = _matmul(dproj, w_pads[i], "nt", "proj_dx", side=scatter_side(beside_dx, i + 1))
            accs = scattered_into(accs, beside_dx, scattered)
        if i > 0:
            dx, dsh1, dsc1, dg1, dfo, dgt2_before = _norm_bwd(s["x_in"], dh, dx1, norm1_g[i][None], s["sc1"], "norm1_bwd",
                                                              gate=(saved[i - 1]["fo"], saved[i - 1]["gt2"]))
        else:
            dx, dsh1, dsc1, dg1 = _norm_bwd(s["x_in"], dh, dx1, norm1_g[i][None], s["sc1"], "norm1_bwd")
        dmod[i] = jnp.concatenate([dsh1, dsc1, dgt1, dsh2, dsc2, dgt2], axis=1)[0]
        small[i] = (dg1[0], d_ws, d_bs[:, :, 0], d_gv[0], d_al[:, 0, 0], d_dt[:, 0, 0], d_go[0], dg2[0])
        parts = [_blocks_from_cols(g_pad, in_pieces, w_in.shape[2], "w_in_blocks"), row_blocks(g_a), row_blocks(g_b),
                 row_blocks(g_o), row_blocks(g_fi), row_blocks(g_fo)]
    other = _run_side(_ReduceScatterD2dSide(parts), "rs_d2d_last")
    sums = [_pair_sum(p, o, core, "rs_pair_sum_%d" % k) for k, (p, o) in enumerate(zip(parts, other))]
    accs = _run_side(_ReduceScatterIciSide(sums, accs, 0), "rs_ici_last")

    rep_w = (ada_b, norm1_g, spatial_w, spatial_b, v_norm_g, a_log, dt_bias, o_norm_g, norm2_g)
    rep_m = (m_ada_b, m_norm1_g, m_spatial_w, m_spatial_b, m_v_norm_g, m_a_log, m_dt_bias, m_o_norm_g, m_norm2_g)
    rep_v = (v_ada_b, v_norm1_g, v_spatial_w, v_spatial_b, v_v_norm_g, v_a_log, v_dt_bias, v_o_norm_g, v_norm2_g)
    rep_parts[0], fin_parts = _run_side(_AllGatherSide([rep_pack(0), _pack([d_final_g[0]])]), "small_grads_last")
    dmod = jnp.stack(dmod)
    d_conv_blocks = jnp.stack(d_conv).reshape(nl, CONV_K, N_DEV, -1).transpose(2, 0, 1, 3).reshape(N_DEV, -1, LANES)
    dmod_blocks = dmod.reshape(nl, N_DEV, -1).transpose(1, 0, 2)
    conv_all, dmod_all = _a2a_direct([d_conv_blocks, dmod_blocks], "small_grads_scatter")
    by_layer = lambda arrs: jnp.stack([_pack([a[i] for a in arrs]) for i in range(nl)])
    rep_out = _sum_adam(jnp.stack(rep_parts), by_layer(rep_w), by_layer(rep_m), by_layer(rep_v), "adam_small")
    fin_out = _sum_adam(fin_parts[None], _pack([final_g])[None], _pack([m_final_g])[None], _pack([v_final_g])[None],
                        "adam_final_g")
    layer_like = [a[0] for a in rep_w]
    rep_out = [[jnp.stack(per_layer) for per_layer in zip(*[_unpack(o[i], layer_like) for i in range(nl)])]
               + _unpack(f[0], [final_g]) for o, f in zip(rep_out, fin_out)]
    conv_out = _sum_adam(conv_all[None], conv_w.reshape(1, -1, LANES), m_conv_w.reshape(1, -1, LANES),
                         v_conv_w.reshape(1, -1, LANES), "adam_conv")
    conv_out = [o.reshape(conv_w.shape) for o in conv_out]
    ada_out = _ada_bwd(c_all[:, :, None], dmod_all.transpose(1, 0, 2), ada_w, m_ada_w, v_ada_w, "ada_bwd_adam")
    big_m = (m_w_in, m_w_branch_a, m_w_branch_b, m_w_out, m_w_ffn_in, m_w_ffn_out)
    big_v = (v_w_in, v_w_branch_a, v_w_branch_b, v_w_out, v_w_ffn_in, v_w_ffn_out)
    turn = lambda k, a: jnp.swapaxes(a, 1, 2) if k == 4 else a
    big_out = [[turn(k, o) for o in _sum_adam(accs[k], turn(k, big[k]), turn(k, big_m[k]), turn(k, big_v[k]),
                                             "adam_big_%d" % k)] for k in range(6)]

    def ordered(kind):
        rep = rep_out[kind]
        return (ada_out[kind], rep[0], rep[1], big_out[0][kind], conv_out[kind], rep[2], rep[3], rep[4], rep[5],
                rep[6], rep[7], big_out[1][kind], big_out[2][kind], big_out[3][kind], rep[8], big_out[4][kind],
                big_out[5][kind], rep[9])

    return (loss, dx[None]) + ordered(0) + ordered(1) + ordered(2) + ordered(3)
```

```python
import functools

import jax
import jax.numpy as jnp
from jax import lax
from jax.experimental import pallas as pl
from jax.experimental.pallas import tpu as pltpu

F32 = jnp.float32
BF16 = jnp.bfloat16
MXU_DTYPE = BF16
WIRE_DTYPE = BF16
ACT_DTYPE = BF16
EPS = 1e-6
LANES = 128
SUBLANES = 8
GDN_CHUNK = 128
A_CHUNK = 128
GROUPS = 8
HEADS = 8
HEAD_DIM = 128
CONV_K = 4
N_DEV = 8
VMEM_LIMIT = 48 * 1024 * 1024
MESH = pl.DeviceIdType.MESH

ADAM_LR = 0.001
ADAM_B1 = 0.9
ADAM_B2 = 0.999
ADAM_EPS = 1e-08
ADAM_WD = 0.01
ADAM_STEP = 10

_NN = (((1,), (0,)), ((), ()))
_NT = (((1,), (1,)), ((), ()))
_TN = (((0,), (0,)), ((), ()))


def _mm(a, b, dims=_NN):
    return lax.dot_general(a.astype(MXU_DTYPE), b.astype(MXU_DTYPE), dims, preferred_element_type=F32)


def _mm_hi(a, b):
    return lax.dot_general(a, b, _NN, precision=lax.Precision.HIGHEST, preferred_element_type=F32)


def _tile(n, cands):
    for c in cands:
        if n % c == 0:
            return c
    return n


def _params(sem=None):
    return pltpu.CompilerParams(dimension_semantics=sem, vmem_limit_bytes=VMEM_LIMIT)


def _sigmoid(x):
    return 1.0 / (1.0 + jnp.exp(-x))


def _silu(x):
    return x * _sigmoid(x)


_GELU_C = 0.7978845608028654
_GELU_A = 0.044715


def _gelu(x):
    return 0.5 * x * (1.0 + jnp.tanh(_GELU_C * (x + _GELU_A * x * x * x)))


def _gelu_and_slope(x):
    t = jnp.tanh(_GELU_C * (x + _GELU_A * x * x * x))
    return 0.5 * x * (1.0 + t), 0.5 * (1.0 + t) + 0.5 * x * (1.0 - t * t) * _GELU_C * (1.0 + 3.0 * _GELU_A * x * x)


def _softplus(x):
    return jnp.maximum(x, 0.0) + jnp.log(1.0 + jnp.exp(-jnp.abs(x)))


_MM_TILES = (1024, 1408, 1664, 512, 256, 128)


class _NoSide:
    operands, out_shape, scratch, aliases, n_in, n_out = [], [], [], {}, 0, 0


def _side_hooks(side, refs, n_main_in, n_main_out, n_main_scratch, grid):
    a = n_main_in + side.n_in
    b = a + n_main_out + side.n_out
    ins, outs, sems = refs[n_main_in:a], refs[a + n_main_out:b], refs[b + n_main_scratch:]
    main = refs[:n_main_in] + refs[a:a + n_main_out] + refs[b:b + n_main_scratch]
    ids = [pl.program_id(k) for k in range(len(grid))]

    def start():
        if side.n_in:
            pl.when(functools.reduce(jnp.logical_and, [i == 0 for i in ids]))(lambda: side.start(ins, outs, sems))

    def finish():
        if side.n_in:
            last = functools.reduce(jnp.logical_and, [i == g - 1 for i, g in zip(ids, grid)])
            pl.when(last)(lambda: side.finish(ins, outs, sems))

    return main, start, finish


def _carrier_call(body, name, grid, in_specs, out_specs, out_shape, scratch, side, args):
    aliases = {len(in_specs) + k: len(out_specs) + v for k, v in side.aliases.items()}
    return pl.pallas_call(
        body, name=name, grid=grid, in_specs=list(in_specs) + [_ANY] * side.n_in,
        out_specs=list(out_specs) + [_ANY] * side.n_out, out_shape=list(out_shape) + list(side.out_shape),
        scratch_shapes=list(scratch) + list(side.scratch), input_output_aliases=aliases,
        compiler_params=_params(("arbitrary",) * len(grid)))(*args, *side.operands)


_MM_VMEM_BUDGET = 44 * 1024 * 1024


def _matmul_tiles(mode, m, n, k, out_bytes):
    tk = _tile(k, _MM_TILES)
    tm = _tile(m, _MM_TILES)
    in_bytes = jnp.dtype(MXU_DTYPE).itemsize
    for tn in _MM_TILES:
        if n % tn:
            continue
        need = 2 * in_bytes * (tm * tk + tk * tn) + tm * tn * (2 * out_bytes + (4 if k > tk else 0))
        if need <= _MM_VMEM_BUDGET:
            return tm, tn, tk
    return tm, _tile(n, (LANES,)), tk


def _matmul(a, b, mode, name, out_dtype=F32, side=_NoSide):
    planes, plane_cols = (a.shape[0], a.shape[2]) if a.ndim == 3 else (1, a.shape[1])
    a_shape = (a.shape[-2], planes * plane_cols)
    if mode == "nn":
        (m, k), n = a_shape, b.shape[1]
    elif mode == "nt":
        (m, k), n = a_shape, b.shape[0]
    else:
        (k, m), n = a_shape, b.shape[1]
    tm, tn, tk = _matmul_tiles(mode, m, n, k, jnp.dtype(out_dtype).itemsize)
    nk = k // tk
    grid = (m // tm, n // tn, nk)
    dims = {"nn": _NN, "nt": _NT, "tn": _TN}[mode]

    def body(*refs):
        (a_ref, b_ref, o_ref, acc_ref), side_start, side_finish = _side_hooks(side, refs, 2, 1, 1, grid)
        kk = pl.program_id(2)
        side_start()
        if nk == 1:
            o_ref[...] = _mm(a_ref[...], b_ref[...], dims).astype(o_ref.dtype)
        else:
            @pl.when(kk == 0)
            def _():
                acc_ref[...] = _mm(a_ref[...], b_ref[...], dims)

            @pl.when(jnp.logical_and(kk > 0, kk < nk - 1))
            def _():
                acc_ref[...] += _mm(a_ref[...], b_ref[...], dims)

            @pl.when(kk == nk - 1)
            def _():
                o_ref[...] = (acc_ref[...] + _mm(a_ref[...], b_ref[...], dims)).astype(o_ref.dtype)

        side_finish()

    if a.ndim == 3:
        per = plane_cols // (tm if mode == "tn" else tk)
        a_spec = (pl.BlockSpec((None, tk, tm), lambda i, j, l: (i // per, l, i % per)) if mode == "tn"
                  else pl.BlockSpec((None, tm, tk), lambda i, j, l: (l // per, i, l % per)))
    else:
        a_spec = (pl.BlockSpec((tk, tm), lambda i, j, l: (l, i)) if mode == "tn"
                  else pl.BlockSpec((tm, tk), lambda i, j, l: (i, l)))
    b_spec = (pl.BlockSpec((tn, tk), lambda i, j, l: (j, l)) if mode == "nt"
              else pl.BlockSpec((tk, tn), lambda i, j, l: (l, j)))
    o_spec = pl.BlockSpec((tm, tn), lambda i, j, l: (i, j))
    out = _carrier_call(body, name, grid, [a_spec, b_spec], [o_spec], [jax.ShapeDtypeStruct((m, n), out_dtype)],
                        [pltpu.VMEM((tm, tn) if nk > 1 else (SUBLANES, LANES), F32)], side, (a, b))
    return out if side.n_in else out[0]


_ROW_TILES = (512, 256, 128)


def _resid_norm(x, delta, gt, g, sc, sh, name):
    t, d = x.shape
    tt = _tile(t, _ROW_TILES)
    has = delta is not None

    def body(*refs):
        if has:
            x_ref, d_ref, gt_ref, g_ref, sc_ref, sh_ref, xo_ref, h_ref = refs
            xv = x_ref[...] + gt_ref[...] * d_ref[...]
            xo_ref[...] = xv
        else:
            x_ref, g_ref, sc_ref, sh_ref, h_ref = refs
            xv = x_ref[...]
        r = lax.rsqrt(jnp.mean(xv * xv, axis=-1, keepdims=True) + EPS)
        y = xv * r * g_ref[...]
        h_ref[...] = (y * (1.0 + sc_ref[...]) + sh_ref[...]).astype(h_ref.dtype)

    row = pl.BlockSpec((tt, d), lambda i: (i, 0))
    vec = pl.BlockSpec((1, d), lambda i: (0, 0))
    if has:
        return pl.pallas_call(
            body, name=name, grid=(t // tt,), in_specs=[row, row, vec, vec, vec, vec], out_specs=[row, row],
            out_shape=[jax.ShapeDtypeStruct((t, d), F32), jax.ShapeDtypeStruct((t, d), MXU_DTYPE)],
            compiler_params=_params(("parallel",)))(x, delta, gt, g, sc, sh)
    h = pl.pallas_call(
        body, name=name + "_first", grid=(t // tt,), in_specs=[row, vec, vec, vec], out_specs=row,
        out_shape=jax.ShapeDtypeStruct((t, d), MXU_DTYPE), compiler_params=_params(("parallel",)))(x, g, sc, sh)
    return x, h


def _final_loss(x, delta, gt, g, target, name):
    t, d = x.shape
    tt = _tile(t, _ROW_TILES)

    def body(x_ref, d_ref, gt_ref, g_ref, tg_ref, dx_ref, dg_ref, loss_ref):
        @pl.when(pl.program_id(0) == 0)
        def _():
            dg_ref[...] = jnp.zeros_like(dg_ref)
            loss_ref[...] = jnp.zeros_like(loss_ref)

        xv = x_ref[...] + gt_ref[...] * d_ref[...]
        r = lax.rsqrt(jnp.mean(xv * xv, axis=-1, keepdims=True) + EPS)
        xh = xv * r
        diff = xh * g_ref[...] - tg_ref[...]
        loss_ref[...] += jnp.sum(diff * diff) * (0.5 / d)
        dy = diff * (1.0 / d)
        dg_ref[...] += jnp.sum(dy * xh, axis=0, keepdims=True)
        dxh = dy * g_ref[...]
        dx_ref[...] = r * (dxh - xh * jnp.mean(dxh * xh, axis=-1, keepdims=True))

    row = pl.BlockSpec((tt, d), lambda i: (i, 0))
    vec = pl.BlockSpec((1, d), lambda i: (0, 0))
    tile = pl.BlockSpec((SUBLANES, LANES), lambda i: (0, 0))
    return pl.pallas_call(
        body, name=name, grid=(t // tt,), in_specs=[row, row, vec, vec, row], out_specs=[row, vec, tile],
        out_shape=[jax.ShapeDtypeStruct((t, d), F32), jax.ShapeDtypeStruct((1, d), F32),
                   jax.ShapeDtypeStruct((SUBLANES, LANES), F32)],
        compiler_params=_params(("arbitrary",)))(x, delta, gt, g, target)


def _norm_bwd(x, dh, dres, g, sc, name, gate=None):
    t, d = x.shape
    tt = _tile(t, _ROW_TILES)
    gated = gate is not None

    def body(*refs):
        x_ref, dh_ref, dr_ref, g_ref, sc_ref = refs[:5]
        dx_ref, dsh_ref, dsc_ref, dg_ref = refs[5 + 2 * gated:9 + 2 * gated]

        @pl.when(pl.program_id(0) == 0)
        def _():
            for acc_ref in refs[6 + 2 * gated:9 + 2 * gated] + refs[10 + 2 * gated:]:
                acc_ref[...] = jnp.zeros_like(acc_ref)

        xv, dh = x_ref[...], dh_ref[...]
        r = lax.rsqrt(jnp.mean(xv * xv, axis=-1, keepdims=True) + EPS)
        xh = xv * r
        gv, sc1 = g_ref[...], 1.0 + sc_ref[...]
        dsh_ref[...] += jnp.sum(dh, axis=0, keepdims=True)
        dsc_ref[...] += jnp.sum(dh * xh, axis=0, keepdims=True) * gv
        dg_ref[...] += jnp.sum(dh * xh, axis=0, keepdims=True) * sc1
        dxh = dh * (gv * sc1)
        dx = dr_ref[...] + r * (dxh - xh * jnp.mean(dxh * xh, axis=-1, keepdims=True))
        dx_ref[...] = dx
        if gated:
            br_ref, gt_ref, db_ref, dgt_ref = refs[5], refs[6], refs[11], refs[12]
            db_ref[...] = (dx * gt_ref[...]).astype(db_ref.dtype)
            dgt_ref[...] += jnp.sum(dx * br_ref[...], axis=0, keepdims=True)

    row = pl.BlockSpec((tt, d), lambda i: (i, 0))
    vec = pl.BlockSpec((1, d), lambda i: (0, 0))
    vshape = jax.ShapeDtypeStruct((1, d), F32)
    in_specs, out_specs = [row, row, row, vec, vec], [row, vec, vec, vec]
    out_shape = [jax.ShapeDtypeStruct((t, d), F32), vshape, vshape, vshape]
    if gated:
        in_specs, out_specs = in_specs + [row, vec], out_specs + [row, vec]
        out_shape = out_shape + [jax.ShapeDtypeStruct((t, d), MXU_DTYPE), vshape]
    return pl.pallas_call(
        body, name=name + ("_gate" if gated else ""), grid=(t // tt,), in_specs=in_specs, out_specs=out_specs,
        out_shape=out_shape, compiler_params=_params(("arbitrary",)))(x, dh, dres, g, sc, *(gate or ()))


def _gate_bwd(dxo, branch, gt, name):
    t, d = dxo.shape
    tt = _tile(t, _ROW_TILES)

    def body(dx_ref, br_ref, gt_ref, db_ref, dgt_ref):
        @pl.when(pl.program_id(0) == 0)
        def _():
            dgt_ref[...] = jnp.zeros_like(dgt_ref)

        dx = dx_ref[...]
        db_ref[...] = (dx * gt_ref[...]).astype(db_ref.dtype)
        dgt_ref[...] += jnp.sum(dx * br_ref[...], axis=0, keepdims=True)

    row = pl.BlockSpec((tt, d), lambda i: (i, 0))
    vec = pl.BlockSpec((1, d), lambda i: (0, 0))
    return pl.pallas_call(
        body, name=name, grid=(t // tt,), in_specs=[row, row, vec], out_specs=[row, vec],
        out_shape=[jax.ShapeDtypeStruct((t, d), MXU_DTYPE), jax.ShapeDtypeStruct((1, d), F32)],
        compiler_params=_params(("arbitrary",)))(dxo, branch, gt)


def _ffn_in_swiglu(h, w, name):
    t, k = h.shape
    f = w.shape[1] // 2
    tm, tn = _tile(t, _MM_TILES), _tile(f, _MM_TILES)
    nj = f // tn

    def body(h_ref, wg_ref, wu_ref, gu_ref, a_ref):
        hv = h_ref[...]
        gu_ref[0] = _mm(hv, wg_ref[...]).astype(gu_ref.dtype)
        gu_ref[1] = _mm(hv, wu_ref[...]).astype(gu_ref.dtype)
        a_ref[...] = (_silu(gu_ref[0].astype(F32)) * gu_ref[1].astype(F32)).astype(a_ref.dtype)

    return pl.pallas_call(
        body, name=name, grid=(t // tm, nj),
        in_specs=[pl.BlockSpec((tm, k), lambda i, j: (i, 0)), pl.BlockSpec((k, tn), lambda i, j: (0, j)),
                  pl.BlockSpec((k, tn), lambda i, j: (0, j + nj))],
        out_specs=[pl.BlockSpec((2, tm, tn), lambda i, j: (0, i, j)), pl.BlockSpec((tm, tn), lambda i, j: (i, j))],
        out_shape=[jax.ShapeDtypeStruct((2, t, f), ACT_DTYPE), jax.ShapeDtypeStruct((t, f), MXU_DTYPE)],
        compiler_params=_params(("parallel", "parallel")))(h, w, w)


def _ffn_out_dx_swiglu(dfo, w, gu, name):
    t, k = dfo.shape
    f = w.shape[0]
    tm, tn = _tile(t, _ROW_TILES), _tile(f, _MM_TILES)

    def body(d_ref, w_ref, gu_ref, o_ref):
        da = _mm(d_ref[...], w_ref[...], _NT)
        gate = gu_ref[0].astype(F32)
        sg = _sigmoid(gate)
        o_ref[0] = (da * gu_ref[1].astype(F32) * (sg * (1.0 + gate * (1.0 - sg)))).astype(o_ref.dtype)
        o_ref[1] = (da * (gate * sg)).astype(o_ref.dtype)

    planes = pl.BlockSpec((2, tm, tn), lambda i, j: (0, i, j))
    return pl.pallas_call(
        body, name=name, grid=(t // tm, f // tn),
        in_specs=[pl.BlockSpec((tm, k), lambda i, j: (i, 0)), pl.BlockSpec((tn, k), lambda i, j: (j, 0)), planes],
        out_specs=planes, out_shape=jax.ShapeDtypeStruct((2, t, f), MXU_DTYPE),
        compiler_params=_params(("parallel", "parallel")))(dfo, w, gu)


class _ProjLayout:
    def __init__(self, d):
        wc = 3 * HEADS * HEAD_DIM
        self.d, self.wc = d, wc
        self.qkv, self.z, self.uv, self.gates, self.ba = 0, wc, wc + d, wc + 3 * d, wc + 5 * d
        self.width = self.ba + LANES
        assert self.z % d == 0 and self.uv % (2 * d) == 0 and self.gates % (2 * d) == 0 and self.ba % LANES == 0

    def pieces(self, shard):
        d, wc, out, lo = self.d, self.wc, [], 0
        for length, dst in ((2 * d, self.uv), (wc, self.qkv), (d, self.z), (2 * HEADS, self.ba), (2 * d, self.gates)):
            pos = lo
            while pos < lo + length:
                j = pos // shard
                n = min(lo + length, (j + 1) * shard) - pos
                out.append((j, pos - j * shard, n, dst + pos - lo))
                pos += n
            lo += length
        return out


def _branches_merge(ya, w_a, yb, w_b, proj, gcol, name):
    t, k = ya.shape
    d = w_a.shape[1]
    tm, tn = _tile(t, _ROW_TILES), _tile(d, _MM_TILES)
    nj = d // tn

    def body(ya_ref, wa_ref, yb_ref, wb_ref, ga_ref, gb_ref, pa_ref, pb_ref, m_ref):
        pa, pb = _mm(ya_ref[...], wa_ref[...]), _mm(yb_ref[...], wb_ref[...])
        pa_ref[...], pb_ref[...] = pa, pb
        sa, sb = _sigmoid(ga_ref[...].astype(F32)), _sigmoid(gb_ref[...].astype(F32))
        m_ref[...] = (sa * pa + sb * pb).astype(m_ref.dtype)

    left = pl.BlockSpec((tm, k), lambda i, j: (i, 0))
    right = pl.BlockSpec((k, tn), lambda i, j: (0, j))
    gate = lambda first: pl.BlockSpec((tm, tn), lambda i, j: (i, first // tn + j))
    out = pl.BlockSpec((tm, tn), lambda i, j: (i, j))
    return pl.pallas_call(
        body, name=name, grid=(t // tm, nj), in_specs=[left, right, left, right, gate(gcol), gate(gcol + d)],
        out_specs=[out, out, out],
        out_shape=[jax.ShapeDtypeStruct((t, d), F32), jax.ShapeDtypeStruct((t, d), F32),
                   jax.ShapeDtypeStruct((t, d), MXU_DTYPE)],
        compiler_params=_params(("parallel", "parallel")))(ya, w_a, yb, w_b, proj, proj)


def _out_dx_merge_bwd(dmo, w_o, pa, pb, proj, gcol, dproj, name):
    t, d = pa.shape
    tt = _tile(t, _ROW_TILES)

    def body(dmo_ref, w_ref, pa_ref, pb_ref, ga_ref, gb_ref, _, dpa_ref, dpb_ref, dg_ref):
        dm = _mm(dmo_ref[...], w_ref[...], _NT)
        sa, sb = _sigmoid(ga_ref[...].astype(F32)), _sigmoid(gb_ref[...].astype(F32))
        dpa_ref[...] = (dm * sa).astype(dpa_ref.dtype)
        dpb_ref[...] = (dm * sb).astype(dpb_ref.dtype)
        dg_ref[:, :d] = (dm * pa_ref[...] * sa * (1.0 - sa)).astype(dg_ref.dtype)
        dg_ref[:, d:] = (dm * pb_ref[...] * sb * (1.0 - sb)).astype(dg_ref.dtype)

    row = pl.BlockSpec((tt, d), lambda i: (i, 0))
    gate = lambda k: pl.BlockSpec((tt, d), lambda i: (i, gcol // d + k))
    wide = pl.BlockSpec((tt, 2 * d), lambda i: (i, gcol // (2 * d)))
    return pl.pallas_call(
        body, name=name, grid=(t // tt,),
        in_specs=[row, pl.BlockSpec((d, d), lambda i: (0, 0)), row, row, gate(0), gate(1), _ANY],
        out_specs=[row, row, wide],
        out_shape=[jax.ShapeDtypeStruct((t, d), MXU_DTYPE), jax.ShapeDtypeStruct((t, d), MXU_DTYPE),
                   jax.ShapeDtypeStruct(dproj.shape, dproj.dtype)],
        input_output_aliases={6: 2}, compiler_params=_params(("parallel",)))(dmo, w_o, pa, pb, proj, proj, dproj)


def _tri_masks(n):
    ri = lax.broadcasted_iota(jnp.int32, (n, n), 0)
    ci = lax.broadcasted_iota(jnp.int32, (n, n), 1)
    return ri >= ci, ri > ci, ri == ci


def _mixer_a_fwd(proj, ucol, w_s, b_col, g_v, name):
    t, w = proj.shape[0], g_v.shape[1]
    c = A_CHUNK

    def body(u_ref, v_ref, w_ref, b_ref, gv_ref, y_ref):
        tril, _, _ = _tri_masks(c)
        ug, vg = _gelu(u_ref[...].astype(F32)), _gelu(v_ref[...].astype(F32))
        for g in range(GROUPS):
            sl = slice(g * c, (g + 1) * c)
            vt = vg[:, sl]
            r = lax.rsqrt(jnp.mean(vt * vt, axis=-1, keepdims=True) + EPS)
            vn = vt * r * gv_ref[:, sl]
            s = _mm(jnp.where(tril, w_ref[g], 0.0), vn) + b_ref[g]
            y_ref[:, sl] = (ug[:, sl] * s).astype(y_ref.dtype)

    return pl.pallas_call(
        body, name=name, grid=(t // c,),
        in_specs=[pl.BlockSpec((c, w), lambda i: (i, ucol // w)), pl.BlockSpec((c, w), lambda i: (i, ucol // w + 1)),
                  pl.BlockSpec((GROUPS, c, c), lambda i: (0, 0, 0)), pl.BlockSpec((GROUPS, c, 1), lambda i: (0, 0, 0)),
                  pl.BlockSpec((1, w), lambda i: (0, 0))],
        out_specs=pl.BlockSpec((c, w), lambda i: (i, 0)), out_shape=jax.ShapeDtypeStruct((t, w), MXU_DTYPE),
        compiler_params=_params(("parallel",)))(proj, proj, w_s, b_col, g_v)


def _mixer_a_bwd(proj, ucol, dy, w_s, w_st, b_col, g_v, dproj, name):
    t, w = proj.shape[0], g_v.shape[1]
    w2 = 2 * w
    c = A_CHUNK

    def body(u_ref, v_ref, dy_ref, w_ref, wt_ref, b_ref, gv_ref, _, duv_ref, dw_ref, db_ref, dgv_ref):
        @pl.when(pl.program_id(0) == 0)
        def _():
            dw_ref[...] = jnp.zeros_like(dw_ref)
            db_ref[...] = jnp.zeros_like(db_ref)
            dgv_ref[...] = jnp.zeros_like(dgv_ref)

        tril, _, _ = _tri_masks(c)
        triu = lax.broadcasted_iota(jnp.int32, (c, c), 0) <= lax.broadcasted_iota(jnp.int32, (c, c), 1)
        (ug, dug), (vg, dvg) = _gelu_and_slope(u_ref[...].astype(F32)), _gelu_and_slope(v_ref[...].astype(F32))
        for g in range(GROUPS):
            sl = slice(g * c, (g + 1) * c)
            vt = vg[:, sl]
            r = lax.rsqrt(jnp.mean(vt * vt, axis=-1, keepdims=True) + EPS)
            vh = vt * r
            gv = gv_ref[:, sl]
            vn = vh * gv
            s = _mm(jnp.where(tril, w_ref[g], 0.0), vn) + b_ref[g]
            dy = dy_ref[:, sl]
            ds = dy * ug[:, sl]
            dw_ref[g] += jnp.where(tril, _mm(ds, vn, _NT), 0.0)
            db_ref[g] += jnp.sum(ds, axis=1, keepdims=True)
            dvn = _mm(jnp.where(triu, wt_ref[g], 0.0), ds)
            dgv_ref[:, sl] += jnp.sum(dvn * vh, axis=0, keepdims=True)
            dvh = dvn * gv
            dvt = r * (dvh - vh * jnp.mean(dvh * vh, axis=-1, keepdims=True))
            duv_ref[:, sl] = (dy * s * dug[:, sl]).astype(duv_ref.dtype)
            duv_ref[:, w + g * c:w + (g + 1) * c] = (dvt * dvg[:, sl]).astype(duv_ref.dtype)

    full3 = lambda shape: pl.BlockSpec(shape, lambda i: (0, 0, 0))
    return pl.pallas_call(
        body, name=name, grid=(t // c,),
        in_specs=[pl.BlockSpec((c, w), lambda i: (i, ucol // w)), pl.BlockSpec((c, w), lambda i: (i, ucol // w + 1)),
                  pl.BlockSpec((c, w), lambda i: (i, 0)), full3((GROUPS, c, c)), full3((GROUPS, c, c)),
                  full3((GROUPS, c, 1)), pl.BlockSpec((1, w), lambda i: (0, 0)), _ANY],
        out_specs=[pl.BlockSpec((c, w2), lambda i: (i, ucol // w2)), full3((GROUPS, c, c)), full3((GROUPS, c, 1)),
                   pl.BlockSpec((1, w), lambda i: (0, 0))],
        out_shape=[jax.ShapeDtypeStruct(dproj.shape, dproj.dtype), jax.ShapeDtypeStruct((GROUPS, c, c), F32),
                   jax.ShapeDtypeStruct((GROUPS, c, 1), F32), jax.ShapeDtypeStruct((1, w), F32)],
        input_output_aliases={7: 0},
        compiler_params=_params(("arbitrary",)))(proj, proj, dy, w_s, w_st, b_col, g_v, dproj)


_Q_SCALE = HEAD_DIM ** -0.5


CONV_HALO = 16


def _conv_taps(x_ref, p_ref, w_ref):
    prev = jnp.where(pl.program_id(0) > 0, p_ref[...].astype(F32), 0.0)
    ext = jnp.concatenate([prev, x_ref[...].astype(F32)], axis=0)
    shifted = [ext[CONV_HALO:]] + [pltpu.roll(ext, s, 0)[CONV_HALO:] for s in range(1, CONV_K)]
    acc = shifted[0] * w_ref[pl.ds(CONV_K - 1, 1), :]
    for s in range(1, CONV_K):
        acc = acc + shifted[s] * w_ref[pl.ds(CONV_K - 1 - s, 1), :]
    return acc, shifted


def _conv_fwd(qkv, w, name):
    t, cw = qkv.shape[0], w.shape[1]
    tt = _tile(t, (256, 128))
    hb = tt // CONV_HALO

    def body(x_ref, p_ref, w_ref, o_ref):
        acc, _ = _conv_taps(x_ref, p_ref, w_ref)
        y = _silu(acc)
        for which in range(3):
            for h in range(HEADS):
                lo = (which * HEADS + h) * HEAD_DIM
                seg = y[:, lo:lo + HEAD_DIM]
                if which < 2:
                    seg = seg * lax.rsqrt(jnp.sum(seg * seg, axis=-1, keepdims=True) + EPS)
                if which == 0:
                    seg = seg * _Q_SCALE
                o_ref[which, h] = seg

    return pl.pallas_call(
        body, name=name, grid=(t // tt,),
        in_specs=[pl.BlockSpec((tt, cw), lambda i: (i, 0)),
                  pl.BlockSpec((CONV_HALO, cw), lambda i: (jnp.maximum(i * hb - 1, 0), 0)),
                  pl.BlockSpec((CONV_K, cw), lambda i: (0, 0))],
        out_specs=pl.BlockSpec((3, HEADS, tt, HEAD_DIM), lambda i: (0, 0, i, 0)),
        out_shape=jax.ShapeDtypeStruct((3, HEADS, t, HEAD_DIM), F32),
        compiler_params=_params(("parallel",)))(qkv, qkv, w)


def _conv_bwd_pre(qkv, dq, dk, dv, w, name):
    t, cw = qkv.shape[0], w.shape[1]
    tt = _tile(t, (256, 128))
    hb = tt // CONV_HALO

    def body(x_ref, p_ref, dq_ref, dk_ref, dv_ref, w_ref, da_ref, dw_ref):
        @pl.when(pl.program_id(0) == 0)
        def _():
            dw_ref[...] = jnp.zeros_like(dw_ref)

        acc, shifted = _conv_taps(x_ref, p_ref, w_ref)
        sg = _sigmoid(acc)
        y = acc * sg
        dsilu = sg * (1.0 + acc * (1.0 - sg))
        d_refs = (dq_ref, dk_ref, dv_ref)
        for which in range(3):
            for h in range(HEADS):
                lo = (which * HEADS + h) * HEAD_DIM
                sl = slice(lo, lo + HEAD_DIM)
                dn = d_refs[which][h]
                if which < 2:
                    seg = y[:, sl]
                    rho = lax.rsqrt(jnp.sum(seg * seg, axis=-1, keepdims=True) + EPS)
                    nrm = seg * rho
                    if which == 0:
                        dn = dn * _Q_SCALE
                    dn = rho * (dn - nrm * jnp.sum(dn * nrm, axis=-1, keepdims=True))
                dacc = dn * dsilu[:, sl]
                da_ref[:, sl] = dacc
                for s in range(CONV_K):
                    dw_ref[pl.ds(CONV_K - 1 - s, 1), sl] += jnp.sum(dacc * shifted[s][:, sl], axis=0, keepdims=True)

    head = pl.BlockSpec((HEADS, tt, HEAD_DIM), lambda i: (0, i, 0))
    return pl.pallas_call(
        body, name=name, grid=(t // tt,),
        in_specs=[pl.BlockSpec((tt, cw), lambda i: (i, 0)),
                  pl.BlockSpec((CONV_HALO, cw), lambda i: (jnp.maximum(i * hb - 1, 0), 0)),
                  head, head, head, pl.BlockSpec((CONV_K, cw), lambda i: (0, 0))],
        out_specs=[pl.BlockSpec((tt, cw), lambda i: (i, 0)), pl.BlockSpec((CONV_K, cw), lambda i: (0, 0))],
        out_shape=[jax.ShapeDtypeStruct((t, cw), F32), jax.ShapeDtypeStruct((CONV_K, cw), F32)],
        compiler_params=_params(("arbitrary",)))(qkv, qkv, dq, dk, dv, w)


def _conv_bwd_in(dacc, w, dproj, name):
    t, cw = dacc.shape
    tt = _tile(t, (256, 128))
    hb = tt // SUBLANES
    nt = t // tt
    rows = tt + SUBLANES

    def body(d_ref, n_ref, w_ref, _, o_ref):
        cur = d_ref[...]
        nxt = jnp.where(pl.program_id(0) < nt - 1, n_ref[...], 0.0)
        ext = jnp.concatenate([cur, nxt], axis=0)
        acc = cur * w_ref[pl.ds(CONV_K - 1, 1), :]
        for s in range(1, CONV_K):
            acc = acc + pltpu.roll(ext, rows - s, 0)[:tt] * w_ref[pl.ds(CONV_K - 1 - s, 1), :]
        o_ref[...] = acc.astype(o_ref.dtype)

    return pl.pallas_call(
        body, name=name, grid=(nt,),
        in_specs=[pl.BlockSpec((tt, cw), lambda i: (i, 0)),
                  pl.BlockSpec((SUBLANES, cw), lambda i: (jnp.minimum((i + 1) * hb, t // SUBLANES - 1), 0)),
                  pl.BlockSpec((CONV_K, cw), lambda i: (0, 0)), _ANY],
        out_specs=pl.BlockSpec((tt, cw), lambda i: (i, 0)), out_shape=jax.ShapeDtypeStruct(dproj.shape, dproj.dtype),
        input_output_aliases={3: 0}, compiler_params=_params(("parallel",)))(dacc, dacc, w, dproj)


_INV_BASE_SHIFT = 3


def _inv_unit_lower(a, eye):
    c = GDN_CHUNK
    ri = lax.broadcasted_iota(jnp.int32, (c, c), 0)
    ci = lax.broadcasted_iota(jnp.int32, (c, c), 1)
    same = lambda sh: (ri >> sh) == (ci >> sh)
    x = jnp.where(same(_INV_BASE_SHIFT), -a, 0.0)
    p = jnp.where(eye, 1.0, 0.0) + x
    xs = _split(x)
    x2 = _mm3(xs, xs)
    x2s, ps = _split(x2), _split(p)
    r = _mm3(x2s, tuple(jnp.concatenate([u, v], axis=-1) for u, v in zip(x2s, ps)))
    x4, p = r[..., :c], p + r[..., c:]
    p = p + _mm3(_split(x4), _split(p))
    for sh in range(_INV_BASE_SHIFT, c.bit_length() - 1):
        off = jnp.where(same(sh + 1) & jnp.logical_not(same(sh)), a, 0.0)
        ps = _split(p)
        p = p - _mm3(ps, _split(_mm3(_split(off), ps)))
    return p


def _split(a):
    hi = a.astype(BF16)
    return hi, (a - hi.astype(F32)).astype(BF16)


def _dot_heads(u, v, dims):
    if u.ndim == 3:
        return jnp.stack([_dot_heads(u[j], v[j], dims) for j in range(u.shape[0])])
    return lax.dot_general(u, v, dims, preferred_element_type=F32)


def _mm3(a, b):
    return _dot_heads(a[0], b[0], _NN) + (_dot_heads(a[0], b[1], _NN) + _dot_heads(a[1], b[0], _NN))


def _hmm(a, b, dims=_NN):
    return _dot_heads(a.astype(MXU_DTYPE), b.astype(MXU_DTYPE), dims)


def _rowsum(x):
    return jnp.sum(x, axis=-1, keepdims=True)


def _colsum(x):
    return jnp.sum(x, axis=-2, keepdims=True)


class _Pre:
    pass


def _gdn_pre(q, k, v, araw, braw, alog, dtb, t_mat=None):
    c = GDN_CHUNK
    p = _Pre()
    p.tril, p.strict, p.eye = _tri_masks(c)
    p.to_col = lambda row: _rowsum(jnp.where(p.eye, row, 0.0))
    p.to_row = lambda col: _colsum(jnp.where(p.eye, col, 0.0))
    p.a_neg = -jnp.exp(alog + jnp.zeros((1, c), F32))
    p.xg = araw + dtb
    p.g_row = p.a_neg * _softplus(p.xg)
    p.beta_row = _sigmoid(braw)
    p.beta = p.to_col(p.beta_row)
    gam = _rowsum(jnp.where(p.tril, p.g_row, 0.0))
    gam_last = _rowsum(p.g_row)
    p.dm = jnp.where(p.tril, jnp.exp(jnp.where(p.tril, gam - p.to_row(gam), 0.0)), 0.0)
    p.e, p.ek, p.el = jnp.exp(gam), jnp.exp(gam_last - gam), jnp.exp(gam_last)
    p.kb = k * p.beta
    p.kk = _hmm(p.kb, k, _NT)
    p.t = _inv_unit_lower(jnp.where(p.strict, p.kk * p.dm, 0.0), p.eye) if t_mat is None else t_mat
    p.vb, p.kbe = v * p.beta, p.kb * p.e
    uw = _hmm(p.t, jnp.concatenate([p.vb, p.kbe], axis=-1))
    p.u, p.w = uw[..., :v.shape[-1]], uw[..., v.shape[-1]:]
    p.qk0 = _hmm(q, k, _NT)
    p.qk = p.qk0 * p.dm
    p.qd, p.kd = q * p.e, k * p.ek
    return p


GDN_HEADS_PER_STEP = 8


def _head_scalars(ref, hb):
    h0 = pl.program_id(0) * hb
    return jnp.stack([jnp.full((1, 1), ref[h0 + j], F32) for j in range(hb)])


def _gdn_specs(n, reverse):
    c, dk, hb = GDN_CHUNK, HEAD_DIM, GDN_HEADS_PER_STEP
    ix = (lambda i: n - 1 - i) if reverse else (lambda i: i)
    smem = pl.BlockSpec(memory_space=pltpu.SMEM)
    qkv = [pl.BlockSpec((None, hb, c, dk), functools.partial(lambda w, h, i: (w, h, ix(i), 0), w)) for w in range(3)]
    row = pl.BlockSpec((hb, None, 1, c), lambda h, i: (h, ix(i), 0, 0))
    tok = pl.BlockSpec((hb, c, dk), lambda h, i: (h, ix(i), 0))
    state = pl.BlockSpec((hb, None, dk, dk), lambda h, i: (h, ix(i), 0, 0))
    return smem, qkv, row, tok, state


def _gdn_fwd(qkv_h, araw, braw, alog, dtb, name, side=_NoSide):
    _, hh, t, dk = qkv_h.shape
    n, hb = t // GDN_CHUNK, GDN_HEADS_PER_STEP
    smem, qkv, row, tok, state = _gdn_specs(n, False)
    grid = (hh // hb, n)

    def body(*refs):
        main, side_start, side_finish = _side_hooks(side, refs, 7, 3, 1, grid)
        alog_ref, dt_ref, q_ref, k_ref, v_ref, a_ref, b_ref, o_ref, so_ref, to_ref, s_ref = main
        side_start()

        @pl.when(pl.program_id(1) == 0)
        def _():
            s_ref[...] = jnp.zeros_like(s_ref)

        p = _gdn_pre(q_ref[...], k_ref[...], v_ref[...], a_ref[...], b_ref[...],
                     _head_scalars(alog_ref, hb), _head_scalars(dt_ref, hb))
        s = s_ref[...]
        vn = p.u - _hmm(p.w, s)
        o_ref[...] = _hmm(p.qd, s) + _hmm(p.qk, vn)
        so_ref[...] = s
        to_ref[...] = p.t
        s_ref[...] = s * p.el + _hmm(p.kd, vn, _TN)
        side_finish()

    mats = jax.ShapeDtypeStruct((hh, n, dk, dk), F32)
    return _carrier_call(
        body, name, grid, [smem, smem] + qkv + [row, row], [tok, state, state],
        [jax.ShapeDtypeStruct((hh, t, dk), F32), mats, mats],
        [pltpu.VMEM((hb, dk, dk), F32)], side, (alog, dtb, qkv_h, qkv_h, qkv_h, araw, braw))


def _gdn_bwd(qkv_h, araw, braw, alog, dtb, states, t_mats, do, name, side=_NoSide):
    _, hh, t, dk = qkv_h.shape
    c, hb = GDN_CHUNK, GDN_HEADS_PER_STEP
    n = t // c
    smem, qkv, row, tok, state = _gdn_specs(n, True)
    acc = pl.BlockSpec((hb, 1, LANES), lambda h, i: (h, 0, 0))
    grid = (hh // hb, n)

    def body(*refs):
        main, side_start, side_finish = _side_hooks(side, refs, 10, 7, 1, grid)
        (alog_ref, dt_ref, q_ref, k_ref, v_ref, a_ref, b_ref, s_ref, t_ref, do_ref,
         dq_ref, dk_ref, dv_ref, da_ref, db_ref, dal_ref, ddt_ref, ds_ref) = main
        side_start()

        @pl.when(pl.program_id(1) == 0)
        def _():
            ds_ref[...] = jnp.zeros_like(ds_ref)
            dal_ref[...] = jnp.zeros_like(dal_ref)
            ddt_ref[...] = jnp.zeros_like(ddt_ref)

        q, k, v = q_ref[...], k_ref[...], v_ref[...]
        p = _gdn_pre(q, k, v, a_ref[...], b_ref[...], _head_scalars(alog_ref, hb), _head_scalars(dt_ref, hb),
                     t_ref[...])
        s, do, dsp = s_ref[...], do_ref[...], ds_ref[...]
        vn = p.u - _hmm(p.w, s)
        dqd = _hmm(do, s, _NT)
        dqk = _hmm(do, vn, _NT)
        dvn = _hmm(p.qk, do, _TN) + _hmm(p.kd, dsp)
        dkd = _hmm(vn, dsp, _NT)
        d_el = _colsum(_rowsum(s * dsp))
        ds_ref[...] = dsp * p.el + _hmm(p.qd, do, _TN) - _hmm(p.w, dvn, _TN)
        dw = -_hmm(dvn, s, _NT)
        d_t = _hmm(dvn, p.vb, _NT) + _hmm(dw, p.kbe, _NT)
        dvb, dkbe = _hmm(p.t, dvn, _TN), _hmm(p.t, dw, _TN)
        d_a = jnp.where(p.strict, -_hmm(p.t, _hmm(d_t, p.t, _NT), _TN), 0.0)
        dkk = d_a * p.dm
        dqk0 = dqk * p.dm
        ddm = d_a * p.kk + dqk * p.qk0
        dkb = _hmm(dkk, k) + dkbe * p.e
        dq_ref[...] = _hmm(dqk0, k) + dqd * p.e
        dk_ref[...] = _hmm(dkk, p.kb, _TN) + _hmm(dqk0, q, _TN) + dkd * p.ek + dkb * p.beta
        dv_ref[...] = dvb * p.beta
        dbeta = _rowsum(dkb * k) + _rowsum(dvb * v)
        d_e = _rowsum(dqd * q) + _rowsum(dkbe * p.kb)
        d_ek = _rowsum(dkd * k)
        m = ddm * p.dm
        dgam = d_e * p.e - d_ek * p.ek + _rowsum(m) - p.to_col(_colsum(m))
        dgam_last = _colsum(d_ek * p.ek) + d_el * p.el
        dg_row = _colsum(jnp.where(p.tril, dgam, 0.0)) + dgam_last
        da_row = dg_row * p.a_neg * _sigmoid(p.xg)
        da_ref[...] = da_row
        db_ref[...] = p.to_row(dbeta) * p.beta_row * (1.0 - p.beta_row)
        dal_ref[...] += _rowsum(dg_row * p.g_row)
        ddt_ref[...] += _rowsum(da_row)
        side_finish()

    tok_shape = jax.ShapeDtypeStruct((hh, t, dk), F32)
    row_shape = jax.ShapeDtypeStruct((hh, n, 1, c), F32)
    acc_shape = jax.ShapeDtypeStruct((hh, 1, LANES), F32)
    return _carrier_call(
        body, name, grid, [smem, smem] + qkv + [row, row, state, state, tok], [tok, tok, tok, row, row, acc, acc],
        [tok_shape, tok_shape, tok_shape, row_shape, row_shape, acc_shape, acc_shape],
        [pltpu.VMEM((hb, dk, dk), F32)], side, (alog, dtb, qkv_h, qkv_h, qkv_h, araw, braw, states, t_mats, do))


def _gdn_post_fwd(o, proj, zcol, g_o, name):
    hh, t, dv = o.shape
    tt = _tile(t, _ROW_TILES)
    zblk = zcol // (hh * dv)

    def body(o_ref, z_ref, g_ref, y_ref):
        for h in range(hh):
            sl = slice(h * dv, (h + 1) * dv)
            ov = o_ref[h]
            r = lax.rsqrt(jnp.mean(ov * ov, axis=-1, keepdims=True) + EPS)
            y_ref[:, sl] = (ov * r * g_ref[...] * _silu(z_ref[:, sl].astype(F32))).astype(y_ref.dtype)

    return pl.pallas_call(
        body, name=name, grid=(t // tt,),
        in_specs=[pl.BlockSpec((hh, tt, dv), lambda i: (0, i, 0)), pl.BlockSpec((tt, hh * dv), lambda i: (i, zblk)),
                  pl.BlockSpec((1, dv), lambda i: (0, 0))],
        out_specs=pl.BlockSpec((tt, hh * dv), lambda i: (i, 0)),
        out_shape=jax.ShapeDtypeStruct((t, hh * dv), MXU_DTYPE), compiler_params=_params(("parallel",)))(o, proj, g_o)


def _gdn_post_bwd(o, proj, zcol, dy, g_o, dproj, name):
    hh, t, dv = o.shape
    tt = _tile(t, _ROW_TILES)
    zblk = zcol // (hh * dv)

    def body(o_ref, z_ref, dy_ref, g_ref, _, do_ref, dz_ref, dg_ref):
        @pl.when(pl.program_id(0) == 0)
        def _():
            dg_ref[...] = jnp.zeros_like(dg_ref)

        gv = g_ref[...]
        for h in range(hh):
            sl = slice(h * dv, (h + 1) * dv)
            ov, zz, dy = o_ref[h], z_ref[:, sl].astype(F32), dy_ref[:, sl]
            r = lax.rsqrt(jnp.mean(ov * ov, axis=-1, keepdims=True) + EPS)
            oh = ov * r
            sg = _sigmoid(zz)
            dz_ref[:, sl] = (dy * oh * gv * (sg * (1.0 + zz * (1.0 - sg)))).astype(dz_ref.dtype)
            don = dy * (zz * sg)
            dg_ref[...] += _colsum(don * oh)
            doh = don * gv
            do_ref[h] = r * (doh - oh * jnp.mean(doh * oh, axis=-1, keepdims=True))

    return pl.pallas_call(
        body, name=name, grid=(t // tt,),
        in_specs=[pl.BlockSpec((hh, tt, dv), lambda i: (0, i, 0)), pl.BlockSpec((tt, hh * dv), lambda i: (i, zblk)),
                  pl.BlockSpec((tt, hh * dv), lambda i: (i, 0)), pl.BlockSpec((1, dv), lambda i: (0, 0)), _ANY],
        out_specs=[pl.BlockSpec((hh, tt, dv), lambda i: (0, i, 0)), pl.BlockSpec((tt, hh * dv), lambda i: (i, zblk)),
                   pl.BlockSpec((1, dv), lambda i: (0, 0))],
        out_shape=[jax.ShapeDtypeStruct((hh, t, dv), F32), jax.ShapeDtypeStruct(dproj.shape, dproj.dtype),
                   jax.ShapeDtypeStruct((1, dv), F32)],
        input_output_aliases={4: 1}, compiler_params=_params(("arbitrary",)))(o, proj, dy, g_o, dproj)


def _cols_as_rows(x, col, name):
    t = x.shape[0]
    tt = _tile(t, _ROW_TILES)

    def body(x_ref, o_ref):
        o_ref[...] = x_ref[...].T

    return pl.pallas_call(
        body, name=name, grid=(t // tt,), in_specs=[pl.BlockSpec((tt, LANES), lambda i: (i, col // LANES))],
        out_specs=pl.BlockSpec((LANES, tt), lambda i: (0, i)), out_shape=jax.ShapeDtypeStruct((LANES, t), x.dtype),
        compiler_params=_params(("parallel",)))(x)


def _rows_into_cols(dst, rows, col, name):
    t = dst.shape[0]
    tt = _tile(t, _ROW_TILES)

    def body(r_ref, _, o_ref):
        o_ref[...] = r_ref[...].T.astype(o_ref.dtype)

    return pl.pallas_call(
        body, name=name, grid=(t // tt,), in_specs=[pl.BlockSpec((LANES, tt), lambda i: (0, i)), _ANY],
        out_specs=pl.BlockSpec((tt, LANES), lambda i: (i, col // LANES)),
        out_shape=jax.ShapeDtypeStruct(dst.shape, dst.dtype), input_output_aliases={1: 0},
        compiler_params=_params(("parallel",)))(rows, dst)


def _adamw(g, w, m, v):
    m = ADAM_B1 * m + (1.0 - ADAM_B1) * g
    v = ADAM_B2 * v + (1.0 - ADAM_B2) * (g * g)
    m_hat = m / (1.0 - ADAM_B1 ** ADAM_STEP)
    v_hat = v / (1.0 - ADAM_B2 ** ADAM_STEP)
    return -ADAM_LR * (m_hat / (jnp.sqrt(v_hat) + ADAM_EPS) + ADAM_WD * w), m, v


def _ada_fwd(c_all, ada_w, name):
    nl, d, cols = ada_w.shape
    b = c_all.shape[0]

    def body(c_ref, w_ref, o_ref):
        o_ref[...] = _mm_hi(_silu(c_ref[...]), w_ref[...])

    return pl.pallas_call(
        body, name=name, grid=(nl,),
        in_specs=[pl.BlockSpec((b, d), lambda i: (0, 0)), pl.BlockSpec((None, d, cols), lambda i: (i, 0, 0))],
        out_specs=pl.BlockSpec((None, b, cols), lambda i: (i, 0, 0)),
        out_shape=jax.ShapeDtypeStruct((nl, b, cols), F32), compiler_params=_params(("parallel",)))(c_all, ada_w)


def _ada_bwd(c_col, dm, w, m, v, name):
    nl, d, cols = w.shape
    b = c_col.shape[0]
    tr = _tile(d, (256, 128))

    def body(c_ref, dm_ref, w_ref, m_ref, v_ref, g_ref, dl_ref, mo_ref, vo_ref):
        g = _silu(c_ref[0]) * dm_ref[pl.ds(0, 1), :]
        for j in range(1, b):
            g = g + _silu(c_ref[j]) * dm_ref[pl.ds(j, 1), :]
        g_ref[...] = g
        dl_ref[...], mo_ref[...], vo_ref[...] = _adamw(g, w_ref[...], m_ref[...], v_ref[...])

    blk = pl.BlockSpec((None, tr, cols), lambda l, i: (l, i, 0))
    shape = jax.ShapeDtypeStruct((nl, d, cols), F32)
    return pl.pallas_call(
        body, name=name, grid=(nl, d // tr),
        in_specs=[pl.BlockSpec((b, tr, 1), lambda l, i: (0, i, 0)), pl.BlockSpec((None, b, cols), lambda l, i: (l, 0, 0)),
                  blk, blk, blk],
        out_specs=[blk, blk, blk, blk], out_shape=[shape] * 4,
        compiler_params=_params(("parallel", "parallel")))(c_col, dm, w, m, v)


_GRAD_ROW_TILES = (256, 128, 176, 88)


def _sum_adam(parts, w, m, v, name):
    nl, npart, r, cdim = parts.shape
    tr = _tile(r, _GRAD_ROW_TILES)

    def body(p_ref, w_ref, m_ref, v_ref, g_ref, dl_ref, mo_ref, vo_ref):
        g = p_ref[0].astype(F32)
        for j in range(1, npart):
            g = g + p_ref[j].astype(F32)
        g_ref[...] = g
        dl_ref[...], mo_ref[...], vo_ref[...] = _adamw(g, w_ref[...], m_ref[...], v_ref[...])

    blk = pl.BlockSpec((None, tr, cdim), lambda l, i: (l, i, 0))
    shape = jax.ShapeDtypeStruct((nl, r, cdim), F32)
    return pl.pallas_call(
        body, name=name, grid=(nl, r // tr),
        in_specs=[pl.BlockSpec((None, npart, tr, cdim), lambda l, i: (l, 0, i, 0)), blk, blk, blk],
        out_specs=[blk, blk, blk, blk], out_shape=[shape] * 4,
        compiler_params=_params(("parallel", "parallel")))(parts, w, m, v)


def _cols_from_blocks(g, plan, width, name):
    _, r, cdim = g.shape
    tr = _tile(r, (256, 128))
    covered = sorted((dst, dst + n) for _, _, n, dst in plan)
    holes, pos = [], 0
    for a, b in covered:
        if a > pos:
            holes.append((pos, a))
        pos = max(pos, b)
    if pos < width:
        holes.append((pos, width))

    def body(g_ref, o_ref):
        for a, b in holes:
            o_ref[:, a:b] = jnp.zeros((tr, b - a), o_ref.dtype)
        for j, src, n, dst in plan:
            o_ref[:, dst:dst + n] = g_ref[j, :, src:src + n]

    return pl.pallas_call(
        body, name=name, grid=(r // tr,), in_specs=[pl.BlockSpec((N_DEV, tr, cdim), lambda i: (0, i, 0))],
        out_specs=pl.BlockSpec((tr, width), lambda i: (i, 0)), out_shape=jax.ShapeDtypeStruct((r, width), g.dtype),
        compiler_params=_params(("parallel",)))(g)


def _blocks_from_cols(w, plan, cdim, name):
    r, width = w.shape
    tr = _tile(r, (256, 128))

    def body(w_ref, o_ref):
        for j, src, n, dst in plan:
            o_ref[j, :, src:src + n] = w_ref[:, dst:dst + n]

    return pl.pallas_call(
        body, name=name, grid=(r // tr,), in_specs=[pl.BlockSpec((tr, width), lambda i: (i, 0))],
        out_specs=pl.BlockSpec((N_DEV, tr, cdim), lambda i: (0, i, 0)),
        out_shape=jax.ShapeDtypeStruct((N_DEV, r, cdim), w.dtype), compiler_params=_params(("parallel",)))(w)


def _pair_sum(x, tmp, core, name):
    _, r, cdim = x.shape
    tr = _tile(r, _GRAD_ROW_TILES)

    def body(core_ref, x_ref, t_ref, o_ref):
        o_ref[...] = (x_ref[...] + t_ref[...]).astype(o_ref.dtype)

    grid_spec = pltpu.PrefetchScalarGridSpec(
        num_scalar_prefetch=1, grid=(N_DEV // 2, r // tr),
        in_specs=[pl.BlockSpec((None, tr, cdim), lambda ch, i, core_ref: (2 * ch + core_ref[0], i, 0)),
                  pl.BlockSpec((None, tr, cdim), lambda ch, i, core_ref: (ch, i, 0))],
        out_specs=pl.BlockSpec((None, tr, cdim), lambda ch, i, core_ref: (ch, i, 0)))
    return pl.pallas_call(
        body, name=name, grid_spec=grid_spec, out_shape=jax.ShapeDtypeStruct((N_DEV // 2, r, cdim), WIRE_DTYPE),
        compiler_params=_params(("parallel", "parallel")))(core, x, tmp)


_ANY = pl.BlockSpec(memory_space=pl.ANY)
_CHIP_FLIPS = ((1, 0), (0, 1), (1, 1))


def _coords():
    return lax.axis_index("x"), lax.axis_index("y"), lax.axis_index("c")


def _flip(v, f):
    return 1 - v if f else v


def _a2a_direct(xs, name):
    n, ncp = len(xs), N_DEV - 1

    def body(*refs):
        ins, outs = refs[:n], refs[n:2 * n]
        send, recv, loc = refs[2 * n:]
        x, y, c = _coords()
        me = 4 * x + 2 * y + c
        local = [pltpu.make_async_copy(ins[i].at[me], outs[i].at[me], loc.at[i]) for i in range(n)]
        for cp in local:
            cp.start()
        remote = []
        for i in range(n):
            for k in range(1, N_DEV):
                px, py, pc = _flip(x, k & 4), _flip(y, k & 2), _flip(c, k & 1)
                cp = pltpu.make_async_remote_copy(
                    src_ref=ins[i].at[4 * px + 2 * py + pc], dst_ref=outs[i].at[me],
                    send_sem=send.at[i * ncp + k - 1], recv_sem=recv.at[i * ncp + k - 1],
                    device_id=(px, py, pc), device_id_type=MESH)
                cp.start()
                remote.append(cp)
        for cp in remote:
            cp.wait()
        for cp in local:
            cp.wait()

    return pl.pallas_call(
        body, name=name, in_specs=[_ANY] * n, out_specs=[_ANY] * n,
        out_shape=[jax.ShapeDtypeStruct(a.shape, a.dtype) for a in xs],
        scratch_shapes=[pltpu.SemaphoreType.DMA((n * ncp,)), pltpu.SemaphoreType.DMA((n * ncp,)),
                        pltpu.SemaphoreType.DMA((n,))])(*xs)


class _AllGatherSide:
    def __init__(self, blocks):
        self.operands = list(blocks)
        n = self.n = len(self.operands)
        self.n_in = self.n_out = n
        self.out_shape = [jax.ShapeDtypeStruct((N_DEV,) + a.shape, a.dtype) for a in self.operands]
        self.aliases = {}
        nici, nd2d = len(_CHIP_FLIPS), N_DEV // 2
        self.scratch = [pltpu.SemaphoreType.DMA((n * nici,)), pltpu.SemaphoreType.DMA((n * nici,)),
                        pltpu.SemaphoreType.DMA((n * nd2d,)), pltpu.SemaphoreType.DMA((n * nd2d,)),
                        pltpu.SemaphoreType.DMA((n,))]

    def _first(self, ins, outs, sems):
        send, recv, _, _, loc = sems
        x, y, c = _coords()
        me = 4 * x + 2 * y + c
        nici = len(_CHIP_FLIPS)
        local = [pltpu.make_async_copy(ins[i], outs[i].at[me], loc.at[i]) for i in range(self.n)]
        remote = [pltpu.make_async_remote_copy(
            src_ref=ins[i], dst_ref=outs[i].at[me], send_sem=send.at[i * nici + j], recv_sem=recv.at[i * nici + j],
            device_id=(_flip(x, fx), _flip(y, fy), c), device_id_type=MESH)
            for i in range(self.n) for j, (fx, fy) in enumerate(_CHIP_FLIPS)]
        return local + remote

    def _second(self, outs, sems):
        _, _, send, recv, _ = sems
        x, y, c = _coords()
        nd2d = N_DEV // 2
        return [pltpu.make_async_remote_copy(
            src_ref=outs[i].at[2 * ch + c], dst_ref=outs[i].at[2 * ch + c], send_sem=send.at[i * nd2d + ch],
            recv_sem=recv.at[i * nd2d + ch], device_id=(x, y, 1 - c), device_id_type=MESH)
            for i in range(self.n) for ch in range(nd2d)]

    def start(self, ins, outs, sems):
        for cp in self._first(ins, outs, sems):
            cp.start()

    def finish(self, ins, outs, sems):
        for cp in self._first(ins, outs, sems):
            cp.wait()
        second = self._second(outs, sems)
        for cp in second:
            cp.start()
        for cp in second:
            cp.wait()


class _ReduceScatterIciSide:
    def __init__(self, sums, accs, layer):
        self.operands = list(sums) + list(accs)
        n = self.n = len(sums)
        self.layer = layer
        self.n_in, self.n_out = 2 * n, n
        self.out_shape = [jax.ShapeDtypeStruct(a.shape, a.dtype) for a in accs]
        self.aliases = {n + i: i for i in range(n)}
        nici = len(_CHIP_FLIPS)
        self.scratch = [pltpu.SemaphoreType.DMA((n * nici,)), pltpu.SemaphoreType.DMA((n * nici,)),
                        pltpu.SemaphoreType.DMA((n,))]

    def _copies(self, ins, outs, sems):
        send, recv, loc = sems
        x, y, c = _coords()
        chip = 2 * x + y
        nici = len(_CHIP_FLIPS)
        local = [pltpu.make_async_copy(ins[i].at[chip], outs[i].at[self.layer, chip], loc.at[i])
                 for i in range(self.n)]
        remote = [pltpu.make_async_remote_copy(
            src_ref=ins[i].at[2 * _flip(x, fx) + _flip(y, fy)], dst_ref=outs[i].at[self.layer, chip],
            send_sem=send.at[i * nici + j], recv_sem=recv.at[i * nici + j],
            device_id=(_flip(x, fx), _flip(y, fy), c), device_id_type=MESH)
            for i in range(self.n) for j, (fx, fy) in enumerate(_CHIP_FLIPS)]
        return local + remote

    def start(self, ins, outs, sems):
        for cp in self._copies(ins, outs, sems):
            cp.start()

    def finish(self, ins, outs, sems):
        for cp in self._copies(ins, outs, sems):
            cp.wait()


def _run_side(side, name):
    def body(*refs):
        ins, outs = refs[:side.n_in], refs[side.n_in:side.n_in + side.n_out]
        sems = refs[side.n_in + side.n_out:]
        side.start(ins, outs, sems)
        side.finish(ins, outs, sems)

    return pl.pallas_call(
        body, name=name, in_specs=[_ANY] * side.n_in, out_specs=[_ANY] * side.n_out, out_shape=side.out_shape,
        input_output_aliases=side.aliases, scratch_shapes=side.scratch)(*side.operands)


class _ReduceScatterD2dSide:
    def __init__(self, parts):
        self.operands = list(parts)
        n = self.n = len(self.operands)
        self.n_in = self.n_out = n
        nd2d = N_DEV // 2
        self.out_shape = [jax.ShapeDtypeStruct((nd2d,) + a.shape[1:], a.dtype) for a in self.operands]
        self.aliases = {}
        self.scratch = [pltpu.SemaphoreType.DMA((n * nd2d,)), pltpu.SemaphoreType.DMA((n * nd2d,))]

    def _copies(self, ins, outs, sems):
        send, recv = sems
        x, y, c = _coords()
        nd2d = N_DEV // 2
        return [pltpu.make_async_remote_copy(
            src_ref=ins[i].at[2 * ch + 1 - c], dst_ref=outs[i].at[ch], send_sem=send.at[i * nd2d + ch],
            recv_sem=recv.at[i * nd2d + ch], device_id=(x, y, 1 - c), device_id_type=MESH)
            for i in range(self.n) for ch in range(nd2d)]

    def start(self, ins, outs, sems):
        for cp in self._copies(ins, outs, sems):
            cp.start()

    def finish(self, ins, outs, sems):
        for cp in self._copies(ins, outs, sems):
            cp.wait()


_PACK_ROWS = 256


def _pack(arrs):
    flat = jnp.concatenate([a.reshape(-1) for a in arrs])
    quantum = _PACK_ROWS * LANES
    total = -(-flat.shape[0] // quantum) * quantum
    return jnp.pad(flat, (0, total - flat.shape[0])).reshape(-1, LANES)


def _unpack(packed, like):
    flat, out, pos = packed.reshape(-1), [], 0
    for a in like:
        out.append(flat[pos:pos + a.size].reshape(a.shape))
        pos += a.size
    return out


def kernel(x, c, ada_w, ada_b, norm1_g, w_in, conv_w, spatial_w, spatial_b, v_norm_g, a_log, dt_bias, o_norm_g, w_branch_a, w_branch_b, w_out, norm2_g, w_ffn_in, w_ffn_out, final_g, loss_target, m_ada_w, m_ada_b, m_norm1_g, m_w_in, m_conv_w, m_spatial_w, m_spatial_b, m_v_norm_g, m_a_log, m_dt_bias, m_o_norm_g, m_w_branch_a, m_w_branch_b, m_w_out, m_norm2_g, m_w_ffn_in, m_w_ffn_out, m_final_g, v_ada_w, v_ada_b, v_norm1_g, v_w_in, v_conv_w, v_spatial_w, v_spatial_b, v_v_norm_g, v_a_log, v_dt_bias, v_o_norm_g, v_w_branch_a, v_w_branch_b, v_w_out, v_norm2_g, v_w_ffn_in, v_w_ffn_out, v_final_g):
    nl, d = ada_w.shape[0], x.shape[2]
    t = x.shape[1]
    nchunk = t // GDN_CHUNK
    xi, yi, ci = _coords()
    me = 4 * xi + 2 * yi + ci
    core = jnp.reshape(ci, (1,)).astype(jnp.int32)
    x0, target = x[0], loss_target[0]
    wcols = 3 * HEADS * HEAD_DIM
    lay = _ProjLayout(d)
    in_pieces = lay.pieces(w_in.shape[2])
    fi_shard = w_ffn_in.shape[2]
    fi_pieces = [(j, 0, fi_shard, fi_shard * j) for j in range(N_DEV)]

    c_all, cw_all = _a2a_direct([jnp.broadcast_to(c[None], (N_DEV,) + c.shape),
                                 jnp.broadcast_to(conv_w[None], (N_DEV,) + conv_w.shape)], "gather_small")
    c_all = c_all[:, 0]
    conv_full = cw_all.transpose(1, 2, 0, 3).reshape(nl, CONV_K, wcols)
    modp = _ada_fwd(c_all, ada_w, "ada_fwd")
    (modx,) = _a2a_direct([modp.transpose(1, 0, 2)], "mod_exchange")
    mod = (modx.transpose(1, 0, 2).reshape(nl, 6 * d) + ada_b).reshape(nl, 6, 1, d)

    big = (w_in, w_branch_a, w_branch_b, w_out, w_ffn_in, w_ffn_out)
    big_wire = [w.astype(WIRE_DTYPE) for w in big]
    gather_in = lambda i: _AllGatherSide([big_wire[0][i]])
    gather_early = lambda i: _AllGatherSide([big_wire[k][i] for k in (1, 2, 3, 5)])
    gather_late = lambda i: _AllGatherSide([big_wire[4][i]] + ([big_wire[0][i + 1]] if i + 1 < nl else []))
    row_full = lambda g: g.reshape(-1, g.shape[2])
    padded_in = lambda g: _cols_from_blocks(g, in_pieces, lay.width, "w_in_cols")
    w_pads = [padded_in(_run_side(gather_in(0), "ag_first")[0])] + [None] * (nl - 1)
    weights = [None] * nl

    def rows_of(ba_rows, lo):
        return ba_rows[lo:lo + HEADS].reshape(HEADS, nchunk, 1, GDN_CHUNK)

    saved = []
    x_cur, delta, gt_prev = x0, None, None
    for i in range(nl):
        sh1, sc1, gt1, sh2, sc2, gt2 = (mod[i, k] for k in range(6))
        s = dict(gt1=gt1, gt2=gt2, sc1=sc1, sc2=sc2)
        s["x_in"], s["h"] = _resid_norm(x_cur, delta, gt_prev, norm1_g[i][None], sc1, sh1, "norm1_fwd")
        s["proj"], g_a, g_b, g_o, g_fo = _matmul(s["h"], w_pads[i], "nn", "proj_fwd", out_dtype=ACT_DTYPE,
                                                 side=gather_early(i))
        ba = _matmul(s["h"], w_pads[i][:, lay.ba:], "nn", "proj_ba_fwd")
        s["b_col"] = spatial_b[i][:, :, None]
        s["ya"] = _mixer_a_fwd(s["proj"], lay.uv, spatial_w[i], s["b_col"], v_norm_g[i][None], "mixer_a_fwd")
        s["qkv_h"] = _conv_fwd(s["proj"], conv_full[i], "conv_fwd")
        ba_rows = _cols_as_rows(ba, 0, "ba_rows")
        s["braw"], s["araw"] = rows_of(ba_rows, 0), rows_of(ba_rows, HEADS)
        s["o"], s["states"], s["t_mats"], g_fi, *g_in = _gdn_fwd(
            s["qkv_h"], s["araw"], s["braw"], a_log[i], dt_bias[i], "gdn_fwd", gather_late(i))
        if g_in:
            w_pads[i + 1] = padded_in(g_in[0])
        weights[i] = (row_full(g_a), row_full(g_b), row_full(g_o),
                      _cols_from_blocks(g_fi, fi_pieces, N_DEV * fi_shard, "w_ffn_in_cols"), row_full(g_fo))
        w_a, w_b, w_o, w_fi, w_fo = weights[i]
        s["yb"] = _gdn_post_fwd(s["o"], s["proj"], lay.z, o_norm_g[i][None], "gdn_post_fwd")
        s["pa"], s["pb"], s["merged"] = _branches_merge(s["ya"], w_a, s["yb"], w_b, s["proj"], lay.gates,
                                                         "branches_merge_fwd")
        s["mo"] = _matmul(s["merged"], w_o, "nn", "out_fwd")
        s["x1"], s["h2"] = _resid_norm(s["x_in"], s["mo"], gt1, norm2_g[i][None], sc2, sh2, "norm2_fwd")
        s["gu"], s["a"] = _ffn_in_swiglu(s["h2"], w_fi, "ffn_in_swiglu_fwd")
        s["fo"] = _matmul(s["a"], w_fo, "nn", "ffn_out_fwd")
        saved.append(s)
        x_cur, delta, gt_prev = s["x1"], s["fo"], gt2
    dx, d_final_g, loss_tile = _final_loss(x_cur, delta, gt_prev, final_g[None], target, "final_loss")
    loss = lax.psum(loss_tile[0, 0], ("x", "y", "c"))

    big_shapes = [(d, w_in.shape[2]), w_branch_a.shape[1:], w_branch_b.shape[1:], w_out.shape[1:],
                  (w_ffn_in.shape[2], d), w_ffn_out.shape[1:]]
    accs = [lax.empty((nl, N_DEV // 2) + tuple(sh), WIRE_DTYPE) for sh in big_shapes]
    row_blocks = lambda g: g.reshape(N_DEV, -1, g.shape[1])
    dmod, small = [None] * nl, [None] * nl
    d_conv = [None] * nl
    parts, sums = None, None
    beside_gdn, beside_dw, beside_dx = (0,), (4,), (1, 2, 3, 5)
    rep_parts = [None] * nl
    rep_pack = lambda i: _pack((dmod[i],) + small[i])

    def scatter_side(idx, layer):
        if sums is None:
            return _NoSide
        return _ReduceScatterIciSide([sums[k] for k in idx], [accs[k] for k in idx], layer)

    def scattered_into(accs, idx, new):
        accs = list(accs)
        for k, a in zip(idx, new):
            accs[k] = a
        return accs

    for i in reversed(range(nl)):
        s = saved[i]
        w_a, w_b, w_o, w_fi, w_fo = weights[i]
        if i == nl - 1:
            dfo, dgt2 = _gate_bwd(dx, s["fo"], s["gt2"], "gate2_bwd")
        else:
            dgt2 = dgt2_before
        g_fo = _matmul(s["a"], dfo, "tn", "ffn_out_dw")
        dgu = _ffn_out_dx_swiglu(dfo, w_fo, s["gu"], "ffn_out_dx_swiglu")
        if parts is None:
            g_fi = _matmul(dgu, s["h2"], "tn", "ffn_in_dw")
        else:
            g_fi, *other = _matmul(dgu, s["h2"], "tn", "ffn_in_dw", side=_ReduceScatterD2dSide(parts))
            sums = [_pair_sum(p, o, core, "rs_pair_sum_%d" % k) for k, (p, o) in enumerate(zip(parts, other))]
        if i + 1 < nl:
            dh2, rep_parts[i + 1] = _matmul(dgu, w_fi, "nt", "ffn_in_dx", side=_AllGatherSide([rep_pack(i + 1)]))
        else:
            dh2 = _matmul(dgu, w_fi, "nt", "ffn_in_dx")
        dx1, dsh2, dsc2, dg2, dmo, dgt1 = _norm_bwd(s["x1"], dh2, dx, norm2_g[i][None], s["sc2"], "norm2_bwd",
                                                    gate=(s["mo"], s["gt1"]))
        g_o = _matmul(s["merged"], dmo, "tn", "out_dw")
        dproj = lax.empty((t, lay.width), MXU_DTYPE)
        dpa, dpb, dproj = _out_dx_merge_bwd(dmo, w_o, s["pa"], s["pb"], s["proj"], lay.gates, dproj,
                                            "out_dx_merge_bwd")
        g_a = _matmul(s["ya"], dpa, "tn", "branch_a_dw")
        dya = _matmul(dpa, w_a, "nt", "branch_a_dx")
        g_b = _matmul(s["yb"], dpb, "tn", "branch_b_dw")
        dyb = _matmul(dpb, w_b, "nt", "branch_b_dx")
        dproj, d_ws, d_bs, d_gv = _mixer_a_bwd(s["proj"], lay.uv, dya, spatial_w[i], jnp.swapaxes(spatial_w[i], 1, 2),
                                               s["b_col"], v_norm_g[i][None], dproj, "mixer_a_bwd")
        do, dproj, d_go = _gdn_post_bwd(s["o"], s["proj"], lay.z, dyb, o_norm_g[i][None], dproj, "gdn_post_bwd")
        dq, dk, dv, d_ar, d_br, d_al, d_dt, *scattered = _gdn_bwd(
            s["qkv_h"], s["araw"], s["braw"], a_log[i], dt_bias[i], s["states"], s["t_mats"], do, "gdn_bwd",
            scatter_side(beside_gdn, i + 1))
        accs = scattered_into(accs, beside_gdn, scattered)
        dacc, d_conv[i] = _conv_bwd_pre(s["proj"], dq, dk, dv, conv_full[i], "conv_bwd_pre")
        dproj = _conv_bwd_in(dacc, conv_full[i], dproj, "conv_bwd_in")
        dba_rows = jnp.pad(jnp.concatenate([d_br.reshape(HEADS, t), d_ar.reshape(HEADS, t)]),
                           ((0, LANES - 2 * HEADS), (0, 0)))
        dproj = _rows_into_cols(dproj, dba_rows, lay.ba, "dproj_ba")
        if sums is None:
            g_pad = _matmul(s["h"], dproj, "tn", "proj_dw")
            dh = _matmul(dproj, w_pads[i], "nt", "proj_dx")
        else:
            g_pad, *scattered = _matmul(s["h"], dproj, "tn", "proj_dw", side=scatter_side(beside_dw, i + 1))
            accs = scattered_into(accs, beside_dw, scattered)
            dh, *scattered = _matmul(dproj, w_pads[i], "nt", "proj_dx", side=scatter_side(beside_dx, i + 1))
            accs = scattered_into(accs, beside_dx, scattered)
        if i > 0:
            dx, dsh1, dsc1, dg1, dfo, dgt2_before = _norm_bwd(s["x_in"], dh, dx1, norm1_g[i][None], s["sc1"], "norm1_bwd",
                                                              gate=(saved[i - 1]["fo"], saved[i - 1]["gt2"]))
        else:
            dx, dsh1, dsc1, dg1 = _norm_bwd(s["x_in"], dh, dx1, norm1_g[i][None], s["sc1"], "norm1_bwd")
        dmod[i] = jnp.concatenate([dsh1, dsc1, dgt1, dsh2, dsc2, dgt2], axis=1)[0]
        small[i] = (dg1[0], d_ws, d_bs[:, :, 0], d_gv[0], d_al[:, 0, 0], d_dt[:, 0, 0], d_go[0], dg2[0])
        parts = [_blocks_from_cols(g_pad, in_pieces, w_in.shape[2], "w_in_blocks"), row_blocks(g_a), row_blocks(g_b),
                 row_blocks(g_o), row_blocks(g_fi), row_blocks(g_fo)]
    other = _run_side(_ReduceScatterD2dSide(parts), "rs_d2d_last")
    sums = [_pair_sum(p, o, core, "rs_pair_sum_%d" % k) for k, (p, o) in enumerate(zip(parts, other))]
    accs = _run_side(_ReduceScatterIciSide(sums, accs, 0), "rs_ici_last")

    rep_w = (ada_b, norm1_g, spatial_w, spatial_b, v_norm_g, a_log, dt_bias, o_norm_g, norm2_g)
    rep_m = (m_ada_b, m_norm1_g, m_spatial_w, m_spatial_b, m_v_norm_g, m_a_log, m_dt_bias, m_o_norm_g, m_norm2_g)
    rep_v = (v_ada_b, v_norm1_g, v_spatial_w, v_spatial_b, v_v_norm_g, v_a_log, v_dt_bias, v_o_norm_g, v_norm2_g)
    rep_parts[0], fin_parts = _run_side(_AllGatherSide([rep_pack(0), _pack([d_final_g[0]])]), "small_grads_last")
    dmod = jnp.stack(dmod)
    d_conv_blocks = jnp.stack(d_conv).reshape(nl, CONV_K, N_DEV, -1).transpose(2, 0, 1, 3).reshape(N_DEV, -1, LANES)
    dmod_blocks = dmod.reshape(nl, N_DEV, -1).transpose(1, 0, 2)
    conv_all, dmod_all = _a2a_direct([d_conv_blocks, dmod_blocks], "small_grads_scatter")
    by_layer = lambda arrs: jnp.stack([_pack([a[i] for a in arrs]) for i in range(nl)])
    rep_out = _sum_adam(jnp.stack(rep_parts), by_layer(rep_w), by_layer(rep_m), by_layer(rep_v), "adam_small")
    fin_out = _sum_adam(fin_parts[None], _pack([final_g])[None], _pack([m_final_g])[None], _pack([v_final_g])[None],
                        "adam_final_g")
    layer_like = [a[0] for a in rep_w]
    rep_out = [[jnp.stack(per_layer) for per_layer in zip(*[_unpack(o[i], layer_like) for i in range(nl)])]
               + _unpack(f[0], [final_g]) for o, f in zip(rep_out, fin_out)]
    conv_out = _sum_adam(conv_all[None], conv_w.reshape(1, -1, LANES), m_conv_w.reshape(1, -1, LANES),
                         v_conv_w.reshape(1, -1, LANES), "adam_conv")
    conv_out = [o.reshape(conv_w.shape) for o in conv_out]
    ada_out = _ada_bwd(c_all[:, :, None], dmod_all.transpose(1, 0, 2), ada_w, m_ada_w, v_ada_w, "ada_bwd_adam")
    big_m = (m_w_in, m_w_branch_a, m_w_branch_b, m_w_out, m_w_ffn_in, m_w_ffn_out)
    big_v = (v_w_in, v_w_branch_a, v_w_branch_b, v_w_out, v_w_ffn_in, v_w_ffn_out)
    turn = lambda k, a: jnp.swapaxes(a, 1, 2) if k == 4 else a
    big_out = [[turn(k, o) for o in _sum_adam(accs[k], turn(k, big[k]), turn(k, big_m[k]), turn(k, big_v[k]),
                                             "adam_big_%d" % k)] for k in range(6)]

    def ordered(kind):
        rep = rep_out[kind]
        return (ada_out[kind], rep[0], rep[1], big_out[0][kind], conv_out[kind], rep[2], rep[3], rep[4], rep[5],
                rep[6], rep[7], big_out[1][kind], big_out[2][kind], big_out[3][kind], rep[8], big_out[4][kind],
                big_out[5][kind], rep[9])

    return (loss, dx[None]) + ordered(0) + ordered(1) + ordered(2) + ordered(3)
```
